```python
import jax, jax.numpy as jnp
from jax import lax
import numpy as np

D_MODEL = 1024
BATCH = 1
SEQ = 16384
DEPTH = 1

GRID_W = 64
CTX_LEN = 256
N_HEADS = 8
HEAD_DIM = 64
ATT_W = N_HEADS * HEAD_DIM
WIN_H = 8
WIN_W = 16
POOL_WINDOWS = (2, 4, 8, 16)
POOL_GROUPS = 4
POOL_DIM = 128
POOL_W = POOL_GROUPS * POOL_DIM
PROJ_W = 3 * ATT_W + POOL_W + 2 * D_MODEL
SPLIT_POINTS = (ATT_W, 2 * ATT_W, 3 * ATT_W, 3 * ATT_W + POOL_W, 3 * ATT_W + POOL_W + D_MODEL)
N_GROUPS = 4
EXPERTS_PER_GROUP = 8
N_EXPERTS = N_GROUPS * EXPERTS_PER_GROUP
TOP_K_EXPERT = 2
D_EXPERT = 512
N_MOD = 6
DEEPNORM_ALPHA = (2.0 * DEPTH) ** 0.25
DEEPNORM_BETA = (8.0 * DEPTH) ** -0.25
LN_EPS = 1e-5
NEG_INF = -1e30

kernel_name = "hybrid_natten_pool_hmoe_dit"


def layer_norm(x, g, b):
    xf = x.astype(jnp.float32)
    mu = jnp.mean(xf, axis=-1, keepdims=True)
    var = jnp.mean(jnp.square(xf - mu), axis=-1, keepdims=True)
    y = (xf - mu) * lax.rsqrt(var + LN_EPS)
    return (y * g.astype(jnp.float32) + b.astype(jnp.float32)).astype(x.dtype)


def adaln_params(cond, w_mod, b_mod):
    mod = (jax.nn.silu(cond) @ w_mod + b_mod)[:, None, :]
    return jnp.split(mod, N_MOD, axis=-1)


def modulate(h, shift, scale):
    return h * (1.0 + scale) + shift


def split_heads(t):
    B, L, _ = t.shape
    return t.reshape(B, L, N_HEADS, HEAD_DIM)


def neighbourhood_attention(q, k, v, k_ctx, v_ctx, rpb):
    B, L, H, Dh = q.shape
    rows = L // GRID_W
    kh = min(WIN_H, rows)
    n_loc = kh * GRID_W
    scale = Dh ** -0.5
    qg = q.reshape(B, rows, GRID_W, H, Dh)
    kg = k.reshape(B, rows, GRID_W, H, Dh)
    vg = v.reshape(B, rows, GRID_W, H, Dh)
    r = jnp.arange(rows, dtype=jnp.int32)
    row_start = jnp.clip(r - kh // 2, 0, rows - kh)
    col = jnp.arange(GRID_W, dtype=jnp.int32)
    col_start = jnp.clip(col - WIN_W // 2, 0, GRID_W - WIN_W)
    col_mask = (col[None, :] >= col_start[:, None]) & (col[None, :] < col_start[:, None] + WIN_W)
    col_off = jnp.clip(col[None, :] - col[:, None], 1 - WIN_W, WIN_W - 1) + (WIN_W - 1)
    rpb32 = rpb.astype(jnp.float32)

    def row_block(args):
        q_row, rs, rq = args
        kb = lax.dynamic_slice_in_dim(kg, rs, kh, axis=1)
        vb = lax.dynamic_slice_in_dim(vg, rs, kh, axis=1)
        row_off = rs + jnp.arange(kh, dtype=jnp.int32) - rq + (WIN_H - 1)
        bias = rpb32[:, row_off[None, :, None], col_off[:, None, :]]
        s_loc = jnp.einsum('bqhd,bikhd->bhqik', q_row, kb,
                           preferred_element_type=jnp.float32) * scale + bias[None]
        s_loc = jnp.where(col_mask[:, None, :], s_loc, NEG_INF).reshape(B, H, GRID_W, n_loc)
        s_ctx = jnp.einsum('bqhd,bchd->bhqc', q_row, k_ctx,
                           preferred_element_type=jnp.float32) * scale
        p = jax.nn.softmax(jnp.concatenate([s_loc, s_ctx], axis=-1), axis=-1).astype(v.dtype)
        o = (jnp.einsum('bhqk,bkhd->bqhd', p[..., :n_loc], vb.reshape(B, n_loc, H, Dh))
             + jnp.einsum('bhqc,bchd->bqhd', p[..., n_loc:], v_ctx))
        return o

    out = lax.map(row_block, (jnp.moveaxis(qg, 1, 0), row_start, r))
    return jnp.moveaxis(out, 0, 1).reshape(B, L, H * Dh)


def context_attention(q, k, v):
    B, C, H, Dh = q.shape
    s = jnp.einsum('bqhd,bkhd->bhqk', q, k, preferred_element_type=jnp.float32) * (Dh ** -0.5)
    p = jax.nn.softmax(s, axis=-1).astype(v.dtype)
    return jnp.einsum('bhqk,bkhd->bqhd', p, v).reshape(B, C, H * Dh)


def multiscale_pool(u, w_grp, layer_scale):
    B, L, _ = u.shape
    ug = u.reshape(B, L, POOL_GROUPS, POOL_DIM)
    ugf = ug.astype(jnp.float32)
    csum = jnp.concatenate([jnp.zeros((B, 1, POOL_GROUPS, POOL_DIM), jnp.float32),
                            jnp.cumsum(ugf, axis=1)], axis=1)
    win = jnp.asarray(POOL_WINDOWS, dtype=jnp.int32)
    t = jnp.arange(L, dtype=jnp.int32)[:, None]
    lo = jnp.clip(t - win // 2, 0, L)
    hi = jnp.clip(t + win - win // 2, 0, L)
    grp = jnp.arange(POOL_GROUPS, dtype=jnp.int32)[None, :]
    mean = (csum[:, hi, grp] - csum[:, lo, grp]) / (hi - lo).astype(jnp.float32)[None, :, :, None]
    pooled = (mean - ugf).astype(u.dtype)
    y = jnp.einsum('blgi,gio->blgo', pooled, w_grp).reshape(B, L, POOL_W)
    return y * layer_scale


def merge_branches(y_attn, y_pool, gate_a, gate_b, w_attn_proj, w_pool_proj, w_out):
    ya = y_attn @ w_attn_proj
    yp = y_pool @ w_pool_proj
    return (jax.nn.sigmoid(gate_a) * ya + jax.nn.sigmoid(gate_b) * yp) @ w_out


def hierarchical_moe(h, w_rg, b_rg, w_re, b_re, w_gate, w_up, w_down):
    B, L, D = h.shape
    T = B * L
    ht = h.reshape(T, D)
    p_group = jax.nn.softmax((ht @ w_rg).astype(jnp.float32) + b_rg.astype(jnp.float32), axis=-1)
    p_top_group, group_idx = lax.top_k(p_group, 1)
    logit_e = ((ht @ w_re).astype(jnp.float32) + b_re.astype(jnp.float32)).reshape(T, N_GROUPS, EXPERTS_PER_GROUP)
    logit_sel = logit_e[jnp.arange(T), group_idx[:, 0]]
    p_e = jax.nn.softmax(logit_sel, axis=-1)
    p_top_e, e_idx = lax.top_k(p_e, TOP_K_EXPERT)
    p_top_e = p_top_e / jnp.sum(p_top_e, axis=-1, keepdims=True)
    combine = p_top_group * p_top_e
    expert_id = group_idx * EXPERTS_PER_GROUP + e_idx
    gates = jnp.sum(jax.nn.one_hot(expert_id, N_EXPERTS, dtype=jnp.float32) * combine[..., None],
                    axis=1).astype(h.dtype)
    y = jnp.zeros((T, D), jnp.float32)
    for g in range(N_GROUPS):
        sl = slice(g * EXPERTS_PER_GROUP, (g + 1) * EXPERTS_PER_GROUP)
        a = jnp.einsum('td,edf->tef', ht, w_gate[sl])
        u = jnp.einsum('td,edf->tef', ht, w_up[sl])
        act = jax.nn.silu(a) * u * gates[:, sl, None]
        y = y + jnp.einsum('tef,efd->td', act, w_down[sl], preferred_element_type=jnp.float32)
    return y.astype(h.dtype).reshape(B, L, D)


def setup_inputs(seed: int = 0) -> dict:
    key = jax.random.key(seed)
    ks = jax.random.split(key, 32)
    f32 = jnp.float32

    def nrm(k, shape, fan_in, gain=1.0):
        return jax.random.normal(k, shape, f32) * (gain * fan_in ** -0.5)

    def small(k, shape, s):
        return jax.random.normal(k, shape, f32) * s

    D = D_MODEL
    return {
        "x": jax.random.normal(ks[0], (BATCH, SEQ, D), f32),
        "c": jax.random.normal(ks[1], (BATCH, D), f32),
        "ctx": jax.random.normal(ks[2], (BATCH, CTX_LEN, D), f32),
        "c_ctx": jax.random.normal(ks[3], (D,), f32),
        "ln_in_g": 1.0 + small(ks[4], (D,), 0.02),
        "ln_in_b": small(ks[5], (D,), 0.02),
        "w_mod": nrm(ks[6], (DEPTH, D, N_MOD * D), D, 0.2),
        "b_mod": small(ks[7], (DEPTH, N_MOD * D), 0.02),
        "w_in": nrm(ks[8], (DEPTH, D, PROJ_W), D),
        "rpb": small(ks[9], (DEPTH, N_HEADS, 2 * WIN_H - 1, 2 * WIN_W - 1), 0.5),
        "w_pool_grp": nrm(ks[10], (DEPTH, POOL_GROUPS, POOL_DIM, POOL_DIM), POOL_DIM),
        "pool_scale": 1.0 + small(ks[11], (DEPTH, POOL_W), 0.02),
        "w_attn_proj": nrm(ks[12], (DEPTH, ATT_W, D), ATT_W),
        "w_pool_proj": nrm(ks[13], (DEPTH, POOL_W, D), POOL_W),
        "w_out": nrm(ks[14], (DEPTH, D, D), D, DEEPNORM_BETA),
        "ln1_g": 1.0 + small(ks[15], (DEPTH, D), 0.02),
        "ln1_b": small(ks[16], (DEPTH, D), 0.02),
        "w_router_group": nrm(ks[17], (DEPTH, D, N_GROUPS), D),
        "b_router_group": small(ks[18], (DEPTH, N_GROUPS), 0.01),
        "w_router_expert": nrm(ks[19], (DEPTH, D, N_EXPERTS), D),
        "b_router_expert": small(ks[20], (DEPTH, N_EXPERTS), 0.01),
        "w_expert_gate": nrm(ks[21], (DEPTH, N_EXPERTS, D, D_EXPERT), D),
        "w_expert_up": nrm(ks[22], (DEPTH, N_EXPERTS, D, D_EXPERT), D),
        "w_expert_down": nrm(ks[23], (DEPTH, N_EXPERTS, D_EXPERT, D), D_EXPERT, DEEPNORM_BETA),
        "ln2_g": 1.0 + small(ks[24], (DEPTH, D), 0.02),
        "ln2_b": small(ks[25], (DEPTH, D), 0.02),
    }


def reference(x, c, ctx, c_ctx, ln_in_g, ln_in_b, w_mod, b_mod, w_in, rpb, w_pool_grp, pool_scale,
              w_attn_proj, w_pool_proj, w_out, ln1_g, ln1_b, w_router_group, b_router_group,
              w_router_expert, b_router_expert, w_expert_gate, w_expert_up, w_expert_down, ln2_g, ln2_b):
    alpha = DEEPNORM_ALPHA
    h = layer_norm(x, ln_in_g, ln_in_b)
    hc = layer_norm(ctx, ln_in_g, ln_in_b)

    for layer in range(DEPTH):
        last = layer == DEPTH - 1
        sh1, sc1, g1, sh2, sc2, g2 = adaln_params(c, w_mod[layer], b_mod[layer])
        csh1, csc1, cg1, csh2, csc2, cg2 = adaln_params(c_ctx[None, :], w_mod[layer], b_mod[layer])
        wl = w_in[layer]

        hc_mod = modulate(hc, csh1, csc1)
        if last:
            k_c, v_c = jnp.split(hc_mod @ wl[:, ATT_W:3 * ATT_W], 2, axis=-1)
        else:
            q_c, k_c, v_c, pool_c, ga_c, gb_c = jnp.split(hc_mod @ wl, SPLIT_POINTS, axis=-1)
        k_c = split_heads(k_c)
        v_c = split_heads(v_c)

        q, k, v, pool_in, ga, gb = jnp.split(modulate(h, sh1, sc1) @ wl, SPLIT_POINTS, axis=-1)
        y_attn = neighbourhood_attention(split_heads(q), split_heads(k), split_heads(v), k_c, v_c, rpb[layer])
        y_pool = multiscale_pool(pool_in, w_pool_grp[layer], pool_scale[layer])
        y = merge_branches(y_attn, y_pool, ga, gb, w_attn_proj[layer], w_pool_proj[layer], w_out[layer])
        h_new = layer_norm(alpha * h + g1 * y, ln1_g[layer], ln1_b[layer])

        ffn = hierarchical_moe(modulate(h_new, sh2, sc2), w_router_group[layer], b_router_group[layer],
                               w_router_expert[layer], b_router_expert[layer],
                               w_expert_gate[layer], w_expert_up[layer], w_expert_down[layer])
        h_new = layer_norm(alpha * h_new + g2 * ffn, ln2_g[layer], ln2_b[layer])

        if not last:
            yc_attn = context_attention(split_heads(q_c), k_c, v_c)
            yc_pool = multiscale_pool(pool_c, w_pool_grp[layer], pool_scale[layer])
            yc = merge_branches(yc_attn, yc_pool, ga_c, gb_c, w_attn_proj[layer], w_pool_proj[layer], w_out[layer])
            hc = layer_norm(alpha * hc + cg1 * yc, ln1_g[layer], ln1_b[layer])
            ffn_c = hierarchical_moe(modulate(hc, csh2, csc2), w_router_group[layer], b_router_group[layer],
                                     w_router_expert[layer], b_router_expert[layer],
                                     w_expert_gate[layer], w_expert_up[layer], w_expert_down[layer])
            hc = layer_norm(alpha * hc + cg2 * ffn_c, ln2_g[layer], ln2_b[layer])
        h = h_new

    return h
```

```python
import functools

import jax
import jax.numpy as jnp
from jax import lax
from jax.experimental import pallas as pl
from jax.experimental.pallas import tpu as pltpu

D_MODEL = 1024
SEQ = 16384
GRID_W = 64
ROWS = SEQ // GRID_W
CTX_LEN = 256
N_HEADS = 8
HEAD_DIM = 64
ATT_W = N_HEADS * HEAD_DIM
WIN_H = 8
WIN_W = 16
POOL_WINDOWS = (2, 4, 8, 16)
POOL_GROUPS = 4
POOL_DIM = 128
POOL_W = POOL_GROUPS * POOL_DIM
PROJ_W = 3 * ATT_W + POOL_W + 2 * D_MODEL
N_GROUPS = 4
EXPERTS_PER_GROUP = 8
N_EXPERTS = N_GROUPS * EXPERTS_PER_GROUP
D_EXPERT = 512
N_MOD = 6
DEEPNORM_ALPHA = 2.0 ** 0.25
LN_EPS = 1e-5
NEG_INF = -1e30

LANES = 128
MOD_ROWS = 8
PROJ_TM = 512
MIX_ROWS = 8
MIX_TQ = MIX_ROWS * GRID_W
KV_HALO = 4 * GRID_W
POOL_HALO = 16
ROUTE_TM = 512
EXP_TM = 256
EXP_TILES = 2 * SEQ // EXP_TM + N_EXPERTS
CMB_TM = 256
HALF = D_MODEL // 2
VMEM_LIMIT = 56 * 1024 * 1024


def _layer_norm(x, g, b):
    mu = jnp.mean(x, axis=-1, keepdims=True)
    xc = x - mu
    var = jnp.mean(xc * xc, axis=-1, keepdims=True)
    return xc * lax.rsqrt(var + LN_EPS) * g + b


def _bdot(a, b):
    return jnp.dot(a, b, preferred_element_type=jnp.float32)


def _split_bf16(a):
    hi = a.astype(jnp.bfloat16)
    lo = (a - hi.astype(jnp.float32)).astype(jnp.bfloat16)
    return hi, lo


def _dot3(a, b):
    a_hi, a_lo = _split_bf16(a)
    b_hi, b_lo = _split_bf16(b)
    return _bdot(a_hi, b_hi) + (_bdot(a_hi, b_lo) + _bdot(a_lo, b_hi))


def _mod_kernel(cond_ref, w_ref, b_ref, o_ref):
    cond = cond_ref[...]
    act = cond * jax.nn.sigmoid(cond)
    o_ref[...] = _dot3(act, w_ref[...]) + b_ref[...]


def _mod_call(cond, w_mod, b_mod):
    tn = 1536
    n = N_MOD * D_MODEL
    return pl.pallas_call(
        _mod_kernel,
        grid=(n // tn,),
        in_specs=[
            pl.BlockSpec((MOD_ROWS, D_MODEL), lambda i: (0, 0)),
            pl.BlockSpec((D_MODEL, tn), lambda i: (0, i)),
            pl.BlockSpec((1, tn), lambda i: (0, i)),
        ],
        out_specs=pl.BlockSpec((MOD_ROWS, tn), lambda i: (0, i)),
        out_shape=jax.ShapeDtypeStruct((MOD_ROWS, n), jnp.float32),
        compiler_params=pltpu.CompilerParams(
            dimension_semantics=("arbitrary",), vmem_limit_bytes=VMEM_LIMIT),
        name="mod",
    )(cond, w_mod, b_mod)


def _proj_kernel(x_ref, mod_ref, g_ref, b_ref, w_ref, o_ref, *, mod_row, q_cols):
    h = _layer_norm(x_ref[...], g_ref[...], b_ref[...])
    shift = mod_ref[mod_row:mod_row + 1, 0:D_MODEL]
    scale = mod_ref[mod_row:mod_row + 1, D_MODEL:2 * D_MODEL]
    hm = (h * (1.0 + scale) + shift).astype(jnp.bfloat16)
    n = o_ref.shape[1]
    for c in range(n // D_MODEL):
        sl = slice(c * D_MODEL, (c + 1) * D_MODEL)
        res = _bdot(hm, w_ref[:, sl])
        if c == 0 and q_cols:
            lane = lax.broadcasted_iota(jnp.int32, (1, D_MODEL), 1)
            res = res * jnp.where(lane < q_cols, HEAD_DIM ** -0.5, 1.0)
        o_ref[:, sl] = res.astype(jnp.bfloat16)


def _proj_call(x, mod, g, b, w, *, mod_row, q_cols, tm):
    rows, n = x.shape[0], w.shape[1]
    return pl.pallas_call(
        functools.partial(_proj_kernel, mod_row=mod_row, q_cols=q_cols),
        grid=(rows // tm,),
        in_specs=[
            pl.BlockSpec((tm, D_MODEL), lambda i: (i, 0)),
            pl.BlockSpec(mod.shape, lambda i: (0, 0)),
            pl.BlockSpec((1, D_MODEL), lambda i: (0, 0)),
            pl.BlockSpec((1, D_MODEL), lambda i: (0, 0)),
            pl.BlockSpec((D_MODEL, n), lambda i: (0, 0), pipeline_mode=pl.Buffered(1)),
        ],
        out_specs=pl.BlockSpec((tm, n), lambda i: (i, 0)),
        out_shape=jax.ShapeDtypeStruct((rows, n), jnp.bfloat16),
        compiler_params=pltpu.CompilerParams(
            dimension_semantics=("arbitrary",), vmem_limit_bytes=VMEM_LIMIT),
        name="proj",
    )(x, mod, g, b, w)


def _attn_bias_table(rpb):
    col = jnp.arange(GRID_W, dtype=jnp.int32)
    col_start = jnp.clip(col - WIN_W // 2, 0, GRID_W - WIN_W)
    col_mask = (col[None, :] >= col_start[:, None]) & (col[None, :] < col_start[:, None] + WIN_W)
    col_off = jnp.clip(col[None, :] - col[:, None], 1 - WIN_W, WIN_W - 1) + (WIN_W - 1)
    var = jnp.arange(WIN_H, dtype=jnp.int32)
    ki = jnp.arange(WIN_H, dtype=jnp.int32)
    row_off = jnp.clip(ki[None, :] - var[:, None] + (WIN_H - 1), 0, 2 * WIN_H - 2)
    tab = rpb.astype(jnp.float32)[:, row_off[:, :, None, None], col_off[None, None, :, :]]
    tab = jnp.where(col_mask[None, None, None], tab, NEG_INF)
    tab = tab.transpose(1, 0, 3, 2, 4)
    return tab.reshape(WIN_H, N_HEADS // 2, 2 * GRID_W, WIN_H * GRID_W)


def _mix_kernel(x_ref, mod_ref, lng_ref, lnb_ref,
                q_ref, kp_ref, kc_ref, kn_ref, vp_ref, vc_ref, vn_ref,
                pp_ref, pc_ref, pn_ref, ga_ref, gb_ref,
                kvc_ref, bias_ref, wgrp_ref, pscale_ref, wap_ref, wpp_ref, wout_ref,
                ln1g_ref, ln1b_ref,
                o_ref,
                kbuf, vbuf, yabuf, pbuf, ypbuf):
    b = pl.program_id(0)
    nb = pl.num_programs(0)

    kbuf[0:KV_HALO, :] = kp_ref[...]
    kbuf[KV_HALO:KV_HALO + MIX_TQ, :] = kc_ref[...]
    kbuf[KV_HALO + MIX_TQ:, :] = kn_ref[...]
    vbuf[0:KV_HALO, :] = vp_ref[...]
    vbuf[KV_HALO:KV_HALO + MIX_TQ, :] = vc_ref[...]
    vbuf[KV_HALO + MIX_TQ:, :] = vn_ref[...]

    lane = lax.broadcasted_iota(jnp.int32, (GRID_W, LANES), 1)
    first_head = lane < HEAD_DIM

    def row_body(j, carry):
        r = b * MIX_ROWS + j
        rs = jnp.clip(r - WIN_H // 2, 0, ROWS - WIN_H)
        off = pl.multiple_of((rs - b * MIX_ROWS + WIN_H // 2) * GRID_W, GRID_W)
        var = r - rs
        qoff = pl.multiple_of(j * GRID_W, GRID_W)
        for pair in range(N_HEADS // 2):
            cols = slice(pair * LANES, (pair + 1) * LANES)
            q = q_ref[pl.ds(qoff, GRID_W), cols]
            zero = jnp.zeros_like(q)
            q2 = jnp.concatenate([jnp.where(first_head, q, zero), jnp.where(first_head, zero, q)], axis=0)
            kw = kbuf[pl.ds(off, WIN_H * GRID_W), cols]
            vw = vbuf[pl.ds(off, WIN_H * GRID_W), cols]
            kctx = kvc_ref[:, cols]
            vctx = kvc_ref[:, ATT_W + pair * LANES:ATT_W + (pair + 1) * LANES]
            nt = (((1,), (1,)), ((), ()))
            s_loc = lax.dot_general(q2, kw, nt, preferred_element_type=jnp.float32) + bias_ref[var, pair]
            s_ctx = lax.dot_general(q2, kctx, nt, preferred_element_type=jnp.float32)
            m = jnp.maximum(jnp.max(s_loc, axis=-1, keepdims=True), jnp.max(s_ctx, axis=-1, keepdims=True))
            p_loc = jnp.exp(s_loc - m)
            p_ctx = jnp.exp(s_ctx - m)
            denom = jnp.sum(p_loc, axis=-1, keepdims=True) + jnp.sum(p_ctx, axis=-1, keepdims=True)
            o2 = _bdot(p_loc.astype(jnp.bfloat16), vw) + _bdot(p_ctx.astype(jnp.bfloat16), vctx)
            o2 = o2 * (1.0 / denom)
            o_pair = jnp.where(first_head, o2[:GRID_W], o2[GRID_W:])
            yabuf[pl.ds(qoff, GRID_W), cols] = o_pair.astype(jnp.bfloat16)
        return carry

    lax.fori_loop(0, MIX_ROWS, row_body, 0)

    pbuf[0:POOL_HALO, :] = jnp.where(b > 0, pp_ref[...].astype(jnp.float32), 0.0)
    pbuf[POOL_HALO:POOL_HALO + MIX_TQ, :] = pc_ref[...].astype(jnp.float32)
    pbuf[POOL_HALO + MIX_TQ:, :] = jnp.where(b < nb - 1, pn_ref[...].astype(jnp.float32), 0.0)
    t_abs = b * MIX_TQ + lax.broadcasted_iota(jnp.int32, (MIX_TQ, 1), 0)
    for g, win in enumerate(POOL_WINDOWS):
        cols = slice(g * POOL_DIM, (g + 1) * POOL_DIM)
        acc = None
        for d in range(-(win // 2), win - win // 2):
            term = pbuf[POOL_HALO + d:POOL_HALO + d + MIX_TQ, cols]
            acc = term if acc is None else acc + term
        count = jnp.minimum(t_abs + (win - win // 2), SEQ) - jnp.maximum(t_abs - win // 2, 0)
        pooled = acc / count.astype(jnp.float32) - pbuf[POOL_HALO:POOL_HALO + MIX_TQ, cols]
        yp = _bdot(pooled.astype(jnp.bfloat16), wgrp_ref[g]) * pscale_ref[:, cols]
        ypbuf[:, cols] = yp.astype(jnp.bfloat16)

    ya = _bdot(yabuf[...], wap_ref[...])
    yp = _bdot(ypbuf[...], wpp_ref[...])
    z = jax.nn.sigmoid(ga_ref[...].astype(jnp.float32)) * ya + jax.nn.sigmoid(gb_ref[...].astype(jnp.float32)) * yp
    y = _bdot(z.astype(jnp.bfloat16), wout_ref[...])
    h = _layer_norm(x_ref[...], lng_ref[...], lnb_ref[...])
    g1 = mod_ref[0:1, 2 * D_MODEL:3 * D_MODEL]
    o_ref[...] = _layer_norm(DEEPNORM_ALPHA * h + g1 * y, ln1g_ref[...], ln1b_ref[...])


def _mix_call(x, mod, lng, lnb, u, kvc, bias, wgrp, pscale, wap, wpp, wout, ln1g, ln1b):
    nb = SEQ // MIX_TQ
    halo_per_blk = MIX_TQ // KV_HALO
    n_halo = SEQ // KV_HALO
    ph_per_blk = MIX_TQ // POOL_HALO
    n_ph = SEQ // POOL_HALO

    def const(shape):
        return pl.BlockSpec(shape, lambda i: (0,) * len(shape), pipeline_mode=pl.Buffered(1))

    def prev_halo(c):
        return pl.BlockSpec((KV_HALO, ATT_W), lambda i: (jnp.maximum(i * halo_per_blk - 1, 0), c))

    def next_halo(c):
        return pl.BlockSpec((KV_HALO, ATT_W), lambda i: (jnp.minimum((i + 1) * halo_per_blk, n_halo - 1), c))

    def cur(c):
        return pl.BlockSpec((MIX_TQ, ATT_W), lambda i: (i, c))

    in_specs = [
        pl.BlockSpec((MIX_TQ, D_MODEL), lambda i: (i, 0)),
        const(mod.shape), const((1, D_MODEL)), const((1, D_MODEL)),
        cur(0),
        prev_halo(1), cur(1), next_halo(1),
        prev_halo(2), cur(2), next_halo(2),
        pl.BlockSpec((POOL_HALO, POOL_W), lambda i: (jnp.maximum(i * ph_per_blk - 1, 0), 3)),
        cur(3),
        pl.BlockSpec((POOL_HALO, POOL_W), lambda i: (jnp.minimum((i + 1) * ph_per_blk, n_ph - 1), 3)),
        pl.BlockSpec((MIX_TQ, D_MODEL), lambda i: (i, 2)),
        pl.BlockSpec((MIX_TQ, D_MODEL), lambda i: (i, 3)),
        const(kvc.shape), const(bias.shape), const(wgrp.shape), const(pscale.shape),
        const(wap.shape), const(wpp.shape), const(wout.shape),
        const((1, D_MODEL)), const((1, D_MODEL)),
    ]
    return pl.pallas_call(
        _mix_kernel,
        grid=(nb,),
        in_specs=in_specs,
        out_specs=pl.BlockSpec((MIX_TQ, D_MODEL), lambda i: (i, 0)),
        out_shape=jax.ShapeDtypeStruct((SEQ, D_MODEL), jnp.float32),
        scratch_shapes=[
            pltpu.VMEM((MIX_TQ + 2 * KV_HALO, ATT_W), jnp.bfloat16),
            pltpu.VMEM((MIX_TQ + 2 * KV_HALO, ATT_W), jnp.bfloat16),
            pltpu.VMEM((MIX_TQ, ATT_W), jnp.bfloat16),
            pltpu.VMEM((MIX_TQ + 2 * POOL_HALO, POOL_W), jnp.float32),
            pltpu.VMEM((MIX_TQ, POOL_W), jnp.bfloat16),
        ],
        compiler_params=pltpu.CompilerParams(
            dimension_semantics=("arbitrary",), vmem_limit_bytes=VMEM_LIMIT),
        name="mix",
    )(x, mod, lng, lnb, u, u, u, u, u, u, u, u, u, u, u, u,
      kvc, bias, wgrp, pscale, wap, wpp, wout, ln1g, ln1b)


META_E0, META_E1, META_C0, META_C1, META_R0, META_R1 = range(6)


def _route_kernel(h_ref, mod_ref, wr_ref, br_ref, xp_ref, meta_ref, cnt_ref, carry_ref):
    i = pl.program_id(0)

    @pl.when(i == 0)
    def _():
        carry_ref[...] = jnp.zeros_like(carry_ref)

    shift = mod_ref[0:1, 3 * D_MODEL:4 * D_MODEL]
    scale = mod_ref[0:1, 4 * D_MODEL:5 * D_MODEL]
    hm = h_ref[...] * (1.0 + scale) + shift

    lo = pltpu.bitcast(hm[:, :HALF].astype(jnp.bfloat16).astype(jnp.float32), jnp.uint32)
    hi = pltpu.bitcast(hm[:, HALF:].astype(jnp.bfloat16).astype(jnp.float32), jnp.uint32)
    xp_ref[...] = (hi & jnp.uint32(0xFFFF0000)) | (lo >> 16)

    logits = _dot3(hm, wr_ref[...]) + br_ref[...]
    tm = logits.shape[0]
    lane = lax.broadcasted_iota(jnp.int32, (tm, LANES), 1)
    big = jnp.int32(1 << 20)
    is_grp = lane < N_GROUPS
    gl = jnp.where(is_grp, logits, -jnp.inf)
    gmax = jnp.max(gl, axis=-1, keepdims=True)
    gidx = jnp.min(jnp.where(gl == gmax, lane, big), axis=-1, keepdims=True)
    gsum = jnp.sum(jnp.where(is_grp, jnp.exp(logits - gmax), 0.0), axis=-1, keepdims=True)
    p_group = 1.0 / gsum

    eid = lane - N_GROUPS
    sel = (eid >= 0) & (eid < N_EXPERTS) & (lax.shift_right_arithmetic(eid, 3) == gidx)
    el = jnp.where(sel, logits, -jnp.inf)
    l0 = jnp.max(el, axis=-1, keepdims=True)
    i0 = jnp.min(jnp.where(el == l0, lane, big), axis=-1, keepdims=True)
    el2 = jnp.where(lane == i0, -jnp.inf, el)
    l1 = jnp.max(el2, axis=-1, keepdims=True)
    i1 = jnp.min(jnp.where(el2 == l1, lane, big), axis=-1, keepdims=True)
    t = jnp.exp(l1 - l0)
    w0 = 1.0 / (1.0 + t)
    w1 = t / (1.0 + t)

    onehot = jnp.where((lane == i0) | (lane == i1), 1.0, 0.0)
    rr = lax.broadcasted_iota(jnp.int32, (tm, tm), 0)
    cc = lax.broadcasted_iota(jnp.int32, (tm, tm), 1)
    tri = jnp.where(cc < rr, 1.0, 0.0).astype(jnp.bfloat16)
    prefix = _bdot(tri, onehot.astype(jnp.bfloat16)) + carry_ref[0:1, :]
    r0 = jnp.sum(jnp.where(lane == i0, prefix, 0.0), axis=-1, keepdims=True)
    r1 = jnp.sum(jnp.where(lane == i1, prefix, 0.0), axis=-1, keepdims=True)
    total = carry_ref[0:1, :] + jnp.sum(onehot, axis=0, keepdims=True)
    carry_ref[...] = jnp.broadcast_to(total, carry_ref.shape)
    cnt_ref[...] = jnp.broadcast_to(total, cnt_ref.shape)

    meta = jnp.zeros((tm, LANES), jnp.float32)
    for idx, val in ((META_E0, (i0 - N_GROUPS).astype(jnp.float32)),
                     (META_E1, (i1 - N_GROUPS).astype(jnp.float32)),
                     (META_C0, p_group * w0), (META_C1, p_group * w1),
                     (META_R0, r0), (META_R1, r1)):
        meta = jnp.where(lane == idx, val, meta)
    meta_ref[...] = meta


def _route_call(h, mod, wr, br):
    tm = ROUTE_TM
    return pl.pallas_call(
        _route_kernel,
        grid=(SEQ // tm,),
        in_specs=[
            pl.BlockSpec((tm, D_MODEL), lambda i: (i, 0)),
            pl.BlockSpec(mod.shape, lambda i: (0, 0)),
            pl.BlockSpec((D_MODEL, LANES), lambda i: (0, 0)),
            pl.BlockSpec((1, LANES), lambda i: (0, 0)),
        ],
        out_specs=[
            pl.BlockSpec((tm, HALF), lambda i: (i, 0)),
            pl.BlockSpec((tm, LANES), lambda i: (i, 0)),
            pl.BlockSpec((8, LANES), lambda i: (0, 0)),
        ],
        out_shape=[
            jax.ShapeDtypeStruct((SEQ, HALF), jnp.uint32),
            jax.ShapeDtypeStruct((SEQ, LANES), jnp.float32),
            jax.ShapeDtypeStruct((8, LANES), jnp.float32),
        ],
        scratch_shapes=[pltpu.VMEM((8, LANES), jnp.float32)],
        compiler_params=pltpu.CompilerParams(
            dimension_semantics=("arbitrary",), vmem_limit_bytes=VMEM_LIMIT),
        name="route",
    )(h, mod, wr, br)


GATHER_UNROLL = 8


def _experts_kernel(te_ref, nact_ref, src_ref,
                    xp_hbm, wg_ref, wu_ref, wd_ref,
                    o_ref,
                    xbuf, wgb, wub, wdb, sem):
    i = pl.program_id(0)
    nact = nact_ref[0]
    slot = lax.rem(i, 2)

    def row_copy(tile, k, s):
        tok = src_ref[tile * EXP_TM + k]
        return pltpu.make_async_copy(xp_hbm.at[pl.ds(tok, 1), :], xbuf.at[s, pl.ds(k, 1), :], sem.at[s])

    def issue(tile, s):
        def body(kk, c):
            for u in range(GATHER_UNROLL):
                row_copy(tile, kk * GATHER_UNROLL + u, s).start()
            return c
        lax.fori_loop(0, EXP_TM // GATHER_UNROLL, body, 0)

    @pl.when((i == 0) & (nact > 0))
    def _():
        issue(0, 0)

    @pl.when(i + 1 < nact)
    def _():
        issue(i + 1, 1 - slot)

    @pl.when(i < nact)
    def _():
        pltpu.make_async_copy(xp_hbm.at[pl.ds(0, EXP_TM), :], xbuf.at[slot], sem.at[slot]).wait()

        new_expert = (i == 0) | (te_ref[i] != te_ref[jnp.maximum(i - 1, 0)])

        @pl.when(new_expert)
        def _():
            wgb[...] = wg_ref[0].astype(jnp.bfloat16)
            wub[...] = wu_ref[0].astype(jnp.bfloat16)
            wdb[...] = wd_ref[0].astype(jnp.bfloat16)

        w = xbuf[slot]
        x_lo = pltpu.bitcast(w << 16, jnp.float32).astype(jnp.bfloat16)
        x_hi = pltpu.bitcast(w & jnp.uint32(0xFFFF0000), jnp.float32).astype(jnp.bfloat16)
        a = _bdot(x_lo, wgb[0:HALF, :]) + _bdot(x_hi, wgb[HALF:, :])
        up = _bdot(x_lo, wub[0:HALF, :]) + _bdot(x_hi, wub[HALF:, :])
        act = (a * jax.nn.sigmoid(a) * up).astype(jnp.bfloat16)
        o_ref[...] = _bdot(act, wdb[...])

    @pl.when(i >= nact)
    def _():
        o_ref[...] = jnp.zeros_like(o_ref)


def _experts_call(tile_expert, nact, src, xp, wg, wu, wd):
    grid_spec = pltpu.PrefetchScalarGridSpec(
        num_scalar_prefetch=3,
        grid=(EXP_TILES,),
        in_specs=[
            pl.BlockSpec(memory_space=pl.ANY),
            pl.BlockSpec((1, D_MODEL, D_EXPERT), lambda i, te, na, sr: (te[i], 0, 0)),
            pl.BlockSpec((1, D_MODEL, D_EXPERT), lambda i, te, na, sr: (te[i], 0, 0)),
            pl.BlockSpec((1, D_EXPERT, D_MODEL), lambda i, te, na, sr: (te[i], 0, 0)),
        ],
        out_specs=pl.BlockSpec((EXP_TM, D_MODEL), lambda i, te, na, sr: (i, 0)),
        scratch_shapes=[
            pltpu.VMEM((2, EXP_TM, HALF), jnp.uint32),
            pltpu.VMEM((D_MODEL, D_EXPERT), jnp.bfloat16),
            pltpu.VMEM((D_MODEL, D_EXPERT), jnp.bfloat16),
            pltpu.VMEM((D_EXPERT, D_MODEL), jnp.bfloat16),
            pltpu.SemaphoreType.DMA((2,)),
        ],
    )
    return pl.pallas_call(
        _experts_kernel,
        grid_spec=grid_spec,
        out_shape=jax.ShapeDtypeStruct((EXP_TILES * EXP_TM, D_MODEL), jnp.float32),
        compiler_params=pltpu.CompilerParams(
            dimension_semantics=("arbitrary",), vmem_limit_bytes=VMEM_LIMIT),
        name="experts",
    )(tile_expert, nact, src, xp, wg, wu, wd)


def _combine_kernel(p0_ref, p1_ref,
                    h_ref, meta_ref, mod_ref, g_ref, b_ref, ys_hbm,
                    o_ref,
                    ybuf, sem):
    i = pl.program_id(0)
    n = pl.num_programs(0)
    slot = lax.rem(i, 2)

    def issue(tile, s):
        def body(kk, c):
            for u in range(GATHER_UNROLL):
                k = kk * GATHER_UNROLL + u
                t = tile * CMB_TM + k
                pltpu.make_async_copy(ys_hbm.at[pl.ds(p0_ref[t], 1), :], ybuf.at[s, 0, pl.ds(k, 1), :], sem.at[s]).start()
                pltpu.make_async_copy(ys_hbm.at[pl.ds(p1_ref[t], 1), :], ybuf.at[s, 1, pl.ds(k, 1), :], sem.at[s]).start()
            return c
        lax.fori_loop(0, CMB_TM // GATHER_UNROLL, body, 0)

    @pl.when(i == 0)
    def _():
        issue(0, 0)

    @pl.when(i + 1 < n)
    def _():
        issue(i + 1, 1 - slot)

    for half in range(2):
        pltpu.make_async_copy(ys_hbm.at[pl.ds(0, CMB_TM), :], ybuf.at[slot, half], sem.at[slot]).wait()

    lane = lax.broadcasted_iota(jnp.int32, (CMB_TM, LANES), 1)
    meta = meta_ref[...]
    c0 = jnp.sum(jnp.where(lane == META_C0, meta, 0.0), axis=-1, keepdims=True)
    c1 = jnp.sum(jnp.where(lane == META_C1, meta, 0.0), axis=-1, keepdims=True)
    ffn = c0 * ybuf[slot, 0] + c1 * ybuf[slot, 1]
    g2 = mod_ref[0:1, 5 * D_MODEL:6 * D_MODEL]
    o_ref[...] = _layer_norm(DEEPNORM_ALPHA * h_ref[...] + g2 * ffn, g_ref[...], b_ref[...])


def _combine_call(pos0, pos1, h, meta, mod, g, b, ys):
    grid_spec = pltpu.PrefetchScalarGridSpec(
        num_scalar_prefetch=2,
        grid=(SEQ // CMB_TM,),
        in_specs=[
            pl.BlockSpec((CMB_TM, D_MODEL), lambda i, p0, p1: (i, 0)),
            pl.BlockSpec((CMB_TM, LANES), lambda i, p0, p1: (i, 0)),
            pl.BlockSpec(mod.shape, lambda i, p0, p1: (0, 0)),
            pl.BlockSpec((1, D_MODEL), lambda i, p0, p1: (0, 0)),
            pl.BlockSpec((1, D_MODEL), lambda i, p0, p1: (0, 0)),
            pl.BlockSpec(memory_space=pl.ANY),
        ],
        out_specs=pl.BlockSpec((CMB_TM, D_MODEL), lambda i, p0, p1: (i, 0)),
        scratch_shapes=[
            pltpu.VMEM((2, 2, CMB_TM, D_MODEL), jnp.float32),
            pltpu.SemaphoreType.DMA((2,)),
        ],
    )
    return pl.pallas_call(
        _combine_kernel,
        grid_spec=grid_spec,
        out_shape=jax.ShapeDtypeStruct((SEQ, D_MODEL), jnp.float32),
        compiler_params=pltpu.CompilerParams(
            dimension_semantics=("arbitrary",), vmem_limit_bytes=VMEM_LIMIT),
        name="combine",
    )(pos0, pos1, h, meta, mod, g, b, ys)


def kernel(x, c, ctx, c_ctx, ln_in_g, ln_in_b, w_mod, b_mod, w_in, rpb, w_pool_grp, pool_scale,
           w_attn_proj, w_pool_proj, w_out, ln1_g, ln1_b, w_router_group, b_router_group,
           w_router_expert, b_router_expert, w_expert_gate, w_expert_up, w_expert_down, ln2_g, ln2_b):
    assert x.shape == (1, SEQ, D_MODEL) and ctx.shape == (1, CTX_LEN, D_MODEL)
    assert w_mod.shape[0] == 1, "single-layer trunk"
    f32, bf16 = jnp.float32, jnp.bfloat16
    row = lambda v: v.reshape(1, -1).astype(f32)

    cond = jnp.zeros((MOD_ROWS, D_MODEL), f32).at[0].set(c[0]).at[1].set(c_ctx)
    mod = _mod_call(cond, w_mod[0], row(b_mod[0]))

    lng, lnb = row(ln_in_g), row(ln_in_b)
    w_in_b = w_in[0].astype(bf16)
    xs = x[0]
    u = _proj_call(xs, mod, lng, lnb, w_in_b, mod_row=0, q_cols=ATT_W, tm=PROJ_TM)
    kvc = _proj_call(ctx[0], mod, lng, lnb, w_in_b[:, ATT_W:3 * ATT_W], mod_row=1, q_cols=0, tm=CTX_LEN)

    h1 = _mix_call(xs, mod, lng, lnb, u, kvc, _attn_bias_table(rpb[0]),
                   w_pool_grp[0].astype(bf16), row(pool_scale[0]),
                   w_attn_proj[0].astype(bf16), w_pool_proj[0].astype(bf16), w_out[0].astype(bf16),
                   row(ln1_g[0]), row(ln1_b[0]))

    wr = jnp.zeros((D_MODEL, LANES), f32)
    wr = wr.at[:, :N_GROUPS].set(w_router_group[0]).at[:, N_GROUPS:N_GROUPS + N_EXPERTS].set(w_router_expert[0])
    br = jnp.zeros((1, LANES), f32)
    br = br.at[0, :N_GROUPS].set(b_router_group[0]).at[0, N_GROUPS:N_GROUPS + N_EXPERTS].set(b_router_expert[0])
    xp, meta, cnt = _route_call(h1, mod, wr, br)

    counts = cnt[0, N_GROUPS:N_GROUPS + N_EXPERTS].astype(jnp.int32)
    tiles_per = (counts + EXP_TM - 1) // EXP_TM
    tile_end = jnp.cumsum(tiles_per)
    seg_start = (tile_end - tiles_per) * EXP_TM
    nact = tile_end[-1:]
    tile_ids = jnp.arange(EXP_TILES, dtype=jnp.int32)
    tile_expert = jnp.minimum(jnp.searchsorted(tile_end, tile_ids, side="right"), N_EXPERTS - 1).astype(jnp.int32)
    tile_expert = jnp.where(tile_ids < nact[0], tile_expert, tile_expert[jnp.maximum(nact[0] - 1, 0)])
    e0 = meta[:, META_E0].astype(jnp.int32)
    e1 = meta[:, META_E1].astype(jnp.int32)
    pos0 = seg_start[e0] + meta[:, META_R0].astype(jnp.int32)
    pos1 = seg_start[e1] + meta[:, META_R1].astype(jnp.int32)
    tok = jnp.arange(SEQ, dtype=jnp.int32)
    src = jnp.zeros((EXP_TILES * EXP_TM,), jnp.int32).at[pos0].set(tok).at[pos1].set(tok)

    ys = _experts_call(tile_expert, nact.astype(jnp.int32), src, xp,
                       w_expert_gate[0], w_expert_up[0], w_expert_down[0])
    out = _combine_call(pos0, pos1, h1, meta, mod, row(ln2_g[0]), row(ln2_b[0]), ys)
    return out[None]
```

```python
import functools

import jax
import jax.numpy as jnp
from jax import lax
from jax.experimental import pallas as pl
from jax.experimental.pallas import tpu as pltpu

D_MODEL = 1024
SEQ = 16384
GRID_W = 64
ROWS = SEQ // GRID_W
CTX_LEN = 256
N_HEADS = 8
HEAD_DIM = 64
ATT_W = N_HEADS * HEAD_DIM
WIN_H = 8
WIN_W = 16
POOL_WINDOWS = (2, 4, 8, 16)
POOL_GROUPS = 4
POOL_DIM = 128
POOL_W = POOL_GROUPS * POOL_DIM
PROJ_W = 3 * ATT_W + POOL_W + 2 * D_MODEL
N_GROUPS = 4
EXPERTS_PER_GROUP = 8
N_EXPERTS = N_GROUPS * EXPERTS_PER_GROUP
D_EXPERT = 512
N_MOD = 6
DEEPNORM_ALPHA = 2.0 ** 0.25
LN_EPS = 1e-5
NEG_INF = -1e30

LANES = 128
MOD_ROWS = 8
PROJ_TM = 512
MIX_ROWS = 8
MIX_TQ = MIX_ROWS * GRID_W
KV_HALO = 4 * GRID_W
POOL_HALO = 16
ROUTE_TM = 512
EXP_TM = 256
EXP_TILES = 2 * SEQ // EXP_TM + N_EXPERTS
CMB_TM = 256
HALF = D_MODEL // 2
VMEM_LIMIT = 56 * 1024 * 1024


def _layer_norm(x, g, b):
    mu = jnp.mean(x, axis=-1, keepdims=True)
    xc = x - mu
    var = jnp.mean(xc * xc, axis=-1, keepdims=True)
    return xc * lax.rsqrt(var + LN_EPS) * g + b


def _bdot(a, b):
    return jnp.dot(a, b, preferred_element_type=jnp.float32)


def _split_bf16(a):
    hi = a.astype(jnp.bfloat16)
    lo = (a - hi.astype(jnp.float32)).astype(jnp.bfloat16)
    return hi, lo


def _dot3(a, b):
    a_hi, a_lo = _split_bf16(a)
    b_hi, b_lo = _split_bf16(b)
    return _bdot(a_hi, b_hi) + (_bdot(a_hi, b_lo) + _bdot(a_lo, b_hi))


def _mod_kernel(cond_ref, w_ref, b_ref, o_ref):
    cond = cond_ref[...]
    act = cond * jax.nn.sigmoid(cond)
    o_ref[...] = _dot3(act, w_ref[...]) + b_ref[...]


def _mod_call(cond, w_mod, b_mod):
    tn = 1536
    n = N_MOD * D_MODEL
    return pl.pallas_call(
        _mod_kernel,
        grid=(n // tn,),
        in_specs=[
            pl.BlockSpec((MOD_ROWS, D_MODEL), lambda i: (0, 0)),
            pl.BlockSpec((D_MODEL, tn), lambda i: (0, i)),
            pl.BlockSpec((1, tn), lambda i: (0, i)),
        ],
        out_specs=pl.BlockSpec((MOD_ROWS, tn), lambda i: (0, i)),
        out_shape=jax.ShapeDtypeStruct((MOD_ROWS, n), jnp.float32),
        compiler_params=pltpu.CompilerParams(
            dimension_semantics=("arbitrary",), vmem_limit_bytes=VMEM_LIMIT),
        name="mod",
    )(cond, w_mod, b_mod)


def _proj_kernel(x_ref, mod_ref, g_ref, b_ref, w_ref, o_ref, *, mod_row, q_cols):
    h = _layer_norm(x_ref[...], g_ref[...], b_ref[...])
    shift = mod_ref[mod_row:mod_row + 1, 0:D_MODEL]
    scale = mod_ref[mod_row:mod_row + 1, D_MODEL:2 * D_MODEL]
    hm = (h * (1.0 + scale) + shift).astype(jnp.bfloat16)
    n = o_ref.shape[1]
    for c in range(n // D_MODEL):
        sl = slice(c * D_MODEL, (c + 1) * D_MODEL)
        res = _bdot(hm, w_ref[:, sl])
        if c == 0 and q_cols:
            lane = lax.broadcasted_iota(jnp.int32, (1, D_MODEL), 1)
            res = res * jnp.where(lane < q_cols, HEAD_DIM ** -0.5, 1.0)
        o_ref[:, sl] = res.astype(jnp.bfloat16)


def _proj_call(x, mod, g, b, w, *, mod_row, q_cols, tm):
    rows, n = x.shape[0], w.shape[1]
    return pl.pallas_call(
        functools.partial(_proj_kernel, mod_row=mod_row, q_cols=q_cols),
        grid=(rows // tm,),
        in_specs=[
            pl.BlockSpec((tm, D_MODEL), lambda i: (i, 0)),
            pl.BlockSpec(mod.shape, lambda i: (0, 0)),
            pl.BlockSpec((1, D_MODEL), lambda i: (0, 0)),
            pl.BlockSpec((1, D_MODEL), lambda i: (0, 0)),
            pl.BlockSpec((D_MODEL, n), lambda i: (0, 0), pipeline_mode=pl.Buffered(1)),
        ],
        out_specs=pl.BlockSpec((tm, n), lambda i: (i, 0)),
        out_shape=jax.ShapeDtypeStruct((rows, n), jnp.bfloat16),
        compiler_params=pltpu.CompilerParams(
            dimension_semantics=("arbitrary",), vmem_limit_bytes=VMEM_LIMIT),
        name="proj",
    )(x, mod, g, b, w)


def _attn_bias_table(rpb):
    col = jnp.arange(GRID_W, dtype=jnp.int32)
    col_start = jnp.clip(col - WIN_W // 2, 0, GRID_W - WIN_W)
    col_mask = (col[None, :] >= col_start[:, None]) & (col[None, :] < col_start[:, None] + WIN_W)
    col_off = jnp.clip(col[None, :] - col[:, None], 1 - WIN_W, WIN_W - 1) + (WIN_W - 1)
    onehot = (col_off[None] == jnp.arange(2 * WIN_W - 1, dtype=jnp.int32)[:, None, None]).astype(jnp.float32)
    tab = jnp.einsum("hrc,cqk->hrqk", rpb.astype(jnp.float32), onehot, precision=lax.Precision.HIGHEST)
    tab = jnp.where(col_mask[None, None], tab, NEG_INF)
    tab = jnp.stack([tab[:, WIN_H - 1 - v:2 * WIN_H - 1 - v] for v in range(WIN_H)], axis=0)
    tab = tab.transpose(0, 1, 3, 2, 4)
    return tab.reshape(WIN_H, N_HEADS // 2, 2 * GRID_W, WIN_H * GRID_W)


def _mix_kernel(x_ref, mod_ref, lng_ref, lnb_ref,
                q_ref, kp_ref, kc_ref, kn_ref, vp_ref, vc_ref, vn_ref,
                pp_ref, pc_ref, pn_ref, ga_ref, gb_ref,
                kvc_ref, bias_ref, wgrp_ref, pscale_ref, wap_ref, wpp_ref, wout_ref,
                ln1g_ref, ln1b_ref,
                o_ref,
                kbuf, vbuf, yabuf, pbuf, ypbuf):
    b = pl.program_id(0)
    nb = pl.num_programs(0)

    kbuf[0:KV_HALO, :] = kp_ref[...]
    kbuf[KV_HALO:KV_HALO + MIX_TQ, :] = kc_ref[...]
    kbuf[KV_HALO + MIX_TQ:, :] = kn_ref[...]
    vbuf[0:KV_HALO, :] = vp_ref[...]
    vbuf[KV_HALO:KV_HALO + MIX_TQ, :] = vc_ref[...]
    vbuf[KV_HALO + MIX_TQ:, :] = vn_ref[...]

    lane = lax.broadcasted_iota(jnp.int32, (GRID_W, LANES), 1)
    first_head = lane < HEAD_DIM

    def row_body(j, carry):
        r = b * MIX_ROWS + j
        rs = jnp.clip(r - WIN_H // 2, 0, ROWS - WIN_H)
        off = pl.multiple_of((rs - b * MIX_ROWS + WIN_H // 2) * GRID_W, GRID_W)
        var = r - rs
        qoff = pl.multiple_of(j * GRID_W, GRID_W)
        for pair in range(N_HEADS // 2):
            cols = slice(pair * LANES, (pair + 1) * LANES)
            q = q_ref[pl.ds(qoff, GRID_W), cols]
            zero = jnp.zeros_like(q)
            q2 = jnp.concatenate([jnp.where(first_head, q, zero), jnp.where(first_head, zero, q)], axis=0)
            kw = kbuf[pl.ds(off, WIN_H * GRID_W), cols]
            vw = vbuf[pl.ds(off, WIN_H * GRID_W), cols]
            kctx = kvc_ref[:, cols]
            vctx = kvc_ref[:, ATT_W + pair * LANES:ATT_W + (pair + 1) * LANES]
            nt = (((1,), (1,)), ((), ()))
            s_loc = lax.dot_general(q2, kw, nt, preferred_element_type=jnp.float32) + bias_ref[var, pair]
            s_ctx = lax.dot_general(q2, kctx, nt, preferred_element_type=jnp.float32)
            m = jnp.maximum(jnp.max(s_loc, axis=-1, keepdims=True), jnp.max(s_ctx, axis=-1, keepdims=True))
            p_loc = jnp.exp(s_loc - m)
            p_ctx = jnp.exp(s_ctx - m)
            denom = jnp.sum(p_loc, axis=-1, keepdims=True) + jnp.sum(p_ctx, axis=-1, keepdims=True)
            o2 = _bdot(p_loc.astype(jnp.bfloat16), vw) + _bdot(p_ctx.astype(jnp.bfloat16), vctx)
            o2 = o2 * (1.0 / denom)
            o_pair = jnp.where(first_head, o2[:GRID_W], o2[GRID_W:])
            yabuf[pl.ds(qoff, GRID_W), cols] = o_pair.astype(jnp.bfloat16)
        return carry

    lax.fori_loop(0, MIX_ROWS, row_body, 0)

    pbuf[0:POOL_HALO, :] = jnp.where(b > 0, pp_ref[...].astype(jnp.float32), 0.0)
    pbuf[POOL_HALO:POOL_HALO + MIX_TQ, :] = pc_ref[...].astype(jnp.float32)
    pbuf[POOL_HALO + MIX_TQ:, :] = jnp.where(b < nb - 1, pn_ref[...].astype(jnp.float32), 0.0)
    t_abs = b * MIX_TQ + lax.broadcasted_iota(jnp.int32, (MIX_TQ, 1), 0)
    for g, win in enumerate(POOL_WINDOWS):
        cols = slice(g * POOL_DIM, (g + 1) * POOL_DIM)
        acc = None
        for d in range(-(win // 2), win - win // 2):
            term = pbuf[POOL_HALO + d:POOL_HALO + d + MIX_TQ, cols]
            acc = term if acc is None else acc + term
        count = jnp.minimum(t_abs + (win - win // 2), SEQ) - jnp.maximum(t_abs - win // 2, 0)
        pooled = acc / count.astype(jnp.float32) - pbuf[POOL_HALO:POOL_HALO + MIX_TQ, cols]
        yp = _bdot(pooled.astype(jnp.bfloat16), wgrp_ref[g]) * pscale_ref[:, cols]
        ypbuf[:, cols] = yp.astype(jnp.bfloat16)

    ya = _bdot(yabuf[...], wap_ref[...])
    yp = _bdot(ypbuf[...], wpp_ref[...])
    z = jax.nn.sigmoid(ga_ref[...].astype(jnp.float32)) * ya + jax.nn.sigmoid(gb_ref[...].astype(jnp.float32)) * yp
    y = _bdot(z.astype(jnp.bfloat16), wout_ref[...])
    h = _layer_norm(x_ref[...], lng_ref[...], lnb_ref[...])
    g1 = mod_ref[0:1, 2 * D_MODEL:3 * D_MODEL]
    o_ref[...] = _layer_norm(DEEPNORM_ALPHA * h + g1 * y, ln1g_ref[...], ln1b_ref[...])


def _mix_call(x, mod, lng, lnb, u, kvc, bias, wgrp, pscale, wap, wpp, wout, ln1g, ln1b):
    nb = SEQ // MIX_TQ
    halo_per_blk = MIX_TQ // KV_HALO
    n_halo = SEQ // KV_HALO
    ph_per_blk = MIX_TQ // POOL_HALO
    n_ph = SEQ // POOL_HALO

    def const(shape):
        return pl.BlockSpec(shape, lambda i: (0,) * len(shape), pipeline_mode=pl.Buffered(1))

    def prev_halo(c):
        return pl.BlockSpec((KV_HALO, ATT_W), lambda i: (jnp.maximum(i * halo_per_blk - 1, 0), c))

    def next_halo(c):
        return pl.BlockSpec((KV_HALO, ATT_W), lambda i: (jnp.minimum((i + 1) * halo_per_blk, n_halo - 1), c))

    def cur(c):
        return pl.BlockSpec((MIX_TQ, ATT_W), lambda i: (i, c))

    in_specs = [
        pl.BlockSpec((MIX_TQ, D_MODEL), lambda i: (i, 0)),
        const(mod.shape), const((1, D_MODEL)), const((1, D_MODEL)),
        cur(0),
        prev_halo(1), cur(1), next_halo(1),
        prev_halo(2), cur(2), next_halo(2),
        pl.BlockSpec((POOL_HALO, POOL_W), lambda i: (jnp.maximum(i * ph_per_blk - 1, 0), 3)),
        cur(3),
        pl.BlockSpec((POOL_HALO, POOL_W), lambda i: (jnp.minimum((i + 1) * ph_per_blk, n_ph - 1), 3)),
        pl.BlockSpec((MIX_TQ, D_MODEL), lambda i: (i, 2)),
        pl.BlockSpec((MIX_TQ, D_MODEL), lambda i: (i, 3)),
        const(kvc.shape), const(bias.shape), const(wgrp.shape), const(pscale.shape),
        const(wap.shape), const(wpp.shape), const(wout.shape),
        const((1, D_MODEL)), const((1, D_MODEL)),
    ]
    return pl.pallas_call(
        _mix_kernel,
        grid=(nb,),
        in_specs=in_specs,
        out_specs=pl.BlockSpec((MIX_TQ, D_MODEL), lambda i: (i, 0)),
        out_shape=jax.ShapeDtypeStruct((SEQ, D_MODEL), jnp.float32),
        scratch_shapes=[
            pltpu.VMEM((MIX_TQ + 2 * KV_HALO, ATT_W), jnp.bfloat16),
            pltpu.VMEM((MIX_TQ + 2 * KV_HALO, ATT_W), jnp.bfloat16),
            pltpu.VMEM((MIX_TQ, ATT_W), jnp.bfloat16),
            pltpu.VMEM((MIX_TQ + 2 * POOL_HALO, POOL_W), jnp.float32),
            pltpu.VMEM((MIX_TQ, POOL_W), jnp.bfloat16),
        ],
        compiler_params=pltpu.CompilerParams(
            dimension_semantics=("arbitrary",), vmem_limit_bytes=VMEM_LIMIT),
        name="mix",
    )(x, mod, lng, lnb, u, u, u, u, u, u, u, u, u, u, u, u,
      kvc, bias, wgrp, pscale, wap, wpp, wout, ln1g, ln1b)


ID_E0, ID_E1, ID_R0, ID_R1 = range(4)
CW_C0, CW_C1 = 0, 1


def _route_kernel(h_ref, mod_ref, wrt_ref, brt_ref, xp_ref, ids_ref, cw_ref, cnt_ref, carry_ref):
    i = pl.program_id(0)

    @pl.when(i == 0)
    def _():
        carry_ref[...] = jnp.zeros_like(carry_ref)

    shift = mod_ref[0:1, 3 * D_MODEL:4 * D_MODEL]
    scale = mod_ref[0:1, 4 * D_MODEL:5 * D_MODEL]
    hm = h_ref[...] * (1.0 + scale) + shift
    tm = hm.shape[0]

    hm_hi, hm_lo = _split_bf16(hm)
    hb = pltpu.bitcast(hm_hi.astype(jnp.float32), jnp.uint32)
    xp_ref[...] = (hb[:, HALF:] & jnp.uint32(0xFFFF0000)) | (hb[:, :HALF] >> 16)

    w_hi, w_lo = _split_bf16(wrt_ref[...])
    nt = (((1,), (1,)), ((), ()))
    dg = functools.partial(lax.dot_general, dimension_numbers=nt, preferred_element_type=jnp.float32)
    logits = dg(w_hi, hm_hi) + (dg(w_hi, hm_lo) + dg(w_lo, hm_hi)) + brt_ref[:, 0:1]

    sub = lax.broadcasted_iota(jnp.int32, (LANES, tm), 0)
    big = jnp.int32(1 << 20)
    is_grp = sub < N_GROUPS
    gl = jnp.where(is_grp, logits, -jnp.inf)
    gmax = jnp.max(gl, axis=0, keepdims=True)
    gidx = jnp.min(jnp.where(gl == gmax, sub, big), axis=0, keepdims=True)
    gsum = jnp.sum(jnp.where(is_grp, jnp.exp(logits - gmax), 0.0), axis=0, keepdims=True)
    p_group = 1.0 / gsum

    eid = sub - N_GROUPS
    sel = (eid >= 0) & (eid < N_EXPERTS) & (lax.shift_right_arithmetic(eid, 3) == gidx)
    el = jnp.where(sel, logits, -jnp.inf)
    l0 = jnp.max(el, axis=0, keepdims=True)
    i0 = jnp.min(jnp.where(el == l0, sub, big), axis=0, keepdims=True)
    el2 = jnp.where(sub == i0, -jnp.inf, el)
    l1 = jnp.max(el2, axis=0, keepdims=True)
    i1 = jnp.min(jnp.where(el2 == l1, sub, big), axis=0, keepdims=True)
    t = jnp.exp(l1 - l0)
    w0 = 1.0 / (1.0 + t)
    w1 = t / (1.0 + t)

    onehot = jnp.where((sub == i0) | (sub == i1), 1.0, 0.0)
    rr = lax.broadcasted_iota(jnp.int32, (tm, tm), 0)
    cc = lax.broadcasted_iota(jnp.int32, (tm, tm), 1)
    earlier = jnp.where(rr < cc, 1.0, 0.0).astype(jnp.bfloat16)
    carry = carry_ref[:, 0:1]
    prefix = _bdot(onehot.astype(jnp.bfloat16), earlier) + carry
    r0 = jnp.sum(jnp.where(sub == i0, prefix, 0.0), axis=0, keepdims=True)
    r1 = jnp.sum(jnp.where(sub == i1, prefix, 0.0), axis=0, keepdims=True)
    total = jnp.broadcast_to(carry + jnp.sum(onehot, axis=1, keepdims=True), carry_ref.shape)
    carry_ref[...] = total
    cnt_ref[...] = total.astype(jnp.int32)

    sub8 = lax.broadcasted_iota(jnp.int32, (8, tm), 0)
    ids = jnp.zeros((8, tm), jnp.int32)
    for idx, val in ((ID_E0, i0 - N_GROUPS), (ID_E1, i1 - N_GROUPS),
                     (ID_R0, r0.astype(jnp.int32)), (ID_R1, r1.astype(jnp.int32))):
        ids = jnp.where(sub8 == idx, val, ids)
    ids_ref[...] = ids

    cwt = jnp.where(sub == CW_C0, p_group * w0, jnp.where(sub == CW_C1, p_group * w1, 0.0))
    cw_ref[...] = cwt.T


def _route_call(h, mod, wrt, brt):
    tm = ROUTE_TM
    return pl.pallas_call(
        _route_kernel,
        grid=(SEQ // tm,),
        in_specs=[
            pl.BlockSpec((tm, D_MODEL), lambda i: (i, 0)),
            pl.BlockSpec(mod.shape, lambda i: (0, 0)),
            pl.BlockSpec((LANES, D_MODEL), lambda i: (0, 0)),
            pl.BlockSpec((LANES, LANES), lambda i: (0, 0)),
        ],
        out_specs=[
            pl.BlockSpec((tm, HALF), lambda i: (i, 0)),
            pl.BlockSpec((8, tm), lambda i: (0, i)),
            pl.BlockSpec((tm, LANES), lambda i: (i, 0)),
            pl.BlockSpec((LANES, LANES), lambda i: (0, 0)),
        ],
        out_shape=[
            jax.ShapeDtypeStruct((SEQ, HALF), jnp.uint32),
            jax.ShapeDtypeStruct((8, SEQ), jnp.int32),
            jax.ShapeDtypeStruct((SEQ, LANES), jnp.float32),
            jax.ShapeDtypeStruct((LANES, LANES), jnp.int32),
        ],
        scratch_shapes=[pltpu.VMEM((LANES, LANES), jnp.float32)],
        compiler_params=pltpu.CompilerParams(
            dimension_semantics=("arbitrary",), vmem_limit_bytes=VMEM_LIMIT),
        name="route",
    )(h, mod, wrt, brt)


SRC_UNROLL = 8


def _experts_kernel(te_ref, nact_ref, seg_ref, e0_ref, e1_ref, r0_ref, r1_ref,
                    xp_hbm, wg_ref, wu_ref, wd_ref,
                    o_ref,
                    src_ref, xbuf, wgb, wub, wdb, sem):
    i = pl.program_id(0)
    last = pl.num_programs(0) - 1
    nact = nact_ref[0]
    slot = lax.rem(i, 2)

    def issue(tile, s):
        base = tile * EXP_TM
        for k in range(EXP_TM):
            tok = src_ref[base + k]
            pltpu.make_async_copy(xp_hbm.at[pl.ds(tok, 1), :], xbuf.at[s, pl.ds(k, 1), :],
                                  sem.at[s]).start(priority=k % 2)

    def wait(s):
        pltpu.make_async_copy(xp_hbm.at[pl.ds(0, EXP_TM), :], xbuf.at[s], sem.at[s]).wait()

    @pl.when(i == 0)
    def _():
        def zero_body(p, c):
            for u in range(SRC_UNROLL):
                src_ref[p * SRC_UNROLL + u] = 0
            return c
        lax.fori_loop(0, EXP_TILES * EXP_TM // SRC_UNROLL, zero_body, 0)

        def fill_body(tt, c):
            for u in range(SRC_UNROLL):
                t = tt * SRC_UNROLL + u
                src_ref[seg_ref[e0_ref[t]] + r0_ref[t]] = t
                src_ref[seg_ref[e1_ref[t]] + r1_ref[t]] = t
            return c
        lax.fori_loop(0, SEQ // SRC_UNROLL, fill_body, 0)
        issue(0, 0)

    @pl.when(i <= nact)
    def _():
        wait(slot)

    @pl.when(i < nact)
    def _():
        new_expert = (i == 0) | (te_ref[i] != te_ref[jnp.maximum(i - 1, 0)])

        @pl.when(new_expert)
        def _():
            wgb[...] = wg_ref[0].astype(jnp.bfloat16)
            wub[...] = wu_ref[0].astype(jnp.bfloat16)
            wdb[...] = wd_ref[0].astype(jnp.bfloat16)

        issue(jnp.minimum(i + 1, last), 1 - slot)
        w = xbuf[slot]
        x_lo = pltpu.bitcast(w << 16, jnp.float32).astype(jnp.bfloat16)
        x_hi = pltpu.bitcast(w & jnp.uint32(0xFFFF0000), jnp.float32).astype(jnp.bfloat16)
        a = _bdot(x_lo, wgb[0:HALF, :]) + _bdot(x_hi, wgb[HALF:, :])
        up = _bdot(x_lo, wub[0:HALF, :]) + _bdot(x_hi, wub[HALF:, :])
        act = (a * jax.nn.sigmoid(a) * up).astype(jnp.bfloat16)
        o_ref[...] = _bdot(act, wdb[...])

    @pl.when(i >= nact)
    def _():
        o_ref[...] = jnp.zeros_like(o_ref)

    @pl.when((i == last) & (i < nact))
    def _():
        wait(1 - slot)


def _experts_call(tile_expert, nact, seg, e0, e1, r0, r1, xp, wg, wu, wd):
    n_pre = 7
    idx = lambda i, te, *_: (te[i], 0, 0)
    grid_spec = pltpu.PrefetchScalarGridSpec(
        num_scalar_prefetch=n_pre,
        grid=(EXP_TILES,),
        in_specs=[
            pl.BlockSpec(memory_space=pl.ANY),
            pl.BlockSpec((1, D_MODEL, D_EXPERT), idx),
            pl.BlockSpec((1, D_MODEL, D_EXPERT), idx),
            pl.BlockSpec((1, D_EXPERT, D_MODEL), idx),
        ],
        out_specs=pl.BlockSpec((EXP_TM, D_MODEL), lambda i, *_: (i, 0)),
        scratch_shapes=[
            pltpu.SMEM((EXP_TILES * EXP_TM,), jnp.int32),
            pltpu.VMEM((2, EXP_TM, HALF), jnp.uint32),
            pltpu.VMEM((D_MODEL, D_EXPERT), jnp.bfloat16),
            pltpu.VMEM((D_MODEL, D_EXPERT), jnp.bfloat16),
            pltpu.VMEM((D_EXPERT, D_MODEL), jnp.bfloat16),
            pltpu.SemaphoreType.DMA((2,)),
        ],
    )
    return pl.pallas_call(
        _experts_kernel,
        grid_spec=grid_spec,
        out_shape=jax.ShapeDtypeStruct((EXP_TILES * EXP_TM, D_MODEL), jnp.float32),
        compiler_params=pltpu.CompilerParams(
            dimension_semantics=("arbitrary",), vmem_limit_bytes=VMEM_LIMIT),
        name="experts",
    )(tile_expert, nact, seg, e0, e1, r0, r1, xp, wg, wu, wd)


GATHER_UNROLL = 8


def _combine_kernel(seg_ref, e0_ref, e1_ref, r0_ref, r1_ref,
                    h_ref, cw_ref, mod_ref, g_ref, b_ref, ys_hbm,
                    o_ref,
                    ybuf, sem):
    i = pl.program_id(0)
    n = pl.num_programs(0)
    slot = lax.rem(i, 2)

    def issue(tile, s):
        def body(kk, c):
            for u in range(GATHER_UNROLL):
                k = kk * GATHER_UNROLL + u
                t = tile * CMB_TM + k
                p0 = seg_ref[e0_ref[t]] + r0_ref[t]
                p1 = seg_ref[e1_ref[t]] + r1_ref[t]
                pltpu.make_async_copy(ys_hbm.at[pl.ds(p0, 1), :], ybuf.at[s, 0, pl.ds(k, 1), :],
                                      sem.at[s]).start(priority=0)
                pltpu.make_async_copy(ys_hbm.at[pl.ds(p1, 1), :], ybuf.at[s, 1, pl.ds(k, 1), :],
                                      sem.at[s]).start(priority=1)
            return c
        lax.fori_loop(0, CMB_TM // GATHER_UNROLL, body, 0)

    @pl.when(i == 0)
    def _():
        issue(0, 0)

    @pl.when(i + 1 < n)
    def _():
        issue(i + 1, 1 - slot)

    for half in range(2):
        pltpu.make_async_copy(ys_hbm.at[pl.ds(0, CMB_TM), :], ybuf.at[slot, half], sem.at[slot]).wait()

    c0 = cw_ref[:, CW_C0:CW_C0 + 1]
    c1 = cw_ref[:, CW_C1:CW_C1 + 1]
    ffn = c0 * ybuf[slot, 0] + c1 * ybuf[slot, 1]
    g2 = mod_ref[0:1, 5 * D_MODEL:6 * D_MODEL]
    o_ref[...] = _layer_norm(DEEPNORM_ALPHA * h_ref[...] + g2 * ffn, g_ref[...], b_ref[...])


def _combine_call(seg, e0, e1, r0, r1, h, cw, mod, g, b, ys):
    grid_spec = pltpu.PrefetchScalarGridSpec(
        num_scalar_prefetch=5,
        grid=(SEQ // CMB_TM,),
        in_specs=[
            pl.BlockSpec((CMB_TM, D_MODEL), lambda i, *_: (i, 0)),
            pl.BlockSpec((CMB_TM, LANES), lambda i, *_: (i, 0)),
            pl.BlockSpec(mod.shape, lambda i, *_: (0, 0)),
            pl.BlockSpec((1, D_MODEL), lambda i, *_: (0, 0)),
            pl.BlockSpec((1, D_MODEL), lambda i, *_: (0, 0)),
            pl.BlockSpec(memory_space=pl.ANY),
        ],
        out_specs=pl.BlockSpec((CMB_TM, D_MODEL), lambda i, *_: (i, 0)),
        scratch_shapes=[
            pltpu.VMEM((2, 2, CMB_TM, D_MODEL), jnp.float32),
            pltpu.SemaphoreType.DMA((2,)),
        ],
    )
    return pl.pallas_call(
        _combine_kernel,
        grid_spec=grid_spec,
        out_shape=jax.ShapeDtypeStruct((SEQ, D_MODEL), jnp.float32),
        compiler_params=pltpu.CompilerParams(
            dimension_semantics=("arbitrary",), vmem_limit_bytes=VMEM_LIMIT),
        name="combine",
    )(seg, e0, e1, r0, r1, h, cw, mod, g, b, ys)


def kernel(x, c, ctx, c_ctx, ln_in_g, ln_in_b, w_mod, b_mod, w_in, rpb, w_pool_grp, pool_scale,
           w_attn_proj, w_pool_proj, w_out, ln1_g, ln1_b, w_router_group, b_router_group,
           w_router_expert, b_router_expert, w_expert_gate, w_expert_up, w_expert_down, ln2_g, ln2_b):
    assert x.shape == (1, SEQ, D_MODEL) and ctx.shape == (1, CTX_LEN, D_MODEL)
    assert w_mod.shape[0] == 1, "single-layer trunk"
    f32, bf16 = jnp.float32, jnp.bfloat16
    row = lambda v: v.reshape(1, -1).astype(f32)

    cond = jnp.concatenate([c, c_ctx[None], jnp.zeros((MOD_ROWS - 2, D_MODEL), f32)], axis=0)
    mod = _mod_call(cond, w_mod[0], row(b_mod[0]))

    lng, lnb = row(ln_in_g), row(ln_in_b)
    w_in_b = w_in[0].astype(bf16)
    xs = x[0]
    u = _proj_call(xs, mod, lng, lnb, w_in_b, mod_row=0, q_cols=ATT_W, tm=PROJ_TM)
    kvc = _proj_call(ctx[0], mod, lng, lnb, w_in_b[:, ATT_W:3 * ATT_W], mod_row=1, q_cols=0, tm=CTX_LEN)

    h1 = _mix_call(xs, mod, lng, lnb, u, kvc, _attn_bias_table(rpb[0]),
                   w_pool_grp[0].astype(bf16), row(pool_scale[0]),
                   w_attn_proj[0].astype(bf16), w_pool_proj[0].astype(bf16), w_out[0].astype(bf16),
                   row(ln1_g[0]), row(ln1_b[0]))

    n_logit = N_GROUPS + N_EXPERTS
    wrt = jnp.concatenate([w_router_group[0].T, w_router_expert[0].T,
                           jnp.zeros((LANES - n_logit, D_MODEL), f32)], axis=0)
    brt = jnp.concatenate([b_router_group[0], b_router_expert[0], jnp.zeros((LANES - n_logit,), f32)])
    brt = jnp.broadcast_to(brt[:, None], (LANES, LANES))
    xp, ids, cw, cnt = _route_call(h1, mod, wrt, brt)

    counts = cnt[N_GROUPS:n_logit, 0]
    tiles_per = (counts + EXP_TM - 1) // EXP_TM
    tile_end = jnp.cumsum(tiles_per)
    seg = (tile_end - tiles_per) * EXP_TM
    nact = tile_end[-1:]
    tile_ids = jnp.arange(EXP_TILES, dtype=jnp.int32)
    tile_expert = jnp.sum((tile_ids[:, None] >= tile_end[None, :]).astype(jnp.int32), axis=1)
    tile_expert = jnp.minimum(tile_expert, N_EXPERTS - 1)
    last_active = jnp.sum((nact - 1 >= tile_end).astype(jnp.int32))
    tile_expert = jnp.where(tile_ids < nact, tile_expert, jnp.minimum(last_active, N_EXPERTS - 1))
    e0, e1, r0, r1 = ids[ID_E0], ids[ID_E1], ids[ID_R0], ids[ID_R1]

    ys = _experts_call(tile_expert, nact, seg, e0, e1, r0, r1, xp,
                       w_expert_gate[0], w_expert_up[0], w_expert_down[0])
    out = _combine_call(seg, e0, e1, r0, r1, h1, cw, mod, row(ln2_g[0]), row(ln2_b[0]), ys)
    return out[None]
```

```python
import functools

import jax
import jax.numpy as jnp
from jax import lax
from jax.experimental import pallas as pl
from jax.experimental.pallas import tpu as pltpu

D_MODEL = 1024
SEQ = 16384
GRID_W = 64
ROWS = SEQ // GRID_W
CTX_LEN = 256
N_HEADS = 8
HEAD_DIM = 64
ATT_W = N_HEADS * HEAD_DIM
WIN_H = 8
WIN_W = 16
POOL_WINDOWS = (2, 4, 8, 16)
POOL_GROUPS = 4
POOL_DIM = 128
POOL_W = POOL_GROUPS * POOL_DIM
PROJ_W = 3 * ATT_W + POOL_W + 2 * D_MODEL
N_GROUPS = 4
EXPERTS_PER_GROUP = 8
N_EXPERTS = N_GROUPS * EXPERTS_PER_GROUP
D_EXPERT = 512
N_MOD = 6
DEEPNORM_ALPHA = 2.0 ** 0.25
LN_EPS = 1e-5
NEG_INF = -1e30

LANES = 128
ROW_TILE = 8
MOD_ROWS = 8
PROJ_TM = 512
MIX_ROWS = 8
MIX_TQ = MIX_ROWS * GRID_W
KV_HALO = 4 * GRID_W
POOL_HALO = 16
ROUTE_TM = 512
EXP_TM = 256
EXP_TILES = 2 * SEQ // EXP_TM + N_EXPERTS
CMB_TM = 256
HALF = D_MODEL // 2
VMEM_LIMIT = 56 * 1024 * 1024


def _layer_norm(x, g, b):
    mu = jnp.mean(x, axis=-1, keepdims=True)
    xc = x - mu
    var = jnp.mean(xc * xc, axis=-1, keepdims=True)
    return xc * lax.rsqrt(var + LN_EPS) * g + b


def _bdot(a, b):
    return jnp.dot(a, b, preferred_element_type=jnp.float32)


def _split_bf16(a):
    hi = a.astype(jnp.bfloat16)
    lo = (a - hi.astype(jnp.float32)).astype(jnp.bfloat16)
    return hi, lo


def _dot3(a, b):
    a_hi, a_lo = _split_bf16(a)
    b_hi, b_lo = _split_bf16(b)
    return _bdot(a_hi, b_hi) + (_bdot(a_hi, b_lo) + _bdot(a_lo, b_hi))


def _load_row_tiles(ref, tokens, lead=()):
    parts = [ref[(*lead, pl.ds(j, tokens, stride=ROW_TILE), slice(None))] for j in range(ROW_TILE)]
    return jnp.concatenate(parts, axis=-1)


def _store_row_tiles(ref, value, lead=()):
    tokens = value.shape[0]
    for j in range(ROW_TILE):
        ref[(*lead, pl.ds(j, tokens, stride=ROW_TILE), slice(None))] = value[:, j * LANES:(j + 1) * LANES]


def _mod_kernel(cond_ref, w_ref, b_ref, o_ref):
    cond = cond_ref[...]
    act = cond * jax.nn.sigmoid(cond)
    o_ref[...] = _dot3(act, w_ref[...]) + b_ref[...]


def _mod_call(cond, w_mod, b_mod):
    tn = 1536
    n = N_MOD * D_MODEL
    return pl.pallas_call(
        _mod_kernel,
        grid=(n // tn,),
        in_specs=[
            pl.BlockSpec((MOD_ROWS, D_MODEL), lambda i: (0, 0)),
            pl.BlockSpec((D_MODEL, tn), lambda i: (0, i)),
            pl.BlockSpec((1, tn), lambda i: (0, i)),
        ],
        out_specs=pl.BlockSpec((MOD_ROWS, tn), lambda i: (0, i)),
        out_shape=jax.ShapeDtypeStruct((MOD_ROWS, n), jnp.float32),
        compiler_params=pltpu.CompilerParams(
            dimension_semantics=("arbitrary",), vmem_limit_bytes=VMEM_LIMIT),
        name="mod",
    )(cond, w_mod, b_mod)


def _proj_kernel(x_ref, mod_ref, g_ref, b_ref, w_ref, o_ref, *, mod_row, q_cols):
    h = _layer_norm(x_ref[...], g_ref[...], b_ref[...])
    shift = mod_ref[mod_row:mod_row + 1, 0:D_MODEL]
    scale = mod_ref[mod_row:mod_row + 1, D_MODEL:2 * D_MODEL]
    hm = (h * (1.0 + scale) + shift).astype(jnp.bfloat16)
    n = o_ref.shape[1]
    for c in range(n // D_MODEL):
        sl = slice(c * D_MODEL, (c + 1) * D_MODEL)
        res = _bdot(hm, w_ref[:, sl])
        if c == 0 and q_cols:
            lane = lax.broadcasted_iota(jnp.int32, (1, D_MODEL), 1)
            res = res * jnp.where(lane < q_cols, HEAD_DIM ** -0.5, 1.0)
        o_ref[:, sl] = res.astype(jnp.bfloat16)


def _proj_call(x, mod, g, b, w, *, mod_row, q_cols, tm):
    rows, n = x.shape[0], w.shape[1]
    return pl.pallas_call(
        functools.partial(_proj_kernel, mod_row=mod_row, q_cols=q_cols),
        grid=(rows // tm,),
        in_specs=[
            pl.BlockSpec((tm, D_MODEL), lambda i: (i, 0)),
            pl.BlockSpec(mod.shape, lambda i: (0, 0)),
            pl.BlockSpec((1, D_MODEL), lambda i: (0, 0)),
            pl.BlockSpec((1, D_MODEL), lambda i: (0, 0)),
            pl.BlockSpec((D_MODEL, n), lambda i: (0, 0), pipeline_mode=pl.Buffered(1)),
        ],
        out_specs=pl.BlockSpec((tm, n), lambda i: (i, 0)),
        out_shape=jax.ShapeDtypeStruct((rows, n), jnp.bfloat16),
        compiler_params=pltpu.CompilerParams(
            dimension_semantics=("arbitrary",), vmem_limit_bytes=VMEM_LIMIT),
        name="proj",
    )(x, mod, g, b, w)


def _attn_bias_table(rpb):
    col = jnp.arange(GRID_W, dtype=jnp.int32)
    col_start = jnp.clip(col - WIN_W // 2, 0, GRID_W - WIN_W)
    col_mask = (col[None, :] >= col_start[:, None]) & (col[None, :] < col_start[:, None] + WIN_W)
    col_off = jnp.clip(col[None, :] - col[:, None], 1 - WIN_W, WIN_W - 1) + (WIN_W - 1)
    onehot = (col_off[None] == jnp.arange(2 * WIN_W - 1, dtype=jnp.int32)[:, None, None]).astype(jnp.float32)
    tab = jnp.einsum("hrc,cqk->hrqk", rpb.astype(jnp.float32), onehot, precision=lax.Precision.HIGHEST)
    tab = jnp.where(col_mask[None, None], tab, NEG_INF)
    tab = jnp.stack([tab[:, WIN_H - 1 - v:2 * WIN_H - 1 - v] for v in range(WIN_H)], axis=0)
    tab = tab.transpose(0, 1, 3, 2, 4)
    return tab.reshape(WIN_H, N_HEADS // 2, 2 * GRID_W, WIN_H * GRID_W)


def _mix_kernel(x_ref, mod_ref, lng_ref, lnb_ref,
                q_ref, kp_ref, kc_ref, kn_ref, vp_ref, vc_ref, vn_ref,
                pp_ref, pc_ref, pn_ref, ga_ref, gb_ref,
                kvc_ref, bias_ref, wgrp_ref, pscale_ref, wap_ref, wpp_ref, wout_ref,
                ln1g_ref, ln1b_ref,
                o_ref,
                kbuf, vbuf, yabuf, pbuf, ypbuf):
    b = pl.program_id(0)
    nb = pl.num_programs(0)

    kbuf[0:KV_HALO, :] = kp_ref[...]
    kbuf[KV_HALO:KV_HALO + MIX_TQ, :] = kc_ref[...]
    kbuf[KV_HALO + MIX_TQ:, :] = kn_ref[...]
    vbuf[0:KV_HALO, :] = vp_ref[...]
    vbuf[KV_HALO:KV_HALO + MIX_TQ, :] = vc_ref[...]
    vbuf[KV_HALO + MIX_TQ:, :] = vn_ref[...]

    lane = lax.broadcasted_iota(jnp.int32, (GRID_W, LANES), 1)
    first_head = lane < HEAD_DIM

    def row_body(j, carry):
        r = b * MIX_ROWS + j
        rs = jnp.clip(r - WIN_H // 2, 0, ROWS - WIN_H)
        off = pl.multiple_of((rs - b * MIX_ROWS + WIN_H // 2) * GRID_W, GRID_W)
        var = r - rs
        qoff = pl.multiple_of(j * GRID_W, GRID_W)
        for pair in range(N_HEADS // 2):
            cols = slice(pair * LANES, (pair + 1) * LANES)
            q = q_ref[pl.ds(qoff, GRID_W), cols]
            zero = jnp.zeros_like(q)
            q2 = jnp.concatenate([jnp.where(first_head, q, zero), jnp.where(first_head, zero, q)], axis=0)
            kw = kbuf[pl.ds(off, WIN_H * GRID_W), cols]
            vw = vbuf[pl.ds(off, WIN_H * GRID_W), cols]
            kctx = kvc_ref[:, cols]
            vctx = kvc_ref[:, ATT_W + pair * LANES:ATT_W + (pair + 1) * LANES]
            nt = (((1,), (1,)), ((), ()))
            s_loc = lax.dot_general(q2, kw, nt, preferred_element_type=jnp.float32) + bias_ref[var, pair]
            s_ctx = lax.dot_general(q2, kctx, nt, preferred_element_type=jnp.float32)
            m = jnp.maximum(jnp.max(s_loc, axis=-1, keepdims=True), jnp.max(s_ctx, axis=-1, keepdims=True))
            p_loc = jnp.exp(s_loc - m)
            p_ctx = jnp.exp(s_ctx - m)
            denom = jnp.sum(p_loc, axis=-1, keepdims=True) + jnp.sum(p_ctx, axis=-1, keepdims=True)
            o2 = _bdot(p_loc.astype(jnp.bfloat16), vw) + _bdot(p_ctx.astype(jnp.bfloat16), vctx)
            o2 = o2 * (1.0 / denom)
            o_pair = jnp.where(first_head, o2[:GRID_W], o2[GRID_W:])
            yabuf[pl.ds(qoff, GRID_W), cols] = o_pair.astype(jnp.bfloat16)
        return carry

    lax.fori_loop(0, MIX_ROWS, row_body, 0)

    pbuf[0:POOL_HALO, :] = jnp.where(b > 0, pp_ref[...].astype(jnp.float32), 0.0)
    pbuf[POOL_HALO:POOL_HALO + MIX_TQ, :] = pc_ref[...].astype(jnp.float32)
    pbuf[POOL_HALO + MIX_TQ:, :] = jnp.where(b < nb - 1, pn_ref[...].astype(jnp.float32), 0.0)
    t_abs = b * MIX_TQ + lax.broadcasted_iota(jnp.int32, (MIX_TQ, 1), 0)
    for g, win in enumerate(POOL_WINDOWS):
        cols = slice(g * POOL_DIM, (g + 1) * POOL_DIM)
        acc = None
        for d in range(-(win // 2), win - win // 2):
            term = pbuf[POOL_HALO + d:POOL_HALO + d + MIX_TQ, cols]
            acc = term if acc is None else acc + term
        count = jnp.minimum(t_abs + (win - win // 2), SEQ) - jnp.maximum(t_abs - win // 2, 0)
        pooled = acc / count.astype(jnp.float32) - pbuf[POOL_HALO:POOL_HALO + MIX_TQ, cols]
        yp = _bdot(pooled.astype(jnp.bfloat16), wgrp_ref[g]) * pscale_ref[:, cols]
        ypbuf[:, cols] = yp.astype(jnp.bfloat16)

    ya = _bdot(yabuf[...], wap_ref[...])
    yp = _bdot(ypbuf[...], wpp_ref[...])
    z = jax.nn.sigmoid(ga_ref[...].astype(jnp.float32)) * ya + jax.nn.sigmoid(gb_ref[...].astype(jnp.float32)) * yp
    y = _bdot(z.astype(jnp.bfloat16), wout_ref[...])
    h = _layer_norm(x_ref[...], lng_ref[...], lnb_ref[...])
    g1 = mod_ref[0:1, 2 * D_MODEL:3 * D_MODEL]
    _store_row_tiles(o_ref, _layer_norm(DEEPNORM_ALPHA * h + g1 * y, ln1g_ref[...], ln1b_ref[...]))


def _mix_call(x, mod, lng, lnb, u, kvc, bias, wgrp, pscale, wap, wpp, wout, ln1g, ln1b):
    nb = SEQ // MIX_TQ
    halo_per_blk = MIX_TQ // KV_HALO
    n_halo = SEQ // KV_HALO
    ph_per_blk = MIX_TQ // POOL_HALO
    n_ph = SEQ // POOL_HALO

    def const(shape):
        return pl.BlockSpec(shape, lambda i: (0,) * len(shape), pipeline_mode=pl.Buffered(1))

    def prev_halo(c):
        return pl.BlockSpec((KV_HALO, ATT_W), lambda i: (jnp.maximum(i * halo_per_blk - 1, 0), c))

    def next_halo(c):
        return pl.BlockSpec((KV_HALO, ATT_W), lambda i: (jnp.minimum((i + 1) * halo_per_blk, n_halo - 1), c))

    def cur(c):
        return pl.BlockSpec((MIX_TQ, ATT_W), lambda i: (i, c))

    in_specs = [
        pl.BlockSpec((MIX_TQ, D_MODEL), lambda i: (i, 0)),
        const(mod.shape), const((1, D_MODEL)), const((1, D_MODEL)),
        cur(0),
        prev_halo(1), cur(1), next_halo(1),
        prev_halo(2), cur(2), next_halo(2),
        pl.BlockSpec((POOL_HALO, POOL_W), lambda i: (jnp.maximum(i * ph_per_blk - 1, 0), 3)),
        cur(3),
        pl.BlockSpec((POOL_HALO, POOL_W), lambda i: (jnp.minimum((i + 1) * ph_per_blk, n_ph - 1), 3)),
        pl.BlockSpec((MIX_TQ, D_MODEL), lambda i: (i, 2)),
        pl.BlockSpec((MIX_TQ, D_MODEL), lambda i: (i, 3)),
        const(kvc.shape), const(bias.shape), const(wgrp.shape), const(pscale.shape),
        const(wap.shape), const(wpp.shape), const(wout.shape),
        const((1, D_MODEL)), const((1, D_MODEL)),
    ]
    return pl.pallas_call(
        _mix_kernel,
        grid=(nb,),
        in_specs=in_specs,
        out_specs=pl.BlockSpec((MIX_TQ * ROW_TILE, LANES), lambda i: (i, 0)),
        out_shape=jax.ShapeDtypeStruct((SEQ * ROW_TILE, LANES), jnp.float32),
        scratch_shapes=[
            pltpu.VMEM((MIX_TQ + 2 * KV_HALO, ATT_W), jnp.bfloat16),
            pltpu.VMEM((MIX_TQ + 2 * KV_HALO, ATT_W), jnp.bfloat16),
            pltpu.VMEM((MIX_TQ, ATT_W), jnp.bfloat16),
            pltpu.VMEM((MIX_TQ + 2 * POOL_HALO, POOL_W), jnp.float32),
            pltpu.VMEM((MIX_TQ, POOL_W), jnp.bfloat16),
        ],
        compiler_params=pltpu.CompilerParams(
            dimension_semantics=("arbitrary",), vmem_limit_bytes=VMEM_LIMIT),
        name="mix",
    )(x, mod, lng, lnb, u, u, u, u, u, u, u, u, u, u, u, u,
      kvc, bias, wgrp, pscale, wap, wpp, wout, ln1g, ln1b)


ID_E0, ID_E1, ID_R0, ID_R1 = 0, 1, 4, 5
POS_A, POS_B = 0, 1
PLAN_EXPERT, PLAN_VALID, PLAN_NACT = 0, 1, 2
PLAN_W = 2 * LANES
CW_C0, CW_C1 = 0, 1


def _route_kernel(h_ref, mod_ref, wrt_ref, brt_ref, pos_ref, cw_ref, plan_ref, carry_ref, ids_all):
    i = pl.program_id(0)
    tm = ROUTE_TM

    @pl.when(i == 0)
    def _():
        carry_ref[...] = jnp.zeros_like(carry_ref)

    shift = mod_ref[0:1, 3 * D_MODEL:4 * D_MODEL]
    scale = mod_ref[0:1, 4 * D_MODEL:5 * D_MODEL]
    hm = _load_row_tiles(h_ref, tm) * (1.0 + scale) + shift

    hm_hi, hm_lo = _split_bf16(hm)
    w_hi, w_lo = _split_bf16(wrt_ref[...])
    nt = (((1,), (1,)), ((), ()))
    dg = functools.partial(lax.dot_general, dimension_numbers=nt, preferred_element_type=jnp.float32)
    logits = dg(w_hi, hm_hi) + (dg(w_hi, hm_lo) + dg(w_lo, hm_hi)) + brt_ref[:, 0:1]

    sub = lax.broadcasted_iota(jnp.int32, (LANES, tm), 0)
    big = jnp.int32(1 << 20)
    is_grp = sub < N_GROUPS
    gl = jnp.where(is_grp, logits, -jnp.inf)
    gmax = jnp.max(gl, axis=0, keepdims=True)
    gidx = jnp.min(jnp.where(gl == gmax, sub, big), axis=0, keepdims=True)
    gsum = jnp.sum(jnp.where(is_grp, jnp.exp(logits - gmax), 0.0), axis=0, keepdims=True)
    p_group = 1.0 / gsum

    eid = sub - N_GROUPS
    sel = (eid >= 0) & (eid < N_EXPERTS) & (lax.shift_right_arithmetic(eid, 3) == gidx)
    el = jnp.where(sel, logits, -jnp.inf)
    l0 = jnp.max(el, axis=0, keepdims=True)
    i0 = jnp.min(jnp.where(el == l0, sub, big), axis=0, keepdims=True)
    el2 = jnp.where(sub == i0, -jnp.inf, el)
    l1 = jnp.max(el2, axis=0, keepdims=True)
    i1 = jnp.min(jnp.where(el2 == l1, sub, big), axis=0, keepdims=True)
    t = jnp.exp(l1 - l0)
    w0 = 1.0 / (1.0 + t)
    w1 = t / (1.0 + t)

    onehot = jnp.where((sub == i0) | (sub == i1), 1.0, 0.0)
    rr = lax.broadcasted_iota(jnp.int32, (tm, tm), 0)
    cc = lax.broadcasted_iota(jnp.int32, (tm, tm), 1)
    earlier = jnp.where(rr < cc, 1.0, 0.0).astype(jnp.bfloat16)
    carry = carry_ref[:, 0:1]
    prefix = _bdot(onehot.astype(jnp.bfloat16), earlier) + carry
    r0 = jnp.sum(jnp.where(sub == i0, prefix, 0.0), axis=0, keepdims=True)
    r1 = jnp.sum(jnp.where(sub == i1, prefix, 0.0), axis=0, keepdims=True)
    total = jnp.broadcast_to(carry + jnp.sum(onehot, axis=1, keepdims=True), carry_ref.shape)
    carry_ref[...] = total

    sub8 = lax.broadcasted_iota(jnp.int32, (ROW_TILE, tm), 0)
    ids = jnp.zeros((ROW_TILE, tm), jnp.int32)
    for idx, val in ((ID_E0, i0 - N_GROUPS), (ID_E1, i1 - N_GROUPS),
                     (ID_R0, r0.astype(jnp.int32)), (ID_R1, r1.astype(jnp.int32))):
        ids = jnp.where(sub8 == idx, val, ids)
    ids_all[:, pl.ds(pl.multiple_of(i * tm, tm), tm)] = ids

    cwt = jnp.where(sub == CW_C0, p_group * w0, jnp.where(sub == CW_C1, p_group * w1, 0.0))
    cw_ref[...] = cwt.T

    @pl.when(i == pl.num_programs(0) - 1)
    def _():
        subq = lax.broadcasted_iota(jnp.int32, (LANES, LANES), 0)
        laneq = lax.broadcasted_iota(jnp.int32, (LANES, LANES), 1)
        cnt = total.astype(jnp.int32)
        tiles = lax.shift_right_logical(cnt + (EXP_TM - 1), EXP_TM.bit_length() - 1).astype(jnp.float32)
        incl = jnp.where(laneq <= subq, 1.0, 0.0).astype(jnp.bfloat16)
        tile_end = _bdot(incl, tiles.astype(jnp.bfloat16))
        tile_start = tile_end - tiles
        seg = (tile_start * EXP_TM).astype(jnp.int32)
        nact = jnp.max(tile_end, axis=0, keepdims=True)

        ids_full = ids_all[...]
        look = jnp.zeros_like(ids_full)
        for e in range(N_EXPERTS):
            look = jnp.where(ids_full == e, seg[N_GROUPS + e, 0], look)
        pos_ref[...] = look + pltpu.roll(ids_full, ID_R0 - ID_E0, axis=0)

        subp = lax.broadcasted_iota(jnp.int32, (LANES, PLAN_W), 0)
        tile = lax.broadcasted_iota(jnp.int32, (LANES, PLAN_W), 1).astype(jnp.float32)
        is_exp = (subp >= N_GROUPS) & (subp < N_GROUPS + N_EXPERTS)
        end_col = tile_end[:, 0:1]
        nact_s = nact[:, 0:1]
        te = jnp.sum(jnp.where(is_exp & (tile >= end_col), 1.0, 0.0), axis=0, keepdims=True)
        te_last = jnp.sum(jnp.where(is_exp & (nact_s - 1.0 >= end_col), 1.0, 0.0), axis=0, keepdims=True)[:, 0:1]
        tile_row = tile[0:1, :]
        te = jnp.minimum(jnp.where(tile_row < nact_s, te, te_last), N_EXPERTS - 1.0)
        mine = (subp - N_GROUPS).astype(jnp.float32) == te
        cnt_sel = jnp.sum(jnp.where(mine, total[:, 0:1], 0.0), axis=0, keepdims=True)
        start_sel = jnp.sum(jnp.where(mine, tile_start[:, 0:1], 0.0), axis=0, keepdims=True)
        valid = jnp.clip(cnt_sel - (tile_row - start_sel) * EXP_TM, 0.0, float(EXP_TM))
        valid = jnp.where(tile_row < nact_s, valid, 0.0)
        subr = lax.broadcasted_iota(jnp.int32, (ROW_TILE, PLAN_W), 0)
        plan = jnp.where(subr == PLAN_EXPERT, te, jnp.where(subr == PLAN_VALID, valid,
                         jnp.where(subr == PLAN_NACT, nact_s, 0.0)))
        plan_ref[...] = plan.astype(jnp.int32)


def _route_call(h, mod, wrt, brt):
    tm = ROUTE_TM
    return pl.pallas_call(
        _route_kernel,
        grid=(SEQ // tm,),
        in_specs=[
            pl.BlockSpec((tm * ROW_TILE, LANES), lambda i: (i, 0)),
            pl.BlockSpec(mod.shape, lambda i: (0, 0)),
            pl.BlockSpec((LANES, D_MODEL), lambda i: (0, 0)),
            pl.BlockSpec((LANES, LANES), lambda i: (0, 0)),
        ],
        out_specs=[
            pl.BlockSpec((ROW_TILE, SEQ), lambda i: (0, 0)),
            pl.BlockSpec((tm, LANES), lambda i: (i, 0)),
            pl.BlockSpec((ROW_TILE, PLAN_W), lambda i: (0, 0)),
        ],
        out_shape=[
            jax.ShapeDtypeStruct((ROW_TILE, SEQ), jnp.int32),
            jax.ShapeDtypeStruct((SEQ, LANES), jnp.float32),
            jax.ShapeDtypeStruct((ROW_TILE, PLAN_W), jnp.int32),
        ],
        scratch_shapes=[pltpu.VMEM((LANES, LANES), jnp.float32),
                        pltpu.VMEM((ROW_TILE, SEQ), jnp.int32)],
        compiler_params=pltpu.CompilerParams(
            dimension_semantics=("arbitrary",), vmem_limit_bytes=VMEM_LIMIT),
        name="route",
    )(h, mod, wrt, brt)


SRC_UNROLL = 8
EXP_CHUNK = 256
TILE_ROWS = EXP_TM * ROW_TILE


def _experts_kernel(te_ref, tv_ref, nact_ref, posa_ref, posb_ref,
                    h_hbm, mod_ref, wg_ref, wu_ref, wd_ref,
                    y_hbm,
                    src_ref, xbuf, ybuf, wgb, wub, wdb, gsem, ssem):
    i = pl.program_id(0)
    last = pl.num_programs(0) - 1
    nact = nact_ref[0]
    slot = lax.rem(i, 2)
    other = 1 - slot

    def gather_copy(p, k, s):
        tok = src_ref[p] & (SEQ - 1)
        return pltpu.make_async_copy(h_hbm.at[pl.ds(pl.multiple_of(tok * ROW_TILE, ROW_TILE), ROW_TILE), :],
                                     xbuf.at[s, pl.ds(pl.multiple_of(k * ROW_TILE, ROW_TILE), ROW_TILE), :],
                                     gsem.at[s])

    def scatter_copy(p, k, s):
        dst = src_ref[p]
        return pltpu.make_async_copy(ybuf.at[s, pl.ds(pl.multiple_of(k * ROW_TILE, ROW_TILE), ROW_TILE), :],
                                     y_hbm.at[pl.ds(pl.multiple_of(dst * ROW_TILE, ROW_TILE), ROW_TILE), :],
                                     ssem.at[s])

    def gather_rolled(tile, s):
        def body(kk, c):
            for u in range(SRC_UNROLL):
                k = kk * SRC_UNROLL + u
                gather_copy(tile * EXP_TM + k, k, s).start()
            return c
        lax.fori_loop(0, EXP_TM // SRC_UNROLL, body, 0)

    def scatter_rolled(tile, s, n):
        def body(k, c):
            scatter_copy(tile * EXP_TM + k, k, s).start()
            return c
        lax.fori_loop(0, n, body, 0)

    def wait_gather(s):
        pltpu.make_async_copy(h_hbm.at[pl.ds(0, TILE_ROWS), :], xbuf.at[s], gsem.at[s]).wait()

    def wait_scatter(s, n):
        rows = pl.multiple_of(n * ROW_TILE, ROW_TILE)
        pltpu.make_async_copy(ybuf.at[s, pl.ds(0, rows), :], y_hbm.at[pl.ds(0, rows), :], ssem.at[s]).wait()

    def compute_chunks(s):
        state = {}

        def load():
            x = _load_row_tiles(xbuf, EXP_TM, lead=(s,))
            shift = mod_ref[0:1, 3 * D_MODEL:4 * D_MODEL]
            scale = mod_ref[0:1, 4 * D_MODEL:5 * D_MODEL]
            state["x"] = (x * (1.0 + scale) + shift).astype(jnp.bfloat16)
            state["act"] = []

        def gate(c):
            def run():
                if c == 0:
                    load()
                state["a"] = _bdot(state["x"], wgb[:, c * EXP_CHUNK:(c + 1) * EXP_CHUNK])
            return run

        def up(c):
            def run():
                a = state["a"]
                u = _bdot(state["x"], wub[:, c * EXP_CHUNK:(c + 1) * EXP_CHUNK])
                state["act"].append((a * jax.nn.sigmoid(a) * u).astype(jnp.bfloat16))
            return run

        def down(c):
            def run():
                if c == 0:
                    state["actf"] = jnp.concatenate(state["act"], axis=-1)
                yc = _bdot(state["actf"], wdb[:, c * EXP_CHUNK:(c + 1) * EXP_CHUNK])
                for jj in range(EXP_CHUNK // LANES):
                    j = c * (EXP_CHUNK // LANES) + jj
                    ybuf[s, pl.ds(j, EXP_TM, stride=ROW_TILE), :] = yc[:, jj * LANES:(jj + 1) * LANES]
            return run

        steps = []
        for c in range(D_EXPERT // EXP_CHUNK):
            steps += [(gate(c), 2), (up(c), 2)]
        return steps + [(down(c), 1) for c in range(D_MODEL // EXP_CHUNK)]

    def run_interleaved(chunks, dmas):
        total_cost = sum(cost for _, cost in chunks)
        done = 0
        for chunk, cost in chunks:
            upto = -(-len(dmas) * (done + cost) // total_cost)
            for d in dmas[-(-len(dmas) * done // total_cost):upto]:
                d()
            done += cost
            chunk()

    def gather_dmas(tile, s):
        return [functools.partial(lambda k: gather_copy(tile * EXP_TM + k, k, s).start(priority=0), k)
                for k in range(EXP_TM)]

    def scatter_dmas(tile, s):
        return [functools.partial(lambda k: scatter_copy(tile * EXP_TM + k, k, s).start(priority=1), k)
                for k in range(EXP_TM)]

    @pl.when(i == 0)
    def _():
        def zero_body(p, c):
            for u in range(SRC_UNROLL):
                src_ref[p * SRC_UNROLL + u] = 0
            return c
        lax.fori_loop(0, EXP_TILES * EXP_TM // SRC_UNROLL, zero_body, 0)

        def fill_body(tt, c):
            ts = [tt * SRC_UNROLL + u for u in range(SRC_UNROLL)]
            pa = [posa_ref[t] for t in ts]
            pb = [posb_ref[t] for t in ts]
            for t, a, b in zip(ts, pa, pb):
                src_ref[a] = t
                src_ref[b] = t + SEQ
            return c
        lax.fori_loop(0, SEQ // SRC_UNROLL, fill_body, 0)
        gather_rolled(0, 0)

    active = i < nact
    prev = jnp.maximum(i - 1, 0)
    prev_ok = (i >= 1) & (i - 1 < nact)
    nprev = tv_ref[prev]
    full_prev = prev_ok & (nprev == EXP_TM)
    nxt = jnp.minimum(i + 1, last)

    @pl.when(i <= nact)
    def _():
        wait_gather(slot)

    @pl.when(active & ((i == 0) | (te_ref[i] != te_ref[prev])))
    def _():
        wgb[...] = wg_ref[0].astype(jnp.bfloat16)
        wub[...] = wu_ref[0].astype(jnp.bfloat16)
        wdb[...] = wd_ref[0].astype(jnp.bfloat16)

    @pl.when(active & jnp.logical_not(full_prev) & prev_ok)
    def _():
        scatter_rolled(prev, other, nprev)

    @pl.when(active & full_prev)
    def _():
        g, sc = gather_dmas(nxt, other), scatter_dmas(prev, other)
        run_interleaved(compute_chunks(slot), [d for pair in zip(sc, g) for d in pair])

    @pl.when(active & jnp.logical_not(full_prev))
    def _():
        run_interleaved(compute_chunks(slot), gather_dmas(nxt, other))

    @pl.when(jnp.logical_not(active) & prev_ok)
    def _():
        scatter_rolled(prev, other, nprev)

    @pl.when(prev_ok)
    def _():
        wait_scatter(other, nprev)

    @pl.when(active & (i == last))
    def _():
        wait_gather(other)
        scatter_rolled(i, slot, tv_ref[i])
        wait_scatter(slot, tv_ref[i])


def _experts_call(te, tv, nact, posa, posb, h, mod, wg, wu, wd):
    idx = lambda i, te_ref, *_: (te_ref[i], 0, 0)
    grid_spec = pltpu.PrefetchScalarGridSpec(
        num_scalar_prefetch=5,
        grid=(EXP_TILES,),
        in_specs=[
            pl.BlockSpec(memory_space=pl.ANY),
            pl.BlockSpec(mod.shape, lambda i, *_: (0, 0)),
            pl.BlockSpec((1, D_MODEL, D_EXPERT), idx),
            pl.BlockSpec((1, D_MODEL, D_EXPERT), idx),
            pl.BlockSpec((1, D_EXPERT, D_MODEL), idx),
        ],
        out_specs=pl.BlockSpec(memory_space=pl.ANY),
        scratch_shapes=[
            pltpu.SMEM((EXP_TILES * EXP_TM,), jnp.int32),
            pltpu.VMEM((2, TILE_ROWS, LANES), jnp.float32),
            pltpu.VMEM((2, TILE_ROWS, LANES), jnp.float32),
            pltpu.VMEM((D_MODEL, D_EXPERT), jnp.bfloat16),
            pltpu.VMEM((D_MODEL, D_EXPERT), jnp.bfloat16),
            pltpu.VMEM((D_EXPERT, D_MODEL), jnp.bfloat16),
            pltpu.SemaphoreType.DMA((2,)),
            pltpu.SemaphoreType.DMA((2,)),
        ],
    )
    return pl.pallas_call(
        _experts_kernel,
        grid_spec=grid_spec,
        out_shape=jax.ShapeDtypeStruct((2 * SEQ * ROW_TILE, LANES), jnp.float32),
        compiler_params=pltpu.CompilerParams(
            dimension_semantics=("arbitrary",), vmem_limit_bytes=VMEM_LIMIT),
        name="experts",
    )(te, tv, nact, posa, posb, h, mod, wg, wu, wd)


def _combine_kernel(h_ref, y0_ref, y1_ref, cw_ref, mod_ref, g_ref, b_ref, o_ref):
    tm = o_ref.shape[0]
    c0 = cw_ref[:, CW_C0:CW_C0 + 1]
    c1 = cw_ref[:, CW_C1:CW_C1 + 1]
    ffn = c0 * _load_row_tiles(y0_ref, tm) + c1 * _load_row_tiles(y1_ref, tm)
    g2 = mod_ref[0:1, 5 * D_MODEL:6 * D_MODEL]
    o_ref[...] = _layer_norm(DEEPNORM_ALPHA * _load_row_tiles(h_ref, tm) + g2 * ffn, g_ref[...], b_ref[...])


def _combine_call(h, y, cw, mod, g, b):
    tm = CMB_TM
    nblk = SEQ // tm
    tiles = lambda off: pl.BlockSpec((tm * ROW_TILE, LANES), lambda i: (i + off, 0))
    return pl.pallas_call(
        _combine_kernel,
        grid=(nblk,),
        in_specs=[
            tiles(0), tiles(0), tiles(nblk),
            pl.BlockSpec((tm, LANES), lambda i: (i, 0)),
            pl.BlockSpec(mod.shape, lambda i: (0, 0)),
            pl.BlockSpec((1, D_MODEL), lambda i: (0, 0)),
            pl.BlockSpec((1, D_MODEL), lambda i: (0, 0)),
        ],
        out_specs=pl.BlockSpec((tm, D_MODEL), lambda i: (i, 0)),
        out_shape=jax.ShapeDtypeStruct((SEQ, D_MODEL), jnp.float32),
        compiler_params=pltpu.CompilerParams(
            dimension_semantics=("arbitrary",), vmem_limit_bytes=VMEM_LIMIT),
        name="combine",
    )(h, y, y, cw, mod, g, b)


def kernel(x, c, ctx, c_ctx, ln_in_g, ln_in_b, w_mod, b_mod, w_in, rpb, w_pool_grp, pool_scale,
           w_attn_proj, w_pool_proj, w_out, ln1_g, ln1_b, w_router_group, b_router_group,
           w_router_expert, b_router_expert, w_expert_gate, w_expert_up, w_expert_down, ln2_g, ln2_b):
    assert x.shape == (1, SEQ, D_MODEL) and ctx.shape == (1, CTX_LEN, D_MODEL)
    assert w_mod.shape[0] == 1, "single-layer trunk"
    f32, bf16 = jnp.float32, jnp.bfloat16
    row = lambda v: v.reshape(1, -1).astype(f32)

    cond = jnp.concatenate([c, c_ctx[None], jnp.zeros((MOD_ROWS - 2, D_MODEL), f32)], axis=0)
    mod = _mod_call(cond, w_mod[0], row(b_mod[0]))

    lng, lnb = row(ln_in_g), row(ln_in_b)
    w_in_b = w_in[0].astype(bf16)
    xs = x[0]
    u = _proj_call(xs, mod, lng, lnb, w_in_b, mod_row=0, q_cols=ATT_W, tm=PROJ_TM)
    kvc = _proj_call(ctx[0], mod, lng, lnb, w_in_b[:, ATT_W:3 * ATT_W], mod_row=1, q_cols=0, tm=CTX_LEN)

    h1 = _mix_call(xs, mod, lng, lnb, u, kvc, _attn_bias_table(rpb[0]),
                   w_pool_grp[0].astype(bf16), row(pool_scale[0]),
                   w_attn_proj[0].astype(bf16), w_pool_proj[0].astype(bf16), w_out[0].astype(bf16),
                   row(ln1_g[0]), row(ln1_b[0]))

    n_logit = N_GROUPS + N_EXPERTS
    wrt = jnp.concatenate([w_router_group[0].T, w_router_expert[0].T,
                           jnp.zeros((LANES - n_logit, D_MODEL), f32)], axis=0)
    brt = jnp.concatenate([b_router_group[0], b_router_expert[0], jnp.zeros((LANES - n_logit,), f32)])
    brt = jnp.broadcast_to(brt[:, None], (LANES, LANES))
    pos, cw, plan = _route_call(h1, mod, wrt, brt)

    y = _experts_call(plan[PLAN_EXPERT, :EXP_TILES], plan[PLAN_VALID, :EXP_TILES], plan[PLAN_NACT, :1],
                      pos[POS_A], pos[POS_B], h1, mod,
                      w_expert_gate[0], w_expert_up[0], w_expert_down[0])
    out = _combine_call(h1, y, cw, mod, row(ln2_g[0]), row(ln2_b[0]))
    return out[None]
```

```python
import functools

import jax
import jax.numpy as jnp
from jax import lax
from jax.experimental import pallas as pl
from jax.experimental.pallas import tpu as pltpu

D_MODEL = 1024
SEQ = 16384
GRID_W = 64
ROWS = SEQ // GRID_W
CTX_LEN = 256
N_HEADS = 8
HEAD_DIM = 64
ATT_W = N_HEADS * HEAD_DIM
WIN_H = 8
WIN_W = 16
POOL_WINDOWS = (2, 4, 8, 16)
POOL_GROUPS = 4
POOL_DIM = 128
POOL_W = POOL_GROUPS * POOL_DIM
PROJ_W = 3 * ATT_W + POOL_W + 2 * D_MODEL
N_GROUPS = 4
EXPERTS_PER_GROUP = 8
N_EXPERTS = N_GROUPS * EXPERTS_PER_GROUP
D_EXPERT = 512
N_MOD = 6
DEEPNORM_ALPHA = 2.0 ** 0.25
LN_EPS = 1e-5
NEG_INF = -1e30

LANES = 128
ROW_TILE = 8
MOD_ROWS = 8
PROJ_TM = 512
MIX_ROWS = 8
MIX_TQ = MIX_ROWS * GRID_W
KV_HALO = 4 * GRID_W
POOL_HALO = 16
ROUTE_TM = 512
EXP_TM = 256
EXP_TILES = 2 * SEQ // EXP_TM + N_EXPERTS
CMB_TM = 256
HALF = D_MODEL // 2
VMEM_LIMIT = 56 * 1024 * 1024


def _layer_norm(x, g, b):
    mu = jnp.mean(x, axis=-1, keepdims=True)
    xc = x - mu
    var = jnp.mean(xc * xc, axis=-1, keepdims=True)
    return xc * lax.rsqrt(var + LN_EPS) * g + b


def _bdot(a, b):
    return jnp.dot(a, b, preferred_element_type=jnp.float32)


def _split_bf16(a):
    hi = a.astype(jnp.bfloat16)
    lo = (a - hi.astype(jnp.float32)).astype(jnp.bfloat16)
    return hi, lo


def _dot3(a, b):
    a_hi, a_lo = _split_bf16(a)
    b_hi, b_lo = _split_bf16(b)
    return _bdot(a_hi, b_hi) + (_bdot(a_hi, b_lo) + _bdot(a_lo, b_hi))


def _load_row_tiles(ref, tokens, lead=()):
    parts = [ref[(*lead, pl.ds(j, tokens, stride=ROW_TILE), slice(None))] for j in range(ROW_TILE)]
    return jnp.concatenate(parts, axis=-1)


def _store_row_tiles(ref, value, lead=()):
    tokens = value.shape[0]
    for j in range(ROW_TILE):
        ref[(*lead, pl.ds(j, tokens, stride=ROW_TILE), slice(None))] = value[:, j * LANES:(j + 1) * LANES]


def _mod_kernel(cond_ref, w_ref, b_ref, o_ref):
    cond = cond_ref[...]
    act = cond * jax.nn.sigmoid(cond)
    o_ref[...] = _dot3(act, w_ref[...]) + b_ref[...]


def _mod_call(cond, w_mod, b_mod):
    tn = 1536
    n = N_MOD * D_MODEL
    return pl.pallas_call(
        _mod_kernel,
        grid=(n // tn,),
        in_specs=[
            pl.BlockSpec((MOD_ROWS, D_MODEL), lambda i: (0, 0)),
            pl.BlockSpec((D_MODEL, tn), lambda i: (0, i)),
            pl.BlockSpec((1, tn), lambda i: (0, i)),
        ],
        out_specs=pl.BlockSpec((MOD_ROWS, tn), lambda i: (0, i)),
        out_shape=jax.ShapeDtypeStruct((MOD_ROWS, n), jnp.float32),
        compiler_params=pltpu.CompilerParams(
            dimension_semantics=("arbitrary",), vmem_limit_bytes=VMEM_LIMIT),
        name="mod",
    )(cond, w_mod, b_mod)


def _proj_kernel(x_ref, mod_ref, g_ref, b_ref, w_ref, o_ref, *, mod_row, q_cols):
    h = _layer_norm(x_ref[...], g_ref[...], b_ref[...])
    shift = mod_ref[mod_row:mod_row + 1, 0:D_MODEL]
    scale = mod_ref[mod_row:mod_row + 1, D_MODEL:2 * D_MODEL]
    hm = (h * (1.0 + scale) + shift).astype(jnp.bfloat16)
    n = o_ref.shape[1]
    for c in range(n // D_MODEL):
        sl = slice(c * D_MODEL, (c + 1) * D_MODEL)
        res = _bdot(hm, w_ref[:, sl])
        if c == 0 and q_cols:
            lane = lax.broadcasted_iota(jnp.int32, (1, D_MODEL), 1)
            res = res * jnp.where(lane < q_cols, HEAD_DIM ** -0.5, 1.0)
        o_ref[:, sl] = res.astype(jnp.bfloat16)


def _proj_call(x, mod, g, b, w, *, mod_row, q_cols, tm):
    rows, n = x.shape[0], w.shape[1]
    return pl.pallas_call(
        functools.partial(_proj_kernel, mod_row=mod_row, q_cols=q_cols),
        grid=(rows // tm,),
        in_specs=[
            pl.BlockSpec((tm, D_MODEL), lambda i: (i, 0)),
            pl.BlockSpec(mod.shape, lambda i: (0, 0)),
            pl.BlockSpec((1, D_MODEL), lambda i: (0, 0)),
            pl.BlockSpec((1, D_MODEL), lambda i: (0, 0)),
            pl.BlockSpec((D_MODEL, n), lambda i: (0, 0), pipeline_mode=pl.Buffered(1)),
        ],
        out_specs=pl.BlockSpec((tm, n), lambda i: (i, 0)),
        out_shape=jax.ShapeDtypeStruct((rows, n), jnp.bfloat16),
        compiler_params=pltpu.CompilerParams(
            dimension_semantics=("arbitrary",), vmem_limit_bytes=VMEM_LIMIT),
        name="proj",
    )(x, mod, g, b, w)


def _attn_bias_table(rpb):
    col = jnp.arange(GRID_W, dtype=jnp.int32)
    col_start = jnp.clip(col - WIN_W // 2, 0, GRID_W - WIN_W)
    col_mask = (col[None, :] >= col_start[:, None]) & (col[None, :] < col_start[:, None] + WIN_W)
    col_off = jnp.clip(col[None, :] - col[:, None], 1 - WIN_W, WIN_W - 1) + (WIN_W - 1)
    onehot = (col_off[None] == jnp.arange(2 * WIN_W - 1, dtype=jnp.int32)[:, None, None]).astype(jnp.float32)
    tab = jnp.einsum("hrc,cqk->hrqk", rpb.astype(jnp.float32), onehot, precision=lax.Precision.HIGHEST)
    tab = jnp.where(col_mask[None, None], tab, NEG_INF)
    tab = jnp.stack([tab[:, WIN_H - 1 - v:2 * WIN_H - 1 - v] for v in range(WIN_H)], axis=0)
    tab = tab.transpose(0, 1, 3, 2, 4)
    return tab.reshape(WIN_H, N_HEADS // 2, 2 * GRID_W, WIN_H * GRID_W)


def _mix_kernel(x_ref, mod_ref, lng_ref, lnb_ref,
                q_ref, kp_ref, kc_ref, kn_ref, vp_ref, vc_ref, vn_ref,
                pp_ref, pc_ref, pn_ref, ga_ref, gb_ref,
                kvc_ref, bias_ref, wgrp_ref, pscale_ref, wap_ref, wpp_ref, wout_ref,
                ln1g_ref, ln1b_ref,
                o_ref,
                kbuf, vbuf, yabuf, pbuf, ypbuf):
    b = pl.program_id(0)
    nb = pl.num_programs(0)

    kbuf[0:KV_HALO, :] = kp_ref[...]
    kbuf[KV_HALO:KV_HALO + MIX_TQ, :] = kc_ref[...]
    kbuf[KV_HALO + MIX_TQ:, :] = kn_ref[...]
    vbuf[0:KV_HALO, :] = vp_ref[...]
    vbuf[KV_HALO:KV_HALO + MIX_TQ, :] = vc_ref[...]
    vbuf[KV_HALO + MIX_TQ:, :] = vn_ref[...]

    lane = lax.broadcasted_iota(jnp.int32, (GRID_W, LANES), 1)
    first_head = lane < HEAD_DIM

    def row_body(j):
        r = b * MIX_ROWS + j
        rs = jnp.clip(r - WIN_H // 2, 0, ROWS - WIN_H)
        off = pl.multiple_of((rs - b * MIX_ROWS + WIN_H // 2) * GRID_W, GRID_W)
        var = r - rs
        qoff = j * GRID_W
        for pair in range(N_HEADS // 2):
            cols = slice(pair * LANES, (pair + 1) * LANES)
            q = q_ref[pl.ds(qoff, GRID_W), cols]
            zero = jnp.zeros_like(q)
            q2 = jnp.concatenate([jnp.where(first_head, q, zero), jnp.where(first_head, zero, q)], axis=0)
            kw = kbuf[pl.ds(off, WIN_H * GRID_W), cols]
            vw = vbuf[pl.ds(off, WIN_H * GRID_W), cols]
            kctx = kvc_ref[:, cols]
            vctx = kvc_ref[:, ATT_W + pair * LANES:ATT_W + (pair + 1) * LANES]
            nt = (((1,), (1,)), ((), ()))
            s_loc = lax.dot_general(q2, kw, nt, preferred_element_type=jnp.float32) + bias_ref[var, pair]
            s_ctx = lax.dot_general(q2, kctx, nt, preferred_element_type=jnp.float32)
            m = jnp.maximum(jnp.max(s_loc, axis=-1, keepdims=True), jnp.max(s_ctx, axis=-1, keepdims=True))
            p_loc = jnp.exp(s_loc - m)
            p_ctx = jnp.exp(s_ctx - m)
            denom = jnp.sum(p_loc, axis=-1, keepdims=True) + jnp.sum(p_ctx, axis=-1, keepdims=True)
            o2 = _bdot(p_loc.astype(jnp.bfloat16), vw) + _bdot(p_ctx.astype(jnp.bfloat16), vctx)
            o2 = o2 * (1.0 / denom)
            o_pair = jnp.where(first_head, o2[:GRID_W], o2[GRID_W:])
            yabuf[pl.ds(qoff, GRID_W), cols] = o_pair.astype(jnp.bfloat16)

    for j in range(MIX_ROWS):
        row_body(j)

    pbuf[0:POOL_HALO, :] = jnp.where(b > 0, pp_ref[...].astype(jnp.float32), 0.0)
    pbuf[POOL_HALO:POOL_HALO + MIX_TQ, :] = pc_ref[...].astype(jnp.float32)
    pbuf[POOL_HALO + MIX_TQ:, :] = jnp.where(b < nb - 1, pn_ref[...].astype(jnp.float32), 0.0)
    t_abs = b * MIX_TQ + lax.broadcasted_iota(jnp.int32, (MIX_TQ, 1), 0)
    for g, win in enumerate(POOL_WINDOWS):
        cols = slice(g * POOL_DIM, (g + 1) * POOL_DIM)
        acc = None
        for d in range(-(win // 2), win - win // 2):
            term = pbuf[POOL_HALO + d:POOL_HALO + d + MIX_TQ, cols]
            acc = term if acc is None else acc + term
        count = jnp.minimum(t_abs + (win - win // 2), SEQ) - jnp.maximum(t_abs - win // 2, 0)
        pooled = acc / count.astype(jnp.float32) - pbuf[POOL_HALO:POOL_HALO + MIX_TQ, cols]
        yp = _bdot(pooled.astype(jnp.bfloat16), wgrp_ref[g]) * pscale_ref[:, cols]
        ypbuf[:, cols] = yp.astype(jnp.bfloat16)

    ya = _bdot(yabuf[...], wap_ref[...])
    yp = _bdot(ypbuf[...], wpp_ref[...])
    z = jax.nn.sigmoid(ga_ref[...].astype(jnp.float32)) * ya + jax.nn.sigmoid(gb_ref[...].astype(jnp.float32)) * yp
    y = _bdot(z.astype(jnp.bfloat16), wout_ref[...])
    h = _layer_norm(x_ref[...], lng_ref[...], lnb_ref[...])
    g1 = mod_ref[0:1, 2 * D_MODEL:3 * D_MODEL]
    _store_row_tiles(o_ref, _layer_norm(DEEPNORM_ALPHA * h + g1 * y, ln1g_ref[...], ln1b_ref[...]))


def _mix_call(x, mod, lng, lnb, u, kvc, bias, wgrp, pscale, wap, wpp, wout, ln1g, ln1b):
    nb = SEQ // MIX_TQ
    halo_per_blk = MIX_TQ // KV_HALO
    n_halo = SEQ // KV_HALO
    ph_per_blk = MIX_TQ // POOL_HALO
    n_ph = SEQ // POOL_HALO

    def const(shape):
        return pl.BlockSpec(shape, lambda i: (0,) * len(shape), pipeline_mode=pl.Buffered(1))

    def prev_halo(c):
        return pl.BlockSpec((KV_HALO, ATT_W), lambda i: (jnp.maximum(i * halo_per_blk - 1, 0), c))

    def next_halo(c):
        return pl.BlockSpec((KV_HALO, ATT_W), lambda i: (jnp.minimum((i + 1) * halo_per_blk, n_halo - 1), c))

    def cur(c):
        return pl.BlockSpec((MIX_TQ, ATT_W), lambda i: (i, c))

    in_specs = [
        pl.BlockSpec((MIX_TQ, D_MODEL), lambda i: (i, 0)),
        const(mod.shape), const((1, D_MODEL)), const((1, D_MODEL)),
        cur(0),
        prev_halo(1), cur(1), next_halo(1),
        prev_halo(2), cur(2), next_halo(2),
        pl.BlockSpec((POOL_HALO, POOL_W), lambda i: (jnp.maximum(i * ph_per_blk - 1, 0), 3)),
        cur(3),
        pl.BlockSpec((POOL_HALO, POOL_W), lambda i: (jnp.minimum((i + 1) * ph_per_blk, n_ph - 1), 3)),
        pl.BlockSpec((MIX_TQ, D_MODEL), lambda i: (i, 2)),
        pl.BlockSpec((MIX_TQ, D_MODEL), lambda i: (i, 3)),
        const(kvc.shape), const(bias.shape), const(wgrp.shape), const(pscale.shape),
        const(wap.shape), const(wpp.shape), const(wout.shape),
        const((1, D_MODEL)), const((1, D_MODEL)),
    ]
    return pl.pallas_call(
        _mix_kernel,
        grid=(nb,),
        in_specs=in_specs,
        out_specs=pl.BlockSpec((MIX_TQ * ROW_TILE, LANES), lambda i: (i, 0)),
        out_shape=jax.ShapeDtypeStruct((SEQ * ROW_TILE, LANES), jnp.float32),
        scratch_shapes=[
            pltpu.VMEM((MIX_TQ + 2 * KV_HALO, ATT_W), jnp.bfloat16),
            pltpu.VMEM((MIX_TQ + 2 * KV_HALO, ATT_W), jnp.bfloat16),
            pltpu.VMEM((MIX_TQ, ATT_W), jnp.bfloat16),
            pltpu.VMEM((MIX_TQ + 2 * POOL_HALO, POOL_W), jnp.float32),
            pltpu.VMEM((MIX_TQ, POOL_W), jnp.bfloat16),
        ],
        compiler_params=pltpu.CompilerParams(
            dimension_semantics=("arbitrary",), vmem_limit_bytes=VMEM_LIMIT),
        name="mix",
    )(x, mod, lng, lnb, u, u, u, u, u, u, u, u, u, u, u, u,
      kvc, bias, wgrp, pscale, wap, wpp, wout, ln1g, ln1b)


ID_E0, ID_E1, ID_R0, ID_R1 = 0, 1, 4, 5
POS_A, POS_B = 0, 1
PLAN_EXPERT, PLAN_VALID, PLAN_NACT = 0, 1, 2
PLAN_W = 2 * LANES
CW_C0, CW_C1 = 0, 1


def _route_kernel(h_ref, mod_ref, wrt_ref, brt_ref, pos_ref, cw_ref, plan_ref, carry_ref, ids_all):
    i = pl.program_id(0)
    tm = ROUTE_TM

    @pl.when(i == 0)
    def _():
        carry_ref[...] = jnp.zeros_like(carry_ref)

    shift = mod_ref[0:1, 3 * D_MODEL:4 * D_MODEL]
    scale = mod_ref[0:1, 4 * D_MODEL:5 * D_MODEL]
    hm = _load_row_tiles(h_ref, tm) * (1.0 + scale) + shift

    hm_hi, hm_lo = _split_bf16(hm)
    w_hi, w_lo = _split_bf16(wrt_ref[...])
    nt = (((1,), (1,)), ((), ()))
    dg = functools.partial(lax.dot_general, dimension_numbers=nt, preferred_element_type=jnp.float32)
    logits = dg(w_hi, hm_hi) + (dg(w_hi, hm_lo) + dg(w_lo, hm_hi)) + brt_ref[:, 0:1]

    sub = lax.broadcasted_iota(jnp.int32, (LANES, tm), 0)
    big = jnp.int32(1 << 20)
    is_grp = sub < N_GROUPS
    gl = jnp.where(is_grp, logits, -jnp.inf)
    gmax = jnp.max(gl, axis=0, keepdims=True)
    gidx = jnp.min(jnp.where(gl == gmax, sub, big), axis=0, keepdims=True)
    gsum = jnp.sum(jnp.where(is_grp, jnp.exp(logits - gmax), 0.0), axis=0, keepdims=True)
    p_group = 1.0 / gsum

    eid = sub - N_GROUPS
    sel = (eid >= 0) & (eid < N_EXPERTS) & (lax.shift_right_arithmetic(eid, 3) == gidx)
    el = jnp.where(sel, logits, -jnp.inf)
    l0 = jnp.max(el, axis=0, keepdims=True)
    i0 = jnp.min(jnp.where(el == l0, sub, big), axis=0, keepdims=True)
    el2 = jnp.where(sub == i0, -jnp.inf, el)
    l1 = jnp.max(el2, axis=0, keepdims=True)
    i1 = jnp.min(jnp.where(el2 == l1, sub, big), axis=0, keepdims=True)
    t = jnp.exp(l1 - l0)
    w0 = 1.0 / (1.0 + t)
    w1 = t / (1.0 + t)

    onehot = jnp.where((sub == i0) | (sub == i1), 1.0, 0.0)
    rr = lax.broadcasted_iota(jnp.int32, (tm, tm), 0)
    cc = lax.broadcasted_iota(jnp.int32, (tm, tm), 1)
    earlier = jnp.where(rr < cc, 1.0, 0.0).astype(jnp.bfloat16)
    carry = carry_ref[:, 0:1]
    prefix = _bdot(onehot.astype(jnp.bfloat16), earlier) + carry
    r0 = jnp.sum(jnp.where(sub == i0, prefix, 0.0), axis=0, keepdims=True)
    r1 = jnp.sum(jnp.where(sub == i1, prefix, 0.0), axis=0, keepdims=True)
    total = jnp.broadcast_to(carry + jnp.sum(onehot, axis=1, keepdims=True), carry_ref.shape)
    carry_ref[...] = total

    sub8 = lax.broadcasted_iota(jnp.int32, (ROW_TILE, tm), 0)
    ids = jnp.zeros((ROW_TILE, tm), jnp.int32)
    for idx, val in ((ID_E0, i0 - N_GROUPS), (ID_E1, i1 - N_GROUPS),
                     (ID_R0, r0.astype(jnp.int32)), (ID_R1, r1.astype(jnp.int32))):
        ids = jnp.where(sub8 == idx, val, ids)
    ids_all[:, pl.ds(pl.multiple_of(i * tm, tm), tm)] = ids

    cwt = jnp.where(sub == CW_C0, p_group * w0, jnp.where(sub == CW_C1, p_group * w1, 0.0))
    cw_ref[...] = cwt.T

    @pl.when(i == pl.num_programs(0) - 1)
    def _():
        subq = lax.broadcasted_iota(jnp.int32, (LANES, LANES), 0)
        laneq = lax.broadcasted_iota(jnp.int32, (LANES, LANES), 1)
        cnt = total.astype(jnp.int32)
        tiles = lax.shift_right_logical(cnt + (EXP_TM - 1), EXP_TM.bit_length() - 1).astype(jnp.float32)
        incl = jnp.where(laneq <= subq, 1.0, 0.0).astype(jnp.bfloat16)
        tile_end = _bdot(incl, tiles.astype(jnp.bfloat16))
        tile_start = tile_end - tiles
        seg = (tile_start * EXP_TM).astype(jnp.int32)
        nact = jnp.max(tile_end, axis=0, keepdims=True)

        ids_full = ids_all[...]
        look = jnp.zeros_like(ids_full)
        for e in range(N_EXPERTS):
            look = jnp.where(ids_full == e, seg[N_GROUPS + e, 0], look)
        pos_ref[...] = look + pltpu.roll(ids_full, ID_R0 - ID_E0, axis=0)

        subp = lax.broadcasted_iota(jnp.int32, (LANES, PLAN_W), 0)
        tile = lax.broadcasted_iota(jnp.int32, (LANES, PLAN_W), 1).astype(jnp.float32)
        is_exp = (subp >= N_GROUPS) & (subp < N_GROUPS + N_EXPERTS)
        end_col = tile_end[:, 0:1]
        nact_s = nact[:, 0:1]
        te = jnp.sum(jnp.where(is_exp & (tile >= end_col), 1.0, 0.0), axis=0, keepdims=True)
        te_last = jnp.sum(jnp.where(is_exp & (nact_s - 1.0 >= end_col), 1.0, 0.0), axis=0, keepdims=True)[:, 0:1]
        tile_row = tile[0:1, :]
        te = jnp.minimum(jnp.where(tile_row < nact_s, te, te_last), N_EXPERTS - 1.0)
        mine = (subp - N_GROUPS).astype(jnp.float32) == te
        cnt_sel = jnp.sum(jnp.where(mine, total[:, 0:1], 0.0), axis=0, keepdims=True)
        start_sel = jnp.sum(jnp.where(mine, tile_start[:, 0:1], 0.0), axis=0, keepdims=True)
        valid = jnp.clip(cnt_sel - (tile_row - start_sel) * EXP_TM, 0.0, float(EXP_TM))
        valid = jnp.where(tile_row < nact_s, valid, 0.0)
        subr = lax.broadcasted_iota(jnp.int32, (ROW_TILE, PLAN_W), 0)
        plan = jnp.where(subr == PLAN_EXPERT, te, jnp.where(subr == PLAN_VALID, valid,
                         jnp.where(subr == PLAN_NACT, nact_s, 0.0)))
        plan_ref[...] = plan.astype(jnp.int32)


def _route_call(h, mod, wrt, brt):
    tm = ROUTE_TM
    return pl.pallas_call(
        _route_kernel,
        grid=(SEQ // tm,),
        in_specs=[
            pl.BlockSpec((tm * ROW_TILE, LANES), lambda i: (i, 0)),
            pl.BlockSpec(mod.shape, lambda i: (0, 0)),
            pl.BlockSpec((LANES, D_MODEL), lambda i: (0, 0)),
            pl.BlockSpec((LANES, LANES), lambda i: (0, 0)),
        ],
        out_specs=[
            pl.BlockSpec((ROW_TILE, SEQ), lambda i: (0, 0)),
            pl.BlockSpec((tm, LANES), lambda i: (i, 0)),
            pl.BlockSpec((ROW_TILE, PLAN_W), lambda i: (0, 0)),
        ],
        out_shape=[
            jax.ShapeDtypeStruct((ROW_TILE, SEQ), jnp.int32),
            jax.ShapeDtypeStruct((SEQ, LANES), jnp.float32),
            jax.ShapeDtypeStruct((ROW_TILE, PLAN_W), jnp.int32),
        ],
        scratch_shapes=[pltpu.VMEM((LANES, LANES), jnp.float32),
                        pltpu.VMEM((ROW_TILE, SEQ), jnp.int32)],
        compiler_params=pltpu.CompilerParams(
            dimension_semantics=("arbitrary",), vmem_limit_bytes=VMEM_LIMIT),
        name="route",
    )(h, mod, wrt, brt)


SRC_UNROLL = 16
EXP_BUFS = 3
EXP_CHUNK = 256
TILE_ROWS = EXP_TM * ROW_TILE


def _experts_kernel(te_ref, tv_ref, nact_ref, posa_ref, posb_ref,
                    h_hbm, mod_ref, wg_ref, wu_ref, wd_ref,
                    y_hbm,
                    src_ref, xbuf, ybuf, wgb, wub, wdb, gsem, ssem):
    i = pl.program_id(0)
    last = pl.num_programs(0) - 1
    nact = nact_ref[0]
    cur = lax.rem(i, EXP_BUFS)
    nxt2 = lax.rem(i + 2, EXP_BUFS)
    prv2 = lax.rem(i + 1, EXP_BUFS)

    def gather_copy(p, k, s):
        tok = src_ref[p] & (SEQ - 1)
        return pltpu.make_async_copy(h_hbm.at[pl.ds(pl.multiple_of(tok * ROW_TILE, ROW_TILE), ROW_TILE), :],
                                     xbuf.at[s, pl.ds(pl.multiple_of(k * ROW_TILE, ROW_TILE), ROW_TILE), :],
                                     gsem.at[s])

    def scatter_copy(p, k, s):
        dst = src_ref[p]
        return pltpu.make_async_copy(ybuf.at[s, pl.ds(pl.multiple_of(k * ROW_TILE, ROW_TILE), ROW_TILE), :],
                                     y_hbm.at[pl.ds(pl.multiple_of(dst * ROW_TILE, ROW_TILE), ROW_TILE), :],
                                     ssem.at[s])

    def gather_rolled(tile, s):
        def body(kk, c):
            for u in range(SRC_UNROLL):
                k = kk * SRC_UNROLL + u
                gather_copy(tile * EXP_TM + k, k, s).start()
            return c
        lax.fori_loop(0, EXP_TM // SRC_UNROLL, body, 0)

    def scatter_rolled(tile, s, n):
        def body(k, c):
            scatter_copy(tile * EXP_TM + k, k, s).start()
            return c
        lax.fori_loop(0, n, body, 0)

    def wait_gather(s):
        pltpu.make_async_copy(h_hbm.at[pl.ds(0, TILE_ROWS), :], xbuf.at[s], gsem.at[s]).wait()

    def wait_scatter(s, n):
        rows = pl.multiple_of(n * ROW_TILE, ROW_TILE)
        pltpu.make_async_copy(ybuf.at[s, pl.ds(0, rows), :], y_hbm.at[pl.ds(0, rows), :], ssem.at[s]).wait()

    def compute_chunks(s):
        state = {}

        def load():
            x = _load_row_tiles(xbuf, EXP_TM, lead=(s,))
            shift = mod_ref[0:1, 3 * D_MODEL:4 * D_MODEL]
            scale = mod_ref[0:1, 4 * D_MODEL:5 * D_MODEL]
            state["x"] = (x * (1.0 + scale) + shift).astype(jnp.bfloat16)
            state["act"] = []

        def gate(c):
            def run():
                if c == 0:
                    load()
                state["a"] = _bdot(state["x"], wgb[:, c * EXP_CHUNK:(c + 1) * EXP_CHUNK])
            return run

        def up(c):
            def run():
                a = state["a"]
                u = _bdot(state["x"], wub[:, c * EXP_CHUNK:(c + 1) * EXP_CHUNK])
                state["act"].append((a * jax.nn.sigmoid(a) * u).astype(jnp.bfloat16))
            return run

        def down(c):
            def run():
                if c == 0:
                    state["actf"] = jnp.concatenate(state["act"], axis=-1)
                yc = _bdot(state["actf"], wdb[:, c * EXP_CHUNK:(c + 1) * EXP_CHUNK])
                for jj in range(EXP_CHUNK // LANES):
                    j = c * (EXP_CHUNK // LANES) + jj
                    ybuf[s, pl.ds(j, EXP_TM, stride=ROW_TILE), :] = yc[:, jj * LANES:(jj + 1) * LANES]
            return run

        steps = []
        for c in range(D_EXPERT // EXP_CHUNK):
            steps += [(gate(c), 2), (up(c), 2)]
        return steps + [(down(c), 1) for c in range(D_MODEL // EXP_CHUNK)]

    def run_interleaved(chunks, dmas):
        total_cost = sum(cost for _, cost in chunks)
        done = 0
        for chunk, cost in chunks:
            upto = -(-len(dmas) * (done + cost) // total_cost)
            for d in dmas[-(-len(dmas) * done // total_cost):upto]:
                d()
            done += cost
            chunk()

    def gather_dmas(tile, s):
        return [functools.partial(lambda k: gather_copy(tile * EXP_TM + k, k, s).start(priority=0), k)
                for k in range(EXP_TM)]

    def scatter_dmas(tile, s):
        return [functools.partial(lambda k: scatter_copy(tile * EXP_TM + k, k, s).start(priority=1), k)
                for k in range(EXP_TM)]

    tile_to_gather = lambda t: jnp.minimum(t, nact - 1)

    @pl.when(i == 0)
    def _():
        def fill_body(tt, c):
            ts = [tt * SRC_UNROLL + u for u in range(SRC_UNROLL)]
            pa = [posa_ref[t] for t in ts]
            pb = [posb_ref[t] for t in ts]
            for t, a, b in zip(ts, pa, pb):
                src_ref[a] = t
                src_ref[b] = t + SEQ
            return c
        lax.fori_loop(0, SEQ // SRC_UNROLL, fill_body, 0)

        def pad_tile(t, c):
            def pad_row(k, c2):
                src_ref[t * EXP_TM + k] = 0
                return c2
            return lax.fori_loop(tv_ref[t], EXP_TM, pad_row, c)
        lax.fori_loop(0, nact, pad_tile, 0)
        gather_rolled(0, 0)
        gather_rolled(tile_to_gather(1), 1)

    active = i < nact
    prev = jnp.maximum(i - 1, 0)
    prev_ok = (i >= 1) & (i - 1 < nact)
    prev2 = jnp.maximum(i - 2, 0)
    prev2_ok = (i >= 2) & (i - 2 < nact)
    nprev = tv_ref[prev]
    full_prev = prev_ok & (nprev == EXP_TM)

    @pl.when((i < 2) | prev2_ok)
    def _():
        wait_gather(cur)

    @pl.when(prev2_ok)
    def _():
        wait_scatter(prv2, tv_ref[prev2])

    @pl.when(active & ((i == 0) | (te_ref[i] != te_ref[prev])))
    def _():
        wgb[...] = wg_ref[0].astype(jnp.bfloat16)
        wub[...] = wu_ref[0].astype(jnp.bfloat16)
        wdb[...] = wd_ref[0].astype(jnp.bfloat16)

    @pl.when(prev_ok & jnp.logical_not(active & full_prev))
    def _():
        scatter_rolled(prev, nxt2, nprev)

    @pl.when(active & full_prev)
    def _():
        g, sc = gather_dmas(tile_to_gather(i + 2), nxt2), scatter_dmas(prev, nxt2)
        run_interleaved(compute_chunks(cur), [d for pair in zip(sc, g) for d in pair])

    @pl.when(active & jnp.logical_not(full_prev))
    def _():
        run_interleaved(compute_chunks(cur), gather_dmas(tile_to_gather(i + 2), nxt2))

    @pl.when(i == last)
    def _():
        @pl.when(prev_ok)
        def _():
            wait_gather(prv2)
            wait_scatter(nxt2, nprev)

        @pl.when(active)
        def _():
            wait_gather(nxt2)
            scatter_rolled(i, cur, tv_ref[i])
            wait_scatter(cur, tv_ref[i])


def _experts_call(te, tv, nact, posa, posb, h, mod, wg, wu, wd):
    idx = lambda i, te_ref, *_: (te_ref[i], 0, 0)
    grid_spec = pltpu.PrefetchScalarGridSpec(
        num_scalar_prefetch=5,
        grid=(EXP_TILES,),
        in_specs=[
            pl.BlockSpec(memory_space=pl.ANY),
            pl.BlockSpec(mod.shape, lambda i, *_: (0, 0)),
            pl.BlockSpec((1, D_MODEL, D_EXPERT), idx),
            pl.BlockSpec((1, D_MODEL, D_EXPERT), idx),
            pl.BlockSpec((1, D_EXPERT, D_MODEL), idx),
        ],
        out_specs=pl.BlockSpec(memory_space=pl.ANY),
        scratch_shapes=[
            pltpu.SMEM((EXP_TILES * EXP_TM,), jnp.int32),
            pltpu.VMEM((EXP_BUFS, TILE_ROWS, LANES), jnp.float32),
            pltpu.VMEM((EXP_BUFS, TILE_ROWS, LANES), jnp.float32),
            pltpu.VMEM((D_MODEL, D_EXPERT), jnp.bfloat16),
            pltpu.VMEM((D_MODEL, D_EXPERT), jnp.bfloat16),
            pltpu.VMEM((D_EXPERT, D_MODEL), jnp.bfloat16),
            pltpu.SemaphoreType.DMA((EXP_BUFS,)),
            pltpu.SemaphoreType.DMA((EXP_BUFS,)),
        ],
    )
    return pl.pallas_call(
        _experts_kernel,
        grid_spec=grid_spec,
        out_shape=jax.ShapeDtypeStruct((2 * SEQ * ROW_TILE, LANES), jnp.float32),
        compiler_params=pltpu.CompilerParams(
            dimension_semantics=("arbitrary",), vmem_limit_bytes=VMEM_LIMIT),
        name="experts",
    )(te, tv, nact, posa, posb, h, mod, wg, wu, wd)


def _combine_kernel(h_ref, y0_ref, y1_ref, cw_ref, mod_ref, g_ref, b_ref, o_ref):
    tm = o_ref.shape[0]
    c0 = cw_ref[:, CW_C0:CW_C0 + 1]
    c1 = cw_ref[:, CW_C1:CW_C1 + 1]
    ffn = c0 * _load_row_tiles(y0_ref, tm) + c1 * _load_row_tiles(y1_ref, tm)
    g2 = mod_ref[0:1, 5 * D_MODEL:6 * D_MODEL]
    o_ref[...] = _layer_norm(DEEPNORM_ALPHA * _load_row_tiles(h_ref, tm) + g2 * ffn, g_ref[...], b_ref[...])


def _combine_call(h, y, cw, mod, g, b):
    tm = CMB_TM
    nblk = SEQ // tm
    tiles = lambda off: pl.BlockSpec((tm * ROW_TILE, LANES), lambda i: (i + off, 0))
    return pl.pallas_call(
        _combine_kernel,
        grid=(nblk,),
        in_specs=[
            tiles(0), tiles(0), tiles(nblk),
            pl.BlockSpec((tm, LANES), lambda i: (i, 0)),
            pl.BlockSpec(mod.shape, lambda i: (0, 0)),
            pl.BlockSpec((1, D_MODEL), lambda i: (0, 0)),
            pl.BlockSpec((1, D_MODEL), lambda i: (0, 0)),
        ],
        out_specs=pl.BlockSpec((tm, D_MODEL), lambda i: (i, 0)),
        out_shape=jax.ShapeDtypeStruct((SEQ, D_MODEL), jnp.float32),
        compiler_params=pltpu.CompilerParams(
            dimension_semantics=("arbitrary",), vmem_limit_bytes=VMEM_LIMIT),
        name="combine",
    )(h, y, y, cw, mod, g, b)


def kernel(x, c, ctx, c_ctx, ln_in_g, ln_in_b, w_mod, b_mod, w_in, rpb, w_pool_grp, pool_scale,
           w_attn_proj, w_pool_proj, w_out, ln1_g, ln1_b, w_router_group, b_router_group,
           w_router_expert, b_router_expert, w_expert_gate, w_expert_up, w_expert_down, ln2_g, ln2_b):
    assert x.shape == (1, SEQ, D_MODEL) and ctx.shape == (1, CTX_LEN, D_MODEL)
    assert w_mod.shape[0] == 1, "single-layer trunk"
    f32, bf16 = jnp.float32, jnp.bfloat16
    row = lambda v: v.reshape(1, -1).astype(f32)

    cond = jnp.concatenate([c, c_ctx[None], jnp.zeros((MOD_ROWS - 2, D_MODEL), f32)], axis=0)
    mod = _mod_call(cond, w_mod[0], row(b_mod[0]))

    lng, lnb = row(ln_in_g), row(ln_in_b)
    w_in_b = w_in[0].astype(bf16)
    xs = x[0]
    u = _proj_call(xs, mod, lng, lnb, w_in_b, mod_row=0, q_cols=ATT_W, tm=PROJ_TM)
    kvc = _proj_call(ctx[0], mod, lng, lnb, w_in_b[:, ATT_W:3 * ATT_W], mod_row=1, q_cols=0, tm=CTX_LEN)

    h1 = _mix_call(xs, mod, lng, lnb, u, kvc, _attn_bias_table(rpb[0]),
                   w_pool_grp[0].astype(bf16), row(pool_scale[0]),
                   w_attn_proj[0].astype(bf16), w_pool_proj[0].astype(bf16), w_out[0].astype(bf16),
                   row(ln1_g[0]), row(ln1_b[0]))

    n_logit = N_GROUPS + N_EXPERTS
    wrt = jnp.concatenate([w_router_group[0].T, w_router_expert[0].T,
                           jnp.zeros((LANES - n_logit, D_MODEL), f32)], axis=0)
    brt = jnp.concatenate([b_router_group[0], b_router_expert[0], jnp.zeros((LANES - n_logit,), f32)])
    brt = jnp.broadcast_to(brt[:, None], (LANES, LANES))
    pos, cw, plan = _route_call(h1, mod, wrt, brt)

    y = _experts_call(plan[PLAN_EXPERT, :EXP_TILES], plan[PLAN_VALID, :EXP_TILES], plan[PLAN_NACT, :1],
                      pos[POS_A], pos[POS_B], h1, mod,
                      w_expert_gate[0], w_expert_up[0], w_expert_down[0])
    out = _combine_call(h1, y, cw, mod, row(ln2_g[0]), row(ln2_b[0]))
    return out[None]
```

```python
import functools

import jax
import jax.numpy as jnp
from jax import lax
from jax.experimental import pallas as pl
from jax.experimental.pallas import tpu as pltpu

D_MODEL = 1024
SEQ = 16384
GRID_W = 64
ROWS = SEQ // GRID_W
CTX_LEN = 256
N_HEADS = 8
HEAD_DIM = 64
ATT_W = N_HEADS * HEAD_DIM
WIN_H = 8
WIN_W = 16
POOL_WINDOWS = (2, 4, 8, 16)
POOL_GROUPS = 4
POOL_DIM = 128
POOL_W = POOL_GROUPS * POOL_DIM
PROJ_W = 3 * ATT_W + POOL_W + 2 * D_MODEL
GATE_COL = 3 * ATT_W + POOL_W
N_GROUPS = 4
EXPERTS_PER_GROUP = 8
N_EXPERTS = N_GROUPS * EXPERTS_PER_GROUP
D_EXPERT = 512
N_MOD = 6
DEEPNORM_ALPHA = 2.0 ** 0.25
LN_EPS = 1e-5
NEG_INF = -1e30

LANES = 128
ROW_TILE = 8
MOD_ROWS = 8
PROJ_TM = 512
MIX_ROWS = 8
MIX_TQ = MIX_ROWS * GRID_W
KV_HALO = 4 * GRID_W
POOL_HALO = 16
ROUTE_TM = 512
EXP_TM = 256
EXP_TILES = 2 * SEQ // EXP_TM + N_EXPERTS
CMB_TM = 256
HALF = D_MODEL // 2
VMEM_LIMIT = 56 * 1024 * 1024


def _layer_norm(x, g, b):
    mu = jnp.mean(x, axis=-1, keepdims=True)
    xc = x - mu
    var = jnp.mean(xc * xc, axis=-1, keepdims=True)
    return xc * lax.rsqrt(var + LN_EPS) * g + b


def _bdot(a, b):
    return jnp.dot(a, b, preferred_element_type=jnp.float32)


def _split_bf16(a):
    hi = a.astype(jnp.bfloat16)
    lo = (a - hi.astype(jnp.float32)).astype(jnp.bfloat16)
    return hi, lo


def _dot3(a, b):
    a_hi, a_lo = _split_bf16(a)
    b_hi, b_lo = _split_bf16(b)
    return _bdot(a_hi, b_hi) + (_bdot(a_hi, b_lo) + _bdot(a_lo, b_hi))


def _load_row_tiles(ref, tokens, lead=()):
    parts = [ref[(*lead, pl.ds(j, tokens, stride=ROW_TILE), slice(None))] for j in range(ROW_TILE)]
    return jnp.concatenate(parts, axis=-1)


def _store_row_tiles(ref, value, lead=()):
    tokens = value.shape[0]
    for j in range(ROW_TILE):
        ref[(*lead, pl.ds(j, tokens, stride=ROW_TILE), slice(None))] = value[:, j * LANES:(j + 1) * LANES]


def _mod_kernel(cond_ref, w_ref, b_ref, o_ref):
    cond = cond_ref[...]
    act = cond * jax.nn.sigmoid(cond)
    o_ref[...] = _dot3(act, w_ref[...]) + b_ref[...]


def _mod_call(cond, w_mod, b_mod):
    tn = 1536
    n = N_MOD * D_MODEL
    return pl.pallas_call(
        _mod_kernel,
        grid=(n // tn,),
        in_specs=[
            pl.BlockSpec((MOD_ROWS, D_MODEL), lambda i: (0, 0)),
            pl.BlockSpec((D_MODEL, tn), lambda i: (0, i)),
            pl.BlockSpec((1, tn), lambda i: (0, i)),
        ],
        out_specs=pl.BlockSpec((MOD_ROWS, tn), lambda i: (0, i)),
        out_shape=jax.ShapeDtypeStruct((MOD_ROWS, n), jnp.float32),
        compiler_params=pltpu.CompilerParams(
            dimension_semantics=("arbitrary",), vmem_limit_bytes=VMEM_LIMIT),
        name="mod",
    )(cond, w_mod, b_mod)


def _proj_kernel(x_ref, mod_ref, g_ref, b_ref, w_ref, o_ref, *h_out, mod_row, latent):
    h = _layer_norm(x_ref[...], g_ref[...], b_ref[...])
    if latent:
        h_out[0][...] = h
    shift = mod_ref[mod_row:mod_row + 1, 0:D_MODEL]
    scale = mod_ref[mod_row:mod_row + 1, D_MODEL:2 * D_MODEL]
    hm = (h * (1.0 + scale) + shift).astype(jnp.bfloat16)
    n = o_ref.shape[1]
    for c in range(n // D_MODEL):
        sl = slice(c * D_MODEL, (c + 1) * D_MODEL)
        res = _bdot(hm, w_ref[:, sl])
        if latent and c == 0:
            lane = lax.broadcasted_iota(jnp.int32, (1, D_MODEL), 1)
            res = res * jnp.where(lane < ATT_W, HEAD_DIM ** -0.5, 1.0)
        if latent and c * D_MODEL >= GATE_COL:
            res = jax.nn.sigmoid(res)
        o_ref[:, sl] = res.astype(jnp.bfloat16)


def _proj_call(x, mod, g, b, w, *, mod_row, latent, tm):
    rows, n = x.shape[0], w.shape[1]
    out_specs = [pl.BlockSpec((tm, n), lambda i: (i, 0))]
    out_shape = [jax.ShapeDtypeStruct((rows, n), jnp.bfloat16)]
    if latent:
        out_specs.append(pl.BlockSpec((tm, D_MODEL), lambda i: (i, 0)))
        out_shape.append(jax.ShapeDtypeStruct((rows, D_MODEL), jnp.float32))
    return pl.pallas_call(
        functools.partial(_proj_kernel, mod_row=mod_row, latent=latent),
        grid=(rows // tm,),
        in_specs=[
            pl.BlockSpec((tm, D_MODEL), lambda i: (i, 0)),
            pl.BlockSpec(mod.shape, lambda i: (0, 0)),
            pl.BlockSpec((1, D_MODEL), lambda i: (0, 0)),
            pl.BlockSpec((1, D_MODEL), lambda i: (0, 0)),
            pl.BlockSpec((D_MODEL, n), lambda i: (0, 0), pipeline_mode=pl.Buffered(1)),
        ],
        out_specs=out_specs,
        out_shape=out_shape,
        compiler_params=pltpu.CompilerParams(
            dimension_semantics=("arbitrary",), vmem_limit_bytes=VMEM_LIMIT),
        name="proj",
    )(x, mod, g, b, w)


def _attn_bias_table(rpb):
    col = jnp.arange(GRID_W, dtype=jnp.int32)
    col_start = jnp.clip(col - WIN_W // 2, 0, GRID_W - WIN_W)
    col_mask = (col[None, :] >= col_start[:, None]) & (col[None, :] < col_start[:, None] + WIN_W)
    col_off = jnp.clip(col[None, :] - col[:, None], 1 - WIN_W, WIN_W - 1) + (WIN_W - 1)
    onehot = (col_off[None] == jnp.arange(2 * WIN_W - 1, dtype=jnp.int32)[:, None, None]).astype(jnp.float32)
    tab = jnp.einsum("hrc,cqk->hrqk", rpb.astype(jnp.float32), onehot, precision=lax.Precision.HIGHEST)
    tab = jnp.where(col_mask[None, None], tab, NEG_INF)
    tab = jnp.stack([tab[:, WIN_H - 1 - v:2 * WIN_H - 1 - v] for v in range(WIN_H)], axis=0)
    tab = tab.transpose(0, 1, 3, 2, 4)
    return tab.reshape(WIN_H, N_HEADS // 2, 2 * GRID_W, WIN_H * GRID_W)


def _mix_kernel(h_ref, mod_ref,
                q_ref, kp_ref, kc_ref, kn_ref, vp_ref, vc_ref, vn_ref,
                pp_ref, pc_ref, pn_ref, ga_ref, gb_ref,
                kvc_ref, bias_ref, wgrp_ref, pscale_ref, wap_ref, wpp_ref, wout_ref,
                ln1g_ref, ln1b_ref,
                o_ref,
                kbuf, vbuf, yabuf, pbuf, ypbuf):
    b = pl.program_id(0)
    nb = pl.num_programs(0)

    kbuf[0:KV_HALO, :] = kp_ref[...]
    kbuf[KV_HALO:KV_HALO + MIX_TQ, :] = kc_ref[...]
    kbuf[KV_HALO + MIX_TQ:, :] = kn_ref[...]
    vbuf[0:KV_HALO, :] = vp_ref[...]
    vbuf[KV_HALO:KV_HALO + MIX_TQ, :] = vc_ref[...]
    vbuf[KV_HALO + MIX_TQ:, :] = vn_ref[...]

    lane = lax.broadcasted_iota(jnp.int32, (GRID_W, LANES), 1)
    first_head = lane < HEAD_DIM

    def row_body(j):
        r = b * MIX_ROWS + j
        rs = jnp.clip(r - WIN_H // 2, 0, ROWS - WIN_H)
        off = pl.multiple_of((rs - b * MIX_ROWS + WIN_H // 2) * GRID_W, GRID_W)
        var = r - rs
        qoff = j * GRID_W
        for pair in range(N_HEADS // 2):
            cols = slice(pair * LANES, (pair + 1) * LANES)
            q = q_ref[pl.ds(qoff, GRID_W), cols]
            zero = jnp.zeros_like(q)
            q2 = jnp.concatenate([jnp.where(first_head, q, zero), jnp.where(first_head, zero, q)], axis=0)
            kw = kbuf[pl.ds(off, WIN_H * GRID_W), cols]
            vw = vbuf[pl.ds(off, WIN_H * GRID_W), cols]
            kctx = kvc_ref[:, cols]
            vctx = kvc_ref[:, ATT_W + pair * LANES:ATT_W + (pair + 1) * LANES]
            nt = (((1,), (1,)), ((), ()))
            s_loc = lax.dot_general(q2, kw, nt, preferred_element_type=jnp.float32) + bias_ref[var, pair]
            s_ctx = lax.dot_general(q2, kctx, nt, preferred_element_type=jnp.float32)
            m = jnp.maximum(jnp.max(s_loc, axis=-1, keepdims=True), jnp.max(s_ctx, axis=-1, keepdims=True))
            p_loc = jnp.exp(s_loc - m)
            p_ctx = jnp.exp(s_ctx - m)
            denom = jnp.sum(p_loc, axis=-1, keepdims=True) + jnp.sum(p_ctx, axis=-1, keepdims=True)
            o2 = _bdot(p_loc.astype(jnp.bfloat16), vw) + _bdot(p_ctx.astype(jnp.bfloat16), vctx)
            o2 = o2 * (1.0 / denom)
            o_pair = jnp.where(first_head, o2[:GRID_W], o2[GRID_W:])
            yabuf[pl.ds(qoff, GRID_W), cols] = o_pair.astype(jnp.bfloat16)

    for j in range(MIX_ROWS):
        row_body(j)

    pbuf[0:POOL_HALO, :] = jnp.where(b > 0, pp_ref[...].astype(jnp.float32), 0.0)
    pbuf[POOL_HALO:POOL_HALO + MIX_TQ, :] = pc_ref[...].astype(jnp.float32)
    pbuf[POOL_HALO + MIX_TQ:, :] = jnp.where(b < nb - 1, pn_ref[...].astype(jnp.float32), 0.0)
    t_abs = b * MIX_TQ + lax.broadcasted_iota(jnp.int32, (MIX_TQ, 1), 0)
    for g, win in enumerate(POOL_WINDOWS):
        cols = slice(g * POOL_DIM, (g + 1) * POOL_DIM)
        acc = None
        for d in range(-(win // 2), win - win // 2):
            term = pbuf[POOL_HALO + d:POOL_HALO + d + MIX_TQ, cols]
            acc = term if acc is None else acc + term
        count = jnp.minimum(t_abs + (win - win // 2), SEQ) - jnp.maximum(t_abs - win // 2, 0)
        pooled = acc / count.astype(jnp.float32) - pbuf[POOL_HALO:POOL_HALO + MIX_TQ, cols]
        yp = _bdot(pooled.astype(jnp.bfloat16), wgrp_ref[g]) * pscale_ref[:, cols]
        ypbuf[:, cols] = yp.astype(jnp.bfloat16)

    ya = _bdot(yabuf[...], wap_ref[...])
    yp = _bdot(ypbuf[...], wpp_ref[...])
    z = ga_ref[...].astype(jnp.float32) * ya + gb_ref[...].astype(jnp.float32) * yp
    y = _bdot(z.astype(jnp.bfloat16), wout_ref[...])
    g1 = mod_ref[0:1, 2 * D_MODEL:3 * D_MODEL]
    _store_row_tiles(o_ref, _layer_norm(DEEPNORM_ALPHA * h_ref[...] + g1 * y, ln1g_ref[...], ln1b_ref[...]))


def _mix_call(h, mod, u, kvc, bias, wgrp, pscale, wap, wpp, wout, ln1g, ln1b):
    nb = SEQ // MIX_TQ
    halo_per_blk = MIX_TQ // KV_HALO
    n_halo = SEQ // KV_HALO
    ph_per_blk = MIX_TQ // POOL_HALO
    n_ph = SEQ // POOL_HALO

    def const(shape):
        return pl.BlockSpec(shape, lambda i: (0,) * len(shape), pipeline_mode=pl.Buffered(1))

    def prev_halo(c):
        return pl.BlockSpec((KV_HALO, ATT_W), lambda i: (jnp.maximum(i * halo_per_blk - 1, 0), c))

    def next_halo(c):
        return pl.BlockSpec((KV_HALO, ATT_W), lambda i: (jnp.minimum((i + 1) * halo_per_blk, n_halo - 1), c))

    def cur(c):
        return pl.BlockSpec((MIX_TQ, ATT_W), lambda i: (i, c))

    in_specs = [
        pl.BlockSpec((MIX_TQ, D_MODEL), lambda i: (i, 0)),
        const(mod.shape),
        cur(0),
        prev_halo(1), cur(1), next_halo(1),
        prev_halo(2), cur(2), next_halo(2),
        pl.BlockSpec((POOL_HALO, POOL_W), lambda i: (jnp.maximum(i * ph_per_blk - 1, 0), 3)),
        cur(3),
        pl.BlockSpec((POOL_HALO, POOL_W), lambda i: (jnp.minimum((i + 1) * ph_per_blk, n_ph - 1), 3)),
        pl.BlockSpec((MIX_TQ, D_MODEL), lambda i: (i, 2)),
        pl.BlockSpec((MIX_TQ, D_MODEL), lambda i: (i, 3)),
        const(kvc.shape), const(bias.shape), const(wgrp.shape), const(pscale.shape),
        const(wap.shape), const(wpp.shape), const(wout.shape),
        const((1, D_MODEL)), const((1, D_MODEL)),
    ]
    return pl.pallas_call(
        _mix_kernel,
        grid=(nb,),
        in_specs=in_specs,
        out_specs=pl.BlockSpec((MIX_TQ * ROW_TILE, LANES), lambda i: (i, 0)),
        out_shape=jax.ShapeDtypeStruct((SEQ * ROW_TILE, LANES), jnp.float32),
        scratch_shapes=[
            pltpu.VMEM((MIX_TQ + 2 * KV_HALO, ATT_W), jnp.bfloat16),
            pltpu.VMEM((MIX_TQ + 2 * KV_HALO, ATT_W), jnp.bfloat16),
            pltpu.VMEM((MIX_TQ, ATT_W), jnp.bfloat16),
            pltpu.VMEM((MIX_TQ + 2 * POOL_HALO, POOL_W), jnp.float32),
            pltpu.VMEM((MIX_TQ, POOL_W), jnp.bfloat16),
        ],
        compiler_params=pltpu.CompilerParams(
            dimension_semantics=("arbitrary",), vmem_limit_bytes=VMEM_LIMIT),
        name="mix",
    )(h, mod, u, u, u, u, u, u, u, u, u, u, u, u,
      kvc, bias, wgrp, pscale, wap, wpp, wout, ln1g, ln1b)


ID_E0, ID_E1, ID_R0, ID_R1 = 0, 1, 4, 5
POS_A, POS_B = 0, 1
PLAN_EXPERT, PLAN_VALID, PLAN_NACT = 0, 1, 2
PLAN_W = 2 * LANES
CW_C0, CW_C1 = 0, 1


def _route_kernel(h_ref, mod_ref, wrt_ref, brt_ref, pos_ref, cw_ref, plan_ref, carry_ref, ids_all):
    i = pl.program_id(0)
    tm = ROUTE_TM

    @pl.when(i == 0)
    def _():
        carry_ref[...] = jnp.zeros_like(carry_ref)

    shift = mod_ref[0:1, 3 * D_MODEL:4 * D_MODEL]
    scale = mod_ref[0:1, 4 * D_MODEL:5 * D_MODEL]
    hm = _load_row_tiles(h_ref, tm) * (1.0 + scale) + shift

    hm_hi, hm_lo = _split_bf16(hm)
    w_hi, w_lo = _split_bf16(wrt_ref[...])
    nt = (((1,), (1,)), ((), ()))
    dg = functools.partial(lax.dot_general, dimension_numbers=nt, preferred_element_type=jnp.float32)
    logits = dg(w_hi, hm_hi) + (dg(w_hi, hm_lo) + dg(w_lo, hm_hi)) + brt_ref[:, 0:1]

    sub = lax.broadcasted_iota(jnp.int32, (LANES, tm), 0)
    big = jnp.int32(1 << 20)
    is_grp = sub < N_GROUPS
    gl = jnp.where(is_grp, logits, -jnp.inf)
    gmax = jnp.max(gl, axis=0, keepdims=True)
    gidx = jnp.min(jnp.where(gl == gmax, sub, big), axis=0, keepdims=True)
    gsum = jnp.sum(jnp.where(is_grp, jnp.exp(logits - gmax), 0.0), axis=0, keepdims=True)
    p_group = 1.0 / gsum

    eid = sub - N_GROUPS
    sel = (eid >= 0) & (eid < N_EXPERTS) & (lax.shift_right_arithmetic(eid, 3) == gidx)
    el = jnp.where(sel, logits, -jnp.inf)
    l0 = jnp.max(el, axis=0, keepdims=True)
    i0 = jnp.min(jnp.where(el == l0, sub, big), axis=0, keepdims=True)
    el2 = jnp.where(sub == i0, -jnp.inf, el)
    l1 = jnp.max(el2, axis=0, keepdims=True)
    i1 = jnp.min(jnp.where(el2 == l1, sub, big), axis=0, keepdims=True)
    t = jnp.exp(l1 - l0)
    w0 = 1.0 / (1.0 + t)
    w1 = t / (1.0 + t)

    onehot = jnp.where((sub == i0) | (sub == i1), 1.0, 0.0)
    rr = lax.broadcasted_iota(jnp.int32, (tm, tm), 0)
    cc = lax.broadcasted_iota(jnp.int32, (tm, tm), 1)
    earlier = jnp.where(rr < cc, 1.0, 0.0).astype(jnp.bfloat16)
    carry = carry_ref[:, 0:1]
    prefix = _bdot(onehot.astype(jnp.bfloat16), earlier) + carry
    r0 = jnp.sum(jnp.where(sub == i0, prefix, 0.0), axis=0, keepdims=True)
    r1 = jnp.sum(jnp.where(sub == i1, prefix, 0.0), axis=0, keepdims=True)
    total = jnp.broadcast_to(carry + jnp.sum(onehot, axis=1, keepdims=True), carry_ref.shape)
    carry_ref[...] = total

    sub8 = lax.broadcasted_iota(jnp.int32, (ROW_TILE, tm), 0)
    ids = jnp.zeros((ROW_TILE, tm), jnp.int32)
    for idx, val in ((ID_E0, i0 - N_GROUPS), (ID_E1, i1 - N_GROUPS),
                     (ID_R0, r0.astype(jnp.int32)), (ID_R1, r1.astype(jnp.int32))):
        ids = jnp.where(sub8 == idx, val, ids)
    ids_all[:, pl.ds(pl.multiple_of(i * tm, tm), tm)] = ids

    cwt = jnp.where(sub == CW_C0, p_group * w0, jnp.where(sub == CW_C1, p_group * w1, 0.0))
    cw_ref[...] = cwt.T

    @pl.when(i == pl.num_programs(0) - 1)
    def _():
        subq = lax.broadcasted_iota(jnp.int32, (LANES, LANES), 0)
        laneq = lax.broadcasted_iota(jnp.int32, (LANES, LANES), 1)
        cnt = total.astype(jnp.int32)
        tiles = lax.shift_right_logical(cnt + (EXP_TM - 1), EXP_TM.bit_length() - 1).astype(jnp.float32)
        incl = jnp.where(laneq <= subq, 1.0, 0.0).astype(jnp.bfloat16)
        tile_end = _bdot(incl, tiles.astype(jnp.bfloat16))
        tile_start = tile_end - tiles
        seg = (tile_start * EXP_TM).astype(jnp.int32)
        nact = jnp.max(tile_end, axis=0, keepdims=True)

        ids_full = ids_all[...]
        look = jnp.zeros_like(ids_full)
        for e in range(N_EXPERTS):
            look = jnp.where(ids_full == e, seg[N_GROUPS + e, 0], look)
        pos_ref[...] = look + pltpu.roll(ids_full, ID_R0 - ID_E0, axis=0)

        subp = lax.broadcasted_iota(jnp.int32, (LANES, PLAN_W), 0)
        tile = lax.broadcasted_iota(jnp.int32, (LANES, PLAN_W), 1).astype(jnp.float32)
        is_exp = (subp >= N_GROUPS) & (subp < N_GROUPS + N_EXPERTS)
        end_col = tile_end[:, 0:1]
        nact_s = nact[:, 0:1]
        te = jnp.sum(jnp.where(is_exp & (tile >= end_col), 1.0, 0.0), axis=0, keepdims=True)
        te_last = jnp.sum(jnp.where(is_exp & (nact_s - 1.0 >= end_col), 1.0, 0.0), axis=0, keepdims=True)[:, 0:1]
        tile_row = tile[0:1, :]
        te = jnp.minimum(jnp.where(tile_row < nact_s, te, te_last), N_EXPERTS - 1.0)
        mine = (subp - N_GROUPS).astype(jnp.float32) == te
        cnt_sel = jnp.sum(jnp.where(mine, total[:, 0:1], 0.0), axis=0, keepdims=True)
        start_sel = jnp.sum(jnp.where(mine, tile_start[:, 0:1], 0.0), axis=0, keepdims=True)
        valid = jnp.clip(cnt_sel - (tile_row - start_sel) * EXP_TM, 0.0, float(EXP_TM))
        valid = jnp.where(tile_row < nact_s, valid, 0.0)
        subr = lax.broadcasted_iota(jnp.int32, (ROW_TILE, PLAN_W), 0)
        plan = jnp.where(subr == PLAN_EXPERT, te, jnp.where(subr == PLAN_VALID, valid,
                         jnp.where(subr == PLAN_NACT, nact_s, 0.0)))
        plan_ref[...] = plan.astype(jnp.int32)


def _route_call(h, mod, wrt, brt):
    tm = ROUTE_TM
    return pl.pallas_call(
        _route_kernel,
        grid=(SEQ // tm,),
        in_specs=[
            pl.BlockSpec((tm * ROW_TILE, LANES), lambda i: (i, 0)),
            pl.BlockSpec(mod.shape, lambda i: (0, 0)),
            pl.BlockSpec((LANES, D_MODEL), lambda i: (0, 0)),
            pl.BlockSpec((LANES, LANES), lambda i: (0, 0)),
        ],
        out_specs=[
            pl.BlockSpec((ROW_TILE, SEQ), lambda i: (0, 0)),
            pl.BlockSpec((tm, LANES), lambda i: (i, 0)),
            pl.BlockSpec((ROW_TILE, PLAN_W), lambda i: (0, 0)),
        ],
        out_shape=[
            jax.ShapeDtypeStruct((ROW_TILE, SEQ), jnp.int32),
            jax.ShapeDtypeStruct((SEQ, LANES), jnp.float32),
            jax.ShapeDtypeStruct((ROW_TILE, PLAN_W), jnp.int32),
        ],
        scratch_shapes=[pltpu.VMEM((LANES, LANES), jnp.float32),
                        pltpu.VMEM((ROW_TILE, SEQ), jnp.int32)],
        compiler_params=pltpu.CompilerParams(
            dimension_semantics=("arbitrary",), vmem_limit_bytes=VMEM_LIMIT),
        name="route",
    )(h, mod, wrt, brt)


SRC_UNROLL = 16
EXP_BUFS = 3
EXP_CHUNK = 256
TILE_ROWS = EXP_TM * ROW_TILE


def _experts_kernel(te_ref, tv_ref, nact_ref, posa_ref, posb_ref,
                    h_hbm, mod_ref, wg_ref, wu_ref, wd_ref,
                    y_hbm,
                    src_ref, xbuf, ybuf, wgb, wub, wdb, gsem, ssem):
    i = pl.program_id(0)
    last = pl.num_programs(0) - 1
    nact = nact_ref[0]
    cur = lax.rem(i, EXP_BUFS)
    nxt2 = lax.rem(i + 2, EXP_BUFS)
    prv2 = lax.rem(i + 1, EXP_BUFS)

    def gather_copy(p, k, s):
        tok = src_ref[p] & (SEQ - 1)
        return pltpu.make_async_copy(h_hbm.at[pl.ds(pl.multiple_of(tok * ROW_TILE, ROW_TILE), ROW_TILE), :],
                                     xbuf.at[s, pl.ds(pl.multiple_of(k * ROW_TILE, ROW_TILE), ROW_TILE), :],
                                     gsem.at[s])

    def scatter_copy(p, k, s):
        dst = src_ref[p]
        return pltpu.make_async_copy(ybuf.at[s, pl.ds(pl.multiple_of(k * ROW_TILE, ROW_TILE), ROW_TILE), :],
                                     y_hbm.at[pl.ds(pl.multiple_of(dst * ROW_TILE, ROW_TILE), ROW_TILE), :],
                                     ssem.at[s])

    def gather_rolled(tile, s):
        def body(kk, c):
            for u in range(SRC_UNROLL):
                k = kk * SRC_UNROLL + u
                gather_copy(tile * EXP_TM + k, k, s).start()
            return c
        lax.fori_loop(0, EXP_TM // SRC_UNROLL, body, 0)

    def scatter_rolled(tile, s, n):
        def body(k, c):
            scatter_copy(tile * EXP_TM + k, k, s).start()
            return c
        lax.fori_loop(0, n, body, 0)

    def wait_gather(s):
        pltpu.make_async_copy(h_hbm.at[pl.ds(0, TILE_ROWS), :], xbuf.at[s], gsem.at[s]).wait()

    def wait_scatter(s, n):
        rows = pl.multiple_of(n * ROW_TILE, ROW_TILE)
        pltpu.make_async_copy(ybuf.at[s, pl.ds(0, rows), :], y_hbm.at[pl.ds(0, rows), :], ssem.at[s]).wait()

    def compute_chunks(s):
        state = {}

        def load():
            x = _load_row_tiles(xbuf, EXP_TM, lead=(s,))
            shift = mod_ref[0:1, 3 * D_MODEL:4 * D_MODEL]
            scale = mod_ref[0:1, 4 * D_MODEL:5 * D_MODEL]
            state["x"] = (x * (1.0 + scale) + shift).astype(jnp.bfloat16)
            state["act"] = []

        def gate(c):
            def run():
                if c == 0:
                    load()
                state["a"] = _bdot(state["x"], wgb[:, c * EXP_CHUNK:(c + 1) * EXP_CHUNK])
            return run

        def up(c):
            def run():
                a = state["a"]
                u = _bdot(state["x"], wub[:, c * EXP_CHUNK:(c + 1) * EXP_CHUNK])
                state["act"].append((a * jax.nn.sigmoid(a) * u).astype(jnp.bfloat16))
            return run

        def down(c):
            def run():
                if c == 0:
                    state["actf"] = jnp.concatenate(state["act"], axis=-1)
                yc = _bdot(state["actf"], wdb[:, c * EXP_CHUNK:(c + 1) * EXP_CHUNK])
                for jj in range(EXP_CHUNK // LANES):
                    j = c * (EXP_CHUNK // LANES) + jj
                    ybuf[s, pl.ds(j, EXP_TM, stride=ROW_TILE), :] = yc[:, jj * LANES:(jj + 1) * LANES]
            return run

        steps = []
        for c in range(D_EXPERT // EXP_CHUNK):
            steps += [(gate(c), 2), (up(c), 2)]
        return steps + [(down(c), 1) for c in range(D_MODEL // EXP_CHUNK)]

    def run_interleaved(chunks, dmas):
        total_cost = sum(cost for _, cost in chunks)
        done = 0
        for chunk, cost in chunks:
            upto = -(-len(dmas) * (done + cost) // total_cost)
            for d in dmas[-(-len(dmas) * done // total_cost):upto]:
                d()
            done += cost
            chunk()

    def gather_dmas(tile, s):
        return [functools.partial(lambda k: gather_copy(tile * EXP_TM + k, k, s).start(priority=k % 2), k)
                for k in range(EXP_TM)]

    def scatter_dmas(tile, s):
        return [functools.partial(lambda k: scatter_copy(tile * EXP_TM + k, k, s).start(priority=(k + 1) % 2), k)
                for k in range(EXP_TM)]

    tile_to_gather = lambda t: jnp.minimum(t, nact - 1)

    @pl.when(i == 0)
    def _():
        def fill_body(tt, c):
            ts = [tt * SRC_UNROLL + u for u in range(SRC_UNROLL)]
            pa = [posa_ref[t] for t in ts]
            pb = [posb_ref[t] for t in ts]
            for t, a, b in zip(ts, pa, pb):
                src_ref[a] = t
                src_ref[b] = t + SEQ
            return c
        lax.fori_loop(0, SEQ // SRC_UNROLL, fill_body, 0)

        def pad_tile(t, c):
            def pad_row(k, c2):
                src_ref[t * EXP_TM + k] = 0
                return c2
            return lax.fori_loop(tv_ref[t], EXP_TM, pad_row, c)
        lax.fori_loop(0, nact, pad_tile, 0)
        gather_rolled(0, 0)
        gather_rolled(tile_to_gather(1), 1)

    active = i < nact
    prev = jnp.maximum(i - 1, 0)
    prev_ok = (i >= 1) & (i - 1 < nact)
    prev2 = jnp.maximum(i - 2, 0)
    prev2_ok = (i >= 2) & (i - 2 < nact)
    nprev = tv_ref[prev]
    full_prev = prev_ok & (nprev == EXP_TM)

    @pl.when((i < 2) | prev2_ok)
    def _():
        wait_gather(cur)

    @pl.when(prev2_ok)
    def _():
        wait_scatter(prv2, tv_ref[prev2])

    @pl.when(active & ((i == 0) | (te_ref[i] != te_ref[prev])))
    def _():
        wgb[...] = wg_ref[0].astype(jnp.bfloat16)
        wub[...] = wu_ref[0].astype(jnp.bfloat16)
        wdb[...] = wd_ref[0].astype(jnp.bfloat16)

    @pl.when(prev_ok & jnp.logical_not(active & full_prev))
    def _():
        scatter_rolled(prev, nxt2, nprev)

    @pl.when(active & full_prev)
    def _():
        g, sc = gather_dmas(tile_to_gather(i + 2), nxt2), scatter_dmas(prev, nxt2)
        run_interleaved(compute_chunks(cur), [d for pair in zip(sc, g) for d in pair])

    @pl.when(active & jnp.logical_not(full_prev))
    def _():
        run_interleaved(compute_chunks(cur), gather_dmas(tile_to_gather(i + 2), nxt2))

    @pl.when(i == last)
    def _():
        @pl.when(prev_ok)
        def _():
            wait_gather(prv2)
            wait_scatter(nxt2, nprev)

        @pl.when(active)
        def _():
            wait_gather(nxt2)
            scatter_rolled(i, cur, tv_ref[i])
            wait_scatter(cur, tv_ref[i])


def _experts_call(te, tv, nact, posa, posb, h, mod, wg, wu, wd):
    idx = lambda i, te_ref, *_: (te_ref[i], 0, 0)
    grid_spec = pltpu.PrefetchScalarGridSpec(
        num_scalar_prefetch=5,
        grid=(EXP_TILES,),
        in_specs=[
            pl.BlockSpec(memory_space=pl.ANY),
            pl.BlockSpec(mod.shape, lambda i, *_: (0, 0)),
            pl.BlockSpec((1, D_MODEL, D_EXPERT), idx),
            pl.BlockSpec((1, D_MODEL, D_EXPERT), idx),
            pl.BlockSpec((1, D_EXPERT, D_MODEL), idx),
        ],
        out_specs=pl.BlockSpec(memory_space=pl.ANY),
        scratch_shapes=[
            pltpu.SMEM((EXP_TILES * EXP_TM,), jnp.int32),
            pltpu.VMEM((EXP_BUFS, TILE_ROWS, LANES), jnp.float32),
            pltpu.VMEM((EXP_BUFS, TILE_ROWS, LANES), jnp.float32),
            pltpu.VMEM((D_MODEL, D_EXPERT), jnp.bfloat16),
            pltpu.VMEM((D_MODEL, D_EXPERT), jnp.bfloat16),
            pltpu.VMEM((D_EXPERT, D_MODEL), jnp.bfloat16),
            pltpu.SemaphoreType.DMA((EXP_BUFS,)),
            pltpu.SemaphoreType.DMA((EXP_BUFS,)),
        ],
    )
    return pl.pallas_call(
        _experts_kernel,
        grid_spec=grid_spec,
        out_shape=jax.ShapeDtypeStruct((2 * SEQ * ROW_TILE, LANES), jnp.float32),
        compiler_params=pltpu.CompilerParams(
            dimension_semantics=("arbitrary",), vmem_limit_bytes=VMEM_LIMIT),
        name="experts",
    )(te, tv, nact, posa, posb, h, mod, wg, wu, wd)


def _combine_kernel(h_ref, y0_ref, y1_ref, cw_ref, mod_ref, g_ref, b_ref, o_ref):
    tm = o_ref.shape[0]
    c0 = cw_ref[:, CW_C0:CW_C0 + 1]
    c1 = cw_ref[:, CW_C1:CW_C1 + 1]
    ffn = c0 * _load_row_tiles(y0_ref, tm) + c1 * _load_row_tiles(y1_ref, tm)
    g2 = mod_ref[0:1, 5 * D_MODEL:6 * D_MODEL]
    o_ref[...] = _layer_norm(DEEPNORM_ALPHA * _load_row_tiles(h_ref, tm) + g2 * ffn, g_ref[...], b_ref[...])


def _combine_call(h, y, cw, mod, g, b):
    tm = CMB_TM
    nblk = SEQ // tm
    tiles = lambda off: pl.BlockSpec((tm * ROW_TILE, LANES), lambda i: (i + off, 0))
    return pl.pallas_call(
        _combine_kernel,
        grid=(nblk,),
        in_specs=[
            tiles(0), tiles(0), tiles(nblk),
            pl.BlockSpec((tm, LANES), lambda i: (i, 0)),
            pl.BlockSpec(mod.shape, lambda i: (0, 0)),
            pl.BlockSpec((1, D_MODEL), lambda i: (0, 0)),
            pl.BlockSpec((1, D_MODEL), lambda i: (0, 0)),
        ],
        out_specs=pl.BlockSpec((tm, D_MODEL), lambda i: (i, 0)),
        out_shape=jax.ShapeDtypeStruct((SEQ, D_MODEL), jnp.float32),
        compiler_params=pltpu.CompilerParams(
            dimension_semantics=("arbitrary",), vmem_limit_bytes=VMEM_LIMIT),
        name="combine",
    )(h, y, y, cw, mod, g, b)


def kernel(x, c, ctx, c_ctx, ln_in_g, ln_in_b, w_mod, b_mod, w_in, rpb, w_pool_grp, pool_scale,
           w_attn_proj, w_pool_proj, w_out, ln1_g, ln1_b, w_router_group, b_router_group,
           w_router_expert, b_router_expert, w_expert_gate, w_expert_up, w_expert_down, ln2_g, ln2_b):
    assert x.shape == (1, SEQ, D_MODEL) and ctx.shape == (1, CTX_LEN, D_MODEL)
    assert w_mod.shape[0] == 1, "single-layer trunk"
    f32, bf16 = jnp.float32, jnp.bfloat16
    row = lambda v: v.reshape(1, -1).astype(f32)

    cond = jnp.concatenate([c, c_ctx[None], jnp.zeros((MOD_ROWS - 2, D_MODEL), f32)], axis=0)
    mod = _mod_call(cond, w_mod[0], row(b_mod[0]))

    lng, lnb = row(ln_in_g), row(ln_in_b)
    w_in_b = w_in[0].astype(bf16)
    u, h0 = _proj_call(x[0], mod, lng, lnb, w_in_b, mod_row=0, latent=True, tm=PROJ_TM)
    kvc, = _proj_call(ctx[0], mod, lng, lnb, w_in_b[:, ATT_W:3 * ATT_W], mod_row=1, latent=False, tm=CTX_LEN)

    h1 = _mix_call(h0, mod, u, kvc, _attn_bias_table(rpb[0]),
                   w_pool_grp[0].astype(bf16), row(pool_scale[0]),
                   w_attn_proj[0].astype(bf16), w_pool_proj[0].astype(bf16), w_out[0].astype(bf16),
                   row(ln1_g[0]), row(ln1_b[0]))

    n_logit = N_GROUPS + N_EXPERTS
    wrt = jnp.concatenate([w_router_group[0].T, w_router_expert[0].T,
                           jnp.zeros((LANES - n_logit, D_MODEL), f32)], axis=0)
    brt = jnp.concatenate([b_router_group[0], b_router_expert[0], jnp.zeros((LANES - n_logit,), f32)])
    brt = jnp.broadcast_to(brt[:, None], (LANES, LANES))
    pos, cw, plan = _route_call(h1, mod, wrt, brt)

    y = _experts_call(plan[PLAN_EXPERT, :EXP_TILES], plan[PLAN_VALID, :EXP_TILES], plan[PLAN_NACT, :1],
                      pos[POS_A], pos[POS_B], h1, mod,
                      w_expert_gate[0], w_expert_up[0], w_expert_down[0])
    out = _combine_call(h1, y, cw, mod, row(ln2_g[0]), row(ln2_b[0]))
    return out[None]
```

```python
import functools

import jax
import jax.numpy as jnp
from jax import lax
from jax.experimental import pallas as pl
from jax.experimental.pallas import tpu as pltpu

D_MODEL = 1024
SEQ = 16384
GRID_W = 64
ROWS = SEQ // GRID_W
CTX_LEN = 256
N_HEADS = 8
HEAD_DIM = 64
ATT_W = N_HEADS * HEAD_DIM
WIN_H = 8
WIN_W = 16
POOL_WINDOWS = (2, 4, 8, 16)
POOL_GROUPS = 4
POOL_DIM = 128
POOL_W = POOL_GROUPS * POOL_DIM
PROJ_W = 3 * ATT_W + POOL_W + 2 * D_MODEL
GATE_COL = 3 * ATT_W + POOL_W
N_GROUPS = 4
EXPERTS_PER_GROUP = 8
N_EXPERTS = N_GROUPS * EXPERTS_PER_GROUP
D_EXPERT = 512
N_MOD = 6
DEEPNORM_ALPHA = 2.0 ** 0.25
LN_EPS = 1e-5
NEG_INF = -1e30

LANES = 128
ROW_TILE = 8
MOD_ROWS = 8
PROJ_TM = 512
MIX_ROWS = 8
MIX_TQ = MIX_ROWS * GRID_W
KV_HALO = 4 * GRID_W
POOL_HALO = 16
ROUTE_TM = 512
EXP_TM = 256
SEQ_PARTS = 2
PART_TOKENS = SEQ // SEQ_PARTS
N_SEG = SEQ_PARTS * N_EXPERTS
EXP_TILES = 2 * SEQ // EXP_TM + N_SEG
CMB_TM = 256
HALF = D_MODEL // 2
VMEM_LIMIT = 56 * 1024 * 1024
EXPERTS_VMEM_LIMIT = 60 * 1024 * 1024


def _layer_norm(x, g, b):
    mu = jnp.mean(x, axis=-1, keepdims=True)
    xc = x - mu
    var = jnp.mean(xc * xc, axis=-1, keepdims=True)
    return xc * lax.rsqrt(var + LN_EPS) * g + b


def _bdot(a, b):
    return jnp.dot(a, b, preferred_element_type=jnp.float32)


def _split_bf16(a):
    hi = a.astype(jnp.bfloat16)
    lo = (a - hi.astype(jnp.float32)).astype(jnp.bfloat16)
    return hi, lo


def _dot3(a, b):
    a_hi, a_lo = _split_bf16(a)
    b_hi, b_lo = _split_bf16(b)
    return _bdot(a_hi, b_hi) + (_bdot(a_hi, b_lo) + _bdot(a_lo, b_hi))


def _load_row_tiles(ref, tokens, lead=()):
    parts = [ref[(*lead, pl.ds(j, tokens, stride=ROW_TILE), slice(None))] for j in range(ROW_TILE)]
    return jnp.concatenate(parts, axis=-1)


def _store_row_tiles(ref, value, lead=()):
    tokens = value.shape[0]
    for j in range(ROW_TILE):
        ref[(*lead, pl.ds(j, tokens, stride=ROW_TILE), slice(None))] = value[:, j * LANES:(j + 1) * LANES]


def _mod_kernel(cond_ref, w_ref, b_ref, o_ref):
    cond = cond_ref[...]
    act = cond * jax.nn.sigmoid(cond)
    o_ref[...] = _dot3(act, w_ref[...]) + b_ref[...]


def _mod_call(cond, w_mod, b_mod):
    tn = 1536
    n = N_MOD * D_MODEL
    return pl.pallas_call(
        _mod_kernel,
        grid=(n // tn,),
        in_specs=[
            pl.BlockSpec((MOD_ROWS, D_MODEL), lambda i: (0, 0)),
            pl.BlockSpec((D_MODEL, tn), lambda i: (0, i)),
            pl.BlockSpec((1, tn), lambda i: (0, i)),
        ],
        out_specs=pl.BlockSpec((MOD_ROWS, tn), lambda i: (0, i)),
        out_shape=jax.ShapeDtypeStruct((MOD_ROWS, n), jnp.float32),
        compiler_params=pltpu.CompilerParams(
            dimension_semantics=("arbitrary",), vmem_limit_bytes=VMEM_LIMIT),
        name="mod",
    )(cond, w_mod, b_mod)


def _proj_kernel(x_ref, mod_ref, g_ref, b_ref, w_ref, o_ref, *h_out, mod_row, latent):
    h = _layer_norm(x_ref[...], g_ref[...], b_ref[...])
    if latent:
        h_out[0][...] = h
    shift = mod_ref[mod_row:mod_row + 1, 0:D_MODEL]
    scale = mod_ref[mod_row:mod_row + 1, D_MODEL:2 * D_MODEL]
    hm = (h * (1.0 + scale) + shift).astype(jnp.bfloat16)
    n = o_ref.shape[1]
    for c in range(n // D_MODEL):
        sl = slice(c * D_MODEL, (c + 1) * D_MODEL)
        res = _bdot(hm, w_ref[:, sl])
        if latent and c == 0:
            lane = lax.broadcasted_iota(jnp.int32, (1, D_MODEL), 1)
            res = res * jnp.where(lane < ATT_W, HEAD_DIM ** -0.5, 1.0)
        if latent and c * D_MODEL >= GATE_COL:
            res = jax.nn.sigmoid(res)
        o_ref[:, sl] = res.astype(jnp.bfloat16)


def _proj_call(x, mod, g, b, w, *, mod_row, latent, tm):
    rows, n = x.shape[0], w.shape[1]
    out_specs = [pl.BlockSpec((tm, n), lambda i: (i, 0))]
    out_shape = [jax.ShapeDtypeStruct((rows, n), jnp.bfloat16)]
    if latent:
        out_specs.append(pl.BlockSpec((tm, D_MODEL), lambda i: (i, 0)))
        out_shape.append(jax.ShapeDtypeStruct((rows, D_MODEL), jnp.float32))
    return pl.pallas_call(
        functools.partial(_proj_kernel, mod_row=mod_row, latent=latent),
        grid=(rows // tm,),
        in_specs=[
            pl.BlockSpec((tm, D_MODEL), lambda i: (i, 0)),
            pl.BlockSpec(mod.shape, lambda i: (0, 0)),
            pl.BlockSpec((1, D_MODEL), lambda i: (0, 0)),
            pl.BlockSpec((1, D_MODEL), lambda i: (0, 0)),
            pl.BlockSpec((D_MODEL, n), lambda i: (0, 0), pipeline_mode=pl.Buffered(1)),
        ],
        out_specs=out_specs,
        out_shape=out_shape,
        compiler_params=pltpu.CompilerParams(
            dimension_semantics=("arbitrary",), vmem_limit_bytes=VMEM_LIMIT),
        name="proj",
    )(x, mod, g, b, w)


def _attn_bias_table(rpb):
    col = jnp.arange(GRID_W, dtype=jnp.int32)
    col_start = jnp.clip(col - WIN_W // 2, 0, GRID_W - WIN_W)
    col_mask = (col[None, :] >= col_start[:, None]) & (col[None, :] < col_start[:, None] + WIN_W)
    col_off = jnp.clip(col[None, :] - col[:, None], 1 - WIN_W, WIN_W - 1) + (WIN_W - 1)
    onehot = (col_off[None] == jnp.arange(2 * WIN_W - 1, dtype=jnp.int32)[:, None, None]).astype(jnp.float32)
    tab = jnp.einsum("hrc,cqk->hrqk", rpb.astype(jnp.float32), onehot, precision=lax.Precision.HIGHEST)
    tab = jnp.where(col_mask[None, None], tab, NEG_INF)
    tab = jnp.stack([tab[:, WIN_H - 1 - v:2 * WIN_H - 1 - v] for v in range(WIN_H)], axis=0)
    tab = tab.transpose(0, 1, 3, 2, 4)
    return tab.reshape(WIN_H, N_HEADS // 2, 2 * GRID_W, WIN_H * GRID_W)


def _mix_kernel(h_ref, mod_ref,
                q_ref, kp_ref, kc_ref, kn_ref, vp_ref, vc_ref, vn_ref,
                pp_ref, pc_ref, pn_ref, ga_ref, gb_ref,
                kvc_ref, bias_ref, wgrp_ref, pscale_ref, wap_ref, wpp_ref, wout_ref,
                ln1g_ref, ln1b_ref,
                o_ref,
                kbuf, vbuf, yabuf, pbuf, ypbuf):
    b = pl.program_id(0)
    nb = pl.num_programs(0)

    kbuf[0:KV_HALO, :] = kp_ref[...]
    kbuf[KV_HALO:KV_HALO + MIX_TQ, :] = kc_ref[...]
    kbuf[KV_HALO + MIX_TQ:, :] = kn_ref[...]
    vbuf[0:KV_HALO, :] = vp_ref[...]
    vbuf[KV_HALO:KV_HALO + MIX_TQ, :] = vc_ref[...]
    vbuf[KV_HALO + MIX_TQ:, :] = vn_ref[...]

    lane = lax.broadcasted_iota(jnp.int32, (GRID_W, LANES), 1)
    first_head = lane < HEAD_DIM

    def row_body(j):
        r = b * MIX_ROWS + j
        rs = jnp.clip(r - WIN_H // 2, 0, ROWS - WIN_H)
        off = pl.multiple_of((rs - b * MIX_ROWS + WIN_H // 2) * GRID_W, GRID_W)
        var = r - rs
        qoff = j * GRID_W
        for pair in range(N_HEADS // 2):
            cols = slice(pair * LANES, (pair + 1) * LANES)
            q = q_ref[pl.ds(qoff, GRID_W), cols]
            zero = jnp.zeros_like(q)
            q2 = jnp.concatenate([jnp.where(first_head, q, zero), jnp.where(first_head, zero, q)], axis=0)
            kw = kbuf[pl.ds(off, WIN_H * GRID_W), cols]
            vw = vbuf[pl.ds(off, WIN_H * GRID_W), cols]
            kctx = kvc_ref[:, cols]
            vctx = kvc_ref[:, ATT_W + pair * LANES:ATT_W + (pair + 1) * LANES]
            nt = (((1,), (1,)), ((), ()))
            s_loc = lax.dot_general(q2, kw, nt, preferred_element_type=jnp.float32) + bias_ref[var, pair]
            s_ctx = lax.dot_general(q2, kctx, nt, preferred_element_type=jnp.float32)
            m = jnp.maximum(jnp.max(s_loc, axis=-1, keepdims=True), jnp.max(s_ctx, axis=-1, keepdims=True))
            p_loc = jnp.exp(s_loc - m)
            p_ctx = jnp.exp(s_ctx - m)
            denom = jnp.sum(p_loc, axis=-1, keepdims=True) + jnp.sum(p_ctx, axis=-1, keepdims=True)
            o2 = _bdot(p_loc.astype(jnp.bfloat16), vw) + _bdot(p_ctx.astype(jnp.bfloat16), vctx)
            o2 = o2 * (1.0 / denom)
            o_pair = jnp.where(first_head, o2[:GRID_W], o2[GRID_W:])
            yabuf[pl.ds(qoff, GRID_W), cols] = o_pair.astype(jnp.bfloat16)

    for j in range(MIX_ROWS):
        row_body(j)

    pbuf[0:POOL_HALO, :] = jnp.where(b > 0, pp_ref[...].astype(jnp.float32), 0.0)
    pbuf[POOL_HALO:POOL_HALO + MIX_TQ, :] = pc_ref[...].astype(jnp.float32)
    pbuf[POOL_HALO + MIX_TQ:, :] = jnp.where(b < nb - 1, pn_ref[...].astype(jnp.float32), 0.0)
    t_abs = b * MIX_TQ + lax.broadcasted_iota(jnp.int32, (MIX_TQ, 1), 0)
    for g, win in enumerate(POOL_WINDOWS):
        cols = slice(g * POOL_DIM, (g + 1) * POOL_DIM)
        acc = None
        for d in range(-(win // 2), win - win // 2):
            term = pbuf[POOL_HALO + d:POOL_HALO + d + MIX_TQ, cols]
            acc = term if acc is None else acc + term
        count = jnp.minimum(t_abs + (win - win // 2), SEQ) - jnp.maximum(t_abs - win // 2, 0)
        pooled = acc / count.astype(jnp.float32) - pbuf[POOL_HALO:POOL_HALO + MIX_TQ, cols]
        yp = _bdot(pooled.astype(jnp.bfloat16), wgrp_ref[g]) * pscale_ref[:, cols]
        ypbuf[:, cols] = yp.astype(jnp.bfloat16)

    ya = _bdot(yabuf[...], wap_ref[...])
    yp = _bdot(ypbuf[...], wpp_ref[...])
    z = ga_ref[...].astype(jnp.float32) * ya + gb_ref[...].astype(jnp.float32) * yp
    y = _bdot(z.astype(jnp.bfloat16), wout_ref[...])
    g1 = mod_ref[0:1, 2 * D_MODEL:3 * D_MODEL]
    _store_row_tiles(o_ref, _layer_norm(DEEPNORM_ALPHA * h_ref[...] + g1 * y, ln1g_ref[...], ln1b_ref[...]))


def _mix_call(h, mod, u, kvc, bias, wgrp, pscale, wap, wpp, wout, ln1g, ln1b):
    nb = SEQ // MIX_TQ
    halo_per_blk = MIX_TQ // KV_HALO
    n_halo = SEQ // KV_HALO
    ph_per_blk = MIX_TQ // POOL_HALO
    n_ph = SEQ // POOL_HALO

    def const(shape):
        return pl.BlockSpec(shape, lambda i: (0,) * len(shape), pipeline_mode=pl.Buffered(1))

    def prev_halo(c):
        return pl.BlockSpec((KV_HALO, ATT_W), lambda i: (jnp.maximum(i * halo_per_blk - 1, 0), c))

    def next_halo(c):
        return pl.BlockSpec((KV_HALO, ATT_W), lambda i: (jnp.minimum((i + 1) * halo_per_blk, n_halo - 1), c))

    def cur(c):
        return pl.BlockSpec((MIX_TQ, ATT_W), lambda i: (i, c))

    in_specs = [
        pl.BlockSpec((MIX_TQ, D_MODEL), lambda i: (i, 0)),
        const(mod.shape),
        cur(0),
        prev_halo(1), cur(1), next_halo(1),
        prev_halo(2), cur(2), next_halo(2),
        pl.BlockSpec((POOL_HALO, POOL_W), lambda i: (jnp.maximum(i * ph_per_blk - 1, 0), 3)),
        cur(3),
        pl.BlockSpec((POOL_HALO, POOL_W), lambda i: (jnp.minimum((i + 1) * ph_per_blk, n_ph - 1), 3)),
        pl.BlockSpec((MIX_TQ, D_MODEL), lambda i: (i, 2)),
        pl.BlockSpec((MIX_TQ, D_MODEL), lambda i: (i, 3)),
        const(kvc.shape), const(bias.shape), const(wgrp.shape), const(pscale.shape),
        const(wap.shape), const(wpp.shape), const(wout.shape),
        const((1, D_MODEL)), const((1, D_MODEL)),
    ]
    return pl.pallas_call(
        _mix_kernel,
        grid=(nb,),
        in_specs=in_specs,
        out_specs=pl.BlockSpec((MIX_TQ * ROW_TILE, LANES), lambda i: (i, 0)),
        out_shape=jax.ShapeDtypeStruct((SEQ * ROW_TILE, LANES), jnp.float32),
        scratch_shapes=[
            pltpu.VMEM((MIX_TQ + 2 * KV_HALO, ATT_W), jnp.bfloat16),
            pltpu.VMEM((MIX_TQ + 2 * KV_HALO, ATT_W), jnp.bfloat16),
            pltpu.VMEM((MIX_TQ, ATT_W), jnp.bfloat16),
            pltpu.VMEM((MIX_TQ + 2 * POOL_HALO, POOL_W), jnp.float32),
            pltpu.VMEM((MIX_TQ, POOL_W), jnp.bfloat16),
        ],
        compiler_params=pltpu.CompilerParams(
            dimension_semantics=("arbitrary",), vmem_limit_bytes=VMEM_LIMIT),
        name="mix",
    )(h, mod, u, u, u, u, u, u, u, u, u, u, u, u,
      kvc, bias, wgrp, pscale, wap, wpp, wout, ln1g, ln1b)


ID_E0, ID_E1, ID_R0, ID_R1 = 0, 1, 4, 5
POS_A, POS_B = 0, 1
PLAN_EXPERT, PLAN_VALID, PLAN_NACT = 0, 1, 2
PLAN_W = 2 * LANES
CW_C0, CW_C1 = 0, 1


def _route_kernel(h_ref, mod_ref, wrt_ref, brt_ref, pos_ref, cw_ref, plan_ref, carry_ref, ids_all):
    i = pl.program_id(0)
    tm = ROUTE_TM

    @pl.when(i == 0)
    def _():
        carry_ref[...] = jnp.zeros_like(carry_ref)

    shift = mod_ref[0:1, 3 * D_MODEL:4 * D_MODEL]
    scale = mod_ref[0:1, 4 * D_MODEL:5 * D_MODEL]
    hm = _load_row_tiles(h_ref, tm) * (1.0 + scale) + shift

    hm_hi, hm_lo = _split_bf16(hm)
    w_hi, w_lo = _split_bf16(wrt_ref[...])
    nt = (((1,), (1,)), ((), ()))
    dg = functools.partial(lax.dot_general, dimension_numbers=nt, preferred_element_type=jnp.float32)
    logits = dg(w_hi, hm_hi) + (dg(w_hi, hm_lo) + dg(w_lo, hm_hi)) + brt_ref[:, 0:1]

    sub = lax.broadcasted_iota(jnp.int32, (LANES, tm), 0)
    big = jnp.int32(1 << 20)
    is_grp = sub < N_GROUPS
    gl = jnp.where(is_grp, logits, -jnp.inf)
    gmax = jnp.max(gl, axis=0, keepdims=True)
    gidx = jnp.min(jnp.where(gl == gmax, sub, big), axis=0, keepdims=True)
    gsum = jnp.sum(jnp.where(is_grp, jnp.exp(logits - gmax), 0.0), axis=0, keepdims=True)
    p_group = 1.0 / gsum

    eid = sub - N_GROUPS
    sel = (eid >= 0) & (eid < N_EXPERTS) & (lax.shift_right_arithmetic(eid, 3) == gidx)
    el = jnp.where(sel, logits, -jnp.inf)
    l0 = jnp.max(el, axis=0, keepdims=True)
    i0 = jnp.min(jnp.where(el == l0, sub, big), axis=0, keepdims=True)
    el2 = jnp.where(sub == i0, -jnp.inf, el)
    l1 = jnp.max(el2, axis=0, keepdims=True)
    i1 = jnp.min(jnp.where(el2 == l1, sub, big), axis=0, keepdims=True)
    t = jnp.exp(l1 - l0)
    w0 = 1.0 / (1.0 + t)
    w1 = t / (1.0 + t)

    half_rows = jnp.where(i >= pl.num_programs(0) // SEQ_PARTS, N_EXPERTS, 0)
    i0 = i0 + half_rows
    i1 = i1 + half_rows
    onehot = jnp.where((sub == i0) | (sub == i1), 1.0, 0.0)
    rr = lax.broadcasted_iota(jnp.int32, (tm, tm), 0)
    cc = lax.broadcasted_iota(jnp.int32, (tm, tm), 1)
    earlier = jnp.where(rr < cc, 1.0, 0.0).astype(jnp.bfloat16)
    carry = carry_ref[:, 0:1]
    prefix = _bdot(onehot.astype(jnp.bfloat16), earlier) + carry
    r0 = jnp.sum(jnp.where(sub == i0, prefix, 0.0), axis=0, keepdims=True)
    r1 = jnp.sum(jnp.where(sub == i1, prefix, 0.0), axis=0, keepdims=True)
    total = jnp.broadcast_to(carry + jnp.sum(onehot, axis=1, keepdims=True), carry_ref.shape)
    carry_ref[...] = total

    sub8 = lax.broadcasted_iota(jnp.int32, (ROW_TILE, tm), 0)
    ids = jnp.zeros((ROW_TILE, tm), jnp.int32)
    for idx, val in ((ID_E0, i0 - N_GROUPS), (ID_E1, i1 - N_GROUPS),
                     (ID_R0, r0.astype(jnp.int32)), (ID_R1, r1.astype(jnp.int32))):
        ids = jnp.where(sub8 == idx, val, ids)
    ids_all[:, pl.ds(pl.multiple_of(i * tm, tm), tm)] = ids

    cwt = jnp.where(sub == CW_C0, p_group * w0, jnp.where(sub == CW_C1, p_group * w1, 0.0))
    cw_ref[...] = cwt.T

    @pl.when(i == pl.num_programs(0) - 1)
    def _():
        subq = lax.broadcasted_iota(jnp.int32, (LANES, LANES), 0)
        laneq = lax.broadcasted_iota(jnp.int32, (LANES, LANES), 1)
        cnt = total.astype(jnp.int32)
        tiles = lax.shift_right_logical(cnt + (EXP_TM - 1), EXP_TM.bit_length() - 1).astype(jnp.float32)
        incl = jnp.where(laneq <= subq, 1.0, 0.0).astype(jnp.bfloat16)
        tile_end = _bdot(incl, tiles.astype(jnp.bfloat16))
        tile_start = tile_end - tiles
        seg = (tile_start * EXP_TM).astype(jnp.int32)
        nact = jnp.max(tile_end, axis=0, keepdims=True)

        ids_full = ids_all[...]
        look = jnp.zeros_like(ids_full)
        for e in range(N_SEG):
            look = jnp.where(ids_full == e, seg[N_GROUPS + e, 0], look)
        pos_ref[...] = look + pltpu.roll(ids_full, ID_R0 - ID_E0, axis=0)

        subp = lax.broadcasted_iota(jnp.int32, (LANES, PLAN_W), 0)
        tile = lax.broadcasted_iota(jnp.int32, (LANES, PLAN_W), 1).astype(jnp.float32)
        is_exp = (subp >= N_GROUPS) & (subp < N_GROUPS + N_SEG)
        end_col = tile_end[:, 0:1]
        nact_s = nact[:, 0:1]
        te = jnp.sum(jnp.where(is_exp & (tile >= end_col), 1.0, 0.0), axis=0, keepdims=True)
        te_last = jnp.sum(jnp.where(is_exp & (nact_s - 1.0 >= end_col), 1.0, 0.0), axis=0, keepdims=True)[:, 0:1]
        tile_row = tile[0:1, :]
        te = jnp.minimum(jnp.where(tile_row < nact_s, te, te_last), N_SEG - 1.0)
        mine = (subp - N_GROUPS).astype(jnp.float32) == te
        cnt_sel = jnp.sum(jnp.where(mine, total[:, 0:1], 0.0), axis=0, keepdims=True)
        start_sel = jnp.sum(jnp.where(mine, tile_start[:, 0:1], 0.0), axis=0, keepdims=True)
        valid = jnp.clip(cnt_sel - (tile_row - start_sel) * EXP_TM, 0.0, float(EXP_TM))
        valid = jnp.where(tile_row < nact_s, valid, 0.0)
        subr = lax.broadcasted_iota(jnp.int32, (ROW_TILE, PLAN_W), 0)
        plan = jnp.where(subr == PLAN_EXPERT, te, jnp.where(subr == PLAN_VALID, valid,
                         jnp.where(subr == PLAN_NACT, nact_s, 0.0)))
        plan_ref[...] = plan.astype(jnp.int32)


def _route_call(h, mod, wrt, brt):
    tm = ROUTE_TM
    return pl.pallas_call(
        _route_kernel,
        grid=(SEQ // tm,),
        in_specs=[
            pl.BlockSpec((tm * ROW_TILE, LANES), lambda i: (i, 0)),
            pl.BlockSpec(mod.shape, lambda i: (0, 0)),
            pl.BlockSpec((LANES, D_MODEL), lambda i: (0, 0)),
            pl.BlockSpec((LANES, LANES), lambda i: (0, 0)),
        ],
        out_specs=[
            pl.BlockSpec((ROW_TILE, SEQ), lambda i: (0, 0)),
            pl.BlockSpec((tm, LANES), lambda i: (i, 0)),
            pl.BlockSpec((ROW_TILE, PLAN_W), lambda i: (0, 0)),
        ],
        out_shape=[
            jax.ShapeDtypeStruct((ROW_TILE, SEQ), jnp.int32),
            jax.ShapeDtypeStruct((SEQ, LANES), jnp.float32),
            jax.ShapeDtypeStruct((ROW_TILE, PLAN_W), jnp.int32),
        ],
        scratch_shapes=[pltpu.VMEM((LANES, LANES), jnp.float32),
                        pltpu.VMEM((ROW_TILE, SEQ), jnp.int32)],
        compiler_params=pltpu.CompilerParams(
            dimension_semantics=("arbitrary",), vmem_limit_bytes=VMEM_LIMIT),
        name="route",
    )(h, mod, wrt, brt)


SRC_UNROLL = 16
EXP_BUFS = 3
EXP_CHUNK = 256
TILE_ROWS = EXP_TM * ROW_TILE


def _experts_kernel(te_ref, tv_ref, nact_ref, posa_ref, posb_ref,
                    h_hbm, mod_ref, wg_ref, wu_ref, wd_ref,
                    y_hbm,
                    src_ref, hres, xbuf, ybuf, wgb, wub, wdb, rsem, ssem):
    i = pl.program_id(0)
    last = pl.num_programs(0) - 1
    nact = nact_ref[0]
    cur = lax.rem(i, EXP_BUFS)
    nxt2 = lax.rem(i + 2, EXP_BUFS)
    prv2 = lax.rem(i + 1, EXP_BUFS)
    part = lax.shift_right_logical(te_ref[i], N_EXPERTS.bit_length() - 1)

    def scatter_copy(p, k, s):
        dst = src_ref[p]
        return pltpu.make_async_copy(ybuf.at[s, pl.ds(pl.multiple_of(k * ROW_TILE, ROW_TILE), ROW_TILE), :],
                                     y_hbm.at[pl.ds(pl.multiple_of(dst * ROW_TILE, ROW_TILE), ROW_TILE), :],
                                     ssem.at[s])

    def scatter_rolled(tile, s, n):
        def body(k, c):
            scatter_copy(tile * EXP_TM + k, k, s).start()
            return c
        lax.fori_loop(0, n, body, 0)

    def gather_rows(tile):
        base = tile * EXP_TM
        first = part * PART_TOKENS
        for k in range(EXP_TM):
            local = (src_ref[base + k] & (SEQ - 1)) - first
            xbuf[k * ROW_TILE:(k + 1) * ROW_TILE, :] = hres[pl.ds(pl.multiple_of(local * ROW_TILE, ROW_TILE),
                                                              ROW_TILE), :]

    def wait_scatter(s, n):
        rows = pl.multiple_of(n * ROW_TILE, ROW_TILE)
        pltpu.make_async_copy(ybuf.at[s, pl.ds(0, rows), :], y_hbm.at[pl.ds(0, rows), :], ssem.at[s]).wait()

    def compute_chunks(s):
        state = {}

        def load():
            x = _load_row_tiles(xbuf, EXP_TM)
            shift = mod_ref[0:1, 3 * D_MODEL:4 * D_MODEL]
            scale = mod_ref[0:1, 4 * D_MODEL:5 * D_MODEL]
            state["x"] = (x * (1.0 + scale) + shift).astype(jnp.bfloat16)
            state["act"] = []

        def gate(c):
            def run():
                if c == 0:
                    load()
                state["a"] = _bdot(state["x"], wgb[:, c * EXP_CHUNK:(c + 1) * EXP_CHUNK])
            return run

        def up(c):
            def run():
                a = state["a"]
                u = _bdot(state["x"], wub[:, c * EXP_CHUNK:(c + 1) * EXP_CHUNK])
                state["act"].append((a * jax.nn.sigmoid(a) * u).astype(jnp.bfloat16))
            return run

        def down(c):
            def run():
                if c == 0:
                    state["actf"] = jnp.concatenate(state["act"], axis=-1)
                yc = _bdot(state["actf"], wdb[:, c * EXP_CHUNK:(c + 1) * EXP_CHUNK])
                for jj in range(EXP_CHUNK // LANES):
                    j = c * (EXP_CHUNK // LANES) + jj
                    ybuf[s, pl.ds(j, EXP_TM, stride=ROW_TILE), :] = yc[:, jj * LANES:(jj + 1) * LANES]
            return run

        steps = []
        for c in range(D_EXPERT // EXP_CHUNK):
            steps += [(gate(c), 2), (up(c), 2)]
        return steps + [(down(c), 1) for c in range(D_MODEL // EXP_CHUNK)]

    def run_interleaved(chunks, dmas):
        total_cost = sum(cost for _, cost in chunks)
        done = 0
        for chunk, cost in chunks:
            upto = -(-len(dmas) * (done + cost) // total_cost)
            for d in dmas[-(-len(dmas) * done // total_cost):upto]:
                d()
            done += cost
            chunk()

    def scatter_dmas(tile, s):
        return [functools.partial(lambda k: scatter_copy(tile * EXP_TM + k, k, s).start(priority=k % 2), k)
                for k in range(EXP_TM)]

    @pl.when(i == 0)
    def _():
        def fill_body(tt, c):
            ts = [tt * SRC_UNROLL + u for u in range(SRC_UNROLL)]
            pa = [posa_ref[t] for t in ts]
            pb = [posb_ref[t] for t in ts]
            for t, a, b in zip(ts, pa, pb):
                src_ref[a] = t
                src_ref[b] = t + SEQ
            return c
        lax.fori_loop(0, SEQ // SRC_UNROLL, fill_body, 0)

        def pad_tile(t, c):
            pad_tok = lax.shift_right_logical(te_ref[t], N_EXPERTS.bit_length() - 1) * PART_TOKENS

            def pad_row(k, c2):
                src_ref[t * EXP_TM + k] = pad_tok
                return c2
            return lax.fori_loop(tv_ref[t], EXP_TM, pad_row, c)
        lax.fori_loop(0, nact, pad_tile, 0)

    active = i < nact
    prev = jnp.maximum(i - 1, 0)
    prev_ok = (i >= 1) & (i - 1 < nact)
    prev2 = jnp.maximum(i - 2, 0)
    prev2_ok = (i >= 2) & (i - 2 < nact)
    nprev = tv_ref[prev]
    full_prev = prev_ok & (nprev == EXP_TM)
    new_segment = (i == 0) | (te_ref[i] != te_ref[prev])

    @pl.when(prev2_ok)
    def _():
        wait_scatter(prv2, tv_ref[prev2])

    @pl.when(active & ((i == 0) | (part != lax.shift_right_logical(te_ref[prev], N_EXPERTS.bit_length() - 1))))
    def _():
        rows = PART_TOKENS * ROW_TILE
        cp = pltpu.make_async_copy(h_hbm.at[pl.ds(pl.multiple_of(part * rows, rows), rows), :], hres, rsem.at[0])
        cp.start()
        cp.wait()

    @pl.when(active & new_segment)
    def _():
        wgb[...] = wg_ref[0].astype(jnp.bfloat16)
        wub[...] = wu_ref[0].astype(jnp.bfloat16)
        wdb[...] = wd_ref[0].astype(jnp.bfloat16)

    @pl.when(prev_ok & jnp.logical_not(active & full_prev))
    def _():
        scatter_rolled(prev, nxt2, nprev)

    @pl.when(active)
    def _():
        gather_rows(i)

    @pl.when(active & full_prev)
    def _():
        run_interleaved(compute_chunks(cur), scatter_dmas(prev, nxt2))

    @pl.when(active & jnp.logical_not(full_prev))
    def _():
        run_interleaved(compute_chunks(cur), [])

    @pl.when(i == last)
    def _():
        @pl.when(prev_ok)
        def _():
            wait_scatter(nxt2, nprev)

        @pl.when(active)
        def _():
            scatter_rolled(i, cur, tv_ref[i])
            wait_scatter(cur, tv_ref[i])


def _experts_call(te, tv, nact, posa, posb, h, mod, wg, wu, wd):
    idx = lambda i, te_ref, *_: (te_ref[i] & (N_EXPERTS - 1), 0, 0)
    grid_spec = pltpu.PrefetchScalarGridSpec(
        num_scalar_prefetch=5,
        grid=(EXP_TILES,),
        in_specs=[
            pl.BlockSpec(memory_space=pl.ANY),
            pl.BlockSpec(mod.shape, lambda i, *_: (0, 0)),
            pl.BlockSpec((1, D_MODEL, D_EXPERT), idx),
            pl.BlockSpec((1, D_MODEL, D_EXPERT), idx),
            pl.BlockSpec((1, D_EXPERT, D_MODEL), idx),
        ],
        out_specs=pl.BlockSpec(memory_space=pl.ANY),
        scratch_shapes=[
            pltpu.SMEM((EXP_TILES * EXP_TM,), jnp.int32),
            pltpu.VMEM((PART_TOKENS * ROW_TILE, LANES), jnp.float32),
            pltpu.VMEM((TILE_ROWS, LANES), jnp.float32),
            pltpu.VMEM((EXP_BUFS, TILE_ROWS, LANES), jnp.float32),
            pltpu.VMEM((D_MODEL, D_EXPERT), jnp.bfloat16),
            pltpu.VMEM((D_MODEL, D_EXPERT), jnp.bfloat16),
            pltpu.VMEM((D_EXPERT, D_MODEL), jnp.bfloat16),
            pltpu.SemaphoreType.DMA((1,)),
            pltpu.SemaphoreType.DMA((EXP_BUFS,)),
        ],
    )
    return pl.pallas_call(
        _experts_kernel,
        grid_spec=grid_spec,
        out_shape=jax.ShapeDtypeStruct((2 * SEQ * ROW_TILE, LANES), jnp.float32),
        compiler_params=pltpu.CompilerParams(
            dimension_semantics=("arbitrary",), vmem_limit_bytes=EXPERTS_VMEM_LIMIT),
        name="experts",
    )(te, tv, nact, posa, posb, h, mod, wg, wu, wd)


def _combine_kernel(h_ref, y0_ref, y1_ref, cw_ref, mod_ref, g_ref, b_ref, o_ref):
    tm = o_ref.shape[0]
    c0 = cw_ref[:, CW_C0:CW_C0 + 1]
    c1 = cw_ref[:, CW_C1:CW_C1 + 1]
    ffn = c0 * _load_row_tiles(y0_ref, tm) + c1 * _load_row_tiles(y1_ref, tm)
    g2 = mod_ref[0:1, 5 * D_MODEL:6 * D_MODEL]
    o_ref[...] = _layer_norm(DEEPNORM_ALPHA * _load_row_tiles(h_ref, tm) + g2 * ffn, g_ref[...], b_ref[...])


def _combine_call(h, y, cw, mod, g, b):
    tm = CMB_TM
    nblk = SEQ // tm
    tiles = lambda off: pl.BlockSpec((tm * ROW_TILE, LANES), lambda i: (i + off, 0))
    return pl.pallas_call(
        _combine_kernel,
        grid=(nblk,),
        in_specs=[
            tiles(0), tiles(0), tiles(nblk),
            pl.BlockSpec((tm, LANES), lambda i: (i, 0)),
            pl.BlockSpec(mod.shape, lambda i: (0, 0)),
            pl.BlockSpec((1, D_MODEL), lambda i: (0, 0)),
            pl.BlockSpec((1, D_MODEL), lambda i: (0, 0)),
        ],
        out_specs=pl.BlockSpec((tm, D_MODEL), lambda i: (i, 0)),
        out_shape=jax.ShapeDtypeStruct((SEQ, D_MODEL), jnp.float32),
        compiler_params=pltpu.CompilerParams(
            dimension_semantics=("arbitrary",), vmem_limit_bytes=VMEM_LIMIT),
        name="combine",
    )(h, y, y, cw, mod, g, b)


def kernel(x, c, ctx, c_ctx, ln_in_g, ln_in_b, w_mod, b_mod, w_in, rpb, w_pool_grp, pool_scale,
           w_attn_proj, w_pool_proj, w_out, ln1_g, ln1_b, w_router_group, b_router_group,
           w_router_expert, b_router_expert, w_expert_gate, w_expert_up, w_expert_down, ln2_g, ln2_b):
    assert x.shape == (1, SEQ, D_MODEL) and ctx.shape == (1, CTX_LEN, D_MODEL)
    assert w_mod.shape[0] == 1, "single-layer trunk"
    f32, bf16 = jnp.float32, jnp.bfloat16
    row = lambda v: v.reshape(1, -1).astype(f32)

    cond = jnp.concatenate([c, c_ctx[None], jnp.zeros((MOD_ROWS - 2, D_MODEL), f32)], axis=0)
    mod = _mod_call(cond, w_mod[0], row(b_mod[0]))

    lng, lnb = row(ln_in_g), row(ln_in_b)
    w_in_b = w_in[0].astype(bf16)
    u, h0 = _proj_call(x[0], mod, lng, lnb, w_in_b, mod_row=0, latent=True, tm=PROJ_TM)
    kvc, = _proj_call(ctx[0], mod, lng, lnb, w_in_b[:, ATT_W:3 * ATT_W], mod_row=1, latent=False, tm=CTX_LEN)

    h1 = _mix_call(h0, mod, u, kvc, _attn_bias_table(rpb[0]),
                   w_pool_grp[0].astype(bf16), row(pool_scale[0]),
                   w_attn_proj[0].astype(bf16), w_pool_proj[0].astype(bf16), w_out[0].astype(bf16),
                   row(ln1_g[0]), row(ln1_b[0]))

    n_logit = N_GROUPS + N_EXPERTS
    wrt = jnp.concatenate([w_router_group[0].T, w_router_expert[0].T,
                           jnp.zeros((LANES - n_logit, D_MODEL), f32)], axis=0)
    brt = jnp.concatenate([b_router_group[0], b_router_expert[0], jnp.zeros((LANES - n_logit,), f32)])
    brt = jnp.broadcast_to(brt[:, None], (LANES, LANES))
    pos, cw, plan = _route_call(h1, mod, wrt, brt)

    y = _experts_call(plan[PLAN_EXPERT, :EXP_TILES], plan[PLAN_VALID, :EXP_TILES], plan[PLAN_NACT, :1],
                      pos[POS_A], pos[POS_B], h1, mod,
                      w_expert_gate[0], w_expert_up[0], w_expert_down[0])
    out = _combine_call(h1, y, cw, mod, row(ln2_g[0]), row(ln2_b[0]))
    return out[None]
```

```python
import functools

import jax
import jax.numpy as jnp
from jax import lax
from jax.experimental import pallas as pl
from jax.experimental.pallas import tpu as pltpu

D_MODEL = 1024
SEQ = 16384
GRID_W = 64
ROWS = SEQ // GRID_W
CTX_LEN = 256
N_HEADS = 8
HEAD_DIM = 64
ATT_W = N_HEADS * HEAD_DIM
WIN_H = 8
WIN_W = 16
POOL_WINDOWS = (2, 4, 8, 16)
POOL_GROUPS = 4
POOL_DIM = 128
POOL_W = POOL_GROUPS * POOL_DIM
PROJ_W = 3 * ATT_W + POOL_W + 2 * D_MODEL
GATE_COL = 3 * ATT_W + POOL_W
N_GROUPS = 4
EXPERTS_PER_GROUP = 8
N_EXPERTS = N_GROUPS * EXPERTS_PER_GROUP
D_EXPERT = 512
N_MOD = 6
DEEPNORM_ALPHA = 2.0 ** 0.25
LN_EPS = 1e-5
NEG_INF = -1e30

LANES = 128
ROW_TILE = 8
MOD_ROWS = 8
PROJ_TM = 512
MIX_ROWS = 8
MIX_TQ = MIX_ROWS * GRID_W
KV_HALO = 4 * GRID_W
POOL_HALO = 16
ROUTE_TM = 512
EXP_TM = 256
SEQ_PARTS = 2
PART_TOKENS = SEQ // SEQ_PARTS
N_SEG = SEQ_PARTS * N_EXPERTS
EXP_TILES = 2 * SEQ // EXP_TM + N_SEG
CMB_TM = 256
HALF = D_MODEL // 2
VMEM_LIMIT = 56 * 1024 * 1024
EXPERTS_VMEM_LIMIT = 60 * 1024 * 1024


def _layer_norm(x, g, b):
    mu = jnp.mean(x, axis=-1, keepdims=True)
    xc = x - mu
    var = jnp.mean(xc * xc, axis=-1, keepdims=True)
    return xc * lax.rsqrt(var + LN_EPS) * g + b


def _bdot(a, b):
    return jnp.dot(a, b, preferred_element_type=jnp.float32)


def _split_bf16(a):
    hi = a.astype(jnp.bfloat16)
    lo = (a - hi.astype(jnp.float32)).astype(jnp.bfloat16)
    return hi, lo


def _dot3(a, b):
    a_hi, a_lo = _split_bf16(a)
    b_hi, b_lo = _split_bf16(b)
    return _bdot(a_hi, b_hi) + (_bdot(a_hi, b_lo) + _bdot(a_lo, b_hi))


def _load_row_tiles(ref, tokens, lead=()):
    parts = [ref[(*lead, pl.ds(j, tokens, stride=ROW_TILE), slice(None))] for j in range(ROW_TILE)]
    return jnp.concatenate(parts, axis=-1)


def _store_row_tiles(ref, value, lead=()):
    tokens = value.shape[0]
    for j in range(ROW_TILE):
        ref[(*lead, pl.ds(j, tokens, stride=ROW_TILE), slice(None))] = value[:, j * LANES:(j + 1) * LANES]


def _mod_kernel(cond_ref, w_ref, b_ref, o_ref):
    cond = cond_ref[...]
    act = cond * jax.nn.sigmoid(cond)
    o_ref[...] = _dot3(act, w_ref[...]) + b_ref[...]


def _mod_call(cond, w_mod, b_mod):
    tn = 1536
    n = N_MOD * D_MODEL
    return pl.pallas_call(
        _mod_kernel,
        grid=(n // tn,),
        in_specs=[
            pl.BlockSpec((MOD_ROWS, D_MODEL), lambda i: (0, 0)),
            pl.BlockSpec((D_MODEL, tn), lambda i: (0, i)),
            pl.BlockSpec((1, tn), lambda i: (0, i)),
        ],
        out_specs=pl.BlockSpec((MOD_ROWS, tn), lambda i: (0, i)),
        out_shape=jax.ShapeDtypeStruct((MOD_ROWS, n), jnp.float32),
        compiler_params=pltpu.CompilerParams(
            dimension_semantics=("arbitrary",), vmem_limit_bytes=VMEM_LIMIT),
        name="mod",
    )(cond, w_mod, b_mod)


def _proj_kernel(x_ref, mod_ref, g_ref, b_ref, w_ref, o_ref, *h_out, mod_row, latent):
    h = _layer_norm(x_ref[...], g_ref[...], b_ref[...])
    if latent:
        h_out[0][...] = h
    shift = mod_ref[mod_row:mod_row + 1, 0:D_MODEL]
    scale = mod_ref[mod_row:mod_row + 1, D_MODEL:2 * D_MODEL]
    hm = (h * (1.0 + scale) + shift).astype(jnp.bfloat16)
    n = o_ref.shape[1]
    for c in range(n // D_MODEL):
        sl = slice(c * D_MODEL, (c + 1) * D_MODEL)
        res = _bdot(hm, w_ref[:, sl])
        if latent and c == 0:
            lane = lax.broadcasted_iota(jnp.int32, (1, D_MODEL), 1)
            res = res * jnp.where(lane < ATT_W, HEAD_DIM ** -0.5, 1.0)
        if latent and c * D_MODEL >= GATE_COL:
            res = jax.nn.sigmoid(res)
        o_ref[:, sl] = res.astype(jnp.bfloat16)


def _proj_call(x, mod, g, b, w, *, mod_row, latent, tm):
    rows, n = x.shape[0], w.shape[1]
    out_specs = [pl.BlockSpec((tm, n), lambda i: (i, 0))]
    out_shape = [jax.ShapeDtypeStruct((rows, n), jnp.bfloat16)]
    if latent:
        out_specs.append(pl.BlockSpec((tm, D_MODEL), lambda i: (i, 0)))
        out_shape.append(jax.ShapeDtypeStruct((rows, D_MODEL), jnp.float32))
    return pl.pallas_call(
        functools.partial(_proj_kernel, mod_row=mod_row, latent=latent),
        grid=(rows // tm,),
        in_specs=[
            pl.BlockSpec((tm, D_MODEL), lambda i: (i, 0)),
            pl.BlockSpec(mod.shape, lambda i: (0, 0)),
            pl.BlockSpec((1, D_MODEL), lambda i: (0, 0)),
            pl.BlockSpec((1, D_MODEL), lambda i: (0, 0)),
            pl.BlockSpec((D_MODEL, n), lambda i: (0, 0), pipeline_mode=pl.Buffered(1)),
        ],
        out_specs=out_specs,
        out_shape=out_shape,
        compiler_params=pltpu.CompilerParams(
            dimension_semantics=("arbitrary",), vmem_limit_bytes=VMEM_LIMIT),
        name="proj",
    )(x, mod, g, b, w)


def _attn_bias_table(rpb):
    col = jnp.arange(GRID_W, dtype=jnp.int32)
    col_start = jnp.clip(col - WIN_W // 2, 0, GRID_W - WIN_W)
    col_mask = (col[None, :] >= col_start[:, None]) & (col[None, :] < col_start[:, None] + WIN_W)
    col_off = jnp.clip(col[None, :] - col[:, None], 1 - WIN_W, WIN_W - 1) + (WIN_W - 1)
    onehot = (col_off[None] == jnp.arange(2 * WIN_W - 1, dtype=jnp.int32)[:, None, None]).astype(jnp.float32)
    tab = jnp.einsum("hrc,cqk->hrqk", rpb.astype(jnp.float32), onehot, precision=lax.Precision.HIGHEST)
    tab = jnp.where(col_mask[None, None], tab, NEG_INF)
    tab = jnp.stack([tab[:, WIN_H - 1 - v:2 * WIN_H - 1 - v] for v in range(WIN_H)], axis=0)
    tab = tab.transpose(0, 1, 3, 2, 4)
    return tab.reshape(WIN_H, N_HEADS // 2, 2 * GRID_W, WIN_H * GRID_W)


def _mix_kernel(h_ref, mod_ref,
                q_ref, kp_ref, kc_ref, kn_ref, vp_ref, vc_ref, vn_ref,
                pp_ref, pc_ref, pn_ref, ga_ref, gb_ref,
                kvc_ref, bias_ref, wgrp_ref, pscale_ref, wap_ref, wpp_ref, wout_ref,
                ln1g_ref, ln1b_ref,
                o_ref,
                kbuf, vbuf, yabuf, pbuf, ypbuf):
    b = pl.program_id(0)
    nb = pl.num_programs(0)

    kbuf[0:KV_HALO, :] = kp_ref[...]
    kbuf[KV_HALO:KV_HALO + MIX_TQ, :] = kc_ref[...]
    kbuf[KV_HALO + MIX_TQ:, :] = kn_ref[...]
    vbuf[0:KV_HALO, :] = vp_ref[...]
    vbuf[KV_HALO:KV_HALO + MIX_TQ, :] = vc_ref[...]
    vbuf[KV_HALO + MIX_TQ:, :] = vn_ref[...]

    lane = lax.broadcasted_iota(jnp.int32, (GRID_W, LANES), 1)
    first_head = lane < HEAD_DIM

    def row_body(j):
        r = b * MIX_ROWS + j
        rs = jnp.clip(r - WIN_H // 2, 0, ROWS - WIN_H)
        off = pl.multiple_of((rs - b * MIX_ROWS + WIN_H // 2) * GRID_W, GRID_W)
        var = r - rs
        qoff = j * GRID_W
        for pair in range(N_HEADS // 2):
            cols = slice(pair * LANES, (pair + 1) * LANES)
            q = q_ref[pl.ds(qoff, GRID_W), cols]
            zero = jnp.zeros_like(q)
            q2 = jnp.concatenate([jnp.where(first_head, q, zero), jnp.where(first_head, zero, q)], axis=0)
            kw = kbuf[pl.ds(off, WIN_H * GRID_W), cols]
            vw = vbuf[pl.ds(off, WIN_H * GRID_W), cols]
            kctx = kvc_ref[:, cols]
            vctx = kvc_ref[:, ATT_W + pair * LANES:ATT_W + (pair + 1) * LANES]
            nt = (((1,), (1,)), ((), ()))
            s_loc = lax.dot_general(q2, kw, nt, preferred_element_type=jnp.float32) + bias_ref[var, pair]
            s_ctx = lax.dot_general(q2, kctx, nt, preferred_element_type=jnp.float32)
            m = jnp.maximum(jnp.max(s_loc, axis=-1, keepdims=True), jnp.max(s_ctx, axis=-1, keepdims=True))
            p_loc = jnp.exp(s_loc - m)
            p_ctx = jnp.exp(s_ctx - m)
            denom = jnp.sum(p_loc, axis=-1, keepdims=True) + jnp.sum(p_ctx, axis=-1, keepdims=True)
            o2 = _bdot(p_loc.astype(jnp.bfloat16), vw) + _bdot(p_ctx.astype(jnp.bfloat16), vctx)
            o2 = o2 * (1.0 / denom)
            o_pair = jnp.where(first_head, o2[:GRID_W], o2[GRID_W:])
            yabuf[pl.ds(qoff, GRID_W), cols] = o_pair.astype(jnp.bfloat16)

    for j in range(MIX_ROWS):
        row_body(j)

    pbuf[0:POOL_HALO, :] = jnp.where(b > 0, pp_ref[...].astype(jnp.float32), 0.0)
    pbuf[POOL_HALO:POOL_HALO + MIX_TQ, :] = pc_ref[...].astype(jnp.float32)
    pbuf[POOL_HALO + MIX_TQ:, :] = jnp.where(b < nb - 1, pn_ref[...].astype(jnp.float32), 0.0)
    t_abs = b * MIX_TQ + lax.broadcasted_iota(jnp.int32, (MIX_TQ, 1), 0)
    for g, win in enumerate(POOL_WINDOWS):
        cols = slice(g * POOL_DIM, (g + 1) * POOL_DIM)
        acc = None
        for d in range(-(win // 2), win - win // 2):
            term = pbuf[POOL_HALO + d:POOL_HALO + d + MIX_TQ, cols]
            acc = term if acc is None else acc + term
        count = jnp.minimum(t_abs + (win - win // 2), SEQ) - jnp.maximum(t_abs - win // 2, 0)
        pooled = acc / count.astype(jnp.float32) - pbuf[POOL_HALO:POOL_HALO + MIX_TQ, cols]
        yp = _bdot(pooled.astype(jnp.bfloat16), wgrp_ref[g]) * pscale_ref[:, cols]
        ypbuf[:, cols] = yp.astype(jnp.bfloat16)

    ya = _bdot(yabuf[...], wap_ref[...])
    yp = _bdot(ypbuf[...], wpp_ref[...])
    z = ga_ref[...].astype(jnp.float32) * ya + gb_ref[...].astype(jnp.float32) * yp
    y = _bdot(z.astype(jnp.bfloat16), wout_ref[...])
    g1 = mod_ref[0:1, 2 * D_MODEL:3 * D_MODEL]
    _store_row_tiles(o_ref, _layer_norm(DEEPNORM_ALPHA * h_ref[...] + g1 * y, ln1g_ref[...], ln1b_ref[...]))


def _mix_call(h, mod, u, kvc, bias, wgrp, pscale, wap, wpp, wout, ln1g, ln1b):
    nb = SEQ // MIX_TQ
    halo_per_blk = MIX_TQ // KV_HALO
    n_halo = SEQ // KV_HALO
    ph_per_blk = MIX_TQ // POOL_HALO
    n_ph = SEQ // POOL_HALO

    def const(shape):
        return pl.BlockSpec(shape, lambda i: (0,) * len(shape), pipeline_mode=pl.Buffered(1))

    def prev_halo(c):
        return pl.BlockSpec((KV_HALO, ATT_W), lambda i: (jnp.maximum(i * halo_per_blk - 1, 0), c))

    def next_halo(c):
        return pl.BlockSpec((KV_HALO, ATT_W), lambda i: (jnp.minimum((i + 1) * halo_per_blk, n_halo - 1), c))

    def cur(c):
        return pl.BlockSpec((MIX_TQ, ATT_W), lambda i: (i, c))

    in_specs = [
        pl.BlockSpec((MIX_TQ, D_MODEL), lambda i: (i, 0)),
        const(mod.shape),
        cur(0),
        prev_halo(1), cur(1), next_halo(1),
        prev_halo(2), cur(2), next_halo(2),
        pl.BlockSpec((POOL_HALO, POOL_W), lambda i: (jnp.maximum(i * ph_per_blk - 1, 0), 3)),
        cur(3),
        pl.BlockSpec((POOL_HALO, POOL_W), lambda i: (jnp.minimum((i + 1) * ph_per_blk, n_ph - 1), 3)),
        pl.BlockSpec((MIX_TQ, D_MODEL), lambda i: (i, 2)),
        pl.BlockSpec((MIX_TQ, D_MODEL), lambda i: (i, 3)),
        const(kvc.shape), const(bias.shape), const(wgrp.shape), const(pscale.shape),
        const(wap.shape), const(wpp.shape), const(wout.shape),
        const((1, D_MODEL)), const((1, D_MODEL)),
    ]
    return pl.pallas_call(
        _mix_kernel,
        grid=(nb,),
        in_specs=in_specs,
        out_specs=pl.BlockSpec((MIX_TQ * ROW_TILE, LANES), lambda i: (i, 0)),
        out_shape=jax.ShapeDtypeStruct((SEQ * ROW_TILE, LANES), jnp.float32),
        scratch_shapes=[
            pltpu.VMEM((MIX_TQ + 2 * KV_HALO, ATT_W), jnp.bfloat16),
            pltpu.VMEM((MIX_TQ + 2 * KV_HALO, ATT_W), jnp.bfloat16),
            pltpu.VMEM((MIX_TQ, ATT_W), jnp.bfloat16),
            pltpu.VMEM((MIX_TQ + 2 * POOL_HALO, POOL_W), jnp.float32),
            pltpu.VMEM((MIX_TQ, POOL_W), jnp.bfloat16),
        ],
        compiler_params=pltpu.CompilerParams(
            dimension_semantics=("arbitrary",), vmem_limit_bytes=VMEM_LIMIT),
        name="mix",
    )(h, mod, u, u, u, u, u, u, u, u, u, u, u, u,
      kvc, bias, wgrp, pscale, wap, wpp, wout, ln1g, ln1b)


ID_E0, ID_E1, ID_R0, ID_R1 = 0, 1, 4, 5
POS_PACKED = 0
POS_BITS = 16
PLAN_EXPERT, PLAN_VALID, PLAN_NACT, PLAN_NEXT = 0, 1, 2, 3
PLAN_W = 2 * LANES
CW_C0, CW_C1 = 0, 1


def _route_kernel(h_ref, mod_ref, wrt_ref, brt_ref, pos_ref, cw_ref, plan_ref, carry_ref, ids_all):
    i = pl.program_id(0)
    tm = ROUTE_TM

    @pl.when(i == 0)
    def _():
        carry_ref[...] = jnp.zeros_like(carry_ref)

    shift = mod_ref[0:1, 3 * D_MODEL:4 * D_MODEL]
    scale = mod_ref[0:1, 4 * D_MODEL:5 * D_MODEL]
    hm = _load_row_tiles(h_ref, tm) * (1.0 + scale) + shift

    hm_hi, hm_lo = _split_bf16(hm)
    w_hi, w_lo = _split_bf16(wrt_ref[...])
    nt = (((1,), (1,)), ((), ()))
    dg = functools.partial(lax.dot_general, dimension_numbers=nt, preferred_element_type=jnp.float32)
    logits = dg(w_hi, hm_hi) + (dg(w_hi, hm_lo) + dg(w_lo, hm_hi)) + brt_ref[:, 0:1]

    sub = lax.broadcasted_iota(jnp.int32, (LANES, tm), 0)
    big = jnp.int32(1 << 20)
    is_grp = sub < N_GROUPS
    gl = jnp.where(is_grp, logits, -jnp.inf)
    gmax = jnp.max(gl, axis=0, keepdims=True)
    gidx = jnp.min(jnp.where(gl == gmax, sub, big), axis=0, keepdims=True)
    gsum = jnp.sum(jnp.where(is_grp, jnp.exp(logits - gmax), 0.0), axis=0, keepdims=True)
    p_group = 1.0 / gsum

    eid = sub - N_GROUPS
    sel = (eid >= 0) & (eid < N_EXPERTS) & (lax.shift_right_arithmetic(eid, 3) == gidx)
    el = jnp.where(sel, logits, -jnp.inf)
    l0 = jnp.max(el, axis=0, keepdims=True)
    i0 = jnp.min(jnp.where(el == l0, sub, big), axis=0, keepdims=True)
    el2 = jnp.where(sub == i0, -jnp.inf, el)
    l1 = jnp.max(el2, axis=0, keepdims=True)
    i1 = jnp.min(jnp.where(el2 == l1, sub, big), axis=0, keepdims=True)
    t = jnp.exp(l1 - l0)
    w0 = 1.0 / (1.0 + t)
    w1 = t / (1.0 + t)

    half_rows = jnp.where(i >= pl.num_programs(0) // SEQ_PARTS, N_EXPERTS, 0)
    i0 = i0 + half_rows
    i1 = i1 + half_rows
    onehot = jnp.where((sub == i0) | (sub == i1), 1.0, 0.0)
    rr = lax.broadcasted_iota(jnp.int32, (tm, tm), 0)
    cc = lax.broadcasted_iota(jnp.int32, (tm, tm), 1)
    earlier = jnp.where(rr < cc, 1.0, 0.0).astype(jnp.bfloat16)
    carry = carry_ref[:, 0:1]
    prefix = _bdot(onehot.astype(jnp.bfloat16), earlier) + carry
    r0 = jnp.sum(jnp.where(sub == i0, prefix, 0.0), axis=0, keepdims=True)
    r1 = jnp.sum(jnp.where(sub == i1, prefix, 0.0), axis=0, keepdims=True)
    total = jnp.broadcast_to(carry + jnp.sum(onehot, axis=1, keepdims=True), carry_ref.shape)
    carry_ref[...] = total

    sub8 = lax.broadcasted_iota(jnp.int32, (ROW_TILE, tm), 0)
    ids = jnp.zeros((ROW_TILE, tm), jnp.int32)
    for idx, val in ((ID_E0, i0 - N_GROUPS), (ID_E1, i1 - N_GROUPS),
                     (ID_R0, r0.astype(jnp.int32)), (ID_R1, r1.astype(jnp.int32))):
        ids = jnp.where(sub8 == idx, val, ids)
    ids_all[:, pl.ds(pl.multiple_of(i * tm, tm), tm)] = ids

    cwt = jnp.where(sub == CW_C0, p_group * w0, jnp.where(sub == CW_C1, p_group * w1, 0.0))
    cw_ref[...] = cwt.T

    @pl.when(i == pl.num_programs(0) - 1)
    def _():
        subq = lax.broadcasted_iota(jnp.int32, (LANES, LANES), 0)
        laneq = lax.broadcasted_iota(jnp.int32, (LANES, LANES), 1)
        cnt = total.astype(jnp.int32)
        tiles = lax.shift_right_logical(cnt + (EXP_TM - 1), EXP_TM.bit_length() - 1).astype(jnp.float32)
        incl = jnp.where(laneq <= subq, 1.0, 0.0).astype(jnp.bfloat16)
        tile_end = _bdot(incl, tiles.astype(jnp.bfloat16))
        tile_start = tile_end - tiles
        seg = (tile_start * EXP_TM).astype(jnp.int32)
        nact = jnp.max(tile_end, axis=0, keepdims=True)

        ids_full = ids_all[...]
        look = jnp.zeros_like(ids_full)
        for e in range(N_SEG):
            look = jnp.where(ids_full == e, seg[N_GROUPS + e, 0], look)
        pos01 = look + pltpu.roll(ids_full, ID_R0 - ID_E0, axis=0)
        assert EXP_TILES * EXP_TM <= 1 << POS_BITS
        pos_ref[...] = pos01 | (pltpu.roll(pos01, ROW_TILE - 1, axis=0) << POS_BITS)

        subp = lax.broadcasted_iota(jnp.int32, (LANES, PLAN_W), 0)
        tile = lax.broadcasted_iota(jnp.int32, (LANES, PLAN_W), 1).astype(jnp.float32)
        is_exp = (subp >= N_GROUPS) & (subp < N_GROUPS + N_SEG)
        end_col = tile_end[:, 0:1]
        nact_s = nact[:, 0:1]
        te = jnp.sum(jnp.where(is_exp & (tile >= end_col), 1.0, 0.0), axis=0, keepdims=True)
        te_last = jnp.sum(jnp.where(is_exp & (nact_s - 1.0 >= end_col), 1.0, 0.0), axis=0, keepdims=True)[:, 0:1]
        tile_row = tile[0:1, :]
        te = jnp.minimum(jnp.where(tile_row < nact_s, te, te_last), N_SEG - 1.0)
        mine = (subp - N_GROUPS).astype(jnp.float32) == te
        cnt_sel = jnp.sum(jnp.where(mine, total[:, 0:1], 0.0), axis=0, keepdims=True)
        start_sel = jnp.sum(jnp.where(mine, tile_start[:, 0:1], 0.0), axis=0, keepdims=True)
        end_sel = jnp.sum(jnp.where(mine, end_col, 0.0), axis=0, keepdims=True)
        valid = jnp.clip(cnt_sel - (tile_row - start_sel) * EXP_TM, 0.0, float(EXP_TM))
        valid = jnp.where(tile_row < nact_s, valid, 0.0)
        subr = lax.broadcasted_iota(jnp.int32, (ROW_TILE, PLAN_W), 0)
        plan = jnp.where(subr == PLAN_EXPERT, te, jnp.where(subr == PLAN_VALID, valid,
                         jnp.where(subr == PLAN_NACT, nact_s, jnp.where(subr == PLAN_NEXT, end_sel, 0.0))))
        plan_ref[...] = plan.astype(jnp.int32)


def _route_call(h, mod, wrt, brt):
    tm = ROUTE_TM
    return pl.pallas_call(
        _route_kernel,
        grid=(SEQ // tm,),
        in_specs=[
            pl.BlockSpec((tm * ROW_TILE, LANES), lambda i: (i, 0)),
            pl.BlockSpec(mod.shape, lambda i: (0, 0)),
            pl.BlockSpec((LANES, D_MODEL), lambda i: (0, 0)),
            pl.BlockSpec((LANES, LANES), lambda i: (0, 0)),
        ],
        out_specs=[
            pl.BlockSpec((ROW_TILE, SEQ), lambda i: (0, 0)),
            pl.BlockSpec((tm, LANES), lambda i: (i, 0)),
            pl.BlockSpec((ROW_TILE, PLAN_W), lambda i: (0, 0)),
        ],
        out_shape=[
            jax.ShapeDtypeStruct((ROW_TILE, SEQ), jnp.int32),
            jax.ShapeDtypeStruct((SEQ, LANES), jnp.float32),
            jax.ShapeDtypeStruct((ROW_TILE, PLAN_W), jnp.int32),
        ],
        scratch_shapes=[pltpu.VMEM((LANES, LANES), jnp.float32),
                        pltpu.VMEM((ROW_TILE, SEQ), jnp.int32)],
        compiler_params=pltpu.CompilerParams(
            dimension_semantics=("arbitrary",), vmem_limit_bytes=VMEM_LIMIT),
        name="route",
    )(h, mod, wrt, brt)


SRC_UNROLL = 8
EXP_BUFS = 3
EXP_CHUNK = 256
TILE_ROWS = EXP_TM * ROW_TILE


def _experts_kernel(te_ref, tv_ref, nact_ref, tnext_ref, pos_ref,
                    h_hbm, mod_ref, wg_hbm, wu_hbm, wd_hbm,
                    y_hbm,
                    src_ref, hres, xbuf, ybuf, wgs, wus, wds, wgb, wub, wdb, rsem, ssem, wsem):
    i = pl.program_id(0)
    last = pl.num_programs(0) - 1
    nact = nact_ref[0]
    cur = lax.rem(i, EXP_BUFS)
    nxt2 = lax.rem(i + 2, EXP_BUFS)
    prv2 = lax.rem(i + 1, EXP_BUFS)
    part = lax.shift_right_logical(te_ref[i], N_EXPERTS.bit_length() - 1)

    def scatter_copy(p, k, s):
        dst = src_ref[p]
        return pltpu.make_async_copy(ybuf.at[s, pl.ds(pl.multiple_of(k * ROW_TILE, ROW_TILE), ROW_TILE), :],
                                     y_hbm.at[pl.ds(pl.multiple_of(dst * ROW_TILE, ROW_TILE), ROW_TILE), :],
                                     ssem.at[s])

    def scatter_rolled(tile, s, n):
        def body(k, c):
            scatter_copy(tile * EXP_TM + k, k, s).start()
            return c
        lax.fori_loop(0, n, body, 0)

    def weight_copies(segment):
        e = segment & (N_EXPERTS - 1)
        return [pltpu.make_async_copy(w_hbm.at[e], stage, wsem.at[n])
                for n, (w_hbm, stage) in enumerate(((wg_hbm, wgs), (wu_hbm, wus), (wd_hbm, wds)))]

    def gather_rows(tile):
        base = tile * EXP_TM
        first = part * PART_TOKENS
        for k in range(EXP_TM):
            local = (src_ref[base + k] & (SEQ - 1)) - first
            xbuf[k * ROW_TILE:(k + 1) * ROW_TILE, :] = hres[pl.ds(pl.multiple_of(local * ROW_TILE, ROW_TILE),
                                                              ROW_TILE), :]

    def wait_scatter(s, n):
        rows = pl.multiple_of(n * ROW_TILE, ROW_TILE)
        pltpu.make_async_copy(ybuf.at[s, pl.ds(0, rows), :], y_hbm.at[pl.ds(0, rows), :], ssem.at[s]).wait()

    def compute_chunks(s):
        state = {}

        def load():
            x = _load_row_tiles(xbuf, EXP_TM)
            shift = mod_ref[0:1, 3 * D_MODEL:4 * D_MODEL]
            scale = mod_ref[0:1, 4 * D_MODEL:5 * D_MODEL]
            state["x"] = (x * (1.0 + scale) + shift).astype(jnp.bfloat16)
            state["act"] = []

        def gate(c):
            def run():
                if c == 0:
                    load()
                state["a"] = _bdot(state["x"], wgb[:, c * EXP_CHUNK:(c + 1) * EXP_CHUNK])
            return run

        def up(c):
            def run():
                a = state["a"]
                u = _bdot(state["x"], wub[:, c * EXP_CHUNK:(c + 1) * EXP_CHUNK])
                state["act"].append((a * jax.nn.sigmoid(a) * u).astype(jnp.bfloat16))
            return run

        def down(c):
            def run():
                if c == 0:
                    state["actf"] = jnp.concatenate(state["act"], axis=-1)
                yc = _bdot(state["actf"], wdb[:, c * EXP_CHUNK:(c + 1) * EXP_CHUNK])
                for jj in range(EXP_CHUNK // LANES):
                    j = c * (EXP_CHUNK // LANES) + jj
                    ybuf[s, pl.ds(j, EXP_TM, stride=ROW_TILE), :] = yc[:, jj * LANES:(jj + 1) * LANES]
            return run

        steps = []
        for c in range(D_EXPERT // EXP_CHUNK):
            steps += [(gate(c), 2), (up(c), 2)]
        return steps + [(down(c), 1) for c in range(D_MODEL // EXP_CHUNK)]

    def run_interleaved(chunks, dmas):
        total_cost = sum(cost for _, cost in chunks)
        done = 0
        for chunk, cost in chunks:
            upto = -(-len(dmas) * (done + cost) // total_cost)
            for d in dmas[-(-len(dmas) * done // total_cost):upto]:
                d()
            done += cost
            chunk()

    def scatter_dmas(tile, s):
        return [functools.partial(lambda k: scatter_copy(tile * EXP_TM + k, k, s).start(priority=k % 2), k)
                for k in range(EXP_TM)]

    @pl.when(i == 0)
    def _():
        for cp in weight_copies(te_ref[0]):
            cp.start()

        def fill_body(tt, c):
            ts = [tt * SRC_UNROLL + u for u in range(SRC_UNROLL)]
            words = [pos_ref[t] for t in ts]
            for t, w in zip(ts, words):
                src_ref[w & ((1 << POS_BITS) - 1)] = t
                src_ref[lax.shift_right_logical(w, POS_BITS)] = t + SEQ
            return c
        lax.fori_loop(0, SEQ // SRC_UNROLL, fill_body, 0)

        def pad_tile(t, c):
            pad_tok = lax.shift_right_logical(te_ref[t], N_EXPERTS.bit_length() - 1) * PART_TOKENS

            def pad_row(k, c2):
                src_ref[t * EXP_TM + k] = pad_tok
                return c2
            return lax.fori_loop(tv_ref[t], EXP_TM, pad_row, c)
        lax.fori_loop(0, nact, pad_tile, 0)

    active = i < nact
    prev = jnp.maximum(i - 1, 0)
    prev_ok = (i >= 1) & (i - 1 < nact)
    prev2 = jnp.maximum(i - 2, 0)
    prev2_ok = (i >= 2) & (i - 2 < nact)
    nprev = tv_ref[prev]
    full_prev = prev_ok & (nprev == EXP_TM)
    new_segment = (i == 0) | (te_ref[i] != te_ref[prev])

    @pl.when(prev2_ok)
    def _():
        wait_scatter(prv2, tv_ref[prev2])

    @pl.when(active & ((i == 0) | (part != lax.shift_right_logical(te_ref[prev], N_EXPERTS.bit_length() - 1))))
    def _():
        rows = PART_TOKENS * ROW_TILE
        cp = pltpu.make_async_copy(h_hbm.at[pl.ds(pl.multiple_of(part * rows, rows), rows), :], hres, rsem.at[0])
        cp.start()
        cp.wait()

    @pl.when(active & new_segment)
    def _():
        for cp in weight_copies(te_ref[i]):
            cp.wait()
        wgb[...] = wgs[...].astype(jnp.bfloat16)
        wub[...] = wus[...].astype(jnp.bfloat16)
        wdb[...] = wds[...].astype(jnp.bfloat16)
        nxt = tnext_ref[i]

        @pl.when(nxt < nact)
        def _():
            for cp in weight_copies(te_ref[jnp.minimum(nxt, last)]):
                cp.start()

    @pl.when(prev_ok & jnp.logical_not(active & full_prev))
    def _():
        scatter_rolled(prev, nxt2, nprev)

    @pl.when(active)
    def _():
        gather_rows(i)

    @pl.when(active & full_prev)
    def _():
        run_interleaved(compute_chunks(cur), scatter_dmas(prev, nxt2))

    @pl.when(active & jnp.logical_not(full_prev))
    def _():
        run_interleaved(compute_chunks(cur), [])

    @pl.when(i == last)
    def _():
        @pl.when(prev_ok)
        def _():
            wait_scatter(nxt2, nprev)

        @pl.when(active)
        def _():
            scatter_rolled(i, cur, tv_ref[i])
            wait_scatter(cur, tv_ref[i])


def _experts_call(te, tv, nact, tnext, pos, h, mod, wg, wu, wd):
    grid_spec = pltpu.PrefetchScalarGridSpec(
        num_scalar_prefetch=5,
        grid=(EXP_TILES,),
        in_specs=[
            pl.BlockSpec(memory_space=pl.ANY),
            pl.BlockSpec(mod.shape, lambda i, *_: (0, 0)),
            pl.BlockSpec(memory_space=pl.ANY),
            pl.BlockSpec(memory_space=pl.ANY),
            pl.BlockSpec(memory_space=pl.ANY),
        ],
        out_specs=pl.BlockSpec(memory_space=pl.ANY),
        scratch_shapes=[
            pltpu.SMEM((EXP_TILES * EXP_TM,), jnp.int32),
            pltpu.VMEM((PART_TOKENS * ROW_TILE, LANES), jnp.float32),
            pltpu.VMEM((TILE_ROWS, LANES), jnp.float32),
            pltpu.VMEM((EXP_BUFS, TILE_ROWS, LANES), jnp.float32),
            pltpu.VMEM((D_MODEL, D_EXPERT), jnp.float32),
            pltpu.VMEM((D_MODEL, D_EXPERT), jnp.float32),
            pltpu.VMEM((D_EXPERT, D_MODEL), jnp.float32),
            pltpu.VMEM((D_MODEL, D_EXPERT), jnp.bfloat16),
            pltpu.VMEM((D_MODEL, D_EXPERT), jnp.bfloat16),
            pltpu.VMEM((D_EXPERT, D_MODEL), jnp.bfloat16),
            pltpu.SemaphoreType.DMA((1,)),
            pltpu.SemaphoreType.DMA((EXP_BUFS,)),
            pltpu.SemaphoreType.DMA((3,)),
        ],
    )
    return pl.pallas_call(
        _experts_kernel,
        grid_spec=grid_spec,
        out_shape=jax.ShapeDtypeStruct((2 * SEQ * ROW_TILE, LANES), jnp.float32),
        compiler_params=pltpu.CompilerParams(
            dimension_semantics=("arbitrary",), vmem_limit_bytes=EXPERTS_VMEM_LIMIT),
        name="experts",
    )(te, tv, nact, tnext, pos, h, mod, wg, wu, wd)


def _combine_kernel(h_ref, y0_ref, y1_ref, cw_ref, mod_ref, g_ref, b_ref, o_ref):
    tm = o_ref.shape[0]
    c0 = cw_ref[:, CW_C0:CW_C0 + 1]
    c1 = cw_ref[:, CW_C1:CW_C1 + 1]
    ffn = c0 * _load_row_tiles(y0_ref, tm) + c1 * _load_row_tiles(y1_ref, tm)
    g2 = mod_ref[0:1, 5 * D_MODEL:6 * D_MODEL]
    o_ref[...] = _layer_norm(DEEPNORM_ALPHA * _load_row_tiles(h_ref, tm) + g2 * ffn, g_ref[...], b_ref[...])


def _combine_call(h, y, cw, mod, g, b):
    tm = CMB_TM
    nblk = SEQ // tm
    tiles = lambda off: pl.BlockSpec((tm * ROW_TILE, LANES), lambda i: (i + off, 0))
    return pl.pallas_call(
        _combine_kernel,
        grid=(nblk,),
        in_specs=[
            tiles(0), tiles(0), tiles(nblk),
            pl.BlockSpec((tm, LANES), lambda i: (i, 0)),
            pl.BlockSpec(mod.shape, lambda i: (0, 0)),
            pl.BlockSpec((1, D_MODEL), lambda i: (0, 0)),
            pl.BlockSpec((1, D_MODEL), lambda i: (0, 0)),
        ],
        out_specs=pl.BlockSpec((tm, D_MODEL), lambda i: (i, 0)),
        out_shape=jax.ShapeDtypeStruct((SEQ, D_MODEL), jnp.float32),
        compiler_params=pltpu.CompilerParams(
            dimension_semantics=("arbitrary",), vmem_limit_bytes=VMEM_LIMIT),
        name="combine",
    )(h, y, y, cw, mod, g, b)


def kernel(x, c, ctx, c_ctx, ln_in_g, ln_in_b, w_mod, b_mod, w_in, rpb, w_pool_grp, pool_scale,
           w_attn_proj, w_pool_proj, w_out, ln1_g, ln1_b, w_router_group, b_router_group,
           w_router_expert, b_router_expert, w_expert_gate, w_expert_up, w_expert_down, ln2_g, ln2_b):
    assert x.shape == (1, SEQ, D_MODEL) and ctx.shape == (1, CTX_LEN, D_MODEL)
    assert w_mod.shape[0] == 1, "single-layer trunk"
    f32, bf16 = jnp.float32, jnp.bfloat16
    row = lambda v: v.reshape(1, -1).astype(f32)

    cond = jnp.concatenate([c, c_ctx[None], jnp.zeros((MOD_ROWS - 2, D_MODEL), f32)], axis=0)
    mod = _mod_call(cond, w_mod[0], row(b_mod[0]))

    lng, lnb = row(ln_in_g), row(ln_in_b)
    w_in_b = w_in[0].astype(bf16)
    u, h0 = _proj_call(x[0], mod, lng, lnb, w_in_b, mod_row=0, latent=True, tm=PROJ_TM)
    kvc, = _proj_call(ctx[0], mod, lng, lnb, w_in_b[:, ATT_W:3 * ATT_W], mod_row=1, latent=False, tm=CTX_LEN)

    h1 = _mix_call(h0, mod, u, kvc, _attn_bias_table(rpb[0]),
                   w_pool_grp[0].astype(bf16), row(pool_scale[0]),
                   w_attn_proj[0].astype(bf16), w_pool_proj[0].astype(bf16), w_out[0].astype(bf16),
                   row(ln1_g[0]), row(ln1_b[0]))

    n_logit = N_GROUPS + N_EXPERTS
    wrt = jnp.concatenate([w_router_group[0].T, w_router_expert[0].T,
                           jnp.zeros((LANES - n_logit, D_MODEL), f32)], axis=0)
    brt = jnp.concatenate([b_router_group[0], b_router_expert[0], jnp.zeros((LANES - n_logit,), f32)])
    brt = jnp.broadcast_to(brt[:, None], (LANES, LANES))
    pos, cw, plan = _route_call(h1, mod, wrt, brt)

    y = _experts_call(plan[PLAN_EXPERT, :EXP_TILES], plan[PLAN_VALID, :EXP_TILES], plan[PLAN_NACT, :1],
                      plan[PLAN_NEXT, :EXP_TILES], pos[POS_PACKED], h1, mod,
                      w_expert_gate[0], w_expert_up[0], w_expert_down[0])
    out = _combine_call(h1, y, cw, mod, row(ln2_g[0]), row(ln2_b[0]))
    return out[None]
```

```python
import functools

import jax
import jax.numpy as jnp
from jax import lax
from jax.experimental import pallas as pl
from jax.experimental.pallas import tpu as pltpu

D_MODEL = 1024
SEQ = 16384
GRID_W = 64
ROWS = SEQ // GRID_W
CTX_LEN = 256
N_HEADS = 8
HEAD_DIM = 64
ATT_W = N_HEADS * HEAD_DIM
WIN_H = 8
WIN_W = 16
POOL_WINDOWS = (2, 4, 8, 16)
POOL_GROUPS = 4
POOL_DIM = 128
POOL_W = POOL_GROUPS * POOL_DIM
PROJ_W = 3 * ATT_W + POOL_W + 2 * D_MODEL
GATE_COL = 3 * ATT_W + POOL_W
N_GROUPS = 4
EXPERTS_PER_GROUP = 8
N_EXPERTS = N_GROUPS * EXPERTS_PER_GROUP
D_EXPERT = 512
N_MOD = 6
DEEPNORM_ALPHA = 2.0 ** 0.25
LN_EPS = 1e-5
NEG_INF = -1e30

LANES = 128
ROW_TILE = 8
MOD_ROWS = 8
PROJ_TM = 512
MIX_ROWS = 8
MIX_TQ = MIX_ROWS * GRID_W
KV_HALO = 4 * GRID_W
POOL_HALO = 16
ROUTE_TM = 512
EXP_TM = 256
SEQ_PARTS = 2
PART_TOKENS = SEQ // SEQ_PARTS
N_SEG = SEQ_PARTS * N_EXPERTS
EXP_TILES = 2 * SEQ // EXP_TM + N_SEG
CMB_TM = 256
HALF = D_MODEL // 2
VMEM_LIMIT = 56 * 1024 * 1024
EXPERTS_VMEM_LIMIT = 60 * 1024 * 1024


def _layer_norm(x, g, b):
    mu = jnp.mean(x, axis=-1, keepdims=True)
    xc = x - mu
    var = jnp.mean(xc * xc, axis=-1, keepdims=True)
    return xc * lax.rsqrt(var + LN_EPS) * g + b


def _bdot(a, b):
    return jnp.dot(a, b, preferred_element_type=jnp.float32)


def _split_bf16(a):
    hi = a.astype(jnp.bfloat16)
    lo = (a - hi.astype(jnp.float32)).astype(jnp.bfloat16)
    return hi, lo


def _dot3(a, b):
    a_hi, a_lo = _split_bf16(a)
    b_hi, b_lo = _split_bf16(b)
    return _bdot(a_hi, b_hi) + (_bdot(a_hi, b_lo) + _bdot(a_lo, b_hi))


def _load_row_tiles(ref, tokens, lead=()):
    parts = [ref[(*lead, pl.ds(j, tokens, stride=ROW_TILE), slice(None))] for j in range(ROW_TILE)]
    return jnp.concatenate(parts, axis=-1)


def _store_row_tiles(ref, value, lead=()):
    tokens = value.shape[0]
    for j in range(ROW_TILE):
        ref[(*lead, pl.ds(j, tokens, stride=ROW_TILE), slice(None))] = value[:, j * LANES:(j + 1) * LANES]


def _mod_kernel(cond_ref, w_ref, b_ref, o_ref):
    cond = cond_ref[...]
    act = cond * jax.nn.sigmoid(cond)
    o_ref[...] = _dot3(act, w_ref[...]) + b_ref[...]


def _mod_call(cond, w_mod, b_mod):
    tn = 1536
    n = N_MOD * D_MODEL
    return pl.pallas_call(
        _mod_kernel,
        grid=(n // tn,),
        in_specs=[
            pl.BlockSpec((MOD_ROWS, D_MODEL), lambda i: (0, 0)),
            pl.BlockSpec((D_MODEL, tn), lambda i: (0, i)),
            pl.BlockSpec((1, tn), lambda i: (0, i)),
        ],
        out_specs=pl.BlockSpec((MOD_ROWS, tn), lambda i: (0, i)),
        out_shape=jax.ShapeDtypeStruct((MOD_ROWS, n), jnp.float32),
        compiler_params=pltpu.CompilerParams(
            dimension_semantics=("arbitrary",), vmem_limit_bytes=VMEM_LIMIT),
        name="mod",
    )(cond, w_mod, b_mod)


def _proj_kernel(x_ref, mod_ref, g_ref, b_ref, w_ref, o_ref, *h_out, mod_row, latent):
    h = _layer_norm(x_ref[...], g_ref[...], b_ref[...])
    if latent:
        h_out[0][...] = h
    shift = mod_ref[mod_row:mod_row + 1, 0:D_MODEL]
    scale = mod_ref[mod_row:mod_row + 1, D_MODEL:2 * D_MODEL]
    hm = (h * (1.0 + scale) + shift).astype(jnp.bfloat16)
    n = o_ref.shape[1]
    for c in range(n // D_MODEL):
        sl = slice(c * D_MODEL, (c + 1) * D_MODEL)
        res = _bdot(hm, w_ref[:, sl])
        if latent and c == 0:
            lane = lax.broadcasted_iota(jnp.int32, (1, D_MODEL), 1)
            res = res * jnp.where(lane < ATT_W, HEAD_DIM ** -0.5, 1.0)
        if latent and c * D_MODEL >= GATE_COL:
            res = jax.nn.sigmoid(res)
        o_ref[:, sl] = res.astype(jnp.bfloat16)


def _proj_call(x, mod, g, b, w, *, mod_row, latent, tm):
    rows, n = x.shape[0], w.shape[1]
    out_specs = [pl.BlockSpec((tm, n), lambda i: (i, 0))]
    out_shape = [jax.ShapeDtypeStruct((rows, n), jnp.bfloat16)]
    if latent:
        out_specs.append(pl.BlockSpec((tm, D_MODEL), lambda i: (i, 0)))
        out_shape.append(jax.ShapeDtypeStruct((rows, D_MODEL), jnp.float32))
    return pl.pallas_call(
        functools.partial(_proj_kernel, mod_row=mod_row, latent=latent),
        grid=(rows // tm,),
        in_specs=[
            pl.BlockSpec((tm, D_MODEL), lambda i: (i, 0)),
            pl.BlockSpec(mod.shape, lambda i: (0, 0)),
            pl.BlockSpec((1, D_MODEL), lambda i: (0, 0)),
            pl.BlockSpec((1, D_MODEL), lambda i: (0, 0)),
            pl.BlockSpec((D_MODEL, n), lambda i: (0, 0), pipeline_mode=pl.Buffered(1)),
        ],
        out_specs=out_specs,
        out_shape=out_shape,
        compiler_params=pltpu.CompilerParams(
            dimension_semantics=("arbitrary",), vmem_limit_bytes=VMEM_LIMIT),
        name="proj",
    )(x, mod, g, b, w)


def _attn_bias_table(rpb):
    col = jnp.arange(GRID_W, dtype=jnp.int32)
    col_start = jnp.clip(col - WIN_W // 2, 0, GRID_W - WIN_W)
    col_mask = (col[None, :] >= col_start[:, None]) & (col[None, :] < col_start[:, None] + WIN_W)
    col_off = jnp.clip(col[None, :] - col[:, None], 1 - WIN_W, WIN_W - 1) + (WIN_W - 1)
    onehot = (col_off[None] == jnp.arange(2 * WIN_W - 1, dtype=jnp.int32)[:, None, None]).astype(jnp.float32)
    tab = jnp.einsum("hrc,cqk->hrqk", rpb.astype(jnp.float32), onehot, precision=lax.Precision.HIGHEST)
    tab = jnp.where(col_mask[None, None], tab, NEG_INF)
    tab = jnp.stack([tab[:, WIN_H - 1 - v:2 * WIN_H - 1 - v] for v in range(WIN_H)], axis=0)
    tab = tab.transpose(0, 1, 3, 2, 4)
    return tab.reshape(WIN_H, N_HEADS // 2, 2 * GRID_W, WIN_H * GRID_W)


def _mix_kernel(h_ref, mod_ref,
                q_ref, kp_ref, kc_ref, kn_ref, vp_ref, vc_ref, vn_ref,
                pp_ref, pc_ref, pn_ref, ga_ref, gb_ref,
                kvc_ref, bias_ref, wgrp_ref, pscale_ref, wap_ref, wpp_ref, wout_ref,
                ln1g_ref, ln1b_ref,
                o_ref,
                kbuf, vbuf, yabuf, pbuf, ypbuf):
    b = pl.program_id(0)
    nb = pl.num_programs(0)

    kbuf[0:KV_HALO, :] = kp_ref[...]
    kbuf[KV_HALO:KV_HALO + MIX_TQ, :] = kc_ref[...]
    kbuf[KV_HALO + MIX_TQ:, :] = kn_ref[...]
    vbuf[0:KV_HALO, :] = vp_ref[...]
    vbuf[KV_HALO:KV_HALO + MIX_TQ, :] = vc_ref[...]
    vbuf[KV_HALO + MIX_TQ:, :] = vn_ref[...]

    lane = lax.broadcasted_iota(jnp.int32, (GRID_W, LANES), 1)
    first_head = lane < HEAD_DIM

    def row_body(j):
        r = b * MIX_ROWS + j
        rs = jnp.clip(r - WIN_H // 2, 0, ROWS - WIN_H)
        off = pl.multiple_of((rs - b * MIX_ROWS + WIN_H // 2) * GRID_W, GRID_W)
        var = r - rs
        qoff = j * GRID_W
        for pair in range(N_HEADS // 2):
            cols = slice(pair * LANES, (pair + 1) * LANES)
            q = q_ref[pl.ds(qoff, GRID_W), cols]
            zero = jnp.zeros_like(q)
            q2 = jnp.concatenate([jnp.where(first_head, q, zero), jnp.where(first_head, zero, q)], axis=0)
            kw = kbuf[pl.ds(off, WIN_H * GRID_W), cols]
            vw = vbuf[pl.ds(off, WIN_H * GRID_W), cols]
            kctx = kvc_ref[:, cols]
            vctx = kvc_ref[:, ATT_W + pair * LANES:ATT_W + (pair + 1) * LANES]
            nt = (((1,), (1,)), ((), ()))
            s_loc = lax.dot_general(q2, kw, nt, preferred_element_type=jnp.float32) + bias_ref[var, pair]
            s_ctx = lax.dot_general(q2, kctx, nt, preferred_element_type=jnp.float32)
            m = jnp.maximum(jnp.max(s_loc, axis=-1, keepdims=True), jnp.max(s_ctx, axis=-1, keepdims=True))
            p_loc = jnp.exp(s_loc - m)
            p_ctx = jnp.exp(s_ctx - m)
            denom = jnp.sum(p_loc, axis=-1, keepdims=True) + jnp.sum(p_ctx, axis=-1, keepdims=True)
            o2 = _bdot(p_loc.astype(jnp.bfloat16), vw) + _bdot(p_ctx.astype(jnp.bfloat16), vctx)
            o2 = o2 * (1.0 / denom)
            o_pair = jnp.where(first_head, o2[:GRID_W], o2[GRID_W:])
            yabuf[pl.ds(qoff, GRID_W), cols] = o_pair.astype(jnp.bfloat16)

    for j in range(MIX_ROWS):
        row_body(j)

    pbuf[0:POOL_HALO, :] = jnp.where(b > 0, pp_ref[...].astype(jnp.float32), 0.0)
    pbuf[POOL_HALO:POOL_HALO + MIX_TQ, :] = pc_ref[...].astype(jnp.float32)
    pbuf[POOL_HALO + MIX_TQ:, :] = jnp.where(b < nb - 1, pn_ref[...].astype(jnp.float32), 0.0)
    t_abs = b * MIX_TQ + lax.broadcasted_iota(jnp.int32, (MIX_TQ, 1), 0)
    for g, win in enumerate(POOL_WINDOWS):
        cols = slice(g * POOL_DIM, (g + 1) * POOL_DIM)
        acc = None
        for d in range(-(win // 2), win - win // 2):
            term = pbuf[POOL_HALO + d:POOL_HALO + d + MIX_TQ, cols]
            acc = term if acc is None else acc + term
        count = jnp.minimum(t_abs + (win - win // 2), SEQ) - jnp.maximum(t_abs - win // 2, 0)
        pooled = acc / count.astype(jnp.float32) - pbuf[POOL_HALO:POOL_HALO + MIX_TQ, cols]
        yp = _bdot(pooled.astype(jnp.bfloat16), wgrp_ref[g]) * pscale_ref[:, cols]
        ypbuf[:, cols] = yp.astype(jnp.bfloat16)

    ya = _bdot(yabuf[...], wap_ref[...])
    yp = _bdot(ypbuf[...], wpp_ref[...])
    z = ga_ref[...].astype(jnp.float32) * ya + gb_ref[...].astype(jnp.float32) * yp
    y = _bdot(z.astype(jnp.bfloat16), wout_ref[...])
    g1 = mod_ref[0:1, 2 * D_MODEL:3 * D_MODEL]
    _store_row_tiles(o_ref, _layer_norm(DEEPNORM_ALPHA * h_ref[...] + g1 * y, ln1g_ref[...], ln1b_ref[...]))


def _mix_call(h, mod, u, kvc, bias, wgrp, pscale, wap, wpp, wout, ln1g, ln1b):
    nb = SEQ // MIX_TQ
    halo_per_blk = MIX_TQ // KV_HALO
    n_halo = SEQ // KV_HALO
    ph_per_blk = MIX_TQ // POOL_HALO
    n_ph = SEQ // POOL_HALO

    def const(shape):
        return pl.BlockSpec(shape, lambda i: (0,) * len(shape), pipeline_mode=pl.Buffered(1))

    def prev_halo(c):
        return pl.BlockSpec((KV_HALO, ATT_W), lambda i: (jnp.maximum(i * halo_per_blk - 1, 0), c))

    def next_halo(c):
        return pl.BlockSpec((KV_HALO, ATT_W), lambda i: (jnp.minimum((i + 1) * halo_per_blk, n_halo - 1), c))

    def cur(c):
        return pl.BlockSpec((MIX_TQ, ATT_W), lambda i: (i, c))

    in_specs = [
        pl.BlockSpec((MIX_TQ, D_MODEL), lambda i: (i, 0)),
        const(mod.shape),
        cur(0),
        prev_halo(1), cur(1), next_halo(1),
        prev_halo(2), cur(2), next_halo(2),
        pl.BlockSpec((POOL_HALO, POOL_W), lambda i: (jnp.maximum(i * ph_per_blk - 1, 0), 3)),
        cur(3),
        pl.BlockSpec((POOL_HALO, POOL_W), lambda i: (jnp.minimum((i + 1) * ph_per_blk, n_ph - 1), 3)),
        pl.BlockSpec((MIX_TQ, D_MODEL), lambda i: (i, 2)),
        pl.BlockSpec((MIX_TQ, D_MODEL), lambda i: (i, 3)),
        const(kvc.shape), const(bias.shape), const(wgrp.shape), const(pscale.shape),
        const(wap.shape), const(wpp.shape), const(wout.shape),
        const((1, D_MODEL)), const((1, D_MODEL)),
    ]
    return pl.pallas_call(
        _mix_kernel,
        grid=(nb,),
        in_specs=in_specs,
        out_specs=pl.BlockSpec((MIX_TQ * ROW_TILE, LANES), lambda i: (i, 0)),
        out_shape=jax.ShapeDtypeStruct((SEQ * ROW_TILE, LANES), jnp.float32),
        scratch_shapes=[
            pltpu.VMEM((MIX_TQ + 2 * KV_HALO, ATT_W), jnp.bfloat16),
            pltpu.VMEM((MIX_TQ + 2 * KV_HALO, ATT_W), jnp.bfloat16),
            pltpu.VMEM((MIX_TQ, ATT_W), jnp.bfloat16),
            pltpu.VMEM((MIX_TQ + 2 * POOL_HALO, POOL_W), jnp.float32),
            pltpu.VMEM((MIX_TQ, POOL_W), jnp.bfloat16),
        ],
        compiler_params=pltpu.CompilerParams(
            dimension_semantics=("arbitrary",), vmem_limit_bytes=VMEM_LIMIT),
        name="mix",
    )(h, mod, u, u, u, u, u, u, u, u, u, u, u, u,
      kvc, bias, wgrp, pscale, wap, wpp, wout, ln1g, ln1b)


ID_E0, ID_E1, ID_R0, ID_R1 = 0, 1, 4, 5
POS_PACKED = 0
POS_BITS = 16
PLAN_EXPERT, PLAN_VALID, PLAN_NACT, PLAN_NEXT = 0, 1, 2, 3
PLAN_W = 2 * LANES
CW_C0, CW_C1 = 0, 1


def _route_kernel(h_ref, mod_ref, wrt_ref, brt_ref, pos_ref, cw_ref, plan_ref, carry_ref, ids_all):
    i = pl.program_id(0)
    tm = ROUTE_TM

    @pl.when(i == 0)
    def _():
        carry_ref[...] = jnp.zeros_like(carry_ref)

    shift = mod_ref[0:1, 3 * D_MODEL:4 * D_MODEL]
    scale = mod_ref[0:1, 4 * D_MODEL:5 * D_MODEL]
    hm = _load_row_tiles(h_ref, tm) * (1.0 + scale) + shift

    hm_hi, hm_lo = _split_bf16(hm)
    w_hi, w_lo = _split_bf16(wrt_ref[...])
    nt = (((1,), (1,)), ((), ()))
    dg = functools.partial(lax.dot_general, dimension_numbers=nt, preferred_element_type=jnp.float32)
    logits = dg(w_hi, hm_hi) + (dg(w_hi, hm_lo) + dg(w_lo, hm_hi)) + brt_ref[:, 0:1]

    sub = lax.broadcasted_iota(jnp.int32, (LANES, tm), 0)
    big = jnp.int32(1 << 20)
    is_grp = sub < N_GROUPS
    gl = jnp.where(is_grp, logits, -jnp.inf)
    gmax = jnp.max(gl, axis=0, keepdims=True)
    gidx = jnp.min(jnp.where(gl == gmax, sub, big), axis=0, keepdims=True)
    gsum = jnp.sum(jnp.where(is_grp, jnp.exp(logits - gmax), 0.0), axis=0, keepdims=True)
    p_group = 1.0 / gsum

    eid = sub - N_GROUPS
    sel = (eid >= 0) & (eid < N_EXPERTS) & (lax.shift_right_arithmetic(eid, 3) == gidx)
    el = jnp.where(sel, logits, -jnp.inf)
    l0 = jnp.max(el, axis=0, keepdims=True)
    i0 = jnp.min(jnp.where(el == l0, sub, big), axis=0, keepdims=True)
    el2 = jnp.where(sub == i0, -jnp.inf, el)
    l1 = jnp.max(el2, axis=0, keepdims=True)
    i1 = jnp.min(jnp.where(el2 == l1, sub, big), axis=0, keepdims=True)
    t = jnp.exp(l1 - l0)
    w0 = 1.0 / (1.0 + t)
    w1 = t / (1.0 + t)

    half_rows = jnp.where(i >= pl.num_programs(0) // SEQ_PARTS, N_EXPERTS, 0)
    i0 = i0 + half_rows
    i1 = i1 + half_rows
    onehot = jnp.where((sub == i0) | (sub == i1), 1.0, 0.0)
    rr = lax.broadcasted_iota(jnp.int32, (tm, tm), 0)
    cc = lax.broadcasted_iota(jnp.int32, (tm, tm), 1)
    earlier = jnp.where(rr < cc, 1.0, 0.0).astype(jnp.bfloat16)
    carry = carry_ref[:, 0:1]
    prefix = _bdot(onehot.astype(jnp.bfloat16), earlier) + carry
    r0 = jnp.sum(jnp.where(sub == i0, prefix, 0.0), axis=0, keepdims=True)
    r1 = jnp.sum(jnp.where(sub == i1, prefix, 0.0), axis=0, keepdims=True)
    total = jnp.broadcast_to(carry + jnp.sum(onehot, axis=1, keepdims=True), carry_ref.shape)
    carry_ref[...] = total

    sub8 = lax.broadcasted_iota(jnp.int32, (ROW_TILE, tm), 0)
    ids = jnp.zeros((ROW_TILE, tm), jnp.int32)
    for idx, val in ((ID_E0, i0 - N_GROUPS), (ID_E1, i1 - N_GROUPS),
                     (ID_R0, r0.astype(jnp.int32)), (ID_R1, r1.astype(jnp.int32))):
        ids = jnp.where(sub8 == idx, val, ids)
    ids_all[:, pl.ds(pl.multiple_of(i * tm, tm), tm)] = ids

    cwt = jnp.where(sub == CW_C0, p_group * w0, jnp.where(sub == CW_C1, p_group * w1, 0.0))
    cw_ref[...] = cwt.T

    @pl.when(i == pl.num_programs(0) - 1)
    def _():
        subq = lax.broadcasted_iota(jnp.int32, (LANES, LANES), 0)
        laneq = lax.broadcasted_iota(jnp.int32, (LANES, LANES), 1)
        cnt = total.astype(jnp.int32)
        tiles = lax.shift_right_logical(cnt + (EXP_TM - 1), EXP_TM.bit_length() - 1).astype(jnp.float32)
        incl = jnp.where(laneq <= subq, 1.0, 0.0).astype(jnp.bfloat16)
        tile_end = _bdot(incl, tiles.astype(jnp.bfloat16))
        tile_start = tile_end - tiles
        seg = (tile_start * EXP_TM).astype(jnp.int32)
        nact = jnp.max(tile_end, axis=0, keepdims=True)

        ids_full = ids_all[...]
        look = jnp.zeros_like(ids_full)
        for e in range(N_SEG):
            look = jnp.where(ids_full == e, seg[N_GROUPS + e, 0], look)
        pos01 = look + pltpu.roll(ids_full, ID_R0 - ID_E0, axis=0)
        assert EXP_TILES * EXP_TM <= 1 << POS_BITS
        pos_ref[...] = pos01 | (pltpu.roll(pos01, ROW_TILE - 1, axis=0) << POS_BITS)

        subp = lax.broadcasted_iota(jnp.int32, (LANES, PLAN_W), 0)
        tile = lax.broadcasted_iota(jnp.int32, (LANES, PLAN_W), 1).astype(jnp.float32)
        is_exp = (subp >= N_GROUPS) & (subp < N_GROUPS + N_SEG)
        end_col = tile_end[:, 0:1]
        nact_s = nact[:, 0:1]
        te = jnp.sum(jnp.where(is_exp & (tile >= end_col), 1.0, 0.0), axis=0, keepdims=True)
        te_last = jnp.sum(jnp.where(is_exp & (nact_s - 1.0 >= end_col), 1.0, 0.0), axis=0, keepdims=True)[:, 0:1]
        tile_row = tile[0:1, :]
        te = jnp.minimum(jnp.where(tile_row < nact_s, te, te_last), N_SEG - 1.0)
        mine = (subp - N_GROUPS).astype(jnp.float32) == te
        cnt_sel = jnp.sum(jnp.where(mine, total[:, 0:1], 0.0), axis=0, keepdims=True)
        start_sel = jnp.sum(jnp.where(mine, tile_start[:, 0:1], 0.0), axis=0, keepdims=True)
        end_sel = jnp.sum(jnp.where(mine, end_col, 0.0), axis=0, keepdims=True)
        valid = jnp.clip(cnt_sel - (tile_row - start_sel) * EXP_TM, 0.0, float(EXP_TM))
        valid = jnp.where(tile_row < nact_s, valid, 0.0)
        subr = lax.broadcasted_iota(jnp.int32, (ROW_TILE, PLAN_W), 0)
        plan = jnp.where(subr == PLAN_EXPERT, te, jnp.where(subr == PLAN_VALID, valid,
                         jnp.where(subr == PLAN_NACT, nact_s, jnp.where(subr == PLAN_NEXT, end_sel, 0.0))))
        plan_ref[...] = plan.astype(jnp.int32)


def _route_call(h, mod, wrt, brt):
    tm = ROUTE_TM
    return pl.pallas_call(
        _route_kernel,
        grid=(SEQ // tm,),
        in_specs=[
            pl.BlockSpec((tm * ROW_TILE, LANES), lambda i: (i, 0)),
            pl.BlockSpec(mod.shape, lambda i: (0, 0)),
            pl.BlockSpec((LANES, D_MODEL), lambda i: (0, 0)),
            pl.BlockSpec((LANES, LANES), lambda i: (0, 0)),
        ],
        out_specs=[
            pl.BlockSpec((ROW_TILE, SEQ), lambda i: (0, 0)),
            pl.BlockSpec((tm, LANES), lambda i: (i, 0)),
            pl.BlockSpec((ROW_TILE, PLAN_W), lambda i: (0, 0)),
        ],
        out_shape=[
            jax.ShapeDtypeStruct((ROW_TILE, SEQ), jnp.int32),
            jax.ShapeDtypeStruct((SEQ, LANES), jnp.float32),
            jax.ShapeDtypeStruct((ROW_TILE, PLAN_W), jnp.int32),
        ],
        scratch_shapes=[pltpu.VMEM((LANES, LANES), jnp.float32),
                        pltpu.VMEM((ROW_TILE, SEQ), jnp.int32)],
        compiler_params=pltpu.CompilerParams(
            dimension_semantics=("arbitrary",), vmem_limit_bytes=VMEM_LIMIT),
        name="route",
    )(h, mod, wrt, brt)


SRC_UNROLL = 8
EXP_BUFS = 3
EXP_CHUNK = 256
TILE_ROWS = EXP_TM * ROW_TILE


def _experts_kernel(te_ref, tv_ref, nact_ref, tnext_ref, pos_ref,
                    h_hbm, mod_ref, wg_hbm, wu_hbm, wd_hbm,
                    y_hbm,
                    src_ref, hres, xbuf, ybuf, wgs, wus, wds, wgb, wub, wdb, rsem, ssem, wsem):
    i = pl.program_id(0)
    last = pl.num_programs(0) - 1
    nact = nact_ref[0]
    cur = lax.rem(i, EXP_BUFS)
    nxt2 = lax.rem(i + 2, EXP_BUFS)
    prv2 = lax.rem(i + 1, EXP_BUFS)
    xcur = lax.rem(i, 2)
    part = lax.shift_right_logical(te_ref[i], N_EXPERTS.bit_length() - 1)

    def scatter_copy(p, k, s):
        dst = src_ref[p]
        return pltpu.make_async_copy(ybuf.at[s, pl.ds(pl.multiple_of(k * ROW_TILE, ROW_TILE), ROW_TILE), :],
                                     y_hbm.at[pl.ds(pl.multiple_of(dst * ROW_TILE, ROW_TILE), ROW_TILE), :],
                                     ssem.at[s])

    def scatter_rolled(tile, s, n):
        def chunk(kk, c):
            for u in range(SRC_UNROLL):
                k = kk * SRC_UNROLL + u
                scatter_copy(tile * EXP_TM + k, k, s).start(priority=u % 2)
            return c
        full = lax.shift_right_logical(n, SRC_UNROLL.bit_length() - 1)
        lax.fori_loop(0, full, chunk, 0)

        def single(k, c):
            scatter_copy(tile * EXP_TM + k, k, s).start()
            return c
        lax.fori_loop(full * SRC_UNROLL, n, single, 0)

    def weight_copies(segment):
        e = segment & (N_EXPERTS - 1)
        return [pltpu.make_async_copy(w_hbm.at[e], stage, wsem.at[n])
                for n, (w_hbm, stage) in enumerate(((wg_hbm, wgs), (wu_hbm, wus), (wd_hbm, wds)))]

    def gather_row(tile, k, s):
        local = (src_ref[tile * EXP_TM + k] - part * PART_TOKENS) & (PART_TOKENS - 1)
        xbuf[s, k * ROW_TILE:(k + 1) * ROW_TILE, :] = hres[pl.ds(pl.multiple_of(local * ROW_TILE, ROW_TILE),
                                                             ROW_TILE), :]

    def gather_items(tile, s):
        return [functools.partial(gather_row, tile, k, s) for k in range(EXP_TM)]

    def wait_scatter(s, n):
        rows = pl.multiple_of(n * ROW_TILE, ROW_TILE)
        pltpu.make_async_copy(ybuf.at[s, pl.ds(0, rows), :], y_hbm.at[pl.ds(0, rows), :], ssem.at[s]).wait()

    def compute_chunks(s, xs):
        state = {}

        def load():
            x = _load_row_tiles(xbuf, EXP_TM, lead=(xs,))
            shift = mod_ref[0:1, 3 * D_MODEL:4 * D_MODEL]
            scale = mod_ref[0:1, 4 * D_MODEL:5 * D_MODEL]
            state["x"] = (x * (1.0 + scale) + shift).astype(jnp.bfloat16)
            state["act"] = []

        def gate(c):
            def run():
                if c == 0:
                    load()
                state["a"] = _bdot(state["x"], wgb[:, c * EXP_CHUNK:(c + 1) * EXP_CHUNK])
            return run

        def up(c):
            def run():
                a = state["a"]
                u = _bdot(state["x"], wub[:, c * EXP_CHUNK:(c + 1) * EXP_CHUNK])
                state["act"].append((a * jax.nn.sigmoid(a) * u).astype(jnp.bfloat16))
            return run

        def down(c):
            def run():
                if c == 0:
                    state["actf"] = jnp.concatenate(state["act"], axis=-1)
                yc = _bdot(state["actf"], wdb[:, c * EXP_CHUNK:(c + 1) * EXP_CHUNK])
                for jj in range(EXP_CHUNK // LANES):
                    j = c * (EXP_CHUNK // LANES) + jj
                    ybuf[s, pl.ds(j, EXP_TM, stride=ROW_TILE), :] = yc[:, jj * LANES:(jj + 1) * LANES]
            return run

        steps = []
        for c in range(D_EXPERT // EXP_CHUNK):
            steps += [(gate(c), 2), (up(c), 2)]
        return steps + [(down(c), 1) for c in range(D_MODEL // EXP_CHUNK)]

    def run_interleaved(chunks, dmas):
        total_cost = sum(cost for _, cost in chunks)
        done = 0
        for chunk, cost in chunks:
            upto = -(-len(dmas) * (done + cost) // total_cost)
            for d in dmas[-(-len(dmas) * done // total_cost):upto]:
                d()
            done += cost
            chunk()

    def scatter_dmas(tile, s):
        return [functools.partial(lambda k: scatter_copy(tile * EXP_TM + k, k, s).start(priority=k % 2), k)
                for k in range(EXP_TM)]

    @pl.when(i == 0)
    def _():
        for cp in weight_copies(te_ref[0]):
            cp.start()

        def fill_body(tt, c):
            ts = [tt * SRC_UNROLL + u for u in range(SRC_UNROLL)]
            words = [pos_ref[t] for t in ts]
            for t, w in zip(ts, words):
                src_ref[w & ((1 << POS_BITS) - 1)] = t
                src_ref[lax.shift_right_logical(w, POS_BITS)] = t + SEQ
            return c
        lax.fori_loop(0, SEQ // SRC_UNROLL, fill_body, 0)

        def pad_tile(t, c):
            pad_tok = lax.shift_right_logical(te_ref[t], N_EXPERTS.bit_length() - 1) * PART_TOKENS

            def pad_row(k, c2):
                src_ref[t * EXP_TM + k] = pad_tok
                return c2
            return lax.fori_loop(tv_ref[t], EXP_TM, pad_row, c)
        lax.fori_loop(0, nact, pad_tile, 0)

    active = i < nact
    prev = jnp.maximum(i - 1, 0)
    prev_ok = (i >= 1) & (i - 1 < nact)
    prev2 = jnp.maximum(i - 2, 0)
    prev2_ok = (i >= 2) & (i - 2 < nact)
    nprev = tv_ref[prev]
    full_prev = prev_ok & (nprev == EXP_TM)
    new_segment = (i == 0) | (te_ref[i] != te_ref[prev])

    @pl.when(prev2_ok)
    def _():
        wait_scatter(prv2, tv_ref[prev2])

    @pl.when(active & ((i == 0) | (part != lax.shift_right_logical(te_ref[prev], N_EXPERTS.bit_length() - 1))))
    def _():
        rows = PART_TOKENS * ROW_TILE
        cp = pltpu.make_async_copy(h_hbm.at[pl.ds(pl.multiple_of(part * rows, rows), rows), :], hres, rsem.at[0])
        cp.start()
        cp.wait()
        for item in gather_items(i, xcur):
            item()

    @pl.when(active & new_segment)
    def _():
        for cp in weight_copies(te_ref[i]):
            cp.wait()
        wgb[...] = wgs[...].astype(jnp.bfloat16)
        wub[...] = wus[...].astype(jnp.bfloat16)
        wdb[...] = wds[...].astype(jnp.bfloat16)
        nxt = tnext_ref[i]

        @pl.when(nxt < nact)
        def _():
            for cp in weight_copies(te_ref[jnp.minimum(nxt, last)]):
                cp.start()

    @pl.when(prev_ok & jnp.logical_not(active & full_prev))
    def _():
        scatter_rolled(prev, nxt2, nprev)

    tile_next = jnp.minimum(i + 1, nact - 1)

    @pl.when(active & full_prev)
    def _():
        sc, g = scatter_dmas(prev, nxt2), gather_items(tile_next, 1 - xcur)
        run_interleaved(compute_chunks(cur, xcur), [item for pair in zip(sc, g) for item in pair])

    @pl.when(active & jnp.logical_not(full_prev))
    def _():
        run_interleaved(compute_chunks(cur, xcur), gather_items(tile_next, 1 - xcur))

    @pl.when(i == last)
    def _():
        @pl.when(prev_ok)
        def _():
            wait_scatter(nxt2, nprev)

        @pl.when(active)
        def _():
            scatter_rolled(i, cur, tv_ref[i])
            wait_scatter(cur, tv_ref[i])


def _experts_call(te, tv, nact, tnext, pos, h, mod, wg, wu, wd):
    grid_spec = pltpu.PrefetchScalarGridSpec(
        num_scalar_prefetch=5,
        grid=(EXP_TILES,),
        in_specs=[
            pl.BlockSpec(memory_space=pl.ANY),
            pl.BlockSpec(mod.shape, lambda i, *_: (0, 0)),
            pl.BlockSpec(memory_space=pl.ANY),
            pl.BlockSpec(memory_space=pl.ANY),
            pl.BlockSpec(memory_space=pl.ANY),
        ],
        out_specs=pl.BlockSpec(memory_space=pl.ANY),
        scratch_shapes=[
            pltpu.SMEM((EXP_TILES * EXP_TM,), jnp.int32),
            pltpu.VMEM((PART_TOKENS * ROW_TILE, LANES), jnp.float32),
            pltpu.VMEM((2, TILE_ROWS, LANES), jnp.float32),
            pltpu.VMEM((EXP_BUFS, TILE_ROWS, LANES), jnp.float32),
            pltpu.VMEM((D_MODEL, D_EXPERT), jnp.float32),
            pltpu.VMEM((D_MODEL, D_EXPERT), jnp.float32),
            pltpu.VMEM((D_EXPERT, D_MODEL), jnp.float32),
            pltpu.VMEM((D_MODEL, D_EXPERT), jnp.bfloat16),
            pltpu.VMEM((D_MODEL, D_EXPERT), jnp.bfloat16),
            pltpu.VMEM((D_EXPERT, D_MODEL), jnp.bfloat16),
            pltpu.SemaphoreType.DMA((1,)),
            pltpu.SemaphoreType.DMA((EXP_BUFS,)),
            pltpu.SemaphoreType.DMA((3,)),
        ],
    )
    return pl.pallas_call(
        _experts_kernel,
        grid_spec=grid_spec,
        out_shape=jax.ShapeDtypeStruct((2 * SEQ * ROW_TILE, LANES), jnp.float32),
        compiler_params=pltpu.CompilerParams(
            dimension_semantics=("arbitrary",), vmem_limit_bytes=EXPERTS_VMEM_LIMIT),
        name="experts",
    )(te, tv, nact, tnext, pos, h, mod, wg, wu, wd)


def _combine_kernel(h_ref, y0_ref, y1_ref, cw_ref, mod_ref, g_ref, b_ref, o_ref):
    tm = o_ref.shape[0]
    c0 = cw_ref[:, CW_C0:CW_C0 + 1]
    c1 = cw_ref[:, CW_C1:CW_C1 + 1]
    ffn = c0 * _load_row_tiles(y0_ref, tm) + c1 * _load_row_tiles(y1_ref, tm)
    g2 = mod_ref[0:1, 5 * D_MODEL:6 * D_MODEL]
    o_ref[...] = _layer_norm(DEEPNORM_ALPHA * _load_row_tiles(h_ref, tm) + g2 * ffn, g_ref[...], b_ref[...])


def _combine_call(h, y, cw, mod, g, b):
    tm = CMB_TM
    nblk = SEQ // tm
    tiles = lambda off: pl.BlockSpec((tm * ROW_TILE, LANES), lambda i: (i + off, 0))
    return pl.pallas_call(
        _combine_kernel,
        grid=(nblk,),
        in_specs=[
            tiles(0), tiles(0), tiles(nblk),
            pl.BlockSpec((tm, LANES), lambda i: (i, 0)),
            pl.BlockSpec(mod.shape, lambda i: (0, 0)),
            pl.BlockSpec((1, D_MODEL), lambda i: (0, 0)),
            pl.BlockSpec((1, D_MODEL), lambda i: (0, 0)),
        ],
        out_specs=pl.BlockSpec((tm, D_MODEL), lambda i: (i, 0)),
        out_shape=jax.ShapeDtypeStruct((SEQ, D_MODEL), jnp.float32),
        compiler_params=pltpu.CompilerParams(
            dimension_semantics=("arbitrary",), vmem_limit_bytes=VMEM_LIMIT),
        name="combine",
    )(h, y, y, cw, mod, g, b)


def kernel(x, c, ctx, c_ctx, ln_in_g, ln_in_b, w_mod, b_mod, w_in, rpb, w_pool_grp, pool_scale,
           w_attn_proj, w_pool_proj, w_out, ln1_g, ln1_b, w_router_group, b_router_group,
           w_router_expert, b_router_expert, w_expert_gate, w_expert_up, w_expert_down, ln2_g, ln2_b):
    assert x.shape == (1, SEQ, D_MODEL) and ctx.shape == (1, CTX_LEN, D_MODEL)
    assert w_mod.shape[0] == 1, "single-layer trunk"
    f32, bf16 = jnp.float32, jnp.bfloat16
    row = lambda v: v.reshape(1, -1).astype(f32)

    cond = jnp.concatenate([c, c_ctx[None], jnp.zeros((MOD_ROWS - 2, D_MODEL), f32)], axis=0)
    mod = _mod_call(cond, w_mod[0], row(b_mod[0]))

    lng, lnb = row(ln_in_g), row(ln_in_b)
    w_in_b = w_in[0].astype(bf16)
    u, h0 = _proj_call(x[0], mod, lng, lnb, w_in_b, mod_row=0, latent=True, tm=PROJ_TM)
    kvc, = _proj_call(ctx[0], mod, lng, lnb, w_in_b[:, ATT_W:3 * ATT_W], mod_row=1, latent=False, tm=CTX_LEN)

    h1 = _mix_call(h0, mod, u, kvc, _attn_bias_table(rpb[0]),
                   w_pool_grp[0].astype(bf16), row(pool_scale[0]),
                   w_attn_proj[0].astype(bf16), w_pool_proj[0].astype(bf16), w_out[0].astype(bf16),
                   row(ln1_g[0]), row(ln1_b[0]))

    n_logit = N_GROUPS + N_EXPERTS
    wrt = jnp.concatenate([w_router_group[0].T, w_router_expert[0].T,
                           jnp.zeros((LANES - n_logit, D_MODEL), f32)], axis=0)
    brt = jnp.concatenate([b_router_group[0], b_router_expert[0], jnp.zeros((LANES - n_logit,), f32)])
    brt = jnp.broadcast_to(brt[:, None], (LANES, LANES))
    pos, cw, plan = _route_call(h1, mod, wrt, brt)

    y = _experts_call(plan[PLAN_EXPERT, :EXP_TILES], plan[PLAN_VALID, :EXP_TILES], plan[PLAN_NACT, :1],
                      plan[PLAN_NEXT, :EXP_TILES], pos[POS_PACKED], h1, mod,
                      w_expert_gate[0], w_expert_up[0], w_expert_down[0])
    out = _combine_call(h1, y, cw, mod, row(ln2_g[0]), row(ln2_b[0]))
    return out[None]
```

```python
import functools

import jax
import jax.numpy as jnp
from jax import lax
from jax.experimental import pallas as pl
from jax.experimental.pallas import tpu as pltpu

D_MODEL = 1024
SEQ = 16384
GRID_W = 64
ROWS = SEQ // GRID_W
CTX_LEN = 256
N_HEADS = 8
HEAD_DIM = 64
ATT_W = N_HEADS * HEAD_DIM
WIN_H = 8
WIN_W = 16
POOL_WINDOWS = (2, 4, 8, 16)
POOL_GROUPS = 4
POOL_DIM = 128
POOL_W = POOL_GROUPS * POOL_DIM
PROJ_W = 3 * ATT_W + POOL_W + 2 * D_MODEL
GATE_COL = 3 * ATT_W + POOL_W
N_GROUPS = 4
EXPERTS_PER_GROUP = 8
N_EXPERTS = N_GROUPS * EXPERTS_PER_GROUP
D_EXPERT = 512
N_MOD = 6
DEEPNORM_ALPHA = 2.0 ** 0.25
LN_EPS = 1e-5
NEG_INF = -1e30

LANES = 128
ROW_TILE = 8
MOD_ROWS = 8
PROJ_TM = 512
MIX_ROWS = 8
MIX_TQ = MIX_ROWS * GRID_W
KV_HALO = 4 * GRID_W
POOL_HALO = 16
ROUTE_TM = 512
EXP_TM = 256
SEQ_PARTS = 2
PART_TOKENS = SEQ // SEQ_PARTS
N_SEG = SEQ_PARTS * N_EXPERTS
EXP_TILES = 2 * SEQ // EXP_TM + N_SEG
CMB_TM = 256
MERGE_ROWS = 128
ATTN_AHEAD = 2
HALF = D_MODEL // 2
VMEM_LIMIT = 56 * 1024 * 1024
EXPERTS_VMEM_LIMIT = 60 * 1024 * 1024


def _layer_norm(x, g, b):
    mu = jnp.mean(x, axis=-1, keepdims=True)
    xc = x - mu
    var = jnp.mean(xc * xc, axis=-1, keepdims=True)
    return xc * lax.rsqrt(var + LN_EPS) * g + b


def _bdot(a, b):
    return jnp.dot(a, b, preferred_element_type=jnp.float32)


def _split_bf16(a):
    hi = a.astype(jnp.bfloat16)
    lo = (a - hi.astype(jnp.float32)).astype(jnp.bfloat16)
    return hi, lo


def _dot3(a, b):
    a_hi, a_lo = _split_bf16(a)
    b_hi, b_lo = _split_bf16(b)
    return _bdot(a_hi, b_hi) + (_bdot(a_hi, b_lo) + _bdot(a_lo, b_hi))


def _load_row_tiles(ref, tokens, lead=()):
    parts = [ref[(*lead, pl.ds(j, tokens, stride=ROW_TILE), slice(None))] for j in range(ROW_TILE)]
    return jnp.concatenate(parts, axis=-1)


def _store_row_tiles(ref, value, lead=()):
    tokens = value.shape[0]
    for j in range(ROW_TILE):
        ref[(*lead, pl.ds(j, tokens, stride=ROW_TILE), slice(None))] = value[:, j * LANES:(j + 1) * LANES]


def _mod_kernel(cond_ref, w_ref, b_ref, o_ref):
    cond = cond_ref[...]
    act = cond * jax.nn.sigmoid(cond)
    o_ref[...] = _dot3(act, w_ref[...]) + b_ref[...]


def _mod_call(cond, w_mod, b_mod):
    tn = 1536
    n = N_MOD * D_MODEL
    return pl.pallas_call(
        _mod_kernel,
        grid=(n // tn,),
        in_specs=[
            pl.BlockSpec((MOD_ROWS, D_MODEL), lambda i: (0, 0)),
            pl.BlockSpec((D_MODEL, tn), lambda i: (0, i)),
            pl.BlockSpec((1, tn), lambda i: (0, i)),
        ],
        out_specs=pl.BlockSpec((MOD_ROWS, tn), lambda i: (0, i)),
        out_shape=jax.ShapeDtypeStruct((MOD_ROWS, n), jnp.float32),
        compiler_params=pltpu.CompilerParams(
            dimension_semantics=("arbitrary",), vmem_limit_bytes=VMEM_LIMIT),
        name="mod",
    )(cond, w_mod, b_mod)


def _proj_kernel(x_ref, mod_ref, g_ref, b_ref, w_ref, o_ref, *h_out, mod_row, latent):
    h = _layer_norm(x_ref[...], g_ref[...], b_ref[...])
    if latent:
        h_out[0][...] = h
    shift = mod_ref[mod_row:mod_row + 1, 0:D_MODEL]
    scale = mod_ref[mod_row:mod_row + 1, D_MODEL:2 * D_MODEL]
    hm = (h * (1.0 + scale) + shift).astype(jnp.bfloat16)
    n = o_ref.shape[1]
    for c in range(n // D_MODEL):
        sl = slice(c * D_MODEL, (c + 1) * D_MODEL)
        res = _bdot(hm, w_ref[:, sl])
        if latent and c == 0:
            lane = lax.broadcasted_iota(jnp.int32, (1, D_MODEL), 1)
            res = res * jnp.where(lane < ATT_W, HEAD_DIM ** -0.5, 1.0)
        if latent and c * D_MODEL >= GATE_COL:
            res = jax.nn.sigmoid(res)
        o_ref[:, sl] = res.astype(jnp.bfloat16)


def _proj_call(x, mod, g, b, w, *, mod_row, latent, tm):
    rows, n = x.shape[0], w.shape[1]
    out_specs = [pl.BlockSpec((tm, n), lambda i: (i, 0))]
    out_shape = [jax.ShapeDtypeStruct((rows, n), jnp.bfloat16)]
    if latent:
        out_specs.append(pl.BlockSpec((tm, D_MODEL), lambda i: (i, 0)))
        out_shape.append(jax.ShapeDtypeStruct((rows, D_MODEL), jnp.float32))
    return pl.pallas_call(
        functools.partial(_proj_kernel, mod_row=mod_row, latent=latent),
        grid=(rows // tm,),
        in_specs=[
            pl.BlockSpec((tm, D_MODEL), lambda i: (i, 0)),
            pl.BlockSpec(mod.shape, lambda i: (0, 0)),
            pl.BlockSpec((1, D_MODEL), lambda i: (0, 0)),
            pl.BlockSpec((1, D_MODEL), lambda i: (0, 0)),
            pl.BlockSpec((D_MODEL, n), lambda i: (0, 0), pipeline_mode=pl.Buffered(1)),
        ],
        out_specs=out_specs,
        out_shape=out_shape,
        compiler_params=pltpu.CompilerParams(
            dimension_semantics=("arbitrary",), vmem_limit_bytes=VMEM_LIMIT),
        name="proj",
    )(x, mod, g, b, w)


def _attn_bias_table(rpb):
    col = jnp.arange(GRID_W, dtype=jnp.int32)
    col_start = jnp.clip(col - WIN_W // 2, 0, GRID_W - WIN_W)
    col_mask = (col[None, :] >= col_start[:, None]) & (col[None, :] < col_start[:, None] + WIN_W)
    col_off = jnp.clip(col[None, :] - col[:, None], 1 - WIN_W, WIN_W - 1) + (WIN_W - 1)
    onehot = (col_off[None] == jnp.arange(2 * WIN_W - 1, dtype=jnp.int32)[:, None, None]).astype(jnp.float32)
    tab = jnp.einsum("hrc,cqk->hrqk", rpb.astype(jnp.float32), onehot, precision=lax.Precision.HIGHEST)
    tab = jnp.where(col_mask[None, None], tab, NEG_INF)
    tab = jnp.stack([tab[:, WIN_H - 1 - v:2 * WIN_H - 1 - v] for v in range(WIN_H)], axis=0)
    tab = tab.transpose(0, 1, 3, 2, 4)
    return tab.reshape(WIN_H, N_HEADS // 2, 2 * GRID_W, WIN_H * GRID_W)


def _mix_kernel(h_ref, mod_ref,
                q_ref, kp_ref, kc_ref, kn_ref, vp_ref, vc_ref, vn_ref,
                pp_ref, pc_ref, pn_ref, ga_ref, gb_ref,
                kvc_ref, bias_ref, wgrp_ref, pscale_ref, wap_ref, wpp_ref, wout_ref,
                ln1g_ref, ln1b_ref,
                o_ref,
                kbuf, vbuf, yabuf, pbuf, ypbuf, zbuf):
    b = pl.program_id(0)
    nb = pl.num_programs(0)

    kbuf[0:KV_HALO, :] = kp_ref[...]
    kbuf[KV_HALO:KV_HALO + MIX_TQ, :] = kc_ref[...]
    kbuf[KV_HALO + MIX_TQ:, :] = kn_ref[...]
    vbuf[0:KV_HALO, :] = vp_ref[...]
    vbuf[KV_HALO:KV_HALO + MIX_TQ, :] = vc_ref[...]
    vbuf[KV_HALO + MIX_TQ:, :] = vn_ref[...]

    lane = lax.broadcasted_iota(jnp.int32, (GRID_W, LANES), 1)
    first_head = lane < HEAD_DIM

    units = [(j, pair) for j in range(MIX_ROWS) for pair in range(N_HEADS // 2)]
    nt = (((1,), (1,)), ((), ()))

    def window(j):
        r = b * MIX_ROWS + j
        rs = jnp.clip(r - WIN_H // 2, 0, ROWS - WIN_H)
        return pl.multiple_of((rs - b * MIX_ROWS + WIN_H // 2) * GRID_W, GRID_W), r - rs

    def scores(j, pair):
        off, var = window(j)
        cols = slice(pair * LANES, (pair + 1) * LANES)
        q = q_ref[j * GRID_W:(j + 1) * GRID_W, cols]
        zero = jnp.zeros_like(q)
        q2 = jnp.concatenate([jnp.where(first_head, q, zero), jnp.where(first_head, zero, q)], axis=0)
        kw = kbuf[pl.ds(off, WIN_H * GRID_W), cols]
        s_loc = lax.dot_general(q2, kw, nt, preferred_element_type=jnp.float32) + bias_ref[var, pair]
        s_ctx = lax.dot_general(q2, kvc_ref[:, cols], nt, preferred_element_type=jnp.float32)
        return s_loc, s_ctx

    def values(j, pair, s_loc, s_ctx):
        off, _ = window(j)
        cols = slice(pair * LANES, (pair + 1) * LANES)
        vw = vbuf[pl.ds(off, WIN_H * GRID_W), cols]
        vctx = kvc_ref[:, ATT_W + pair * LANES:ATT_W + (pair + 1) * LANES]
        m = jnp.maximum(jnp.max(s_loc, axis=-1, keepdims=True), jnp.max(s_ctx, axis=-1, keepdims=True))
        p_loc = jnp.exp(s_loc - m)
        p_ctx = jnp.exp(s_ctx - m)
        denom = jnp.sum(p_loc, axis=-1, keepdims=True) + jnp.sum(p_ctx, axis=-1, keepdims=True)
        o2 = _bdot(p_loc.astype(jnp.bfloat16), vw) + _bdot(p_ctx.astype(jnp.bfloat16), vctx)
        o2 = o2 * (1.0 / denom)
        o_pair = jnp.where(first_head, o2[:GRID_W], o2[GRID_W:])
        yabuf[j * GRID_W:(j + 1) * GRID_W, cols] = o_pair.astype(jnp.bfloat16)

    pbuf[0:POOL_HALO, :] = jnp.where(b > 0, pp_ref[...].astype(jnp.float32), 0.0)
    pbuf[POOL_HALO:POOL_HALO + MIX_TQ, :] = pc_ref[...].astype(jnp.float32)
    pbuf[POOL_HALO + MIX_TQ:, :] = jnp.where(b < nb - 1, pn_ref[...].astype(jnp.float32), 0.0)
    edge = lax.broadcasted_iota(jnp.int32, (ROW_TILE, 1), 0)

    def pool_group(g):
        win = POOL_WINDOWS[g]
        lo, hi = win // 2, win - win // 2
        cols = slice(g * POOL_DIM, (g + 1) * POOL_DIM)
        acc = None
        for d in range(-lo, hi):
            term = pbuf[POOL_HALO + d:POOL_HALO + d + MIX_TQ, cols]
            acc = term if acc is None else acc + term
        assert max(lo, hi) <= ROW_TILE
        top = jnp.where(b == 0, 1.0 / (win - jnp.maximum(lo - edge, 0)).astype(jnp.float32), 1.0 / win)
        bot = jnp.where(b == nb - 1,
                        1.0 / (win - jnp.maximum(edge + hi - ROW_TILE, 0)).astype(jnp.float32), 1.0 / win)
        inv = jnp.concatenate([top, jnp.full((MIX_TQ - 2 * ROW_TILE, 1), 1.0 / win, jnp.float32), bot], axis=0)
        pooled = acc * inv - pbuf[POOL_HALO:POOL_HALO + MIX_TQ, cols]
        yp = _bdot(pooled.astype(jnp.bfloat16), wgrp_ref[g]) * pscale_ref[:, cols]
        ypbuf[:, cols] = yp.astype(jnp.bfloat16)

    def pooled_branch():
        zbuf[...] = gb_ref[...].astype(jnp.float32) * _bdot(ypbuf[...], wpp_ref[...])

    extra = {}
    for g in range(POOL_GROUPS):
        extra[(g + 1) * len(units) // (POOL_GROUPS + 2)] = functools.partial(pool_group, g)
    extra[(POOL_GROUPS + 1) * len(units) // (POOL_GROUPS + 2)] = pooled_branch
    pending = [scores(*u) for u in units[:ATTN_AHEAD]]
    for n, u in enumerate(units):
        if n + ATTN_AHEAD < len(units):
            pending.append(scores(*units[n + ATTN_AHEAD]))
        values(*u, *pending.pop(0))
        if n in extra:
            extra[n]()

    g1 = mod_ref[0:1, 2 * D_MODEL:3 * D_MODEL]
    n_chunks = MIX_TQ // MERGE_ROWS
    rows = lambda c: slice(c * MERGE_ROWS, (c + 1) * MERGE_ROWS)
    z, y = {}, {}

    def stage_a(c):
        ya = _bdot(yabuf[rows(c), :], wap_ref[...])
        z[c] = (ga_ref[rows(c), :].astype(jnp.float32) * ya + zbuf[rows(c), :]).astype(jnp.bfloat16)

    def stage_b(c):
        y[c] = _bdot(z.pop(c), wout_ref[...])

    def stage_c(c):
        hn = _layer_norm(DEEPNORM_ALPHA * h_ref[rows(c), :] + g1 * y.pop(c), ln1g_ref[...], ln1b_ref[...])
        for j in range(ROW_TILE):
            o_ref[pl.ds(c * MERGE_ROWS * ROW_TILE + j, MERGE_ROWS, stride=ROW_TILE), :] = hn[:, j * LANES:(j + 1) * LANES]

    for t in range(n_chunks + 2):
        if t < n_chunks:
            stage_a(t)
        if 0 <= t - 1 < n_chunks:
            stage_b(t - 1)
        if 0 <= t - 2 < n_chunks:
            stage_c(t - 2)


def _mix_call(h, mod, u, kvc, bias, wgrp, pscale, wap, wpp, wout, ln1g, ln1b):
    nb = SEQ // MIX_TQ
    halo_per_blk = MIX_TQ // KV_HALO
    n_halo = SEQ // KV_HALO
    ph_per_blk = MIX_TQ // POOL_HALO
    n_ph = SEQ // POOL_HALO

    def const(shape):
        return pl.BlockSpec(shape, lambda i: (0,) * len(shape), pipeline_mode=pl.Buffered(1))

    def prev_halo(c):
        return pl.BlockSpec((KV_HALO, ATT_W), lambda i: (jnp.maximum(i * halo_per_blk - 1, 0), c))

    def next_halo(c):
        return pl.BlockSpec((KV_HALO, ATT_W), lambda i: (jnp.minimum((i + 1) * halo_per_blk, n_halo - 1), c))

    def cur(c):
        return pl.BlockSpec((MIX_TQ, ATT_W), lambda i: (i, c))

    in_specs = [
        pl.BlockSpec((MIX_TQ, D_MODEL), lambda i: (i, 0)),
        const(mod.shape),
        cur(0),
        prev_halo(1), cur(1), next_halo(1),
        prev_halo(2), cur(2), next_halo(2),
        pl.BlockSpec((POOL_HALO, POOL_W), lambda i: (jnp.maximum(i * ph_per_blk - 1, 0), 3)),
        cur(3),
        pl.BlockSpec((POOL_HALO, POOL_W), lambda i: (jnp.minimum((i + 1) * ph_per_blk, n_ph - 1), 3)),
        pl.BlockSpec((MIX_TQ, D_MODEL), lambda i: (i, 2)),
        pl.BlockSpec((MIX_TQ, D_MODEL), lambda i: (i, 3)),
        const(kvc.shape), const(bias.shape), const(wgrp.shape), const(pscale.shape),
        const(wap.shape), const(wpp.shape), const(wout.shape),
        const((1, D_MODEL)), const((1, D_MODEL)),
    ]
    return pl.pallas_call(
        _mix_kernel,
        grid=(nb,),
        in_specs=in_specs,
        out_specs=pl.BlockSpec((MIX_TQ * ROW_TILE, LANES), lambda i: (i, 0)),
        out_shape=jax.ShapeDtypeStruct((SEQ * ROW_TILE, LANES), jnp.float32),
        scratch_shapes=[
            pltpu.VMEM((MIX_TQ + 2 * KV_HALO, ATT_W), jnp.bfloat16),
            pltpu.VMEM((MIX_TQ + 2 * KV_HALO, ATT_W), jnp.bfloat16),
            pltpu.VMEM((MIX_TQ, ATT_W), jnp.bfloat16),
            pltpu.VMEM((MIX_TQ + 2 * POOL_HALO, POOL_W), jnp.float32),
            pltpu.VMEM((MIX_TQ, POOL_W), jnp.bfloat16),
            pltpu.VMEM((MIX_TQ, D_MODEL), jnp.float32),
        ],
        compiler_params=pltpu.CompilerParams(
            dimension_semantics=("arbitrary",), vmem_limit_bytes=VMEM_LIMIT),
        name="mix",
    )(h, mod, u, u, u, u, u, u, u, u, u, u, u, u,
      kvc, bias, wgrp, pscale, wap, wpp, wout, ln1g, ln1b)


ID_E0, ID_E1, ID_R0, ID_R1 = 0, 1, 4, 5
POS_PACKED = 0
POS_BITS = 16
PLAN_EXPERT, PLAN_VALID, PLAN_NACT, PLAN_NEXT = 0, 1, 2, 3
PLAN_W = 2 * LANES
CW_C0, CW_C1 = 0, 1


def _route_kernel(h_ref, mod_ref, wrt_ref, brt_ref, pos_ref, cw_ref, plan_ref, carry_ref, ids_all):
    i = pl.program_id(0)
    tm = ROUTE_TM

    @pl.when(i == 0)
    def _():
        carry_ref[...] = jnp.zeros_like(carry_ref)

    shift = mod_ref[0:1, 3 * D_MODEL:4 * D_MODEL]
    scale = mod_ref[0:1, 4 * D_MODEL:5 * D_MODEL]
    hm = _load_row_tiles(h_ref, tm) * (1.0 + scale) + shift

    hm_hi, hm_lo = _split_bf16(hm)
    w_hi, w_lo = _split_bf16(wrt_ref[...])
    nt = (((1,), (1,)), ((), ()))
    dg = functools.partial(lax.dot_general, dimension_numbers=nt, preferred_element_type=jnp.float32)
    logits = dg(w_hi, hm_hi) + (dg(w_hi, hm_lo) + dg(w_lo, hm_hi)) + brt_ref[:, 0:1]

    sub = lax.broadcasted_iota(jnp.int32, (LANES, tm), 0)
    big = jnp.int32(1 << 20)
    is_grp = sub < N_GROUPS
    gl = jnp.where(is_grp, logits, -jnp.inf)
    gmax = jnp.max(gl, axis=0, keepdims=True)
    gidx = jnp.min(jnp.where(gl == gmax, sub, big), axis=0, keepdims=True)
    gsum = jnp.sum(jnp.where(is_grp, jnp.exp(logits - gmax), 0.0), axis=0, keepdims=True)
    p_group = 1.0 / gsum

    eid = sub - N_GROUPS
    sel = (eid >= 0) & (eid < N_EXPERTS) & (lax.shift_right_arithmetic(eid, 3) == gidx)
    el = jnp.where(sel, logits, -jnp.inf)
    l0 = jnp.max(el, axis=0, keepdims=True)
    i0 = jnp.min(jnp.where(el == l0, sub, big), axis=0, keepdims=True)
    el2 = jnp.where(sub == i0, -jnp.inf, el)
    l1 = jnp.max(el2, axis=0, keepdims=True)
    i1 = jnp.min(jnp.where(el2 == l1, sub, big), axis=0, keepdims=True)
    t = jnp.exp(l1 - l0)
    w0 = 1.0 / (1.0 + t)
    w1 = t / (1.0 + t)

    half_rows = jnp.where(i >= pl.num_programs(0) // SEQ_PARTS, N_EXPERTS, 0)
    i0 = i0 + half_rows
    i1 = i1 + half_rows
    onehot = jnp.where((sub == i0) | (sub == i1), 1.0, 0.0)
    rr = lax.broadcasted_iota(jnp.int32, (tm, tm), 0)
    cc = lax.broadcasted_iota(jnp.int32, (tm, tm), 1)
    earlier = jnp.where(rr < cc, 1.0, 0.0).astype(jnp.bfloat16)
    carry = carry_ref[:, 0:1]
    prefix = _bdot(onehot.astype(jnp.bfloat16), earlier) + carry
    r0 = jnp.sum(jnp.where(sub == i0, prefix, 0.0), axis=0, keepdims=True)
    r1 = jnp.sum(jnp.where(sub == i1, prefix, 0.0), axis=0, keepdims=True)
    total = jnp.broadcast_to(carry + jnp.sum(onehot, axis=1, keepdims=True), carry_ref.shape)
    carry_ref[...] = total

    sub8 = lax.broadcasted_iota(jnp.int32, (ROW_TILE, tm), 0)
    ids = jnp.zeros((ROW_TILE, tm), jnp.int32)
    for idx, val in ((ID_E0, i0 - N_GROUPS), (ID_E1, i1 - N_GROUPS),
                     (ID_R0, r0.astype(jnp.int32)), (ID_R1, r1.astype(jnp.int32))):
        ids = jnp.where(sub8 == idx, val, ids)
    ids_all[:, pl.ds(pl.multiple_of(i * tm, tm), tm)] = ids

    cwt = jnp.where(sub == CW_C0, p_group * w0, jnp.where(sub == CW_C1, p_group * w1, 0.0))
    cw_ref[...] = cwt.T

    @pl.when(i == pl.num_programs(0) - 1)
    def _():
        subq = lax.broadcasted_iota(jnp.int32, (LANES, LANES), 0)
        laneq = lax.broadcasted_iota(jnp.int32, (LANES, LANES), 1)
        cnt = total.astype(jnp.int32)
        tiles = lax.shift_right_logical(cnt + (EXP_TM - 1), EXP_TM.bit_length() - 1).astype(jnp.float32)
        incl = jnp.where(laneq <= subq, 1.0, 0.0).astype(jnp.bfloat16)
        tile_end = _bdot(incl, tiles.astype(jnp.bfloat16))
        tile_start = tile_end - tiles
        seg = (tile_start * EXP_TM).astype(jnp.int32)
        nact = jnp.max(tile_end, axis=0, keepdims=True)

        ids_full = ids_all[...]
        look = jnp.zeros_like(ids_full)
        for e in range(N_SEG):
            look = jnp.where(ids_full == e, seg[N_GROUPS + e, 0], look)
        pos01 = look + pltpu.roll(ids_full, ID_R0 - ID_E0, axis=0)
        assert EXP_TILES * EXP_TM <= 1 << POS_BITS
        pos_ref[...] = pos01 | (pltpu.roll(pos01, ROW_TILE - 1, axis=0) << POS_BITS)

        subp = lax.broadcasted_iota(jnp.int32, (LANES, PLAN_W), 0)
        tile = lax.broadcasted_iota(jnp.int32, (LANES, PLAN_W), 1).astype(jnp.float32)
        is_exp = (subp >= N_GROUPS) & (subp < N_GROUPS + N_SEG)
        end_col = tile_end[:, 0:1]
        nact_s = nact[:, 0:1]
        te = jnp.sum(jnp.where(is_exp & (tile >= end_col), 1.0, 0.0), axis=0, keepdims=True)
        te_last = jnp.sum(jnp.where(is_exp & (nact_s - 1.0 >= end_col), 1.0, 0.0), axis=0, keepdims=True)[:, 0:1]
        tile_row = tile[0:1, :]
        te = jnp.minimum(jnp.where(tile_row < nact_s, te, te_last), N_SEG - 1.0)
        mine = (subp - N_GROUPS).astype(jnp.float32) == te
        cnt_sel = jnp.sum(jnp.where(mine, total[:, 0:1], 0.0), axis=0, keepdims=True)
        start_sel = jnp.sum(jnp.where(mine, tile_start[:, 0:1], 0.0), axis=0, keepdims=True)
        end_sel = jnp.sum(jnp.where(mine, end_col, 0.0), axis=0, keepdims=True)
        valid = jnp.clip(cnt_sel - (tile_row - start_sel) * EXP_TM, 0.0, float(EXP_TM))
        valid = jnp.where(tile_row < nact_s, valid, 0.0)
        subr = lax.broadcasted_iota(jnp.int32, (ROW_TILE, PLAN_W), 0)
        plan = jnp.where(subr == PLAN_EXPERT, te, jnp.where(subr == PLAN_VALID, valid,
                         jnp.where(subr == PLAN_NACT, nact_s, jnp.where(subr == PLAN_NEXT, end_sel, 0.0))))
        plan_ref[...] = plan.astype(jnp.int32)


def _route_call(h, mod, wrt, brt):
    tm = ROUTE_TM
    return pl.pallas_call(
        _route_kernel,
        grid=(SEQ // tm,),
        in_specs=[
            pl.BlockSpec((tm * ROW_TILE, LANES), lambda i: (i, 0)),
            pl.BlockSpec(mod.shape, lambda i: (0, 0)),
            pl.BlockSpec((LANES, D_MODEL), lambda i: (0, 0)),
            pl.BlockSpec((LANES, LANES), lambda i: (0, 0)),
        ],
        out_specs=[
            pl.BlockSpec((ROW_TILE, SEQ), lambda i: (0, 0)),
            pl.BlockSpec((tm, LANES), lambda i: (i, 0)),
            pl.BlockSpec((ROW_TILE, PLAN_W), lambda i: (0, 0)),
        ],
        out_shape=[
            jax.ShapeDtypeStruct((ROW_TILE, SEQ), jnp.int32),
            jax.ShapeDtypeStruct((SEQ, LANES), jnp.float32),
            jax.ShapeDtypeStruct((ROW_TILE, PLAN_W), jnp.int32),
        ],
        scratch_shapes=[pltpu.VMEM((LANES, LANES), jnp.float32),
                        pltpu.VMEM((ROW_TILE, SEQ), jnp.int32)],
        compiler_params=pltpu.CompilerParams(
            dimension_semantics=("arbitrary",), vmem_limit_bytes=VMEM_LIMIT),
        name="route",
    )(h, mod, wrt, brt)


SRC_UNROLL = 8
EXP_BUFS = 3
EXP_CHUNK = 256
TILE_ROWS = EXP_TM * ROW_TILE


def _experts_kernel(te_ref, tv_ref, nact_ref, tnext_ref, pos_ref,
                    h_hbm, mod_ref, wg_hbm, wu_hbm, wd_hbm,
                    y_hbm,
                    src_ref, hres, xbuf, ybuf, wgs, wus, wds, wgb, wub, wdb, rsem, ssem, wsem):
    i = pl.program_id(0)
    last = pl.num_programs(0) - 1
    nact = nact_ref[0]
    cur = lax.rem(i, EXP_BUFS)
    nxt2 = lax.rem(i + 2, EXP_BUFS)
    prv2 = lax.rem(i + 1, EXP_BUFS)
    xcur = lax.rem(i, 2)
    part = lax.shift_right_logical(te_ref[i], N_EXPERTS.bit_length() - 1)

    def scatter_copy(p, k, s):
        dst = src_ref[p]
        return pltpu.make_async_copy(ybuf.at[s, pl.ds(pl.multiple_of(k * ROW_TILE, ROW_TILE), ROW_TILE), :],
                                     y_hbm.at[pl.ds(pl.multiple_of(dst * ROW_TILE, ROW_TILE), ROW_TILE), :],
                                     ssem.at[s])

    def scatter_rolled(tile, s, n):
        def chunk(kk, c):
            for u in range(SRC_UNROLL):
                k = kk * SRC_UNROLL + u
                scatter_copy(tile * EXP_TM + k, k, s).start(priority=u % 2)
            return c
        full = lax.shift_right_logical(n, SRC_UNROLL.bit_length() - 1)
        lax.fori_loop(0, full, chunk, 0)

        def single(k, c):
            scatter_copy(tile * EXP_TM + k, k, s).start()
            return c
        lax.fori_loop(full * SRC_UNROLL, n, single, 0)

    def weight_copies(segment):
        e = segment & (N_EXPERTS - 1)
        return [pltpu.make_async_copy(w_hbm.at[e], stage, wsem.at[n])
                for n, (w_hbm, stage) in enumerate(((wg_hbm, wgs), (wu_hbm, wus), (wd_hbm, wds)))]

    def gather_row(tile, k, s):
        local = (src_ref[tile * EXP_TM + k] - part * PART_TOKENS) & (PART_TOKENS - 1)
        xbuf[s, k * ROW_TILE:(k + 1) * ROW_TILE, :] = hres[pl.ds(pl.multiple_of(local * ROW_TILE, ROW_TILE),
                                                             ROW_TILE), :]

    def gather_items(tile, s):
        return [functools.partial(gather_row, tile, k, s) for k in range(EXP_TM)]

    def wait_scatter(s, n):
        rows = pl.multiple_of(n * ROW_TILE, ROW_TILE)
        pltpu.make_async_copy(ybuf.at[s, pl.ds(0, rows), :], y_hbm.at[pl.ds(0, rows), :], ssem.at[s]).wait()

    def compute_chunks(s, xs):
        state = {}

        def load():
            x = _load_row_tiles(xbuf, EXP_TM, lead=(xs,))
            shift = mod_ref[0:1, 3 * D_MODEL:4 * D_MODEL]
            scale = mod_ref[0:1, 4 * D_MODEL:5 * D_MODEL]
            state["x"] = (x * (1.0 + scale) + shift).astype(jnp.bfloat16)
            state["act"] = []

        def gate(c):
            def run():
                if c == 0:
                    load()
                state["a"] = _bdot(state["x"], wgb[:, c * EXP_CHUNK:(c + 1) * EXP_CHUNK])
            return run

        def up(c):
            def run():
                a = state["a"]
                u = _bdot(state["x"], wub[:, c * EXP_CHUNK:(c + 1) * EXP_CHUNK])
                state["act"].append((a * jax.nn.sigmoid(a) * u).astype(jnp.bfloat16))
            return run

        def down(c):
            def run():
                if c == 0:
                    state["actf"] = jnp.concatenate(state["act"], axis=-1)
                yc = _bdot(state["actf"], wdb[:, c * EXP_CHUNK:(c + 1) * EXP_CHUNK])
                for jj in range(EXP_CHUNK // LANES):
                    j = c * (EXP_CHUNK // LANES) + jj
                    ybuf[s, pl.ds(j, EXP_TM, stride=ROW_TILE), :] = yc[:, jj * LANES:(jj + 1) * LANES]
            return run

        steps = []
        for c in range(D_EXPERT // EXP_CHUNK):
            steps += [(gate(c), 2), (up(c), 2)]
        return steps + [(down(c), 1) for c in range(D_MODEL // EXP_CHUNK)]

    def run_interleaved(chunks, dmas):
        total_cost = sum(cost for _, cost in chunks)
        done = 0
        for chunk, cost in chunks:
            upto = -(-len(dmas) * (done + cost) // total_cost)
            for d in dmas[-(-len(dmas) * done // total_cost):upto]:
                d()
            done += cost
            chunk()

    def scatter_dmas(tile, s):
        return [functools.partial(lambda k: scatter_copy(tile * EXP_TM + k, k, s).start(priority=k % 2), k)
                for k in range(EXP_TM)]

    @pl.when(i == 0)
    def _():
        for cp in weight_copies(te_ref[0]):
            cp.start()

        def fill_body(tt, c):
            ts = [tt * SRC_UNROLL + u for u in range(SRC_UNROLL)]
            words = [pos_ref[t] for t in ts]
            for t, w in zip(ts, words):
                src_ref[w & ((1 << POS_BITS) - 1)] = t
                src_ref[lax.shift_right_logical(w, POS_BITS)] = t + SEQ
            return c
        lax.fori_loop(0, SEQ // SRC_UNROLL, fill_body, 0)

        def pad_tile(t, c):
            pad_tok = lax.shift_right_logical(te_ref[t], N_EXPERTS.bit_length() - 1) * PART_TOKENS

            def pad_row(k, c2):
                src_ref[t * EXP_TM + k] = pad_tok
                return c2
            return lax.fori_loop(tv_ref[t], EXP_TM, pad_row, c)
        lax.fori_loop(0, nact, pad_tile, 0)

    active = i < nact
    prev = jnp.maximum(i - 1, 0)
    prev_ok = (i >= 1) & (i - 1 < nact)
    prev2 = jnp.maximum(i - 2, 0)
    prev2_ok = (i >= 2) & (i - 2 < nact)
    nprev = tv_ref[prev]
    full_prev = prev_ok & (nprev == EXP_TM)
    new_segment = (i == 0) | (te_ref[i] != te_ref[prev])

    @pl.when(prev2_ok)
    def _():
        wait_scatter(prv2, tv_ref[prev2])

    @pl.when(active & ((i == 0) | (part != lax.shift_right_logical(te_ref[prev], N_EXPERTS.bit_length() - 1))))
    def _():
        rows = PART_TOKENS * ROW_TILE
        cp = pltpu.make_async_copy(h_hbm.at[pl.ds(pl.multiple_of(part * rows, rows), rows), :], hres, rsem.at[0])
        cp.start()
        cp.wait()
        for item in gather_items(i, xcur):
            item()

    @pl.when(active & new_segment)
    def _():
        for cp in weight_copies(te_ref[i]):
            cp.wait()
        wgb[...] = wgs[...].astype(jnp.bfloat16)
        wub[...] = wus[...].astype(jnp.bfloat16)
        wdb[...] = wds[...].astype(jnp.bfloat16)
        nxt = tnext_ref[i]

        @pl.when(nxt < nact)
        def _():
            for cp in weight_copies(te_ref[jnp.minimum(nxt, last)]):
                cp.start()

    @pl.when(prev_ok & jnp.logical_not(active & full_prev))
    def _():
        scatter_rolled(prev, nxt2, nprev)

    tile_next = jnp.minimum(i + 1, nact - 1)

    @pl.when(active & full_prev)
    def _():
        sc, g = scatter_dmas(prev, nxt2), gather_items(tile_next, 1 - xcur)
        run_interleaved(compute_chunks(cur, xcur), [item for pair in zip(sc, g) for item in pair])

    @pl.when(active & jnp.logical_not(full_prev))
    def _():
        run_interleaved(compute_chunks(cur, xcur), gather_items(tile_next, 1 - xcur))

    @pl.when(i == last)
    def _():
        @pl.when(prev_ok)
        def _():
            wait_scatter(nxt2, nprev)

        @pl.when(active)
        def _():
            scatter_rolled(i, cur, tv_ref[i])
            wait_scatter(cur, tv_ref[i])


def _experts_call(te, tv, nact, tnext, pos, h, mod, wg, wu, wd):
    grid_spec = pltpu.PrefetchScalarGridSpec(
        num_scalar_prefetch=5,
        grid=(EXP_TILES,),
        in_specs=[
            pl.BlockSpec(memory_space=pl.ANY),
            pl.BlockSpec(mod.shape, lambda i, *_: (0, 0)),
            pl.BlockSpec(memory_space=pl.ANY),
            pl.BlockSpec(memory_space=pl.ANY),
            pl.BlockSpec(memory_space=pl.ANY),
        ],
        out_specs=pl.BlockSpec(memory_space=pl.ANY),
        scratch_shapes=[
            pltpu.SMEM((EXP_TILES * EXP_TM,), jnp.int32),
            pltpu.VMEM((PART_TOKENS * ROW_TILE, LANES), jnp.float32),
            pltpu.VMEM((2, TILE_ROWS, LANES), jnp.float32),
            pltpu.VMEM((EXP_BUFS, TILE_ROWS, LANES), jnp.float32),
            pltpu.VMEM((D_MODEL, D_EXPERT), jnp.float32),
            pltpu.VMEM((D_MODEL, D_EXPERT), jnp.float32),
            pltpu.VMEM((D_EXPERT, D_MODEL), jnp.float32),
            pltpu.VMEM((D_MODEL, D_EXPERT), jnp.bfloat16),
            pltpu.VMEM((D_MODEL, D_EXPERT), jnp.bfloat16),
            pltpu.VMEM((D_EXPERT, D_MODEL), jnp.bfloat16),
            pltpu.SemaphoreType.DMA((1,)),
            pltpu.SemaphoreType.DMA((EXP_BUFS,)),
            pltpu.SemaphoreType.DMA((3,)),
        ],
    )
    return pl.pallas_call(
        _experts_kernel,
        grid_spec=grid_spec,
        out_shape=jax.ShapeDtypeStruct((2 * SEQ * ROW_TILE, LANES), jnp.float32),
        compiler_params=pltpu.CompilerParams(
            dimension_semantics=("arbitrary",), vmem_limit_bytes=EXPERTS_VMEM_LIMIT),
        name="experts",
    )(te, tv, nact, tnext, pos, h, mod, wg, wu, wd)


def _combine_kernel(h_ref, y0_ref, y1_ref, cw_ref, mod_ref, g_ref, b_ref, o_ref):
    tm = o_ref.shape[0]
    c0 = cw_ref[:, CW_C0:CW_C0 + 1]
    c1 = cw_ref[:, CW_C1:CW_C1 + 1]
    ffn = c0 * _load_row_tiles(y0_ref, tm) + c1 * _load_row_tiles(y1_ref, tm)
    g2 = mod_ref[0:1, 5 * D_MODEL:6 * D_MODEL]
    o_ref[...] = _layer_norm(DEEPNORM_ALPHA * _load_row_tiles(h_ref, tm) + g2 * ffn, g_ref[...], b_ref[...])


def _combine_call(h, y, cw, mod, g, b):
    tm = CMB_TM
    nblk = SEQ // tm
    tiles = lambda off: pl.BlockSpec((tm * ROW_TILE, LANES), lambda i: (i + off, 0))
    return pl.pallas_call(
        _combine_kernel,
        grid=(nblk,),
        in_specs=[
            tiles(0), tiles(0), tiles(nblk),
            pl.BlockSpec((tm, LANES), lambda i: (i, 0)),
            pl.BlockSpec(mod.shape, lambda i: (0, 0)),
            pl.BlockSpec((1, D_MODEL), lambda i: (0, 0)),
            pl.BlockSpec((1, D_MODEL), lambda i: (0, 0)),
        ],
        out_specs=pl.BlockSpec((tm, D_MODEL), lambda i: (i, 0)),
        out_shape=jax.ShapeDtypeStruct((SEQ, D_MODEL), jnp.float32),
        compiler_params=pltpu.CompilerParams(
            dimension_semantics=("arbitrary",), vmem_limit_bytes=VMEM_LIMIT),
        name="combine",
    )(h, y, y, cw, mod, g, b)


def kernel(x, c, ctx, c_ctx, ln_in_g, ln_in_b, w_mod, b_mod, w_in, rpb, w_pool_grp, pool_scale,
           w_attn_proj, w_pool_proj, w_out, ln1_g, ln1_b, w_router_group, b_router_group,
           w_router_expert, b_router_expert, w_expert_gate, w_expert_up, w_expert_down, ln2_g, ln2_b):
    assert x.shape == (1, SEQ, D_MODEL) and ctx.shape == (1, CTX_LEN, D_MODEL)
    assert w_mod.shape[0] == 1, "single-layer trunk"
    f32, bf16 = jnp.float32, jnp.bfloat16
    row = lambda v: v.reshape(1, -1).astype(f32)

    cond = jnp.concatenate([c, c_ctx[None], jnp.zeros((MOD_ROWS - 2, D_MODEL), f32)], axis=0)
    mod = _mod_call(cond, w_mod[0], row(b_mod[0]))

    lng, lnb = row(ln_in_g), row(ln_in_b)
    w_in_b = w_in[0].astype(bf16)
    u, h0 = _proj_call(x[0], mod, lng, lnb, w_in_b, mod_row=0, latent=True, tm=PROJ_TM)
    kvc, = _proj_call(ctx[0], mod, lng, lnb, w_in_b[:, ATT_W:3 * ATT_W], mod_row=1, latent=False, tm=CTX_LEN)

    h1 = _mix_call(h0, mod, u, kvc, _attn_bias_table(rpb[0]),
                   w_pool_grp[0].astype(bf16), row(pool_scale[0]),
                   w_attn_proj[0].astype(bf16), w_pool_proj[0].astype(bf16), w_out[0].astype(bf16),
                   row(ln1_g[0]), row(ln1_b[0]))

    n_logit = N_GROUPS + N_EXPERTS
    wrt = jnp.concatenate([w_router_group[0].T, w_router_expert[0].T,
                           jnp.zeros((LANES - n_logit, D_MODEL), f32)], axis=0)
    brt = jnp.concatenate([b_router_group[0], b_router_expert[0], jnp.zeros((LANES - n_logit,), f32)])
    brt = jnp.broadcast_to(brt[:, None], (LANES, LANES))
    pos, cw, plan = _route_call(h1, mod, wrt, brt)

    y = _experts_call(plan[PLAN_EXPERT, :EXP_TILES], plan[PLAN_VALID, :EXP_TILES], plan[PLAN_NACT, :1],
                      plan[PLAN_NEXT, :EXP_TILES], pos[POS_PACKED], h1, mod,
                      w_expert_gate[0], w_expert_up[0], w_expert_down[0])
    out = _combine_call(h1, y, cw, mod, row(ln2_g[0]), row(ln2_b[0]))
    return out[None]
```

```python
import functools

import jax
import jax.numpy as jnp
from jax import lax
from jax.experimental import pallas as pl
from jax.experimental.pallas import tpu as pltpu

D_MODEL = 1024
SEQ = 16384
GRID_W = 64
ROWS = SEQ // GRID_W
CTX_LEN = 256
N_HEADS = 8
HEAD_DIM = 64
ATT_W = N_HEADS * HEAD_DIM
WIN_H = 8
WIN_W = 16
POOL_WINDOWS = (2, 4, 8, 16)
POOL_GROUPS = 4
POOL_DIM = 128
POOL_W = POOL_GROUPS * POOL_DIM
PROJ_W = 3 * ATT_W + POOL_W + 2 * D_MODEL
GATE_COL = 3 * ATT_W + POOL_W
N_GROUPS = 4
EXPERTS_PER_GROUP = 8
N_EXPERTS = N_GROUPS * EXPERTS_PER_GROUP
D_EXPERT = 512
N_MOD = 6
DEEPNORM_ALPHA = 2.0 ** 0.25
LN_EPS = 1e-5
NEG_INF = -1e30

LANES = 128
ROW_TILE = 8
MOD_ROWS = 8
PROJ_TM = 512
PROJ_SUB = 256
MIX_ROWS = 8
MIX_TQ = MIX_ROWS * GRID_W
KV_HALO = 4 * GRID_W
POOL_HALO = 16
ROUTE_TM = 512
EXP_TM = 256
SEQ_PARTS = 2
PART_TOKENS = SEQ // SEQ_PARTS
N_SEG = SEQ_PARTS * N_EXPERTS
EXP_TILES = 2 * SEQ // EXP_TM + N_SEG
CMB_TM = 512
MERGE_ROWS = 128
ATTN_AHEAD = 2
HALF = D_MODEL // 2
VMEM_LIMIT = 56 * 1024 * 1024
EXPERTS_VMEM_LIMIT = 60 * 1024 * 1024


def _layer_norm(x, g, b):
    mu = jnp.mean(x, axis=-1, keepdims=True)
    xc = x - mu
    var = jnp.mean(xc * xc, axis=-1, keepdims=True)
    return xc * lax.rsqrt(var + LN_EPS) * g + b


def _bdot(a, b):
    return jnp.dot(a, b, preferred_element_type=jnp.float32)


def _split_bf16(a):
    hi = a.astype(jnp.bfloat16)
    lo = (a - hi.astype(jnp.float32)).astype(jnp.bfloat16)
    return hi, lo


def _dot3(a, b):
    a_hi, a_lo = _split_bf16(a)
    b_hi, b_lo = _split_bf16(b)
    return _bdot(a_hi, b_hi) + (_bdot(a_hi, b_lo) + _bdot(a_lo, b_hi))


def _load_row_tiles(ref, tokens, lead=()):
    parts = [ref[(*lead, pl.ds(j, tokens, stride=ROW_TILE), slice(None))] for j in range(ROW_TILE)]
    return jnp.concatenate(parts, axis=-1)


def _store_row_tiles(ref, value, lead=()):
    tokens = value.shape[0]
    for j in range(ROW_TILE):
        ref[(*lead, pl.ds(j, tokens, stride=ROW_TILE), slice(None))] = value[:, j * LANES:(j + 1) * LANES]


def _mod_kernel(cond_ref, w_ref, b_ref, o_ref):
    cond = cond_ref[...]
    act = cond * jax.nn.sigmoid(cond)
    o_ref[...] = _dot3(act, w_ref[...]) + b_ref[...]


def _mod_call(cond, w_mod, b_mod):
    tn = 1536
    n = N_MOD * D_MODEL
    return pl.pallas_call(
        _mod_kernel,
        grid=(n // tn,),
        in_specs=[
            pl.BlockSpec((MOD_ROWS, D_MODEL), lambda i: (0, 0)),
            pl.BlockSpec((D_MODEL, tn), lambda i: (0, i)),
            pl.BlockSpec((1, tn), lambda i: (0, i)),
        ],
        out_specs=pl.BlockSpec((MOD_ROWS, tn), lambda i: (0, i)),
        out_shape=jax.ShapeDtypeStruct((MOD_ROWS, n), jnp.float32),
        compiler_params=pltpu.CompilerParams(
            dimension_semantics=("arbitrary",), vmem_limit_bytes=VMEM_LIMIT),
        name="mod",
    )(cond, w_mod, b_mod)


def _proj_kernel(x_ref, mod_ref, g_ref, b_ref, w_ref, o_ref, *h_out, mod_row, latent):
    shift = mod_ref[mod_row:mod_row + 1, 0:D_MODEL]
    scale = mod_ref[mod_row:mod_row + 1, D_MODEL:2 * D_MODEL]
    tm, n = o_ref.shape
    sub = min(tm, PROJ_SUB)

    def prep(r):
        rows = slice(r * sub, (r + 1) * sub)
        h = _layer_norm(x_ref[rows, :], g_ref[...], b_ref[...])
        if latent:
            h_out[0][rows, :] = h
        return (h * (1.0 + scale) + shift).astype(jnp.bfloat16)

    def finish(r, c, res):
        if latent and c == 0:
            lane = lax.broadcasted_iota(jnp.int32, (1, D_MODEL), 1)
            res = res * jnp.where(lane < ATT_W, HEAD_DIM ** -0.5, 1.0)
        if latent and c * D_MODEL >= GATE_COL:
            res = jax.nn.sigmoid(res)
        o_ref[r * sub:(r + 1) * sub, c * D_MODEL:(c + 1) * D_MODEL] = res.astype(jnp.bfloat16)

    hm = {0: prep(0)}
    waiting = None
    for r in range(tm // sub):
        for c in range(n // D_MODEL):
            res = _bdot(hm[r], w_ref[:, c * D_MODEL:(c + 1) * D_MODEL])
            if c == 0 and (r + 1) * sub < tm:
                hm[r + 1] = prep(r + 1)
            if waiting is not None:
                finish(*waiting)
            waiting = (r, c, res)
    finish(*waiting)


def _proj_call(x, mod, g, b, w, *, mod_row, latent, tm):
    rows, n = x.shape[0], w.shape[1]
    out_specs = [pl.BlockSpec((tm, n), lambda i: (i, 0))]
    out_shape = [jax.ShapeDtypeStruct((rows, n), jnp.bfloat16)]
    if latent:
        out_specs.append(pl.BlockSpec((tm, D_MODEL), lambda i: (i, 0)))
        out_shape.append(jax.ShapeDtypeStruct((rows, D_MODEL), jnp.float32))
    return pl.pallas_call(
        functools.partial(_proj_kernel, mod_row=mod_row, latent=latent),
        grid=(rows // tm,),
        in_specs=[
            pl.BlockSpec((tm, D_MODEL), lambda i: (i, 0)),
            pl.BlockSpec(mod.shape, lambda i: (0, 0)),
            pl.BlockSpec((1, D_MODEL), lambda i: (0, 0)),
            pl.BlockSpec((1, D_MODEL), lambda i: (0, 0)),
            pl.BlockSpec((D_MODEL, n), lambda i: (0, 0), pipeline_mode=pl.Buffered(1)),
        ],
        out_specs=out_specs,
        out_shape=out_shape,
        compiler_params=pltpu.CompilerParams(
            dimension_semantics=("arbitrary",), vmem_limit_bytes=VMEM_LIMIT),
        name="proj",
    )(x, mod, g, b, w)


def _attn_bias_table(rpb):
    col = jnp.arange(GRID_W, dtype=jnp.int32)
    col_start = jnp.clip(col - WIN_W // 2, 0, GRID_W - WIN_W)
    col_mask = (col[None, :] >= col_start[:, None]) & (col[None, :] < col_start[:, None] + WIN_W)
    col_off = jnp.clip(col[None, :] - col[:, None], 1 - WIN_W, WIN_W - 1) + (WIN_W - 1)
    onehot = (col_off[None] == jnp.arange(2 * WIN_W - 1, dtype=jnp.int32)[:, None, None]).astype(jnp.float32)
    tab = jnp.einsum("hrc,cqk->hqrk", rpb.astype(jnp.float32), onehot, precision=lax.Precision.HIGHEST)
    tab = jnp.where(col_mask[None, :, None, :], tab, NEG_INF)
    tab = jnp.stack([tab[:, :, WIN_H - 1 - v:2 * WIN_H - 1 - v] for v in range(WIN_H)], axis=0)
    return tab.reshape(WIN_H, N_HEADS // 2, 2 * GRID_W, WIN_H * GRID_W)


def _mix_kernel(h_ref, mod_ref,
                q_ref, kp_ref, kc_ref, kn_ref, vp_ref, vc_ref, vn_ref,
                pp_ref, pc_ref, pn_ref, ga_ref, gb_ref,
                kvc_ref, bias_ref, wgrp_ref, pscale_ref, wap_ref, wpp_ref, wout_ref,
                ln1g_ref, ln1b_ref,
                o_ref,
                kbuf, vbuf, yabuf, pbuf, ypbuf, zbuf):
    b = pl.program_id(0)
    nb = pl.num_programs(0)

    kbuf[0:KV_HALO, :] = kp_ref[...]
    kbuf[KV_HALO:KV_HALO + MIX_TQ, :] = kc_ref[...]
    kbuf[KV_HALO + MIX_TQ:, :] = kn_ref[...]
    vbuf[0:KV_HALO, :] = vp_ref[...]
    vbuf[KV_HALO:KV_HALO + MIX_TQ, :] = vc_ref[...]
    vbuf[KV_HALO + MIX_TQ:, :] = vn_ref[...]

    lane = lax.broadcasted_iota(jnp.int32, (GRID_W, LANES), 1)
    first_head = lane < HEAD_DIM

    units = [(j, pair) for j in range(MIX_ROWS) for pair in range(N_HEADS // 2)]
    nt = (((1,), (1,)), ((), ()))

    def window(j):
        r = b * MIX_ROWS + j
        rs = jnp.clip(r - WIN_H // 2, 0, ROWS - WIN_H)
        return pl.multiple_of((rs - b * MIX_ROWS + WIN_H // 2) * GRID_W, GRID_W), r - rs

    def scores(j, pair):
        off, var = window(j)
        cols = slice(pair * LANES, (pair + 1) * LANES)
        q = q_ref[j * GRID_W:(j + 1) * GRID_W, cols]
        zero = jnp.zeros_like(q)
        q2 = jnp.concatenate([jnp.where(first_head, q, zero), jnp.where(first_head, zero, q)], axis=0)
        kw = kbuf[pl.ds(off, WIN_H * GRID_W), cols]
        s_loc = lax.dot_general(q2, kw, nt, preferred_element_type=jnp.float32) + bias_ref[var, pair]
        s_ctx = lax.dot_general(q2, kvc_ref[:, cols], nt, preferred_element_type=jnp.float32)
        return s_loc, s_ctx

    def values(j, pair, s_loc, s_ctx):
        off, _ = window(j)
        cols = slice(pair * LANES, (pair + 1) * LANES)
        vw = vbuf[pl.ds(off, WIN_H * GRID_W), cols]
        vctx = kvc_ref[:, ATT_W + pair * LANES:ATT_W + (pair + 1) * LANES]
        m = jnp.maximum(jnp.max(s_loc, axis=-1, keepdims=True), jnp.max(s_ctx, axis=-1, keepdims=True))
        p_loc = jnp.exp(s_loc - m)
        p_ctx = jnp.exp(s_ctx - m)
        denom = jnp.sum(p_loc, axis=-1, keepdims=True) + jnp.sum(p_ctx, axis=-1, keepdims=True)
        o2 = _bdot(p_loc.astype(jnp.bfloat16), vw) + _bdot(p_ctx.astype(jnp.bfloat16), vctx)
        o2 = o2 * (1.0 / denom)
        o_pair = jnp.where(first_head, o2[:GRID_W], o2[GRID_W:])
        yabuf[j * GRID_W:(j + 1) * GRID_W, cols] = o_pair.astype(jnp.bfloat16)

    pbuf[0:POOL_HALO, :] = jnp.where(b > 0, pp_ref[...].astype(jnp.float32), 0.0)
    pbuf[POOL_HALO:POOL_HALO + MIX_TQ, :] = pc_ref[...].astype(jnp.float32)
    pbuf[POOL_HALO + MIX_TQ:, :] = jnp.where(b < nb - 1, pn_ref[...].astype(jnp.float32), 0.0)
    edge = lax.broadcasted_iota(jnp.int32, (ROW_TILE, 1), 0)

    def pool_group(g):
        win = POOL_WINDOWS[g]
        lo, hi = win // 2, win - win // 2
        cols = slice(g * POOL_DIM, (g + 1) * POOL_DIM)
        acc = None
        for d in range(-lo, hi):
            term = pbuf[POOL_HALO + d:POOL_HALO + d + MIX_TQ, cols]
            acc = term if acc is None else acc + term
        assert max(lo, hi) <= ROW_TILE
        top = jnp.where(b == 0, 1.0 / (win - jnp.maximum(lo - edge, 0)).astype(jnp.float32), 1.0 / win)
        bot = jnp.where(b == nb - 1,
                        1.0 / (win - jnp.maximum(edge + hi - ROW_TILE, 0)).astype(jnp.float32), 1.0 / win)
        inv = jnp.concatenate([top, jnp.full((MIX_TQ - 2 * ROW_TILE, 1), 1.0 / win, jnp.float32), bot], axis=0)
        pooled = acc * inv - pbuf[POOL_HALO:POOL_HALO + MIX_TQ, cols]
        yp = _bdot(pooled.astype(jnp.bfloat16), wgrp_ref[g]) * pscale_ref[:, cols]
        ypbuf[:, cols] = yp.astype(jnp.bfloat16)

    def pooled_branch():
        zbuf[...] = gb_ref[...].astype(jnp.float32) * _bdot(ypbuf[...], wpp_ref[...])

    extra = {}
    for g in range(POOL_GROUPS):
        extra[(g + 1) * len(units) // (POOL_GROUPS + 2)] = functools.partial(pool_group, g)
    extra[(POOL_GROUPS + 1) * len(units) // (POOL_GROUPS + 2)] = pooled_branch
    pending = [scores(*u) for u in units[:ATTN_AHEAD]]
    for n, u in enumerate(units):
        if n + ATTN_AHEAD < len(units):
            pending.append(scores(*units[n + ATTN_AHEAD]))
        values(*u, *pending.pop(0))
        if n in extra:
            extra[n]()

    g1 = mod_ref[0:1, 2 * D_MODEL:3 * D_MODEL]
    n_chunks = MIX_TQ // MERGE_ROWS
    rows = lambda c: slice(c * MERGE_ROWS, (c + 1) * MERGE_ROWS)
    z, y = {}, {}

    def stage_a(c):
        ya = _bdot(yabuf[rows(c), :], wap_ref[...])
        z[c] = (ga_ref[rows(c), :].astype(jnp.float32) * ya + zbuf[rows(c), :]).astype(jnp.bfloat16)

    def stage_b(c):
        y[c] = _bdot(z.pop(c), wout_ref[...])

    def stage_c(c):
        hn = _layer_norm(DEEPNORM_ALPHA * h_ref[rows(c), :] + g1 * y.pop(c), ln1g_ref[...], ln1b_ref[...])
        for j in range(ROW_TILE):
            o_ref[pl.ds(c * MERGE_ROWS * ROW_TILE + j, MERGE_ROWS, stride=ROW_TILE), :] = hn[:, j * LANES:(j + 1) * LANES]

    for t in range(n_chunks + 2):
        if t < n_chunks:
            stage_a(t)
        if 0 <= t - 1 < n_chunks:
            stage_b(t - 1)
        if 0 <= t - 2 < n_chunks:
            stage_c(t - 2)


def _mix_call(h, mod, u, kvc, bias, wgrp, pscale, wap, wpp, wout, ln1g, ln1b):
    nb = SEQ // MIX_TQ
    halo_per_blk = MIX_TQ // KV_HALO
    n_halo = SEQ // KV_HALO
    ph_per_blk = MIX_TQ // POOL_HALO
    n_ph = SEQ // POOL_HALO

    def const(shape):
        return pl.BlockSpec(shape, lambda i: (0,) * len(shape), pipeline_mode=pl.Buffered(1))

    def prev_halo(c):
        return pl.BlockSpec((KV_HALO, ATT_W), lambda i: (jnp.maximum(i * halo_per_blk - 1, 0), c))

    def next_halo(c):
        return pl.BlockSpec((KV_HALO, ATT_W), lambda i: (jnp.minimum((i + 1) * halo_per_blk, n_halo - 1), c))

    def cur(c):
        return pl.BlockSpec((MIX_TQ, ATT_W), lambda i: (i, c))

    in_specs = [
        pl.BlockSpec((MIX_TQ, D_MODEL), lambda i: (i, 0)),
        const(mod.shape),
        cur(0),
        prev_halo(1), cur(1), next_halo(1),
        prev_halo(2), cur(2), next_halo(2),
        pl.BlockSpec((POOL_HALO, POOL_W), lambda i: (jnp.maximum(i * ph_per_blk - 1, 0), 3)),
        cur(3),
        pl.BlockSpec((POOL_HALO, POOL_W), lambda i: (jnp.minimum((i + 1) * ph_per_blk, n_ph - 1), 3)),
        pl.BlockSpec((MIX_TQ, D_MODEL), lambda i: (i, 2)),
        pl.BlockSpec((MIX_TQ, D_MODEL), lambda i: (i, 3)),
        const(kvc.shape), const(bias.shape), const(wgrp.shape), const(pscale.shape),
        const(wap.shape), const(wpp.shape), const(wout.shape),
        const((1, D_MODEL)), const((1, D_MODEL)),
    ]
    return pl.pallas_call(
        _mix_kernel,
        grid=(nb,),
        in_specs=in_specs,
        out_specs=pl.BlockSpec((MIX_TQ * ROW_TILE, LANES), lambda i: (i, 0)),
        out_shape=jax.ShapeDtypeStruct((SEQ * ROW_TILE, LANES), jnp.float32),
        scratch_shapes=[
            pltpu.VMEM((MIX_TQ + 2 * KV_HALO, ATT_W), jnp.bfloat16),
            pltpu.VMEM((MIX_TQ + 2 * KV_HALO, ATT_W), jnp.bfloat16),
            pltpu.VMEM((MIX_TQ, ATT_W), jnp.bfloat16),
            pltpu.VMEM((MIX_TQ + 2 * POOL_HALO, POOL_W), jnp.float32),
            pltpu.VMEM((MIX_TQ, POOL_W), jnp.bfloat16),
            pltpu.VMEM((MIX_TQ, D_MODEL), jnp.float32),
        ],
        compiler_params=pltpu.CompilerParams(
            dimension_semantics=("arbitrary",), vmem_limit_bytes=VMEM_LIMIT),
        name="mix",
    )(h, mod, u, u, u, u, u, u, u, u, u, u, u, u,
      kvc, bias, wgrp, pscale, wap, wpp, wout, ln1g, ln1b)


ID_E0, ID_E1, ID_R0, ID_R1 = 0, 1, 4, 5
POS_PACKED = 0
POS_BITS = 16
PLAN_EXPERT, PLAN_VALID, PLAN_NACT, PLAN_NEXT = 0, 1, 2, 3
PLAN_W = 2 * LANES
CW_C0, CW_C1 = 0, 1


def _route_kernel(h_ref, mod_ref, wrt_ref, brt_ref, pos_ref, cw_ref, plan_ref, carry_ref, ids_all):
    i = pl.program_id(0)
    tm = ROUTE_TM

    @pl.when(i == 0)
    def _():
        carry_ref[...] = jnp.zeros_like(carry_ref)

    shift = mod_ref[0:1, 3 * D_MODEL:4 * D_MODEL]
    scale = mod_ref[0:1, 4 * D_MODEL:5 * D_MODEL]
    hm = _load_row_tiles(h_ref, tm) * (1.0 + scale) + shift

    hm_hi, hm_lo = _split_bf16(hm)
    w_hi, w_lo = _split_bf16(wrt_ref[...])
    nt = (((1,), (1,)), ((), ()))
    dg = functools.partial(lax.dot_general, dimension_numbers=nt, preferred_element_type=jnp.float32)
    logits = dg(w_hi, hm_hi) + (dg(w_hi, hm_lo) + dg(w_lo, hm_hi)) + brt_ref[:, 0:1]

    sub = lax.broadcasted_iota(jnp.int32, (LANES, tm), 0)
    big = jnp.int32(1 << 20)
    is_grp = sub < N_GROUPS
    gl = jnp.where(is_grp, logits, -jnp.inf)
    gmax = jnp.max(gl, axis=0, keepdims=True)
    gidx = jnp.min(jnp.where(gl == gmax, sub, big), axis=0, keepdims=True)
    gsum = jnp.sum(jnp.where(is_grp, jnp.exp(logits - gmax), 0.0), axis=0, keepdims=True)
    p_group = 1.0 / gsum

    eid = sub - N_GROUPS
    sel = (eid >= 0) & (eid < N_EXPERTS) & (lax.shift_right_arithmetic(eid, 3) == gidx)
    el = jnp.where(sel, logits, -jnp.inf)
    l0 = jnp.max(el, axis=0, keepdims=True)
    i0 = jnp.min(jnp.where(el == l0, sub, big), axis=0, keepdims=True)
    el2 = jnp.where(sub == i0, -jnp.inf, el)
    l1 = jnp.max(el2, axis=0, keepdims=True)
    i1 = jnp.min(jnp.where(el2 == l1, sub, big), axis=0, keepdims=True)
    t = jnp.exp(l1 - l0)
    w0 = 1.0 / (1.0 + t)
    w1 = t / (1.0 + t)

    half_rows = jnp.where(i >= pl.num_programs(0) // SEQ_PARTS, N_EXPERTS, 0)
    i0 = i0 + half_rows
    i1 = i1 + half_rows
    onehot = jnp.where((sub == i0) | (sub == i1), 1.0, 0.0)
    rr = lax.broadcasted_iota(jnp.int32, (tm, tm), 0)
    cc = lax.broadcasted_iota(jnp.int32, (tm, tm), 1)
    earlier = jnp.where(rr < cc, 1.0, 0.0).astype(jnp.bfloat16)
    carry = carry_ref[:, 0:1]
    prefix = _bdot(onehot.astype(jnp.bfloat16), earlier) + carry
    r0 = jnp.sum(jnp.where(sub == i0, prefix, 0.0), axis=0, keepdims=True)
    r1 = jnp.sum(jnp.where(sub == i1, prefix, 0.0), axis=0, keepdims=True)
    total = jnp.broadcast_to(carry + jnp.sum(onehot, axis=1, keepdims=True), carry_ref.shape)
    carry_ref[...] = total

    sub8 = lax.broadcasted_iota(jnp.int32, (ROW_TILE, tm), 0)
    ids = jnp.zeros((ROW_TILE, tm), jnp.int32)
    for idx, val in ((ID_E0, i0 - N_GROUPS), (ID_E1, i1 - N_GROUPS),
                     (ID_R0, r0.astype(jnp.int32)), (ID_R1, r1.astype(jnp.int32))):
        ids = jnp.where(sub8 == idx, val, ids)
    ids_all[:, pl.ds(pl.multiple_of(i * tm, tm), tm)] = ids

    cwt = jnp.where(sub == CW_C0, p_group * w0, jnp.where(sub == CW_C1, p_group * w1, 0.0))
    cw_ref[...] = cwt.T

    @pl.when(i == pl.num_programs(0) - 1)
    def _():
        subq = lax.broadcasted_iota(jnp.int32, (LANES, LANES), 0)
        laneq = lax.broadcasted_iota(jnp.int32, (LANES, LANES), 1)
        cnt = total.astype(jnp.int32)
        tiles = lax.shift_right_logical(cnt + (EXP_TM - 1), EXP_TM.bit_length() - 1).astype(jnp.float32)
        incl = jnp.where(laneq <= subq, 1.0, 0.0).astype(jnp.bfloat16)
        tile_end = _bdot(incl, tiles.astype(jnp.bfloat16))
        tile_start = tile_end - tiles
        seg = (tile_start * EXP_TM).astype(jnp.int32)
        nact = jnp.max(tile_end, axis=0, keepdims=True)

        ids_full = ids_all[...]
        look = jnp.zeros_like(ids_full)
        for e in range(N_SEG):
            look = jnp.where(ids_full == e, seg[N_GROUPS + e, 0], look)
        pos01 = look + pltpu.roll(ids_full, ID_R0 - ID_E0, axis=0)
        assert EXP_TILES * EXP_TM <= 1 << POS_BITS
        pos_ref[...] = pos01 | (pltpu.roll(pos01, ROW_TILE - 1, axis=0) << POS_BITS)

        subp = lax.broadcasted_iota(jnp.int32, (LANES, PLAN_W), 0)
        tile = lax.broadcasted_iota(jnp.int32, (LANES, PLAN_W), 1).astype(jnp.float32)
        is_exp = (subp >= N_GROUPS) & (subp < N_GROUPS + N_SEG)
        end_col = tile_end[:, 0:1]
        nact_s = nact[:, 0:1]
        te = jnp.sum(jnp.where(is_exp & (tile >= end_col), 1.0, 0.0), axis=0, keepdims=True)
        te_last = jnp.sum(jnp.where(is_exp & (nact_s - 1.0 >= end_col), 1.0, 0.0), axis=0, keepdims=True)[:, 0:1]
        tile_row = tile[0:1, :]
        te = jnp.minimum(jnp.where(tile_row < nact_s, te, te_last), N_SEG - 1.0)
        mine = (subp - N_GROUPS).astype(jnp.float32) == te
        cnt_sel = jnp.sum(jnp.where(mine, total[:, 0:1], 0.0), axis=0, keepdims=True)
        start_sel = jnp.sum(jnp.where(mine, tile_start[:, 0:1], 0.0), axis=0, keepdims=True)
        end_sel = jnp.sum(jnp.where(mine, end_col, 0.0), axis=0, keepdims=True)
        valid = jnp.clip(cnt_sel - (tile_row - start_sel) * EXP_TM, 0.0, float(EXP_TM))
        valid = jnp.where(tile_row < nact_s, valid, 0.0)
        subr = lax.broadcasted_iota(jnp.int32, (ROW_TILE, PLAN_W), 0)
        plan = jnp.where(subr == PLAN_EXPERT, te, jnp.where(subr == PLAN_VALID, valid,
                         jnp.where(subr == PLAN_NACT, nact_s, jnp.where(subr == PLAN_NEXT, end_sel, 0.0))))
        plan_ref[...] = plan.astype(jnp.int32)


def _route_call(h, mod, wrt, brt):
    tm = ROUTE_TM
    return pl.pallas_call(
        _route_kernel,
        grid=(SEQ // tm,),
        in_specs=[
            pl.BlockSpec((tm * ROW_TILE, LANES), lambda i: (i, 0)),
            pl.BlockSpec(mod.shape, lambda i: (0, 0)),
            pl.BlockSpec((LANES, D_MODEL), lambda i: (0, 0)),
            pl.BlockSpec((LANES, LANES), lambda i: (0, 0)),
        ],
        out_specs=[
            pl.BlockSpec((ROW_TILE, SEQ), lambda i: (0, 0)),
            pl.BlockSpec((tm, LANES), lambda i: (i, 0)),
            pl.BlockSpec((ROW_TILE, PLAN_W), lambda i: (0, 0)),
        ],
        out_shape=[
            jax.ShapeDtypeStruct((ROW_TILE, SEQ), jnp.int32),
            jax.ShapeDtypeStruct((SEQ, LANES), jnp.float32),
            jax.ShapeDtypeStruct((ROW_TILE, PLAN_W), jnp.int32),
        ],
        scratch_shapes=[pltpu.VMEM((LANES, LANES), jnp.float32),
                        pltpu.VMEM((ROW_TILE, SEQ), jnp.int32)],
        compiler_params=pltpu.CompilerParams(
            dimension_semantics=("arbitrary",), vmem_limit_bytes=VMEM_LIMIT),
        name="route",
    )(h, mod, wrt, brt)


SRC_UNROLL = 8
EXP_BUFS = 3
EXP_CHUNK = 256
TILE_ROWS = EXP_TM * ROW_TILE


def _experts_kernel(te_ref, tv_ref, nact_ref, tnext_ref, pos_ref,
                    h_hbm, mod_ref, wg_hbm, wu_hbm, wd_hbm,
                    y_hbm,
                    src_ref, hres, xbuf, ybuf, wgs, wus, wds, wgb, wub, wdb, rsem, ssem, wsem):
    i = pl.program_id(0)
    last = pl.num_programs(0) - 1
    nact = nact_ref[0]
    cur = lax.rem(i, EXP_BUFS)
    nxt2 = lax.rem(i + 2, EXP_BUFS)
    prv2 = lax.rem(i + 1, EXP_BUFS)
    xcur = lax.rem(i, 2)
    part = lax.shift_right_logical(te_ref[i], N_EXPERTS.bit_length() - 1)

    def scatter_copy(p, k, s):
        dst = src_ref[p]
        return pltpu.make_async_copy(ybuf.at[s, pl.ds(pl.multiple_of(k * ROW_TILE, ROW_TILE), ROW_TILE), :],
                                     y_hbm.at[pl.ds(pl.multiple_of(dst * ROW_TILE, ROW_TILE), ROW_TILE), :],
                                     ssem.at[s])

    def scatter_rolled(tile, s, n):
        def chunk(kk, c):
            for u in range(SRC_UNROLL):
                k = kk * SRC_UNROLL + u
                scatter_copy(tile * EXP_TM + k, k, s).start(priority=u % 2)
            return c
        full = lax.shift_right_logical(n, SRC_UNROLL.bit_length() - 1)
        lax.fori_loop(0, full, chunk, 0)

        def single(k, c):
            scatter_copy(tile * EXP_TM + k, k, s).start()
            return c
        lax.fori_loop(full * SRC_UNROLL, n, single, 0)

    def weight_copies(segment):
        e = segment & (N_EXPERTS - 1)
        return [pltpu.make_async_copy(w_hbm.at[e], stage, wsem.at[n])
                for n, (w_hbm, stage) in enumerate(((wg_hbm, wgs), (wu_hbm, wus), (wd_hbm, wds)))]

    def gather_row(tile, k, s):
        local = (src_ref[tile * EXP_TM + k] - part * PART_TOKENS) & (PART_TOKENS - 1)
        xbuf[s, k * ROW_TILE:(k + 1) * ROW_TILE, :] = hres[pl.ds(pl.multiple_of(local * ROW_TILE, ROW_TILE),
                                                             ROW_TILE), :]

    def gather_items(tile, s):
        return [functools.partial(gather_row, tile, k, s) for k in range(EXP_TM)]

    def wait_scatter(s, n):
        rows = pl.multiple_of(n * ROW_TILE, ROW_TILE)
        pltpu.make_async_copy(ybuf.at[s, pl.ds(0, rows), :], y_hbm.at[pl.ds(0, rows), :], ssem.at[s]).wait()

    def compute_chunks(s, xs):
        state = {}

        def load():
            x = _load_row_tiles(xbuf, EXP_TM, lead=(xs,))
            shift = mod_ref[0:1, 3 * D_MODEL:4 * D_MODEL]
            scale = mod_ref[0:1, 4 * D_MODEL:5 * D_MODEL]
            state["x"] = (x * (1.0 + scale) + shift).astype(jnp.bfloat16)
            state["act"] = []

        def gate(c):
            def run():
                if c == 0:
                    load()
                state["a"] = _bdot(state["x"], wgb[:, c * EXP_CHUNK:(c + 1) * EXP_CHUNK])
            return run

        def up(c):
            def run():
                a = state["a"]
                u = _bdot(state["x"], wub[:, c * EXP_CHUNK:(c + 1) * EXP_CHUNK])
                state["act"].append((a * jax.nn.sigmoid(a) * u).astype(jnp.bfloat16))
            return run

        def down(c):
            def run():
                if c == 0:
                    state["actf"] = jnp.concatenate(state["act"], axis=-1)
                yc = _bdot(state["actf"], wdb[:, c * EXP_CHUNK:(c + 1) * EXP_CHUNK])
                for jj in range(EXP_CHUNK // LANES):
                    j = c * (EXP_CHUNK // LANES) + jj
                    ybuf[s, pl.ds(j, EXP_TM, stride=ROW_TILE), :] = yc[:, jj * LANES:(jj + 1) * LANES]
            return run

        steps = []
        for c in range(D_EXPERT // EXP_CHUNK):
            steps += [(gate(c), 2), (up(c), 2)]
        return steps + [(down(c), 1) for c in range(D_MODEL // EXP_CHUNK)]

    def run_interleaved(chunks, dmas):
        total_cost = sum(cost for _, cost in chunks)
        done = 0
        for chunk, cost in chunks:
            upto = -(-len(dmas) * (done + cost) // total_cost)
            for d in dmas[-(-len(dmas) * done // total_cost):upto]:
                d()
            done += cost
            chunk()

    def scatter_dmas(tile, s):
        return [functools.partial(lambda k: scatter_copy(tile * EXP_TM + k, k, s).start(priority=k % 2), k)
                for k in range(EXP_TM)]

    @pl.when(i == 0)
    def _():
        for cp in weight_copies(te_ref[0]):
            cp.start()

        def fill_body(tt, c):
            ts = [tt * SRC_UNROLL + u for u in range(SRC_UNROLL)]
            words = [pos_ref[t] for t in ts]
            for t, w in zip(ts, words):
                src_ref[w & ((1 << POS_BITS) - 1)] = t
                src_ref[lax.shift_right_logical(w, POS_BITS)] = t + SEQ
            return c
        lax.fori_loop(0, SEQ // SRC_UNROLL, fill_body, 0)

        def pad_tile(t, c):
            pad_tok = lax.shift_right_logical(te_ref[t], N_EXPERTS.bit_length() - 1) * PART_TOKENS

            def pad_row(k, c2):
                src_ref[t * EXP_TM + k] = pad_tok
                return c2
            return lax.fori_loop(tv_ref[t], EXP_TM, pad_row, c)
        lax.fori_loop(0, nact, pad_tile, 0)

    active = i < nact
    prev = jnp.maximum(i - 1, 0)
    prev_ok = (i >= 1) & (i - 1 < nact)
    prev2 = jnp.maximum(i - 2, 0)
    prev2_ok = (i >= 2) & (i - 2 < nact)
    nprev = tv_ref[prev]
    full_prev = prev_ok & (nprev == EXP_TM)
    new_segment = (i == 0) | (te_ref[i] != te_ref[prev])

    @pl.when(prev2_ok)
    def _():
        wait_scatter(prv2, tv_ref[prev2])

    @pl.when(active & ((i == 0) | (part != lax.shift_right_logical(te_ref[prev], N_EXPERTS.bit_length() - 1))))
    def _():
        rows = PART_TOKENS * ROW_TILE
        cp = pltpu.make_async_copy(h_hbm.at[pl.ds(pl.multiple_of(part * rows, rows), rows), :], hres, rsem.at[0])
        cp.start()
        cp.wait()
        for item in gather_items(i, xcur):
            item()

    @pl.when(active & new_segment)
    def _():
        for cp in weight_copies(te_ref[i]):
            cp.wait()
        wgb[...] = wgs[...].astype(jnp.bfloat16)
        wub[...] = wus[...].astype(jnp.bfloat16)
        wdb[...] = wds[...].astype(jnp.bfloat16)
        nxt = tnext_ref[i]

        @pl.when(nxt < nact)
        def _():
            for cp in weight_copies(te_ref[jnp.minimum(nxt, last)]):
                cp.start()

    @pl.when(prev_ok & jnp.logical_not(active & full_prev))
    def _():
        scatter_rolled(prev, nxt2, nprev)

    tile_next = jnp.minimum(i + 1, nact - 1)

    @pl.when(active & full_prev)
    def _():
        sc, g = scatter_dmas(prev, nxt2), gather_items(tile_next, 1 - xcur)
        run_interleaved(compute_chunks(cur, xcur), [item for pair in zip(sc, g) for item in pair])

    @pl.when(active & jnp.logical_not(full_prev))
    def _():
        run_interleaved(compute_chunks(cur, xcur), gather_items(tile_next, 1 - xcur))

    @pl.when(i == last)
    def _():
        @pl.when(prev_ok)
        def _():
            wait_scatter(nxt2, nprev)

        @pl.when(active)
        def _():
            scatter_rolled(i, cur, tv_ref[i])
            wait_scatter(cur, tv_ref[i])


def _experts_call(te, tv, nact, tnext, pos, h, mod, wg, wu, wd):
    grid_spec = pltpu.PrefetchScalarGridSpec(
        num_scalar_prefetch=5,
        grid=(EXP_TILES,),
        in_specs=[
            pl.BlockSpec(memory_space=pl.ANY),
            pl.BlockSpec(mod.shape, lambda i, *_: (0, 0)),
            pl.BlockSpec(memory_space=pl.ANY),
            pl.BlockSpec(memory_space=pl.ANY),
            pl.BlockSpec(memory_space=pl.ANY),
        ],
        out_specs=pl.BlockSpec(memory_space=pl.ANY),
        scratch_shapes=[
            pltpu.SMEM((EXP_TILES * EXP_TM,), jnp.int32),
            pltpu.VMEM((PART_TOKENS * ROW_TILE, LANES), jnp.float32),
            pltpu.VMEM((2, TILE_ROWS, LANES), jnp.float32),
            pltpu.VMEM((EXP_BUFS, TILE_ROWS, LANES), jnp.float32),
            pltpu.VMEM((D_MODEL, D_EXPERT), jnp.float32),
            pltpu.VMEM((D_MODEL, D_EXPERT), jnp.float32),
            pltpu.VMEM((D_EXPERT, D_MODEL), jnp.float32),
            pltpu.VMEM((D_MODEL, D_EXPERT), jnp.bfloat16),
            pltpu.VMEM((D_MODEL, D_EXPERT), jnp.bfloat16),
            pltpu.VMEM((D_EXPERT, D_MODEL), jnp.bfloat16),
            pltpu.SemaphoreType.DMA((1,)),
            pltpu.SemaphoreType.DMA((EXP_BUFS,)),
            pltpu.SemaphoreType.DMA((3,)),
        ],
    )
    return pl.pallas_call(
        _experts_kernel,
        grid_spec=grid_spec,
        out_shape=jax.ShapeDtypeStruct((2 * SEQ * ROW_TILE, LANES), jnp.float32),
        compiler_params=pltpu.CompilerParams(
            dimension_semantics=("arbitrary",), vmem_limit_bytes=EXPERTS_VMEM_LIMIT),
        name="experts",
    )(te, tv, nact, tnext, pos, h, mod, wg, wu, wd)


def _combine_kernel(h_ref, y0_ref, y1_ref, cw_ref, mod_ref, g_ref, b_ref, o_ref):
    tm = o_ref.shape[0]
    c0 = cw_ref[:, CW_C0:CW_C0 + 1]
    c1 = cw_ref[:, CW_C1:CW_C1 + 1]
    ffn = c0 * _load_row_tiles(y0_ref, tm) + c1 * _load_row_tiles(y1_ref, tm)
    g2 = mod_ref[0:1, 5 * D_MODEL:6 * D_MODEL]
    o_ref[...] = _layer_norm(DEEPNORM_ALPHA * _load_row_tiles(h_ref, tm) + g2 * ffn, g_ref[...], b_ref[...])


def _combine_call(h, y, cw, mod, g, b):
    tm = CMB_TM
    nblk = SEQ // tm
    tiles = lambda off: pl.BlockSpec((tm * ROW_TILE, LANES), lambda i: (i + off, 0))
    return pl.pallas_call(
        _combine_kernel,
        grid=(nblk,),
        in_specs=[
            tiles(0), tiles(0), tiles(nblk),
            pl.BlockSpec((tm, LANES), lambda i: (i, 0)),
            pl.BlockSpec(mod.shape, lambda i: (0, 0)),
            pl.BlockSpec((1, D_MODEL), lambda i: (0, 0)),
            pl.BlockSpec((1, D_MODEL), lambda i: (0, 0)),
        ],
        out_specs=pl.BlockSpec((tm, D_MODEL), lambda i: (i, 0)),
        out_shape=jax.ShapeDtypeStruct((SEQ, D_MODEL), jnp.float32),
        compiler_params=pltpu.CompilerParams(
            dimension_semantics=("arbitrary",), vmem_limit_bytes=VMEM_LIMIT),
        name="combine",
    )(h, y, y, cw, mod, g, b)


def kernel(x, c, ctx, c_ctx, ln_in_g, ln_in_b, w_mod, b_mod, w_in, rpb, w_pool_grp, pool_scale,
           w_attn_proj, w_pool_proj, w_out, ln1_g, ln1_b, w_router_group, b_router_group,
           w_router_expert, b_router_expert, w_expert_gate, w_expert_up, w_expert_down, ln2_g, ln2_b):
    assert x.shape == (1, SEQ, D_MODEL) and ctx.shape == (1, CTX_LEN, D_MODEL)
    assert w_mod.shape[0] == 1, "single-layer trunk"
    f32, bf16 = jnp.float32, jnp.bfloat16
    row = lambda v: v.reshape(1, -1).astype(f32)

    cond = jnp.concatenate([c, c_ctx[None], jnp.zeros((MOD_ROWS - 2, D_MODEL), f32)], axis=0)
    mod = _mod_call(cond, w_mod[0], row(b_mod[0]))

    lng, lnb = row(ln_in_g), row(ln_in_b)
    w_in_b = w_in[0].astype(bf16)
    u, h0 = _proj_call(x[0], mod, lng, lnb, w_in_b, mod_row=0, latent=True, tm=PROJ_TM)
    kvc, = _proj_call(ctx[0], mod, lng, lnb, w_in_b[:, ATT_W:3 * ATT_W], mod_row=1, latent=False, tm=CTX_LEN)

    h1 = _mix_call(h0, mod, u, kvc, _attn_bias_table(rpb[0]),
                   w_pool_grp[0].astype(bf16), row(pool_scale[0]),
                   w_attn_proj[0].astype(bf16), w_pool_proj[0].astype(bf16), w_out[0].astype(bf16),
                   row(ln1_g[0]), row(ln1_b[0]))

    n_logit = N_GROUPS + N_EXPERTS
    wrt = jnp.concatenate([w_router_group[0].T, w_router_expert[0].T,
                           jnp.zeros((LANES - n_logit, D_MODEL), f32)], axis=0)
    brt = jnp.concatenate([b_router_group[0], b_router_expert[0], jnp.zeros((LANES - n_logit,), f32)])
    brt = jnp.broadcast_to(brt[:, None], (LANES, LANES))
    pos, cw, plan = _route_call(h1, mod, wrt, brt)

    y = _experts_call(plan[PLAN_EXPERT, :EXP_TILES], plan[PLAN_VALID, :EXP_TILES], plan[PLAN_NACT, :1],
                      plan[PLAN_NEXT, :EXP_TILES], pos[POS_PACKED], h1, mod,
                      w_expert_gate[0], w_expert_up[0], w_expert_down[0])
    out = _combine_call(h1, y, cw, mod, row(ln2_g[0]), row(ln2_b[0]))
    return out[None]
```

```python
import functools

import jax
import jax.numpy as jnp
from jax import lax
from jax.experimental import pallas as pl
from jax.experimental.pallas import tpu as pltpu

D_MODEL = 1024
SEQ = 16384
GRID_W = 64
ROWS = SEQ // GRID_W
CTX_LEN = 256
N_HEADS = 8
HEAD_DIM = 64
ATT_W = N_HEADS * HEAD_DIM
WIN_H = 8
WIN_W = 16
POOL_WINDOWS = (2, 4, 8, 16)
POOL_GROUPS = 4
POOL_DIM = 128
POOL_W = POOL_GROUPS * POOL_DIM
PROJ_W = 3 * ATT_W + POOL_W + 2 * D_MODEL
GATE_COL = 3 * ATT_W + POOL_W
N_GROUPS = 4
EXPERTS_PER_GROUP = 8
N_EXPERTS = N_GROUPS * EXPERTS_PER_GROUP
D_EXPERT = 512
N_MOD = 6
DEEPNORM_ALPHA = 2.0 ** 0.25
LN_EPS = 1e-5
NEG_INF = -1e30

LANES = 128
ROW_TILE = 8
MOD_ROWS = 8
PROJ_TM = 512
PROJ_SUB = 256
MIX_ROWS = 8
MIX_TQ = MIX_ROWS * GRID_W
KV_HALO = 4 * GRID_W
POOL_HALO = 16
ROUTE_TM = 512
EXP_TM = 256
SEQ_PARTS = 2
PART_TOKENS = SEQ // SEQ_PARTS
N_SEG = SEQ_PARTS * N_EXPERTS
EXP_TILES = 2 * SEQ // EXP_TM + N_SEG
CMB_TM = 512
CMB_CHUNK = 128
MERGE_ROWS = 128
ATTN_AHEAD = 2
HALF = D_MODEL // 2
VMEM_LIMIT = 56 * 1024 * 1024
EXPERTS_VMEM_LIMIT = 60 * 1024 * 1024


def _layer_norm(x, g, b):
    mu = jnp.mean(x, axis=-1, keepdims=True)
    xc = x - mu
    var = jnp.mean(xc * xc, axis=-1, keepdims=True)
    return xc * lax.rsqrt(var + LN_EPS) * g + b


def _bdot(a, b):
    return jnp.dot(a, b, preferred_element_type=jnp.float32)


def _split_bf16(a):
    hi = a.astype(jnp.bfloat16)
    lo = (a - hi.astype(jnp.float32)).astype(jnp.bfloat16)
    return hi, lo


def _dot3(a, b):
    a_hi, a_lo = _split_bf16(a)
    b_hi, b_lo = _split_bf16(b)
    return _bdot(a_hi, b_hi) + (_bdot(a_hi, b_lo) + _bdot(a_lo, b_hi))


def _load_row_tiles(ref, tokens, lead=(), first=0):
    parts = [ref[(*lead, pl.ds(first * ROW_TILE + j, tokens, stride=ROW_TILE), slice(None))]
             for j in range(ROW_TILE)]
    return jnp.concatenate(parts, axis=-1)


def _store_row_tiles(ref, value, lead=()):
    tokens = value.shape[0]
    for j in range(ROW_TILE):
        ref[(*lead, pl.ds(j, tokens, stride=ROW_TILE), slice(None))] = value[:, j * LANES:(j + 1) * LANES]


def _mod_kernel(cond_ref, w_ref, b_ref, o_ref):
    cond = cond_ref[...]
    act = cond * jax.nn.sigmoid(cond)
    o_ref[...] = _dot3(act, w_ref[...]) + b_ref[...]


def _mod_call(cond, w_mod, b_mod):
    tn = 1536
    n = N_MOD * D_MODEL
    return pl.pallas_call(
        _mod_kernel,
        grid=(n // tn,),
        in_specs=[
            pl.BlockSpec((MOD_ROWS, D_MODEL), lambda i: (0, 0)),
            pl.BlockSpec((D_MODEL, tn), lambda i: (0, i)),
            pl.BlockSpec((1, tn), lambda i: (0, i)),
        ],
        out_specs=pl.BlockSpec((MOD_ROWS, tn), lambda i: (0, i)),
        out_shape=jax.ShapeDtypeStruct((MOD_ROWS, n), jnp.float32),
        compiler_params=pltpu.CompilerParams(
            dimension_semantics=("arbitrary",), vmem_limit_bytes=VMEM_LIMIT),
        name="mod",
    )(cond, w_mod, b_mod)


def _proj_kernel(x_ref, mod_ref, g_ref, b_ref, w_ref, o_ref, *h_out, mod_row, latent):
    shift = mod_ref[mod_row:mod_row + 1, 0:D_MODEL]
    scale = mod_ref[mod_row:mod_row + 1, D_MODEL:2 * D_MODEL]
    tm, n = o_ref.shape
    sub = min(tm, PROJ_SUB)

    def prep(r):
        rows = slice(r * sub, (r + 1) * sub)
        h = _layer_norm(x_ref[rows, :], g_ref[...], b_ref[...])
        if latent:
            h_out[0][rows, :] = h
        return (h * (1.0 + scale) + shift).astype(jnp.bfloat16)

    def finish(r, c, res):
        if latent and c == 0:
            lane = lax.broadcasted_iota(jnp.int32, (1, D_MODEL), 1)
            res = res * jnp.where(lane < ATT_W, HEAD_DIM ** -0.5, 1.0)
        if latent and c * D_MODEL >= GATE_COL:
            res = jax.nn.sigmoid(res)
        o_ref[r * sub:(r + 1) * sub, c * D_MODEL:(c + 1) * D_MODEL] = res.astype(jnp.bfloat16)

    hm = {0: prep(0)}
    waiting = None
    for r in range(tm // sub):
        for c in range(n // D_MODEL):
            res = _bdot(hm[r], w_ref[:, c * D_MODEL:(c + 1) * D_MODEL])
            if c == 0 and (r + 1) * sub < tm:
                hm[r + 1] = prep(r + 1)
            if waiting is not None:
                finish(*waiting)
            waiting = (r, c, res)
    finish(*waiting)


def _proj_call(x, mod, g, b, w, *, mod_row, latent, tm):
    rows, n = x.shape[0], w.shape[1]
    out_specs = [pl.BlockSpec((tm, n), lambda i: (i, 0))]
    out_shape = [jax.ShapeDtypeStruct((rows, n), jnp.bfloat16)]
    if latent:
        out_specs.append(pl.BlockSpec((tm, D_MODEL), lambda i: (i, 0)))
        out_shape.append(jax.ShapeDtypeStruct((rows, D_MODEL), jnp.float32))
    return pl.pallas_call(
        functools.partial(_proj_kernel, mod_row=mod_row, latent=latent),
        grid=(rows // tm,),
        in_specs=[
            pl.BlockSpec((tm, D_MODEL), lambda i: (i, 0)),
            pl.BlockSpec(mod.shape, lambda i: (0, 0)),
            pl.BlockSpec((1, D_MODEL), lambda i: (0, 0)),
            pl.BlockSpec((1, D_MODEL), lambda i: (0, 0)),
            pl.BlockSpec((D_MODEL, n), lambda i: (0, 0), pipeline_mode=pl.Buffered(1)),
        ],
        out_specs=out_specs,
        out_shape=out_shape,
        compiler_params=pltpu.CompilerParams(
            dimension_semantics=("arbitrary",), vmem_limit_bytes=VMEM_LIMIT),
        name="proj",
    )(x, mod, g, b, w)


def _attn_bias_table(rpb):
    col = jnp.arange(GRID_W, dtype=jnp.int32)
    col_start = jnp.clip(col - WIN_W // 2, 0, GRID_W - WIN_W)
    col_mask = (col[None, :] >= col_start[:, None]) & (col[None, :] < col_start[:, None] + WIN_W)
    col_off = jnp.clip(col[None, :] - col[:, None], 1 - WIN_W, WIN_W - 1) + (WIN_W - 1)
    onehot = (col_off[None] == jnp.arange(2 * WIN_W - 1, dtype=jnp.int32)[:, None, None]).astype(jnp.float32)
    tab = jnp.einsum("hrc,cqk->hqrk", rpb.astype(jnp.float32), onehot, precision=lax.Precision.HIGHEST)
    tab = jnp.where(col_mask[None, :, None, :], tab, NEG_INF)
    tab = jnp.stack([tab[:, :, WIN_H - 1 - v:2 * WIN_H - 1 - v] for v in range(WIN_H)], axis=0)
    return tab.reshape(WIN_H, N_HEADS // 2, 2 * GRID_W, WIN_H * GRID_W)


def _mix_kernel(h_ref, mod_ref,
                q_ref, kp_ref, kc_ref, kn_ref, vp_ref, vc_ref, vn_ref,
                pp_ref, pc_ref, pn_ref, ga_ref, gb_ref,
                kvc_ref, bias_ref, wgrp_ref, pscale_ref, wap_ref, wpp_ref, wout_ref,
                ln1g_ref, ln1b_ref,
                o_ref,
                kbuf, vbuf, yabuf, pbuf, ypbuf, zbuf):
    b = pl.program_id(0)
    nb = pl.num_programs(0)

    kbuf[0:KV_HALO, :] = kp_ref[...]
    kbuf[KV_HALO:KV_HALO + MIX_TQ, :] = kc_ref[...]
    kbuf[KV_HALO + MIX_TQ:, :] = kn_ref[...]
    vbuf[0:KV_HALO, :] = vp_ref[...]
    vbuf[KV_HALO:KV_HALO + MIX_TQ, :] = vc_ref[...]
    vbuf[KV_HALO + MIX_TQ:, :] = vn_ref[...]

    lane = lax.broadcasted_iota(jnp.int32, (GRID_W, LANES), 1)
    first_head = lane < HEAD_DIM

    units = [(j, pair) for j in range(MIX_ROWS) for pair in range(N_HEADS // 2)]
    nt = (((1,), (1,)), ((), ()))

    def window(j):
        r = b * MIX_ROWS + j
        rs = jnp.clip(r - WIN_H // 2, 0, ROWS - WIN_H)
        return pl.multiple_of((rs - b * MIX_ROWS + WIN_H // 2) * GRID_W, GRID_W), r - rs

    def scores(j, pair):
        off, var = window(j)
        cols = slice(pair * LANES, (pair + 1) * LANES)
        q = q_ref[j * GRID_W:(j + 1) * GRID_W, cols]
        zero = jnp.zeros_like(q)
        q2 = jnp.concatenate([jnp.where(first_head, q, zero), jnp.where(first_head, zero, q)], axis=0)
        kw = kbuf[pl.ds(off, WIN_H * GRID_W), cols]
        s_loc = lax.dot_general(q2, kw, nt, preferred_element_type=jnp.float32) + bias_ref[var, pair]
        s_ctx = lax.dot_general(q2, kvc_ref[:, cols], nt, preferred_element_type=jnp.float32)
        return s_loc, s_ctx

    def values(j, pair, s_loc, s_ctx):
        off, _ = window(j)
        cols = slice(pair * LANES, (pair + 1) * LANES)
        vw = vbuf[pl.ds(off, WIN_H * GRID_W), cols]
        vctx = kvc_ref[:, ATT_W + pair * LANES:ATT_W + (pair + 1) * LANES]
        m = jnp.maximum(jnp.max(s_loc, axis=-1, keepdims=True), jnp.max(s_ctx, axis=-1, keepdims=True))
        p_loc = jnp.exp(s_loc - m)
        p_ctx = jnp.exp(s_ctx - m)
        denom = jnp.sum(p_loc, axis=-1, keepdims=True) + jnp.sum(p_ctx, axis=-1, keepdims=True)
        o2 = _bdot(p_loc.astype(jnp.bfloat16), vw) + _bdot(p_ctx.astype(jnp.bfloat16), vctx)
        o2 = o2 * (1.0 / denom)
        o_pair = jnp.where(first_head, o2[:GRID_W], o2[GRID_W:])
        yabuf[j * GRID_W:(j + 1) * GRID_W, cols] = o_pair.astype(jnp.bfloat16)

    pbuf[0:POOL_HALO, :] = jnp.where(b > 0, pp_ref[...].astype(jnp.float32), 0.0)
    pbuf[POOL_HALO:POOL_HALO + MIX_TQ, :] = pc_ref[...].astype(jnp.float32)
    pbuf[POOL_HALO + MIX_TQ:, :] = jnp.where(b < nb - 1, pn_ref[...].astype(jnp.float32), 0.0)
    edge = lax.broadcasted_iota(jnp.int32, (ROW_TILE, 1), 0)

    def pool_group(g):
        win = POOL_WINDOWS[g]
        lo, hi = win // 2, win - win // 2
        cols = slice(g * POOL_DIM, (g + 1) * POOL_DIM)
        acc = None
        for d in range(-lo, hi):
            term = pbuf[POOL_HALO + d:POOL_HALO + d + MIX_TQ, cols]
            acc = term if acc is None else acc + term
        assert max(lo, hi) <= ROW_TILE
        top = jnp.where(b == 0, 1.0 / (win - jnp.maximum(lo - edge, 0)).astype(jnp.float32), 1.0 / win)
        bot = jnp.where(b == nb - 1,
                        1.0 / (win - jnp.maximum(edge + hi - ROW_TILE, 0)).astype(jnp.float32), 1.0 / win)
        inv = jnp.concatenate([top, jnp.full((MIX_TQ - 2 * ROW_TILE, 1), 1.0 / win, jnp.float32), bot], axis=0)
        pooled = acc * inv - pbuf[POOL_HALO:POOL_HALO + MIX_TQ, cols]
        yp = _bdot(pooled.astype(jnp.bfloat16), wgrp_ref[g]) * pscale_ref[:, cols]
        ypbuf[:, cols] = yp.astype(jnp.bfloat16)

    def pooled_branch():
        zbuf[...] = gb_ref[...].astype(jnp.float32) * _bdot(ypbuf[...], wpp_ref[...])

    extra = {}
    for g in range(POOL_GROUPS):
        extra[(g + 1) * len(units) // (POOL_GROUPS + 2)] = functools.partial(pool_group, g)
    extra[(POOL_GROUPS + 1) * len(units) // (POOL_GROUPS + 2)] = pooled_branch
    pending = [scores(*u) for u in units[:ATTN_AHEAD]]
    for n, u in enumerate(units):
        if n + ATTN_AHEAD < len(units):
            pending.append(scores(*units[n + ATTN_AHEAD]))
        values(*u, *pending.pop(0))
        if n in extra:
            extra[n]()

    g1 = mod_ref[0:1, 2 * D_MODEL:3 * D_MODEL]
    n_chunks = MIX_TQ // MERGE_ROWS
    rows = lambda c: slice(c * MERGE_ROWS, (c + 1) * MERGE_ROWS)
    z, y = {}, {}

    def stage_a(c):
        ya = _bdot(yabuf[rows(c), :], wap_ref[...])
        z[c] = (ga_ref[rows(c), :].astype(jnp.float32) * ya + zbuf[rows(c), :]).astype(jnp.bfloat16)

    def stage_b(c):
        y[c] = _bdot(z.pop(c), wout_ref[...])

    def stage_c(c):
        hn = _layer_norm(DEEPNORM_ALPHA * h_ref[rows(c), :] + g1 * y.pop(c), ln1g_ref[...], ln1b_ref[...])
        for j in range(ROW_TILE):
            o_ref[pl.ds(c * MERGE_ROWS * ROW_TILE + j, MERGE_ROWS, stride=ROW_TILE), :] = hn[:, j * LANES:(j + 1) * LANES]

    for t in range(n_chunks + 2):
        if t < n_chunks:
            stage_a(t)
        if 0 <= t - 1 < n_chunks:
            stage_b(t - 1)
        if 0 <= t - 2 < n_chunks:
            stage_c(t - 2)


def _mix_call(h, mod, u, kvc, bias, wgrp, pscale, wap, wpp, wout, ln1g, ln1b):
    nb = SEQ // MIX_TQ
    halo_per_blk = MIX_TQ // KV_HALO
    n_halo = SEQ // KV_HALO
    ph_per_blk = MIX_TQ // POOL_HALO
    n_ph = SEQ // POOL_HALO

    def const(shape):
        return pl.BlockSpec(shape, lambda i: (0,) * len(shape), pipeline_mode=pl.Buffered(1))

    def prev_halo(c):
        return pl.BlockSpec((KV_HALO, ATT_W), lambda i: (jnp.maximum(i * halo_per_blk - 1, 0), c))

    def next_halo(c):
        return pl.BlockSpec((KV_HALO, ATT_W), lambda i: (jnp.minimum((i + 1) * halo_per_blk, n_halo - 1), c))

    def cur(c):
        return pl.BlockSpec((MIX_TQ, ATT_W), lambda i: (i, c))

    in_specs = [
        pl.BlockSpec((MIX_TQ, D_MODEL), lambda i: (i, 0)),
        const(mod.shape),
        cur(0),
        prev_halo(1), cur(1), next_halo(1),
        prev_halo(2), cur(2), next_halo(2),
        pl.BlockSpec((POOL_HALO, POOL_W), lambda i: (jnp.maximum(i * ph_per_blk - 1, 0), 3)),
        cur(3),
        pl.BlockSpec((POOL_HALO, POOL_W), lambda i: (jnp.minimum((i + 1) * ph_per_blk, n_ph - 1), 3)),
        pl.BlockSpec((MIX_TQ, D_MODEL), lambda i: (i, 2)),
        pl.BlockSpec((MIX_TQ, D_MODEL), lambda i: (i, 3)),
        const(kvc.shape), const(bias.shape), const(wgrp.shape), const(pscale.shape),
        const(wap.shape), const(wpp.shape), const(wout.shape),
        const((1, D_MODEL)), const((1, D_MODEL)),
    ]
    return pl.pallas_call(
        _mix_kernel,
        grid=(nb,),
        in_specs=in_specs,
        out_specs=pl.BlockSpec((MIX_TQ * ROW_TILE, LANES), lambda i: (i, 0)),
        out_shape=jax.ShapeDtypeStruct((SEQ * ROW_TILE, LANES), jnp.float32),
        scratch_shapes=[
            pltpu.VMEM((MIX_TQ + 2 * KV_HALO, ATT_W), jnp.bfloat16),
            pltpu.VMEM((MIX_TQ + 2 * KV_HALO, ATT_W), jnp.bfloat16),
            pltpu.VMEM((MIX_TQ, ATT_W), jnp.bfloat16),
            pltpu.VMEM((MIX_TQ + 2 * POOL_HALO, POOL_W), jnp.float32),
            pltpu.VMEM((MIX_TQ, POOL_W), jnp.bfloat16),
            pltpu.VMEM((MIX_TQ, D_MODEL), jnp.float32),
        ],
        compiler_params=pltpu.CompilerParams(
            dimension_semantics=("arbitrary",), vmem_limit_bytes=VMEM_LIMIT),
        name="mix",
    )(h, mod, u, u, u, u, u, u, u, u, u, u, u, u,
      kvc, bias, wgrp, pscale, wap, wpp, wout, ln1g, ln1b)


ID_E0, ID_E1, ID_R0, ID_R1 = 0, 1, 4, 5
POS_PACKED = 0
POS_BITS = 16
PLAN_EXPERT, PLAN_VALID, PLAN_NACT, PLAN_NEXT = 0, 1, 2, 3
PLAN_W = 2 * LANES
CW_C0, CW_C1 = 0, 1


def _route_kernel(h_ref, mod_ref, wrt_ref, brt_ref, pos_ref, cw_ref, plan_ref, carry_ref, ids_all):
    i = pl.program_id(0)
    tm = ROUTE_TM

    @pl.when(i == 0)
    def _():
        carry_ref[...] = jnp.zeros_like(carry_ref)

    shift = mod_ref[0:1, 3 * D_MODEL:4 * D_MODEL]
    scale = mod_ref[0:1, 4 * D_MODEL:5 * D_MODEL]
    hm = _load_row_tiles(h_ref, tm) * (1.0 + scale) + shift

    hm_hi, hm_lo = _split_bf16(hm)
    w_hi, w_lo = _split_bf16(wrt_ref[...])
    nt = (((1,), (1,)), ((), ()))
    dg = functools.partial(lax.dot_general, dimension_numbers=nt, preferred_element_type=jnp.float32)
    logits = dg(w_hi, hm_hi) + (dg(w_hi, hm_lo) + dg(w_lo, hm_hi)) + brt_ref[:, 0:1]

    sub = lax.broadcasted_iota(jnp.int32, (LANES, tm), 0)
    big = jnp.int32(1 << 20)
    is_grp = sub < N_GROUPS
    gl = jnp.where(is_grp, logits, -jnp.inf)
    gmax = jnp.max(gl, axis=0, keepdims=True)
    gidx = jnp.min(jnp.where(gl == gmax, sub, big), axis=0, keepdims=True)
    gsum = jnp.sum(jnp.where(is_grp, jnp.exp(logits - gmax), 0.0), axis=0, keepdims=True)
    p_group = 1.0 / gsum

    eid = sub - N_GROUPS
    sel = (eid >= 0) & (eid < N_EXPERTS) & (lax.shift_right_arithmetic(eid, 3) == gidx)
    el = jnp.where(sel, logits, -jnp.inf)
    l0 = jnp.max(el, axis=0, keepdims=True)
    i0 = jnp.min(jnp.where(el == l0, sub, big), axis=0, keepdims=True)
    el2 = jnp.where(sub == i0, -jnp.inf, el)
    l1 = jnp.max(el2, axis=0, keepdims=True)
    i1 = jnp.min(jnp.where(el2 == l1, sub, big), axis=0, keepdims=True)
    t = jnp.exp(l1 - l0)
    w0 = 1.0 / (1.0 + t)
    w1 = t / (1.0 + t)

    half_rows = jnp.where(i >= pl.num_programs(0) // SEQ_PARTS, N_EXPERTS, 0)
    i0 = i0 + half_rows
    i1 = i1 + half_rows
    onehot = jnp.where((sub == i0) | (sub == i1), 1.0, 0.0)
    rr = lax.broadcasted_iota(jnp.int32, (tm, tm), 0)
    cc = lax.broadcasted_iota(jnp.int32, (tm, tm), 1)
    earlier = jnp.where(rr < cc, 1.0, 0.0).astype(jnp.bfloat16)
    carry = carry_ref[:, 0:1]
    prefix = _bdot(onehot.astype(jnp.bfloat16), earlier) + carry
    r0 = jnp.sum(jnp.where(sub == i0, prefix, 0.0), axis=0, keepdims=True)
    r1 = jnp.sum(jnp.where(sub == i1, prefix, 0.0), axis=0, keepdims=True)
    total = jnp.broadcast_to(carry + jnp.sum(onehot, axis=1, keepdims=True), carry_ref.shape)
    carry_ref[...] = total

    sub8 = lax.broadcasted_iota(jnp.int32, (ROW_TILE, tm), 0)
    ids = jnp.zeros((ROW_TILE, tm), jnp.int32)
    for idx, val in ((ID_E0, i0 - N_GROUPS), (ID_E1, i1 - N_GROUPS),
                     (ID_R0, r0.astype(jnp.int32)), (ID_R1, r1.astype(jnp.int32))):
        ids = jnp.where(sub8 == idx, val, ids)
    ids_all[:, pl.ds(pl.multiple_of(i * tm, tm), tm)] = ids

    cwt = jnp.where(sub == CW_C0, p_group * w0, jnp.where(sub == CW_C1, p_group * w1, 0.0))
    cw_ref[...] = cwt.T

    @pl.when(i == pl.num_programs(0) - 1)
    def _():
        subq = lax.broadcasted_iota(jnp.int32, (LANES, LANES), 0)
        laneq = lax.broadcasted_iota(jnp.int32, (LANES, LANES), 1)
        cnt = total.astype(jnp.int32)
        tiles = lax.shift_right_logical(cnt + (EXP_TM - 1), EXP_TM.bit_length() - 1).astype(jnp.float32)
        incl = jnp.where(laneq <= subq, 1.0, 0.0).astype(jnp.bfloat16)
        tile_end = _bdot(incl, tiles.astype(jnp.bfloat16))
        tile_start = tile_end - tiles
        seg = (tile_start * EXP_TM).astype(jnp.int32)
        nact = jnp.max(tile_end, axis=0, keepdims=True)

        ids_full = ids_all[...]
        look = jnp.zeros_like(ids_full)
        for e in range(N_SEG):
            look = jnp.where(ids_full == e, seg[N_GROUPS + e, 0], look)
        pos01 = look + pltpu.roll(ids_full, ID_R0 - ID_E0, axis=0)
        assert EXP_TILES * EXP_TM <= 1 << POS_BITS
        pos_ref[...] = pos01 | (pltpu.roll(pos01, ROW_TILE - 1, axis=0) << POS_BITS)

        subp = lax.broadcasted_iota(jnp.int32, (LANES, PLAN_W), 0)
        tile = lax.broadcasted_iota(jnp.int32, (LANES, PLAN_W), 1).astype(jnp.float32)
        is_exp = (subp >= N_GROUPS) & (subp < N_GROUPS + N_SEG)
        end_col = tile_end[:, 0:1]
        nact_s = nact[:, 0:1]
        te = jnp.sum(jnp.where(is_exp & (tile >= end_col), 1.0, 0.0), axis=0, keepdims=True)
        te_last = jnp.sum(jnp.where(is_exp & (nact_s - 1.0 >= end_col), 1.0, 0.0), axis=0, keepdims=True)[:, 0:1]
        tile_row = tile[0:1, :]
        te = jnp.minimum(jnp.where(tile_row < nact_s, te, te_last), N_SEG - 1.0)
        mine = (subp - N_GROUPS).astype(jnp.float32) == te
        cnt_sel = jnp.sum(jnp.where(mine, total[:, 0:1], 0.0), axis=0, keepdims=True)
        start_sel = jnp.sum(jnp.where(mine, tile_start[:, 0:1], 0.0), axis=0, keepdims=True)
        end_sel = jnp.sum(jnp.where(mine, end_col, 0.0), axis=0, keepdims=True)
        valid = jnp.clip(cnt_sel - (tile_row - start_sel) * EXP_TM, 0.0, float(EXP_TM))
        valid = jnp.where(tile_row < nact_s, valid, 0.0)
        subr = lax.broadcasted_iota(jnp.int32, (ROW_TILE, PLAN_W), 0)
        plan = jnp.where(subr == PLAN_EXPERT, te, jnp.where(subr == PLAN_VALID, valid,
                         jnp.where(subr == PLAN_NACT, nact_s, jnp.where(subr == PLAN_NEXT, end_sel, 0.0))))
        plan_ref[...] = plan.astype(jnp.int32)


def _route_call(h, mod, wrt, brt):
    tm = ROUTE_TM
    return pl.pallas_call(
        _route_kernel,
        grid=(SEQ // tm,),
        in_specs=[
            pl.BlockSpec((tm * ROW_TILE, LANES), lambda i: (i, 0)),
            pl.BlockSpec(mod.shape, lambda i: (0, 0)),
            pl.BlockSpec((LANES, D_MODEL), lambda i: (0, 0)),
            pl.BlockSpec((LANES, LANES), lambda i: (0, 0)),
        ],
        out_specs=[
            pl.BlockSpec((ROW_TILE, SEQ), lambda i: (0, 0)),
            pl.BlockSpec((tm, LANES), lambda i: (i, 0)),
            pl.BlockSpec((ROW_TILE, PLAN_W), lambda i: (0, 0)),
        ],
        out_shape=[
            jax.ShapeDtypeStruct((ROW_TILE, SEQ), jnp.int32),
            jax.ShapeDtypeStruct((SEQ, LANES), jnp.float32),
            jax.ShapeDtypeStruct((ROW_TILE, PLAN_W), jnp.int32),
        ],
        scratch_shapes=[pltpu.VMEM((LANES, LANES), jnp.float32),
                        pltpu.VMEM((ROW_TILE, SEQ), jnp.int32)],
        compiler_params=pltpu.CompilerParams(
            dimension_semantics=("arbitrary",), vmem_limit_bytes=VMEM_LIMIT),
        name="route",
    )(h, mod, wrt, brt)


SRC_UNROLL = 8
EXP_CHUNK = 256
TILE_ROWS = EXP_TM * ROW_TILE


def _experts_kernel(te_ref, tv_ref, nact_ref, tnext_ref, pos_ref,
                    h_hbm, mod_ref, wg_hbm, wu_hbm, wd_hbm,
                    y_ref,
                    src_ref, hres, xbuf, wgs, wus, wds, wgb, wub, wdb, rsem, wsem):
    i = pl.program_id(0)
    last = pl.num_programs(0) - 1
    nact = nact_ref[0]
    xcur = lax.rem(i, 2)
    part = lax.shift_right_logical(te_ref[i], N_EXPERTS.bit_length() - 1)

    def weight_copies(segment):
        e = segment & (N_EXPERTS - 1)
        return [pltpu.make_async_copy(w_hbm.at[e], stage, wsem.at[n])
                for n, (w_hbm, stage) in enumerate(((wg_hbm, wgs), (wu_hbm, wus), (wd_hbm, wds)))]

    def gather_row(tile, k, s):
        local = (src_ref[tile * EXP_TM + k] - part * PART_TOKENS) & (PART_TOKENS - 1)
        xbuf[s, k * ROW_TILE:(k + 1) * ROW_TILE, :] = hres[pl.ds(pl.multiple_of(local * ROW_TILE, ROW_TILE),
                                                             ROW_TILE), :]

    def gather_items(tile, s):
        return [functools.partial(gather_row, tile, k, s) for k in range(EXP_TM)]

    def compute_chunks(xs):
        state = {}

        def load():
            x = _load_row_tiles(xbuf, EXP_TM, lead=(xs,))
            shift = mod_ref[0:1, 3 * D_MODEL:4 * D_MODEL]
            scale = mod_ref[0:1, 4 * D_MODEL:5 * D_MODEL]
            state["x"] = (x * (1.0 + scale) + shift).astype(jnp.bfloat16)
            state["act"] = []

        def gate(c):
            def run():
                if c == 0:
                    load()
                state["a"] = _bdot(state["x"], wgb[:, c * EXP_CHUNK:(c + 1) * EXP_CHUNK])
            return run

        def up(c):
            def run():
                a = state["a"]
                u = _bdot(state["x"], wub[:, c * EXP_CHUNK:(c + 1) * EXP_CHUNK])
                state["act"].append((a * jax.nn.sigmoid(a) * u).astype(jnp.bfloat16))
            return run

        def down(c):
            def run():
                if c == 0:
                    state["actf"] = jnp.concatenate(state["act"], axis=-1)
                yc = _bdot(state["actf"], wdb[:, c * EXP_CHUNK:(c + 1) * EXP_CHUNK])
                for jj in range(EXP_CHUNK // LANES):
                    j = c * (EXP_CHUNK // LANES) + jj
                    y_ref[pl.ds(j, EXP_TM, stride=ROW_TILE), :] = yc[:, jj * LANES:(jj + 1) * LANES]
            return run

        steps = []
        for c in range(D_EXPERT // EXP_CHUNK):
            steps += [(gate(c), 2), (up(c), 2)]
        return steps + [(down(c), 1) for c in range(D_MODEL // EXP_CHUNK)]

    @pl.when(i == 0)
    def _():
        for cp in weight_copies(te_ref[0]):
            cp.start()

        def fill_body(tt, c):
            ts = [tt * SRC_UNROLL + u for u in range(SRC_UNROLL)]
            words = [pos_ref[t] for t in ts]
            for t, w in zip(ts, words):
                src_ref[w & ((1 << POS_BITS) - 1)] = t
                src_ref[lax.shift_right_logical(w, POS_BITS)] = t
            return c
        lax.fori_loop(0, SEQ // SRC_UNROLL, fill_body, 0)

        def pad_tile(t, c):
            pad_tok = lax.shift_right_logical(te_ref[t], N_EXPERTS.bit_length() - 1) * PART_TOKENS

            def pad_row(k, c2):
                src_ref[t * EXP_TM + k] = pad_tok
                return c2
            return lax.fori_loop(tv_ref[t], EXP_TM, pad_row, c)
        lax.fori_loop(0, nact, pad_tile, 0)

    active = i < nact
    prev = jnp.maximum(i - 1, 0)
    new_segment = (i == 0) | (te_ref[i] != te_ref[prev])

    @pl.when(active & ((i == 0) | (part != lax.shift_right_logical(te_ref[prev], N_EXPERTS.bit_length() - 1))))
    def _():
        rows = PART_TOKENS * ROW_TILE
        cp = pltpu.make_async_copy(h_hbm.at[pl.ds(pl.multiple_of(part * rows, rows), rows), :], hres, rsem.at[0])
        cp.start()
        cp.wait()
        for item in gather_items(i, xcur):
            item()

    @pl.when(active & new_segment)
    def _():
        for cp in weight_copies(te_ref[i]):
            cp.wait()
        wgb[...] = wgs[...].astype(jnp.bfloat16)
        wub[...] = wus[...].astype(jnp.bfloat16)
        wdb[...] = wds[...].astype(jnp.bfloat16)
        nxt = tnext_ref[i]

        @pl.when(nxt < nact)
        def _():
            for cp in weight_copies(te_ref[jnp.minimum(nxt, last)]):
                cp.start()

    @pl.when(active)
    def _():
        chunks = compute_chunks(xcur)
        items = gather_items(jnp.minimum(i + 1, nact - 1), 1 - xcur)
        total_cost = sum(cost for _, cost in chunks)
        done = 0
        for chunk, cost in chunks:
            upto = -(-len(items) * (done + cost) // total_cost)
            for item in items[-(-len(items) * done // total_cost):upto]:
                item()
            done += cost
            chunk()

    @pl.when(jnp.logical_not(active))
    def _():
        y_ref[...] = jnp.zeros_like(y_ref)


def _experts_call(te, tv, nact, tnext, pos, h, mod, wg, wu, wd):
    grid_spec = pltpu.PrefetchScalarGridSpec(
        num_scalar_prefetch=5,
        grid=(EXP_TILES,),
        in_specs=[
            pl.BlockSpec(memory_space=pl.ANY),
            pl.BlockSpec(mod.shape, lambda i, *_: (0, 0)),
            pl.BlockSpec(memory_space=pl.ANY),
            pl.BlockSpec(memory_space=pl.ANY),
            pl.BlockSpec(memory_space=pl.ANY),
        ],
        out_specs=pl.BlockSpec((TILE_ROWS, LANES), lambda i, *_: (i, 0)),
        scratch_shapes=[
            pltpu.SMEM((EXP_TILES * EXP_TM,), jnp.int32),
            pltpu.VMEM((PART_TOKENS * ROW_TILE, LANES), jnp.float32),
            pltpu.VMEM((2, TILE_ROWS, LANES), jnp.float32),
            pltpu.VMEM((D_MODEL, D_EXPERT), jnp.float32),
            pltpu.VMEM((D_MODEL, D_EXPERT), jnp.float32),
            pltpu.VMEM((D_EXPERT, D_MODEL), jnp.float32),
            pltpu.VMEM((D_MODEL, D_EXPERT), jnp.bfloat16),
            pltpu.VMEM((D_MODEL, D_EXPERT), jnp.bfloat16),
            pltpu.VMEM((D_EXPERT, D_MODEL), jnp.bfloat16),
            pltpu.SemaphoreType.DMA((1,)),
            pltpu.SemaphoreType.DMA((3,)),
        ],
    )
    return pl.pallas_call(
        _experts_kernel,
        grid_spec=grid_spec,
        out_shape=jax.ShapeDtypeStruct((EXP_TILES * TILE_ROWS, LANES), jnp.float32),
        compiler_params=pltpu.CompilerParams(
            dimension_semantics=("arbitrary",), vmem_limit_bytes=EXPERTS_VMEM_LIMIT),
        name="experts",
    )(te, tv, nact, tnext, pos, h, mod, wg, wu, wd)


def _combine_kernel(pos_ref, h_ref, cw_ref, mod_ref, g_ref, b_ref, ys_hbm, o_ref, ybuf, sem):
    i = pl.program_id(0)
    tm = CMB_TM
    slot = lax.rem(i, 2)

    def start_row(tile, k, s):
        word = pos_ref[tile * tm + k]
        for half, p in ((0, word & ((1 << POS_BITS) - 1)), (1, lax.shift_right_logical(word, POS_BITS))):
            pltpu.make_async_copy(ys_hbm.at[pl.ds(pl.multiple_of(p * ROW_TILE, ROW_TILE), ROW_TILE), :],
                                  ybuf.at[s, half, pl.ds(pl.multiple_of(k * ROW_TILE, ROW_TILE), ROW_TILE), :],
                                  sem.at[s]).start(priority=half)

    @pl.when(i == 0)
    def _():
        def body(kk, c):
            for u in range(SRC_UNROLL):
                start_row(0, kk * SRC_UNROLL + u, 0)
            return c
        lax.fori_loop(0, tm // SRC_UNROLL, body, 0)

    for half in range(2):
        pltpu.make_async_copy(ys_hbm.at[pl.ds(0, tm * ROW_TILE), :], ybuf.at[slot, half], sem.at[slot]).wait()

    g2 = mod_ref[0:1, 5 * D_MODEL:6 * D_MODEL]

    def chunk(c):
        rows = slice(c * CMB_CHUNK, (c + 1) * CMB_CHUNK)
        y0 = _load_row_tiles(ybuf, CMB_CHUNK, lead=(slot, 0), first=c * CMB_CHUNK)
        y1 = _load_row_tiles(ybuf, CMB_CHUNK, lead=(slot, 1), first=c * CMB_CHUNK)
        ffn = cw_ref[rows, CW_C0:CW_C0 + 1] * y0 + cw_ref[rows, CW_C1:CW_C1 + 1] * y1
        h = _load_row_tiles(h_ref, CMB_CHUNK, first=c * CMB_CHUNK)
        o_ref[rows, :] = _layer_norm(DEEPNORM_ALPHA * h + g2 * ffn, g_ref[...], b_ref[...])

    n_chunks = tm // CMB_CHUNK

    @pl.when(i + 1 < pl.num_programs(0))
    def _():
        per = tm // n_chunks
        for c in range(n_chunks):
            for k in range(c * per, (c + 1) * per):
                start_row(i + 1, k, 1 - slot)
            chunk(c)

    @pl.when(i + 1 == pl.num_programs(0))
    def _():
        for c in range(n_chunks):
            chunk(c)


def _combine_call(pos, h, ys, cw, mod, g, b):
    tm = CMB_TM
    grid_spec = pltpu.PrefetchScalarGridSpec(
        num_scalar_prefetch=1,
        grid=(SEQ // tm,),
        in_specs=[
            pl.BlockSpec((tm * ROW_TILE, LANES), lambda i, *_: (i, 0)),
            pl.BlockSpec((tm, LANES), lambda i, *_: (i, 0)),
            pl.BlockSpec(mod.shape, lambda i, *_: (0, 0)),
            pl.BlockSpec((1, D_MODEL), lambda i, *_: (0, 0)),
            pl.BlockSpec((1, D_MODEL), lambda i, *_: (0, 0)),
            pl.BlockSpec(memory_space=pl.ANY),
        ],
        out_specs=pl.BlockSpec((tm, D_MODEL), lambda i, *_: (i, 0)),
        scratch_shapes=[
            pltpu.VMEM((2, 2, tm * ROW_TILE, LANES), jnp.float32),
            pltpu.SemaphoreType.DMA((2,)),
        ],
    )
    return pl.pallas_call(
        _combine_kernel,
        grid_spec=grid_spec,
        out_shape=jax.ShapeDtypeStruct((SEQ, D_MODEL), jnp.float32),
        compiler_params=pltpu.CompilerParams(
            dimension_semantics=("arbitrary",), vmem_limit_bytes=VMEM_LIMIT),
        name="combine",
    )(pos, h, cw, mod, g, b, ys)


def kernel(x, c, ctx, c_ctx, ln_in_g, ln_in_b, w_mod, b_mod, w_in, rpb, w_pool_grp, pool_scale,
           w_attn_proj, w_pool_proj, w_out, ln1_g, ln1_b, w_router_group, b_router_group,
           w_router_expert, b_router_expert, w_expert_gate, w_expert_up, w_expert_down, ln2_g, ln2_b):
    assert x.shape == (1, SEQ, D_MODEL) and ctx.shape == (1, CTX_LEN, D_MODEL)
    assert w_mod.shape[0] == 1, "single-layer trunk"
    f32, bf16 = jnp.float32, jnp.bfloat16
    row = lambda v: v.reshape(1, -1).astype(f32)

    cond = jnp.concatenate([c, c_ctx[None], jnp.zeros((MOD_ROWS - 2, D_MODEL), f32)], axis=0)
    mod = _mod_call(cond, w_mod[0], row(b_mod[0]))

    lng, lnb = row(ln_in_g), row(ln_in_b)
    w_in_b = w_in[0].astype(bf16)
    u, h0 = _proj_call(x[0], mod, lng, lnb, w_in_b, mod_row=0, latent=True, tm=PROJ_TM)
    kvc, = _proj_call(ctx[0], mod, lng, lnb, w_in_b[:, ATT_W:3 * ATT_W], mod_row=1, latent=False, tm=CTX_LEN)

    h1 = _mix_call(h0, mod, u, kvc, _attn_bias_table(rpb[0]),
                   w_pool_grp[0].astype(bf16), row(pool_scale[0]),
                   w_attn_proj[0].astype(bf16), w_pool_proj[0].astype(bf16), w_out[0].astype(bf16),
                   row(ln1_g[0]), row(ln1_b[0]))

    n_logit = N_GROUPS + N_EXPERTS
    wrt = jnp.concatenate([w_router_group[0].T, w_router_expert[0].T,
                           jnp.zeros((LANES - n_logit, D_MODEL), f32)], axis=0)
    brt = jnp.concatenate([b_router_group[0], b_router_expert[0], jnp.zeros((LANES - n_logit,), f32)])
    brt = jnp.broadcast_to(brt[:, None], (LANES, LANES))
    pos, cw, plan = _route_call(h1, mod, wrt, brt)

    y = _experts_call(plan[PLAN_EXPERT, :EXP_TILES], plan[PLAN_VALID, :EXP_TILES], plan[PLAN_NACT, :1],
                      plan[PLAN_NEXT, :EXP_TILES], pos[POS_PACKED], h1, mod,
                      w_expert_gate[0], w_expert_up[0], w_expert_down[0])
    out = _combine_call(pos[POS_PACKED], h1, y, cw, mod, row(ln2_g[0]), row(ln2_b[0]))
    return out[None]
```

```python
import functools

import jax
import jax.numpy as jnp
from jax import lax
from jax.experimental import pallas as pl
from jax.experimental.pallas import tpu as pltpu

D_MODEL = 1024
SEQ = 16384
GRID_W = 64
ROWS = SEQ // GRID_W
CTX_LEN = 256
N_HEADS = 8
HEAD_DIM = 64
ATT_W = N_HEADS * HEAD_DIM
WIN_H = 8
WIN_W = 16
POOL_WINDOWS = (2, 4, 8, 16)
POOL_GROUPS = 4
POOL_DIM = 128
POOL_W = POOL_GROUPS * POOL_DIM
PROJ_W = 3 * ATT_W + POOL_W + 2 * D_MODEL
GATE_COL = 3 * ATT_W + POOL_W
N_GROUPS = 4
EXPERTS_PER_GROUP = 8
N_EXPERTS = N_GROUPS * EXPERTS_PER_GROUP
D_EXPERT = 512
N_MOD = 6
DEEPNORM_ALPHA = 2.0 ** 0.25
LN_EPS = 1e-5
NEG_INF = -1e30

LANES = 128
ROW_TILE = 8
MOD_ROWS = 8
PROJ_TM = 512
PROJ_SUB = 256
MIX_ROWS = 8
MIX_TQ = MIX_ROWS * GRID_W
KV_HALO = 4 * GRID_W
POOL_HALO = 16
ROUTE_TM = 512
EXP_TM = 256
SEQ_PARTS = 2
PART_TOKENS = SEQ // SEQ_PARTS
N_SEG = SEQ_PARTS * N_EXPERTS
EXP_TILES = 2 * SEQ // EXP_TM + N_SEG
CMB_TM = 512
CMB_CHUNK = 128
MERGE_ROWS = 128
ATTN_AHEAD = 2
HALF = D_MODEL // 2
VMEM_LIMIT = 56 * 1024 * 1024
EXPERTS_VMEM_LIMIT = 60 * 1024 * 1024


def _layer_norm(x, g, b):
    mu = jnp.mean(x, axis=-1, keepdims=True)
    xc = x - mu
    var = jnp.mean(xc * xc, axis=-1, keepdims=True)
    return xc * lax.rsqrt(var + LN_EPS) * g + b


def _bdot(a, b):
    return jnp.dot(a, b, preferred_element_type=jnp.float32)


def _split_bf16(a):
    hi = a.astype(jnp.bfloat16)
    lo = (a - hi.astype(jnp.float32)).astype(jnp.bfloat16)
    return hi, lo


def _dot3(a, b):
    a_hi, a_lo = _split_bf16(a)
    b_hi, b_lo = _split_bf16(b)
    return _bdot(a_hi, b_hi) + (_bdot(a_hi, b_lo) + _bdot(a_lo, b_hi))


def _load_row_tiles(ref, tokens, lead=(), first=0):
    parts = [ref[(*lead, pl.ds(first * ROW_TILE + j, tokens, stride=ROW_TILE), slice(None))]
             for j in range(ROW_TILE)]
    return jnp.concatenate(parts, axis=-1)


def _store_row_tiles(ref, value, lead=()):
    tokens = value.shape[0]
    for j in range(ROW_TILE):
        ref[(*lead, pl.ds(j, tokens, stride=ROW_TILE), slice(None))] = value[:, j * LANES:(j + 1) * LANES]


def _mod_kernel(cond_ref, w_ref, b_ref, o_ref):
    cond = cond_ref[...]
    act = cond * jax.nn.sigmoid(cond)
    o_ref[...] = _dot3(act, w_ref[...]) + b_ref[...]


def _mod_call(cond, w_mod, b_mod):
    tn = 1536
    n = N_MOD * D_MODEL
    return pl.pallas_call(
        _mod_kernel,
        grid=(n // tn,),
        in_specs=[
            pl.BlockSpec((MOD_ROWS, D_MODEL), lambda i: (0, 0)),
            pl.BlockSpec((D_MODEL, tn), lambda i: (0, i)),
            pl.BlockSpec((1, tn), lambda i: (0, i)),
        ],
        out_specs=pl.BlockSpec((MOD_ROWS, tn), lambda i: (0, i)),
        out_shape=jax.ShapeDtypeStruct((MOD_ROWS, n), jnp.float32),
        compiler_params=pltpu.CompilerParams(
            dimension_semantics=("arbitrary",), vmem_limit_bytes=VMEM_LIMIT),
        name="mod",
    )(cond, w_mod, b_mod)


def _proj_kernel(x_ref, mod_ref, g_ref, b_ref, w_ref, o_ref, *h_out, mod_row, latent):
    shift = mod_ref[mod_row:mod_row + 1, 0:D_MODEL]
    scale = mod_ref[mod_row:mod_row + 1, D_MODEL:2 * D_MODEL]
    tm, n = o_ref.shape
    sub = min(tm, PROJ_SUB)

    def prep(r):
        rows = slice(r * sub, (r + 1) * sub)
        h = _layer_norm(x_ref[rows, :], g_ref[...], b_ref[...])
        if latent:
            h_out[0][rows, :] = h
        return (h * (1.0 + scale) + shift).astype(jnp.bfloat16)

    def finish(r, c, res):
        if latent and c == 0:
            lane = lax.broadcasted_iota(jnp.int32, (1, D_MODEL), 1)
            res = res * jnp.where(lane < ATT_W, HEAD_DIM ** -0.5, 1.0)
        if latent and c * D_MODEL >= GATE_COL:
            res = jax.nn.sigmoid(res)
        o_ref[r * sub:(r + 1) * sub, c * D_MODEL:(c + 1) * D_MODEL] = res.astype(jnp.bfloat16)

    hm = {0: prep(0)}
    waiting = None
    for r in range(tm // sub):
        for c in range(n // D_MODEL):
            res = _bdot(hm[r], w_ref[:, c * D_MODEL:(c + 1) * D_MODEL])
            if c == 0 and (r + 1) * sub < tm:
                hm[r + 1] = prep(r + 1)
            if waiting is not None:
                finish(*waiting)
            waiting = (r, c, res)
    finish(*waiting)


def _proj_call(x, mod, g, b, w, *, mod_row, latent, tm):
    rows, n = x.shape[0], w.shape[1]
    out_specs = [pl.BlockSpec((tm, n), lambda i: (i, 0))]
    out_shape = [jax.ShapeDtypeStruct((rows, n), jnp.bfloat16)]
    if latent:
        out_specs.append(pl.BlockSpec((tm, D_MODEL), lambda i: (i, 0)))
        out_shape.append(jax.ShapeDtypeStruct((rows, D_MODEL), jnp.float32))
    return pl.pallas_call(
        functools.partial(_proj_kernel, mod_row=mod_row, latent=latent),
        grid=(rows // tm,),
        in_specs=[
            pl.BlockSpec((tm, D_MODEL), lambda i: (i, 0)),
            pl.BlockSpec(mod.shape, lambda i: (0, 0)),
            pl.BlockSpec((1, D_MODEL), lambda i: (0, 0)),
            pl.BlockSpec((1, D_MODEL), lambda i: (0, 0)),
            pl.BlockSpec((D_MODEL, n), lambda i: (0, 0), pipeline_mode=pl.Buffered(1)),
        ],
        out_specs=out_specs,
        out_shape=out_shape,
        compiler_params=pltpu.CompilerParams(
            dimension_semantics=("arbitrary",), vmem_limit_bytes=VMEM_LIMIT),
        name="proj",
    )(x, mod, g, b, w)


def _attn_bias_table(rpb):
    col = jnp.arange(GRID_W, dtype=jnp.int32)
    col_start = jnp.clip(col - WIN_W // 2, 0, GRID_W - WIN_W)
    col_mask = (col[None, :] >= col_start[:, None]) & (col[None, :] < col_start[:, None] + WIN_W)
    col_off = jnp.clip(col[None, :] - col[:, None], 1 - WIN_W, WIN_W - 1) + (WIN_W - 1)
    onehot = (col_off[None] == jnp.arange(2 * WIN_W - 1, dtype=jnp.int32)[:, None, None]).astype(jnp.float32)
    tab = jnp.einsum("hrc,cqk->hqrk", rpb.astype(jnp.float32), onehot, precision=lax.Precision.HIGHEST)
    tab = jnp.where(col_mask[None, :, None, :], tab, NEG_INF)
    tab = jnp.stack([tab[:, :, WIN_H - 1 - v:2 * WIN_H - 1 - v] for v in range(WIN_H)], axis=0)
    return tab.reshape(WIN_H, N_HEADS // 2, 2 * GRID_W, WIN_H * GRID_W)


def _mix_kernel(h_ref, mod_ref,
                q_ref, kp_ref, kc_ref, kn_ref, vp_ref, vc_ref, vn_ref,
                pp_ref, pc_ref, pn_ref, ga_ref, gb_ref,
                kvc_ref, bias_ref, wgrp_ref, pscale_ref, wap_ref, wpp_ref, wout_ref,
                ln1g_ref, ln1b_ref,
                o_ref,
                kbuf, vbuf, yabuf, pbuf, ypbuf, zbuf):
    b = pl.program_id(0)
    nb = pl.num_programs(0)

    kbuf[0:KV_HALO, :] = kp_ref[...]
    kbuf[KV_HALO:KV_HALO + MIX_TQ, :] = kc_ref[...]
    kbuf[KV_HALO + MIX_TQ:, :] = kn_ref[...]
    vbuf[0:KV_HALO, :] = vp_ref[...]
    vbuf[KV_HALO:KV_HALO + MIX_TQ, :] = vc_ref[...]
    vbuf[KV_HALO + MIX_TQ:, :] = vn_ref[...]

    lane = lax.broadcasted_iota(jnp.int32, (GRID_W, LANES), 1)
    first_head = lane < HEAD_DIM

    units = [(j, pair) for j in range(MIX_ROWS) for pair in range(N_HEADS // 2)]
    nt = (((1,), (1,)), ((), ()))

    def window(j):
        r = b * MIX_ROWS + j
        rs = jnp.clip(r - WIN_H // 2, 0, ROWS - WIN_H)
        return pl.multiple_of((rs - b * MIX_ROWS + WIN_H // 2) * GRID_W, GRID_W), r - rs

    def scores(j, pair):
        off, var = window(j)
        cols = slice(pair * LANES, (pair + 1) * LANES)
        q = q_ref[j * GRID_W:(j + 1) * GRID_W, cols]
        zero = jnp.zeros_like(q)
        q2 = jnp.concatenate([jnp.where(first_head, q, zero), jnp.where(first_head, zero, q)], axis=0)
        kw = kbuf[pl.ds(off, WIN_H * GRID_W), cols]
        s_loc = lax.dot_general(q2, kw, nt, preferred_element_type=jnp.float32) + bias_ref[var, pair]
        s_ctx = lax.dot_general(q2, kvc_ref[:, cols], nt, preferred_element_type=jnp.float32)
        return s_loc, s_ctx

    def values(j, pair, s_loc, s_ctx):
        off, _ = window(j)
        cols = slice(pair * LANES, (pair + 1) * LANES)
        vw = vbuf[pl.ds(off, WIN_H * GRID_W), cols]
        vctx = kvc_ref[:, ATT_W + pair * LANES:ATT_W + (pair + 1) * LANES]
        m = jnp.maximum(jnp.max(s_loc, axis=-1, keepdims=True), jnp.max(s_ctx, axis=-1, keepdims=True))
        p_loc = jnp.exp(s_loc - m).astype(jnp.bfloat16)
        p_ctx = jnp.exp(s_ctx - m).astype(jnp.bfloat16)
        ones = lambda rows: jnp.ones((rows, LANES), jnp.bfloat16)
        o2 = (_bdot(p_loc, jnp.concatenate([vw, ones(WIN_H * GRID_W)], axis=1))
              + _bdot(p_ctx, jnp.concatenate([vctx, ones(CTX_LEN)], axis=1)))
        o2 = o2[:, :LANES] / o2[:, LANES:]
        o_pair = jnp.where(first_head, o2[:GRID_W], o2[GRID_W:])
        yabuf[j * GRID_W:(j + 1) * GRID_W, cols] = o_pair.astype(jnp.bfloat16)

    pbuf[0:POOL_HALO, :] = jnp.where(b > 0, pp_ref[...].astype(jnp.float32), 0.0)
    pbuf[POOL_HALO:POOL_HALO + MIX_TQ, :] = pc_ref[...].astype(jnp.float32)
    pbuf[POOL_HALO + MIX_TQ:, :] = jnp.where(b < nb - 1, pn_ref[...].astype(jnp.float32), 0.0)
    edge = lax.broadcasted_iota(jnp.int32, (ROW_TILE, 1), 0)

    def pool_group(g):
        win = POOL_WINDOWS[g]
        lo, hi = win // 2, win - win // 2
        cols = slice(g * POOL_DIM, (g + 1) * POOL_DIM)
        acc = None
        for d in range(-lo, hi):
            term = pbuf[POOL_HALO + d:POOL_HALO + d + MIX_TQ, cols]
            acc = term if acc is None else acc + term
        assert max(lo, hi) <= ROW_TILE
        top = jnp.where(b == 0, 1.0 / (win - jnp.maximum(lo - edge, 0)).astype(jnp.float32), 1.0 / win)
        bot = jnp.where(b == nb - 1,
                        1.0 / (win - jnp.maximum(edge + hi - ROW_TILE, 0)).astype(jnp.float32), 1.0 / win)
        inv = jnp.concatenate([top, jnp.full((MIX_TQ - 2 * ROW_TILE, 1), 1.0 / win, jnp.float32), bot], axis=0)
        pooled = acc * inv - pbuf[POOL_HALO:POOL_HALO + MIX_TQ, cols]
        yp = _bdot(pooled.astype(jnp.bfloat16), wgrp_ref[g]) * pscale_ref[:, cols]
        ypbuf[:, cols] = yp.astype(jnp.bfloat16)

    def pooled_branch():
        zbuf[...] = gb_ref[...].astype(jnp.float32) * _bdot(ypbuf[...], wpp_ref[...])

    extra = {}
    for g in range(POOL_GROUPS):
        extra[(g + 1) * len(units) // (POOL_GROUPS + 2)] = functools.partial(pool_group, g)
    extra[(POOL_GROUPS + 1) * len(units) // (POOL_GROUPS + 2)] = pooled_branch
    pending = [scores(*u) for u in units[:ATTN_AHEAD]]
    for n, u in enumerate(units):
        if n + ATTN_AHEAD < len(units):
            pending.append(scores(*units[n + ATTN_AHEAD]))
        values(*u, *pending.pop(0))
        if n in extra:
            extra[n]()

    g1 = mod_ref[0:1, 2 * D_MODEL:3 * D_MODEL]
    n_chunks = MIX_TQ // MERGE_ROWS
    rows = lambda c: slice(c * MERGE_ROWS, (c + 1) * MERGE_ROWS)
    z, y = {}, {}

    def stage_a(c):
        ya = _bdot(yabuf[rows(c), :], wap_ref[...])
        z[c] = (ga_ref[rows(c), :].astype(jnp.float32) * ya + zbuf[rows(c), :]).astype(jnp.bfloat16)

    def stage_b(c):
        y[c] = _bdot(z.pop(c), wout_ref[...])

    def stage_c(c):
        hn = _layer_norm(DEEPNORM_ALPHA * h_ref[rows(c), :] + g1 * y.pop(c), ln1g_ref[...], ln1b_ref[...])
        for j in range(ROW_TILE):
            o_ref[pl.ds(c * MERGE_ROWS * ROW_TILE + j, MERGE_ROWS, stride=ROW_TILE), :] = hn[:, j * LANES:(j + 1) * LANES]

    for t in range(n_chunks + 2):
        if t < n_chunks:
            stage_a(t)
        if 0 <= t - 1 < n_chunks:
            stage_b(t - 1)
        if 0 <= t - 2 < n_chunks:
            stage_c(t - 2)


def _mix_call(h, mod, u, kvc, bias, wgrp, pscale, wap, wpp, wout, ln1g, ln1b):
    nb = SEQ // MIX_TQ
    halo_per_blk = MIX_TQ // KV_HALO
    n_halo = SEQ // KV_HALO
    ph_per_blk = MIX_TQ // POOL_HALO
    n_ph = SEQ // POOL_HALO

    def const(shape):
        return pl.BlockSpec(shape, lambda i: (0,) * len(shape), pipeline_mode=pl.Buffered(1))

    def prev_halo(c):
        return pl.BlockSpec((KV_HALO, ATT_W), lambda i: (jnp.maximum(i * halo_per_blk - 1, 0), c))

    def next_halo(c):
        return pl.BlockSpec((KV_HALO, ATT_W), lambda i: (jnp.minimum((i + 1) * halo_per_blk, n_halo - 1), c))

    def cur(c):
        return pl.BlockSpec((MIX_TQ, ATT_W), lambda i: (i, c))

    in_specs = [
        pl.BlockSpec((MIX_TQ, D_MODEL), lambda i: (i, 0)),
        const(mod.shape),
        cur(0),
        prev_halo(1), cur(1), next_halo(1),
        prev_halo(2), cur(2), next_halo(2),
        pl.BlockSpec((POOL_HALO, POOL_W), lambda i: (jnp.maximum(i * ph_per_blk - 1, 0), 3)),
        cur(3),
        pl.BlockSpec((POOL_HALO, POOL_W), lambda i: (jnp.minimum((i + 1) * ph_per_blk, n_ph - 1), 3)),
        pl.BlockSpec((MIX_TQ, D_MODEL), lambda i: (i, 2)),
        pl.BlockSpec((MIX_TQ, D_MODEL), lambda i: (i, 3)),
        const(kvc.shape), const(bias.shape), const(wgrp.shape), const(pscale.shape),
        const(wap.shape), const(wpp.shape), const(wout.shape),
        const((1, D_MODEL)), const((1, D_MODEL)),
    ]
    return pl.pallas_call(
        _mix_kernel,
        grid=(nb,),
        in_specs=in_specs,
        out_specs=pl.BlockSpec((MIX_TQ * ROW_TILE, LANES), lambda i: (i, 0)),
        out_shape=jax.ShapeDtypeStruct((SEQ * ROW_TILE, LANES), jnp.float32),
        scratch_shapes=[
            pltpu.VMEM((MIX_TQ + 2 * KV_HALO, ATT_W), jnp.bfloat16),
            pltpu.VMEM((MIX_TQ + 2 * KV_HALO, ATT_W), jnp.bfloat16),
            pltpu.VMEM((MIX_TQ, ATT_W), jnp.bfloat16),
            pltpu.VMEM((MIX_TQ + 2 * POOL_HALO, POOL_W), jnp.float32),
            pltpu.VMEM((MIX_TQ, POOL_W), jnp.bfloat16),
            pltpu.VMEM((MIX_TQ, D_MODEL), jnp.float32),
        ],
        compiler_params=pltpu.CompilerParams(
            dimension_semantics=("arbitrary",), vmem_limit_bytes=VMEM_LIMIT),
        name="mix",
    )(h, mod, u, u, u, u, u, u, u, u, u, u, u, u,
      kvc, bias, wgrp, pscale, wap, wpp, wout, ln1g, ln1b)


ID_E0, ID_E1, ID_R0, ID_R1 = 0, 1, 4, 5
POS_PACKED = 0
POS_BITS = 16
PLAN_EXPERT, PLAN_VALID, PLAN_NACT, PLAN_NEXT = 0, 1, 2, 3
PLAN_W = 2 * LANES
CW_C0, CW_C1 = 0, 1


def _route_kernel(h_ref, mod_ref, wrt_ref, brt_ref, pos_ref, cw_ref, plan_ref, carry_ref, ids_all):
    i = pl.program_id(0)
    tm = ROUTE_TM

    @pl.when(i == 0)
    def _():
        carry_ref[...] = jnp.zeros_like(carry_ref)

    shift = mod_ref[0:1, 3 * D_MODEL:4 * D_MODEL]
    scale = mod_ref[0:1, 4 * D_MODEL:5 * D_MODEL]
    hm = _load_row_tiles(h_ref, tm) * (1.0 + scale) + shift

    hm_hi, hm_lo = _split_bf16(hm)
    w_hi, w_lo = _split_bf16(wrt_ref[...])
    nt = (((1,), (1,)), ((), ()))
    dg = functools.partial(lax.dot_general, dimension_numbers=nt, preferred_element_type=jnp.float32)
    logits = dg(w_hi, hm_hi) + (dg(w_hi, hm_lo) + dg(w_lo, hm_hi)) + brt_ref[:, 0:1]

    sub = lax.broadcasted_iota(jnp.int32, (LANES, tm), 0)
    big = jnp.int32(1 << 20)
    is_grp = sub < N_GROUPS
    gl = jnp.where(is_grp, logits, -jnp.inf)
    gmax = jnp.max(gl, axis=0, keepdims=True)
    gidx = jnp.min(jnp.where(gl == gmax, sub, big), axis=0, keepdims=True)
    gsum = jnp.sum(jnp.where(is_grp, jnp.exp(logits - gmax), 0.0), axis=0, keepdims=True)
    p_group = 1.0 / gsum

    eid = sub - N_GROUPS
    sel = (eid >= 0) & (eid < N_EXPERTS) & (lax.shift_right_arithmetic(eid, 3) == gidx)
    el = jnp.where(sel, logits, -jnp.inf)
    l0 = jnp.max(el, axis=0, keepdims=True)
    i0 = jnp.min(jnp.where(el == l0, sub, big), axis=0, keepdims=True)
    el2 = jnp.where(sub == i0, -jnp.inf, el)
    l1 = jnp.max(el2, axis=0, keepdims=True)
    i1 = jnp.min(jnp.where(el2 == l1, sub, big), axis=0, keepdims=True)
    t = jnp.exp(l1 - l0)
    w0 = 1.0 / (1.0 + t)
    w1 = t / (1.0 + t)

    half_rows = jnp.where(i >= pl.num_programs(0) // SEQ_PARTS, N_EXPERTS, 0)
    i0 = i0 + half_rows
    i1 = i1 + half_rows
    onehot = jnp.where((sub == i0) | (sub == i1), 1.0, 0.0)
    rr = lax.broadcasted_iota(jnp.int32, (tm, tm), 0)
    cc = lax.broadcasted_iota(jnp.int32, (tm, tm), 1)
    earlier = jnp.where(rr < cc, 1.0, 0.0).astype(jnp.bfloat16)
    carry = carry_ref[:, 0:1]
    prefix = _bdot(onehot.astype(jnp.bfloat16), earlier) + carry
    r0 = jnp.sum(jnp.where(sub == i0, prefix, 0.0), axis=0, keepdims=True)
    r1 = jnp.sum(jnp.where(sub == i1, prefix, 0.0), axis=0, keepdims=True)
    total = jnp.broadcast_to(carry + jnp.sum(onehot, axis=1, keepdims=True), carry_ref.shape)
    carry_ref[...] = total

    sub8 = lax.broadcasted_iota(jnp.int32, (ROW_TILE, tm), 0)
    ids = jnp.zeros((ROW_TILE, tm), jnp.int32)
    for idx, val in ((ID_E0, i0 - N_GROUPS), (ID_E1, i1 - N_GROUPS),
                     (ID_R0, r0.astype(jnp.int32)), (ID_R1, r1.astype(jnp.int32))):
        ids = jnp.where(sub8 == idx, val, ids)
    ids_all[:, pl.ds(pl.multiple_of(i * tm, tm), tm)] = ids

    cwt = jnp.where(sub == CW_C0, p_group * w0, jnp.where(sub == CW_C1, p_group * w1, 0.0))
    cw_ref[...] = cwt.T

    @pl.when(i == pl.num_programs(0) - 1)
    def _():
        subq = lax.broadcasted_iota(jnp.int32, (LANES, LANES), 0)
        laneq = lax.broadcasted_iota(jnp.int32, (LANES, LANES), 1)
        cnt = total.astype(jnp.int32)
        tiles = lax.shift_right_logical(cnt + (EXP_TM - 1), EXP_TM.bit_length() - 1).astype(jnp.float32)
        incl = jnp.where(laneq <= subq, 1.0, 0.0).astype(jnp.bfloat16)
        tile_end = _bdot(incl, tiles.astype(jnp.bfloat16))
        tile_start = tile_end - tiles
        seg = (tile_start * EXP_TM).astype(jnp.int32)
        nact = jnp.max(tile_end, axis=0, keepdims=True)

        ids_full = ids_all[...]
        look = jnp.zeros_like(ids_full)
        for e in range(N_SEG):
            look = jnp.where(ids_full == e, seg[N_GROUPS + e, 0], look)
        pos01 = look + pltpu.roll(ids_full, ID_R0 - ID_E0, axis=0)
        assert EXP_TILES * EXP_TM <= 1 << POS_BITS
        pos_ref[...] = pos01 | (pltpu.roll(pos01, ROW_TILE - 1, axis=0) << POS_BITS)

        subp = lax.broadcasted_iota(jnp.int32, (LANES, PLAN_W), 0)
        tile = lax.broadcasted_iota(jnp.int32, (LANES, PLAN_W), 1).astype(jnp.float32)
        is_exp = (subp >= N_GROUPS) & (subp < N_GROUPS + N_SEG)
        end_col = tile_end[:, 0:1]
        nact_s = nact[:, 0:1]
        te = jnp.sum(jnp.where(is_exp & (tile >= end_col), 1.0, 0.0), axis=0, keepdims=True)
        te_last = jnp.sum(jnp.where(is_exp & (nact_s - 1.0 >= end_col), 1.0, 0.0), axis=0, keepdims=True)[:, 0:1]
        tile_row = tile[0:1, :]
        te = jnp.minimum(jnp.where(tile_row < nact_s, te, te_last), N_SEG - 1.0)
        mine = (subp - N_GROUPS).astype(jnp.float32) == te
        cnt_sel = jnp.sum(jnp.where(mine, total[:, 0:1], 0.0), axis=0, keepdims=True)
        start_sel = jnp.sum(jnp.where(mine, tile_start[:, 0:1], 0.0), axis=0, keepdims=True)
        end_sel = jnp.sum(jnp.where(mine, end_col, 0.0), axis=0, keepdims=True)
        valid = jnp.clip(cnt_sel - (tile_row - start_sel) * EXP_TM, 0.0, float(EXP_TM))
        valid = jnp.where(tile_row < nact_s, valid, 0.0)
        subr = lax.broadcasted_iota(jnp.int32, (ROW_TILE, PLAN_W), 0)
        plan = jnp.where(subr == PLAN_EXPERT, te, jnp.where(subr == PLAN_VALID, valid,
                         jnp.where(subr == PLAN_NACT, nact_s, jnp.where(subr == PLAN_NEXT, end_sel, 0.0))))
        plan_ref[...] = plan.astype(jnp.int32)


def _route_call(h, mod, wrt, brt):
    tm = ROUTE_TM
    return pl.pallas_call(
        _route_kernel,
        grid=(SEQ // tm,),
        in_specs=[
            pl.BlockSpec((tm * ROW_TILE, LANES), lambda i: (i, 0)),
            pl.BlockSpec(mod.shape, lambda i: (0, 0)),
            pl.BlockSpec((LANES, D_MODEL), lambda i: (0, 0)),
            pl.BlockSpec((LANES, LANES), lambda i: (0, 0)),
        ],
        out_specs=[
            pl.BlockSpec((ROW_TILE, SEQ), lambda i: (0, 0)),
            pl.BlockSpec((tm, LANES), lambda i: (i, 0)),
            pl.BlockSpec((ROW_TILE, PLAN_W), lambda i: (0, 0)),
        ],
        out_shape=[
            jax.ShapeDtypeStruct((ROW_TILE, SEQ), jnp.int32),
            jax.ShapeDtypeStruct((SEQ, LANES), jnp.float32),
            jax.ShapeDtypeStruct((ROW_TILE, PLAN_W), jnp.int32),
        ],
        scratch_shapes=[pltpu.VMEM((LANES, LANES), jnp.float32),
                        pltpu.VMEM((ROW_TILE, SEQ), jnp.int32)],
        compiler_params=pltpu.CompilerParams(
            dimension_semantics=("arbitrary",), vmem_limit_bytes=VMEM_LIMIT),
        name="route",
    )(h, mod, wrt, brt)


SRC_UNROLL = 8
EXP_CHUNK = 256
TILE_ROWS = EXP_TM * ROW_TILE


def _experts_kernel(te_ref, tv_ref, nact_ref, tnext_ref, pos_ref,
                    h_hbm, mod_ref, wg_hbm, wu_hbm, wd_hbm,
                    y_ref,
                    src_ref, hres, xbuf, wgs, wus, wds, wgb, wub, wdb, rsem, wsem):
    i = pl.program_id(0)
    last = pl.num_programs(0) - 1
    nact = nact_ref[0]
    xcur = lax.rem(i, 2)
    part = lax.shift_right_logical(te_ref[i], N_EXPERTS.bit_length() - 1)

    def weight_copies(segment):
        e = segment & (N_EXPERTS - 1)
        return [pltpu.make_async_copy(w_hbm.at[e], stage, wsem.at[n])
                for n, (w_hbm, stage) in enumerate(((wg_hbm, wgs), (wu_hbm, wus), (wd_hbm, wds)))]

    def gather_row(tile, k, s):
        local = (src_ref[tile * EXP_TM + k] - part * PART_TOKENS) & (PART_TOKENS - 1)
        xbuf[s, k * ROW_TILE:(k + 1) * ROW_TILE, :] = hres[pl.ds(pl.multiple_of(local * ROW_TILE, ROW_TILE),
                                                             ROW_TILE), :]

    def gather_items(tile, s):
        return [functools.partial(gather_row, tile, k, s) for k in range(EXP_TM)]

    def compute_chunks(xs):
        state = {}

        def load():
            x = _load_row_tiles(xbuf, EXP_TM, lead=(xs,))
            shift = mod_ref[0:1, 3 * D_MODEL:4 * D_MODEL]
            scale = mod_ref[0:1, 4 * D_MODEL:5 * D_MODEL]
            state["x"] = (x * (1.0 + scale) + shift).astype(jnp.bfloat16)
            state["act"] = []

        def gate(c):
            def run():
                if c == 0:
                    load()
                state["a"] = _bdot(state["x"], wgb[:, c * EXP_CHUNK:(c + 1) * EXP_CHUNK])
            return run

        def up(c):
            def run():
                a = state["a"]
                u = _bdot(state["x"], wub[:, c * EXP_CHUNK:(c + 1) * EXP_CHUNK])
                state["act"].append((a * jax.nn.sigmoid(a) * u).astype(jnp.bfloat16))
            return run

        def down(c):
            def run():
                if c == 0:
                    state["actf"] = jnp.concatenate(state["act"], axis=-1)
                yc = _bdot(state["actf"], wdb[:, c * EXP_CHUNK:(c + 1) * EXP_CHUNK])
                for jj in range(EXP_CHUNK // LANES):
                    j = c * (EXP_CHUNK // LANES) + jj
                    y_ref[pl.ds(j, EXP_TM, stride=ROW_TILE), :] = yc[:, jj * LANES:(jj + 1) * LANES]
            return run

        steps = []
        for c in range(D_EXPERT // EXP_CHUNK):
            steps += [(gate(c), 2), (up(c), 2)]
        return steps + [(down(c), 1) for c in range(D_MODEL // EXP_CHUNK)]

    def resident_copy(p):
        rows = PART_TOKENS * ROW_TILE
        return pltpu.make_async_copy(h_hbm.at[pl.ds(pl.multiple_of(p * rows, rows), rows), :], hres, rsem.at[0])

    @pl.when(i == 0)
    def _():
        resident_copy(part).start()
        for cp in weight_copies(te_ref[0]):
            cp.start()

        def fill_body(tt, c):
            ts = [tt * SRC_UNROLL + u for u in range(SRC_UNROLL)]
            words = [pos_ref[t] for t in ts]
            for t, w in zip(ts, words):
                src_ref[w & ((1 << POS_BITS) - 1)] = t
                src_ref[lax.shift_right_logical(w, POS_BITS)] = t
            return c
        lax.fori_loop(0, SEQ // SRC_UNROLL, fill_body, 0)

        def pad_tile(t, c):
            pad_tok = lax.shift_right_logical(te_ref[t], N_EXPERTS.bit_length() - 1) * PART_TOKENS

            def pad_row(k, c2):
                src_ref[t * EXP_TM + k] = pad_tok
                return c2
            return lax.fori_loop(tv_ref[t], EXP_TM, pad_row, c)
        lax.fori_loop(0, nact, pad_tile, 0)

    active = i < nact
    prev = jnp.maximum(i - 1, 0)
    new_segment = (i == 0) | (te_ref[i] != te_ref[prev])

    @pl.when(active & ((i == 0) | (part != lax.shift_right_logical(te_ref[prev], N_EXPERTS.bit_length() - 1))))
    def _():
        @pl.when(i > 0)
        def _():
            resident_copy(part).start()
        resident_copy(part).wait()
        for item in gather_items(i, xcur):
            item()

    @pl.when(active & new_segment)
    def _():
        for cp in weight_copies(te_ref[i]):
            cp.wait()
        wgb[...] = wgs[...].astype(jnp.bfloat16)
        wub[...] = wus[...].astype(jnp.bfloat16)
        wdb[...] = wds[...].astype(jnp.bfloat16)
        nxt = tnext_ref[i]

        @pl.when(nxt < nact)
        def _():
            for cp in weight_copies(te_ref[jnp.minimum(nxt, last)]):
                cp.start()

    @pl.when(active)
    def _():
        chunks = compute_chunks(xcur)
        items = gather_items(jnp.minimum(i + 1, nact - 1), 1 - xcur)
        total_cost = sum(cost for _, cost in chunks)
        done = 0
        for chunk, cost in chunks:
            upto = -(-len(items) * (done + cost) // total_cost)
            for item in items[-(-len(items) * done // total_cost):upto]:
                item()
            done += cost
            chunk()

    @pl.when(jnp.logical_not(active))
    def _():
        y_ref[...] = jnp.zeros_like(y_ref)


def _experts_call(te, tv, nact, tnext, pos, h, mod, wg, wu, wd):
    grid_spec = pltpu.PrefetchScalarGridSpec(
        num_scalar_prefetch=5,
        grid=(EXP_TILES,),
        in_specs=[
            pl.BlockSpec(memory_space=pl.ANY),
            pl.BlockSpec(mod.shape, lambda i, *_: (0, 0)),
            pl.BlockSpec(memory_space=pl.ANY),
            pl.BlockSpec(memory_space=pl.ANY),
            pl.BlockSpec(memory_space=pl.ANY),
        ],
        out_specs=pl.BlockSpec((TILE_ROWS, LANES), lambda i, *_: (i, 0)),
        scratch_shapes=[
            pltpu.SMEM((EXP_TILES * EXP_TM,), jnp.int32),
            pltpu.VMEM((PART_TOKENS * ROW_TILE, LANES), jnp.float32),
            pltpu.VMEM((2, TILE_ROWS, LANES), jnp.float32),
            pltpu.VMEM((D_MODEL, D_EXPERT), jnp.float32),
            pltpu.VMEM((D_MODEL, D_EXPERT), jnp.float32),
            pltpu.VMEM((D_EXPERT, D_MODEL), jnp.float32),
            pltpu.VMEM((D_MODEL, D_EXPERT), jnp.bfloat16),
            pltpu.VMEM((D_MODEL, D_EXPERT), jnp.bfloat16),
            pltpu.VMEM((D_EXPERT, D_MODEL), jnp.bfloat16),
            pltpu.SemaphoreType.DMA((1,)),
            pltpu.SemaphoreType.DMA((3,)),
        ],
    )
    return pl.pallas_call(
        _experts_kernel,
        grid_spec=grid_spec,
        out_shape=jax.ShapeDtypeStruct((EXP_TILES * TILE_ROWS, LANES), jnp.float32),
        compiler_params=pltpu.CompilerParams(
            dimension_semantics=("arbitrary",), vmem_limit_bytes=EXPERTS_VMEM_LIMIT),
        name="experts",
    )(te, tv, nact, tnext, pos, h, mod, wg, wu, wd)


def _combine_kernel(pos_ref, h_ref, cw_ref, mod_ref, g_ref, b_ref, ys_hbm, o_ref, ybuf, sem):
    i = pl.program_id(0)
    tm = CMB_TM
    slot = lax.rem(i, 2)

    def start_row(tile, k, s):
        word = pos_ref[tile * tm + k]
        for half, p in ((0, word & ((1 << POS_BITS) - 1)), (1, lax.shift_right_logical(word, POS_BITS))):
            pltpu.make_async_copy(ys_hbm.at[pl.ds(pl.multiple_of(p * ROW_TILE, ROW_TILE), ROW_TILE), :],
                                  ybuf.at[s, half, pl.ds(pl.multiple_of(k * ROW_TILE, ROW_TILE), ROW_TILE), :],
                                  sem.at[s]).start(priority=half)

    @pl.when(i == 0)
    def _():
        def body(kk, c):
            for u in range(SRC_UNROLL):
                start_row(0, kk * SRC_UNROLL + u, 0)
            return c
        lax.fori_loop(0, tm // SRC_UNROLL, body, 0)

    for half in range(2):
        pltpu.make_async_copy(ys_hbm.at[pl.ds(0, tm * ROW_TILE), :], ybuf.at[slot, half], sem.at[slot]).wait()

    g2 = mod_ref[0:1, 5 * D_MODEL:6 * D_MODEL]

    def chunk(c):
        rows = slice(c * CMB_CHUNK, (c + 1) * CMB_CHUNK)
        y0 = _load_row_tiles(ybuf, CMB_CHUNK, lead=(slot, 0), first=c * CMB_CHUNK)
        y1 = _load_row_tiles(ybuf, CMB_CHUNK, lead=(slot, 1), first=c * CMB_CHUNK)
        ffn = cw_ref[rows, CW_C0:CW_C0 + 1] * y0 + cw_ref[rows, CW_C1:CW_C1 + 1] * y1
        h = _load_row_tiles(h_ref, CMB_CHUNK, first=c * CMB_CHUNK)
        o_ref[rows, :] = _layer_norm(DEEPNORM_ALPHA * h + g2 * ffn, g_ref[...], b_ref[...])

    n_chunks = tm // CMB_CHUNK

    @pl.when(i + 1 < pl.num_programs(0))
    def _():
        per = tm // n_chunks
        for c in range(n_chunks):
            for k in range(c * per, (c + 1) * per):
                start_row(i + 1, k, 1 - slot)
            chunk(c)

    @pl.when(i + 1 == pl.num_programs(0))
    def _():
        for c in range(n_chunks):
            chunk(c)


def _combine_call(pos, h, ys, cw, mod, g, b):
    tm = CMB_TM
    grid_spec = pltpu.PrefetchScalarGridSpec(
        num_scalar_prefetch=1,
        grid=(SEQ // tm,),
        in_specs=[
            pl.BlockSpec((tm * ROW_TILE, LANES), lambda i, *_: (i, 0)),
            pl.BlockSpec((tm, LANES), lambda i, *_: (i, 0)),
            pl.BlockSpec(mod.shape, lambda i, *_: (0, 0)),
            pl.BlockSpec((1, D_MODEL), lambda i, *_: (0, 0)),
            pl.BlockSpec((1, D_MODEL), lambda i, *_: (0, 0)),
            pl.BlockSpec(memory_space=pl.ANY),
        ],
        out_specs=pl.BlockSpec((tm, D_MODEL), lambda i, *_: (i, 0)),
        scratch_shapes=[
            pltpu.VMEM((2, 2, tm * ROW_TILE, LANES), jnp.float32),
            pltpu.SemaphoreType.DMA((2,)),
        ],
    )
    return pl.pallas_call(
        _combine_kernel,
        grid_spec=grid_spec,
        out_shape=jax.ShapeDtypeStruct((SEQ, D_MODEL), jnp.float32),
        compiler_params=pltpu.CompilerParams(
            dimension_semantics=("arbitrary",), vmem_limit_bytes=VMEM_LIMIT),
        name="combine",
    )(pos, h, cw, mod, g, b, ys)


def kernel(x, c, ctx, c_ctx, ln_in_g, ln_in_b, w_mod, b_mod, w_in, rpb, w_pool_grp, pool_scale,
           w_attn_proj, w_pool_proj, w_out, ln1_g, ln1_b, w_router_group, b_router_group,
           w_router_expert, b_router_expert, w_expert_gate, w_expert_up, w_expert_down, ln2_g, ln2_b):
    assert x.shape == (1, SEQ, D_MODEL) and ctx.shape == (1, CTX_LEN, D_MODEL)
    assert w_mod.shape[0] == 1, "single-layer trunk"
    f32, bf16 = jnp.float32, jnp.bfloat16
    row = lambda v: v.reshape(1, -1).astype(f32)

    cond = jnp.concatenate([c, c_ctx[None], jnp.zeros((MOD_ROWS - 2, D_MODEL), f32)], axis=0)
    mod = _mod_call(cond, w_mod[0], row(b_mod[0]))

    lng, lnb = row(ln_in_g), row(ln_in_b)
    w_in_b = w_in[0].astype(bf16)
    u, h0 = _proj_call(x[0], mod, lng, lnb, w_in_b, mod_row=0, latent=True, tm=PROJ_TM)
    kvc, = _proj_call(ctx[0], mod, lng, lnb, w_in_b[:, ATT_W:3 * ATT_W], mod_row=1, latent=False, tm=CTX_LEN)

    h1 = _mix_call(h0, mod, u, kvc, _attn_bias_table(rpb[0]),
                   w_pool_grp[0].astype(bf16), row(pool_scale[0]),
                   w_attn_proj[0].astype(bf16), w_pool_proj[0].astype(bf16), w_out[0].astype(bf16),
                   row(ln1_g[0]), row(ln1_b[0]))

    n_logit = N_GROUPS + N_EXPERTS
    wrt = jnp.concatenate([w_router_group[0].T, w_router_expert[0].T,
                           jnp.zeros((LANES - n_logit, D_MODEL), f32)], axis=0)
    brt = jnp.concatenate([b_router_group[0], b_router_expert[0], jnp.zeros((LANES - n_logit,), f32)])
    brt = jnp.broadcast_to(brt[:, None], (LANES, LANES))
    pos, cw, plan = _route_call(h1, mod, wrt, brt)

    y = _experts_call(plan[PLAN_EXPERT, :EXP_TILES], plan[PLAN_VALID, :EXP_TILES], plan[PLAN_NACT, :1],
                      plan[PLAN_NEXT, :EXP_TILES], pos[POS_PACKED], h1, mod,
                      w_expert_gate[0], w_expert_up[0], w_expert_down[0])
    out = _combine_call(pos[POS_PACKED], h1, y, cw, mod, row(ln2_g[0]), row(ln2_b[0]))
    return out[None]
```

```python
import functools

import jax
import jax.numpy as jnp
from jax import lax
from jax.experimental import pallas as pl
from jax.experimental.pallas import tpu as pltpu

D_MODEL = 1024
SEQ = 16384
GRID_W = 64
ROWS = SEQ // GRID_W
CTX_LEN = 256
N_HEADS = 8
HEAD_DIM = 64
ATT_W = N_HEADS * HEAD_DIM
WIN_H = 8
WIN_W = 16
POOL_WINDOWS = (2, 4, 8, 16)
POOL_GROUPS = 4
POOL_DIM = 128
POOL_W = POOL_GROUPS * POOL_DIM
PROJ_W = 3 * ATT_W + POOL_W + 2 * D_MODEL
GATE_COL = 3 * ATT_W + POOL_W
N_GROUPS = 4
EXPERTS_PER_GROUP = 8
N_EXPERTS = N_GROUPS * EXPERTS_PER_GROUP
D_EXPERT = 512
N_MOD = 6
DEEPNORM_ALPHA = 2.0 ** 0.25
LN_EPS = 1e-5
NEG_INF = -1e30

LANES = 128
ROW_TILE = 8
MOD_ROWS = 8
PROJ_TM = 512
PROJ_SUB = 256
MIX_ROWS = 8
MIX_TQ = MIX_ROWS * GRID_W
KV_HALO = 4 * GRID_W
POOL_HALO = 16
ROUTE_TM = 512
EXP_TM = 256
SEQ_PARTS = 2
PART_TOKENS = SEQ // SEQ_PARTS
N_SEG = SEQ_PARTS * N_EXPERTS
EXP_TILES = 2 * SEQ // EXP_TM + N_SEG
CMB_TM = 512
CMB_CHUNK = 128
MERGE_ROWS = 128
ATTN_AHEAD = 2
HALF = D_MODEL // 2
VMEM_LIMIT = 56 * 1024 * 1024
EXPERTS_VMEM_LIMIT = 60 * 1024 * 1024


def _layer_norm(x, g, b):
    mu = jnp.mean(x, axis=-1, keepdims=True)
    xc = x - mu
    var = jnp.mean(xc * xc, axis=-1, keepdims=True)
    return xc * lax.rsqrt(var + LN_EPS) * g + b


def _bdot(a, b):
    return jnp.dot(a, b, preferred_element_type=jnp.float32)


def _split_bf16(a):
    hi = a.astype(jnp.bfloat16)
    lo = (a - hi.astype(jnp.float32)).astype(jnp.bfloat16)
    return hi, lo


def _dot3(a, b):
    a_hi, a_lo = _split_bf16(a)
    b_hi, b_lo = _split_bf16(b)
    return _bdot(a_hi, b_hi) + (_bdot(a_hi, b_lo) + _bdot(a_lo, b_hi))


def _load_row_tiles(ref, tokens, lead=(), first=0):
    parts = [ref[(*lead, pl.ds(first * ROW_TILE + j, tokens, stride=ROW_TILE), slice(None))]
             for j in range(ROW_TILE)]
    return jnp.concatenate(parts, axis=-1)


def _store_row_tiles(ref, value, lead=()):
    tokens = value.shape[0]
    for j in range(ROW_TILE):
        ref[(*lead, pl.ds(j, tokens, stride=ROW_TILE), slice(None))] = value[:, j * LANES:(j + 1) * LANES]


def _mod_kernel(cond_ref, w_ref, b_ref, o_ref):
    cond = cond_ref[...]
    act = cond * jax.nn.sigmoid(cond)
    o_ref[...] = _dot3(act, w_ref[...]) + b_ref[...]


def _mod_call(cond, w_mod, b_mod):
    tn = 1536
    n = N_MOD * D_MODEL
    return pl.pallas_call(
        _mod_kernel,
        grid=(n // tn,),
        in_specs=[
            pl.BlockSpec((MOD_ROWS, D_MODEL), lambda i: (0, 0)),
            pl.BlockSpec((D_MODEL, tn), lambda i: (0, i)),
            pl.BlockSpec((1, tn), lambda i: (0, i)),
        ],
        out_specs=pl.BlockSpec((MOD_ROWS, tn), lambda i: (0, i)),
        out_shape=jax.ShapeDtypeStruct((MOD_ROWS, n), jnp.float32),
        compiler_params=pltpu.CompilerParams(
            dimension_semantics=("arbitrary",), vmem_limit_bytes=VMEM_LIMIT),
        name="mod",
    )(cond, w_mod, b_mod)


def _proj_kernel(x_ref, mod_ref, g_ref, b_ref, w_ref, o_ref, *h_out, mod_row, latent):
    shift = mod_ref[mod_row:mod_row + 1, 0:D_MODEL]
    scale = mod_ref[mod_row:mod_row + 1, D_MODEL:2 * D_MODEL]
    tm, n = o_ref.shape
    sub = min(tm, PROJ_SUB)

    def prep(r):
        rows = slice(r * sub, (r + 1) * sub)
        h = _layer_norm(x_ref[rows, :], g_ref[...], b_ref[...])
        if latent:
            h_out[0][rows, :] = h
        return (h * (1.0 + scale) + shift).astype(jnp.bfloat16)

    def finish(r, c, res):
        if latent and c == 0:
            lane = lax.broadcasted_iota(jnp.int32, (1, D_MODEL), 1)
            res = res * jnp.where(lane < ATT_W, HEAD_DIM ** -0.5, 1.0)
        if latent and c * D_MODEL >= GATE_COL:
            res = jax.nn.sigmoid(res)
        o_ref[r * sub:(r + 1) * sub, c * D_MODEL:(c + 1) * D_MODEL] = res.astype(jnp.bfloat16)

    hm = {0: prep(0)}
    waiting = None
    for r in range(tm // sub):
        for c in range(n // D_MODEL):
            res = _bdot(hm[r], w_ref[:, c * D_MODEL:(c + 1) * D_MODEL])
            if c == 0 and (r + 1) * sub < tm:
                hm[r + 1] = prep(r + 1)
            if waiting is not None:
                finish(*waiting)
            waiting = (r, c, res)
    finish(*waiting)


def _proj_call(x, mod, g, b, w, *, mod_row, latent, tm):
    rows, n = x.shape[0], w.shape[1]
    out_specs = [pl.BlockSpec((tm, n), lambda i: (i, 0))]
    out_shape = [jax.ShapeDtypeStruct((rows, n), jnp.bfloat16)]
    if latent:
        out_specs.append(pl.BlockSpec((tm, D_MODEL), lambda i: (i, 0)))
        out_shape.append(jax.ShapeDtypeStruct((rows, D_MODEL), jnp.float32))
    return pl.pallas_call(
        functools.partial(_proj_kernel, mod_row=mod_row, latent=latent),
        grid=(rows // tm,),
        in_specs=[
            pl.BlockSpec((tm, D_MODEL), lambda i: (i, 0)),
            pl.BlockSpec(mod.shape, lambda i: (0, 0)),
            pl.BlockSpec((1, D_MODEL), lambda i: (0, 0)),
            pl.BlockSpec((1, D_MODEL), lambda i: (0, 0)),
            pl.BlockSpec((D_MODEL, n), lambda i: (0, 0), pipeline_mode=pl.Buffered(1)),
        ],
        out_specs=out_specs,
        out_shape=out_shape,
        compiler_params=pltpu.CompilerParams(
            dimension_semantics=("arbitrary",), vmem_limit_bytes=VMEM_LIMIT),
        name="proj",
    )(x, mod, g, b, w)


def _attn_bias_table(rpb):
    col = jnp.arange(GRID_W, dtype=jnp.int32)
    col_start = jnp.clip(col - WIN_W // 2, 0, GRID_W - WIN_W)
    col_mask = (col[None, :] >= col_start[:, None]) & (col[None, :] < col_start[:, None] + WIN_W)
    col_off = jnp.clip(col[None, :] - col[:, None], 1 - WIN_W, WIN_W - 1) + (WIN_W - 1)
    onehot = (col_off[None] == jnp.arange(2 * WIN_W - 1, dtype=jnp.int32)[:, None, None]).astype(jnp.float32)
    tab = jnp.einsum("hrc,cqk->hqrk", rpb.astype(jnp.float32), onehot, precision=lax.Precision.HIGHEST)
    tab = jnp.where(col_mask[None, :, None, :], tab, NEG_INF)
    tab = jnp.stack([tab[:, :, WIN_H - 1 - v:2 * WIN_H - 1 - v] for v in range(WIN_H)], axis=0)
    return tab.reshape(WIN_H, N_HEADS // 2, 2 * GRID_W, WIN_H * GRID_W)


def _mix_kernel(h_ref, mod_ref,
                q_ref, kp_ref, kc_ref, kn_ref, vp_ref, vc_ref, vn_ref,
                pp_ref, pc_ref, pn_ref, ga_ref, gb_ref,
                kvc_ref, bias_ref, wgrp_ref, pscale_ref, wap_ref, wpp_ref, wout_ref,
                ln1g_ref, ln1b_ref,
                o_ref,
                kbuf, vbuf, yabuf, pbuf, ypbuf, zbuf):
    b = pl.program_id(0)
    nb = pl.num_programs(0)

    kbuf[0:KV_HALO, :] = kp_ref[...]
    kbuf[KV_HALO:KV_HALO + MIX_TQ, :] = kc_ref[...]
    kbuf[KV_HALO + MIX_TQ:, :] = kn_ref[...]
    vbuf[0:KV_HALO, :] = vp_ref[...]
    vbuf[KV_HALO:KV_HALO + MIX_TQ, :] = vc_ref[...]
    vbuf[KV_HALO + MIX_TQ:, :] = vn_ref[...]

    lane = lax.broadcasted_iota(jnp.int32, (GRID_W, LANES), 1)
    first_head = lane < HEAD_DIM

    units = [(j, pair) for j in range(MIX_ROWS) for pair in range(N_HEADS // 2)]
    nt = (((1,), (1,)), ((), ()))

    def window(j):
        r = b * MIX_ROWS + j
        rs = jnp.clip(r - WIN_H // 2, 0, ROWS - WIN_H)
        return pl.multiple_of((rs - b * MIX_ROWS + WIN_H // 2) * GRID_W, GRID_W), r - rs

    def scores(j, pair):
        off, var = window(j)
        cols = slice(pair * LANES, (pair + 1) * LANES)
        q = q_ref[j * GRID_W:(j + 1) * GRID_W, cols]
        zero = jnp.zeros_like(q)
        q2 = jnp.concatenate([jnp.where(first_head, q, zero), jnp.where(first_head, zero, q)], axis=0)
        kw = kbuf[pl.ds(off, WIN_H * GRID_W), cols]
        s_loc = lax.dot_general(q2, kw, nt, preferred_element_type=jnp.float32) + bias_ref[var, pair]
        s_ctx = lax.dot_general(q2, kvc_ref[:, cols], nt, preferred_element_type=jnp.float32)
        return s_loc, s_ctx

    def values(j, pair, s_loc, s_ctx):
        off, _ = window(j)
        cols = slice(pair * LANES, (pair + 1) * LANES)
        vw = vbuf[pl.ds(off, WIN_H * GRID_W), cols]
        vctx = kvc_ref[:, ATT_W + pair * LANES:ATT_W + (pair + 1) * LANES]
        m = jnp.maximum(jnp.max(s_loc, axis=-1, keepdims=True), jnp.max(s_ctx, axis=-1, keepdims=True))
        p_loc = jnp.exp(s_loc - m).astype(jnp.bfloat16)
        p_ctx = jnp.exp(s_ctx - m).astype(jnp.bfloat16)
        ones = lambda rows: jnp.ones((rows, LANES), jnp.bfloat16)
        o2 = (_bdot(p_loc, jnp.concatenate([vw, ones(WIN_H * GRID_W)], axis=1))
              + _bdot(p_ctx, jnp.concatenate([vctx, ones(CTX_LEN)], axis=1)))
        o2 = o2[:, :LANES] / o2[:, LANES:]
        o_pair = jnp.where(first_head, o2[:GRID_W], o2[GRID_W:])
        yabuf[j * GRID_W:(j + 1) * GRID_W, cols] = o_pair.astype(jnp.bfloat16)

    pbuf[0:POOL_HALO, :] = jnp.where(b > 0, pp_ref[...].astype(jnp.float32), 0.0)
    pbuf[POOL_HALO:POOL_HALO + MIX_TQ, :] = pc_ref[...].astype(jnp.float32)
    pbuf[POOL_HALO + MIX_TQ:, :] = jnp.where(b < nb - 1, pn_ref[...].astype(jnp.float32), 0.0)
    edge = lax.broadcasted_iota(jnp.int32, (ROW_TILE, 1), 0)

    def pool_group(g):
        win = POOL_WINDOWS[g]
        lo, hi = win // 2, win - win // 2
        cols = slice(g * POOL_DIM, (g + 1) * POOL_DIM)
        acc = None
        for d in range(-lo, hi):
            term = pbuf[POOL_HALO + d:POOL_HALO + d + MIX_TQ, cols]
            acc = term if acc is None else acc + term
        assert max(lo, hi) <= ROW_TILE
        top = jnp.where(b == 0, 1.0 / (win - jnp.maximum(lo - edge, 0)).astype(jnp.float32), 1.0 / win)
        bot = jnp.where(b == nb - 1,
                        1.0 / (win - jnp.maximum(edge + hi - ROW_TILE, 0)).astype(jnp.float32), 1.0 / win)
        inv = jnp.concatenate([top, jnp.full((MIX_TQ - 2 * ROW_TILE, 1), 1.0 / win, jnp.float32), bot], axis=0)
        pooled = acc * inv - pbuf[POOL_HALO:POOL_HALO + MIX_TQ, cols]
        yp = _bdot(pooled.astype(jnp.bfloat16), wgrp_ref[g]) * pscale_ref[:, cols]
        ypbuf[:, cols] = yp.astype(jnp.bfloat16)

    def pooled_branch():
        zbuf[...] = gb_ref[...].astype(jnp.float32) * _bdot(ypbuf[...], wpp_ref[...])

    extra = {}
    for g in range(POOL_GROUPS):
        extra[(g + 1) * len(units) // (POOL_GROUPS + 2)] = functools.partial(pool_group, g)
    extra[(POOL_GROUPS + 1) * len(units) // (POOL_GROUPS + 2)] = pooled_branch
    pending = [scores(*u) for u in units[:ATTN_AHEAD]]
    for n, u in enumerate(units):
        if n + ATTN_AHEAD < len(units):
            pending.append(scores(*units[n + ATTN_AHEAD]))
        values(*u, *pending.pop(0))
        if n in extra:
            extra[n]()

    g1 = mod_ref[0:1, 2 * D_MODEL:3 * D_MODEL]
    n_chunks = MIX_TQ // MERGE_ROWS
    rows = lambda c: slice(c * MERGE_ROWS, (c + 1) * MERGE_ROWS)
    z, y = {}, {}

    def stage_a(c):
        ya = _bdot(yabuf[rows(c), :], wap_ref[...])
        z[c] = (ga_ref[rows(c), :].astype(jnp.float32) * ya + zbuf[rows(c), :]).astype(jnp.bfloat16)

    def stage_b(c):
        y[c] = _bdot(z.pop(c), wout_ref[...])

    def stage_c(c):
        hn = _layer_norm(DEEPNORM_ALPHA * h_ref[rows(c), :] + g1 * y.pop(c), ln1g_ref[...], ln1b_ref[...])
        for j in range(ROW_TILE):
            o_ref[pl.ds(c * MERGE_ROWS * ROW_TILE + j, MERGE_ROWS, stride=ROW_TILE), :] = hn[:, j * LANES:(j + 1) * LANES]

    for t in range(n_chunks + 2):
        if t < n_chunks:
            stage_a(t)
        if 0 <= t - 1 < n_chunks:
            stage_b(t - 1)
        if 0 <= t - 2 < n_chunks:
            stage_c(t - 2)


def _mix_call(h, mod, u, kvc, bias, wgrp, pscale, wap, wpp, wout, ln1g, ln1b):
    nb = SEQ // MIX_TQ
    halo_per_blk = MIX_TQ // KV_HALO
    n_halo = SEQ // KV_HALO
    ph_per_blk = MIX_TQ // POOL_HALO
    n_ph = SEQ // POOL_HALO

    def const(shape):
        return pl.BlockSpec(shape, lambda i: (0,) * len(shape), pipeline_mode=pl.Buffered(1))

    def prev_halo(c):
        return pl.BlockSpec((KV_HALO, ATT_W), lambda i: (jnp.maximum(i * halo_per_blk - 1, 0), c))

    def next_halo(c):
        return pl.BlockSpec((KV_HALO, ATT_W), lambda i: (jnp.minimum((i + 1) * halo_per_blk, n_halo - 1), c))

    def cur(c):
        return pl.BlockSpec((MIX_TQ, ATT_W), lambda i: (i, c))

    in_specs = [
        pl.BlockSpec((MIX_TQ, D_MODEL), lambda i: (i, 0)),
        const(mod.shape),
        cur(0),
        prev_halo(1), cur(1), next_halo(1),
        prev_halo(2), cur(2), next_halo(2),
        pl.BlockSpec((POOL_HALO, POOL_W), lambda i: (jnp.maximum(i * ph_per_blk - 1, 0), 3)),
        cur(3),
        pl.BlockSpec((POOL_HALO, POOL_W), lambda i: (jnp.minimum((i + 1) * ph_per_blk, n_ph - 1), 3)),
        pl.BlockSpec((MIX_TQ, D_MODEL), lambda i: (i, 2)),
        pl.BlockSpec((MIX_TQ, D_MODEL), lambda i: (i, 3)),
        const(kvc.shape), const(bias.shape), const(wgrp.shape), const(pscale.shape),
        const(wap.shape), const(wpp.shape), const(wout.shape),
        const((1, D_MODEL)), const((1, D_MODEL)),
    ]
    return pl.pallas_call(
        _mix_kernel,
        grid=(nb,),
        in_specs=in_specs,
        out_specs=pl.BlockSpec((MIX_TQ * ROW_TILE, LANES), lambda i: (i, 0)),
        out_shape=jax.ShapeDtypeStruct((SEQ * ROW_TILE, LANES), jnp.float32),
        scratch_shapes=[
            pltpu.VMEM((MIX_TQ + 2 * KV_HALO, ATT_W), jnp.bfloat16),
            pltpu.VMEM((MIX_TQ + 2 * KV_HALO, ATT_W), jnp.bfloat16),
            pltpu.VMEM((MIX_TQ, ATT_W), jnp.bfloat16),
            pltpu.VMEM((MIX_TQ + 2 * POOL_HALO, POOL_W), jnp.float32),
            pltpu.VMEM((MIX_TQ, POOL_W), jnp.bfloat16),
            pltpu.VMEM((MIX_TQ, D_MODEL), jnp.float32),
        ],
        compiler_params=pltpu.CompilerParams(
            dimension_semantics=("arbitrary",), vmem_limit_bytes=VMEM_LIMIT),
        name="mix",
    )(h, mod, u, u, u, u, u, u, u, u, u, u, u, u,
      kvc, bias, wgrp, pscale, wap, wpp, wout, ln1g, ln1b)


ID_E0, ID_E1, ID_R0, ID_R1 = 0, 1, 4, 5
POS_PACKED = 0
POS_BITS = 16
PLAN_EXPERT, PLAN_VALID, PLAN_NACT, PLAN_NEXT = 0, 1, 2, 3
PLAN_W = 2 * LANES
CW_C0, CW_C1 = 0, 1


def _route_kernel(h_ref, mod_ref, wrt_ref, brt_ref, pos_ref, cw_ref, plan_ref, carry_ref, ids_all):
    i = pl.program_id(0)
    tm = ROUTE_TM

    @pl.when(i == 0)
    def _():
        carry_ref[...] = jnp.zeros_like(carry_ref)

    shift = mod_ref[0:1, 3 * D_MODEL:4 * D_MODEL]
    scale = mod_ref[0:1, 4 * D_MODEL:5 * D_MODEL]
    hm = _load_row_tiles(h_ref, tm) * (1.0 + scale) + shift

    hm_hi, hm_lo = _split_bf16(hm)
    w_hi, w_lo = _split_bf16(wrt_ref[...])
    nt = (((1,), (1,)), ((), ()))
    dg = functools.partial(lax.dot_general, dimension_numbers=nt, preferred_element_type=jnp.float32)
    logits = dg(w_hi, hm_hi) + (dg(w_hi, hm_lo) + dg(w_lo, hm_hi)) + brt_ref[:, 0:1]

    sub = lax.broadcasted_iota(jnp.int32, (LANES, tm), 0)
    big = jnp.int32(1 << 20)
    is_grp = sub < N_GROUPS
    gl = jnp.where(is_grp, logits, -jnp.inf)
    gmax = jnp.max(gl, axis=0, keepdims=True)
    gidx = jnp.min(jnp.where(gl == gmax, sub, big), axis=0, keepdims=True)
    gsum = jnp.sum(jnp.where(is_grp, jnp.exp(logits - gmax), 0.0), axis=0, keepdims=True)
    p_group = 1.0 / gsum

    eid = sub - N_GROUPS
    sel = (eid >= 0) & (eid < N_EXPERTS) & (lax.shift_right_arithmetic(eid, 3) == gidx)
    el = jnp.where(sel, logits, -jnp.inf)
    l0 = jnp.max(el, axis=0, keepdims=True)
    i0 = jnp.min(jnp.where(el == l0, sub, big), axis=0, keepdims=True)
    el2 = jnp.where(sub == i0, -jnp.inf, el)
    l1 = jnp.max(el2, axis=0, keepdims=True)
    i1 = jnp.min(jnp.where(el2 == l1, sub, big), axis=0, keepdims=True)
    t = jnp.exp(l1 - l0)
    w0 = 1.0 / (1.0 + t)
    w1 = t / (1.0 + t)

    half_rows = jnp.where(i >= pl.num_programs(0) // SEQ_PARTS, N_EXPERTS, 0)
    i0 = i0 + half_rows
    i1 = i1 + half_rows
    onehot = jnp.where((sub == i0) | (sub == i1), 1.0, 0.0)
    rr = lax.broadcasted_iota(jnp.int32, (tm, tm), 0)
    cc = lax.broadcasted_iota(jnp.int32, (tm, tm), 1)
    earlier = jnp.where(rr < cc, 1.0, 0.0).astype(jnp.bfloat16)
    carry = carry_ref[:, 0:1]
    prefix = _bdot(onehot.astype(jnp.bfloat16), earlier) + carry
    r0 = jnp.sum(jnp.where(sub == i0, prefix, 0.0), axis=0, keepdims=True)
    r1 = jnp.sum(jnp.where(sub == i1, prefix, 0.0), axis=0, keepdims=True)
    total = jnp.broadcast_to(carry + jnp.sum(onehot, axis=1, keepdims=True), carry_ref.shape)
    carry_ref[...] = total

    sub8 = lax.broadcasted_iota(jnp.int32, (ROW_TILE, tm), 0)
    ids = jnp.zeros((ROW_TILE, tm), jnp.int32)
    for idx, val in ((ID_E0, i0 - N_GROUPS), (ID_E1, i1 - N_GROUPS),
                     (ID_R0, r0.astype(jnp.int32)), (ID_R1, r1.astype(jnp.int32))):
        ids = jnp.where(sub8 == idx, val, ids)
    ids_all[:, pl.ds(pl.multiple_of(i * tm, tm), tm)] = ids

    cwt = jnp.where(sub == CW_C0, p_group * w0, jnp.where(sub == CW_C1, p_group * w1, 0.0))
    cw_ref[...] = cwt.T

    @pl.when(i == pl.num_programs(0) - 1)
    def _():
        subq = lax.broadcasted_iota(jnp.int32, (LANES, LANES), 0)
        laneq = lax.broadcasted_iota(jnp.int32, (LANES, LANES), 1)
        cnt = total.astype(jnp.int32)
        tiles = lax.shift_right_logical(cnt + (EXP_TM - 1), EXP_TM.bit_length() - 1).astype(jnp.float32)
        incl = jnp.where(laneq <= subq, 1.0, 0.0).astype(jnp.bfloat16)
        tile_end = _bdot(incl, tiles.astype(jnp.bfloat16))
        tile_start = tile_end - tiles
        seg = (tile_start * EXP_TM).astype(jnp.int32)
        nact = jnp.max(tile_end, axis=0, keepdims=True)

        ids_full = ids_all[...]
        look = jnp.zeros_like(ids_full)
        for e in range(N_SEG):
            look = jnp.where(ids_full == e, seg[N_GROUPS + e, 0], look)
        pos01 = look + pltpu.roll(ids_full, ID_R0 - ID_E0, axis=0)
        assert EXP_TILES * EXP_TM <= 1 << POS_BITS
        pos_ref[...] = pos01 | (pltpu.roll(pos01, ROW_TILE - 1, axis=0) << POS_BITS)

        subp = lax.broadcasted_iota(jnp.int32, (LANES, PLAN_W), 0)
        tile = lax.broadcasted_iota(jnp.int32, (LANES, PLAN_W), 1).astype(jnp.float32)
        is_exp = (subp >= N_GROUPS) & (subp < N_GROUPS + N_SEG)
        end_col = tile_end[:, 0:1]
        nact_s = nact[:, 0:1]
        te = jnp.sum(jnp.where(is_exp & (tile >= end_col), 1.0, 0.0), axis=0, keepdims=True)
        te_last = jnp.sum(jnp.where(is_exp & (nact_s - 1.0 >= end_col), 1.0, 0.0), axis=0, keepdims=True)[:, 0:1]
        tile_row = tile[0:1, :]
        te = jnp.minimum(jnp.where(tile_row < nact_s, te, te_last), N_SEG - 1.0)
        mine = (subp - N_GROUPS).astype(jnp.float32) == te
        cnt_sel = jnp.sum(jnp.where(mine, total[:, 0:1], 0.0), axis=0, keepdims=True)
        start_sel = jnp.sum(jnp.where(mine, tile_start[:, 0:1], 0.0), axis=0, keepdims=True)
        end_sel = jnp.sum(jnp.where(mine, end_col, 0.0), axis=0, keepdims=True)
        valid = jnp.clip(cnt_sel - (tile_row - start_sel) * EXP_TM, 0.0, float(EXP_TM))
        valid = jnp.where(tile_row < nact_s, valid, 0.0)
        subr = lax.broadcasted_iota(jnp.int32, (ROW_TILE, PLAN_W), 0)
        plan = jnp.where(subr == PLAN_EXPERT, te, jnp.where(subr == PLAN_VALID, valid,
                         jnp.where(subr == PLAN_NACT, nact_s, jnp.where(subr == PLAN_NEXT, end_sel, 0.0))))
        plan_ref[...] = plan.astype(jnp.int32)


def _route_call(h, mod, wrt, brt):
    tm = ROUTE_TM
    return pl.pallas_call(
        _route_kernel,
        grid=(SEQ // tm,),
        in_specs=[
            pl.BlockSpec((tm * ROW_TILE, LANES), lambda i: (i, 0)),
            pl.BlockSpec(mod.shape, lambda i: (0, 0)),
            pl.BlockSpec((LANES, D_MODEL), lambda i: (0, 0)),
            pl.BlockSpec((LANES, LANES), lambda i: (0, 0)),
        ],
        out_specs=[
            pl.BlockSpec((ROW_TILE, SEQ), lambda i: (0, 0)),
            pl.BlockSpec((tm, LANES), lambda i: (i, 0)),
            pl.BlockSpec((ROW_TILE, PLAN_W), lambda i: (0, 0)),
        ],
        out_shape=[
            jax.ShapeDtypeStruct((ROW_TILE, SEQ), jnp.int32),
            jax.ShapeDtypeStruct((SEQ, LANES), jnp.float32),
            jax.ShapeDtypeStruct((ROW_TILE, PLAN_W), jnp.int32),
        ],
        scratch_shapes=[pltpu.VMEM((LANES, LANES), jnp.float32),
                        pltpu.VMEM((ROW_TILE, SEQ), jnp.int32)],
        compiler_params=pltpu.CompilerParams(
            dimension_semantics=("arbitrary",), vmem_limit_bytes=VMEM_LIMIT),
        name="route",
    )(h, mod, wrt, brt)


SRC_UNROLL = 8
EXP_CHUNK = 256
PREP_AFTER_DOWN = 1
TILE_ROWS = EXP_TM * ROW_TILE


def _experts_kernel(te_ref, tv_ref, nact_ref, tnext_ref, pos_ref,
                    h_hbm, mod_ref, wg_hbm, wu_hbm, wd_hbm,
                    y_ref,
                    src_ref, hres, xbuf, xmat, wgs, wus, wds, wgb, wub, wdb, rsem, wsem):
    i = pl.program_id(0)
    last = pl.num_programs(0) - 1
    nact = nact_ref[0]
    xcur = lax.rem(i, 2)
    part = lax.shift_right_logical(te_ref[i], N_EXPERTS.bit_length() - 1)

    def weight_copies(segment):
        e = segment & (N_EXPERTS - 1)
        return [pltpu.make_async_copy(w_hbm.at[e], stage, wsem.at[n])
                for n, (w_hbm, stage) in enumerate(((wg_hbm, wgs), (wu_hbm, wus), (wd_hbm, wds)))]

    def gather_row(tile, k, s):
        local = (src_ref[tile * EXP_TM + k] - part * PART_TOKENS) & (PART_TOKENS - 1)
        xbuf[s, k * ROW_TILE:(k + 1) * ROW_TILE, :] = hres[pl.ds(pl.multiple_of(local * ROW_TILE, ROW_TILE),
                                                             ROW_TILE), :]

    def gather_items(tile, s):
        return [functools.partial(gather_row, tile, k, s) for k in range(EXP_TM)]

    def prepare_input(xs):
        x = _load_row_tiles(xbuf, EXP_TM, lead=(xs,))
        shift = mod_ref[0:1, 3 * D_MODEL:4 * D_MODEL]
        scale = mod_ref[0:1, 4 * D_MODEL:5 * D_MODEL]
        xmat[xs] = (x * (1.0 + scale) + shift).astype(jnp.bfloat16)

    def compute_chunks(xs):
        state = {"act": []}

        def gate(c):
            def run():
                state["a"] = _bdot(xmat[xs], wgb[:, c * EXP_CHUNK:(c + 1) * EXP_CHUNK])
            return run

        def up(c):
            def run():
                a = state["a"]
                u = _bdot(xmat[xs], wub[:, c * EXP_CHUNK:(c + 1) * EXP_CHUNK])
                state["act"].append((a * jax.nn.sigmoid(a) * u).astype(jnp.bfloat16))
            return run

        def down(c):
            def run():
                if c == 0:
                    state["actf"] = jnp.concatenate(state["act"], axis=-1)
                yc = _bdot(state["actf"], wdb[:, c * EXP_CHUNK:(c + 1) * EXP_CHUNK])
                for jj in range(EXP_CHUNK // LANES):
                    j = c * (EXP_CHUNK // LANES) + jj
                    y_ref[pl.ds(j, EXP_TM, stride=ROW_TILE), :] = yc[:, jj * LANES:(jj + 1) * LANES]
            return run

        first = []
        for c in range(D_EXPERT // EXP_CHUNK):
            first += [gate(c), up(c)]
        return first, [down(c) for c in range(D_MODEL // EXP_CHUNK)]

    def resident_copy(p):
        rows = PART_TOKENS * ROW_TILE
        return pltpu.make_async_copy(h_hbm.at[pl.ds(pl.multiple_of(p * rows, rows), rows), :], hres, rsem.at[0])

    @pl.when(i == 0)
    def _():
        resident_copy(part).start()
        for cp in weight_copies(te_ref[0]):
            cp.start()

        def fill_body(tt, c):
            ts = [tt * SRC_UNROLL + u for u in range(SRC_UNROLL)]
            words = [pos_ref[t] for t in ts]
            for t, w in zip(ts, words):
                src_ref[w & ((1 << POS_BITS) - 1)] = t
                src_ref[lax.shift_right_logical(w, POS_BITS)] = t
            return c
        lax.fori_loop(0, SEQ // SRC_UNROLL, fill_body, 0)

        def pad_tile(t, c):
            pad_tok = lax.shift_right_logical(te_ref[t], N_EXPERTS.bit_length() - 1) * PART_TOKENS

            def pad_row(k, c2):
                src_ref[t * EXP_TM + k] = pad_tok
                return c2
            return lax.fori_loop(tv_ref[t], EXP_TM, pad_row, c)
        lax.fori_loop(0, nact, pad_tile, 0)

    active = i < nact
    prev = jnp.maximum(i - 1, 0)
    new_segment = (i == 0) | (te_ref[i] != te_ref[prev])

    @pl.when(active & ((i == 0) | (part != lax.shift_right_logical(te_ref[prev], N_EXPERTS.bit_length() - 1))))
    def _():
        @pl.when(i > 0)
        def _():
            resident_copy(part).start()
        resident_copy(part).wait()
        for item in gather_items(i, xcur):
            item()
        prepare_input(xcur)

    @pl.when(active & new_segment)
    def _():
        for cp in weight_copies(te_ref[i]):
            cp.wait()
        wgb[...] = wgs[...].astype(jnp.bfloat16)
        wub[...] = wus[...].astype(jnp.bfloat16)
        wdb[...] = wds[...].astype(jnp.bfloat16)
        nxt = tnext_ref[i]

        @pl.when(nxt < nact)
        def _():
            for cp in weight_copies(te_ref[jnp.minimum(nxt, last)]):
                cp.start()

    @pl.when(active)
    def _():
        first, second = compute_chunks(xcur)
        items = gather_items(jnp.minimum(i + 1, nact - 1), 1 - xcur)
        per = -(-len(items) // len(first))
        for n, chunk in enumerate(first):
            for item in items[n * per:(n + 1) * per]:
                item()
            chunk()
        for n, chunk in enumerate(second):
            chunk()
            if n == PREP_AFTER_DOWN:
                prepare_input(1 - xcur)

    @pl.when(jnp.logical_not(active))
    def _():
        y_ref[...] = jnp.zeros_like(y_ref)


def _experts_call(te, tv, nact, tnext, pos, h, mod, wg, wu, wd):
    grid_spec = pltpu.PrefetchScalarGridSpec(
        num_scalar_prefetch=5,
        grid=(EXP_TILES,),
        in_specs=[
            pl.BlockSpec(memory_space=pl.ANY),
            pl.BlockSpec(mod.shape, lambda i, *_: (0, 0)),
            pl.BlockSpec(memory_space=pl.ANY),
            pl.BlockSpec(memory_space=pl.ANY),
            pl.BlockSpec(memory_space=pl.ANY),
        ],
        out_specs=pl.BlockSpec((TILE_ROWS, LANES), lambda i, *_: (i, 0)),
        scratch_shapes=[
            pltpu.SMEM((EXP_TILES * EXP_TM,), jnp.int32),
            pltpu.VMEM((PART_TOKENS * ROW_TILE, LANES), jnp.float32),
            pltpu.VMEM((2, TILE_ROWS, LANES), jnp.float32),
            pltpu.VMEM((2, EXP_TM, D_MODEL), jnp.bfloat16),
            pltpu.VMEM((D_MODEL, D_EXPERT), jnp.float32),
            pltpu.VMEM((D_MODEL, D_EXPERT), jnp.float32),
            pltpu.VMEM((D_EXPERT, D_MODEL), jnp.float32),
            pltpu.VMEM((D_MODEL, D_EXPERT), jnp.bfloat16),
            pltpu.VMEM((D_MODEL, D_EXPERT), jnp.bfloat16),
            pltpu.VMEM((D_EXPERT, D_MODEL), jnp.bfloat16),
            pltpu.SemaphoreType.DMA((1,)),
            pltpu.SemaphoreType.DMA((3,)),
        ],
    )
    return pl.pallas_call(
        _experts_kernel,
        grid_spec=grid_spec,
        out_shape=jax.ShapeDtypeStruct((EXP_TILES * TILE_ROWS, LANES), jnp.float32),
        compiler_params=pltpu.CompilerParams(
            dimension_semantics=("arbitrary",), vmem_limit_bytes=EXPERTS_VMEM_LIMIT),
        name="experts",
    )(te, tv, nact, tnext, pos, h, mod, wg, wu, wd)


def _combine_kernel(pos_ref, h_ref, cw_ref, mod_ref, g_ref, b_ref, ys_hbm, o_ref, ybuf, sem):
    i = pl.program_id(0)
    tm = CMB_TM
    slot = lax.rem(i, 2)

    def start_row(tile, k, s):
        word = pos_ref[tile * tm + k]
        for half, p in ((0, word & ((1 << POS_BITS) - 1)), (1, lax.shift_right_logical(word, POS_BITS))):
            pltpu.make_async_copy(ys_hbm.at[pl.ds(pl.multiple_of(p * ROW_TILE, ROW_TILE), ROW_TILE), :],
                                  ybuf.at[s, half, pl.ds(pl.multiple_of(k * ROW_TILE, ROW_TILE), ROW_TILE), :],
                                  sem.at[s]).start(priority=half)

    @pl.when(i == 0)
    def _():
        def body(kk, c):
            for u in range(SRC_UNROLL):
                start_row(0, kk * SRC_UNROLL + u, 0)
            return c
        lax.fori_loop(0, tm // SRC_UNROLL, body, 0)

    for half in range(2):
        pltpu.make_async_copy(ys_hbm.at[pl.ds(0, tm * ROW_TILE), :], ybuf.at[slot, half], sem.at[slot]).wait()

    g2 = mod_ref[0:1, 5 * D_MODEL:6 * D_MODEL]

    def chunk(c):
        rows = slice(c * CMB_CHUNK, (c + 1) * CMB_CHUNK)
        y0 = _load_row_tiles(ybuf, CMB_CHUNK, lead=(slot, 0), first=c * CMB_CHUNK)
        y1 = _load_row_tiles(ybuf, CMB_CHUNK, lead=(slot, 1), first=c * CMB_CHUNK)
        ffn = cw_ref[rows, CW_C0:CW_C0 + 1] * y0 + cw_ref[rows, CW_C1:CW_C1 + 1] * y1
        h = _load_row_tiles(h_ref, CMB_CHUNK, first=c * CMB_CHUNK)
        o_ref[rows, :] = _layer_norm(DEEPNORM_ALPHA * h + g2 * ffn, g_ref[...], b_ref[...])

    n_chunks = tm // CMB_CHUNK

    @pl.when(i + 1 < pl.num_programs(0))
    def _():
        per = tm // n_chunks
        for c in range(n_chunks):
            for k in range(c * per, (c + 1) * per):
                start_row(i + 1, k, 1 - slot)
            chunk(c)

    @pl.when(i + 1 == pl.num_programs(0))
    def _():
        for c in range(n_chunks):
            chunk(c)


def _combine_call(pos, h, ys, cw, mod, g, b):
    tm = CMB_TM
    grid_spec = pltpu.PrefetchScalarGridSpec(
        num_scalar_prefetch=1,
        grid=(SEQ // tm,),
        in_specs=[
            pl.BlockSpec((tm * ROW_TILE, LANES), lambda i, *_: (i, 0)),
            pl.BlockSpec((tm, LANES), lambda i, *_: (i, 0)),
            pl.BlockSpec(mod.shape, lambda i, *_: (0, 0)),
            pl.BlockSpec((1, D_MODEL), lambda i, *_: (0, 0)),
            pl.BlockSpec((1, D_MODEL), lambda i, *_: (0, 0)),
            pl.BlockSpec(memory_space=pl.ANY),
        ],
        out_specs=pl.BlockSpec((tm, D_MODEL), lambda i, *_: (i, 0)),
        scratch_shapes=[
            pltpu.VMEM((2, 2, tm * ROW_TILE, LANES), jnp.float32),
            pltpu.SemaphoreType.DMA((2,)),
        ],
    )
    return pl.pallas_call(
        _combine_kernel,
        grid_spec=grid_spec,
        out_shape=jax.ShapeDtypeStruct((SEQ, D_MODEL), jnp.float32),
        compiler_params=pltpu.CompilerParams(
            dimension_semantics=("arbitrary",), vmem_limit_bytes=VMEM_LIMIT),
        name="combine",
    )(pos, h, cw, mod, g, b, ys)


def kernel(x, c, ctx, c_ctx, ln_in_g, ln_in_b, w_mod, b_mod, w_in, rpb, w_pool_grp, pool_scale,
           w_attn_proj, w_pool_proj, w_out, ln1_g, ln1_b, w_router_group, b_router_group,
           w_router_expert, b_router_expert, w_expert_gate, w_expert_up, w_expert_down, ln2_g, ln2_b):
    assert x.shape == (1, SEQ, D_MODEL) and ctx.shape == (1, CTX_LEN, D_MODEL)
    assert w_mod.shape[0] == 1, "single-layer trunk"
    f32, bf16 = jnp.float32, jnp.bfloat16
    row = lambda v: v.reshape(1, -1).astype(f32)

    cond = jnp.concatenate([c, c_ctx[None], jnp.zeros((MOD_ROWS - 2, D_MODEL), f32)], axis=0)
    mod = _mod_call(cond, w_mod[0], row(b_mod[0]))

    lng, lnb = row(ln_in_g), row(ln_in_b)
    w_in_b = w_in[0].astype(bf16)
    u, h0 = _proj_call(x[0], mod, lng, lnb, w_in_b, mod_row=0, latent=True, tm=PROJ_TM)
    kvc, = _proj_call(ctx[0], mod, lng, lnb, w_in_b[:, ATT_W:3 * ATT_W], mod_row=1, latent=False, tm=CTX_LEN)

    h1 = _mix_call(h0, mod, u, kvc, _attn_bias_table(rpb[0]),
                   w_pool_grp[0].astype(bf16), row(pool_scale[0]),
                   w_attn_proj[0].astype(bf16), w_pool_proj[0].astype(bf16), w_out[0].astype(bf16),
                   row(ln1_g[0]), row(ln1_b[0]))

    n_logit = N_GROUPS + N_EXPERTS
    wrt = jnp.concatenate([w_router_group[0].T, w_router_expert[0].T,
                           jnp.zeros((LANES - n_logit, D_MODEL), f32)], axis=0)
    brt = jnp.concatenate([b_router_group[0], b_router_expert[0], jnp.zeros((LANES - n_logit,), f32)])
    brt = jnp.broadcast_to(brt[:, None], (LANES, LANES))
    pos, cw, plan = _route_call(h1, mod, wrt, brt)

    y = _experts_call(plan[PLAN_EXPERT, :EXP_TILES], plan[PLAN_VALID, :EXP_TILES], plan[PLAN_NACT, :1],
                      plan[PLAN_NEXT, :EXP_TILES], pos[POS_PACKED], h1, mod,
                      w_expert_gate[0], w_expert_up[0], w_expert_down[0])
    out = _combine_call(pos[POS_PACKED], h1, y, cw, mod, row(ln2_g[0]), row(ln2_b[0]))
    return out[None]
```

```python
import functools

import jax
import jax.numpy as jnp
from jax import lax
from jax.experimental import pallas as pl
from jax.experimental.pallas import tpu as pltpu

D_MODEL = 1024
SEQ = 16384
GRID_W = 64
ROWS = SEQ // GRID_W
CTX_LEN = 256
N_HEADS = 8
HEAD_DIM = 64
ATT_W = N_HEADS * HEAD_DIM
WIN_H = 8
WIN_W = 16
POOL_WINDOWS = (2, 4, 8, 16)
POOL_GROUPS = 4
POOL_DIM = 128
POOL_W = POOL_GROUPS * POOL_DIM
PROJ_W = 3 * ATT_W + POOL_W + 2 * D_MODEL
GATE_COL = 3 * ATT_W + POOL_W
N_GROUPS = 4
EXPERTS_PER_GROUP = 8
N_EXPERTS = N_GROUPS * EXPERTS_PER_GROUP
D_EXPERT = 512
N_MOD = 6
DEEPNORM_ALPHA = 2.0 ** 0.25
LN_EPS = 1e-5
NEG_INF = -1e30

LANES = 128
ROW_TILE = 8
MOD_ROWS = 8
PROJ_TM = 512
PROJ_SUB = 256
MIX_ROWS = 8
MIX_TQ = MIX_ROWS * GRID_W
KV_HALO = 4 * GRID_W
POOL_HALO = 16
ROUTE_TM = 512
EXP_TM = 256
SEQ_PARTS = 2
PART_TOKENS = SEQ // SEQ_PARTS
N_SEG = SEQ_PARTS * N_EXPERTS
EXP_TILES = 2 * SEQ // EXP_TM + N_SEG
CMB_TM = 512
CMB_CHUNK = 128
MERGE_ROWS = 128
ATTN_AHEAD = 2
HALF = D_MODEL // 2
VMEM_LIMIT = 56 * 1024 * 1024
EXPERTS_VMEM_LIMIT = 60 * 1024 * 1024


def _layer_norm(x, g, b):
    mu = jnp.mean(x, axis=-1, keepdims=True)
    xc = x - mu
    var = jnp.mean(xc * xc, axis=-1, keepdims=True)
    return xc * lax.rsqrt(var + LN_EPS) * g + b


def _bdot(a, b):
    return jnp.dot(a, b, preferred_element_type=jnp.float32)


def _split_bf16(a):
    hi = a.astype(jnp.bfloat16)
    lo = (a - hi.astype(jnp.float32)).astype(jnp.bfloat16)
    return hi, lo


def _dot3(a, b):
    a_hi, a_lo = _split_bf16(a)
    b_hi, b_lo = _split_bf16(b)
    return _bdot(a_hi, b_hi) + (_bdot(a_hi, b_lo) + _bdot(a_lo, b_hi))


def _load_row_tiles(ref, tokens, lead=(), first=0):
    parts = [ref[(*lead, pl.ds(first * ROW_TILE + j, tokens, stride=ROW_TILE), slice(None))]
             for j in range(ROW_TILE)]
    return jnp.concatenate(parts, axis=-1)


def _store_row_tiles(ref, value, lead=()):
    tokens = value.shape[0]
    for j in range(ROW_TILE):
        ref[(*lead, pl.ds(j, tokens, stride=ROW_TILE), slice(None))] = value[:, j * LANES:(j + 1) * LANES]


def _mod_kernel(cond_ref, w_ref, b_ref, o_ref):
    cond = cond_ref[...]
    act = cond * jax.nn.sigmoid(cond)
    o_ref[...] = _dot3(act, w_ref[...]) + b_ref[...]


def _mod_call(cond, w_mod, b_mod):
    tn = 1536
    n = N_MOD * D_MODEL
    return pl.pallas_call(
        _mod_kernel,
        grid=(n // tn,),
        in_specs=[
            pl.BlockSpec((MOD_ROWS, D_MODEL), lambda i: (0, 0)),
            pl.BlockSpec((D_MODEL, tn), lambda i: (0, i)),
            pl.BlockSpec((1, tn), lambda i: (0, i)),
        ],
        out_specs=pl.BlockSpec((MOD_ROWS, tn), lambda i: (0, i)),
        out_shape=jax.ShapeDtypeStruct((MOD_ROWS, n), jnp.float32),
        compiler_params=pltpu.CompilerParams(
            dimension_semantics=("arbitrary",), vmem_limit_bytes=VMEM_LIMIT),
        name="mod",
    )(cond, w_mod, b_mod)


def _proj_kernel(x_ref, mod_ref, g_ref, b_ref, w_ref, o_ref, *h_out, mod_row, latent):
    shift = mod_ref[mod_row:mod_row + 1, 0:D_MODEL]
    scale = mod_ref[mod_row:mod_row + 1, D_MODEL:2 * D_MODEL]
    tm, n = o_ref.shape
    sub = min(tm, PROJ_SUB)

    def prep(r):
        rows = slice(r * sub, (r + 1) * sub)
        h = _layer_norm(x_ref[rows, :], g_ref[...], b_ref[...])
        if latent:
            h_out[0][rows, :] = h
        return (h * (1.0 + scale) + shift).astype(jnp.bfloat16)

    def finish(r, c, res):
        if latent and c == 0:
            lane = lax.broadcasted_iota(jnp.int32, (1, D_MODEL), 1)
            res = res * jnp.where(lane < ATT_W, HEAD_DIM ** -0.5, 1.0)
        if latent and c * D_MODEL >= GATE_COL:
            res = jax.nn.sigmoid(res)
        o_ref[r * sub:(r + 1) * sub, c * D_MODEL:(c + 1) * D_MODEL] = res.astype(jnp.bfloat16)

    hm = {0: prep(0)}
    waiting = None
    for r in range(tm // sub):
        for c in range(n // D_MODEL):
            res = _bdot(hm[r], w_ref[:, c * D_MODEL:(c + 1) * D_MODEL])
            if c == 0 and (r + 1) * sub < tm:
                hm[r + 1] = prep(r + 1)
            if waiting is not None:
                finish(*waiting)
            waiting = (r, c, res)
    finish(*waiting)


def _proj_call(x, mod, g, b, w, *, mod_row, latent, tm):
    rows, n = x.shape[0], w.shape[1]
    out_specs = [pl.BlockSpec((tm, n), lambda i: (i, 0))]
    out_shape = [jax.ShapeDtypeStruct((rows, n), jnp.bfloat16)]
    if latent:
        out_specs.append(pl.BlockSpec((tm, D_MODEL), lambda i: (i, 0)))
        out_shape.append(jax.ShapeDtypeStruct((rows, D_MODEL), jnp.float32))
    return pl.pallas_call(
        functools.partial(_proj_kernel, mod_row=mod_row, latent=latent),
        grid=(rows // tm,),
        in_specs=[
            pl.BlockSpec((tm, D_MODEL), lambda i: (i, 0)),
            pl.BlockSpec(mod.shape, lambda i: (0, 0)),
            pl.BlockSpec((1, D_MODEL), lambda i: (0, 0)),
            pl.BlockSpec((1, D_MODEL), lambda i: (0, 0)),
            pl.BlockSpec((D_MODEL, n), lambda i: (0, 0), pipeline_mode=pl.Buffered(1)),
        ],
        out_specs=out_specs,
        out_shape=out_shape,
        compiler_params=pltpu.CompilerParams(
            dimension_semantics=("arbitrary",), vmem_limit_bytes=VMEM_LIMIT),
        name="proj",
    )(x, mod, g, b, w)


def _attn_bias_table(rpb):
    col = jnp.arange(GRID_W, dtype=jnp.int32)
    col_start = jnp.clip(col - WIN_W // 2, 0, GRID_W - WIN_W)
    col_mask = (col[None, :] >= col_start[:, None]) & (col[None, :] < col_start[:, None] + WIN_W)
    col_off = jnp.clip(col[None, :] - col[:, None], 1 - WIN_W, WIN_W - 1) + (WIN_W - 1)
    onehot = (col_off[None] == jnp.arange(2 * WIN_W - 1, dtype=jnp.int32)[:, None, None]).astype(jnp.float32)
    tab = jnp.einsum("hrc,cqk->hqrk", rpb.astype(jnp.float32), onehot, precision=lax.Precision.HIGHEST)
    tab = jnp.where(col_mask[None, :, None, :], tab, NEG_INF)
    tab = jnp.stack([tab[:, :, WIN_H - 1 - v:2 * WIN_H - 1 - v] for v in range(WIN_H)], axis=0)
    return tab.reshape(WIN_H, N_HEADS // 2, 2 * GRID_W, WIN_H * GRID_W)


def _mix_kernel(h_ref, mod_ref,
                q_ref, kp_ref, kc_ref, kn_ref, vp_ref, vc_ref, vn_ref,
                pp_ref, pc_ref, pn_ref, ga_ref, gb_ref,
                kvc_ref, bias_ref, wgrp_ref, pscale_ref, wap_ref, wpp_ref, wout_ref,
                ln1g_ref, ln1b_ref,
                o_ref,
                kbuf, vbuf, yabuf, pbuf, ypbuf, zbuf):
    b = pl.program_id(0)
    nb = pl.num_programs(0)

    kbuf[0:KV_HALO, :] = kp_ref[...]
    kbuf[KV_HALO:KV_HALO + MIX_TQ, :] = kc_ref[...]
    kbuf[KV_HALO + MIX_TQ:, :] = kn_ref[...]
    vbuf[0:KV_HALO, :] = vp_ref[...]
    vbuf[KV_HALO:KV_HALO + MIX_TQ, :] = vc_ref[...]
    vbuf[KV_HALO + MIX_TQ:, :] = vn_ref[...]

    lane = lax.broadcasted_iota(jnp.int32, (GRID_W, LANES), 1)
    first_head = lane < HEAD_DIM

    units = [(j, pair) for j in range(MIX_ROWS) for pair in range(N_HEADS // 2)]
    nt = (((1,), (1,)), ((), ()))

    def window(j):
        r = b * MIX_ROWS + j
        rs = jnp.clip(r - WIN_H // 2, 0, ROWS - WIN_H)
        return pl.multiple_of((rs - b * MIX_ROWS + WIN_H // 2) * GRID_W, GRID_W), r - rs

    def scores(j, pair):
        off, var = window(j)
        cols = slice(pair * LANES, (pair + 1) * LANES)
        q = q_ref[j * GRID_W:(j + 1) * GRID_W, cols]
        zero = jnp.zeros_like(q)
        q2 = jnp.concatenate([jnp.where(first_head, q, zero), jnp.where(first_head, zero, q)], axis=0)
        kw = kbuf[pl.ds(off, WIN_H * GRID_W), cols]
        s_loc = lax.dot_general(q2, kw, nt, preferred_element_type=jnp.float32) + bias_ref[var, pair]
        s_ctx = lax.dot_general(q2, kvc_ref[:, cols], nt, preferred_element_type=jnp.float32)
        return s_loc, s_ctx

    def values(j, pair, s_loc, s_ctx):
        off, _ = window(j)
        cols = slice(pair * LANES, (pair + 1) * LANES)
        vw = vbuf[pl.ds(off, WIN_H * GRID_W), cols]
        vctx = kvc_ref[:, ATT_W + pair * LANES:ATT_W + (pair + 1) * LANES]
        m = jnp.maximum(jnp.max(s_loc, axis=-1, keepdims=True), jnp.max(s_ctx, axis=-1, keepdims=True))
        p_loc = jnp.exp(s_loc - m).astype(jnp.bfloat16)
        p_ctx = jnp.exp(s_ctx - m).astype(jnp.bfloat16)
        ones = lambda rows: jnp.ones((rows, LANES), jnp.bfloat16)
        o2 = (_bdot(p_loc, jnp.concatenate([vw, ones(WIN_H * GRID_W)], axis=1))
              + _bdot(p_ctx, jnp.concatenate([vctx, ones(CTX_LEN)], axis=1)))
        o2 = o2[:, :LANES] / o2[:, LANES:]
        o_pair = jnp.where(first_head, o2[:GRID_W], o2[GRID_W:])
        yabuf[j * GRID_W:(j + 1) * GRID_W, cols] = o_pair.astype(jnp.bfloat16)

    pbuf[0:POOL_HALO, :] = jnp.where(b > 0, pp_ref[...].astype(jnp.float32), 0.0)
    pbuf[POOL_HALO:POOL_HALO + MIX_TQ, :] = pc_ref[...].astype(jnp.float32)
    pbuf[POOL_HALO + MIX_TQ:, :] = jnp.where(b < nb - 1, pn_ref[...].astype(jnp.float32), 0.0)
    edge = lax.broadcasted_iota(jnp.int32, (ROW_TILE, 1), 0)

    def pool_group(g):
        win = POOL_WINDOWS[g]
        lo, hi = win // 2, win - win // 2
        cols = slice(g * POOL_DIM, (g + 1) * POOL_DIM)
        acc = None
        for d in range(-lo, hi):
            term = pbuf[POOL_HALO + d:POOL_HALO + d + MIX_TQ, cols]
            acc = term if acc is None else acc + term
        assert max(lo, hi) <= ROW_TILE
        top = jnp.where(b == 0, 1.0 / (win - jnp.maximum(lo - edge, 0)).astype(jnp.float32), 1.0 / win)
        bot = jnp.where(b == nb - 1,
                        1.0 / (win - jnp.maximum(edge + hi - ROW_TILE, 0)).astype(jnp.float32), 1.0 / win)
        inv = jnp.concatenate([top, jnp.full((MIX_TQ - 2 * ROW_TILE, 1), 1.0 / win, jnp.float32), bot], axis=0)
        pooled = acc * inv - pbuf[POOL_HALO:POOL_HALO + MIX_TQ, cols]
        yp = _bdot(pooled.astype(jnp.bfloat16), wgrp_ref[g]) * pscale_ref[:, cols]
        ypbuf[:, cols] = yp.astype(jnp.bfloat16)

    def pooled_branch():
        zbuf[...] = gb_ref[...].astype(jnp.float32) * _bdot(ypbuf[...], wpp_ref[...])

    extra = {}
    for g in range(POOL_GROUPS):
        extra[(g + 1) * len(units) // (POOL_GROUPS + 2)] = functools.partial(pool_group, g)
    extra[(POOL_GROUPS + 1) * len(units) // (POOL_GROUPS + 2)] = pooled_branch
    pending = [scores(*u) for u in units[:ATTN_AHEAD]]
    for n, u in enumerate(units):
        if n + ATTN_AHEAD < len(units):
            pending.append(scores(*units[n + ATTN_AHEAD]))
        values(*u, *pending.pop(0))
        if n in extra:
            extra[n]()

    g1 = mod_ref[0:1, 2 * D_MODEL:3 * D_MODEL]
    n_chunks = MIX_TQ // MERGE_ROWS
    rows = lambda c: slice(c * MERGE_ROWS, (c + 1) * MERGE_ROWS)
    z, y = {}, {}

    def stage_a(c):
        ya = _bdot(yabuf[rows(c), :], wap_ref[...])
        z[c] = (ga_ref[rows(c), :].astype(jnp.float32) * ya + zbuf[rows(c), :]).astype(jnp.bfloat16)

    def stage_b(c):
        y[c] = _bdot(z.pop(c), wout_ref[...])

    def stage_c(c):
        hn = _layer_norm(DEEPNORM_ALPHA * h_ref[rows(c), :] + g1 * y.pop(c), ln1g_ref[...], ln1b_ref[...])
        for j in range(ROW_TILE):
            o_ref[pl.ds(c * MERGE_ROWS * ROW_TILE + j, MERGE_ROWS, stride=ROW_TILE), :] = hn[:, j * LANES:(j + 1) * LANES]

    for t in range(n_chunks + 2):
        if t < n_chunks:
            stage_a(t)
        if 0 <= t - 1 < n_chunks:
            stage_b(t - 1)
        if 0 <= t - 2 < n_chunks:
            stage_c(t - 2)


def _mix_call(h, mod, u, kvc, bias, wgrp, pscale, wap, wpp, wout, ln1g, ln1b):
    nb = SEQ // MIX_TQ
    halo_per_blk = MIX_TQ // KV_HALO
    n_halo = SEQ // KV_HALO
    ph_per_blk = MIX_TQ // POOL_HALO
    n_ph = SEQ // POOL_HALO

    def const(shape):
        return pl.BlockSpec(shape, lambda i: (0,) * len(shape), pipeline_mode=pl.Buffered(1))

    def prev_halo(c):
        return pl.BlockSpec((KV_HALO, ATT_W), lambda i: (jnp.maximum(i * halo_per_blk - 1, 0), c))

    def next_halo(c):
        return pl.BlockSpec((KV_HALO, ATT_W), lambda i: (jnp.minimum((i + 1) * halo_per_blk, n_halo - 1), c))

    def cur(c):
        return pl.BlockSpec((MIX_TQ, ATT_W), lambda i: (i, c))

    in_specs = [
        pl.BlockSpec((MIX_TQ, D_MODEL), lambda i: (i, 0)),
        const(mod.shape),
        cur(0),
        prev_halo(1), cur(1), next_halo(1),
        prev_halo(2), cur(2), next_halo(2),
        pl.BlockSpec((POOL_HALO, POOL_W), lambda i: (jnp.maximum(i * ph_per_blk - 1, 0), 3)),
        cur(3),
        pl.BlockSpec((POOL_HALO, POOL_W), lambda i: (jnp.minimum((i + 1) * ph_per_blk, n_ph - 1), 3)),
        pl.BlockSpec((MIX_TQ, D_MODEL), lambda i: (i, 2)),
        pl.BlockSpec((MIX_TQ, D_MODEL), lambda i: (i, 3)),
        const(kvc.shape), const(bias.shape), const(wgrp.shape), const(pscale.shape),
        const(wap.shape), const(wpp.shape), const(wout.shape),
        const((1, D_MODEL)), const((1, D_MODEL)),
    ]
    return pl.pallas_call(
        _mix_kernel,
        grid=(nb,),
        in_specs=in_specs,
        out_specs=pl.BlockSpec((MIX_TQ * ROW_TILE, LANES), lambda i: (i, 0)),
        out_shape=jax.ShapeDtypeStruct((SEQ * ROW_TILE, LANES), jnp.float32),
        scratch_shapes=[
            pltpu.VMEM((MIX_TQ + 2 * KV_HALO, ATT_W), jnp.bfloat16),
            pltpu.VMEM((MIX_TQ + 2 * KV_HALO, ATT_W), jnp.bfloat16),
            pltpu.VMEM((MIX_TQ, ATT_W), jnp.bfloat16),
            pltpu.VMEM((MIX_TQ + 2 * POOL_HALO, POOL_W), jnp.float32),
            pltpu.VMEM((MIX_TQ, POOL_W), jnp.bfloat16),
            pltpu.VMEM((MIX_TQ, D_MODEL), jnp.float32),
        ],
        compiler_params=pltpu.CompilerParams(
            dimension_semantics=("arbitrary",), vmem_limit_bytes=VMEM_LIMIT),
        name="mix",
    )(h, mod, u, u, u, u, u, u, u, u, u, u, u, u,
      kvc, bias, wgrp, pscale, wap, wpp, wout, ln1g, ln1b)


ID_E0, ID_E1, ID_R0, ID_R1 = 0, 1, 4, 5
POS_PACKED = 0
POS_BITS = 16
PLAN_EXPERT, PLAN_VALID, PLAN_NACT, PLAN_NEXT = 0, 1, 2, 3
PLAN_W = 2 * LANES
CW_C0, CW_C1 = 0, 1


def _route_kernel(h_ref, mod_ref, wrt_ref, brt_ref, pos_ref, cw_ref, plan_ref, carry_ref, ids_all):
    i = pl.program_id(0)
    tm = ROUTE_TM

    @pl.when(i == 0)
    def _():
        carry_ref[...] = jnp.zeros_like(carry_ref)

    shift = mod_ref[0:1, 3 * D_MODEL:4 * D_MODEL]
    scale = mod_ref[0:1, 4 * D_MODEL:5 * D_MODEL]
    hm = _load_row_tiles(h_ref, tm) * (1.0 + scale) + shift

    hm_hi, hm_lo = _split_bf16(hm)
    w_hi, w_lo = _split_bf16(wrt_ref[...])
    nt = (((1,), (1,)), ((), ()))
    dg = functools.partial(lax.dot_general, dimension_numbers=nt, preferred_element_type=jnp.float32)
    logits = dg(w_hi, hm_hi) + (dg(w_hi, hm_lo) + dg(w_lo, hm_hi)) + brt_ref[:, 0:1]

    sub = lax.broadcasted_iota(jnp.int32, (LANES, tm), 0)
    big = jnp.int32(1 << 20)
    is_grp = sub < N_GROUPS
    gl = jnp.where(is_grp, logits, -jnp.inf)
    gmax = jnp.max(gl, axis=0, keepdims=True)
    gidx = jnp.min(jnp.where(gl == gmax, sub, big), axis=0, keepdims=True)
    gsum = jnp.sum(jnp.where(is_grp, jnp.exp(logits - gmax), 0.0), axis=0, keepdims=True)
    p_group = 1.0 / gsum

    eid = sub - N_GROUPS
    sel = (eid >= 0) & (eid < N_EXPERTS) & (lax.shift_right_arithmetic(eid, 3) == gidx)
    el = jnp.where(sel, logits, -jnp.inf)
    l0 = jnp.max(el, axis=0, keepdims=True)
    i0 = jnp.min(jnp.where(el == l0, sub, big), axis=0, keepdims=True)
    el2 = jnp.where(sub == i0, -jnp.inf, el)
    l1 = jnp.max(el2, axis=0, keepdims=True)
    i1 = jnp.min(jnp.where(el2 == l1, sub, big), axis=0, keepdims=True)
    t = jnp.exp(l1 - l0)
    w0 = 1.0 / (1.0 + t)
    w1 = t / (1.0 + t)

    half_rows = jnp.where(i >= pl.num_programs(0) // SEQ_PARTS, N_EXPERTS, 0)
    i0 = i0 + half_rows
    i1 = i1 + half_rows
    onehot = jnp.where((sub == i0) | (sub == i1), 1.0, 0.0)
    rr = lax.broadcasted_iota(jnp.int32, (tm, tm), 0)
    cc = lax.broadcasted_iota(jnp.int32, (tm, tm), 1)
    earlier = jnp.where(rr < cc, 1.0, 0.0).astype(jnp.bfloat16)
    carry = carry_ref[:, 0:1]
    prefix = _bdot(onehot.astype(jnp.bfloat16), earlier) + carry
    r0 = jnp.sum(jnp.where(sub == i0, prefix, 0.0), axis=0, keepdims=True)
    r1 = jnp.sum(jnp.where(sub == i1, prefix, 0.0), axis=0, keepdims=True)
    total = jnp.broadcast_to(carry + jnp.sum(onehot, axis=1, keepdims=True), carry_ref.shape)
    carry_ref[...] = total

    sub8 = lax.broadcasted_iota(jnp.int32, (ROW_TILE, tm), 0)
    ids = jnp.zeros((ROW_TILE, tm), jnp.int32)
    for idx, val in ((ID_E0, i0 - N_GROUPS), (ID_E1, i1 - N_GROUPS),
                     (ID_R0, r0.astype(jnp.int32)), (ID_R1, r1.astype(jnp.int32))):
        ids = jnp.where(sub8 == idx, val, ids)
    ids_all[:, pl.ds(pl.multiple_of(i * tm, tm), tm)] = ids

    cwt = jnp.where(sub == CW_C0, p_group * w0, jnp.where(sub == CW_C1, p_group * w1, 0.0))
    cw_ref[...] = cwt.T

    @pl.when(i == pl.num_programs(0) - 1)
    def _():
        subq = lax.broadcasted_iota(jnp.int32, (LANES, LANES), 0)
        laneq = lax.broadcasted_iota(jnp.int32, (LANES, LANES), 1)
        cnt = total.astype(jnp.int32)
        tiles = lax.shift_right_logical(cnt + (EXP_TM - 1), EXP_TM.bit_length() - 1).astype(jnp.float32)
        incl = jnp.where(laneq <= subq, 1.0, 0.0).astype(jnp.bfloat16)
        tile_end = _bdot(incl, tiles.astype(jnp.bfloat16))
        tile_start = tile_end - tiles
        seg = (tile_start * EXP_TM).astype(jnp.int32)
        nact = jnp.max(tile_end, axis=0, keepdims=True)

        ids_full = ids_all[...]
        look = jnp.zeros_like(ids_full)
        for e in range(N_SEG):
            look = jnp.where(ids_full == e, seg[N_GROUPS + e, 0], look)
        pos01 = look + pltpu.roll(ids_full, ID_R0 - ID_E0, axis=0)
        assert EXP_TILES * EXP_TM <= 1 << POS_BITS
        pos_ref[...] = pos01 | (pltpu.roll(pos01, ROW_TILE - 1, axis=0) << POS_BITS)

        subp = lax.broadcasted_iota(jnp.int32, (LANES, PLAN_W), 0)
        tile = lax.broadcasted_iota(jnp.int32, (LANES, PLAN_W), 1).astype(jnp.float32)
        is_exp = (subp >= N_GROUPS) & (subp < N_GROUPS + N_SEG)
        end_col = tile_end[:, 0:1]
        nact_s = nact[:, 0:1]
        te = jnp.sum(jnp.where(is_exp & (tile >= end_col), 1.0, 0.0), axis=0, keepdims=True)
        te_last = jnp.sum(jnp.where(is_exp & (nact_s - 1.0 >= end_col), 1.0, 0.0), axis=0, keepdims=True)[:, 0:1]
        tile_row = tile[0:1, :]
        te = jnp.minimum(jnp.where(tile_row < nact_s, te, te_last), N_SEG - 1.0)
        mine = (subp - N_GROUPS).astype(jnp.float32) == te
        cnt_sel = jnp.sum(jnp.where(mine, total[:, 0:1], 0.0), axis=0, keepdims=True)
        start_sel = jnp.sum(jnp.where(mine, tile_start[:, 0:1], 0.0), axis=0, keepdims=True)
        end_sel = jnp.sum(jnp.where(mine, end_col, 0.0), axis=0, keepdims=True)
        valid = jnp.clip(cnt_sel - (tile_row - start_sel) * EXP_TM, 0.0, float(EXP_TM))
        valid = jnp.where(tile_row < nact_s, valid, 0.0)
        subr = lax.broadcasted_iota(jnp.int32, (ROW_TILE, PLAN_W), 0)
        plan = jnp.where(subr == PLAN_EXPERT, te, jnp.where(subr == PLAN_VALID, valid,
                         jnp.where(subr == PLAN_NACT, nact_s, jnp.where(subr == PLAN_NEXT, end_sel, 0.0))))
        plan_ref[...] = plan.astype(jnp.int32)


def _route_call(h, mod, wrt, brt):
    tm = ROUTE_TM
    return pl.pallas_call(
        _route_kernel,
        grid=(SEQ // tm,),
        in_specs=[
            pl.BlockSpec((tm * ROW_TILE, LANES), lambda i: (i, 0)),
            pl.BlockSpec(mod.shape, lambda i: (0, 0)),
            pl.BlockSpec((LANES, D_MODEL), lambda i: (0, 0)),
            pl.BlockSpec((LANES, LANES), lambda i: (0, 0)),
        ],
        out_specs=[
            pl.BlockSpec((ROW_TILE, SEQ), lambda i: (0, 0)),
            pl.BlockSpec((tm, LANES), lambda i: (i, 0)),
            pl.BlockSpec((ROW_TILE, PLAN_W), lambda i: (0, 0)),
        ],
        out_shape=[
            jax.ShapeDtypeStruct((ROW_TILE, SEQ), jnp.int32),
            jax.ShapeDtypeStruct((SEQ, LANES), jnp.float32),
            jax.ShapeDtypeStruct((ROW_TILE, PLAN_W), jnp.int32),
        ],
        scratch_shapes=[pltpu.VMEM((LANES, LANES), jnp.float32),
                        pltpu.VMEM((ROW_TILE, SEQ), jnp.int32)],
        compiler_params=pltpu.CompilerParams(
            dimension_semantics=("arbitrary",), vmem_limit_bytes=VMEM_LIMIT),
        name="route",
    )(h, mod, wrt, brt)


SRC_UNROLL = 8
EXP_CHUNK = 256
PREP_AFTER_DOWN = 1
TILE_ROWS = EXP_TM * ROW_TILE


def _experts_kernel(te_ref, tv_ref, nact_ref, tnext_ref, pos_ref,
                    h_hbm, mod_ref, wg_hbm, wu_hbm, wd_hbm,
                    y_ref,
                    src_ref, stage_ref, hres, xbuf, xmat, wgs, wus, wds, wgb, wub, wdb, rsem, wsem):
    i = pl.program_id(0)
    last = pl.num_programs(0) - 1
    nact = nact_ref[0]
    xcur = lax.rem(i, 2)
    part = lax.shift_right_logical(te_ref[i], N_EXPERTS.bit_length() - 1)

    def weight_copies(segment, st):
        e = segment & (N_EXPERTS - 1)
        return [pltpu.make_async_copy(w_hbm.at[e], stage.at[st], wsem.at[st, n])
                for n, (w_hbm, stage) in enumerate(((wg_hbm, wgs), (wu_hbm, wus), (wd_hbm, wds)))]

    def tile_after(t):
        return tnext_ref[jnp.minimum(t, last)]

    def gather_row(tile, k, s):
        local = (src_ref[tile * EXP_TM + k] - part * PART_TOKENS) & (PART_TOKENS - 1)
        xbuf[s, k * ROW_TILE:(k + 1) * ROW_TILE, :] = hres[pl.ds(pl.multiple_of(local * ROW_TILE, ROW_TILE),
                                                             ROW_TILE), :]

    def gather_items(tile, s):
        return [functools.partial(gather_row, tile, k, s) for k in range(EXP_TM)]

    def prepare_input(xs):
        x = _load_row_tiles(xbuf, EXP_TM, lead=(xs,))
        shift = mod_ref[0:1, 3 * D_MODEL:4 * D_MODEL]
        scale = mod_ref[0:1, 4 * D_MODEL:5 * D_MODEL]
        xmat[xs] = (x * (1.0 + scale) + shift).astype(jnp.bfloat16)

    def compute_chunks(xs):
        state = {"act": []}

        def gate(c):
            def run():
                state["a"] = _bdot(xmat[xs], wgb[:, c * EXP_CHUNK:(c + 1) * EXP_CHUNK])
            return run

        def up(c):
            def run():
                a = state["a"]
                u = _bdot(xmat[xs], wub[:, c * EXP_CHUNK:(c + 1) * EXP_CHUNK])
                state["act"].append((a * jax.nn.sigmoid(a) * u).astype(jnp.bfloat16))
            return run

        def down(c):
            def run():
                if c == 0:
                    state["actf"] = jnp.concatenate(state["act"], axis=-1)
                yc = _bdot(state["actf"], wdb[:, c * EXP_CHUNK:(c + 1) * EXP_CHUNK])
                for jj in range(EXP_CHUNK // LANES):
                    j = c * (EXP_CHUNK // LANES) + jj
                    y_ref[pl.ds(j, EXP_TM, stride=ROW_TILE), :] = yc[:, jj * LANES:(jj + 1) * LANES]
            return run

        first = []
        for c in range(D_EXPERT // EXP_CHUNK):
            first += [gate(c), up(c)]
        return first, [down(c) for c in range(D_MODEL // EXP_CHUNK)]

    def resident_copy(p):
        rows = PART_TOKENS * ROW_TILE
        return pltpu.make_async_copy(h_hbm.at[pl.ds(pl.multiple_of(p * rows, rows), rows), :], hres, rsem.at[0])

    @pl.when(i == 0)
    def _():
        resident_copy(part).start()
        stage_ref[0] = 0
        for cp in weight_copies(te_ref[0], 0):
            cp.start()
        second = tile_after(0)

        @pl.when(second < nact)
        def _():
            for cp in weight_copies(te_ref[jnp.minimum(second, last)], 1):
                cp.start()

        def fill_body(tt, c):
            ts = [tt * SRC_UNROLL + u for u in range(SRC_UNROLL)]
            words = [pos_ref[t] for t in ts]
            for t, w in zip(ts, words):
                src_ref[w & ((1 << POS_BITS) - 1)] = t
                src_ref[lax.shift_right_logical(w, POS_BITS)] = t
            return c
        lax.fori_loop(0, SEQ // SRC_UNROLL, fill_body, 0)

        def pad_tile(t, c):
            pad_tok = lax.shift_right_logical(te_ref[t], N_EXPERTS.bit_length() - 1) * PART_TOKENS

            def pad_row(k, c2):
                src_ref[t * EXP_TM + k] = pad_tok
                return c2
            return lax.fori_loop(tv_ref[t], EXP_TM, pad_row, c)
        lax.fori_loop(0, nact, pad_tile, 0)

    active = i < nact
    prev = jnp.maximum(i - 1, 0)
    new_segment = (i == 0) | (te_ref[i] != te_ref[prev])

    @pl.when(active & ((i == 0) | (part != lax.shift_right_logical(te_ref[prev], N_EXPERTS.bit_length() - 1))))
    def _():
        @pl.when(i > 0)
        def _():
            resident_copy(part).start()
        resident_copy(part).wait()
        for item in gather_items(i, xcur):
            item()
        prepare_input(xcur)

    @pl.when(active & new_segment)
    def _():
        st = stage_ref[0]
        for cp in weight_copies(te_ref[i], st):
            cp.wait()
        wgb[...] = wgs[st].astype(jnp.bfloat16)
        wub[...] = wus[st].astype(jnp.bfloat16)
        wdb[...] = wds[st].astype(jnp.bfloat16)
        stage_ref[0] = 1 - st
        nxt = tile_after(i)
        nxt2 = tile_after(nxt)

        @pl.when((nxt < nact) & (nxt2 < nact))
        def _():
            for cp in weight_copies(te_ref[jnp.minimum(nxt2, last)], st):
                cp.start()

    @pl.when(active)
    def _():
        first, second = compute_chunks(xcur)
        items = gather_items(jnp.minimum(i + 1, nact - 1), 1 - xcur)
        per = -(-len(items) // len(first))
        for n, chunk in enumerate(first):
            for item in items[n * per:(n + 1) * per]:
                item()
            chunk()
        for n, chunk in enumerate(second):
            chunk()
            if n == PREP_AFTER_DOWN:
                prepare_input(1 - xcur)

    @pl.when(jnp.logical_not(active))
    def _():
        y_ref[...] = jnp.zeros_like(y_ref)


def _experts_call(te, tv, nact, tnext, pos, h, mod, wg, wu, wd):
    grid_spec = pltpu.PrefetchScalarGridSpec(
        num_scalar_prefetch=5,
        grid=(EXP_TILES,),
        in_specs=[
            pl.BlockSpec(memory_space=pl.ANY),
            pl.BlockSpec(mod.shape, lambda i, *_: (0, 0)),
            pl.BlockSpec(memory_space=pl.ANY),
            pl.BlockSpec(memory_space=pl.ANY),
            pl.BlockSpec(memory_space=pl.ANY),
        ],
        out_specs=pl.BlockSpec((TILE_ROWS, LANES), lambda i, *_: (i, 0)),
        scratch_shapes=[
            pltpu.SMEM((EXP_TILES * EXP_TM,), jnp.int32),
            pltpu.SMEM((1,), jnp.int32),
            pltpu.VMEM((PART_TOKENS * ROW_TILE, LANES), jnp.float32),
            pltpu.VMEM((2, TILE_ROWS, LANES), jnp.float32),
            pltpu.VMEM((2, EXP_TM, D_MODEL), jnp.bfloat16),
            pltpu.VMEM((2, D_MODEL, D_EXPERT), jnp.float32),
            pltpu.VMEM((2, D_MODEL, D_EXPERT), jnp.float32),
            pltpu.VMEM((2, D_EXPERT, D_MODEL), jnp.float32),
            pltpu.VMEM((D_MODEL, D_EXPERT), jnp.bfloat16),
            pltpu.VMEM((D_MODEL, D_EXPERT), jnp.bfloat16),
            pltpu.VMEM((D_EXPERT, D_MODEL), jnp.bfloat16),
            pltpu.SemaphoreType.DMA((1,)),
            pltpu.SemaphoreType.DMA((2, 3)),
        ],
    )
    return pl.pallas_call(
        _experts_kernel,
        grid_spec=grid_spec,
        out_shape=jax.ShapeDtypeStruct((EXP_TILES * TILE_ROWS, LANES), jnp.float32),
        compiler_params=pltpu.CompilerParams(
            dimension_semantics=("arbitrary",), vmem_limit_bytes=EXPERTS_VMEM_LIMIT),
        name="experts",
    )(te, tv, nact, tnext, pos, h, mod, wg, wu, wd)


def _combine_kernel(pos_ref, h_ref, cw_ref, mod_ref, g_ref, b_ref, ys_hbm, o_ref, ybuf, sem):
    i = pl.program_id(0)
    tm = CMB_TM
    slot = lax.rem(i, 2)

    def start_row(tile, k, s):
        word = pos_ref[tile * tm + k]
        for half, p in ((0, word & ((1 << POS_BITS) - 1)), (1, lax.shift_right_logical(word, POS_BITS))):
            pltpu.make_async_copy(ys_hbm.at[pl.ds(pl.multiple_of(p * ROW_TILE, ROW_TILE), ROW_TILE), :],
                                  ybuf.at[s, half, pl.ds(pl.multiple_of(k * ROW_TILE, ROW_TILE), ROW_TILE), :],
                                  sem.at[s]).start(priority=half)

    @pl.when(i == 0)
    def _():
        def body(kk, c):
            for u in range(SRC_UNROLL):
                start_row(0, kk * SRC_UNROLL + u, 0)
            return c
        lax.fori_loop(0, tm // SRC_UNROLL, body, 0)

    for half in range(2):
        pltpu.make_async_copy(ys_hbm.at[pl.ds(0, tm * ROW_TILE), :], ybuf.at[slot, half], sem.at[slot]).wait()

    g2 = mod_ref[0:1, 5 * D_MODEL:6 * D_MODEL]

    def chunk(c):
        rows = slice(c * CMB_CHUNK, (c + 1) * CMB_CHUNK)
        y0 = _load_row_tiles(ybuf, CMB_CHUNK, lead=(slot, 0), first=c * CMB_CHUNK)
        y1 = _load_row_tiles(ybuf, CMB_CHUNK, lead=(slot, 1), first=c * CMB_CHUNK)
        ffn = cw_ref[rows, CW_C0:CW_C0 + 1] * y0 + cw_ref[rows, CW_C1:CW_C1 + 1] * y1
        h = _load_row_tiles(h_ref, CMB_CHUNK, first=c * CMB_CHUNK)
        o_ref[rows, :] = _layer_norm(DEEPNORM_ALPHA * h + g2 * ffn, g_ref[...], b_ref[...])

    n_chunks = tm // CMB_CHUNK

    @pl.when(i + 1 < pl.num_programs(0))
    def _():
        per = tm // n_chunks
        for c in range(n_chunks):
            for k in range(c * per, (c + 1) * per):
                start_row(i + 1, k, 1 - slot)
            chunk(c)

    @pl.when(i + 1 == pl.num_programs(0))
    def _():
        for c in range(n_chunks):
            chunk(c)


def _combine_call(pos, h, ys, cw, mod, g, b):
    tm = CMB_TM
    grid_spec = pltpu.PrefetchScalarGridSpec(
        num_scalar_prefetch=1,
        grid=(SEQ // tm,),
        in_specs=[
            pl.BlockSpec((tm * ROW_TILE, LANES), lambda i, *_: (i, 0)),
            pl.BlockSpec((tm, LANES), lambda i, *_: (i, 0)),
            pl.BlockSpec(mod.shape, lambda i, *_: (0, 0)),
            pl.BlockSpec((1, D_MODEL), lambda i, *_: (0, 0)),
            pl.BlockSpec((1, D_MODEL), lambda i, *_: (0, 0)),
            pl.BlockSpec(memory_space=pl.ANY),
        ],
        out_specs=pl.BlockSpec((tm, D_MODEL), lambda i, *_: (i, 0)),
        scratch_shapes=[
            pltpu.VMEM((2, 2, tm * ROW_TILE, LANES), jnp.float32),
            pltpu.SemaphoreType.DMA((2,)),
        ],
    )
    return pl.pallas_call(
        _combine_kernel,
        grid_spec=grid_spec,
        out_shape=jax.ShapeDtypeStruct((SEQ, D_MODEL), jnp.float32),
        compiler_params=pltpu.CompilerParams(
            dimension_semantics=("arbitrary",), vmem_limit_bytes=VMEM_LIMIT),
        name="combine",
    )(pos, h, cw, mod, g, b, ys)


def kernel(x, c, ctx, c_ctx, ln_in_g, ln_in_b, w_mod, b_mod, w_in, rpb, w_pool_grp, pool_scale,
           w_attn_proj, w_pool_proj, w_out, ln1_g, ln1_b, w_router_group, b_router_group,
           w_router_expert, b_router_expert, w_expert_gate, w_expert_up, w_expert_down, ln2_g, ln2_b):
    assert x.shape == (1, SEQ, D_MODEL) and ctx.shape == (1, CTX_LEN, D_MODEL)
    assert w_mod.shape[0] == 1, "single-layer trunk"
    f32, bf16 = jnp.float32, jnp.bfloat16
    row = lambda v: v.reshape(1, -1).astype(f32)

    cond = jnp.concatenate([c, c_ctx[None], jnp.zeros((MOD_ROWS - 2, D_MODEL), f32)], axis=0)
    mod = _mod_call(cond, w_mod[0], row(b_mod[0]))

    lng, lnb = row(ln_in_g), row(ln_in_b)
    w_in_b = w_in[0].astype(bf16)
    u, h0 = _proj_call(x[0], mod, lng, lnb, w_in_b, mod_row=0, latent=True, tm=PROJ_TM)
    kvc, = _proj_call(ctx[0], mod, lng, lnb, w_in_b[:, ATT_W:3 * ATT_W], mod_row=1, latent=False, tm=CTX_LEN)

    h1 = _mix_call(h0, mod, u, kvc, _attn_bias_table(rpb[0]),
                   w_pool_grp[0].astype(bf16), row(pool_scale[0]),
                   w_attn_proj[0].astype(bf16), w_pool_proj[0].astype(bf16), w_out[0].astype(bf16),
                   row(ln1_g[0]), row(ln1_b[0]))

    n_logit = N_GROUPS + N_EXPERTS
    wrt = jnp.concatenate([w_router_group[0].T, w_router_expert[0].T,
                           jnp.zeros((LANES - n_logit, D_MODEL), f32)], axis=0)
    brt = jnp.concatenate([b_router_group[0], b_router_expert[0], jnp.zeros((LANES - n_logit,), f32)])
    brt = jnp.broadcast_to(brt[:, None], (LANES, LANES))
    pos, cw, plan = _route_call(h1, mod, wrt, brt)

    y = _experts_call(plan[PLAN_EXPERT, :EXP_TILES], plan[PLAN_VALID, :EXP_TILES], plan[PLAN_NACT, :1],
                      plan[PLAN_NEXT, :EXP_TILES], pos[POS_PACKED], h1, mod,
                      w_expert_gate[0], w_expert_up[0], w_expert_down[0])
    out = _combine_call(pos[POS_PACKED], h1, y, cw, mod, row(ln2_g[0]), row(ln2_b[0]))
    return out[None]
```

```python
import functools

import jax
import jax.numpy as jnp
from jax import lax
from jax.experimental import pallas as pl
from jax.experimental.pallas import tpu as pltpu

D_MODEL = 1024
SEQ = 16384
GRID_W = 64
ROWS = SEQ // GRID_W
CTX_LEN = 256
N_HEADS = 8
HEAD_DIM = 64
ATT_W = N_HEADS * HEAD_DIM
WIN_H = 8
WIN_W = 16
POOL_WINDOWS = (2, 4, 8, 16)
POOL_GROUPS = 4
POOL_DIM = 128
POOL_W = POOL_GROUPS * POOL_DIM
PROJ_W = 3 * ATT_W + POOL_W + 2 * D_MODEL
GATE_COL = 3 * ATT_W + POOL_W
N_GROUPS = 4
EXPERTS_PER_GROUP = 8
N_EXPERTS = N_GROUPS * EXPERTS_PER_GROUP
D_EXPERT = 512
N_MOD = 6
DEEPNORM_ALPHA = 2.0 ** 0.25
LN_EPS = 1e-5
NEG_INF = -1e30

LANES = 128
ROW_TILE = 8
MOD_ROWS = 8
PROJ_TM = 512
PROJ_SUB = 256
MIX_ROWS = 8
MIX_TQ = MIX_ROWS * GRID_W
KV_HALO = 4 * GRID_W
POOL_HALO = 16
ROUTE_TM = 512
EXP_TM = 256
SEQ_PARTS = 2
PART_TOKENS = SEQ // SEQ_PARTS
N_SEG = SEQ_PARTS * N_EXPERTS
EXP_TILES = 2 * SEQ // EXP_TM + N_SEG
CMB_TM = 512
CMB_CHUNK = 128
MERGE_ROWS = 128
BIAS_LANES = 1024
ATTN_AHEAD = 2
HALF = D_MODEL // 2
VMEM_LIMIT = 56 * 1024 * 1024
EXPERTS_VMEM_LIMIT = 60 * 1024 * 1024


def _layer_norm(x, g, b):
    mu = jnp.mean(x, axis=-1, keepdims=True)
    xc = x - mu
    var = jnp.mean(xc * xc, axis=-1, keepdims=True)
    return xc * lax.rsqrt(var + LN_EPS) * g + b


def _bdot(a, b):
    return jnp.dot(a, b, preferred_element_type=jnp.float32)


def _split_bf16(a):
    hi = a.astype(jnp.bfloat16)
    lo = (a - hi.astype(jnp.float32)).astype(jnp.bfloat16)
    return hi, lo


def _dot3(a, b):
    a_hi, a_lo = _split_bf16(a)
    b_hi, b_lo = _split_bf16(b)
    return _bdot(a_hi, b_hi) + (_bdot(a_hi, b_lo) + _bdot(a_lo, b_hi))


def _load_row_tiles(ref, tokens, lead=(), first=0):
    parts = [ref[(*lead, pl.ds(first * ROW_TILE + j, tokens, stride=ROW_TILE), slice(None))]
             for j in range(ROW_TILE)]
    return jnp.concatenate(parts, axis=-1)


def _store_row_tiles(ref, value, lead=()):
    tokens = value.shape[0]
    for j in range(ROW_TILE):
        ref[(*lead, pl.ds(j, tokens, stride=ROW_TILE), slice(None))] = value[:, j * LANES:(j + 1) * LANES]


def _mod_kernel(cond_ref, w_ref, b_ref, o_ref):
    cond = cond_ref[...]
    act = cond * jax.nn.sigmoid(cond)
    o_ref[...] = _dot3(act, w_ref[...]) + b_ref[...]


def _mod_call(cond, w_mod, b_mod):
    tn = 1536
    n = N_MOD * D_MODEL
    return pl.pallas_call(
        _mod_kernel,
        grid=(n // tn,),
        in_specs=[
            pl.BlockSpec((MOD_ROWS, D_MODEL), lambda i: (0, 0)),
            pl.BlockSpec((D_MODEL, tn), lambda i: (0, i)),
            pl.BlockSpec((1, tn), lambda i: (0, i)),
        ],
        out_specs=pl.BlockSpec((MOD_ROWS, tn), lambda i: (0, i)),
        out_shape=jax.ShapeDtypeStruct((MOD_ROWS, n), jnp.float32),
        compiler_params=pltpu.CompilerParams(
            dimension_semantics=("arbitrary",), vmem_limit_bytes=VMEM_LIMIT),
        name="mod",
    )(cond, w_mod, b_mod)


def _proj_kernel(x_ref, mod_ref, g_ref, b_ref, w_ref, o_ref, *h_out, mod_row, latent):
    shift = mod_ref[mod_row:mod_row + 1, 0:D_MODEL]
    scale = mod_ref[mod_row:mod_row + 1, D_MODEL:2 * D_MODEL]
    tm, n = o_ref.shape
    sub = min(tm, PROJ_SUB)

    def prep(r):
        rows = slice(r * sub, (r + 1) * sub)
        h = _layer_norm(x_ref[rows, :], g_ref[...], b_ref[...])
        if latent:
            h_out[0][rows, :] = h
        return (h * (1.0 + scale) + shift).astype(jnp.bfloat16)

    def finish(r, c, res):
        if latent and c == 0:
            lane = lax.broadcasted_iota(jnp.int32, (1, D_MODEL), 1)
            res = res * jnp.where(lane < ATT_W, HEAD_DIM ** -0.5, 1.0)
        if latent and c * D_MODEL >= GATE_COL:
            res = jax.nn.sigmoid(res)
        o_ref[r * sub:(r + 1) * sub, c * D_MODEL:(c + 1) * D_MODEL] = res.astype(jnp.bfloat16)

    hm = {0: prep(0)}
    waiting = None
    for r in range(tm // sub):
        for c in range(n // D_MODEL):
            res = _bdot(hm[r], w_ref[:, c * D_MODEL:(c + 1) * D_MODEL])
            if c == 0 and (r + 1) * sub < tm:
                hm[r + 1] = prep(r + 1)
            if waiting is not None:
                finish(*waiting)
            waiting = (r, c, res)
    finish(*waiting)


def _proj_call(x, mod, g, b, w, *, mod_row, latent, tm):
    rows, n = x.shape[0], w.shape[1]
    out_specs = [pl.BlockSpec((tm, n), lambda i: (i, 0))]
    out_shape = [jax.ShapeDtypeStruct((rows, n), jnp.bfloat16)]
    if latent:
        out_specs.append(pl.BlockSpec((tm, D_MODEL), lambda i: (i, 0)))
        out_shape.append(jax.ShapeDtypeStruct((rows, D_MODEL), jnp.float32))
    return pl.pallas_call(
        functools.partial(_proj_kernel, mod_row=mod_row, latent=latent),
        grid=(rows // tm,),
        in_specs=[
            pl.BlockSpec((tm, D_MODEL), lambda i: (i, 0)),
            pl.BlockSpec(mod.shape, lambda i: (0, 0)),
            pl.BlockSpec((1, D_MODEL), lambda i: (0, 0)),
            pl.BlockSpec((1, D_MODEL), lambda i: (0, 0)),
            pl.BlockSpec((D_MODEL, n), lambda i: (0, 0), pipeline_mode=pl.Buffered(1)),
        ],
        out_specs=out_specs,
        out_shape=out_shape,
        compiler_params=pltpu.CompilerParams(
            dimension_semantics=("arbitrary",), vmem_limit_bytes=VMEM_LIMIT),
        name="proj",
    )(x, mod, g, b, w)


def _attn_bias_table(rpb):
    col = jnp.arange(GRID_W, dtype=jnp.int32)
    col_start = jnp.clip(col - WIN_W // 2, 0, GRID_W - WIN_W)
    col_mask = (col[None, :] >= col_start[:, None]) & (col[None, :] < col_start[:, None] + WIN_W)
    col_off = jnp.clip(col[None, :] - col[:, None], 1 - WIN_W, WIN_W - 1) + (WIN_W - 1)
    onehot = (col_off[None] == jnp.arange(2 * WIN_W - 1, dtype=jnp.int32)[:, None, None]).astype(jnp.float32)
    tab = jnp.einsum("hrc,cqk->hqrk", rpb.astype(jnp.float32), onehot, precision=lax.Precision.HIGHEST)
    tab = jnp.where(col_mask[None, :, None, :], tab, NEG_INF)
    n_rows = 2 * WIN_H - 1
    flat = tab.reshape(N_HEADS // 2, 2 * GRID_W, n_rows * GRID_W)
    even = jnp.pad(flat, ((0, 0), (0, 0), (0, BIAS_LANES - n_rows * GRID_W)))
    odd = jnp.pad(flat[:, :, GRID_W:], ((0, 0), (0, 0), (0, BIAS_LANES - (n_rows - 1) * GRID_W)))

    def window_kernel(even_ref, odd_ref, o_ref):
        start = WIN_H - 1 - pl.program_id(0)
        base = pl.multiple_of(lax.shift_right_logical(start, 1) * LANES, LANES)
        width = WIN_H * GRID_W
        o_ref[0] = jnp.where((start & 1) == 0, even_ref[:, :, pl.ds(base, width)], odd_ref[:, :, pl.ds(base, width)])

    full = pl.BlockSpec(even.shape, lambda v: (0, 0, 0))
    return pl.pallas_call(
        window_kernel,
        grid=(WIN_H,),
        in_specs=[full, full],
        out_specs=pl.BlockSpec((1, N_HEADS // 2, 2 * GRID_W, WIN_H * GRID_W), lambda v: (v, 0, 0, 0)),
        out_shape=jax.ShapeDtypeStruct((WIN_H, N_HEADS // 2, 2 * GRID_W, WIN_H * GRID_W), jnp.float32),
        compiler_params=pltpu.CompilerParams(dimension_semantics=("arbitrary",), vmem_limit_bytes=VMEM_LIMIT),
        name="bias_table",
    )(even, odd)


def _mix_kernel(h_ref, mod_ref,
                q_ref, kp_ref, kc_ref, kn_ref, vp_ref, vc_ref, vn_ref,
                pp_ref, pc_ref, pn_ref, ga_ref, gb_ref,
                kvc_ref, bias_ref, wgrp_ref, pscale_ref, wap_ref, wpp_ref, wout_ref,
                ln1g_ref, ln1b_ref,
                o_ref,
                kbuf, vbuf, yabuf, pbuf, ypbuf, zbuf):
    b = pl.program_id(0)
    nb = pl.num_programs(0)

    kbuf[0:KV_HALO, :] = kp_ref[...]
    kbuf[KV_HALO:KV_HALO + MIX_TQ, :] = kc_ref[...]
    kbuf[KV_HALO + MIX_TQ:, :] = kn_ref[...]
    vbuf[0:KV_HALO, :] = vp_ref[...]
    vbuf[KV_HALO:KV_HALO + MIX_TQ, :] = vc_ref[...]
    vbuf[KV_HALO + MIX_TQ:, :] = vn_ref[...]

    lane = lax.broadcasted_iota(jnp.int32, (GRID_W, LANES), 1)
    first_head = lane < HEAD_DIM

    units = [(j, pair) for j in range(MIX_ROWS) for pair in range(N_HEADS // 2)]
    nt = (((1,), (1,)), ((), ()))

    def window(j):
        r = b * MIX_ROWS + j
        rs = jnp.clip(r - WIN_H // 2, 0, ROWS - WIN_H)
        return pl.multiple_of((rs - b * MIX_ROWS + WIN_H // 2) * GRID_W, GRID_W), r - rs

    def scores(j, pair):
        off, var = window(j)
        cols = slice(pair * LANES, (pair + 1) * LANES)
        q = q_ref[j * GRID_W:(j + 1) * GRID_W, cols]
        zero = jnp.zeros_like(q)
        q2 = jnp.concatenate([jnp.where(first_head, q, zero), jnp.where(first_head, zero, q)], axis=0)
        kw = kbuf[pl.ds(off, WIN_H * GRID_W), cols]
        s_loc = lax.dot_general(q2, kw, nt, preferred_element_type=jnp.float32) + bias_ref[var, pair]
        s_ctx = lax.dot_general(q2, kvc_ref[:, cols], nt, preferred_element_type=jnp.float32)
        return s_loc, s_ctx

    def values(j, pair, s_loc, s_ctx):
        off, _ = window(j)
        cols = slice(pair * LANES, (pair + 1) * LANES)
        vw = vbuf[pl.ds(off, WIN_H * GRID_W), cols]
        vctx = kvc_ref[:, ATT_W + pair * LANES:ATT_W + (pair + 1) * LANES]
        m = jnp.maximum(jnp.max(s_loc, axis=-1, keepdims=True), jnp.max(s_ctx, axis=-1, keepdims=True))
        p_loc = jnp.exp(s_loc - m).astype(jnp.bfloat16)
        p_ctx = jnp.exp(s_ctx - m).astype(jnp.bfloat16)
        ones = lambda rows: jnp.ones((rows, LANES), jnp.bfloat16)
        o2 = (_bdot(p_loc, jnp.concatenate([vw, ones(WIN_H * GRID_W)], axis=1))
              + _bdot(p_ctx, jnp.concatenate([vctx, ones(CTX_LEN)], axis=1)))
        o2 = o2[:, :LANES] / o2[:, LANES:]
        o_pair = jnp.where(first_head, o2[:GRID_W], o2[GRID_W:])
        yabuf[j * GRID_W:(j + 1) * GRID_W, cols] = o_pair.astype(jnp.bfloat16)

    pbuf[0:POOL_HALO, :] = jnp.where(b > 0, pp_ref[...].astype(jnp.float32), 0.0)
    pbuf[POOL_HALO:POOL_HALO + MIX_TQ, :] = pc_ref[...].astype(jnp.float32)
    pbuf[POOL_HALO + MIX_TQ:, :] = jnp.where(b < nb - 1, pn_ref[...].astype(jnp.float32), 0.0)
    edge = lax.broadcasted_iota(jnp.int32, (ROW_TILE, 1), 0)

    def pool_group(g):
        win = POOL_WINDOWS[g]
        lo, hi = win // 2, win - win // 2
        cols = slice(g * POOL_DIM, (g + 1) * POOL_DIM)
        acc = None
        for d in range(-lo, hi):
            term = pbuf[POOL_HALO + d:POOL_HALO + d + MIX_TQ, cols]
            acc = term if acc is None else acc + term
        assert max(lo, hi) <= ROW_TILE
        top = jnp.where(b == 0, 1.0 / (win - jnp.maximum(lo - edge, 0)).astype(jnp.float32), 1.0 / win)
        bot = jnp.where(b == nb - 1,
                        1.0 / (win - jnp.maximum(edge + hi - ROW_TILE, 0)).astype(jnp.float32), 1.0 / win)
        inv = jnp.concatenate([top, jnp.full((MIX_TQ - 2 * ROW_TILE, 1), 1.0 / win, jnp.float32), bot], axis=0)
        pooled = acc * inv - pbuf[POOL_HALO:POOL_HALO + MIX_TQ, cols]
        yp = _bdot(pooled.astype(jnp.bfloat16), wgrp_ref[g]) * pscale_ref[:, cols]
        ypbuf[:, cols] = yp.astype(jnp.bfloat16)

    def pooled_branch():
        zbuf[...] = gb_ref[...].astype(jnp.float32) * _bdot(ypbuf[...], wpp_ref[...])

    extra = {}
    for g in range(POOL_GROUPS):
        extra[(g + 1) * len(units) // (POOL_GROUPS + 2)] = functools.partial(pool_group, g)
    extra[(POOL_GROUPS + 1) * len(units) // (POOL_GROUPS + 2)] = pooled_branch
    pending = [scores(*u) for u in units[:ATTN_AHEAD]]
    for n, u in enumerate(units):
        if n + ATTN_AHEAD < len(units):
            pending.append(scores(*units[n + ATTN_AHEAD]))
        values(*u, *pending.pop(0))
        if n in extra:
            extra[n]()

    g1 = mod_ref[0:1, 2 * D_MODEL:3 * D_MODEL]
    n_chunks = MIX_TQ // MERGE_ROWS
    rows = lambda c: slice(c * MERGE_ROWS, (c + 1) * MERGE_ROWS)
    z, y = {}, {}

    def stage_a(c):
        ya = _bdot(yabuf[rows(c), :], wap_ref[...])
        z[c] = (ga_ref[rows(c), :].astype(jnp.float32) * ya + zbuf[rows(c), :]).astype(jnp.bfloat16)

    def stage_b(c):
        y[c] = _bdot(z.pop(c), wout_ref[...])

    def stage_c(c):
        hn = _layer_norm(DEEPNORM_ALPHA * h_ref[rows(c), :] + g1 * y.pop(c), ln1g_ref[...], ln1b_ref[...])
        for j in range(ROW_TILE):
            o_ref[pl.ds(c * MERGE_ROWS * ROW_TILE + j, MERGE_ROWS, stride=ROW_TILE), :] = hn[:, j * LANES:(j + 1) * LANES]

    for t in range(n_chunks + 2):
        if t < n_chunks:
            stage_a(t)
        if 0 <= t - 1 < n_chunks:
            stage_b(t - 1)
        if 0 <= t - 2 < n_chunks:
            stage_c(t - 2)


def _mix_call(h, mod, u, kvc, bias, wgrp, pscale, wap, wpp, wout, ln1g, ln1b):
    nb = SEQ // MIX_TQ
    halo_per_blk = MIX_TQ // KV_HALO
    n_halo = SEQ // KV_HALO
    ph_per_blk = MIX_TQ // POOL_HALO
    n_ph = SEQ // POOL_HALO

    def const(shape):
        return pl.BlockSpec(shape, lambda i: (0,) * len(shape), pipeline_mode=pl.Buffered(1))

    def prev_halo(c):
        return pl.BlockSpec((KV_HALO, ATT_W), lambda i: (jnp.maximum(i * halo_per_blk - 1, 0), c))

    def next_halo(c):
        return pl.BlockSpec((KV_HALO, ATT_W), lambda i: (jnp.minimum((i + 1) * halo_per_blk, n_halo - 1), c))

    def cur(c):
        return pl.BlockSpec((MIX_TQ, ATT_W), lambda i: (i, c))

    in_specs = [
        pl.BlockSpec((MIX_TQ, D_MODEL), lambda i: (i, 0)),
        const(mod.shape),
        cur(0),
        prev_halo(1), cur(1), next_halo(1),
        prev_halo(2), cur(2), next_halo(2),
        pl.BlockSpec((POOL_HALO, POOL_W), lambda i: (jnp.maximum(i * ph_per_blk - 1, 0), 3)),
        cur(3),
        pl.BlockSpec((POOL_HALO, POOL_W), lambda i: (jnp.minimum((i + 1) * ph_per_blk, n_ph - 1), 3)),
        pl.BlockSpec((MIX_TQ, D_MODEL), lambda i: (i, 2)),
        pl.BlockSpec((MIX_TQ, D_MODEL), lambda i: (i, 3)),
        const(kvc.shape), const(bias.shape), const(wgrp.shape), const(pscale.shape),
        const(wap.shape), const(wpp.shape), const(wout.shape),
        const((1, D_MODEL)), const((1, D_MODEL)),
    ]
    return pl.pallas_call(
        _mix_kernel,
        grid=(nb,),
        in_specs=in_specs,
        out_specs=pl.BlockSpec((MIX_TQ * ROW_TILE, LANES), lambda i: (i, 0)),
        out_shape=jax.ShapeDtypeStruct((SEQ * ROW_TILE, LANES), jnp.float32),
        scratch_shapes=[
            pltpu.VMEM((MIX_TQ + 2 * KV_HALO, ATT_W), jnp.bfloat16),
            pltpu.VMEM((MIX_TQ + 2 * KV_HALO, ATT_W), jnp.bfloat16),
            pltpu.VMEM((MIX_TQ, ATT_W), jnp.bfloat16),
            pltpu.VMEM((MIX_TQ + 2 * POOL_HALO, POOL_W), jnp.float32),
            pltpu.VMEM((MIX_TQ, POOL_W), jnp.bfloat16),
            pltpu.VMEM((MIX_TQ, D_MODEL), jnp.float32),
        ],
        compiler_params=pltpu.CompilerParams(
            dimension_semantics=("arbitrary",), vmem_limit_bytes=VMEM_LIMIT),
        name="mix",
    )(h, mod, u, u, u, u, u, u, u, u, u, u, u, u,
      kvc, bias, wgrp, pscale, wap, wpp, wout, ln1g, ln1b)


ID_E0, ID_E1, ID_R0, ID_R1 = 0, 1, 4, 5
POS_PACKED = 0
POS_BITS = 16
PLAN_EXPERT, PLAN_VALID, PLAN_NACT, PLAN_NEXT = 0, 1, 2, 3
PLAN_W = 2 * LANES
CW_C0, CW_C1 = 0, 1


def _route_kernel(h_ref, mod_ref, wrt_ref, brt_ref, pos_ref, cw_ref, plan_ref, carry_ref, ids_all):
    i = pl.program_id(0)
    tm = ROUTE_TM

    @pl.when(i == 0)
    def _():
        carry_ref[...] = jnp.zeros_like(carry_ref)

    shift = mod_ref[0:1, 3 * D_MODEL:4 * D_MODEL]
    scale = mod_ref[0:1, 4 * D_MODEL:5 * D_MODEL]
    hm = _load_row_tiles(h_ref, tm) * (1.0 + scale) + shift

    hm_hi, hm_lo = _split_bf16(hm)
    w_hi, w_lo = _split_bf16(wrt_ref[...])
    nt = (((1,), (1,)), ((), ()))
    dg = functools.partial(lax.dot_general, dimension_numbers=nt, preferred_element_type=jnp.float32)
    logits = dg(w_hi, hm_hi) + (dg(w_hi, hm_lo) + dg(w_lo, hm_hi)) + brt_ref[:, 0:1]

    sub = lax.broadcasted_iota(jnp.int32, (LANES, tm), 0)
    big = jnp.int32(1 << 20)
    is_grp = sub < N_GROUPS
    gl = jnp.where(is_grp, logits, -jnp.inf)
    gmax = jnp.max(gl, axis=0, keepdims=True)
    gidx = jnp.min(jnp.where(gl == gmax, sub, big), axis=0, keepdims=True)
    gsum = jnp.sum(jnp.where(is_grp, jnp.exp(logits - gmax), 0.0), axis=0, keepdims=True)
    p_group = 1.0 / gsum

    eid = sub - N_GROUPS
    sel = (eid >= 0) & (eid < N_EXPERTS) & (lax.shift_right_arithmetic(eid, 3) == gidx)
    el = jnp.where(sel, logits, -jnp.inf)
    l0 = jnp.max(el, axis=0, keepdims=True)
    i0 = jnp.min(jnp.where(el == l0, sub, big), axis=0, keepdims=True)
    el2 = jnp.where(sub == i0, -jnp.inf, el)
    l1 = jnp.max(el2, axis=0, keepdims=True)
    i1 = jnp.min(jnp.where(el2 == l1, sub, big), axis=0, keepdims=True)
    t = jnp.exp(l1 - l0)
    w0 = 1.0 / (1.0 + t)
    w1 = t / (1.0 + t)

    half_rows = jnp.where(i >= pl.num_programs(0) // SEQ_PARTS, N_EXPERTS, 0)
    i0 = i0 + half_rows
    i1 = i1 + half_rows
    onehot = jnp.where((sub == i0) | (sub == i1), 1.0, 0.0)
    rr = lax.broadcasted_iota(jnp.int32, (tm, tm), 0)
    cc = lax.broadcasted_iota(jnp.int32, (tm, tm), 1)
    earlier = jnp.where(rr < cc, 1.0, 0.0).astype(jnp.bfloat16)
    carry = carry_ref[:, 0:1]
    prefix = _bdot(onehot.astype(jnp.bfloat16), earlier) + carry
    r0 = jnp.sum(jnp.where(sub == i0, prefix, 0.0), axis=0, keepdims=True)
    r1 = jnp.sum(jnp.where(sub == i1, prefix, 0.0), axis=0, keepdims=True)
    total = jnp.broadcast_to(carry + jnp.sum(onehot, axis=1, keepdims=True), carry_ref.shape)
    carry_ref[...] = total

    sub8 = lax.broadcasted_iota(jnp.int32, (ROW_TILE, tm), 0)
    ids = jnp.zeros((ROW_TILE, tm), jnp.int32)
    for idx, val in ((ID_E0, i0 - N_GROUPS), (ID_E1, i1 - N_GROUPS),
                     (ID_R0, r0.astype(jnp.int32)), (ID_R1, r1.astype(jnp.int32))):
        ids = jnp.where(sub8 == idx, val, ids)
    ids_all[:, pl.ds(pl.multiple_of(i * tm, tm), tm)] = ids

    cwt = jnp.where(sub == CW_C0, p_group * w0, jnp.where(sub == CW_C1, p_group * w1, 0.0))
    cw_ref[...] = cwt.T

    @pl.when(i == pl.num_programs(0) - 1)
    def _():
        subq = lax.broadcasted_iota(jnp.int32, (LANES, LANES), 0)
        laneq = lax.broadcasted_iota(jnp.int32, (LANES, LANES), 1)
        cnt = total.astype(jnp.int32)
        tiles = lax.shift_right_logical(cnt + (EXP_TM - 1), EXP_TM.bit_length() - 1).astype(jnp.float32)
        incl = jnp.where(laneq <= subq, 1.0, 0.0).astype(jnp.bfloat16)
        tile_end = _bdot(incl, tiles.astype(jnp.bfloat16))
        tile_start = tile_end - tiles
        seg = (tile_start * EXP_TM).astype(jnp.int32)
        nact = jnp.max(tile_end, axis=0, keepdims=True)

        ids_full = ids_all[...]
        look = jnp.zeros_like(ids_full)
        for e in range(N_SEG):
            look = jnp.where(ids_full == e, seg[N_GROUPS + e, 0], look)
        pos01 = look + pltpu.roll(ids_full, ID_R0 - ID_E0, axis=0)
        assert EXP_TILES * EXP_TM <= 1 << POS_BITS
        pos_ref[...] = pos01 | (pltpu.roll(pos01, ROW_TILE - 1, axis=0) << POS_BITS)

        subp = lax.broadcasted_iota(jnp.int32, (LANES, PLAN_W), 0)
        tile = lax.broadcasted_iota(jnp.int32, (LANES, PLAN_W), 1).astype(jnp.float32)
        is_exp = (subp >= N_GROUPS) & (subp < N_GROUPS + N_SEG)
        end_col = tile_end[:, 0:1]
        nact_s = nact[:, 0:1]
        te = jnp.sum(jnp.where(is_exp & (tile >= end_col), 1.0, 0.0), axis=0, keepdims=True)
        te_last = jnp.sum(jnp.where(is_exp & (nact_s - 1.0 >= end_col), 1.0, 0.0), axis=0, keepdims=True)[:, 0:1]
        tile_row = tile[0:1, :]
        te = jnp.minimum(jnp.where(tile_row < nact_s, te, te_last), N_SEG - 1.0)
        mine = (subp - N_GROUPS).astype(jnp.float32) == te
        cnt_sel = jnp.sum(jnp.where(mine, total[:, 0:1], 0.0), axis=0, keepdims=True)
        start_sel = jnp.sum(jnp.where(mine, tile_start[:, 0:1], 0.0), axis=0, keepdims=True)
        end_sel = jnp.sum(jnp.where(mine, end_col, 0.0), axis=0, keepdims=True)
        valid = jnp.clip(cnt_sel - (tile_row - start_sel) * EXP_TM, 0.0, float(EXP_TM))
        valid = jnp.where(tile_row < nact_s, valid, 0.0)
        subr = lax.broadcasted_iota(jnp.int32, (ROW_TILE, PLAN_W), 0)
        plan = jnp.where(subr == PLAN_EXPERT, te, jnp.where(subr == PLAN_VALID, valid,
                         jnp.where(subr == PLAN_NACT, nact_s, jnp.where(subr == PLAN_NEXT, end_sel, 0.0))))
        plan_ref[...] = plan.astype(jnp.int32)


def _route_call(h, mod, wrt, brt):
    tm = ROUTE_TM
    return pl.pallas_call(
        _route_kernel,
        grid=(SEQ // tm,),
        in_specs=[
            pl.BlockSpec((tm * ROW_TILE, LANES), lambda i: (i, 0)),
            pl.BlockSpec(mod.shape, lambda i: (0, 0)),
            pl.BlockSpec((LANES, D_MODEL), lambda i: (0, 0)),
            pl.BlockSpec((LANES, LANES), lambda i: (0, 0)),
        ],
        out_specs=[
            pl.BlockSpec((ROW_TILE, SEQ), lambda i: (0, 0)),
            pl.BlockSpec((tm, LANES), lambda i: (i, 0)),
            pl.BlockSpec((ROW_TILE, PLAN_W), lambda i: (0, 0)),
        ],
        out_shape=[
            jax.ShapeDtypeStruct((ROW_TILE, SEQ), jnp.int32),
            jax.ShapeDtypeStruct((SEQ, LANES), jnp.float32),
            jax.ShapeDtypeStruct((ROW_TILE, PLAN_W), jnp.int32),
        ],
        scratch_shapes=[pltpu.VMEM((LANES, LANES), jnp.float32),
                        pltpu.VMEM((ROW_TILE, SEQ), jnp.int32)],
        compiler_params=pltpu.CompilerParams(
            dimension_semantics=("arbitrary",), vmem_limit_bytes=VMEM_LIMIT),
        name="route",
    )(h, mod, wrt, brt)


SRC_UNROLL = 8
EXP_CHUNK = 256
PREP_AFTER_DOWN = 1
TILE_ROWS = EXP_TM * ROW_TILE


def _experts_kernel(te_ref, tv_ref, nact_ref, tnext_ref, pos_ref,
                    h_hbm, mod_ref, wg_hbm, wu_hbm, wd_hbm,
                    y_ref,
                    src_ref, stage_ref, hres, xbuf, xmat, wgs, wus, wds, wgb, wub, wdb, rsem, wsem):
    i = pl.program_id(0)
    last = pl.num_programs(0) - 1
    nact = nact_ref[0]
    xcur = lax.rem(i, 2)
    part = lax.shift_right_logical(te_ref[i], N_EXPERTS.bit_length() - 1)

    def weight_copies(segment, st):
        e = segment & (N_EXPERTS - 1)
        return [pltpu.make_async_copy(w_hbm.at[e], stage.at[st], wsem.at[st, n])
                for n, (w_hbm, stage) in enumerate(((wg_hbm, wgs), (wu_hbm, wus), (wd_hbm, wds)))]

    def tile_after(t):
        return tnext_ref[jnp.minimum(t, last)]

    def gather_row(tile, k, s):
        local = (src_ref[tile * EXP_TM + k] - part * PART_TOKENS) & (PART_TOKENS - 1)
        xbuf[s, k * ROW_TILE:(k + 1) * ROW_TILE, :] = hres[pl.ds(pl.multiple_of(local * ROW_TILE, ROW_TILE),
                                                             ROW_TILE), :]

    def gather_items(tile, s):
        return [functools.partial(gather_row, tile, k, s) for k in range(EXP_TM)]

    def prepare_input(xs):
        x = _load_row_tiles(xbuf, EXP_TM, lead=(xs,))
        shift = mod_ref[0:1, 3 * D_MODEL:4 * D_MODEL]
        scale = mod_ref[0:1, 4 * D_MODEL:5 * D_MODEL]
        xmat[xs] = (x * (1.0 + scale) + shift).astype(jnp.bfloat16)

    def compute_chunks(xs):
        state = {"act": []}

        def gate(c):
            def run():
                state["a"] = _bdot(xmat[xs], wgb[:, c * EXP_CHUNK:(c + 1) * EXP_CHUNK])
            return run

        def up(c):
            def run():
                a = state["a"]
                u = _bdot(xmat[xs], wub[:, c * EXP_CHUNK:(c + 1) * EXP_CHUNK])
                state["act"].append((a * jax.nn.sigmoid(a) * u).astype(jnp.bfloat16))
            return run

        def down(c):
            def run():
                if c == 0:
                    state["actf"] = jnp.concatenate(state["act"], axis=-1)
                yc = _bdot(state["actf"], wdb[:, c * EXP_CHUNK:(c + 1) * EXP_CHUNK])
                for jj in range(EXP_CHUNK // LANES):
                    j = c * (EXP_CHUNK // LANES) + jj
                    y_ref[pl.ds(j, EXP_TM, stride=ROW_TILE), :] = yc[:, jj * LANES:(jj + 1) * LANES]
            return run

        first = []
        for c in range(D_EXPERT // EXP_CHUNK):
            first += [gate(c), up(c)]
        return first, [down(c) for c in range(D_MODEL // EXP_CHUNK)]

    def resident_copy(p):
        rows = PART_TOKENS * ROW_TILE
        return pltpu.make_async_copy(h_hbm.at[pl.ds(pl.multiple_of(p * rows, rows), rows), :], hres, rsem.at[0])

    @pl.when(i == 0)
    def _():
        resident_copy(part).start()
        stage_ref[0] = 0
        for cp in weight_copies(te_ref[0], 0):
            cp.start()
        second = tile_after(0)

        @pl.when(second < nact)
        def _():
            for cp in weight_copies(te_ref[jnp.minimum(second, last)], 1):
                cp.start()

        def fill_body(tt, c):
            ts = [tt * SRC_UNROLL + u for u in range(SRC_UNROLL)]
            words = [pos_ref[t] for t in ts]
            for t, w in zip(ts, words):
                src_ref[w & ((1 << POS_BITS) - 1)] = t
                src_ref[lax.shift_right_logical(w, POS_BITS)] = t
            return c
        lax.fori_loop(0, SEQ // SRC_UNROLL, fill_body, 0)

        def pad_tile(t, c):
            pad_tok = lax.shift_right_logical(te_ref[t], N_EXPERTS.bit_length() - 1) * PART_TOKENS

            def pad_row(k, c2):
                src_ref[t * EXP_TM + k] = pad_tok
                return c2
            return lax.fori_loop(tv_ref[t], EXP_TM, pad_row, c)
        lax.fori_loop(0, nact, pad_tile, 0)

    active = i < nact
    prev = jnp.maximum(i - 1, 0)
    new_segment = (i == 0) | (te_ref[i] != te_ref[prev])

    @pl.when(active & ((i == 0) | (part != lax.shift_right_logical(te_ref[prev], N_EXPERTS.bit_length() - 1))))
    def _():
        @pl.when(i > 0)
        def _():
            resident_copy(part).start()
        resident_copy(part).wait()
        for item in gather_items(i, xcur):
            item()
        prepare_input(xcur)

    @pl.when(active & new_segment)
    def _():
        st = stage_ref[0]
        for cp in weight_copies(te_ref[i], st):
            cp.wait()
        wgb[...] = wgs[st].astype(jnp.bfloat16)
        wub[...] = wus[st].astype(jnp.bfloat16)
        wdb[...] = wds[st].astype(jnp.bfloat16)
        stage_ref[0] = 1 - st
        nxt = tile_after(i)
        nxt2 = tile_after(nxt)

        @pl.when((nxt < nact) & (nxt2 < nact))
        def _():
            for cp in weight_copies(te_ref[jnp.minimum(nxt2, last)], st):
                cp.start()

    @pl.when(active)
    def _():
        first, second = compute_chunks(xcur)
        items = gather_items(jnp.minimum(i + 1, nact - 1), 1 - xcur)
        per = -(-len(items) // len(first))
        for n, chunk in enumerate(first):
            for item in items[n * per:(n + 1) * per]:
                item()
            chunk()
        for n, chunk in enumerate(second):
            chunk()
            if n == PREP_AFTER_DOWN:
                prepare_input(1 - xcur)

    @pl.when(jnp.logical_not(active))
    def _():
        y_ref[...] = jnp.zeros_like(y_ref)


def _experts_call(te, tv, nact, tnext, pos, h, mod, wg, wu, wd):
    grid_spec = pltpu.PrefetchScalarGridSpec(
        num_scalar_prefetch=5,
        grid=(EXP_TILES,),
        in_specs=[
            pl.BlockSpec(memory_space=pl.ANY),
            pl.BlockSpec(mod.shape, lambda i, *_: (0, 0)),
            pl.BlockSpec(memory_space=pl.ANY),
            pl.BlockSpec(memory_space=pl.ANY),
            pl.BlockSpec(memory_space=pl.ANY),
        ],
        out_specs=pl.BlockSpec((TILE_ROWS, LANES), lambda i, *_: (i, 0)),
        scratch_shapes=[
            pltpu.SMEM((EXP_TILES * EXP_TM,), jnp.int32),
            pltpu.SMEM((1,), jnp.int32),
            pltpu.VMEM((PART_TOKENS * ROW_TILE, LANES), jnp.float32),
            pltpu.VMEM((2, TILE_ROWS, LANES), jnp.float32),
            pltpu.VMEM((2, EXP_TM, D_MODEL), jnp.bfloat16),
            pltpu.VMEM((2, D_MODEL, D_EXPERT), jnp.float32),
            pltpu.VMEM((2, D_MODEL, D_EXPERT), jnp.float32),
            pltpu.VMEM((2, D_EXPERT, D_MODEL), jnp.float32),
            pltpu.VMEM((D_MODEL, D_EXPERT), jnp.bfloat16),
            pltpu.VMEM((D_MODEL, D_EXPERT), jnp.bfloat16),
            pltpu.VMEM((D_EXPERT, D_MODEL), jnp.bfloat16),
            pltpu.SemaphoreType.DMA((1,)),
            pltpu.SemaphoreType.DMA((2, 3)),
        ],
    )
    return pl.pallas_call(
        _experts_kernel,
        grid_spec=grid_spec,
        out_shape=jax.ShapeDtypeStruct((EXP_TILES * TILE_ROWS, LANES), jnp.float32),
        compiler_params=pltpu.CompilerParams(
            dimension_semantics=("arbitrary",), vmem_limit_bytes=EXPERTS_VMEM_LIMIT),
        name="experts",
    )(te, tv, nact, tnext, pos, h, mod, wg, wu, wd)


def _combine_kernel(pos_ref, h_ref, cw_ref, mod_ref, g_ref, b_ref, ys_hbm, o_ref, ybuf, sem):
    i = pl.program_id(0)
    tm = CMB_TM
    slot = lax.rem(i, 2)

    def start_row(tile, k, s):
        word = pos_ref[tile * tm + k]
        for half, p in ((0, word & ((1 << POS_BITS) - 1)), (1, lax.shift_right_logical(word, POS_BITS))):
            pltpu.make_async_copy(ys_hbm.at[pl.ds(pl.multiple_of(p * ROW_TILE, ROW_TILE), ROW_TILE), :],
                                  ybuf.at[s, half, pl.ds(pl.multiple_of(k * ROW_TILE, ROW_TILE), ROW_TILE), :],
                                  sem.at[s]).start(priority=half)

    @pl.when(i == 0)
    def _():
        def body(kk, c):
            for u in range(SRC_UNROLL):
                start_row(0, kk * SRC_UNROLL + u, 0)
            return c
        lax.fori_loop(0, tm // SRC_UNROLL, body, 0)

    for half in range(2):
        pltpu.make_async_copy(ys_hbm.at[pl.ds(0, tm * ROW_TILE), :], ybuf.at[slot, half], sem.at[slot]).wait()

    g2 = mod_ref[0:1, 5 * D_MODEL:6 * D_MODEL]

    def chunk(c):
        rows = slice(c * CMB_CHUNK, (c + 1) * CMB_CHUNK)
        y0 = _load_row_tiles(ybuf, CMB_CHUNK, lead=(slot, 0), first=c * CMB_CHUNK)
        y1 = _load_row_tiles(ybuf, CMB_CHUNK, lead=(slot, 1), first=c * CMB_CHUNK)
        ffn = cw_ref[rows, CW_C0:CW_C0 + 1] * y0 + cw_ref[rows, CW_C1:CW_C1 + 1] * y1
        h = _load_row_tiles(h_ref, CMB_CHUNK, first=c * CMB_CHUNK)
        o_ref[rows, :] = _layer_norm(DEEPNORM_ALPHA * h + g2 * ffn, g_ref[...], b_ref[...])

    n_chunks = tm // CMB_CHUNK

    @pl.when(i + 1 < pl.num_programs(0))
    def _():
        per = tm // n_chunks
        for c in range(n_chunks):
            for k in range(c * per, (c + 1) * per):
                start_row(i + 1, k, 1 - slot)
            chunk(c)

    @pl.when(i + 1 == pl.num_programs(0))
    def _():
        for c in range(n_chunks):
            chunk(c)


def _combine_call(pos, h, ys, cw, mod, g, b):
    tm = CMB_TM
    grid_spec = pltpu.PrefetchScalarGridSpec(
        num_scalar_prefetch=1,
        grid=(SEQ // tm,),
        in_specs=[
            pl.BlockSpec((tm * ROW_TILE, LANES), lambda i, *_: (i, 0)),
            pl.BlockSpec((tm, LANES), lambda i, *_: (i, 0)),
            pl.BlockSpec(mod.shape, lambda i, *_: (0, 0)),
            pl.BlockSpec((1, D_MODEL), lambda i, *_: (0, 0)),
            pl.BlockSpec((1, D_MODEL), lambda i, *_: (0, 0)),
            pl.BlockSpec(memory_space=pl.ANY),
        ],
        out_specs=pl.BlockSpec((tm, D_MODEL), lambda i, *_: (i, 0)),
        scratch_shapes=[
            pltpu.VMEM((2, 2, tm * ROW_TILE, LANES), jnp.float32),
            pltpu.SemaphoreType.DMA((2,)),
        ],
    )
    return pl.pallas_call(
        _combine_kernel,
        grid_spec=grid_spec,
        out_shape=jax.ShapeDtypeStruct((SEQ, D_MODEL), jnp.float32),
        compiler_params=pltpu.CompilerParams(
            dimension_semantics=("arbitrary",), vmem_limit_bytes=VMEM_LIMIT),
        name="combine",
    )(pos, h, cw, mod, g, b, ys)


def kernel(x, c, ctx, c_ctx, ln_in_g, ln_in_b, w_mod, b_mod, w_in, rpb, w_pool_grp, pool_scale,
           w_attn_proj, w_pool_proj, w_out, ln1_g, ln1_b, w_router_group, b_router_group,
           w_router_expert, b_router_expert, w_expert_gate, w_expert_up, w_expert_down, ln2_g, ln2_b):
    assert x.shape == (1, SEQ, D_MODEL) and ctx.shape == (1, CTX_LEN, D_MODEL)
    assert w_mod.shape[0] == 1, "single-layer trunk"
    f32, bf16 = jnp.float32, jnp.bfloat16
    row = lambda v: v.reshape(1, -1).astype(f32)

    cond = jnp.concatenate([c, c_ctx[None], jnp.zeros((MOD_ROWS - 2, D_MODEL), f32)], axis=0)
    mod = _mod_call(cond, w_mod[0], row(b_mod[0]))

    lng, lnb = row(ln_in_g), row(ln_in_b)
    w_in_b = w_in[0].astype(bf16)
    u, h0 = _proj_call(x[0], mod, lng, lnb, w_in_b, mod_row=0, latent=True, tm=PROJ_TM)
    kvc, = _proj_call(ctx[0], mod, lng, lnb, w_in_b[:, ATT_W:3 * ATT_W], mod_row=1, latent=False, tm=CTX_LEN)

    h1 = _mix_call(h0, mod, u, kvc, _attn_bias_table(rpb[0]),
                   w_pool_grp[0].astype(bf16), row(pool_scale[0]),
                   w_attn_proj[0].astype(bf16), w_pool_proj[0].astype(bf16), w_out[0].astype(bf16),
                   row(ln1_g[0]), row(ln1_b[0]))

    n_logit = N_GROUPS + N_EXPERTS
    wrt = jnp.concatenate([w_router_group[0].T, w_router_expert[0].T,
                           jnp.zeros((LANES - n_logit, D_MODEL), f32)], axis=0)
    brt = jnp.concatenate([b_router_group[0], b_router_expert[0], jnp.zeros((LANES - n_logit,), f32)])
    brt = jnp.broadcast_to(brt[:, None], (LANES, LANES))
    pos, cw, plan = _route_call(h1, mod, wrt, brt)

    posw = pos[POS_PACKED]
    y = _experts_call(plan[PLAN_EXPERT, :EXP_TILES], plan[PLAN_VALID, :EXP_TILES], plan[PLAN_NACT, :1],
                      plan[PLAN_NEXT, :EXP_TILES], posw, h1, mod,
                      w_expert_gate[0], w_expert_up[0], w_expert_down[0])
    out = _combine_call(posw, h1, y, cw, mod, row(ln2_g[0]), row(ln2_b[0]))
    return out[None]
```

```python
import functools

import jax
import jax.numpy as jnp
from jax import lax
from jax.experimental import pallas as pl
from jax.experimental.pallas import tpu as pltpu

D_MODEL = 1024
SEQ = 16384
GRID_W = 64
ROWS = SEQ // GRID_W
CTX_LEN = 256
N_HEADS = 8
HEAD_DIM = 64
ATT_W = N_HEADS * HEAD_DIM
WIN_H = 8
WIN_W = 16
POOL_WINDOWS = (2, 4, 8, 16)
POOL_GROUPS = 4
POOL_DIM = 128
POOL_W = POOL_GROUPS * POOL_DIM
PROJ_W = 3 * ATT_W + POOL_W + 2 * D_MODEL
GATE_COL = 3 * ATT_W + POOL_W
N_GROUPS = 4
EXPERTS_PER_GROUP = 8
N_EXPERTS = N_GROUPS * EXPERTS_PER_GROUP
D_EXPERT = 512
N_MOD = 6
DEEPNORM_ALPHA = 2.0 ** 0.25
LN_EPS = 1e-5
NEG_INF = -1e30

LANES = 128
ROW_TILE = 8
MOD_ROWS = 8
PROJ_TM = 512
PROJ_SUB = 256
MIX_ROWS = 8
MIX_TQ = MIX_ROWS * GRID_W
KV_HALO = 4 * GRID_W
POOL_HALO = 16
ROUTE_TM = 512
ROUTE_LOGIT_ROWS = 40
ROUTE_SEG_ROWS = 72
EXP_TM = 256
SEQ_PARTS = 2
PART_TOKENS = SEQ // SEQ_PARTS
N_SEG = SEQ_PARTS * N_EXPERTS
EXP_TILES = 2 * SEQ // EXP_TM + N_SEG
CMB_TM = 512
CMB_CHUNK = 128
MERGE_ROWS = 256
BIAS_LANES = 1024
ATTN_AHEAD = 3
HALF = D_MODEL // 2
VMEM_LIMIT = 56 * 1024 * 1024
EXPERTS_VMEM_LIMIT = 60 * 1024 * 1024


def _layer_norm(x, g, b):
    mu = jnp.mean(x, axis=-1, keepdims=True)
    xc = x - mu
    var = jnp.mean(xc * xc, axis=-1, keepdims=True)
    return xc * lax.rsqrt(var + LN_EPS) * g + b


def _bdot(a, b):
    return jnp.dot(a, b, preferred_element_type=jnp.float32)


def _split_bf16(a):
    hi = a.astype(jnp.bfloat16)
    lo = (a - hi.astype(jnp.float32)).astype(jnp.bfloat16)
    return hi, lo


def _dot3(a, b):
    a_hi, a_lo = _split_bf16(a)
    b_hi, b_lo = _split_bf16(b)
    return _bdot(a_hi, b_hi) + (_bdot(a_hi, b_lo) + _bdot(a_lo, b_hi))


def _load_row_tiles(ref, tokens, lead=(), first=0):
    parts = [ref[(*lead, pl.ds(first * ROW_TILE + j, tokens, stride=ROW_TILE), slice(None))]
             for j in range(ROW_TILE)]
    return jnp.concatenate(parts, axis=-1)


def _store_row_tiles(ref, value, lead=()):
    tokens = value.shape[0]
    for j in range(ROW_TILE):
        ref[(*lead, pl.ds(j, tokens, stride=ROW_TILE), slice(None))] = value[:, j * LANES:(j + 1) * LANES]


def _mod_kernel(cond_ref, w_ref, b_ref, o_ref):
    cond = cond_ref[...]
    act = cond * jax.nn.sigmoid(cond)
    o_ref[...] = _dot3(act, w_ref[...]) + b_ref[...]


def _mod_call(cond, w_mod, b_mod):
    tn = 1536
    n = N_MOD * D_MODEL
    return pl.pallas_call(
        _mod_kernel,
        grid=(n // tn,),
        in_specs=[
            pl.BlockSpec((MOD_ROWS, D_MODEL), lambda i: (0, 0)),
            pl.BlockSpec((D_MODEL, tn), lambda i: (0, i)),
            pl.BlockSpec((1, tn), lambda i: (0, i)),
        ],
        out_specs=pl.BlockSpec((MOD_ROWS, tn), lambda i: (0, i)),
        out_shape=jax.ShapeDtypeStruct((MOD_ROWS, n), jnp.float32),
        compiler_params=pltpu.CompilerParams(
            dimension_semantics=("arbitrary",), vmem_limit_bytes=VMEM_LIMIT),
        name="mod",
    )(cond, w_mod, b_mod)


def _proj_kernel(x_ref, mod_ref, g_ref, b_ref, w_ref, o_ref, *h_out, mod_row, latent):
    shift = mod_ref[mod_row:mod_row + 1, 0:D_MODEL]
    scale = mod_ref[mod_row:mod_row + 1, D_MODEL:2 * D_MODEL]
    tm, n = o_ref.shape
    sub = min(tm, PROJ_SUB)

    def prep(r):
        rows = slice(r * sub, (r + 1) * sub)
        h = _layer_norm(x_ref[rows, :], g_ref[...], b_ref[...])
        if latent:
            h_out[0][rows, :] = h
        return (h * (1.0 + scale) + shift).astype(jnp.bfloat16)

    def finish(r, c, res):
        if latent and c == 0:
            lane = lax.broadcasted_iota(jnp.int32, (1, D_MODEL), 1)
            res = res * jnp.where(lane < ATT_W, HEAD_DIM ** -0.5, 1.0)
        if latent and c * D_MODEL >= GATE_COL:
            res = jax.nn.sigmoid(res)
        o_ref[r * sub:(r + 1) * sub, c * D_MODEL:(c + 1) * D_MODEL] = res.astype(jnp.bfloat16)

    hm = {0: prep(0)}
    waiting = None
    for r in range(tm // sub):
        for c in range(n // D_MODEL):
            res = _bdot(hm[r], w_ref[:, c * D_MODEL:(c + 1) * D_MODEL])
            if c == 0 and (r + 1) * sub < tm:
                hm[r + 1] = prep(r + 1)
            if waiting is not None:
                finish(*waiting)
            waiting = (r, c, res)
    finish(*waiting)


def _proj_call(x, mod, g, b, w, *, mod_row, latent, tm):
    rows, n = x.shape[0], w.shape[1]
    out_specs = [pl.BlockSpec((tm, n), lambda i: (i, 0))]
    out_shape = [jax.ShapeDtypeStruct((rows, n), jnp.bfloat16)]
    if latent:
        out_specs.append(pl.BlockSpec((tm, D_MODEL), lambda i: (i, 0)))
        out_shape.append(jax.ShapeDtypeStruct((rows, D_MODEL), jnp.float32))
    return pl.pallas_call(
        functools.partial(_proj_kernel, mod_row=mod_row, latent=latent),
        grid=(rows // tm,),
        in_specs=[
            pl.BlockSpec((tm, D_MODEL), lambda i: (i, 0)),
            pl.BlockSpec(mod.shape, lambda i: (0, 0)),
            pl.BlockSpec((1, D_MODEL), lambda i: (0, 0)),
            pl.BlockSpec((1, D_MODEL), lambda i: (0, 0)),
            pl.BlockSpec((D_MODEL, n), lambda i: (0, 0), pipeline_mode=pl.Buffered(1)),
        ],
        out_specs=out_specs,
        out_shape=out_shape,
        compiler_params=pltpu.CompilerParams(
            dimension_semantics=("arbitrary",), vmem_limit_bytes=VMEM_LIMIT),
        name="proj",
    )(x, mod, g, b, w)


def _attn_bias_table(rpb):
    col = jnp.arange(GRID_W, dtype=jnp.int32)
    col_start = jnp.clip(col - WIN_W // 2, 0, GRID_W - WIN_W)
    col_mask = (col[None, :] >= col_start[:, None]) & (col[None, :] < col_start[:, None] + WIN_W)
    col_off = jnp.clip(col[None, :] - col[:, None], 1 - WIN_W, WIN_W - 1) + (WIN_W - 1)
    onehot = (col_off[None] == jnp.arange(2 * WIN_W - 1, dtype=jnp.int32)[:, None, None]).astype(jnp.float32)
    tab = jnp.einsum("hrc,cqk->hqrk", rpb.astype(jnp.float32), onehot, precision=lax.Precision.HIGHEST)
    tab = jnp.where(col_mask[None, :, None, :], tab, NEG_INF)
    n_rows = 2 * WIN_H - 1
    flat = tab.reshape(N_HEADS // 2, 2 * GRID_W, n_rows * GRID_W)
    even = jnp.pad(flat, ((0, 0), (0, 0), (0, BIAS_LANES - n_rows * GRID_W)))
    odd = jnp.pad(flat[:, :, GRID_W:], ((0, 0), (0, 0), (0, BIAS_LANES - (n_rows - 1) * GRID_W)))

    def window_kernel(even_ref, odd_ref, o_ref):
        start = WIN_H - 1 - pl.program_id(0)
        base = pl.multiple_of(lax.shift_right_logical(start, 1) * LANES, LANES)
        width = WIN_H * GRID_W
        o_ref[0] = jnp.where((start & 1) == 0, even_ref[:, :, pl.ds(base, width)], odd_ref[:, :, pl.ds(base, width)])

    full = pl.BlockSpec(even.shape, lambda v: (0, 0, 0))
    return pl.pallas_call(
        window_kernel,
        grid=(WIN_H,),
        in_specs=[full, full],
        out_specs=pl.BlockSpec((1, N_HEADS // 2, 2 * GRID_W, WIN_H * GRID_W), lambda v: (v, 0, 0, 0)),
        out_shape=jax.ShapeDtypeStruct((WIN_H, N_HEADS // 2, 2 * GRID_W, WIN_H * GRID_W), jnp.float32),
        compiler_params=pltpu.CompilerParams(dimension_semantics=("arbitrary",), vmem_limit_bytes=VMEM_LIMIT),
        name="bias_table",
    )(even, odd)


def _mix_kernel(h_ref, mod_ref,
                q_ref, kp_ref, kc_ref, kn_ref, vp_ref, vc_ref, vn_ref,
                pp_ref, pc_ref, pn_ref, ga_ref, gb_ref,
                kvc_ref, bias_ref, wgrp_ref, pscale_ref, wap_ref, wpp_ref, wout_ref,
                ln1g_ref, ln1b_ref,
                o_ref,
                kbuf, vbuf, yabuf, pbuf, ypbuf, zbuf):
    b = pl.program_id(0)
    nb = pl.num_programs(0)

    kbuf[0:KV_HALO, :] = kp_ref[...]
    kbuf[KV_HALO:KV_HALO + MIX_TQ, :] = kc_ref[...]
    kbuf[KV_HALO + MIX_TQ:, :] = kn_ref[...]
    vbuf[0:KV_HALO, :] = vp_ref[...]
    vbuf[KV_HALO:KV_HALO + MIX_TQ, :] = vc_ref[...]
    vbuf[KV_HALO + MIX_TQ:, :] = vn_ref[...]

    lane = lax.broadcasted_iota(jnp.int32, (GRID_W, LANES), 1)
    first_head = lane < HEAD_DIM

    units = [(j, pair) for j in range(MIX_ROWS) for pair in range(N_HEADS // 2)]
    nt = (((1,), (1,)), ((), ()))

    def window(j):
        r = b * MIX_ROWS + j
        rs = jnp.clip(r - WIN_H // 2, 0, ROWS - WIN_H)
        return pl.multiple_of((rs - b * MIX_ROWS + WIN_H // 2) * GRID_W, GRID_W), r - rs

    def scores(j, pair):
        off, var = window(j)
        cols = slice(pair * LANES, (pair + 1) * LANES)
        q = q_ref[j * GRID_W:(j + 1) * GRID_W, cols]
        zero = jnp.zeros_like(q)
        q2 = jnp.concatenate([jnp.where(first_head, q, zero), jnp.where(first_head, zero, q)], axis=0)
        kw = kbuf[pl.ds(off, WIN_H * GRID_W), cols]
        s_loc = lax.dot_general(q2, kw, nt, preferred_element_type=jnp.float32) + bias_ref[var, pair]
        s_ctx = lax.dot_general(q2, kvc_ref[:, cols], nt, preferred_element_type=jnp.float32)
        return s_loc, s_ctx

    def values(j, pair, s_loc, s_ctx):
        off, _ = window(j)
        cols = slice(pair * LANES, (pair + 1) * LANES)
        vw = vbuf[pl.ds(off, WIN_H * GRID_W), cols]
        vctx = kvc_ref[:, ATT_W + pair * LANES:ATT_W + (pair + 1) * LANES]
        m = jnp.maximum(jnp.max(s_loc, axis=-1, keepdims=True), jnp.max(s_ctx, axis=-1, keepdims=True))
        p_loc = jnp.exp(s_loc - m).astype(jnp.bfloat16)
        p_ctx = jnp.exp(s_ctx - m).astype(jnp.bfloat16)
        ones = lambda rows: jnp.ones((rows, LANES), jnp.bfloat16)
        o2 = (_bdot(p_loc, jnp.concatenate([vw, ones(WIN_H * GRID_W)], axis=1))
              + _bdot(p_ctx, jnp.concatenate([vctx, ones(CTX_LEN)], axis=1)))
        o2 = o2[:, :LANES] / o2[:, LANES:]
        o_pair = jnp.where(first_head, o2[:GRID_W], o2[GRID_W:])
        yabuf[j * GRID_W:(j + 1) * GRID_W, cols] = o_pair.astype(jnp.bfloat16)

    pbuf[0:POOL_HALO, :] = jnp.where(b > 0, pp_ref[...].astype(jnp.float32), 0.0)
    pbuf[POOL_HALO:POOL_HALO + MIX_TQ, :] = pc_ref[...].astype(jnp.float32)
    pbuf[POOL_HALO + MIX_TQ:, :] = jnp.where(b < nb - 1, pn_ref[...].astype(jnp.float32), 0.0)
    edge = lax.broadcasted_iota(jnp.int32, (ROW_TILE, 1), 0)

    def pool_group(g):
        win = POOL_WINDOWS[g]
        lo, hi = win // 2, win - win // 2
        cols = slice(g * POOL_DIM, (g + 1) * POOL_DIM)
        acc = None
        for d in range(-lo, hi):
            term = pbuf[POOL_HALO + d:POOL_HALO + d + MIX_TQ, cols]
            acc = term if acc is None else acc + term
        assert max(lo, hi) <= ROW_TILE
        top = jnp.where(b == 0, 1.0 / (win - jnp.maximum(lo - edge, 0)).astype(jnp.float32), 1.0 / win)
        bot = jnp.where(b == nb - 1,
                        1.0 / (win - jnp.maximum(edge + hi - ROW_TILE, 0)).astype(jnp.float32), 1.0 / win)
        inv = jnp.concatenate([top, jnp.full((MIX_TQ - 2 * ROW_TILE, 1), 1.0 / win, jnp.float32), bot], axis=0)
        pooled = acc * inv - pbuf[POOL_HALO:POOL_HALO + MIX_TQ, cols]
        yp = _bdot(pooled.astype(jnp.bfloat16), wgrp_ref[g]) * pscale_ref[:, cols]
        ypbuf[:, cols] = yp.astype(jnp.bfloat16)

    def pooled_branch():
        zbuf[...] = gb_ref[...].astype(jnp.float32) * _bdot(ypbuf[...], wpp_ref[...])

    extra = {}
    for g in range(POOL_GROUPS):
        extra[(g + 1) * len(units) // (POOL_GROUPS + 2)] = functools.partial(pool_group, g)
    extra[(POOL_GROUPS + 1) * len(units) // (POOL_GROUPS + 2)] = pooled_branch
    pending = [scores(*u) for u in units[:ATTN_AHEAD]]
    for n, u in enumerate(units):
        if n + ATTN_AHEAD < len(units):
            pending.append(scores(*units[n + ATTN_AHEAD]))
        values(*u, *pending.pop(0))
        if n in extra:
            extra[n]()

    g1 = mod_ref[0:1, 2 * D_MODEL:3 * D_MODEL]
    n_chunks = MIX_TQ // MERGE_ROWS
    rows = lambda c: slice(c * MERGE_ROWS, (c + 1) * MERGE_ROWS)
    z, y = {}, {}

    def stage_a(c):
        ya = _bdot(yabuf[rows(c), :], wap_ref[...])
        z[c] = (ga_ref[rows(c), :].astype(jnp.float32) * ya + zbuf[rows(c), :]).astype(jnp.bfloat16)

    def stage_b(c):
        y[c] = _bdot(z.pop(c), wout_ref[...])

    def stage_c(c):
        hn = _layer_norm(DEEPNORM_ALPHA * h_ref[rows(c), :] + g1 * y.pop(c), ln1g_ref[...], ln1b_ref[...])
        for j in range(ROW_TILE):
            o_ref[pl.ds(c * MERGE_ROWS * ROW_TILE + j, MERGE_ROWS, stride=ROW_TILE), :] = hn[:, j * LANES:(j + 1) * LANES]

    for t in range(n_chunks + 2):
        if t < n_chunks:
            stage_a(t)
        if 0 <= t - 1 < n_chunks:
            stage_b(t - 1)
        if 0 <= t - 2 < n_chunks:
            stage_c(t - 2)


def _mix_call(h, mod, u, kvc, bias, wgrp, pscale, wap, wpp, wout, ln1g, ln1b):
    nb = SEQ // MIX_TQ
    halo_per_blk = MIX_TQ // KV_HALO
    n_halo = SEQ // KV_HALO
    ph_per_blk = MIX_TQ // POOL_HALO
    n_ph = SEQ // POOL_HALO

    def const(shape):
        return pl.BlockSpec(shape, lambda i: (0,) * len(shape), pipeline_mode=pl.Buffered(1))

    def prev_halo(c):
        return pl.BlockSpec((KV_HALO, ATT_W), lambda i: (jnp.maximum(i * halo_per_blk - 1, 0), c))

    def next_halo(c):
        return pl.BlockSpec((KV_HALO, ATT_W), lambda i: (jnp.minimum((i + 1) * halo_per_blk, n_halo - 1), c))

    def cur(c):
        return pl.BlockSpec((MIX_TQ, ATT_W), lambda i: (i, c))

    in_specs = [
        pl.BlockSpec((MIX_TQ, D_MODEL), lambda i: (i, 0)),
        const(mod.shape),
        cur(0),
        prev_halo(1), cur(1), next_halo(1),
        prev_halo(2), cur(2), next_halo(2),
        pl.BlockSpec((POOL_HALO, POOL_W), lambda i: (jnp.maximum(i * ph_per_blk - 1, 0), 3)),
        cur(3),
        pl.BlockSpec((POOL_HALO, POOL_W), lambda i: (jnp.minimum((i + 1) * ph_per_blk, n_ph - 1), 3)),
        pl.BlockSpec((MIX_TQ, D_MODEL), lambda i: (i, 2)),
        pl.BlockSpec((MIX_TQ, D_MODEL), lambda i: (i, 3)),
        const(kvc.shape), const(bias.shape), const(wgrp.shape), const(pscale.shape),
        const(wap.shape), const(wpp.shape), const(wout.shape),
        const((1, D_MODEL)), const((1, D_MODEL)),
    ]
    return pl.pallas_call(
        _mix_kernel,
        grid=(nb,),
        in_specs=in_specs,
        out_specs=pl.BlockSpec((MIX_TQ * ROW_TILE, LANES), lambda i: (i, 0)),
        out_shape=jax.ShapeDtypeStruct((SEQ * ROW_TILE, LANES), jnp.float32),
        scratch_shapes=[
            pltpu.VMEM((MIX_TQ + 2 * KV_HALO, ATT_W), jnp.bfloat16),
            pltpu.VMEM((MIX_TQ + 2 * KV_HALO, ATT_W), jnp.bfloat16),
            pltpu.VMEM((MIX_TQ, ATT_W), jnp.bfloat16),
            pltpu.VMEM((MIX_TQ + 2 * POOL_HALO, POOL_W), jnp.float32),
            pltpu.VMEM((MIX_TQ, POOL_W), jnp.bfloat16),
            pltpu.VMEM((MIX_TQ, D_MODEL), jnp.float32),
        ],
        compiler_params=pltpu.CompilerParams(
            dimension_semantics=("arbitrary",), vmem_limit_bytes=VMEM_LIMIT),
        name="mix",
    )(h, mod, u, u, u, u, u, u, u, u, u, u, u, u,
      kvc, bias, wgrp, pscale, wap, wpp, wout, ln1g, ln1b)


ID_E0, ID_E1, ID_R0, ID_R1 = 0, 1, 4, 5
POS_PACKED = 0
POS_BITS = 16
PLAN_EXPERT, PLAN_VALID, PLAN_NACT, PLAN_NEXT = 0, 1, 2, 3
PLAN_W = 2 * LANES
CW_C0, CW_C1 = 0, 1


def _route_kernel(h_ref, mod_ref, wrt_ref, brt_ref, pos_ref, cw_ref, plan_ref, carry_ref, ids_all):
    i = pl.program_id(0)
    tm = ROUTE_TM

    @pl.when(i == 0)
    def _():
        carry_ref[...] = jnp.zeros_like(carry_ref)

    shift = mod_ref[0:1, 3 * D_MODEL:4 * D_MODEL]
    scale = mod_ref[0:1, 4 * D_MODEL:5 * D_MODEL]
    hm = _load_row_tiles(h_ref, tm) * (1.0 + scale) + shift

    hm_hi, hm_lo = _split_bf16(hm)
    w_hi, w_lo = _split_bf16(wrt_ref[...])
    nt = (((1,), (1,)), ((), ()))
    dg = functools.partial(lax.dot_general, dimension_numbers=nt, preferred_element_type=jnp.float32)
    logits = dg(w_hi, hm_hi) + (dg(w_hi, hm_lo) + dg(w_lo, hm_hi)) + brt_ref[:, 0:1]

    sub = lax.broadcasted_iota(jnp.int32, (ROUTE_LOGIT_ROWS, tm), 0)
    big = jnp.int32(1 << 20)
    is_grp = sub < N_GROUPS
    gl = jnp.where(is_grp, logits, -jnp.inf)
    gmax = jnp.max(gl, axis=0, keepdims=True)
    gidx = jnp.min(jnp.where(gl == gmax, sub, big), axis=0, keepdims=True)
    gsum = jnp.sum(jnp.where(is_grp, jnp.exp(logits - gmax), 0.0), axis=0, keepdims=True)
    p_group = 1.0 / gsum

    eid = sub - N_GROUPS
    sel = (eid >= 0) & (eid < N_EXPERTS) & (lax.shift_right_arithmetic(eid, 3) == gidx)
    el = jnp.where(sel, logits, -jnp.inf)
    l0 = jnp.max(el, axis=0, keepdims=True)
    i0 = jnp.min(jnp.where(el == l0, sub, big), axis=0, keepdims=True)
    el2 = jnp.where(sub == i0, -jnp.inf, el)
    l1 = jnp.max(el2, axis=0, keepdims=True)
    i1 = jnp.min(jnp.where(el2 == l1, sub, big), axis=0, keepdims=True)
    t = jnp.exp(l1 - l0)
    w0 = 1.0 / (1.0 + t)
    w1 = t / (1.0 + t)

    half_rows = jnp.where(i >= pl.num_programs(0) // SEQ_PARTS, N_EXPERTS, 0)
    i0 = i0 + half_rows
    i1 = i1 + half_rows
    subs = lax.broadcasted_iota(jnp.int32, (ROUTE_SEG_ROWS, tm), 0)
    onehot = jnp.where((subs == i0) | (subs == i1), 1.0, 0.0)
    rr = lax.broadcasted_iota(jnp.int32, (tm, tm), 0)
    cc = lax.broadcasted_iota(jnp.int32, (tm, tm), 1)
    earlier = jnp.where(rr < cc, 1.0, 0.0).astype(jnp.bfloat16)
    carry = carry_ref[:, 0:1]
    prefix = _bdot(onehot.astype(jnp.bfloat16), earlier) + carry
    r0 = jnp.sum(jnp.where(subs == i0, prefix, 0.0), axis=0, keepdims=True)
    r1 = jnp.sum(jnp.where(subs == i1, prefix, 0.0), axis=0, keepdims=True)
    total = jnp.broadcast_to(carry + jnp.sum(onehot, axis=1, keepdims=True), carry_ref.shape)
    carry_ref[...] = total

    sub8 = lax.broadcasted_iota(jnp.int32, (ROW_TILE, tm), 0)
    ids = jnp.zeros((ROW_TILE, tm), jnp.int32)
    for idx, val in ((ID_E0, i0 - N_GROUPS), (ID_E1, i1 - N_GROUPS),
                     (ID_R0, r0.astype(jnp.int32)), (ID_R1, r1.astype(jnp.int32))):
        ids = jnp.where(sub8 == idx, val, ids)
    ids_all[:, pl.ds(pl.multiple_of(i * tm, tm), tm)] = ids

    cw8 = jnp.where(sub8 == CW_C0, p_group * w0, jnp.where(sub8 == CW_C1, p_group * w1, 0.0))
    cw_ref[...] = jnp.concatenate([cw8, jnp.zeros((LANES - ROW_TILE, tm), jnp.float32)], axis=0).T

    @pl.when(i == pl.num_programs(0) - 1)
    def _():
        subq = lax.broadcasted_iota(jnp.int32, (LANES, LANES), 0)
        laneq = lax.broadcasted_iota(jnp.int32, (LANES, LANES), 1)
        total = jnp.concatenate([carry_ref[...], jnp.zeros((LANES - ROUTE_SEG_ROWS, LANES), jnp.float32)], axis=0)
        cnt = total.astype(jnp.int32)
        tiles = lax.shift_right_logical(cnt + (EXP_TM - 1), EXP_TM.bit_length() - 1).astype(jnp.float32)
        incl = jnp.where(laneq <= subq, 1.0, 0.0).astype(jnp.bfloat16)
        tile_end = _bdot(incl, tiles.astype(jnp.bfloat16))
        tile_start = tile_end - tiles
        seg = (tile_start * EXP_TM).astype(jnp.int32)
        nact = jnp.max(tile_end, axis=0, keepdims=True)

        ids_full = ids_all[...]
        look = jnp.zeros_like(ids_full)
        for e in range(N_SEG):
            look = jnp.where(ids_full == e, seg[N_GROUPS + e, 0], look)
        pos01 = look + pltpu.roll(ids_full, ID_R0 - ID_E0, axis=0)
        assert EXP_TILES * EXP_TM <= 1 << POS_BITS
        pos_ref[...] = pos01 | (pltpu.roll(pos01, ROW_TILE - 1, axis=0) << POS_BITS)

        subp = lax.broadcasted_iota(jnp.int32, (LANES, PLAN_W), 0)
        tile = lax.broadcasted_iota(jnp.int32, (LANES, PLAN_W), 1).astype(jnp.float32)
        is_exp = (subp >= N_GROUPS) & (subp < N_GROUPS + N_SEG)
        end_col = tile_end[:, 0:1]
        nact_s = nact[:, 0:1]
        te = jnp.sum(jnp.where(is_exp & (tile >= end_col), 1.0, 0.0), axis=0, keepdims=True)
        te_last = jnp.sum(jnp.where(is_exp & (nact_s - 1.0 >= end_col), 1.0, 0.0), axis=0, keepdims=True)[:, 0:1]
        tile_row = tile[0:1, :]
        te = jnp.minimum(jnp.where(tile_row < nact_s, te, te_last), N_SEG - 1.0)
        mine = (subp - N_GROUPS).astype(jnp.float32) == te
        cnt_sel = jnp.sum(jnp.where(mine, total[:, 0:1], 0.0), axis=0, keepdims=True)
        start_sel = jnp.sum(jnp.where(mine, tile_start[:, 0:1], 0.0), axis=0, keepdims=True)
        end_sel = jnp.sum(jnp.where(mine, end_col, 0.0), axis=0, keepdims=True)
        valid = jnp.clip(cnt_sel - (tile_row - start_sel) * EXP_TM, 0.0, float(EXP_TM))
        valid = jnp.where(tile_row < nact_s, valid, 0.0)
        subr = lax.broadcasted_iota(jnp.int32, (ROW_TILE, PLAN_W), 0)
        plan = jnp.where(subr == PLAN_EXPERT, te, jnp.where(subr == PLAN_VALID, valid,
                         jnp.where(subr == PLAN_NACT, nact_s, jnp.where(subr == PLAN_NEXT, end_sel, 0.0))))
        plan_ref[...] = plan.astype(jnp.int32)


def _route_call(h, mod, wrt, brt):
    tm = ROUTE_TM
    return pl.pallas_call(
        _route_kernel,
        grid=(SEQ // tm,),
        in_specs=[
            pl.BlockSpec((tm * ROW_TILE, LANES), lambda i: (i, 0)),
            pl.BlockSpec(mod.shape, lambda i: (0, 0)),
            pl.BlockSpec((ROUTE_LOGIT_ROWS, D_MODEL), lambda i: (0, 0)),
            pl.BlockSpec((ROUTE_LOGIT_ROWS, LANES), lambda i: (0, 0)),
        ],
        out_specs=[
            pl.BlockSpec((ROW_TILE, SEQ), lambda i: (0, 0)),
            pl.BlockSpec((tm, LANES), lambda i: (i, 0)),
            pl.BlockSpec((ROW_TILE, PLAN_W), lambda i: (0, 0)),
        ],
        out_shape=[
            jax.ShapeDtypeStruct((ROW_TILE, SEQ), jnp.int32),
            jax.ShapeDtypeStruct((SEQ, LANES), jnp.float32),
            jax.ShapeDtypeStruct((ROW_TILE, PLAN_W), jnp.int32),
        ],
        scratch_shapes=[pltpu.VMEM((ROUTE_SEG_ROWS, LANES), jnp.float32),
                        pltpu.VMEM((ROW_TILE, SEQ), jnp.int32)],
        compiler_params=pltpu.CompilerParams(
            dimension_semantics=("arbitrary",), vmem_limit_bytes=VMEM_LIMIT),
        name="route",
    )(h, mod, wrt, brt)


SRC_UNROLL = 8
EXP_CHUNK = 256
PREP_AFTER_DOWN = 1
TILE_ROWS = EXP_TM * ROW_TILE


def _experts_kernel(te_ref, tv_ref, nact_ref, tnext_ref, pos_ref,
                    h_hbm, mod_ref, wg_hbm, wu_hbm, wd_hbm,
                    y_ref,
                    src_ref, stage_ref, hres, xbuf, xmat, wgs, wus, wds, wgb, wub, wdb, rsem, wsem):
    i = pl.program_id(0)
    last = pl.num_programs(0) - 1
    nact = nact_ref[0]
    xcur = lax.rem(i, 2)
    part = lax.shift_right_logical(te_ref[i], N_EXPERTS.bit_length() - 1)

    def weight_copies(segment, st):
        e = segment & (N_EXPERTS - 1)
        return [pltpu.make_async_copy(w_hbm.at[e], stage.at[st], wsem.at[st, n])
                for n, (w_hbm, stage) in enumerate(((wg_hbm, wgs), (wu_hbm, wus), (wd_hbm, wds)))]

    def tile_after(t):
        return tnext_ref[jnp.minimum(t, last)]

    def gather_row(tile, k, s):
        local = (src_ref[tile * EXP_TM + k] - part * PART_TOKENS) & (PART_TOKENS - 1)
        xbuf[s, k * ROW_TILE:(k + 1) * ROW_TILE, :] = hres[pl.ds(pl.multiple_of(local * ROW_TILE, ROW_TILE),
                                                             ROW_TILE), :]

    def gather_items(tile, s):
        return [functools.partial(gather_row, tile, k, s) for k in range(EXP_TM)]

    def prepare_input(xs):
        x = _load_row_tiles(xbuf, EXP_TM, lead=(xs,))
        shift = mod_ref[0:1, 3 * D_MODEL:4 * D_MODEL]
        scale = mod_ref[0:1, 4 * D_MODEL:5 * D_MODEL]
        xmat[xs] = (x * (1.0 + scale) + shift).astype(jnp.bfloat16)

    def compute_chunks(xs):
        state = {"act": []}

        def gate(c):
            def run():
                state["a"] = _bdot(xmat[xs], wgb[:, c * EXP_CHUNK:(c + 1) * EXP_CHUNK])
            return run

        def up(c):
            def run():
                a = state["a"]
                u = _bdot(xmat[xs], wub[:, c * EXP_CHUNK:(c + 1) * EXP_CHUNK])
                state["act"].append((a * jax.nn.sigmoid(a) * u).astype(jnp.bfloat16))
            return run

        def down(c):
            def run():
                if c == 0:
                    state["actf"] = jnp.concatenate(state["act"], axis=-1)
                yc = _bdot(state["actf"], wdb[:, c * EXP_CHUNK:(c + 1) * EXP_CHUNK])
                for jj in range(EXP_CHUNK // LANES):
                    j = c * (EXP_CHUNK // LANES) + jj
                    y_ref[pl.ds(j, EXP_TM, stride=ROW_TILE), :] = yc[:, jj * LANES:(jj + 1) * LANES]
            return run

        first = []
        for c in range(D_EXPERT // EXP_CHUNK):
            first += [gate(c), up(c)]
        return first, [down(c) for c in range(D_MODEL // EXP_CHUNK)]

    def resident_copy(p):
        rows = PART_TOKENS * ROW_TILE
        return pltpu.make_async_copy(h_hbm.at[pl.ds(pl.multiple_of(p * rows, rows), rows), :], hres, rsem.at[0])

    @pl.when(i == 0)
    def _():
        resident_copy(part).start()
        stage_ref[0] = 0
        for cp in weight_copies(te_ref[0], 0):
            cp.start()
        second = tile_after(0)

        @pl.when(second < nact)
        def _():
            for cp in weight_copies(te_ref[jnp.minimum(second, last)], 1):
                cp.start()

        def fill_body(tt, c):
            ts = [tt * SRC_UNROLL + u for u in range(SRC_UNROLL)]
            words = [pos_ref[t] for t in ts]
            for t, w in zip(ts, words):
                src_ref[w & ((1 << POS_BITS) - 1)] = t
                src_ref[lax.shift_right_logical(w, POS_BITS)] = t
            return c
        lax.fori_loop(0, SEQ // SRC_UNROLL, fill_body, 0)

        def pad_tile(t, c):
            pad_tok = lax.shift_right_logical(te_ref[t], N_EXPERTS.bit_length() - 1) * PART_TOKENS

            def pad_row(k, c2):
                src_ref[t * EXP_TM + k] = pad_tok
                return c2
            return lax.fori_loop(tv_ref[t], EXP_TM, pad_row, c)
        lax.fori_loop(0, nact, pad_tile, 0)

    active = i < nact
    prev = jnp.maximum(i - 1, 0)
    new_segment = (i == 0) | (te_ref[i] != te_ref[prev])

    @pl.when(active & ((i == 0) | (part != lax.shift_right_logical(te_ref[prev], N_EXPERTS.bit_length() - 1))))
    def _():
        @pl.when(i > 0)
        def _():
            resident_copy(part).start()
        resident_copy(part).wait()
        for item in gather_items(i, xcur):
            item()
        prepare_input(xcur)

    @pl.when(active & new_segment)
    def _():
        st = stage_ref[0]
        for cp in weight_copies(te_ref[i], st):
            cp.wait()
        wgb[...] = wgs[st].astype(jnp.bfloat16)
        wub[...] = wus[st].astype(jnp.bfloat16)
        wdb[...] = wds[st].astype(jnp.bfloat16)
        stage_ref[0] = 1 - st
        nxt = tile_after(i)
        nxt2 = tile_after(nxt)

        @pl.when((nxt < nact) & (nxt2 < nact))
        def _():
            for cp in weight_copies(te_ref[jnp.minimum(nxt2, last)], st):
                cp.start()

    @pl.when(active)
    def _():
        first, second = compute_chunks(xcur)
        items = gather_items(jnp.minimum(i + 1, nact - 1), 1 - xcur)
        per = -(-len(items) // len(first))
        for n, chunk in enumerate(first):
            for item in items[n * per:(n + 1) * per]:
                item()
            chunk()
        for n, chunk in enumerate(second):
            chunk()
            if n == PREP_AFTER_DOWN:
                prepare_input(1 - xcur)

    @pl.when(jnp.logical_not(active))
    def _():
        y_ref[...] = jnp.zeros_like(y_ref)


def _experts_call(te, tv, nact, tnext, pos, h, mod, wg, wu, wd):
    grid_spec = pltpu.PrefetchScalarGridSpec(
        num_scalar_prefetch=5,
        grid=(EXP_TILES,),
        in_specs=[
            pl.BlockSpec(memory_space=pl.ANY),
            pl.BlockSpec(mod.shape, lambda i, *_: (0, 0)),
            pl.BlockSpec(memory_space=pl.ANY),
            pl.BlockSpec(memory_space=pl.ANY),
            pl.BlockSpec(memory_space=pl.ANY),
        ],
        out_specs=pl.BlockSpec((TILE_ROWS, LANES), lambda i, *_: (i, 0)),
        scratch_shapes=[
            pltpu.SMEM((EXP_TILES * EXP_TM,), jnp.int32),
            pltpu.SMEM((1,), jnp.int32),
            pltpu.VMEM((PART_TOKENS * ROW_TILE, LANES), jnp.float32),
            pltpu.VMEM((2, TILE_ROWS, LANES), jnp.float32),
            pltpu.VMEM((2, EXP_TM, D_MODEL), jnp.bfloat16),
            pltpu.VMEM((2, D_MODEL, D_EXPERT), jnp.float32),
            pltpu.VMEM((2, D_MODEL, D_EXPERT), jnp.float32),
            pltpu.VMEM((2, D_EXPERT, D_MODEL), jnp.float32),
            pltpu.VMEM((D_MODEL, D_EXPERT), jnp.bfloat16),
            pltpu.VMEM((D_MODEL, D_EXPERT), jnp.bfloat16),
            pltpu.VMEM((D_EXPERT, D_MODEL), jnp.bfloat16),
            pltpu.SemaphoreType.DMA((1,)),
            pltpu.SemaphoreType.DMA((2, 3)),
        ],
    )
    return pl.pallas_call(
        _experts_kernel,
        grid_spec=grid_spec,
        out_shape=jax.ShapeDtypeStruct((EXP_TILES * TILE_ROWS, LANES), jnp.float32),
        compiler_params=pltpu.CompilerParams(
            dimension_semantics=("arbitrary",), vmem_limit_bytes=EXPERTS_VMEM_LIMIT),
        name="experts",
    )(te, tv, nact, tnext, pos, h, mod, wg, wu, wd)


def _combine_kernel(pos_ref, h_ref, cw_ref, mod_ref, g_ref, b_ref, ys_hbm, o_ref, ybuf, sem):
    i = pl.program_id(0)
    tm = CMB_TM
    slot = lax.rem(i, 2)

    def start_row(tile, k, s):
        word = pos_ref[tile * tm + k]
        for half, p in ((0, word & ((1 << POS_BITS) - 1)), (1, lax.shift_right_logical(word, POS_BITS))):
            pltpu.make_async_copy(ys_hbm.at[pl.ds(pl.multiple_of(p * ROW_TILE, ROW_TILE), ROW_TILE), :],
                                  ybuf.at[s, half, pl.ds(pl.multiple_of(k * ROW_TILE, ROW_TILE), ROW_TILE), :],
                                  sem.at[s]).start(priority=half)

    @pl.when(i == 0)
    def _():
        def body(kk, c):
            for u in range(SRC_UNROLL):
                start_row(0, kk * SRC_UNROLL + u, 0)
            return c
        lax.fori_loop(0, tm // SRC_UNROLL, body, 0)

    for half in range(2):
        pltpu.make_async_copy(ys_hbm.at[pl.ds(0, tm * ROW_TILE), :], ybuf.at[slot, half], sem.at[slot]).wait()

    g2 = mod_ref[0:1, 5 * D_MODEL:6 * D_MODEL]

    def chunk(c):
        rows = slice(c * CMB_CHUNK, (c + 1) * CMB_CHUNK)
        y0 = _load_row_tiles(ybuf, CMB_CHUNK, lead=(slot, 0), first=c * CMB_CHUNK)
        y1 = _load_row_tiles(ybuf, CMB_CHUNK, lead=(slot, 1), first=c * CMB_CHUNK)
        ffn = cw_ref[rows, CW_C0:CW_C0 + 1] * y0 + cw_ref[rows, CW_C1:CW_C1 + 1] * y1
        h = _load_row_tiles(h_ref, CMB_CHUNK, first=c * CMB_CHUNK)
        o_ref[rows, :] = _layer_norm(DEEPNORM_ALPHA * h + g2 * ffn, g_ref[...], b_ref[...])

    n_chunks = tm // CMB_CHUNK

    @pl.when(i + 1 < pl.num_programs(0))
    def _():
        per = tm // n_chunks
        for c in range(n_chunks):
            for k in range(c * per, (c + 1) * per):
                start_row(i + 1, k, 1 - slot)
            chunk(c)

    @pl.when(i + 1 == pl.num_programs(0))
    def _():
        for c in range(n_chunks):
            chunk(c)


def _combine_call(pos, h, ys, cw, mod, g, b):
    tm = CMB_TM
    grid_spec = pltpu.PrefetchScalarGridSpec(
        num_scalar_prefetch=1,
        grid=(SEQ // tm,),
        in_specs=[
            pl.BlockSpec((tm * ROW_TILE, LANES), lambda i, *_: (i, 0)),
            pl.BlockSpec((tm, LANES), lambda i, *_: (i, 0)),
            pl.BlockSpec(mod.shape, lambda i, *_: (0, 0)),
            pl.BlockSpec((1, D_MODEL), lambda i, *_: (0, 0)),
            pl.BlockSpec((1, D_MODEL), lambda i, *_: (0, 0)),
            pl.BlockSpec(memory_space=pl.ANY),
        ],
        out_specs=pl.BlockSpec((tm, D_MODEL), lambda i, *_: (i, 0)),
        scratch_shapes=[
            pltpu.VMEM((2, 2, tm * ROW_TILE, LANES), jnp.float32),
            pltpu.SemaphoreType.DMA((2,)),
        ],
    )
    return pl.pallas_call(
        _combine_kernel,
        grid_spec=grid_spec,
        out_shape=jax.ShapeDtypeStruct((SEQ, D_MODEL), jnp.float32),
        compiler_params=pltpu.CompilerParams(
            dimension_semantics=("arbitrary",), vmem_limit_bytes=VMEM_LIMIT),
        name="combine",
    )(pos, h, cw, mod, g, b, ys)


def kernel(x, c, ctx, c_ctx, ln_in_g, ln_in_b, w_mod, b_mod, w_in, rpb, w_pool_grp, pool_scale,
           w_attn_proj, w_pool_proj, w_out, ln1_g, ln1_b, w_router_group, b_router_group,
           w_router_expert, b_router_expert, w_expert_gate, w_expert_up, w_expert_down, ln2_g, ln2_b):
    assert x.shape == (1, SEQ, D_MODEL) and ctx.shape == (1, CTX_LEN, D_MODEL)
    assert w_mod.shape[0] == 1, "single-layer trunk"
    f32, bf16 = jnp.float32, jnp.bfloat16
    row = lambda v: v.reshape(1, -1).astype(f32)

    cond = jnp.concatenate([c, c_ctx[None], jnp.zeros((MOD_ROWS - 2, D_MODEL), f32)], axis=0)
    mod = _mod_call(cond, w_mod[0], row(b_mod[0]))

    lng, lnb = row(ln_in_g), row(ln_in_b)
    w_in_b = w_in[0].astype(bf16)
    u, h0 = _proj_call(x[0], mod, lng, lnb, w_in_b, mod_row=0, latent=True, tm=PROJ_TM)
    kvc, = _proj_call(ctx[0], mod, lng, lnb, w_in_b[:, ATT_W:3 * ATT_W], mod_row=1, latent=False, tm=CTX_LEN)

    h1 = _mix_call(h0, mod, u, kvc, _attn_bias_table(rpb[0]),
                   w_pool_grp[0].astype(bf16), row(pool_scale[0]),
                   w_attn_proj[0].astype(bf16), w_pool_proj[0].astype(bf16), w_out[0].astype(bf16),
                   row(ln1_g[0]), row(ln1_b[0]))

    n_logit = N_GROUPS + N_EXPERTS
    wrt = jnp.concatenate([w_router_group[0].T, w_router_expert[0].T,
                           jnp.zeros((ROUTE_LOGIT_ROWS - n_logit, D_MODEL), f32)], axis=0)
    brt = jnp.concatenate([b_router_group[0], b_router_expert[0], jnp.zeros((ROUTE_LOGIT_ROWS - n_logit,), f32)])
    brt = jnp.broadcast_to(brt[:, None], (ROUTE_LOGIT_ROWS, LANES))
    pos, cw, plan = _route_call(h1, mod, wrt, brt)

    posw = pos[POS_PACKED]
    y = _experts_call(plan[PLAN_EXPERT, :EXP_TILES], plan[PLAN_VALID, :EXP_TILES], plan[PLAN_NACT, :1],
                      plan[PLAN_NEXT, :EXP_TILES], posw, h1, mod,
                      w_expert_gate[0], w_expert_up[0], w_expert_down[0])
    out = _combine_call(posw, h1, y, cw, mod, row(ln2_g[0]), row(ln2_b[0]))
    return out[None]
```

```python
import functools

import jax
import jax.numpy as jnp
from jax import lax
from jax.experimental import pallas as pl
from jax.experimental.pallas import tpu as pltpu

D_MODEL = 1024
SEQ = 16384
GRID_W = 64
ROWS = SEQ // GRID_W
CTX_LEN = 256
N_HEADS = 8
HEAD_DIM = 64
ATT_W = N_HEADS * HEAD_DIM
WIN_H = 8
WIN_W = 16
POOL_WINDOWS = (2, 4, 8, 16)
POOL_GROUPS = 4
POOL_DIM = 128
POOL_W = POOL_GROUPS * POOL_DIM
PROJ_W = 3 * ATT_W + POOL_W + 2 * D_MODEL
GATE_COL = 3 * ATT_W + POOL_W
N_GROUPS = 4
EXPERTS_PER_GROUP = 8
N_EXPERTS = N_GROUPS * EXPERTS_PER_GROUP
D_EXPERT = 512
N_MOD = 6
DEEPNORM_ALPHA = 2.0 ** 0.25
LN_EPS = 1e-5
NEG_INF = -1e30

LANES = 128
ROW_TILE = 8
MOD_ROWS = 8
PROJ_TM = 512
PROJ_SUB = 256
MIX_ROWS = 8
MIX_TQ = MIX_ROWS * GRID_W
KV_HALO = 4 * GRID_W
POOL_HALO = 16
ROUTE_TM = 512
ROUTE_LOGIT_ROWS = 40
ROUTE_SEG_ROWS = 72
EXP_TM = 256
SEQ_PARTS = 2
PART_TOKENS = SEQ // SEQ_PARTS
N_SEG = SEQ_PARTS * N_EXPERTS
EXP_TILES = 2 * SEQ // EXP_TM + N_SEG
CMB_TM = 512
CMB_CHUNK = 128
MERGE_ROWS = 256
BIAS_LANES = 1024
ATTN_AHEAD = 3
HALF = D_MODEL // 2
VMEM_LIMIT = 56 * 1024 * 1024
EXPERTS_VMEM_LIMIT = 60 * 1024 * 1024


def _layer_norm(x, g, b):
    mu = jnp.mean(x, axis=-1, keepdims=True)
    xc = x - mu
    var = jnp.mean(xc * xc, axis=-1, keepdims=True)
    return xc * lax.rsqrt(var + LN_EPS) * g + b


def _bdot(a, b):
    return jnp.dot(a, b, preferred_element_type=jnp.float32)


def _split_bf16(a):
    hi = a.astype(jnp.bfloat16)
    lo = (a - hi.astype(jnp.float32)).astype(jnp.bfloat16)
    return hi, lo


def _dot3(a, b):
    a_hi, a_lo = _split_bf16(a)
    b_hi, b_lo = _split_bf16(b)
    return _bdot(a_hi, b_hi) + (_bdot(a_hi, b_lo) + _bdot(a_lo, b_hi))


def _load_row_tiles(ref, tokens, lead=(), first=0):
    parts = [ref[(*lead, pl.ds(first * ROW_TILE + j, tokens, stride=ROW_TILE), slice(None))]
             for j in range(ROW_TILE)]
    return jnp.concatenate(parts, axis=-1)


def _store_row_tiles(ref, value, lead=()):
    tokens = value.shape[0]
    for j in range(ROW_TILE):
        ref[(*lead, pl.ds(j, tokens, stride=ROW_TILE), slice(None))] = value[:, j * LANES:(j + 1) * LANES]


def _mod_kernel(cond_ref, w_ref, b_ref, o_ref):
    cond = cond_ref[...]
    act = cond * jax.nn.sigmoid(cond)
    o_ref[...] = _dot3(act, w_ref[...]) + b_ref[...]


def _mod_call(cond, w_mod, b_mod):
    tn = 1536
    n = N_MOD * D_MODEL
    return pl.pallas_call(
        _mod_kernel,
        grid=(n // tn,),
        in_specs=[
            pl.BlockSpec((MOD_ROWS, D_MODEL), lambda i: (0, 0)),
            pl.BlockSpec((D_MODEL, tn), lambda i: (0, i)),
            pl.BlockSpec((1, tn), lambda i: (0, i)),
        ],
        out_specs=pl.BlockSpec((MOD_ROWS, tn), lambda i: (0, i)),
        out_shape=jax.ShapeDtypeStruct((MOD_ROWS, n), jnp.float32),
        compiler_params=pltpu.CompilerParams(
            dimension_semantics=("arbitrary",), vmem_limit_bytes=VMEM_LIMIT),
        name="mod",
    )(cond, w_mod, b_mod)


def _proj_kernel(x_ref, mod_ref, g_ref, b_ref, w_ref, o_ref, *h_out, mod_row, latent):
    shift = mod_ref[mod_row:mod_row + 1, 0:D_MODEL]
    scale = mod_ref[mod_row:mod_row + 1, D_MODEL:2 * D_MODEL]
    tm, n = o_ref.shape
    sub = min(tm, PROJ_SUB)

    def prep(r):
        rows = slice(r * sub, (r + 1) * sub)
        h = _layer_norm(x_ref[rows, :], g_ref[...], b_ref[...])
        if latent:
            h_out[0][rows, :] = h
        return (h * (1.0 + scale) + shift).astype(jnp.bfloat16)

    def finish(r, c, res):
        if latent and c == 0:
            lane = lax.broadcasted_iota(jnp.int32, (1, D_MODEL), 1)
            res = res * jnp.where(lane < ATT_W, HEAD_DIM ** -0.5, 1.0)
        if latent and c * D_MODEL >= GATE_COL:
            res = jax.nn.sigmoid(res)
        o_ref[r * sub:(r + 1) * sub, c * D_MODEL:(c + 1) * D_MODEL] = res.astype(jnp.bfloat16)

    hm = {0: prep(0)}
    waiting = None
    for r in range(tm // sub):
        for c in range(n // D_MODEL):
            res = _bdot(hm[r], w_ref[:, c * D_MODEL:(c + 1) * D_MODEL])
            if c == 0 and (r + 1) * sub < tm:
                hm[r + 1] = prep(r + 1)
            if waiting is not None:
                finish(*waiting)
            waiting = (r, c, res)
    finish(*waiting)


def _proj_call(x, mod, g, b, w, *, mod_row, latent, tm):
    rows, n = x.shape[0], w.shape[1]
    out_specs = [pl.BlockSpec((tm, n), lambda i: (i, 0))]
    out_shape = [jax.ShapeDtypeStruct((rows, n), jnp.bfloat16)]
    if latent:
        out_specs.append(pl.BlockSpec((tm, D_MODEL), lambda i: (i, 0)))
        out_shape.append(jax.ShapeDtypeStruct((rows, D_MODEL), jnp.float32))
    return pl.pallas_call(
        functools.partial(_proj_kernel, mod_row=mod_row, latent=latent),
        grid=(rows // tm,),
        in_specs=[
            pl.BlockSpec((tm, D_MODEL), lambda i: (i, 0)),
            pl.BlockSpec(mod.shape, lambda i: (0, 0)),
            pl.BlockSpec((1, D_MODEL), lambda i: (0, 0)),
            pl.BlockSpec((1, D_MODEL), lambda i: (0, 0)),
            pl.BlockSpec((D_MODEL, n), lambda i: (0, 0), pipeline_mode=pl.Buffered(1)),
        ],
        out_specs=out_specs,
        out_shape=out_shape,
        compiler_params=pltpu.CompilerParams(
            dimension_semantics=("arbitrary",), vmem_limit_bytes=VMEM_LIMIT),
        name="proj",
    )(x, mod, g, b, w)


def _attn_bias_table(rpb):
    col = jnp.arange(GRID_W, dtype=jnp.int32)
    col_start = jnp.clip(col - WIN_W // 2, 0, GRID_W - WIN_W)
    col_mask = (col[None, :] >= col_start[:, None]) & (col[None, :] < col_start[:, None] + WIN_W)
    col_off = jnp.clip(col[None, :] - col[:, None], 1 - WIN_W, WIN_W - 1) + (WIN_W - 1)
    onehot = (col_off[None] == jnp.arange(2 * WIN_W - 1, dtype=jnp.int32)[:, None, None]).astype(jnp.float32)
    tab = jnp.einsum("hrc,cqk->hqrk", rpb.astype(jnp.float32), onehot, precision=lax.Precision.HIGHEST)
    tab = jnp.where(col_mask[None, :, None, :], tab, NEG_INF)
    n_rows = 2 * WIN_H - 1
    flat = tab.reshape(N_HEADS // 2, 2 * GRID_W, n_rows * GRID_W)
    even = jnp.pad(flat, ((0, 0), (0, 0), (0, BIAS_LANES - n_rows * GRID_W)))
    odd = jnp.pad(flat[:, :, GRID_W:], ((0, 0), (0, 0), (0, BIAS_LANES - (n_rows - 1) * GRID_W)))

    def window_kernel(even_ref, odd_ref, o_ref):
        start = WIN_H - 1 - pl.program_id(0)
        base = pl.multiple_of(lax.shift_right_logical(start, 1) * LANES, LANES)
        width = WIN_H * GRID_W
        o_ref[0] = jnp.where((start & 1) == 0, even_ref[:, :, pl.ds(base, width)], odd_ref[:, :, pl.ds(base, width)])

    full = pl.BlockSpec(even.shape, lambda v: (0, 0, 0))
    return pl.pallas_call(
        window_kernel,
        grid=(WIN_H,),
        in_specs=[full, full],
        out_specs=pl.BlockSpec((1, N_HEADS // 2, 2 * GRID_W, WIN_H * GRID_W), lambda v: (v, 0, 0, 0)),
        out_shape=jax.ShapeDtypeStruct((WIN_H, N_HEADS // 2, 2 * GRID_W, WIN_H * GRID_W), jnp.float32),
        compiler_params=pltpu.CompilerParams(dimension_semantics=("arbitrary",), vmem_limit_bytes=VMEM_LIMIT),
        name="bias_table",
    )(even, odd)


def _mix_kernel(h_ref, mod_ref,
                q_ref, kp_ref, kc_ref, kn_ref, vp_ref, vc_ref, vn_ref,
                pp_ref, pc_ref, pn_ref, ga_ref, gb_ref,
                kvc_ref, bias_ref, wgrp_ref, pscale_ref, wap_ref, wpp_ref, wout_ref,
                ln1g_ref, ln1b_ref,
                o_ref,
                kbuf, vbuf, yabuf, pbuf, ypbuf, zbuf):
    b = pl.program_id(0)
    nb = pl.num_programs(0)

    kbuf[0:KV_HALO, :] = kp_ref[...]
    kbuf[KV_HALO:KV_HALO + MIX_TQ, :] = kc_ref[...]
    kbuf[KV_HALO + MIX_TQ:, :] = kn_ref[...]
    vbuf[0:KV_HALO, :] = vp_ref[...]
    vbuf[KV_HALO:KV_HALO + MIX_TQ, :] = vc_ref[...]
    vbuf[KV_HALO + MIX_TQ:, :] = vn_ref[...]

    lane = lax.broadcasted_iota(jnp.int32, (GRID_W, LANES), 1)
    first_head = lane < HEAD_DIM

    units = [(j, pair) for j in range(MIX_ROWS) for pair in range(N_HEADS // 2)]
    nt = (((1,), (1,)), ((), ()))

    def window(j):
        r = b * MIX_ROWS + j
        rs = jnp.clip(r - WIN_H // 2, 0, ROWS - WIN_H)
        return pl.multiple_of((rs - b * MIX_ROWS + WIN_H // 2) * GRID_W, GRID_W), r - rs

    def scores(j, pair):
        off, var = window(j)
        cols = slice(pair * LANES, (pair + 1) * LANES)
        q = q_ref[j * GRID_W:(j + 1) * GRID_W, cols]
        zero = jnp.zeros_like(q)
        q2 = jnp.concatenate([jnp.where(first_head, q, zero), jnp.where(first_head, zero, q)], axis=0)
        kw = kbuf[pl.ds(off, WIN_H * GRID_W), cols]
        s_loc = lax.dot_general(q2, kw, nt, preferred_element_type=jnp.float32) + bias_ref[var, pair]
        s_ctx = lax.dot_general(q2, kvc_ref[:, cols], nt, preferred_element_type=jnp.float32)
        return s_loc, s_ctx

    def values(j, pair, s_loc, s_ctx):
        off, _ = window(j)
        cols = slice(pair * LANES, (pair + 1) * LANES)
        vw = vbuf[pl.ds(off, WIN_H * GRID_W), cols]
        vctx = kvc_ref[:, ATT_W + pair * LANES:ATT_W + (pair + 1) * LANES]
        m = jnp.maximum(jnp.max(s_loc, axis=-1, keepdims=True), jnp.max(s_ctx, axis=-1, keepdims=True))
        p_loc = jnp.exp(s_loc - m).astype(jnp.bfloat16)
        p_ctx = jnp.exp(s_ctx - m).astype(jnp.bfloat16)
        ones = lambda rows: jnp.ones((rows, LANES), jnp.bfloat16)
        o2 = (_bdot(p_loc, jnp.concatenate([vw, ones(WIN_H * GRID_W)], axis=1))
              + _bdot(p_ctx, jnp.concatenate([vctx, ones(CTX_LEN)], axis=1)))
        o2 = o2[:, :LANES] / o2[:, LANES:]
        o_pair = jnp.where(first_head, o2[:GRID_W], o2[GRID_W:])
        yabuf[j * GRID_W:(j + 1) * GRID_W, cols] = o_pair.astype(jnp.bfloat16)

    pbuf[0:POOL_HALO, :] = jnp.where(b > 0, pp_ref[...].astype(jnp.float32), 0.0)
    pbuf[POOL_HALO:POOL_HALO + MIX_TQ, :] = pc_ref[...].astype(jnp.float32)
    pbuf[POOL_HALO + MIX_TQ:, :] = jnp.where(b < nb - 1, pn_ref[...].astype(jnp.float32), 0.0)
    edge = lax.broadcasted_iota(jnp.int32, (ROW_TILE, 1), 0)

    def pool_group(g):
        win = POOL_WINDOWS[g]
        lo, hi = win // 2, win - win // 2
        cols = slice(g * POOL_DIM, (g + 1) * POOL_DIM)
        acc = None
        for d in range(-lo, hi):
            term = pbuf[POOL_HALO + d:POOL_HALO + d + MIX_TQ, cols]
            acc = term if acc is None else acc + term
        assert max(lo, hi) <= ROW_TILE
        top = jnp.where(b == 0, 1.0 / (win - jnp.maximum(lo - edge, 0)).astype(jnp.float32), 1.0 / win)
        bot = jnp.where(b == nb - 1,
                        1.0 / (win - jnp.maximum(edge + hi - ROW_TILE, 0)).astype(jnp.float32), 1.0 / win)
        inv = jnp.concatenate([top, jnp.full((MIX_TQ - 2 * ROW_TILE, 1), 1.0 / win, jnp.float32), bot], axis=0)
        pooled = acc * inv - pbuf[POOL_HALO:POOL_HALO + MIX_TQ, cols]
        yp = _bdot(pooled.astype(jnp.bfloat16), wgrp_ref[g]) * pscale_ref[:, cols]
        ypbuf[:, cols] = yp.astype(jnp.bfloat16)

    def pooled_branch():
        zbuf[...] = gb_ref[...].astype(jnp.float32) * _bdot(ypbuf[...], wpp_ref[...])

    extra = {}
    for g in range(POOL_GROUPS):
        extra[(g + 1) * len(units) // (POOL_GROUPS + 2)] = functools.partial(pool_group, g)
    extra[(POOL_GROUPS + 1) * len(units) // (POOL_GROUPS + 2)] = pooled_branch
    pending = [scores(*u) for u in units[:ATTN_AHEAD]]
    for n, u in enumerate(units):
        if n + ATTN_AHEAD < len(units):
            pending.append(scores(*units[n + ATTN_AHEAD]))
        values(*u, *pending.pop(0))
        if n in extra:
            extra[n]()

    g1 = mod_ref[0:1, 2 * D_MODEL:3 * D_MODEL]
    n_chunks = MIX_TQ // MERGE_ROWS
    rows = lambda c: slice(c * MERGE_ROWS, (c + 1) * MERGE_ROWS)
    z, y = {}, {}

    def stage_a(c):
        ya = _bdot(yabuf[rows(c), :], wap_ref[...])
        z[c] = (ga_ref[rows(c), :].astype(jnp.float32) * ya + zbuf[rows(c), :]).astype(jnp.bfloat16)

    def stage_b(c):
        y[c] = _bdot(z.pop(c), wout_ref[...])

    def stage_c(c):
        hn = _layer_norm(DEEPNORM_ALPHA * h_ref[rows(c), :] + g1 * y.pop(c), ln1g_ref[...], ln1b_ref[...])
        for j in range(ROW_TILE):
            o_ref[pl.ds(c * MERGE_ROWS * ROW_TILE + j, MERGE_ROWS, stride=ROW_TILE), :] = hn[:, j * LANES:(j + 1) * LANES]

    for t in range(n_chunks + 2):
        if t < n_chunks:
            stage_a(t)
        if 0 <= t - 1 < n_chunks:
            stage_b(t - 1)
        if 0 <= t - 2 < n_chunks:
            stage_c(t - 2)


def _mix_call(h, mod, u, kvc, bias, wgrp, pscale, wap, wpp, wout, ln1g, ln1b):
    nb = SEQ // MIX_TQ
    halo_per_blk = MIX_TQ // KV_HALO
    n_halo = SEQ // KV_HALO
    ph_per_blk = MIX_TQ // POOL_HALO
    n_ph = SEQ // POOL_HALO

    def const(shape):
        return pl.BlockSpec(shape, lambda i: (0,) * len(shape), pipeline_mode=pl.Buffered(1))

    def prev_halo(c):
        return pl.BlockSpec((KV_HALO, ATT_W), lambda i: (jnp.maximum(i * halo_per_blk - 1, 0), c))

    def next_halo(c):
        return pl.BlockSpec((KV_HALO, ATT_W), lambda i: (jnp.minimum((i + 1) * halo_per_blk, n_halo - 1), c))

    def cur(c):
        return pl.BlockSpec((MIX_TQ, ATT_W), lambda i: (i, c))

    in_specs = [
        pl.BlockSpec((MIX_TQ, D_MODEL), lambda i: (i, 0)),
        const(mod.shape),
        cur(0),
        prev_halo(1), cur(1), next_halo(1),
        prev_halo(2), cur(2), next_halo(2),
        pl.BlockSpec((POOL_HALO, POOL_W), lambda i: (jnp.maximum(i * ph_per_blk - 1, 0), 3)),
        cur(3),
        pl.BlockSpec((POOL_HALO, POOL_W), lambda i: (jnp.minimum((i + 1) * ph_per_blk, n_ph - 1), 3)),
        pl.BlockSpec((MIX_TQ, D_MODEL), lambda i: (i, 2)),
        pl.BlockSpec((MIX_TQ, D_MODEL), lambda i: (i, 3)),
        const(kvc.shape), const(bias.shape), const(wgrp.shape), const(pscale.shape),
        const(wap.shape), const(wpp.shape), const(wout.shape),
        const((1, D_MODEL)), const((1, D_MODEL)),
    ]
    return pl.pallas_call(
        _mix_kernel,
        grid=(nb,),
        in_specs=in_specs,
        out_specs=pl.BlockSpec((MIX_TQ * ROW_TILE, LANES), lambda i: (i, 0)),
        out_shape=jax.ShapeDtypeStruct((SEQ * ROW_TILE, LANES), jnp.float32),
        scratch_shapes=[
            pltpu.VMEM((MIX_TQ + 2 * KV_HALO, ATT_W), jnp.bfloat16),
            pltpu.VMEM((MIX_TQ + 2 * KV_HALO, ATT_W), jnp.bfloat16),
            pltpu.VMEM((MIX_TQ, ATT_W), jnp.bfloat16),
            pltpu.VMEM((MIX_TQ + 2 * POOL_HALO, POOL_W), jnp.float32),
            pltpu.VMEM((MIX_TQ, POOL_W), jnp.bfloat16),
            pltpu.VMEM((MIX_TQ, D_MODEL), jnp.float32),
        ],
        compiler_params=pltpu.CompilerParams(
            dimension_semantics=("arbitrary",), vmem_limit_bytes=VMEM_LIMIT),
        name="mix",
    )(h, mod, u, u, u, u, u, u, u, u, u, u, u, u,
      kvc, bias, wgrp, pscale, wap, wpp, wout, ln1g, ln1b)


ID_E0, ID_E1, ID_R0, ID_R1 = 0, 1, 4, 5
POS_PACKED = 0
POS_BITS = 16
PLAN_EXPERT, PLAN_VALID, PLAN_NACT, PLAN_NEXT = 0, 1, 2, 3
PLAN_W = 2 * LANES
CW_C0, CW_C1 = 0, 1


def _route_kernel(h_ref, mod_ref, wrt_ref, brt_ref, pos_ref, cw_ref, plan_ref, carry_ref, ids_all):
    i = pl.program_id(0)
    tm = ROUTE_TM

    @pl.when(i == 0)
    def _():
        carry_ref[...] = jnp.zeros_like(carry_ref)

    shift = mod_ref[0:1, 3 * D_MODEL:4 * D_MODEL]
    scale = mod_ref[0:1, 4 * D_MODEL:5 * D_MODEL]
    hm = _load_row_tiles(h_ref, tm) * (1.0 + scale) + shift

    hm_hi, hm_lo = _split_bf16(hm)
    w_hi, w_lo = _split_bf16(wrt_ref[...])
    nt = (((1,), (1,)), ((), ()))
    dg = functools.partial(lax.dot_general, dimension_numbers=nt, preferred_element_type=jnp.float32)
    logits = dg(w_hi, hm_hi) + (dg(w_hi, hm_lo) + dg(w_lo, hm_hi)) + brt_ref[:, 0:1]

    sub = lax.broadcasted_iota(jnp.int32, (ROUTE_LOGIT_ROWS, tm), 0)
    big = jnp.int32(1 << 20)
    is_grp = sub < N_GROUPS
    gl = jnp.where(is_grp, logits, -jnp.inf)
    gmax = jnp.max(gl, axis=0, keepdims=True)
    gidx = jnp.min(jnp.where(gl == gmax, sub, big), axis=0, keepdims=True)
    gsum = jnp.sum(jnp.where(is_grp, jnp.exp(logits - gmax), 0.0), axis=0, keepdims=True)
    p_group = 1.0 / gsum

    eid = sub - N_GROUPS
    sel = (eid >= 0) & (eid < N_EXPERTS) & (lax.shift_right_arithmetic(eid, 3) == gidx)
    el = jnp.where(sel, logits, -jnp.inf)
    l0 = jnp.max(el, axis=0, keepdims=True)
    i0 = jnp.min(jnp.where(el == l0, sub, big), axis=0, keepdims=True)
    el2 = jnp.where(sub == i0, -jnp.inf, el)
    l1 = jnp.max(el2, axis=0, keepdims=True)
    i1 = jnp.min(jnp.where(el2 == l1, sub, big), axis=0, keepdims=True)
    t = jnp.exp(l1 - l0)
    w0 = 1.0 / (1.0 + t)
    w1 = t / (1.0 + t)

    half_rows = jnp.where(i >= pl.num_programs(0) // SEQ_PARTS, N_EXPERTS, 0)
    i0 = i0 + half_rows
    i1 = i1 + half_rows
    subs = lax.broadcasted_iota(jnp.int32, (ROUTE_SEG_ROWS, tm), 0)
    onehot = jnp.where((subs == i0) | (subs == i1), 1.0, 0.0)
    rr = lax.broadcasted_iota(jnp.int32, (tm, tm), 0)
    cc = lax.broadcasted_iota(jnp.int32, (tm, tm), 1)
    earlier = jnp.where(rr < cc, 1.0, 0.0).astype(jnp.bfloat16)
    carry = carry_ref[:, 0:1]
    prefix = _bdot(onehot.astype(jnp.bfloat16), earlier) + carry
    r0 = jnp.sum(jnp.where(subs == i0, prefix, 0.0), axis=0, keepdims=True)
    r1 = jnp.sum(jnp.where(subs == i1, prefix, 0.0), axis=0, keepdims=True)
    total = jnp.broadcast_to(carry + jnp.sum(onehot, axis=1, keepdims=True), carry_ref.shape)
    carry_ref[...] = total

    sub8 = lax.broadcasted_iota(jnp.int32, (ROW_TILE, tm), 0)
    ids = jnp.zeros((ROW_TILE, tm), jnp.int32)
    for idx, val in ((ID_E0, i0 - N_GROUPS), (ID_E1, i1 - N_GROUPS),
                     (ID_R0, r0.astype(jnp.int32)), (ID_R1, r1.astype(jnp.int32))):
        ids = jnp.where(sub8 == idx, val, ids)
    ids_all[:, pl.ds(pl.multiple_of(i * tm, tm), tm)] = ids

    cw8 = jnp.where(sub8 == CW_C0, p_group * w0, jnp.where(sub8 == CW_C1, p_group * w1, 0.0))
    cw_ref[...] = jnp.concatenate([cw8, jnp.zeros((LANES - ROW_TILE, tm), jnp.float32)], axis=0).T

    @pl.when(i == pl.num_programs(0) - 1)
    def _():
        subq = lax.broadcasted_iota(jnp.int32, (LANES, LANES), 0)
        laneq = lax.broadcasted_iota(jnp.int32, (LANES, LANES), 1)
        total = jnp.concatenate([carry_ref[...], jnp.zeros((LANES - ROUTE_SEG_ROWS, LANES), jnp.float32)], axis=0)
        cnt = total.astype(jnp.int32)
        tiles = lax.shift_right_logical(cnt + (EXP_TM - 1), EXP_TM.bit_length() - 1).astype(jnp.float32)
        incl = jnp.where(laneq <= subq, 1.0, 0.0).astype(jnp.bfloat16)
        tile_end = _bdot(incl, tiles.astype(jnp.bfloat16))
        tile_start = tile_end - tiles
        seg = (tile_start * EXP_TM).astype(jnp.int32)
        nact = jnp.max(tile_end, axis=0, keepdims=True)

        ids_full = ids_all[...]
        look = jnp.zeros_like(ids_full)
        for e in range(N_SEG):
            look = jnp.where(ids_full == e, seg[N_GROUPS + e, 0], look)
        pos01 = look + pltpu.roll(ids_full, ID_R0 - ID_E0, axis=0)
        assert EXP_TILES * EXP_TM <= 1 << POS_BITS
        pos_ref[...] = pos01 | (pltpu.roll(pos01, ROW_TILE - 1, axis=0) << POS_BITS)

        subp = lax.broadcasted_iota(jnp.int32, (LANES, PLAN_W), 0)
        tile = lax.broadcasted_iota(jnp.int32, (LANES, PLAN_W), 1).astype(jnp.float32)
        is_exp = (subp >= N_GROUPS) & (subp < N_GROUPS + N_SEG)
        end_col = tile_end[:, 0:1]
        nact_s = nact[:, 0:1]
        te = jnp.sum(jnp.where(is_exp & (tile >= end_col), 1.0, 0.0), axis=0, keepdims=True)
        te_last = jnp.sum(jnp.where(is_exp & (nact_s - 1.0 >= end_col), 1.0, 0.0), axis=0, keepdims=True)[:, 0:1]
        tile_row = tile[0:1, :]
        te = jnp.minimum(jnp.where(tile_row < nact_s, te, te_last), N_SEG - 1.0)
        mine = (subp - N_GROUPS).astype(jnp.float32) == te
        cnt_sel = jnp.sum(jnp.where(mine, total[:, 0:1], 0.0), axis=0, keepdims=True)
        start_sel = jnp.sum(jnp.where(mine, tile_start[:, 0:1], 0.0), axis=0, keepdims=True)
        end_sel = jnp.sum(jnp.where(mine, end_col, 0.0), axis=0, keepdims=True)
        valid = jnp.clip(cnt_sel - (tile_row - start_sel) * EXP_TM, 0.0, float(EXP_TM))
        valid = jnp.where(tile_row < nact_s, valid, 0.0)
        subr = lax.broadcasted_iota(jnp.int32, (ROW_TILE, PLAN_W), 0)
        plan = jnp.where(subr == PLAN_EXPERT, te, jnp.where(subr == PLAN_VALID, valid,
                         jnp.where(subr == PLAN_NACT, nact_s, jnp.where(subr == PLAN_NEXT, end_sel, 0.0))))
        plan_ref[...] = plan.astype(jnp.int32)


def _route_call(h, mod, wrt, brt):
    tm = ROUTE_TM
    return pl.pallas_call(
        _route_kernel,
        grid=(SEQ // tm,),
        in_specs=[
            pl.BlockSpec((tm * ROW_TILE, LANES), lambda i: (i, 0)),
            pl.BlockSpec(mod.shape, lambda i: (0, 0)),
            pl.BlockSpec((ROUTE_LOGIT_ROWS, D_MODEL), lambda i: (0, 0)),
            pl.BlockSpec((ROUTE_LOGIT_ROWS, LANES), lambda i: (0, 0)),
        ],
        out_specs=[
            pl.BlockSpec((ROW_TILE, SEQ), lambda i: (0, 0)),
            pl.BlockSpec((tm, LANES), lambda i: (i, 0)),
            pl.BlockSpec((ROW_TILE, PLAN_W), lambda i: (0, 0)),
        ],
        out_shape=[
            jax.ShapeDtypeStruct((ROW_TILE, SEQ), jnp.int32),
            jax.ShapeDtypeStruct((SEQ, LANES), jnp.float32),
            jax.ShapeDtypeStruct((ROW_TILE, PLAN_W), jnp.int32),
        ],
        scratch_shapes=[pltpu.VMEM((ROUTE_SEG_ROWS, LANES), jnp.float32),
                        pltpu.VMEM((ROW_TILE, SEQ), jnp.int32)],
        compiler_params=pltpu.CompilerParams(
            dimension_semantics=("arbitrary",), vmem_limit_bytes=VMEM_LIMIT),
        name="route",
    )(h, mod, wrt, brt)


SRC_UNROLL = 8
EXP_CHUNK = 256
CAST_CHUNKS = 8
PREP_AFTER_DOWN = 1
TILE_ROWS = EXP_TM * ROW_TILE


def _experts_kernel(te_ref, tv_ref, nact_ref, tnext_ref, pos_ref,
                    h_hbm, mod_ref, wg_hbm, wu_hbm, wd_hbm,
                    y_ref,
                    src_ref, ord_ref, hres, xbuf, xmat, wgs, wus, wds,
                    wgb0, wub0, wdb0, wgb1, wub1, wdb1, rsem, wsem):
    i = pl.program_id(0)
    last = pl.num_programs(0) - 1
    nact = nact_ref[0]
    wsets = ((wgb0, wub0, wdb0), (wgb1, wub1, wdb1))
    xcur = lax.rem(i, 2)
    part = lax.shift_right_logical(te_ref[i], N_EXPERTS.bit_length() - 1)

    def weight_copies(segment, st):
        e = segment & (N_EXPERTS - 1)
        return [pltpu.make_async_copy(w_hbm.at[e], stage.at[st], wsem.at[st, n])
                for n, (w_hbm, stage) in enumerate(((wg_hbm, wgs), (wu_hbm, wus), (wd_hbm, wds)))]

    def tile_after(t):
        return tnext_ref[jnp.minimum(t, last)]

    def gather_row(tile, k, s):
        local = (src_ref[tile * EXP_TM + k] - part * PART_TOKENS) & (PART_TOKENS - 1)
        xbuf[s, k * ROW_TILE:(k + 1) * ROW_TILE, :] = hres[pl.ds(pl.multiple_of(local * ROW_TILE, ROW_TILE),
                                                             ROW_TILE), :]

    def gather_items(tile, s):
        return [functools.partial(gather_row, tile, k, s) for k in range(EXP_TM)]

    def prepare_input(xs):
        x = _load_row_tiles(xbuf, EXP_TM, lead=(xs,))
        shift = mod_ref[0:1, 3 * D_MODEL:4 * D_MODEL]
        scale = mod_ref[0:1, 4 * D_MODEL:5 * D_MODEL]
        xmat[xs] = (x * (1.0 + scale) + shift).astype(jnp.bfloat16)

    def compute_chunks(xs, ws):
        wgb, wub, wdb = wsets[ws]

        state = {"act": []}

        def gate(c):
            def run():
                state["a"] = _bdot(xmat[xs], wgb[:, c * EXP_CHUNK:(c + 1) * EXP_CHUNK])
            return run

        def up(c):
            def run():
                a = state["a"]
                u = _bdot(xmat[xs], wub[:, c * EXP_CHUNK:(c + 1) * EXP_CHUNK])
                state["act"].append((a * jax.nn.sigmoid(a) * u).astype(jnp.bfloat16))
            return run

        def down(c):
            def run():
                if c == 0:
                    state["actf"] = jnp.concatenate(state["act"], axis=-1)
                yc = _bdot(state["actf"], wdb[:, c * EXP_CHUNK:(c + 1) * EXP_CHUNK])
                for jj in range(EXP_CHUNK // LANES):
                    j = c * (EXP_CHUNK // LANES) + jj
                    y_ref[pl.ds(j, EXP_TM, stride=ROW_TILE), :] = yc[:, jj * LANES:(jj + 1) * LANES]
            return run

        first = []
        for c in range(D_EXPERT // EXP_CHUNK):
            first += [gate(c), up(c)]
        return first, [down(c) for c in range(D_MODEL // EXP_CHUNK)]

    def resident_copy(p):
        rows = PART_TOKENS * ROW_TILE
        return pltpu.make_async_copy(h_hbm.at[pl.ds(pl.multiple_of(p * rows, rows), rows), :], hres, rsem.at[0])

    @pl.when(i == 0)
    def _():
        resident_copy(part).start()
        ord_ref[0] = 0
        for cp in weight_copies(te_ref[0], 0):
            cp.start()
        second = tile_after(0)

        @pl.when(second < nact)
        def _():
            for cp in weight_copies(te_ref[jnp.minimum(second, last)], 1):
                cp.start()

        def fill_body(tt, c):
            ts = [tt * SRC_UNROLL + u for u in range(SRC_UNROLL)]
            words = [pos_ref[t] for t in ts]
            for t, w in zip(ts, words):
                src_ref[w & ((1 << POS_BITS) - 1)] = t
                src_ref[lax.shift_right_logical(w, POS_BITS)] = t
            return c
        lax.fori_loop(0, SEQ // SRC_UNROLL, fill_body, 0)

        def pad_tile(t, c):
            pad_tok = lax.shift_right_logical(te_ref[t], N_EXPERTS.bit_length() - 1) * PART_TOKENS

            def pad_row(k, c2):
                src_ref[t * EXP_TM + k] = pad_tok
                return c2
            return lax.fori_loop(tv_ref[t], EXP_TM, pad_row, c)
        lax.fori_loop(0, nact, pad_tile, 0)

    active = i < nact
    prev = jnp.maximum(i - 1, 0)
    new_segment = (i == 0) | (te_ref[i] != te_ref[prev])

    @pl.when(active & ((i == 0) | (part != lax.shift_right_logical(te_ref[prev], N_EXPERTS.bit_length() - 1))))
    def _():
        @pl.when(i > 0)
        def _():
            resident_copy(part).start()
        resident_copy(part).wait()
        for item in gather_items(i, xcur):
            item()
        prepare_input(xcur)

    @pl.when(active & new_segment & (i > 0))
    def _():
        ord_ref[0] = ord_ref[0] + 1

    parity = ord_ref[0] & 1
    after1 = tile_after(i)
    after2 = tile_after(after1)
    after3 = tile_after(after2)

    def cast_items(st):
        def block(stage, dst, rows):
            def run():
                dst[rows, :] = stage[st, rows, :].astype(jnp.bfloat16)
            return run
        out = []
        for r in range(CAST_CHUNKS):
            up_rows = slice(r * D_MODEL // CAST_CHUNKS, (r + 1) * D_MODEL // CAST_CHUNKS)
            dn_rows = slice(r * D_EXPERT // CAST_CHUNKS, (r + 1) * D_EXPERT // CAST_CHUNKS)
            out += [block(stage, dst, rows)
                    for stage, dst, rows in zip((wgs, wus, wds), wsets[st], (up_rows, up_rows, dn_rows))]
        return out

    @pl.when(i == 0)
    def _():
        for cp in weight_copies(te_ref[0], 0):
            cp.wait()
        for item in cast_items(0):
            item()

        @pl.when(after2 < nact)
        def _():
            for cp in weight_copies(te_ref[jnp.minimum(after2, last)], 0):
                cp.start()

    def run_tile(par, extra_items):
        first, second = compute_chunks(xcur, par)
        gathers = gather_items(jnp.minimum(i + 1, nact - 1), 1 - xcur)
        per = -(-len(gathers) // len(first))
        per_extra = -(-len(extra_items) // (len(first) + len(second)))
        for n, chunk in enumerate(first + second):
            if n < len(first):
                for item in gathers[n * per:(n + 1) * per]:
                    item()
            for item in extra_items[n * per_extra:(n + 1) * per_extra]:
                item()
            chunk()
            if n == len(first) + PREP_AFTER_DOWN:
                prepare_input(1 - xcur)

    last_of_segment = (i + 1 >= nact) | (te_ref[jnp.minimum(i + 1, last)] != te_ref[i])
    cast_next = active & last_of_segment & (after1 < nact)

    for par in range(2):
        @pl.when(cast_next & (parity == par))
        def _():
            for cp in weight_copies(te_ref[jnp.minimum(after1, last)], 1 - par):
                cp.wait()
            run_tile(par, cast_items(1 - par))

            @pl.when(after3 < nact)
            def _():
                for cp in weight_copies(te_ref[jnp.minimum(after3, last)], 1 - par):
                    cp.start()

        @pl.when(active & jnp.logical_not(cast_next) & (parity == par))
        def _():
            run_tile(par, [])

    @pl.when(jnp.logical_not(active))
    def _():
        y_ref[...] = jnp.zeros_like(y_ref)


def _experts_call(te, tv, nact, tnext, pos, h, mod, wg, wu, wd):
    grid_spec = pltpu.PrefetchScalarGridSpec(
        num_scalar_prefetch=5,
        grid=(EXP_TILES,),
        in_specs=[
            pl.BlockSpec(memory_space=pl.ANY),
            pl.BlockSpec(mod.shape, lambda i, *_: (0, 0)),
            pl.BlockSpec(memory_space=pl.ANY),
            pl.BlockSpec(memory_space=pl.ANY),
            pl.BlockSpec(memory_space=pl.ANY),
        ],
        out_specs=pl.BlockSpec((TILE_ROWS, LANES), lambda i, *_: (i, 0)),
        scratch_shapes=[
            pltpu.SMEM((EXP_TILES * EXP_TM,), jnp.int32),
            pltpu.SMEM((1,), jnp.int32),
            pltpu.VMEM((PART_TOKENS * ROW_TILE, LANES), jnp.float32),
            pltpu.VMEM((2, TILE_ROWS, LANES), jnp.float32),
            pltpu.VMEM((2, EXP_TM, D_MODEL), jnp.bfloat16),
            pltpu.VMEM((2, D_MODEL, D_EXPERT), jnp.float32),
            pltpu.VMEM((2, D_MODEL, D_EXPERT), jnp.float32),
            pltpu.VMEM((2, D_EXPERT, D_MODEL), jnp.float32),
            pltpu.VMEM((D_MODEL, D_EXPERT), jnp.bfloat16),
            pltpu.VMEM((D_MODEL, D_EXPERT), jnp.bfloat16),
            pltpu.VMEM((D_EXPERT, D_MODEL), jnp.bfloat16),
            pltpu.VMEM((D_MODEL, D_EXPERT), jnp.bfloat16),
            pltpu.VMEM((D_MODEL, D_EXPERT), jnp.bfloat16),
            pltpu.VMEM((D_EXPERT, D_MODEL), jnp.bfloat16),
            pltpu.SemaphoreType.DMA((1,)),
            pltpu.SemaphoreType.DMA((2, 3)),
        ],
    )
    return pl.pallas_call(
        _experts_kernel,
        grid_spec=grid_spec,
        out_shape=jax.ShapeDtypeStruct((EXP_TILES * TILE_ROWS, LANES), jnp.float32),
        compiler_params=pltpu.CompilerParams(
            dimension_semantics=("arbitrary",), vmem_limit_bytes=EXPERTS_VMEM_LIMIT),
        name="experts",
    )(te, tv, nact, tnext, pos, h, mod, wg, wu, wd)


def _combine_kernel(pos_ref, h_ref, cw_ref, mod_ref, g_ref, b_ref, ys_hbm, o_ref, ybuf, sem):
    i = pl.program_id(0)
    tm = CMB_TM
    slot = lax.rem(i, 2)

    def start_row(tile, k, s):
        word = pos_ref[tile * tm + k]
        for half, p in ((0, word & ((1 << POS_BITS) - 1)), (1, lax.shift_right_logical(word, POS_BITS))):
            pltpu.make_async_copy(ys_hbm.at[pl.ds(pl.multiple_of(p * ROW_TILE, ROW_TILE), ROW_TILE), :],
                                  ybuf.at[s, half, pl.ds(pl.multiple_of(k * ROW_TILE, ROW_TILE), ROW_TILE), :],
                                  sem.at[s]).start(priority=half)

    @pl.when(i == 0)
    def _():
        def body(kk, c):
            for u in range(SRC_UNROLL):
                start_row(0, kk * SRC_UNROLL + u, 0)
            return c
        lax.fori_loop(0, tm // SRC_UNROLL, body, 0)

    for half in range(2):
        pltpu.make_async_copy(ys_hbm.at[pl.ds(0, tm * ROW_TILE), :], ybuf.at[slot, half], sem.at[slot]).wait()

    g2 = mod_ref[0:1, 5 * D_MODEL:6 * D_MODEL]

    def chunk(c):
        rows = slice(c * CMB_CHUNK, (c + 1) * CMB_CHUNK)
        y0 = _load_row_tiles(ybuf, CMB_CHUNK, lead=(slot, 0), first=c * CMB_CHUNK)
        y1 = _load_row_tiles(ybuf, CMB_CHUNK, lead=(slot, 1), first=c * CMB_CHUNK)
        ffn = cw_ref[rows, CW_C0:CW_C0 + 1] * y0 + cw_ref[rows, CW_C1:CW_C1 + 1] * y1
        h = _load_row_tiles(h_ref, CMB_CHUNK, first=c * CMB_CHUNK)
        o_ref[rows, :] = _layer_norm(DEEPNORM_ALPHA * h + g2 * ffn, g_ref[...], b_ref[...])

    n_chunks = tm // CMB_CHUNK

    @pl.when(i + 1 < pl.num_programs(0))
    def _():
        per = tm // n_chunks
        for c in range(n_chunks):
            for k in range(c * per, (c + 1) * per):
                start_row(i + 1, k, 1 - slot)
            chunk(c)

    @pl.when(i + 1 == pl.num_programs(0))
    def _():
        for c in range(n_chunks):
            chunk(c)


def _combine_call(pos, h, ys, cw, mod, g, b):
    tm = CMB_TM
    grid_spec = pltpu.PrefetchScalarGridSpec(
        num_scalar_prefetch=1,
        grid=(SEQ // tm,),
        in_specs=[
            pl.BlockSpec((tm * ROW_TILE, LANES), lambda i, *_: (i, 0)),
            pl.BlockSpec((tm, LANES), lambda i, *_: (i, 0)),
            pl.BlockSpec(mod.shape, lambda i, *_: (0, 0)),
            pl.BlockSpec((1, D_MODEL), lambda i, *_: (0, 0)),
            pl.BlockSpec((1, D_MODEL), lambda i, *_: (0, 0)),
            pl.BlockSpec(memory_space=pl.ANY),
        ],
        out_specs=pl.BlockSpec((tm, D_MODEL), lambda i, *_: (i, 0)),
        scratch_shapes=[
            pltpu.VMEM((2, 2, tm * ROW_TILE, LANES), jnp.float32),
            pltpu.SemaphoreType.DMA((2,)),
        ],
    )
    return pl.pallas_call(
        _combine_kernel,
        grid_spec=grid_spec,
        out_shape=jax.ShapeDtypeStruct((SEQ, D_MODEL), jnp.float32),
        compiler_params=pltpu.CompilerParams(
            dimension_semantics=("arbitrary",), vmem_limit_bytes=VMEM_LIMIT),
        name="combine",
    )(pos, h, cw, mod, g, b, ys)


def kernel(x, c, ctx, c_ctx, ln_in_g, ln_in_b, w_mod, b_mod, w_in, rpb, w_pool_grp, pool_scale,
           w_attn_proj, w_pool_proj, w_out, ln1_g, ln1_b, w_router_group, b_router_group,
           w_router_expert, b_router_expert, w_expert_gate, w_expert_up, w_expert_down, ln2_g, ln2_b):
    assert x.shape == (1, SEQ, D_MODEL) and ctx.shape == (1, CTX_LEN, D_MODEL)
    assert w_mod.shape[0] == 1, "single-layer trunk"
    f32, bf16 = jnp.float32, jnp.bfloat16
    row = lambda v: v.reshape(1, -1).astype(f32)

    cond = jnp.concatenate([c, c_ctx[None], jnp.zeros((MOD_ROWS - 2, D_MODEL), f32)], axis=0)
    mod = _mod_call(cond, w_mod[0], row(b_mod[0]))

    lng, lnb = row(ln_in_g), row(ln_in_b)
    w_in_b = w_in[0].astype(bf16)
    u, h0 = _proj_call(x[0], mod, lng, lnb, w_in_b, mod_row=0, latent=True, tm=PROJ_TM)
    kvc, = _proj_call(ctx[0], mod, lng, lnb, w_in_b[:, ATT_W:3 * ATT_W], mod_row=1, latent=False, tm=CTX_LEN)

    h1 = _mix_call(h0, mod, u, kvc, _attn_bias_table(rpb[0]),
                   w_pool_grp[0].astype(bf16), row(pool_scale[0]),
                   w_attn_proj[0].astype(bf16), w_pool_proj[0].astype(bf16), w_out[0].astype(bf16),
                   row(ln1_g[0]), row(ln1_b[0]))

    n_logit = N_GROUPS + N_EXPERTS
    wrt = jnp.concatenate([w_router_group[0].T, w_router_expert[0].T,
                           jnp.zeros((ROUTE_LOGIT_ROWS - n_logit, D_MODEL), f32)], axis=0)
    brt = jnp.concatenate([b_router_group[0], b_router_expert[0], jnp.zeros((ROUTE_LOGIT_ROWS - n_logit,), f32)])
    brt = jnp.broadcast_to(brt[:, None], (ROUTE_LOGIT_ROWS, LANES))
    pos, cw, plan = _route_call(h1, mod, wrt, brt)

    posw = pos[POS_PACKED]
    y = _experts_call(plan[PLAN_EXPERT, :EXP_TILES], plan[PLAN_VALID, :EXP_TILES], plan[PLAN_NACT, :1],
                      plan[PLAN_NEXT, :EXP_TILES], posw, h1, mod,
                      w_expert_gate[0], w_expert_up[0], w_expert_down[0])
    out = _combine_call(posw, h1, y, cw, mod, row(ln2_g[0]), row(ln2_b[0]))
    return out[None]
```

```python
import functools

import jax
import jax.numpy as jnp
from jax import lax
from jax.experimental import pallas as pl
from jax.experimental.pallas import tpu as pltpu

D_MODEL = 1024
SEQ = 16384
GRID_W = 64
ROWS = SEQ // GRID_W
CTX_LEN = 256
N_HEADS = 8
HEAD_DIM = 64
ATT_W = N_HEADS * HEAD_DIM
WIN_H = 8
WIN_W = 16
POOL_WINDOWS = (2, 4, 8, 16)
POOL_GROUPS = 4
POOL_DIM = 128
POOL_W = POOL_GROUPS * POOL_DIM
PROJ_W = 3 * ATT_W + POOL_W + 2 * D_MODEL
GATE_COL = 3 * ATT_W + POOL_W
N_GROUPS = 4
EXPERTS_PER_GROUP = 8
N_EXPERTS = N_GROUPS * EXPERTS_PER_GROUP
D_EXPERT = 512
N_MOD = 6
DEEPNORM_ALPHA = 2.0 ** 0.25
LN_EPS = 1e-5
NEG_INF = -1e30

LANES = 128
ROW_TILE = 8
MOD_ROWS = 8
PROJ_TM = 512
PROJ_SUB = 256
MIX_ROWS = 8
MIX_TQ = MIX_ROWS * GRID_W
KV_HALO = 4 * GRID_W
POOL_HALO = 16
ROUTE_TM = 512
ROUTE_LOGIT_ROWS = 40
ROUTE_SEG_ROWS = 72
EXP_TM = 256
SEQ_PARTS = 2
PART_TOKENS = SEQ // SEQ_PARTS
N_SEG = SEQ_PARTS * N_EXPERTS
EXP_TILES = 2 * SEQ // EXP_TM + N_SEG
CMB_TM = 512
CMB_CHUNK = 64
MERGE_ROWS = 256
BIAS_LANES = 1024
ATTN_AHEAD = 3
HALF = D_MODEL // 2
VMEM_LIMIT = 56 * 1024 * 1024
EXPERTS_VMEM_LIMIT = 60 * 1024 * 1024


def _layer_norm(x, g, b):
    mu = jnp.mean(x, axis=-1, keepdims=True)
    xc = x - mu
    var = jnp.mean(xc * xc, axis=-1, keepdims=True)
    return xc * lax.rsqrt(var + LN_EPS) * g + b


def _bdot(a, b):
    return jnp.dot(a, b, preferred_element_type=jnp.float32)


def _split_bf16(a):
    hi = a.astype(jnp.bfloat16)
    lo = (a - hi.astype(jnp.float32)).astype(jnp.bfloat16)
    return hi, lo


def _dot3(a, b):
    a_hi, a_lo = _split_bf16(a)
    b_hi, b_lo = _split_bf16(b)
    return _bdot(a_hi, b_hi) + (_bdot(a_hi, b_lo) + _bdot(a_lo, b_hi))


def _load_row_tiles(ref, tokens, lead=(), first=0):
    parts = [ref[(*lead, pl.ds(first * ROW_TILE + j, tokens, stride=ROW_TILE), slice(None))]
             for j in range(ROW_TILE)]
    return jnp.concatenate(parts, axis=-1)


def _store_row_tiles(ref, value, lead=()):
    tokens = value.shape[0]
    for j in range(ROW_TILE):
        ref[(*lead, pl.ds(j, tokens, stride=ROW_TILE), slice(None))] = value[:, j * LANES:(j + 1) * LANES]


def _mod_kernel(cond_ref, w_ref, b_ref, o_ref):
    cond = cond_ref[...]
    act = cond * jax.nn.sigmoid(cond)
    o_ref[...] = _dot3(act, w_ref[...]) + b_ref[...]


def _mod_call(cond, w_mod, b_mod):
    tn = 1536
    n = N_MOD * D_MODEL
    return pl.pallas_call(
        _mod_kernel,
        grid=(n // tn,),
        in_specs=[
            pl.BlockSpec((MOD_ROWS, D_MODEL), lambda i: (0, 0)),
            pl.BlockSpec((D_MODEL, tn), lambda i: (0, i)),
            pl.BlockSpec((1, tn), lambda i: (0, i)),
        ],
        out_specs=pl.BlockSpec((MOD_ROWS, tn), lambda i: (0, i)),
        out_shape=jax.ShapeDtypeStruct((MOD_ROWS, n), jnp.float32),
        compiler_params=pltpu.CompilerParams(
            dimension_semantics=("arbitrary",), vmem_limit_bytes=VMEM_LIMIT),
        name="mod",
    )(cond, w_mod, b_mod)


def _proj_kernel(x_ref, mod_ref, g_ref, b_ref, w_ref, o_ref, *h_out, mod_row, latent):
    shift = mod_ref[mod_row:mod_row + 1, 0:D_MODEL]
    scale = mod_ref[mod_row:mod_row + 1, D_MODEL:2 * D_MODEL]
    tm, n = o_ref.shape
    sub = min(tm, PROJ_SUB)

    def prep(r):
        rows = slice(r * sub, (r + 1) * sub)
        h = _layer_norm(x_ref[rows, :], g_ref[...], b_ref[...])
        if latent:
            h_out[0][rows, :] = h
        return (h * (1.0 + scale) + shift).astype(jnp.bfloat16)

    def finish(r, c, res):
        if latent and c == 0:
            lane = lax.broadcasted_iota(jnp.int32, (1, D_MODEL), 1)
            res = res * jnp.where(lane < ATT_W, HEAD_DIM ** -0.5, 1.0)
        if latent and c * D_MODEL >= GATE_COL:
            res = jax.nn.sigmoid(res)
        o_ref[r * sub:(r + 1) * sub, c * D_MODEL:(c + 1) * D_MODEL] = res.astype(jnp.bfloat16)

    hm = {0: prep(0)}
    waiting = None
    for r in range(tm // sub):
        for c in range(n // D_MODEL):
            res = _bdot(hm[r], w_ref[:, c * D_MODEL:(c + 1) * D_MODEL])
            if c == 0 and (r + 1) * sub < tm:
                hm[r + 1] = prep(r + 1)
            if waiting is not None:
                finish(*waiting)
            waiting = (r, c, res)
    finish(*waiting)


def _proj_call(x, mod, g, b, w, *, mod_row, latent, tm):
    rows, n = x.shape[0], w.shape[1]
    out_specs = [pl.BlockSpec((tm, n), lambda i: (i, 0))]
    out_shape = [jax.ShapeDtypeStruct((rows, n), jnp.bfloat16)]
    if latent:
        out_specs.append(pl.BlockSpec((tm, D_MODEL), lambda i: (i, 0)))
        out_shape.append(jax.ShapeDtypeStruct((rows, D_MODEL), jnp.float32))
    return pl.pallas_call(
        functools.partial(_proj_kernel, mod_row=mod_row, latent=latent),
        grid=(rows // tm,),
        in_specs=[
            pl.BlockSpec((tm, D_MODEL), lambda i: (i, 0)),
            pl.BlockSpec(mod.shape, lambda i: (0, 0)),
            pl.BlockSpec((1, D_MODEL), lambda i: (0, 0)),
            pl.BlockSpec((1, D_MODEL), lambda i: (0, 0)),
            pl.BlockSpec((D_MODEL, n), lambda i: (0, 0), pipeline_mode=pl.Buffered(1)),
        ],
        out_specs=out_specs,
        out_shape=out_shape,
        compiler_params=pltpu.CompilerParams(
            dimension_semantics=("arbitrary",), vmem_limit_bytes=VMEM_LIMIT),
        name="proj",
    )(x, mod, g, b, w)


def _attn_bias_table(rpb):
    col = jnp.arange(GRID_W, dtype=jnp.int32)
    col_start = jnp.clip(col - WIN_W // 2, 0, GRID_W - WIN_W)
    col_mask = (col[None, :] >= col_start[:, None]) & (col[None, :] < col_start[:, None] + WIN_W)
    col_off = jnp.clip(col[None, :] - col[:, None], 1 - WIN_W, WIN_W - 1) + (WIN_W - 1)
    onehot = (col_off[None] == jnp.arange(2 * WIN_W - 1, dtype=jnp.int32)[:, None, None]).astype(jnp.float32)
    tab = jnp.einsum("hrc,cqk->hqrk", rpb.astype(jnp.float32), onehot, precision=lax.Precision.HIGHEST)
    tab = jnp.where(col_mask[None, :, None, :], tab, NEG_INF)
    n_rows = 2 * WIN_H - 1
    flat = tab.reshape(N_HEADS // 2, 2 * GRID_W, n_rows * GRID_W)
    even = jnp.pad(flat, ((0, 0), (0, 0), (0, BIAS_LANES - n_rows * GRID_W)))
    odd = jnp.pad(flat[:, :, GRID_W:], ((0, 0), (0, 0), (0, BIAS_LANES - (n_rows - 1) * GRID_W)))

    def window_kernel(even_ref, odd_ref, o_ref):
        start = WIN_H - 1 - pl.program_id(0)
        base = pl.multiple_of(lax.shift_right_logical(start, 1) * LANES, LANES)
        width = WIN_H * GRID_W
        o_ref[0] = jnp.where((start & 1) == 0, even_ref[:, :, pl.ds(base, width)], odd_ref[:, :, pl.ds(base, width)])

    full = pl.BlockSpec(even.shape, lambda v: (0, 0, 0))
    return pl.pallas_call(
        window_kernel,
        grid=(WIN_H,),
        in_specs=[full, full],
        out_specs=pl.BlockSpec((1, N_HEADS // 2, 2 * GRID_W, WIN_H * GRID_W), lambda v: (v, 0, 0, 0)),
        out_shape=jax.ShapeDtypeStruct((WIN_H, N_HEADS // 2, 2 * GRID_W, WIN_H * GRID_W), jnp.float32),
        compiler_params=pltpu.CompilerParams(dimension_semantics=("arbitrary",), vmem_limit_bytes=VMEM_LIMIT),
        name="bias_table",
    )(even, odd)


def _mix_kernel(h_ref, mod_ref,
                q_ref, kp_ref, kc_ref, kn_ref, vp_ref, vc_ref, vn_ref,
                pp_ref, pc_ref, pn_ref, ga_ref, gb_ref,
                kvc_ref, bias_ref, wgrp_ref, pscale_ref, wap_ref, wpp_ref, wout_ref,
                ln1g_ref, ln1b_ref,
                o_ref,
                kbuf, vbuf, yabuf, pbuf, ypbuf, zbuf):
    b = pl.program_id(0)
    nb = pl.num_programs(0)

    kbuf[0:KV_HALO, :] = kp_ref[...]
    kbuf[KV_HALO:KV_HALO + MIX_TQ, :] = kc_ref[...]
    kbuf[KV_HALO + MIX_TQ:, :] = kn_ref[...]
    vbuf[0:KV_HALO, :] = vp_ref[...]
    vbuf[KV_HALO:KV_HALO + MIX_TQ, :] = vc_ref[...]
    vbuf[KV_HALO + MIX_TQ:, :] = vn_ref[...]

    lane = lax.broadcasted_iota(jnp.int32, (GRID_W, LANES), 1)
    first_head = lane < HEAD_DIM

    units = [(j, pair) for j in range(MIX_ROWS) for pair in range(N_HEADS // 2)]
    nt = (((1,), (1,)), ((), ()))

    def window(j):
        r = b * MIX_ROWS + j
        rs = jnp.clip(r - WIN_H // 2, 0, ROWS - WIN_H)
        return pl.multiple_of((rs - b * MIX_ROWS + WIN_H // 2) * GRID_W, GRID_W), r - rs

    def scores(j, pair):
        off, var = window(j)
        cols = slice(pair * LANES, (pair + 1) * LANES)
        q = q_ref[j * GRID_W:(j + 1) * GRID_W, cols]
        zero = jnp.zeros_like(q)
        q2 = jnp.concatenate([jnp.where(first_head, q, zero), jnp.where(first_head, zero, q)], axis=0)
        kw = kbuf[pl.ds(off, WIN_H * GRID_W), cols]
        s_loc = lax.dot_general(q2, kw, nt, preferred_element_type=jnp.float32) + bias_ref[var, pair]
        s_ctx = lax.dot_general(q2, kvc_ref[:, cols], nt, preferred_element_type=jnp.float32)
        return s_loc, s_ctx

    def values(j, pair, s_loc, s_ctx):
        off, _ = window(j)
        cols = slice(pair * LANES, (pair + 1) * LANES)
        vw = vbuf[pl.ds(off, WIN_H * GRID_W), cols]
        vctx = kvc_ref[:, ATT_W + pair * LANES:ATT_W + (pair + 1) * LANES]
        m = jnp.maximum(jnp.max(s_loc, axis=-1, keepdims=True), jnp.max(s_ctx, axis=-1, keepdims=True))
        p_loc = jnp.exp(s_loc - m).astype(jnp.bfloat16)
        p_ctx = jnp.exp(s_ctx - m).astype(jnp.bfloat16)
        ones = lambda rows: jnp.ones((rows, LANES), jnp.bfloat16)
        o2 = (_bdot(p_loc, jnp.concatenate([vw, ones(WIN_H * GRID_W)], axis=1))
              + _bdot(p_ctx, jnp.concatenate([vctx, ones(CTX_LEN)], axis=1)))
        o2 = o2[:, :LANES] / o2[:, LANES:]
        o_pair = jnp.where(first_head, o2[:GRID_W], o2[GRID_W:])
        yabuf[j * GRID_W:(j + 1) * GRID_W, cols] = o_pair.astype(jnp.bfloat16)

    pbuf[0:POOL_HALO, :] = jnp.where(b > 0, pp_ref[...].astype(jnp.float32), 0.0)
    pbuf[POOL_HALO:POOL_HALO + MIX_TQ, :] = pc_ref[...].astype(jnp.float32)
    pbuf[POOL_HALO + MIX_TQ:, :] = jnp.where(b < nb - 1, pn_ref[...].astype(jnp.float32), 0.0)
    edge = lax.broadcasted_iota(jnp.int32, (ROW_TILE, 1), 0)

    def pool_group(g):
        win = POOL_WINDOWS[g]
        lo, hi = win // 2, win - win // 2
        cols = slice(g * POOL_DIM, (g + 1) * POOL_DIM)
        acc = None
        for d in range(-lo, hi):
            term = pbuf[POOL_HALO + d:POOL_HALO + d + MIX_TQ, cols]
            acc = term if acc is None else acc + term
        assert max(lo, hi) <= ROW_TILE
        top = jnp.where(b == 0, 1.0 / (win - jnp.maximum(lo - edge, 0)).astype(jnp.float32), 1.0 / win)
        bot = jnp.where(b == nb - 1,
                        1.0 / (win - jnp.maximum(edge + hi - ROW_TILE, 0)).astype(jnp.float32), 1.0 / win)
        inv = jnp.concatenate([top, jnp.full((MIX_TQ - 2 * ROW_TILE, 1), 1.0 / win, jnp.float32), bot], axis=0)
        pooled = acc * inv - pbuf[POOL_HALO:POOL_HALO + MIX_TQ, cols]
        yp = _bdot(pooled.astype(jnp.bfloat16), wgrp_ref[g]) * pscale_ref[:, cols]
        ypbuf[:, cols] = yp.astype(jnp.bfloat16)

    def pooled_branch():
        zbuf[...] = gb_ref[...].astype(jnp.float32) * _bdot(ypbuf[...], wpp_ref[...])

    extra = {}
    for g in range(POOL_GROUPS):
        extra[(g + 1) * len(units) // (POOL_GROUPS + 2)] = functools.partial(pool_group, g)
    extra[(POOL_GROUPS + 1) * len(units) // (POOL_GROUPS + 2)] = pooled_branch
    pending = [scores(*u) for u in units[:ATTN_AHEAD]]
    for n, u in enumerate(units):
        if n + ATTN_AHEAD < len(units):
            pending.append(scores(*units[n + ATTN_AHEAD]))
        values(*u, *pending.pop(0))
        if n in extra:
            extra[n]()

    g1 = mod_ref[0:1, 2 * D_MODEL:3 * D_MODEL]
    n_chunks = MIX_TQ // MERGE_ROWS
    rows = lambda c: slice(c * MERGE_ROWS, (c + 1) * MERGE_ROWS)
    z, y = {}, {}

    def stage_a(c):
        ya = _bdot(yabuf[rows(c), :], wap_ref[...])
        z[c] = (ga_ref[rows(c), :].astype(jnp.float32) * ya + zbuf[rows(c), :]).astype(jnp.bfloat16)

    def stage_b(c):
        y[c] = _bdot(z.pop(c), wout_ref[...])

    def stage_c(c):
        hn = _layer_norm(DEEPNORM_ALPHA * h_ref[rows(c), :] + g1 * y.pop(c), ln1g_ref[...], ln1b_ref[...])
        for j in range(ROW_TILE):
            o_ref[pl.ds(c * MERGE_ROWS * ROW_TILE + j, MERGE_ROWS, stride=ROW_TILE), :] = hn[:, j * LANES:(j + 1) * LANES]

    for t in range(n_chunks + 2):
        if t < n_chunks:
            stage_a(t)
        if 0 <= t - 1 < n_chunks:
            stage_b(t - 1)
        if 0 <= t - 2 < n_chunks:
            stage_c(t - 2)


def _mix_call(h, mod, u, kvc, bias, wgrp, pscale, wap, wpp, wout, ln1g, ln1b):
    nb = SEQ // MIX_TQ
    halo_per_blk = MIX_TQ // KV_HALO
    n_halo = SEQ // KV_HALO
    ph_per_blk = MIX_TQ // POOL_HALO
    n_ph = SEQ // POOL_HALO

    def const(shape):
        return pl.BlockSpec(shape, lambda i: (0,) * len(shape), pipeline_mode=pl.Buffered(1))

    def prev_halo(c):
        return pl.BlockSpec((KV_HALO, ATT_W), lambda i: (jnp.maximum(i * halo_per_blk - 1, 0), c))

    def next_halo(c):
        return pl.BlockSpec((KV_HALO, ATT_W), lambda i: (jnp.minimum((i + 1) * halo_per_blk, n_halo - 1), c))

    def cur(c):
        return pl.BlockSpec((MIX_TQ, ATT_W), lambda i: (i, c))

    in_specs = [
        pl.BlockSpec((MIX_TQ, D_MODEL), lambda i: (i, 0)),
        const(mod.shape),
        cur(0),
        prev_halo(1), cur(1), next_halo(1),
        prev_halo(2), cur(2), next_halo(2),
        pl.BlockSpec((POOL_HALO, POOL_W), lambda i: (jnp.maximum(i * ph_per_blk - 1, 0), 3)),
        cur(3),
        pl.BlockSpec((POOL_HALO, POOL_W), lambda i: (jnp.minimum((i + 1) * ph_per_blk, n_ph - 1), 3)),
        pl.BlockSpec((MIX_TQ, D_MODEL), lambda i: (i, 2)),
        pl.BlockSpec((MIX_TQ, D_MODEL), lambda i: (i, 3)),
        const(kvc.shape), const(bias.shape), const(wgrp.shape), const(pscale.shape),
        const(wap.shape), const(wpp.shape), const(wout.shape),
        const((1, D_MODEL)), const((1, D_MODEL)),
    ]
    return pl.pallas_call(
        _mix_kernel,
        grid=(nb,),
        in_specs=in_specs,
        out_specs=pl.BlockSpec((MIX_TQ * ROW_TILE, LANES), lambda i: (i, 0)),
        out_shape=jax.ShapeDtypeStruct((SEQ * ROW_TILE, LANES), jnp.float32),
        scratch_shapes=[
            pltpu.VMEM((MIX_TQ + 2 * KV_HALO, ATT_W), jnp.bfloat16),
            pltpu.VMEM((MIX_TQ + 2 * KV_HALO, ATT_W), jnp.bfloat16),
            pltpu.VMEM((MIX_TQ, ATT_W), jnp.bfloat16),
            pltpu.VMEM((MIX_TQ + 2 * POOL_HALO, POOL_W), jnp.float32),
            pltpu.VMEM((MIX_TQ, POOL_W), jnp.bfloat16),
            pltpu.VMEM((MIX_TQ, D_MODEL), jnp.float32),
        ],
        compiler_params=pltpu.CompilerParams(
            dimension_semantics=("arbitrary",), vmem_limit_bytes=VMEM_LIMIT),
        name="mix",
    )(h, mod, u, u, u, u, u, u, u, u, u, u, u, u,
      kvc, bias, wgrp, pscale, wap, wpp, wout, ln1g, ln1b)


ID_E0, ID_E1, ID_R0, ID_R1 = 0, 1, 4, 5
POS_PACKED = 0
POS_BITS = 16
PLAN_EXPERT, PLAN_VALID, PLAN_NACT, PLAN_NEXT = 0, 1, 2, 3
PLAN_W = 2 * LANES
CW_C0, CW_C1 = 0, 1


def _route_kernel(h_ref, mod_ref, wrt_ref, brt_ref, pos_ref, cw_ref, plan_ref, carry_ref, ids_all):
    i = pl.program_id(0)
    tm = ROUTE_TM

    @pl.when(i == 0)
    def _():
        carry_ref[...] = jnp.zeros_like(carry_ref)

    shift = mod_ref[0:1, 3 * D_MODEL:4 * D_MODEL]
    scale = mod_ref[0:1, 4 * D_MODEL:5 * D_MODEL]
    hm = _load_row_tiles(h_ref, tm) * (1.0 + scale) + shift

    hm_hi, hm_lo = _split_bf16(hm)
    w_hi, w_lo = _split_bf16(wrt_ref[...])
    nt = (((1,), (1,)), ((), ()))
    dg = functools.partial(lax.dot_general, dimension_numbers=nt, preferred_element_type=jnp.float32)
    logits = dg(w_hi, hm_hi) + (dg(w_hi, hm_lo) + dg(w_lo, hm_hi)) + brt_ref[:, 0:1]

    sub = lax.broadcasted_iota(jnp.int32, (ROUTE_LOGIT_ROWS, tm), 0)
    big = jnp.int32(1 << 20)
    is_grp = sub < N_GROUPS
    gl = jnp.where(is_grp, logits, -jnp.inf)
    gmax = jnp.max(gl, axis=0, keepdims=True)
    gidx = jnp.min(jnp.where(gl == gmax, sub, big), axis=0, keepdims=True)
    gsum = jnp.sum(jnp.where(is_grp, jnp.exp(logits - gmax), 0.0), axis=0, keepdims=True)
    p_group = 1.0 / gsum

    eid = sub - N_GROUPS
    sel = (eid >= 0) & (eid < N_EXPERTS) & (lax.shift_right_arithmetic(eid, 3) == gidx)
    el = jnp.where(sel, logits, -jnp.inf)
    l0 = jnp.max(el, axis=0, keepdims=True)
    i0 = jnp.min(jnp.where(el == l0, sub, big), axis=0, keepdims=True)
    el2 = jnp.where(sub == i0, -jnp.inf, el)
    l1 = jnp.max(el2, axis=0, keepdims=True)
    i1 = jnp.min(jnp.where(el2 == l1, sub, big), axis=0, keepdims=True)
    t = jnp.exp(l1 - l0)
    w0 = 1.0 / (1.0 + t)
    w1 = t / (1.0 + t)

    half_rows = jnp.where(i >= pl.num_programs(0) // SEQ_PARTS, N_EXPERTS, 0)
    i0 = i0 + half_rows
    i1 = i1 + half_rows
    subs = lax.broadcasted_iota(jnp.int32, (ROUTE_SEG_ROWS, tm), 0)
    onehot = jnp.where((subs == i0) | (subs == i1), 1.0, 0.0)
    rr = lax.broadcasted_iota(jnp.int32, (tm, tm), 0)
    cc = lax.broadcasted_iota(jnp.int32, (tm, tm), 1)
    earlier = jnp.where(rr < cc, 1.0, 0.0).astype(jnp.bfloat16)
    carry = carry_ref[:, 0:1]
    prefix = _bdot(onehot.astype(jnp.bfloat16), earlier) + carry
    r0 = jnp.sum(jnp.where(subs == i0, prefix, 0.0), axis=0, keepdims=True)
    r1 = jnp.sum(jnp.where(subs == i1, prefix, 0.0), axis=0, keepdims=True)
    total = jnp.broadcast_to(carry + jnp.sum(onehot, axis=1, keepdims=True), carry_ref.shape)
    carry_ref[...] = total

    sub8 = lax.broadcasted_iota(jnp.int32, (ROW_TILE, tm), 0)
    ids = jnp.zeros((ROW_TILE, tm), jnp.int32)
    for idx, val in ((ID_E0, i0 - N_GROUPS), (ID_E1, i1 - N_GROUPS),
                     (ID_R0, r0.astype(jnp.int32)), (ID_R1, r1.astype(jnp.int32))):
        ids = jnp.where(sub8 == idx, val, ids)
    ids_all[:, pl.ds(pl.multiple_of(i * tm, tm), tm)] = ids

    cw8 = jnp.where(sub8 == CW_C0, p_group * w0, jnp.where(sub8 == CW_C1, p_group * w1, 0.0))
    cw_ref[...] = jnp.concatenate([cw8, jnp.zeros((LANES - ROW_TILE, tm), jnp.float32)], axis=0).T

    @pl.when(i == pl.num_programs(0) - 1)
    def _():
        subq = lax.broadcasted_iota(jnp.int32, (LANES, LANES), 0)
        laneq = lax.broadcasted_iota(jnp.int32, (LANES, LANES), 1)
        total = jnp.concatenate([carry_ref[...], jnp.zeros((LANES - ROUTE_SEG_ROWS, LANES), jnp.float32)], axis=0)
        cnt = total.astype(jnp.int32)
        tiles = lax.shift_right_logical(cnt + (EXP_TM - 1), EXP_TM.bit_length() - 1).astype(jnp.float32)
        incl = jnp.where(laneq <= subq, 1.0, 0.0).astype(jnp.bfloat16)
        tile_end = _bdot(incl, tiles.astype(jnp.bfloat16))
        tile_start = tile_end - tiles
        seg = (tile_start * EXP_TM).astype(jnp.int32)
        nact = jnp.max(tile_end, axis=0, keepdims=True)

        ids_full = ids_all[...]
        look = jnp.zeros_like(ids_full)
        for e in range(N_SEG):
            look = jnp.where(ids_full == e, seg[N_GROUPS + e, 0], look)
        pos01 = look + pltpu.roll(ids_full, ID_R0 - ID_E0, axis=0)
        assert EXP_TILES * EXP_TM <= 1 << POS_BITS
        pos_ref[...] = pos01 | (pltpu.roll(pos01, ROW_TILE - 1, axis=0) << POS_BITS)

        subp = lax.broadcasted_iota(jnp.int32, (LANES, PLAN_W), 0)
        tile = lax.broadcasted_iota(jnp.int32, (LANES, PLAN_W), 1).astype(jnp.float32)
        is_exp = (subp >= N_GROUPS) & (subp < N_GROUPS + N_SEG)
        end_col = tile_end[:, 0:1]
        nact_s = nact[:, 0:1]
        te = jnp.sum(jnp.where(is_exp & (tile >= end_col), 1.0, 0.0), axis=0, keepdims=True)
        te_last = jnp.sum(jnp.where(is_exp & (nact_s - 1.0 >= end_col), 1.0, 0.0), axis=0, keepdims=True)[:, 0:1]
        tile_row = tile[0:1, :]
        te = jnp.minimum(jnp.where(tile_row < nact_s, te, te_last), N_SEG - 1.0)
        mine = (subp - N_GROUPS).astype(jnp.float32) == te
        cnt_sel = jnp.sum(jnp.where(mine, total[:, 0:1], 0.0), axis=0, keepdims=True)
        start_sel = jnp.sum(jnp.where(mine, tile_start[:, 0:1], 0.0), axis=0, keepdims=True)
        end_sel = jnp.sum(jnp.where(mine, end_col, 0.0), axis=0, keepdims=True)
        valid = jnp.clip(cnt_sel - (tile_row - start_sel) * EXP_TM, 0.0, float(EXP_TM))
        valid = jnp.where(tile_row < nact_s, valid, 0.0)
        subr = lax.broadcasted_iota(jnp.int32, (ROW_TILE, PLAN_W), 0)
        plan = jnp.where(subr == PLAN_EXPERT, te, jnp.where(subr == PLAN_VALID, valid,
                         jnp.where(subr == PLAN_NACT, nact_s, jnp.where(subr == PLAN_NEXT, end_sel, 0.0))))
        plan_ref[...] = plan.astype(jnp.int32)


def _route_call(h, mod, wrt, brt):
    tm = ROUTE_TM
    return pl.pallas_call(
        _route_kernel,
        grid=(SEQ // tm,),
        in_specs=[
            pl.BlockSpec((tm * ROW_TILE, LANES), lambda i: (i, 0)),
            pl.BlockSpec(mod.shape, lambda i: (0, 0)),
            pl.BlockSpec((ROUTE_LOGIT_ROWS, D_MODEL), lambda i: (0, 0)),
            pl.BlockSpec((ROUTE_LOGIT_ROWS, LANES), lambda i: (0, 0)),
        ],
        out_specs=[
            pl.BlockSpec((ROW_TILE, SEQ), lambda i: (0, 0)),
            pl.BlockSpec((tm, LANES), lambda i: (i, 0)),
            pl.BlockSpec((ROW_TILE, PLAN_W), lambda i: (0, 0)),
        ],
        out_shape=[
            jax.ShapeDtypeStruct((ROW_TILE, SEQ), jnp.int32),
            jax.ShapeDtypeStruct((SEQ, LANES), jnp.float32),
            jax.ShapeDtypeStruct((ROW_TILE, PLAN_W), jnp.int32),
        ],
        scratch_shapes=[pltpu.VMEM((ROUTE_SEG_ROWS, LANES), jnp.float32),
                        pltpu.VMEM((ROW_TILE, SEQ), jnp.int32)],
        compiler_params=pltpu.CompilerParams(
            dimension_semantics=("arbitrary",), vmem_limit_bytes=VMEM_LIMIT),
        name="route",
    )(h, mod, wrt, brt)


SRC_UNROLL = 8
EXP_CHUNK = 256
CAST_CHUNKS = 8
PREP_AFTER_DOWN = 1
TILE_ROWS = EXP_TM * ROW_TILE


def _experts_kernel(te_ref, tv_ref, nact_ref, tnext_ref, pos_ref,
                    h_hbm, mod_ref, wg_hbm, wu_hbm, wd_hbm,
                    y_ref,
                    src_ref, ord_ref, hres, xbuf, xmat, wgs, wus, wds,
                    wgb0, wub0, wdb0, wgb1, wub1, wdb1, rsem, wsem):
    i = pl.program_id(0)
    last = pl.num_programs(0) - 1
    nact = nact_ref[0]
    wsets = ((wgb0, wub0, wdb0), (wgb1, wub1, wdb1))
    xcur = lax.rem(i, 2)
    part = lax.shift_right_logical(te_ref[i], N_EXPERTS.bit_length() - 1)

    def weight_copies(segment, st):
        e = segment & (N_EXPERTS - 1)
        return [pltpu.make_async_copy(w_hbm.at[e], stage.at[st], wsem.at[st, n])
                for n, (w_hbm, stage) in enumerate(((wg_hbm, wgs), (wu_hbm, wus), (wd_hbm, wds)))]

    def tile_after(t):
        return tnext_ref[jnp.minimum(t, last)]

    def gather_row(tile, k, s):
        local = (src_ref[tile * EXP_TM + k] - part * PART_TOKENS) & (PART_TOKENS - 1)
        xbuf[s, k * ROW_TILE:(k + 1) * ROW_TILE, :] = hres[pl.ds(pl.multiple_of(local * ROW_TILE, ROW_TILE),
                                                             ROW_TILE), :]

    def gather_items(tile, s):
        return [functools.partial(gather_row, tile, k, s) for k in range(EXP_TM)]

    def prepare_input(xs):
        x = _load_row_tiles(xbuf, EXP_TM, lead=(xs,))
        shift = mod_ref[0:1, 3 * D_MODEL:4 * D_MODEL]
        scale = mod_ref[0:1, 4 * D_MODEL:5 * D_MODEL]
        xmat[xs] = (x * (1.0 + scale) + shift).astype(jnp.bfloat16)

    def compute_chunks(xs, ws):
        wgb, wub, wdb = wsets[ws]

        state = {"act": []}

        def gate(c):
            def run():
                state["a"] = _bdot(xmat[xs], wgb[:, c * EXP_CHUNK:(c + 1) * EXP_CHUNK])
            return run

        def up(c):
            def run():
                a = state["a"]
                u = _bdot(xmat[xs], wub[:, c * EXP_CHUNK:(c + 1) * EXP_CHUNK])
                state["act"].append((a * jax.nn.sigmoid(a) * u).astype(jnp.bfloat16))
            return run

        def down(c):
            def run():
                if c == 0:
                    state["actf"] = jnp.concatenate(state["act"], axis=-1)
                yc = _bdot(state["actf"], wdb[:, c * EXP_CHUNK:(c + 1) * EXP_CHUNK])
                for jj in range(EXP_CHUNK // LANES):
                    j = c * (EXP_CHUNK // LANES) + jj
                    y_ref[pl.ds(j, EXP_TM, stride=ROW_TILE), :] = yc[:, jj * LANES:(jj + 1) * LANES]
            return run

        first = []
        for c in range(D_EXPERT // EXP_CHUNK):
            first += [gate(c), up(c)]
        return first, [down(c) for c in range(D_MODEL // EXP_CHUNK)]

    def resident_copy(p):
        rows = PART_TOKENS * ROW_TILE
        return pltpu.make_async_copy(h_hbm.at[pl.ds(pl.multiple_of(p * rows, rows), rows), :], hres, rsem.at[0])

    @pl.when(i == 0)
    def _():
        resident_copy(part).start()
        ord_ref[0] = 0
        for cp in weight_copies(te_ref[0], 0):
            cp.start()
        second = tile_after(0)

        @pl.when(second < nact)
        def _():
            for cp in weight_copies(te_ref[jnp.minimum(second, last)], 1):
                cp.start()

        def fill_body(tt, c):
            ts = [tt * SRC_UNROLL + u for u in range(SRC_UNROLL)]
            words = [pos_ref[t] for t in ts]
            for t, w in zip(ts, words):
                src_ref[w & ((1 << POS_BITS) - 1)] = t
                src_ref[lax.shift_right_logical(w, POS_BITS)] = t
            return c
        lax.fori_loop(0, SEQ // SRC_UNROLL, fill_body, 0)

        def pad_tile(t, c):
            pad_tok = lax.shift_right_logical(te_ref[t], N_EXPERTS.bit_length() - 1) * PART_TOKENS

            def pad_row(k, c2):
                src_ref[t * EXP_TM + k] = pad_tok
                return c2
            return lax.fori_loop(tv_ref[t], EXP_TM, pad_row, c)
        lax.fori_loop(0, nact, pad_tile, 0)

    active = i < nact
    prev = jnp.maximum(i - 1, 0)
    new_segment = (i == 0) | (te_ref[i] != te_ref[prev])

    @pl.when(active & ((i == 0) | (part != lax.shift_right_logical(te_ref[prev], N_EXPERTS.bit_length() - 1))))
    def _():
        @pl.when(i > 0)
        def _():
            resident_copy(part).start()
        resident_copy(part).wait()
        for item in gather_items(i, xcur):
            item()
        prepare_input(xcur)

    @pl.when(active & new_segment & (i > 0))
    def _():
        ord_ref[0] = ord_ref[0] + 1

    parity = ord_ref[0] & 1
    after1 = tile_after(i)
    after2 = tile_after(after1)
    after3 = tile_after(after2)

    def cast_items(st):
        def block(stage, dst, rows):
            def run():
                dst[rows, :] = stage[st, rows, :].astype(jnp.bfloat16)
            return run
        out = []
        for r in range(CAST_CHUNKS):
            up_rows = slice(r * D_MODEL // CAST_CHUNKS, (r + 1) * D_MODEL // CAST_CHUNKS)
            dn_rows = slice(r * D_EXPERT // CAST_CHUNKS, (r + 1) * D_EXPERT // CAST_CHUNKS)
            out += [block(stage, dst, rows)
                    for stage, dst, rows in zip((wgs, wus, wds), wsets[st], (up_rows, up_rows, dn_rows))]
        return out

    @pl.when(i == 0)
    def _():
        for cp in weight_copies(te_ref[0], 0):
            cp.wait()
        for item in cast_items(0):
            item()

        @pl.when(after2 < nact)
        def _():
            for cp in weight_copies(te_ref[jnp.minimum(after2, last)], 0):
                cp.start()

    def run_tile(par, extra_items):
        first, second = compute_chunks(xcur, par)
        gathers = gather_items(jnp.minimum(i + 1, nact - 1), 1 - xcur)
        per = -(-len(gathers) // len(first))
        per_extra = -(-len(extra_items) // (len(first) + len(second)))
        for n, chunk in enumerate(first + second):
            if n < len(first):
                for item in gathers[n * per:(n + 1) * per]:
                    item()
            for item in extra_items[n * per_extra:(n + 1) * per_extra]:
                item()
            chunk()
            if n == len(first) + PREP_AFTER_DOWN:
                prepare_input(1 - xcur)

    last_of_segment = (i + 1 >= nact) | (te_ref[jnp.minimum(i + 1, last)] != te_ref[i])
    cast_next = active & last_of_segment & (after1 < nact)

    for par in range(2):
        @pl.when(cast_next & (parity == par))
        def _():
            for cp in weight_copies(te_ref[jnp.minimum(after1, last)], 1 - par):
                cp.wait()
            run_tile(par, cast_items(1 - par))

            @pl.when(after3 < nact)
            def _():
                for cp in weight_copies(te_ref[jnp.minimum(after3, last)], 1 - par):
                    cp.start()

        @pl.when(active & jnp.logical_not(cast_next) & (parity == par))
        def _():
            run_tile(par, [])

    @pl.when(jnp.logical_not(active))
    def _():
        y_ref[...] = jnp.zeros_like(y_ref)


def _experts_call(te, tv, nact, tnext, pos, h, mod, wg, wu, wd):
    grid_spec = pltpu.PrefetchScalarGridSpec(
        num_scalar_prefetch=5,
        grid=(EXP_TILES,),
        in_specs=[
            pl.BlockSpec(memory_space=pl.ANY),
            pl.BlockSpec(mod.shape, lambda i, *_: (0, 0)),
            pl.BlockSpec(memory_space=pl.ANY),
            pl.BlockSpec(memory_space=pl.ANY),
            pl.BlockSpec(memory_space=pl.ANY),
        ],
        out_specs=pl.BlockSpec((TILE_ROWS, LANES), lambda i, *_: (i, 0)),
        scratch_shapes=[
            pltpu.SMEM((EXP_TILES * EXP_TM,), jnp.int32),
            pltpu.SMEM((1,), jnp.int32),
            pltpu.VMEM((PART_TOKENS * ROW_TILE, LANES), jnp.float32),
            pltpu.VMEM((2, TILE_ROWS, LANES), jnp.float32),
            pltpu.VMEM((2, EXP_TM, D_MODEL), jnp.bfloat16),
            pltpu.VMEM((2, D_MODEL, D_EXPERT), jnp.float32),
            pltpu.VMEM((2, D_MODEL, D_EXPERT), jnp.float32),
            pltpu.VMEM((2, D_EXPERT, D_MODEL), jnp.float32),
            pltpu.VMEM((D_MODEL, D_EXPERT), jnp.bfloat16),
            pltpu.VMEM((D_MODEL, D_EXPERT), jnp.bfloat16),
            pltpu.VMEM((D_EXPERT, D_MODEL), jnp.bfloat16),
            pltpu.VMEM((D_MODEL, D_EXPERT), jnp.bfloat16),
            pltpu.VMEM((D_MODEL, D_EXPERT), jnp.bfloat16),
            pltpu.VMEM((D_EXPERT, D_MODEL), jnp.bfloat16),
            pltpu.SemaphoreType.DMA((1,)),
            pltpu.SemaphoreType.DMA((2, 3)),
        ],
    )
    return pl.pallas_call(
        _experts_kernel,
        grid_spec=grid_spec,
        out_shape=jax.ShapeDtypeStruct((EXP_TILES * TILE_ROWS, LANES), jnp.float32),
        compiler_params=pltpu.CompilerParams(
            dimension_semantics=("arbitrary",), vmem_limit_bytes=EXPERTS_VMEM_LIMIT),
        name="experts",
    )(te, tv, nact, tnext, pos, h, mod, wg, wu, wd)


def _combine_kernel(pos_ref, h_ref, cw_ref, mod_ref, g_ref, b_ref, ys_hbm, o_ref, ybuf, sem):
    i = pl.program_id(0)
    tm = CMB_TM
    slot = lax.rem(i, 2)

    def start_row(tile, k, s):
        word = pos_ref[tile * tm + k]
        for half, p in ((0, word & ((1 << POS_BITS) - 1)), (1, lax.shift_right_logical(word, POS_BITS))):
            pltpu.make_async_copy(ys_hbm.at[pl.ds(pl.multiple_of(p * ROW_TILE, ROW_TILE), ROW_TILE), :],
                                  ybuf.at[s, half, pl.ds(pl.multiple_of(k * ROW_TILE, ROW_TILE), ROW_TILE), :],
                                  sem.at[s]).start(priority=half)

    @pl.when(i == 0)
    def _():
        def body(kk, c):
            for u in range(SRC_UNROLL):
                start_row(0, kk * SRC_UNROLL + u, 0)
            return c
        lax.fori_loop(0, tm // SRC_UNROLL, body, 0)

    for half in range(2):
        pltpu.make_async_copy(ys_hbm.at[pl.ds(0, tm * ROW_TILE), :], ybuf.at[slot, half], sem.at[slot]).wait()

    g2 = mod_ref[0:1, 5 * D_MODEL:6 * D_MODEL]

    def load(c):
        rows = slice(c * CMB_CHUNK, (c + 1) * CMB_CHUNK)
        y0 = _load_row_tiles(ybuf, CMB_CHUNK, lead=(slot, 0), first=c * CMB_CHUNK)
        y1 = _load_row_tiles(ybuf, CMB_CHUNK, lead=(slot, 1), first=c * CMB_CHUNK)
        h = _load_row_tiles(h_ref, CMB_CHUNK, first=c * CMB_CHUNK)
        return y0, y1, h, cw_ref[rows, CW_C0:CW_C0 + 1], cw_ref[rows, CW_C1:CW_C1 + 1]

    def finish(c, y0, y1, h, c0, c1):
        rows = slice(c * CMB_CHUNK, (c + 1) * CMB_CHUNK)
        o_ref[rows, :] = _layer_norm(DEEPNORM_ALPHA * h + g2 * (c0 * y0 + c1 * y1), g_ref[...], b_ref[...])

    n_chunks = tm // CMB_CHUNK

    @pl.when(i + 1 < pl.num_programs(0))
    def _():
        per = tm // n_chunks
        for c in range(n_chunks):
            operands = load(c)
            for k in range(c * per, (c + 1) * per):
                start_row(i + 1, k, 1 - slot)
            finish(c, *operands)

    @pl.when(i + 1 == pl.num_programs(0))
    def _():
        for c in range(n_chunks):
            finish(c, *load(c))


def _combine_call(pos, h, ys, cw, mod, g, b):
    tm = CMB_TM
    grid_spec = pltpu.PrefetchScalarGridSpec(
        num_scalar_prefetch=1,
        grid=(SEQ // tm,),
        in_specs=[
            pl.BlockSpec((tm * ROW_TILE, LANES), lambda i, *_: (i, 0)),
            pl.BlockSpec((tm, LANES), lambda i, *_: (i, 0)),
            pl.BlockSpec(mod.shape, lambda i, *_: (0, 0)),
            pl.BlockSpec((1, D_MODEL), lambda i, *_: (0, 0)),
            pl.BlockSpec((1, D_MODEL), lambda i, *_: (0, 0)),
            pl.BlockSpec(memory_space=pl.ANY),
        ],
        out_specs=pl.BlockSpec((tm, D_MODEL), lambda i, *_: (i, 0)),
        scratch_shapes=[
            pltpu.VMEM((2, 2, tm * ROW_TILE, LANES), jnp.float32),
            pltpu.SemaphoreType.DMA((2,)),
        ],
    )
    return pl.pallas_call(
        _combine_kernel,
        grid_spec=grid_spec,
        out_shape=jax.ShapeDtypeStruct((SEQ, D_MODEL), jnp.float32),
        compiler_params=pltpu.CompilerParams(
            dimension_semantics=("arbitrary",), vmem_limit_bytes=VMEM_LIMIT),
        name="combine",
    )(pos, h, cw, mod, g, b, ys)


def kernel(x, c, ctx, c_ctx, ln_in_g, ln_in_b, w_mod, b_mod, w_in, rpb, w_pool_grp, pool_scale,
           w_attn_proj, w_pool_proj, w_out, ln1_g, ln1_b, w_router_group, b_router_group,
           w_router_expert, b_router_expert, w_expert_gate, w_expert_up, w_expert_down, ln2_g, ln2_b):
    assert x.shape == (1, SEQ, D_MODEL) and ctx.shape == (1, CTX_LEN, D_MODEL)
    assert w_mod.shape[0] == 1, "single-layer trunk"
    f32, bf16 = jnp.float32, jnp.bfloat16
    row = lambda v: v.reshape(1, -1).astype(f32)

    cond = jnp.concatenate([c, c_ctx[None], jnp.zeros((MOD_ROWS - 2, D_MODEL), f32)], axis=0)
    mod = _mod_call(cond, w_mod[0], row(b_mod[0]))

    lng, lnb = row(ln_in_g), row(ln_in_b)
    w_in_b = w_in[0].astype(bf16)
    u, h0 = _proj_call(x[0], mod, lng, lnb, w_in_b, mod_row=0, latent=True, tm=PROJ_TM)
    kvc, = _proj_call(ctx[0], mod, lng, lnb, w_in_b[:, ATT_W:3 * ATT_W], mod_row=1, latent=False, tm=CTX_LEN)

    h1 = _mix_call(h0, mod, u, kvc, _attn_bias_table(rpb[0]),
                   w_pool_grp[0].astype(bf16), row(pool_scale[0]),
                   w_attn_proj[0].astype(bf16), w_pool_proj[0].astype(bf16), w_out[0].astype(bf16),
                   row(ln1_g[0]), row(ln1_b[0]))

    n_logit = N_GROUPS + N_EXPERTS
    wrt = jnp.concatenate([w_router_group[0].T, w_router_expert[0].T,
                           jnp.zeros((ROUTE_LOGIT_ROWS - n_logit, D_MODEL), f32)], axis=0)
    brt = jnp.concatenate([b_router_group[0], b_router_expert[0], jnp.zeros((ROUTE_LOGIT_ROWS - n_logit,), f32)])
    brt = jnp.broadcast_to(brt[:, None], (ROUTE_LOGIT_ROWS, LANES))
    pos, cw, plan = _route_call(h1, mod, wrt, brt)

    posw = pos[POS_PACKED]
    y = _experts_call(plan[PLAN_EXPERT, :EXP_TILES], plan[PLAN_VALID, :EXP_TILES], plan[PLAN_NACT, :1],
                      plan[PLAN_NEXT, :EXP_TILES], posw, h1, mod,
                      w_expert_gate[0], w_expert_up[0], w_expert_down[0])
    out = _combine_call(posw, h1, y, cw, mod, row(ln2_g[0]), row(ln2_b[0]))
    return out[None]
```

```python
import functools

import jax
import jax.numpy as jnp
from jax import lax
from jax.experimental import pallas as pl
from jax.experimental.pallas import tpu as pltpu

D_MODEL = 1024
SEQ = 16384
GRID_W = 64
ROWS = SEQ // GRID_W
CTX_LEN = 256
N_HEADS = 8
HEAD_DIM = 64
ATT_W = N_HEADS * HEAD_DIM
WIN_H = 8
WIN_W = 16
POOL_WINDOWS = (2, 4, 8, 16)
POOL_GROUPS = 4
POOL_DIM = 128
POOL_W = POOL_GROUPS * POOL_DIM
PROJ_W = 3 * ATT_W + POOL_W + 2 * D_MODEL
GATE_COL = 3 * ATT_W + POOL_W
N_GROUPS = 4
EXPERTS_PER_GROUP = 8
N_EXPERTS = N_GROUPS * EXPERTS_PER_GROUP
D_EXPERT = 512
N_MOD = 6
DEEPNORM_ALPHA = 2.0 ** 0.25
LN_EPS = 1e-5
NEG_INF = -1e30

LANES = 128
ROW_TILE = 8
MOD_ROWS = 8
PROJ_TM = 1024
PROJ_SUB = 256
MIX_ROWS = 8
MIX_TQ = MIX_ROWS * GRID_W
KV_HALO = 4 * GRID_W
POOL_HALO = 16
ROUTE_TM = 1024
ROUTE_LOGIT_ROWS = 40
ROUTE_SEG_ROWS = 72
EXP_TM = 256
SEQ_PARTS = 2
PART_TOKENS = SEQ // SEQ_PARTS
N_SEG = SEQ_PARTS * N_EXPERTS
EXP_TILES = 2 * SEQ // EXP_TM + N_SEG
CMB_TM = 512
CMB_CHUNK = 128
MERGE_ROWS = 256
BIAS_LANES = 1024
ATTN_AHEAD = 3
HALF = D_MODEL // 2
VMEM_LIMIT = 56 * 1024 * 1024
EXPERTS_VMEM_LIMIT = 60 * 1024 * 1024


def _layer_norm(x, g, b):
    mu = jnp.mean(x, axis=-1, keepdims=True)
    xc = x - mu
    var = jnp.mean(xc * xc, axis=-1, keepdims=True)
    return xc * lax.rsqrt(var + LN_EPS) * g + b


def _bdot(a, b):
    return jnp.dot(a, b, preferred_element_type=jnp.float32)


def _split_bf16(a):
    hi = a.astype(jnp.bfloat16)
    lo = (a - hi.astype(jnp.float32)).astype(jnp.bfloat16)
    return hi, lo


def _dot3(a, b):
    a_hi, a_lo = _split_bf16(a)
    b_hi, b_lo = _split_bf16(b)
    return _bdot(a_hi, b_hi) + (_bdot(a_hi, b_lo) + _bdot(a_lo, b_hi))


def _load_row_tiles(ref, tokens, lead=(), first=0):
    parts = [ref[(*lead, pl.ds(first * ROW_TILE + j, tokens, stride=ROW_TILE), slice(None))]
             for j in range(ROW_TILE)]
    return jnp.concatenate(parts, axis=-1)


def _store_row_tiles(ref, value, lead=()):
    tokens = value.shape[0]
    for j in range(ROW_TILE):
        ref[(*lead, pl.ds(j, tokens, stride=ROW_TILE), slice(None))] = value[:, j * LANES:(j + 1) * LANES]


def _mod_kernel(cond_ref, w_ref, b_ref, o_ref):
    cond = cond_ref[...]
    act = cond * jax.nn.sigmoid(cond)
    o_ref[...] = _dot3(act, w_ref[...]) + b_ref[...]


def _mod_call(cond, w_mod, b_mod):
    tn = 1536
    n = N_MOD * D_MODEL
    return pl.pallas_call(
        _mod_kernel,
        grid=(n // tn,),
        in_specs=[
            pl.BlockSpec((MOD_ROWS, D_MODEL), lambda i: (0, 0)),
            pl.BlockSpec((D_MODEL, tn), lambda i: (0, i)),
            pl.BlockSpec((1, tn), lambda i: (0, i)),
        ],
        out_specs=pl.BlockSpec((MOD_ROWS, tn), lambda i: (0, i)),
        out_shape=jax.ShapeDtypeStruct((MOD_ROWS, n), jnp.float32),
        compiler_params=pltpu.CompilerParams(
            dimension_semantics=("arbitrary",), vmem_limit_bytes=VMEM_LIMIT),
        name="mod",
    )(cond, w_mod, b_mod)


def _proj_kernel(x_ref, mod_ref, g_ref, b_ref, w_ref, o_ref, *h_out, mod_row, latent):
    shift = mod_ref[mod_row:mod_row + 1, 0:D_MODEL]
    scale = mod_ref[mod_row:mod_row + 1, D_MODEL:2 * D_MODEL]
    tm, n = o_ref.shape
    sub = min(tm, PROJ_SUB)

    def prep(r):
        rows = slice(r * sub, (r + 1) * sub)
        h = _layer_norm(x_ref[rows, :], g_ref[...], b_ref[...])
        if latent:
            h_out[0][rows, :] = h
        return (h * (1.0 + scale) + shift).astype(jnp.bfloat16)

    def finish(r, c, res):
        if latent and c == 0:
            lane = lax.broadcasted_iota(jnp.int32, (1, D_MODEL), 1)
            res = res * jnp.where(lane < ATT_W, HEAD_DIM ** -0.5, 1.0)
        if latent and c * D_MODEL >= GATE_COL:
            res = jax.nn.sigmoid(res)
        o_ref[r * sub:(r + 1) * sub, c * D_MODEL:(c + 1) * D_MODEL] = res.astype(jnp.bfloat16)

    hm = {0: prep(0)}
    waiting = None
    for r in range(tm // sub):
        for c in range(n // D_MODEL):
            res = _bdot(hm[r], w_ref[:, c * D_MODEL:(c + 1) * D_MODEL])
            if c == 0 and (r + 1) * sub < tm:
                hm[r + 1] = prep(r + 1)
            if waiting is not None:
                finish(*waiting)
            waiting = (r, c, res)
    finish(*waiting)


def _proj_call(x, mod, g, b, w, *, mod_row, latent, tm):
    rows, n = x.shape[0], w.shape[1]
    out_specs = [pl.BlockSpec((tm, n), lambda i: (i, 0))]
    out_shape = [jax.ShapeDtypeStruct((rows, n), jnp.bfloat16)]
    if latent:
        out_specs.append(pl.BlockSpec((tm, D_MODEL), lambda i: (i, 0)))
        out_shape.append(jax.ShapeDtypeStruct((rows, D_MODEL), jnp.float32))
    return pl.pallas_call(
        functools.partial(_proj_kernel, mod_row=mod_row, latent=latent),
        grid=(rows // tm,),
        in_specs=[
            pl.BlockSpec((tm, D_MODEL), lambda i: (i, 0)),
            pl.BlockSpec(mod.shape, lambda i: (0, 0)),
            pl.BlockSpec((1, D_MODEL), lambda i: (0, 0)),
            pl.BlockSpec((1, D_MODEL), lambda i: (0, 0)),
            pl.BlockSpec((D_MODEL, n), lambda i: (0, 0), pipeline_mode=pl.Buffered(1)),
        ],
        out_specs=out_specs,
        out_shape=out_shape,
        compiler_params=pltpu.CompilerParams(
            dimension_semantics=("arbitrary",), vmem_limit_bytes=VMEM_LIMIT),
        name="proj",
    )(x, mod, g, b, w)


def _attn_bias_table(rpb):
    col = jnp.arange(GRID_W, dtype=jnp.int32)
    col_start = jnp.clip(col - WIN_W // 2, 0, GRID_W - WIN_W)
    col_mask = (col[None, :] >= col_start[:, None]) & (col[None, :] < col_start[:, None] + WIN_W)
    col_off = jnp.clip(col[None, :] - col[:, None], 1 - WIN_W, WIN_W - 1) + (WIN_W - 1)
    onehot = (col_off[None] == jnp.arange(2 * WIN_W - 1, dtype=jnp.int32)[:, None, None]).astype(jnp.float32)
    tab = jnp.einsum("hrc,cqk->hqrk", rpb.astype(jnp.float32), onehot, precision=lax.Precision.HIGHEST)
    tab = jnp.where(col_mask[None, :, None, :], tab, NEG_INF)
    n_rows = 2 * WIN_H - 1
    flat = tab.reshape(N_HEADS // 2, 2 * GRID_W, n_rows * GRID_W)
    even = jnp.pad(flat, ((0, 0), (0, 0), (0, BIAS_LANES - n_rows * GRID_W)))
    odd = jnp.pad(flat[:, :, GRID_W:], ((0, 0), (0, 0), (0, BIAS_LANES - (n_rows - 1) * GRID_W)))

    def window_kernel(even_ref, odd_ref, o_ref):
        start = WIN_H - 1 - pl.program_id(0)
        base = pl.multiple_of(lax.shift_right_logical(start, 1) * LANES, LANES)
        width = WIN_H * GRID_W
        o_ref[0] = jnp.where((start & 1) == 0, even_ref[:, :, pl.ds(base, width)], odd_ref[:, :, pl.ds(base, width)])

    full = pl.BlockSpec(even.shape, lambda v: (0, 0, 0))
    return pl.pallas_call(
        window_kernel,
        grid=(WIN_H,),
        in_specs=[full, full],
        out_specs=pl.BlockSpec((1, N_HEADS // 2, 2 * GRID_W, WIN_H * GRID_W), lambda v: (v, 0, 0, 0)),
        out_shape=jax.ShapeDtypeStruct((WIN_H, N_HEADS // 2, 2 * GRID_W, WIN_H * GRID_W), jnp.float32),
        compiler_params=pltpu.CompilerParams(dimension_semantics=("arbitrary",), vmem_limit_bytes=VMEM_LIMIT),
        name="bias_table",
    )(even, odd)


def _mix_kernel(h_ref, mod_ref,
                q_ref, kp_ref, kc_ref, kn_ref, vp_ref, vc_ref, vn_ref,
                pp_ref, pc_ref, pn_ref, ga_ref, gb_ref,
                kvc_ref, bias_ref, wgrp_ref, pscale_ref, wap_ref, wpp_ref, wout_ref,
                ln1g_ref, ln1b_ref,
                o_ref,
                kbuf, vbuf, yabuf, pbuf, ypbuf, zbuf):
    b = pl.program_id(0)
    nb = pl.num_programs(0)

    kbuf[0:KV_HALO, :] = kp_ref[...]
    kbuf[KV_HALO:KV_HALO + MIX_TQ, :] = kc_ref[...]
    kbuf[KV_HALO + MIX_TQ:, :] = kn_ref[...]
    vbuf[0:KV_HALO, :] = vp_ref[...]
    vbuf[KV_HALO:KV_HALO + MIX_TQ, :] = vc_ref[...]
    vbuf[KV_HALO + MIX_TQ:, :] = vn_ref[...]

    lane = lax.broadcasted_iota(jnp.int32, (GRID_W, LANES), 1)
    first_head = lane < HEAD_DIM

    units = [(j, pair) for j in range(MIX_ROWS) for pair in range(N_HEADS // 2)]
    nt = (((1,), (1,)), ((), ()))

    def window(j):
        r = b * MIX_ROWS + j
        rs = jnp.clip(r - WIN_H // 2, 0, ROWS - WIN_H)
        return pl.multiple_of((rs - b * MIX_ROWS + WIN_H // 2) * GRID_W, GRID_W), r - rs

    def scores(j, pair):
        off, var = window(j)
        cols = slice(pair * LANES, (pair + 1) * LANES)
        q = q_ref[j * GRID_W:(j + 1) * GRID_W, cols]
        zero = jnp.zeros_like(q)
        q2 = jnp.concatenate([jnp.where(first_head, q, zero), jnp.where(first_head, zero, q)], axis=0)
        kw = kbuf[pl.ds(off, WIN_H * GRID_W), cols]
        s_loc = lax.dot_general(q2, kw, nt, preferred_element_type=jnp.float32) + bias_ref[var, pair]
        s_ctx = lax.dot_general(q2, kvc_ref[:, cols], nt, preferred_element_type=jnp.float32)
        return s_loc, s_ctx

    def values(j, pair, s_loc, s_ctx):
        off, _ = window(j)
        cols = slice(pair * LANES, (pair + 1) * LANES)
        vw = vbuf[pl.ds(off, WIN_H * GRID_W), cols]
        vctx = kvc_ref[:, ATT_W + pair * LANES:ATT_W + (pair + 1) * LANES]
        m = jnp.maximum(jnp.max(s_loc, axis=-1, keepdims=True), jnp.max(s_ctx, axis=-1, keepdims=True))
        p_loc = jnp.exp(s_loc - m).astype(jnp.bfloat16)
        p_ctx = jnp.exp(s_ctx - m).astype(jnp.bfloat16)
        ones = lambda rows: jnp.ones((rows, LANES), jnp.bfloat16)
        o2 = (_bdot(p_loc, jnp.concatenate([vw, ones(WIN_H * GRID_W)], axis=1))
              + _bdot(p_ctx, jnp.concatenate([vctx, ones(CTX_LEN)], axis=1)))
        o2 = o2[:, :LANES] / o2[:, LANES:]
        o_pair = jnp.where(first_head, o2[:GRID_W], o2[GRID_W:])
        yabuf[j * GRID_W:(j + 1) * GRID_W, cols] = o_pair.astype(jnp.bfloat16)

    pbuf[0:POOL_HALO, :] = jnp.where(b > 0, pp_ref[...].astype(jnp.float32), 0.0)
    pbuf[POOL_HALO:POOL_HALO + MIX_TQ, :] = pc_ref[...].astype(jnp.float32)
    pbuf[POOL_HALO + MIX_TQ:, :] = jnp.where(b < nb - 1, pn_ref[...].astype(jnp.float32), 0.0)
    edge = lax.broadcasted_iota(jnp.int32, (ROW_TILE, 1), 0)

    def pool_group(g):
        win = POOL_WINDOWS[g]
        lo, hi = win // 2, win - win // 2
        cols = slice(g * POOL_DIM, (g + 1) * POOL_DIM)
        acc = None
        for d in range(-lo, hi):
            term = pbuf[POOL_HALO + d:POOL_HALO + d + MIX_TQ, cols]
            acc = term if acc is None else acc + term
        assert max(lo, hi) <= ROW_TILE
        top = jnp.where(b == 0, 1.0 / (win - jnp.maximum(lo - edge, 0)).astype(jnp.float32), 1.0 / win)
        bot = jnp.where(b == nb - 1,
                        1.0 / (win - jnp.maximum(edge + hi - ROW_TILE, 0)).astype(jnp.float32), 1.0 / win)
        inv = jnp.concatenate([top, jnp.full((MIX_TQ - 2 * ROW_TILE, 1), 1.0 / win, jnp.float32), bot], axis=0)
        pooled = acc * inv - pbuf[POOL_HALO:POOL_HALO + MIX_TQ, cols]
        yp = _bdot(pooled.astype(jnp.bfloat16), wgrp_ref[g]) * pscale_ref[:, cols]
        ypbuf[:, cols] = yp.astype(jnp.bfloat16)

    def pooled_branch():
        zbuf[...] = gb_ref[...].astype(jnp.float32) * _bdot(ypbuf[...], wpp_ref[...])

    extra = {}
    for g in range(POOL_GROUPS):
        extra[(g + 1) * len(units) // (POOL_GROUPS + 2)] = functools.partial(pool_group, g)
    extra[(POOL_GROUPS + 1) * len(units) // (POOL_GROUPS + 2)] = pooled_branch
    pending = [scores(*u) for u in units[:ATTN_AHEAD]]
    for n, u in enumerate(units):
        if n + ATTN_AHEAD < len(units):
            pending.append(scores(*units[n + ATTN_AHEAD]))
        values(*u, *pending.pop(0))
        if n in extra:
            extra[n]()

    g1 = mod_ref[0:1, 2 * D_MODEL:3 * D_MODEL]
    n_chunks = MIX_TQ // MERGE_ROWS
    rows = lambda c: slice(c * MERGE_ROWS, (c + 1) * MERGE_ROWS)
    z, y = {}, {}

    def stage_a(c):
        ya = _bdot(yabuf[rows(c), :], wap_ref[...])
        z[c] = (ga_ref[rows(c), :].astype(jnp.float32) * ya + zbuf[rows(c), :]).astype(jnp.bfloat16)

    def stage_b(c):
        y[c] = _bdot(z.pop(c), wout_ref[...])

    def stage_c(c):
        hn = _layer_norm(DEEPNORM_ALPHA * h_ref[rows(c), :] + g1 * y.pop(c), ln1g_ref[...], ln1b_ref[...])
        for j in range(ROW_TILE):
            o_ref[pl.ds(c * MERGE_ROWS * ROW_TILE + j, MERGE_ROWS, stride=ROW_TILE), :] = hn[:, j * LANES:(j + 1) * LANES]

    for t in range(n_chunks + 2):
        if t < n_chunks:
            stage_a(t)
        if 0 <= t - 1 < n_chunks:
            stage_b(t - 1)
        if 0 <= t - 2 < n_chunks:
            stage_c(t - 2)


def _mix_call(h, mod, u, kvc, bias, wgrp, pscale, wap, wpp, wout, ln1g, ln1b):
    nb = SEQ // MIX_TQ
    halo_per_blk = MIX_TQ // KV_HALO
    n_halo = SEQ // KV_HALO
    ph_per_blk = MIX_TQ // POOL_HALO
    n_ph = SEQ // POOL_HALO

    def const(shape):
        return pl.BlockSpec(shape, lambda i: (0,) * len(shape), pipeline_mode=pl.Buffered(1))

    def prev_halo(c):
        return pl.BlockSpec((KV_HALO, ATT_W), lambda i: (jnp.maximum(i * halo_per_blk - 1, 0), c))

    def next_halo(c):
        return pl.BlockSpec((KV_HALO, ATT_W), lambda i: (jnp.minimum((i + 1) * halo_per_blk, n_halo - 1), c))

    def cur(c):
        return pl.BlockSpec((MIX_TQ, ATT_W), lambda i: (i, c))

    in_specs = [
        pl.BlockSpec((MIX_TQ, D_MODEL), lambda i: (i, 0)),
        const(mod.shape),
        cur(0),
        prev_halo(1), cur(1), next_halo(1),
        prev_halo(2), cur(2), next_halo(2),
        pl.BlockSpec((POOL_HALO, POOL_W), lambda i: (jnp.maximum(i * ph_per_blk - 1, 0), 3)),
        cur(3),
        pl.BlockSpec((POOL_HALO, POOL_W), lambda i: (jnp.minimum((i + 1) * ph_per_blk, n_ph - 1), 3)),
        pl.BlockSpec((MIX_TQ, D_MODEL), lambda i: (i, 2)),
        pl.BlockSpec((MIX_TQ, D_MODEL), lambda i: (i, 3)),
        const(kvc.shape), const(bias.shape), const(wgrp.shape), const(pscale.shape),
        const(wap.shape), const(wpp.shape), const(wout.shape),
        const((1, D_MODEL)), const((1, D_MODEL)),
    ]
    return pl.pallas_call(
        _mix_kernel,
        grid=(nb,),
        in_specs=in_specs,
        out_specs=pl.BlockSpec((MIX_TQ * ROW_TILE, LANES), lambda i: (i, 0)),
        out_shape=jax.ShapeDtypeStruct((SEQ * ROW_TILE, LANES), jnp.float32),
        scratch_shapes=[
            pltpu.VMEM((MIX_TQ + 2 * KV_HALO, ATT_W), jnp.bfloat16),
            pltpu.VMEM((MIX_TQ + 2 * KV_HALO, ATT_W), jnp.bfloat16),
            pltpu.VMEM((MIX_TQ, ATT_W), jnp.bfloat16),
            pltpu.VMEM((MIX_TQ + 2 * POOL_HALO, POOL_W), jnp.float32),
            pltpu.VMEM((MIX_TQ, POOL_W), jnp.bfloat16),
            pltpu.VMEM((MIX_TQ, D_MODEL), jnp.float32),
        ],
        compiler_params=pltpu.CompilerParams(
            dimension_semantics=("arbitrary",), vmem_limit_bytes=VMEM_LIMIT),
        name="mix",
    )(h, mod, u, u, u, u, u, u, u, u, u, u, u, u,
      kvc, bias, wgrp, pscale, wap, wpp, wout, ln1g, ln1b)


ID_E0, ID_E1, ID_R0, ID_R1 = 0, 1, 4, 5
POS_PACKED = 0
POS_BITS = 16
PLAN_EXPERT, PLAN_VALID, PLAN_NACT, PLAN_NEXT = 0, 1, 2, 3
PLAN_W = 2 * LANES
CW_C0, CW_C1 = 0, 1


def _route_kernel(h_ref, mod_ref, wrt_ref, brt_ref, pos_ref, cw_ref, plan_ref, carry_ref, ids_all):
    i = pl.program_id(0)
    tm = ROUTE_TM

    @pl.when(i == 0)
    def _():
        carry_ref[...] = jnp.zeros_like(carry_ref)

    shift = mod_ref[0:1, 3 * D_MODEL:4 * D_MODEL]
    scale = mod_ref[0:1, 4 * D_MODEL:5 * D_MODEL]
    hm = _load_row_tiles(h_ref, tm) * (1.0 + scale) + shift

    hm_hi, hm_lo = _split_bf16(hm)
    w_hi, w_lo = _split_bf16(wrt_ref[...])
    nt = (((1,), (1,)), ((), ()))
    dg = functools.partial(lax.dot_general, dimension_numbers=nt, preferred_element_type=jnp.float32)
    logits = dg(w_hi, hm_hi) + (dg(w_hi, hm_lo) + dg(w_lo, hm_hi)) + brt_ref[:, 0:1]

    sub = lax.broadcasted_iota(jnp.int32, (ROUTE_LOGIT_ROWS, tm), 0)
    big = jnp.int32(1 << 20)
    is_grp = sub < N_GROUPS
    gl = jnp.where(is_grp, logits, -jnp.inf)
    gmax = jnp.max(gl, axis=0, keepdims=True)
    gidx = jnp.min(jnp.where(gl == gmax, sub, big), axis=0, keepdims=True)
    gsum = jnp.sum(jnp.where(is_grp, jnp.exp(logits - gmax), 0.0), axis=0, keepdims=True)
    p_group = 1.0 / gsum

    eid = sub - N_GROUPS
    sel = (eid >= 0) & (eid < N_EXPERTS) & (lax.shift_right_arithmetic(eid, 3) == gidx)
    el = jnp.where(sel, logits, -jnp.inf)
    l0 = jnp.max(el, axis=0, keepdims=True)
    i0 = jnp.min(jnp.where(el == l0, sub, big), axis=0, keepdims=True)
    el2 = jnp.where(sub == i0, -jnp.inf, el)
    l1 = jnp.max(el2, axis=0, keepdims=True)
    i1 = jnp.min(jnp.where(el2 == l1, sub, big), axis=0, keepdims=True)
    t = jnp.exp(l1 - l0)
    w0 = 1.0 / (1.0 + t)
    w1 = t / (1.0 + t)

    half_rows = jnp.where(i >= pl.num_programs(0) // SEQ_PARTS, N_EXPERTS, 0)
    i0 = i0 + half_rows
    i1 = i1 + half_rows
    subs = lax.broadcasted_iota(jnp.int32, (ROUTE_SEG_ROWS, tm), 0)
    onehot = jnp.where((subs == i0) | (subs == i1), 1.0, 0.0)
    rr = lax.broadcasted_iota(jnp.int32, (tm, tm), 0)
    cc = lax.broadcasted_iota(jnp.int32, (tm, tm), 1)
    earlier = jnp.where(rr < cc, 1.0, 0.0).astype(jnp.bfloat16)
    carry = carry_ref[:, 0:1]
    prefix = _bdot(onehot.astype(jnp.bfloat16), earlier) + carry
    r0 = jnp.sum(jnp.where(subs == i0, prefix, 0.0), axis=0, keepdims=True)
    r1 = jnp.sum(jnp.where(subs == i1, prefix, 0.0), axis=0, keepdims=True)
    total = jnp.broadcast_to(carry + jnp.sum(onehot, axis=1, keepdims=True), carry_ref.shape)
    carry_ref[...] = total

    sub8 = lax.broadcasted_iota(jnp.int32, (ROW_TILE, tm), 0)
    ids = jnp.zeros((ROW_TILE, tm), jnp.int32)
    for idx, val in ((ID_E0, i0 - N_GROUPS), (ID_E1, i1 - N_GROUPS),
                     (ID_R0, r0.astype(jnp.int32)), (ID_R1, r1.astype(jnp.int32))):
        ids = jnp.where(sub8 == idx, val, ids)
    ids_all[:, pl.ds(pl.multiple_of(i * tm, tm), tm)] = ids

    cw8 = jnp.where(sub8 == CW_C0, p_group * w0, jnp.where(sub8 == CW_C1, p_group * w1, 0.0))
    cw_ref[...] = jnp.concatenate([cw8, jnp.zeros((LANES - ROW_TILE, tm), jnp.float32)], axis=0).T

    @pl.when(i == pl.num_programs(0) - 1)
    def _():
        subq = lax.broadcasted_iota(jnp.int32, (LANES, LANES), 0)
        laneq = lax.broadcasted_iota(jnp.int32, (LANES, LANES), 1)
        total = jnp.concatenate([carry_ref[...], jnp.zeros((LANES - ROUTE_SEG_ROWS, LANES), jnp.float32)], axis=0)
        cnt = total.astype(jnp.int32)
        tiles = lax.shift_right_logical(cnt + (EXP_TM - 1), EXP_TM.bit_length() - 1).astype(jnp.float32)
        incl = jnp.where(laneq <= subq, 1.0, 0.0).astype(jnp.bfloat16)
        tile_end = _bdot(incl, tiles.astype(jnp.bfloat16))
        tile_start = tile_end - tiles
        seg = (tile_start * EXP_TM).astype(jnp.int32)
        nact = jnp.max(tile_end, axis=0, keepdims=True)

        ids_full = ids_all[...]
        look = jnp.zeros_like(ids_full)
        for e in range(N_SEG):
            look = jnp.where(ids_full == e, seg[N_GROUPS + e, 0], look)
        pos01 = look + pltpu.roll(ids_full, ID_R0 - ID_E0, axis=0)
        assert EXP_TILES * EXP_TM <= 1 << POS_BITS
        pos_ref[...] = pos01 | (pltpu.roll(pos01, ROW_TILE - 1, axis=0) << POS_BITS)

        subp = lax.broadcasted_iota(jnp.int32, (LANES, PLAN_W), 0)
        tile = lax.broadcasted_iota(jnp.int32, (LANES, PLAN_W), 1).astype(jnp.float32)
        is_exp = (subp >= N_GROUPS) & (subp < N_GROUPS + N_SEG)
        end_col = tile_end[:, 0:1]
        nact_s = nact[:, 0:1]
        te = jnp.sum(jnp.where(is_exp & (tile >= end_col), 1.0, 0.0), axis=0, keepdims=True)
        te_last = jnp.sum(jnp.where(is_exp & (nact_s - 1.0 >= end_col), 1.0, 0.0), axis=0, keepdims=True)[:, 0:1]
        tile_row = tile[0:1, :]
        te = jnp.minimum(jnp.where(tile_row < nact_s, te, te_last), N_SEG - 1.0)
        mine = (subp - N_GROUPS).astype(jnp.float32) == te
        cnt_sel = jnp.sum(jnp.where(mine, total[:, 0:1], 0.0), axis=0, keepdims=True)
        start_sel = jnp.sum(jnp.where(mine, tile_start[:, 0:1], 0.0), axis=0, keepdims=True)
        end_sel = jnp.sum(jnp.where(mine, end_col, 0.0), axis=0, keepdims=True)
        valid = jnp.clip(cnt_sel - (tile_row - start_sel) * EXP_TM, 0.0, float(EXP_TM))
        valid = jnp.where(tile_row < nact_s, valid, 0.0)
        subr = lax.broadcasted_iota(jnp.int32, (ROW_TILE, PLAN_W), 0)
        plan = jnp.where(subr == PLAN_EXPERT, te, jnp.where(subr == PLAN_VALID, valid,
                         jnp.where(subr == PLAN_NACT, nact_s, jnp.where(subr == PLAN_NEXT, end_sel, 0.0))))
        plan_ref[...] = plan.astype(jnp.int32)


def _route_call(h, mod, wrt, brt):
    tm = ROUTE_TM
    return pl.pallas_call(
        _route_kernel,
        grid=(SEQ // tm,),
        in_specs=[
            pl.BlockSpec((tm * ROW_TILE, LANES), lambda i: (i, 0)),
            pl.BlockSpec(mod.shape, lambda i: (0, 0)),
            pl.BlockSpec((ROUTE_LOGIT_ROWS, D_MODEL), lambda i: (0, 0)),
            pl.BlockSpec((ROUTE_LOGIT_ROWS, LANES), lambda i: (0, 0)),
        ],
        out_specs=[
            pl.BlockSpec((ROW_TILE, SEQ), lambda i: (0, 0)),
            pl.BlockSpec((tm, LANES), lambda i: (i, 0)),
            pl.BlockSpec((ROW_TILE, PLAN_W), lambda i: (0, 0)),
        ],
        out_shape=[
            jax.ShapeDtypeStruct((ROW_TILE, SEQ), jnp.int32),
            jax.ShapeDtypeStruct((SEQ, LANES), jnp.float32),
            jax.ShapeDtypeStruct((ROW_TILE, PLAN_W), jnp.int32),
        ],
        scratch_shapes=[pltpu.VMEM((ROUTE_SEG_ROWS, LANES), jnp.float32),
                        pltpu.VMEM((ROW_TILE, SEQ), jnp.int32)],
        compiler_params=pltpu.CompilerParams(
            dimension_semantics=("arbitrary",), vmem_limit_bytes=VMEM_LIMIT),
        name="route",
    )(h, mod, wrt, brt)


SRC_UNROLL = 8
EXP_CHUNK = 256
CAST_CHUNKS = 8
PREP_AFTER_DOWN = 1
TILE_ROWS = EXP_TM * ROW_TILE


def _experts_kernel(te_ref, tv_ref, nact_ref, tnext_ref, pos_ref,
                    h_hbm, mod_ref, wg_hbm, wu_hbm, wd_hbm,
                    y_ref,
                    src_ref, ord_ref, hres, xbuf, xmat, wgs, wus, wds,
                    wgb0, wub0, wdb0, wgb1, wub1, wdb1, rsem, wsem):
    i = pl.program_id(0)
    last = pl.num_programs(0) - 1
    nact = nact_ref[0]
    wsets = ((wgb0, wub0, wdb0), (wgb1, wub1, wdb1))
    xcur = lax.rem(i, 2)
    part = lax.shift_right_logical(te_ref[i], N_EXPERTS.bit_length() - 1)

    def weight_copies(segment, st):
        e = segment & (N_EXPERTS - 1)
        return [pltpu.make_async_copy(w_hbm.at[e], stage.at[st], wsem.at[st, n])
                for n, (w_hbm, stage) in enumerate(((wg_hbm, wgs), (wu_hbm, wus), (wd_hbm, wds)))]

    def tile_after(t):
        return tnext_ref[jnp.minimum(t, last)]

    def gather_row(tile, k, s):
        local = (src_ref[tile * EXP_TM + k] - part * PART_TOKENS) & (PART_TOKENS - 1)
        xbuf[s, k * ROW_TILE:(k + 1) * ROW_TILE, :] = hres[pl.ds(pl.multiple_of(local * ROW_TILE, ROW_TILE),
                                                             ROW_TILE), :]

    def gather_items(tile, s):
        return [functools.partial(gather_row, tile, k, s) for k in range(EXP_TM)]

    def prepare_input(xs):
        x = _load_row_tiles(xbuf, EXP_TM, lead=(xs,))
        shift = mod_ref[0:1, 3 * D_MODEL:4 * D_MODEL]
        scale = mod_ref[0:1, 4 * D_MODEL:5 * D_MODEL]
        xmat[xs] = (x * (1.0 + scale) + shift).astype(jnp.bfloat16)

    def compute_chunks(xs, ws):
        wgb, wub, wdb = wsets[ws]

        state = {"act": []}

        def gate(c):
            def run():
                state["a"] = _bdot(xmat[xs], wgb[:, c * EXP_CHUNK:(c + 1) * EXP_CHUNK])
            return run

        def up(c):
            def run():
                a = state["a"]
                u = _bdot(xmat[xs], wub[:, c * EXP_CHUNK:(c + 1) * EXP_CHUNK])
                state["act"].append((a * jax.nn.sigmoid(a) * u).astype(jnp.bfloat16))
            return run

        def down(c):
            def run():
                if c == 0:
                    state["actf"] = jnp.concatenate(state["act"], axis=-1)
                yc = _bdot(state["actf"], wdb[:, c * EXP_CHUNK:(c + 1) * EXP_CHUNK])
                for jj in range(EXP_CHUNK // LANES):
                    j = c * (EXP_CHUNK // LANES) + jj
                    y_ref[pl.ds(j, EXP_TM, stride=ROW_TILE), :] = yc[:, jj * LANES:(jj + 1) * LANES]
            return run

        first = []
        for c in range(D_EXPERT // EXP_CHUNK):
            first += [gate(c), up(c)]
        return first, [down(c) for c in range(D_MODEL // EXP_CHUNK)]

    def resident_copy(p):
        rows = PART_TOKENS * ROW_TILE
        return pltpu.make_async_copy(h_hbm.at[pl.ds(pl.multiple_of(p * rows, rows), rows), :], hres, rsem.at[0])

    @pl.when(i == 0)
    def _():
        resident_copy(part).start()
        ord_ref[0] = 0
        for cp in weight_copies(te_ref[0], 0):
            cp.start()
        second = tile_after(0)

        @pl.when(second < nact)
        def _():
            for cp in weight_copies(te_ref[jnp.minimum(second, last)], 1):
                cp.start()

        def fill_body(tt, c):
            ts = [tt * SRC_UNROLL + u for u in range(SRC_UNROLL)]
            words = [pos_ref[t] for t in ts]
            for t, w in zip(ts, words):
                src_ref[w & ((1 << POS_BITS) - 1)] = t
                src_ref[lax.shift_right_logical(w, POS_BITS)] = t
            return c
        lax.fori_loop(0, SEQ // SRC_UNROLL, fill_body, 0)

        def pad_tile(t, c):
            pad_tok = lax.shift_right_logical(te_ref[t], N_EXPERTS.bit_length() - 1) * PART_TOKENS

            def pad_row(k, c2):
                src_ref[t * EXP_TM + k] = pad_tok
                return c2
            return lax.fori_loop(tv_ref[t], EXP_TM, pad_row, c)
        lax.fori_loop(0, nact, pad_tile, 0)

    active = i < nact
    prev = jnp.maximum(i - 1, 0)
    new_segment = (i == 0) | (te_ref[i] != te_ref[prev])

    @pl.when(active & ((i == 0) | (part != lax.shift_right_logical(te_ref[prev], N_EXPERTS.bit_length() - 1))))
    def _():
        @pl.when(i > 0)
        def _():
            resident_copy(part).start()
        resident_copy(part).wait()
        for item in gather_items(i, xcur):
            item()
        prepare_input(xcur)

    @pl.when(active & new_segment & (i > 0))
    def _():
        ord_ref[0] = ord_ref[0] + 1

    parity = ord_ref[0] & 1
    after1 = tile_after(i)
    after2 = tile_after(after1)
    after3 = tile_after(after2)

    def cast_items(st):
        def block(stage, dst, rows):
            def run():
                dst[rows, :] = stage[st, rows, :].astype(jnp.bfloat16)
            return run
        out = []
        for r in range(CAST_CHUNKS):
            up_rows = slice(r * D_MODEL // CAST_CHUNKS, (r + 1) * D_MODEL // CAST_CHUNKS)
            dn_rows = slice(r * D_EXPERT // CAST_CHUNKS, (r + 1) * D_EXPERT // CAST_CHUNKS)
            out += [block(stage, dst, rows)
                    for stage, dst, rows in zip((wgs, wus, wds), wsets[st], (up_rows, up_rows, dn_rows))]
        return out

    @pl.when(i == 0)
    def _():
        for cp in weight_copies(te_ref[0], 0):
            cp.wait()
        for item in cast_items(0):
            item()

        @pl.when(after2 < nact)
        def _():
            for cp in weight_copies(te_ref[jnp.minimum(after2, last)], 0):
                cp.start()

    def run_tile(par, extra_items):
        first, second = compute_chunks(xcur, par)
        gathers = gather_items(jnp.minimum(i + 1, nact - 1), 1 - xcur)
        per = -(-len(gathers) // len(first))
        per_extra = -(-len(extra_items) // (len(first) + len(second)))
        for n, chunk in enumerate(first + second):
            if n < len(first):
                for item in gathers[n * per:(n + 1) * per]:
                    item()
            for item in extra_items[n * per_extra:(n + 1) * per_extra]:
                item()
            chunk()
            if n == len(first) + PREP_AFTER_DOWN:
                prepare_input(1 - xcur)

    last_of_segment = (i + 1 >= nact) | (te_ref[jnp.minimum(i + 1, last)] != te_ref[i])
    cast_next = active & last_of_segment & (after1 < nact)

    for par in range(2):
        @pl.when(cast_next & (parity == par))
        def _():
            for cp in weight_copies(te_ref[jnp.minimum(after1, last)], 1 - par):
                cp.wait()
            run_tile(par, cast_items(1 - par))

            @pl.when(after3 < nact)
            def _():
                for cp in weight_copies(te_ref[jnp.minimum(after3, last)], 1 - par):
                    cp.start()

        @pl.when(active & jnp.logical_not(cast_next) & (parity == par))
        def _():
            run_tile(par, [])

    @pl.when(jnp.logical_not(active))
    def _():
        y_ref[...] = jnp.zeros_like(y_ref)


def _experts_call(te, tv, nact, tnext, pos, h, mod, wg, wu, wd):
    grid_spec = pltpu.PrefetchScalarGridSpec(
        num_scalar_prefetch=5,
        grid=(EXP_TILES,),
        in_specs=[
            pl.BlockSpec(memory_space=pl.ANY),
            pl.BlockSpec(mod.shape, lambda i, *_: (0, 0)),
            pl.BlockSpec(memory_space=pl.ANY),
            pl.BlockSpec(memory_space=pl.ANY),
            pl.BlockSpec(memory_space=pl.ANY),
        ],
        out_specs=pl.BlockSpec((TILE_ROWS, LANES), lambda i, *_: (i, 0)),
        scratch_shapes=[
            pltpu.SMEM((EXP_TILES * EXP_TM,), jnp.int32),
            pltpu.SMEM((1,), jnp.int32),
            pltpu.VMEM((PART_TOKENS * ROW_TILE, LANES), jnp.float32),
            pltpu.VMEM((2, TILE_ROWS, LANES), jnp.float32),
            pltpu.VMEM((2, EXP_TM, D_MODEL), jnp.bfloat16),
            pltpu.VMEM((2, D_MODEL, D_EXPERT), jnp.float32),
            pltpu.VMEM((2, D_MODEL, D_EXPERT), jnp.float32),
            pltpu.VMEM((2, D_EXPERT, D_MODEL), jnp.float32),
            pltpu.VMEM((D_MODEL, D_EXPERT), jnp.bfloat16),
            pltpu.VMEM((D_MODEL, D_EXPERT), jnp.bfloat16),
            pltpu.VMEM((D_EXPERT, D_MODEL), jnp.bfloat16),
            pltpu.VMEM((D_MODEL, D_EXPERT), jnp.bfloat16),
            pltpu.VMEM((D_MODEL, D_EXPERT), jnp.bfloat16),
            pltpu.VMEM((D_EXPERT, D_MODEL), jnp.bfloat16),
            pltpu.SemaphoreType.DMA((1,)),
            pltpu.SemaphoreType.DMA((2, 3)),
        ],
    )
    return pl.pallas_call(
        _experts_kernel,
        grid_spec=grid_spec,
        out_shape=jax.ShapeDtypeStruct((EXP_TILES * TILE_ROWS, LANES), jnp.float32),
        compiler_params=pltpu.CompilerParams(
            dimension_semantics=("arbitrary",), vmem_limit_bytes=EXPERTS_VMEM_LIMIT),
        name="experts",
    )(te, tv, nact, tnext, pos, h, mod, wg, wu, wd)


def _combine_kernel(pos_ref, h_ref, cw_ref, mod_ref, g_ref, b_ref, ys_hbm, o_ref, ybuf, sem):
    i = pl.program_id(0)
    tm = CMB_TM
    slot = lax.rem(i, 2)

    def start_row(tile, k, s):
        word = pos_ref[tile * tm + k]
        for half, p in ((0, word & ((1 << POS_BITS) - 1)), (1, lax.shift_right_logical(word, POS_BITS))):
            pltpu.make_async_copy(ys_hbm.at[pl.ds(pl.multiple_of(p * ROW_TILE, ROW_TILE), ROW_TILE), :],
                                  ybuf.at[s, half, pl.ds(pl.multiple_of(k * ROW_TILE, ROW_TILE), ROW_TILE), :],
                                  sem.at[s]).start(priority=half)

    @pl.when(i == 0)
    def _():
        def body(kk, c):
            for u in range(SRC_UNROLL):
                start_row(0, kk * SRC_UNROLL + u, 0)
            return c
        lax.fori_loop(0, tm // SRC_UNROLL, body, 0)

    for half in range(2):
        pltpu.make_async_copy(ys_hbm.at[pl.ds(0, tm * ROW_TILE), :], ybuf.at[slot, half], sem.at[slot]).wait()

    g2 = mod_ref[0:1, 5 * D_MODEL:6 * D_MODEL]

    def chunk(c):
        rows = slice(c * CMB_CHUNK, (c + 1) * CMB_CHUNK)
        y0 = _load_row_tiles(ybuf, CMB_CHUNK, lead=(slot, 0), first=c * CMB_CHUNK)
        y1 = _load_row_tiles(ybuf, CMB_CHUNK, lead=(slot, 1), first=c * CMB_CHUNK)
        ffn = cw_ref[rows, CW_C0:CW_C0 + 1] * y0 + cw_ref[rows, CW_C1:CW_C1 + 1] * y1
        h = _load_row_tiles(h_ref, CMB_CHUNK, first=c * CMB_CHUNK)
        o_ref[rows, :] = _layer_norm(DEEPNORM_ALPHA * h + g2 * ffn, g_ref[...], b_ref[...])

    n_chunks = tm // CMB_CHUNK

    @pl.when(i + 1 < pl.num_programs(0))
    def _():
        per = tm // n_chunks
        for c in range(n_chunks):
            for k in range(c * per, (c + 1) * per):
                start_row(i + 1, k, 1 - slot)
            chunk(c)

    @pl.when(i + 1 == pl.num_programs(0))
    def _():
        for c in range(n_chunks):
            chunk(c)


def _combine_call(pos, h, ys, cw, mod, g, b):
    tm = CMB_TM
    grid_spec = pltpu.PrefetchScalarGridSpec(
        num_scalar_prefetch=1,
        grid=(SEQ // tm,),
        in_specs=[
            pl.BlockSpec((tm * ROW_TILE, LANES), lambda i, *_: (i, 0)),
            pl.BlockSpec((tm, LANES), lambda i, *_: (i, 0)),
            pl.BlockSpec(mod.shape, lambda i, *_: (0, 0)),
            pl.BlockSpec((1, D_MODEL), lambda i, *_: (0, 0)),
            pl.BlockSpec((1, D_MODEL), lambda i, *_: (0, 0)),
            pl.BlockSpec(memory_space=pl.ANY),
        ],
        out_specs=pl.BlockSpec((tm, D_MODEL), lambda i, *_: (i, 0)),
        scratch_shapes=[
            pltpu.VMEM((2, 2, tm * ROW_TILE, LANES), jnp.float32),
            pltpu.SemaphoreType.DMA((2,)),
        ],
    )
    return pl.pallas_call(
        _combine_kernel,
        grid_spec=grid_spec,
        out_shape=jax.ShapeDtypeStruct((SEQ, D_MODEL), jnp.float32),
        compiler_params=pltpu.CompilerParams(
            dimension_semantics=("arbitrary",), vmem_limit_bytes=VMEM_LIMIT),
        name="combine",
    )(pos, h, cw, mod, g, b, ys)


def kernel(x, c, ctx, c_ctx, ln_in_g, ln_in_b, w_mod, b_mod, w_in, rpb, w_pool_grp, pool_scale,
           w_attn_proj, w_pool_proj, w_out, ln1_g, ln1_b, w_router_group, b_router_group,
           w_router_expert, b_router_expert, w_expert_gate, w_expert_up, w_expert_down, ln2_g, ln2_b):
    assert x.shape == (1, SEQ, D_MODEL) and ctx.shape == (1, CTX_LEN, D_MODEL)
    assert w_mod.shape[0] == 1, "single-layer trunk"
    f32, bf16 = jnp.float32, jnp.bfloat16
    row = lambda v: v.reshape(1, -1).astype(f32)

    cond = jnp.concatenate([c, c_ctx[None], jnp.zeros((MOD_ROWS - 2, D_MODEL), f32)], axis=0)
    mod = _mod_call(cond, w_mod[0], row(b_mod[0]))

    lng, lnb = row(ln_in_g), row(ln_in_b)
    w_in_b = w_in[0].astype(bf16)
    u, h0 = _proj_call(x[0], mod, lng, lnb, w_in_b, mod_row=0, latent=True, tm=PROJ_TM)
    kvc, = _proj_call(ctx[0], mod, lng, lnb, w_in_b[:, ATT_W:3 * ATT_W], mod_row=1, latent=False, tm=CTX_LEN)

    h1 = _mix_call(h0, mod, u, kvc, _attn_bias_table(rpb[0]),
                   w_pool_grp[0].astype(bf16), row(pool_scale[0]),
                   w_attn_proj[0].astype(bf16), w_pool_proj[0].astype(bf16), w_out[0].astype(bf16),
                   row(ln1_g[0]), row(ln1_b[0]))

    n_logit = N_GROUPS + N_EXPERTS
    wrt = jnp.concatenate([w_router_group[0].T, w_router_expert[0].T,
                           jnp.zeros((ROUTE_LOGIT_ROWS - n_logit, D_MODEL), f32)], axis=0)
    brt = jnp.concatenate([b_router_group[0], b_router_expert[0], jnp.zeros((ROUTE_LOGIT_ROWS - n_logit,), f32)])
    brt = jnp.broadcast_to(brt[:, None], (ROUTE_LOGIT_ROWS, LANES))
    pos, cw, plan = _route_call(h1, mod, wrt, brt)

    posw = pos[POS_PACKED]
    y = _experts_call(plan[PLAN_EXPERT, :EXP_TILES], plan[PLAN_VALID, :EXP_TILES], plan[PLAN_NACT, :1],
                      plan[PLAN_NEXT, :EXP_TILES], posw, h1, mod,
                      w_expert_gate[0], w_expert_up[0], w_expert_down[0])
    out = _combine_call(posw, h1, y, cw, mod, row(ln2_g[0]), row(ln2_b[0]))
    return out[None]
```

```python
import functools

import jax
import jax.numpy as jnp
from jax import lax
from jax.experimental import pallas as pl
from jax.experimental.pallas import tpu as pltpu

D_MODEL = 1024
SEQ = 16384
GRID_W = 64
ROWS = SEQ // GRID_W
CTX_LEN = 256
N_HEADS = 8
HEAD_DIM = 64
ATT_W = N_HEADS * HEAD_DIM
WIN_H = 8
WIN_W = 16
POOL_WINDOWS = (2, 4, 8, 16)
POOL_GROUPS = 4
POOL_DIM = 128
POOL_W = POOL_GROUPS * POOL_DIM
GATE_COL = 3 * ATT_W + POOL_W
N_GROUPS = 4
EXPERTS_PER_GROUP = 8
N_EXPERTS = N_GROUPS * EXPERTS_PER_GROUP
D_EXPERT = 512
N_MOD = 6
DEEPNORM_ALPHA = 2.0 ** 0.25
LN_EPS = 1e-5
NEG_INF = -1e30

LANES = 128
ROW_TILE = 8
MOD_ROWS = 8
PROJ_TM = 1024
PROJ_SUB = 256
MIX_ROWS = 8
MIX_TQ = MIX_ROWS * GRID_W
KV_HALO = 4 * GRID_W
POOL_HALO = 16
ROUTE_TM = 1024
ROUTE_LOGIT_ROWS = 40
ROUTE_SEG_ROWS = 72
EXP_TM = 256
SEQ_PARTS = 2
PART_TOKENS = SEQ // SEQ_PARTS
N_SEG = SEQ_PARTS * N_EXPERTS
EXP_TILES = 2 * SEQ // EXP_TM + N_SEG
CMB_TM = 512
CMB_CHUNK = 128
MERGE_ROWS = 256
BIAS_LANES = 1024
ATTN_AHEAD = 3
VMEM_LIMIT = 56 * 1024 * 1024
EXPERTS_VMEM_LIMIT = 60 * 1024 * 1024


def _layer_norm(x, g, b):
    mu = jnp.mean(x, axis=-1, keepdims=True)
    xc = x - mu
    var = jnp.mean(xc * xc, axis=-1, keepdims=True)
    return xc * lax.rsqrt(var + LN_EPS) * g + b


def _bdot(a, b):
    return jnp.dot(a, b, preferred_element_type=jnp.float32)


def _split_bf16(a):
    hi = a.astype(jnp.bfloat16)
    lo = (a - hi.astype(jnp.float32)).astype(jnp.bfloat16)
    return hi, lo


def _dot3(a, b):
    a_hi, a_lo = _split_bf16(a)
    b_hi, b_lo = _split_bf16(b)
    return _bdot(a_hi, b_hi) + (_bdot(a_hi, b_lo) + _bdot(a_lo, b_hi))


def _load_row_tiles(ref, tokens, lead=(), first=0):
    parts = [ref[(*lead, pl.ds(first * ROW_TILE + j, tokens, stride=ROW_TILE), slice(None))]
             for j in range(ROW_TILE)]
    return jnp.concatenate(parts, axis=-1)


def _store_row_tiles(ref, value, lead=()):
    tokens = value.shape[0]
    for j in range(ROW_TILE):
        ref[(*lead, pl.ds(j, tokens, stride=ROW_TILE), slice(None))] = value[:, j * LANES:(j + 1) * LANES]


def _mod_kernel(cond_ref, w_ref, b_ref, o_ref):
    cond = cond_ref[...]
    act = cond * jax.nn.sigmoid(cond)
    o_ref[...] = _dot3(act, w_ref[...]) + b_ref[...]


def _mod_call(cond, w_mod, b_mod):
    tn = 1536
    n = N_MOD * D_MODEL
    return pl.pallas_call(
        _mod_kernel,
        grid=(n // tn,),
        in_specs=[
            pl.BlockSpec((MOD_ROWS, D_MODEL), lambda i: (0, 0)),
            pl.BlockSpec((D_MODEL, tn), lambda i: (0, i)),
            pl.BlockSpec((1, tn), lambda i: (0, i)),
        ],
        out_specs=pl.BlockSpec((MOD_ROWS, tn), lambda i: (0, i)),
        out_shape=jax.ShapeDtypeStruct((MOD_ROWS, n), jnp.float32),
        compiler_params=pltpu.CompilerParams(
            dimension_semantics=("arbitrary",), vmem_limit_bytes=VMEM_LIMIT),
        name="mod",
    )(cond, w_mod, b_mod)


def _proj_kernel(x_ref, mod_ref, g_ref, b_ref, w_ref, o_ref, *h_out, mod_row, latent):
    shift = mod_ref[mod_row:mod_row + 1, 0:D_MODEL]
    scale = mod_ref[mod_row:mod_row + 1, D_MODEL:2 * D_MODEL]
    tm, n = o_ref.shape
    sub = min(tm, PROJ_SUB)

    def prep(r):
        rows = slice(r * sub, (r + 1) * sub)
        h = _layer_norm(x_ref[rows, :], g_ref[...], b_ref[...])
        if latent:
            h_out[0][rows, :] = h
        return (h * (1.0 + scale) + shift).astype(jnp.bfloat16)

    def finish(r, c, res):
        if latent and c == 0:
            lane = lax.broadcasted_iota(jnp.int32, (1, D_MODEL), 1)
            res = res * jnp.where(lane < ATT_W, HEAD_DIM ** -0.5, 1.0)
        if latent and c * D_MODEL >= GATE_COL:
            res = jax.nn.sigmoid(res)
        o_ref[r * sub:(r + 1) * sub, c * D_MODEL:(c + 1) * D_MODEL] = res.astype(jnp.bfloat16)

    hm = {0: prep(0)}
    waiting = None
    for r in range(tm // sub):
        for c in range(n // D_MODEL):
            res = _bdot(hm[r], w_ref[:, c * D_MODEL:(c + 1) * D_MODEL])
            if c == 0 and (r + 1) * sub < tm:
                hm[r + 1] = prep(r + 1)
            if waiting is not None:
                finish(*waiting)
            waiting = (r, c, res)
    finish(*waiting)


def _proj_call(x, mod, g, b, w, *, mod_row, latent, tm):
    rows, n = x.shape[0], w.shape[1]
    out_specs = [pl.BlockSpec((tm, n), lambda i: (i, 0))]
    out_shape = [jax.ShapeDtypeStruct((rows, n), jnp.bfloat16)]
    if latent:
        out_specs.append(pl.BlockSpec((tm, D_MODEL), lambda i: (i, 0)))
        out_shape.append(jax.ShapeDtypeStruct((rows, D_MODEL), jnp.float32))
    return pl.pallas_call(
        functools.partial(_proj_kernel, mod_row=mod_row, latent=latent),
        grid=(rows // tm,),
        in_specs=[
            pl.BlockSpec((tm, D_MODEL), lambda i: (i, 0)),
            pl.BlockSpec(mod.shape, lambda i: (0, 0)),
            pl.BlockSpec((1, D_MODEL), lambda i: (0, 0)),
            pl.BlockSpec((1, D_MODEL), lambda i: (0, 0)),
            pl.BlockSpec((D_MODEL, n), lambda i: (0, 0), pipeline_mode=pl.Buffered(1)),
        ],
        out_specs=out_specs,
        out_shape=out_shape,
        compiler_params=pltpu.CompilerParams(
            dimension_semantics=("arbitrary",), vmem_limit_bytes=VMEM_LIMIT),
        name="proj",
    )(x, mod, g, b, w)


def _attn_bias_table(rpb):
    col = jnp.arange(GRID_W, dtype=jnp.int32)
    col_start = jnp.clip(col - WIN_W // 2, 0, GRID_W - WIN_W)
    col_mask = (col[None, :] >= col_start[:, None]) & (col[None, :] < col_start[:, None] + WIN_W)
    col_off = jnp.clip(col[None, :] - col[:, None], 1 - WIN_W, WIN_W - 1) + (WIN_W - 1)
    onehot = (col_off[None] == jnp.arange(2 * WIN_W - 1, dtype=jnp.int32)[:, None, None]).astype(jnp.float32)
    tab = jnp.einsum("hrc,cqk->hqrk", rpb.astype(jnp.float32), onehot, precision=lax.Precision.HIGHEST)
    tab = jnp.where(col_mask[None, :, None, :], tab, NEG_INF)
    n_rows = 2 * WIN_H - 1
    flat = tab.reshape(N_HEADS // 2, 2 * GRID_W, n_rows * GRID_W)
    even = jnp.pad(flat, ((0, 0), (0, 0), (0, BIAS_LANES - n_rows * GRID_W)))
    odd = jnp.pad(flat[:, :, GRID_W:], ((0, 0), (0, 0), (0, BIAS_LANES - (n_rows - 1) * GRID_W)))

    def window_kernel(even_ref, odd_ref, o_ref):
        start = WIN_H - 1 - pl.program_id(0)
        base = pl.multiple_of(lax.shift_right_logical(start, 1) * LANES, LANES)
        width = WIN_H * GRID_W
        o_ref[0] = jnp.where((start & 1) == 0, even_ref[:, :, pl.ds(base, width)], odd_ref[:, :, pl.ds(base, width)])

    full = pl.BlockSpec(even.shape, lambda v: (0, 0, 0))
    return pl.pallas_call(
        window_kernel,
        grid=(WIN_H,),
        in_specs=[full, full],
        out_specs=pl.BlockSpec((1, N_HEADS // 2, 2 * GRID_W, WIN_H * GRID_W), lambda v: (v, 0, 0, 0)),
        out_shape=jax.ShapeDtypeStruct((WIN_H, N_HEADS // 2, 2 * GRID_W, WIN_H * GRID_W), jnp.float32),
        compiler_params=pltpu.CompilerParams(dimension_semantics=("arbitrary",), vmem_limit_bytes=VMEM_LIMIT),
        name="bias_table",
    )(even, odd)


def _mix_kernel(h_ref, mod_ref,
                q_ref, kp_ref, kc_ref, kn_ref, vp_ref, vc_ref, vn_ref,
                pp_ref, pc_ref, pn_ref, ga_ref, gb_ref,
                kvc_ref, bias_ref, wgrp_ref, pscale_ref, wap_ref, wpp_ref, wout_ref,
                ln1g_ref, ln1b_ref,
                o_ref,
                kbuf, vbuf, yabuf, pbuf, ypbuf, zbuf):
    b = pl.program_id(0)
    nb = pl.num_programs(0)

    kbuf[0:KV_HALO, :] = kp_ref[...]
    kbuf[KV_HALO:KV_HALO + MIX_TQ, :] = kc_ref[...]
    kbuf[KV_HALO + MIX_TQ:, :] = kn_ref[...]
    vbuf[0:KV_HALO, :] = vp_ref[...]
    vbuf[KV_HALO:KV_HALO + MIX_TQ, :] = vc_ref[...]
    vbuf[KV_HALO + MIX_TQ:, :] = vn_ref[...]

    lane = lax.broadcasted_iota(jnp.int32, (GRID_W, LANES), 1)
    first_head = lane < HEAD_DIM

    units = [(j, pair) for j in range(MIX_ROWS) for pair in range(N_HEADS // 2)]
    nt = (((1,), (1,)), ((), ()))

    def window(j):
        r = b * MIX_ROWS + j
        rs = jnp.clip(r - WIN_H // 2, 0, ROWS - WIN_H)
        return pl.multiple_of((rs - b * MIX_ROWS + WIN_H // 2) * GRID_W, GRID_W), r - rs

    def scores(j, pair):
        off, var = window(j)
        cols = slice(pair * LANES, (pair + 1) * LANES)
        q = q_ref[j * GRID_W:(j + 1) * GRID_W, cols]
        zero = jnp.zeros_like(q)
        q2 = jnp.concatenate([jnp.where(first_head, q, zero), jnp.where(first_head, zero, q)], axis=0)
        kw = kbuf[pl.ds(off, WIN_H * GRID_W), cols]
        s_loc = lax.dot_general(q2, kw, nt, preferred_element_type=jnp.float32) + bias_ref[var, pair]
        s_ctx = lax.dot_general(q2, kvc_ref[:, cols], nt, preferred_element_type=jnp.float32)
        return s_loc, s_ctx

    def values(j, pair, s_loc, s_ctx):
        off, _ = window(j)
        cols = slice(pair * LANES, (pair + 1) * LANES)
        vw = vbuf[pl.ds(off, WIN_H * GRID_W), cols]
        vctx = kvc_ref[:, ATT_W + pair * LANES:ATT_W + (pair + 1) * LANES]
        m = jnp.maximum(jnp.max(s_loc, axis=-1, keepdims=True), jnp.max(s_ctx, axis=-1, keepdims=True))
        p_loc = jnp.exp(s_loc - m).astype(jnp.bfloat16)
        p_ctx = jnp.exp(s_ctx - m).astype(jnp.bfloat16)
        ones = lambda rows: jnp.ones((rows, LANES), jnp.bfloat16)
        o2 = (_bdot(p_loc, jnp.concatenate([vw, ones(WIN_H * GRID_W)], axis=1))
              + _bdot(p_ctx, jnp.concatenate([vctx, ones(CTX_LEN)], axis=1)))
        o2 = o2[:, :LANES] / o2[:, LANES:]
        o_pair = jnp.where(first_head, o2[:GRID_W], o2[GRID_W:])
        yabuf[j * GRID_W:(j + 1) * GRID_W, cols] = o_pair.astype(jnp.bfloat16)

    pbuf[0:POOL_HALO, :] = jnp.where(b > 0, pp_ref[...].astype(jnp.float32), 0.0)
    pbuf[POOL_HALO:POOL_HALO + MIX_TQ, :] = pc_ref[...].astype(jnp.float32)
    pbuf[POOL_HALO + MIX_TQ:, :] = jnp.where(b < nb - 1, pn_ref[...].astype(jnp.float32), 0.0)
    edge = lax.broadcasted_iota(jnp.int32, (ROW_TILE, 1), 0)

    def pool_group(g):
        win = POOL_WINDOWS[g]
        lo, hi = win // 2, win - win // 2
        cols = slice(g * POOL_DIM, (g + 1) * POOL_DIM)
        acc = None
        for d in range(-lo, hi):
            term = pbuf[POOL_HALO + d:POOL_HALO + d + MIX_TQ, cols]
            acc = term if acc is None else acc + term
        assert max(lo, hi) <= ROW_TILE
        top = jnp.where(b == 0, 1.0 / (win - jnp.maximum(lo - edge, 0)).astype(jnp.float32), 1.0 / win)
        bot = jnp.where(b == nb - 1,
                        1.0 / (win - jnp.maximum(edge + hi - ROW_TILE, 0)).astype(jnp.float32), 1.0 / win)
        inv = jnp.concatenate([top, jnp.full((MIX_TQ - 2 * ROW_TILE, 1), 1.0 / win, jnp.float32), bot], axis=0)
        pooled = acc * inv - pbuf[POOL_HALO:POOL_HALO + MIX_TQ, cols]
        yp = _bdot(pooled.astype(jnp.bfloat16), wgrp_ref[g]) * pscale_ref[:, cols]
        ypbuf[:, cols] = yp.astype(jnp.bfloat16)

    def pooled_branch():
        zbuf[...] = gb_ref[...].astype(jnp.float32) * _bdot(ypbuf[...], wpp_ref[...])

    extra = {}
    for g in range(POOL_GROUPS):
        extra[(g + 1) * len(units) // (POOL_GROUPS + 2)] = functools.partial(pool_group, g)
    extra[(POOL_GROUPS + 1) * len(units) // (POOL_GROUPS + 2)] = pooled_branch
    pending = [scores(*u) for u in units[:ATTN_AHEAD]]
    for n, u in enumerate(units):
        if n + ATTN_AHEAD < len(units):
            pending.append(scores(*units[n + ATTN_AHEAD]))
        values(*u, *pending.pop(0))
        if n in extra:
            extra[n]()

    g1 = mod_ref[0:1, 2 * D_MODEL:3 * D_MODEL]
    n_chunks = MIX_TQ // MERGE_ROWS
    rows = lambda c: slice(c * MERGE_ROWS, (c + 1) * MERGE_ROWS)
    z, y = {}, {}

    def stage_a(c):
        ya = _bdot(yabuf[rows(c), :], wap_ref[...])
        z[c] = (ga_ref[rows(c), :].astype(jnp.float32) * ya + zbuf[rows(c), :]).astype(jnp.bfloat16)

    def stage_b(c):
        y[c] = _bdot(z.pop(c), wout_ref[...])

    def stage_c(c):
        hn = _layer_norm(DEEPNORM_ALPHA * h_ref[rows(c), :] + g1 * y.pop(c), ln1g_ref[...], ln1b_ref[...])
        for j in range(ROW_TILE):
            o_ref[pl.ds(c * MERGE_ROWS * ROW_TILE + j, MERGE_ROWS, stride=ROW_TILE), :] = hn[:, j * LANES:(j + 1) * LANES]

    for t in range(n_chunks + 2):
        if t < n_chunks:
            stage_a(t)
        if 0 <= t - 1 < n_chunks:
            stage_b(t - 1)
        if 0 <= t - 2 < n_chunks:
            stage_c(t - 2)


def _mix_call(h, mod, u, kvc, bias, wgrp, pscale, wap, wpp, wout, ln1g, ln1b):
    nb = SEQ // MIX_TQ
    halo_per_blk = MIX_TQ // KV_HALO
    n_halo = SEQ // KV_HALO
    ph_per_blk = MIX_TQ // POOL_HALO
    n_ph = SEQ // POOL_HALO

    def const(shape):
        return pl.BlockSpec(shape, lambda i: (0,) * len(shape), pipeline_mode=pl.Buffered(1))

    def prev_halo(c):
        return pl.BlockSpec((KV_HALO, ATT_W), lambda i: (jnp.maximum(i * halo_per_blk - 1, 0), c))

    def next_halo(c):
        return pl.BlockSpec((KV_HALO, ATT_W), lambda i: (jnp.minimum((i + 1) * halo_per_blk, n_halo - 1), c))

    def cur(c):
        return pl.BlockSpec((MIX_TQ, ATT_W), lambda i: (i, c))

    in_specs = [
        pl.BlockSpec((MIX_TQ, D_MODEL), lambda i: (i, 0)),
        const(mod.shape),
        cur(0),
        prev_halo(1), cur(1), next_halo(1),
        prev_halo(2), cur(2), next_halo(2),
        pl.BlockSpec((POOL_HALO, POOL_W), lambda i: (jnp.maximum(i * ph_per_blk - 1, 0), 3)),
        cur(3),
        pl.BlockSpec((POOL_HALO, POOL_W), lambda i: (jnp.minimum((i + 1) * ph_per_blk, n_ph - 1), 3)),
        pl.BlockSpec((MIX_TQ, D_MODEL), lambda i: (i, 2)),
        pl.BlockSpec((MIX_TQ, D_MODEL), lambda i: (i, 3)),
        const(kvc.shape), const(bias.shape), const(wgrp.shape), const(pscale.shape),
        const(wap.shape), const(wpp.shape), const(wout.shape),
        const((1, D_MODEL)), const((1, D_MODEL)),
    ]
    return pl.pallas_call(
        _mix_kernel,
        grid=(nb,),
        in_specs=in_specs,
        out_specs=pl.BlockSpec((MIX_TQ * ROW_TILE, LANES), lambda i: (i, 0)),
        out_shape=jax.ShapeDtypeStruct((SEQ * ROW_TILE, LANES), jnp.float32),
        scratch_shapes=[
            pltpu.VMEM((MIX_TQ + 2 * KV_HALO, ATT_W), jnp.bfloat16),
            pltpu.VMEM((MIX_TQ + 2 * KV_HALO, ATT_W), jnp.bfloat16),
            pltpu.VMEM((MIX_TQ, ATT_W), jnp.bfloat16),
            pltpu.VMEM((MIX_TQ + 2 * POOL_HALO, POOL_W), jnp.float32),
            pltpu.VMEM((MIX_TQ, POOL_W), jnp.bfloat16),
            pltpu.VMEM((MIX_TQ, D_MODEL), jnp.float32),
        ],
        compiler_params=pltpu.CompilerParams(
            dimension_semantics=("arbitrary",), vmem_limit_bytes=VMEM_LIMIT),
        name="mix",
    )(h, mod, u, u, u, u, u, u, u, u, u, u, u, u,
      kvc, bias, wgrp, pscale, wap, wpp, wout, ln1g, ln1b)


ID_E0, ID_E1, ID_R0, ID_R1 = 0, 1, 4, 5
POS_PACKED = 0
POS_BITS = 16
PLAN_EXPERT, PLAN_VALID, PLAN_NACT, PLAN_NEXT = 0, 1, 2, 3
PLAN_W = 2 * LANES
CW_C0, CW_C1 = 0, 1


def _route_kernel(h_ref, mod_ref, wrt_ref, brt_ref, pos_ref, cw_ref, plan_ref, carry_ref, ids_all):
    i = pl.program_id(0)
    tm = ROUTE_TM

    @pl.when(i == 0)
    def _():
        carry_ref[...] = jnp.zeros_like(carry_ref)

    shift = mod_ref[0:1, 3 * D_MODEL:4 * D_MODEL]
    scale = mod_ref[0:1, 4 * D_MODEL:5 * D_MODEL]
    hm = _load_row_tiles(h_ref, tm) * (1.0 + scale) + shift

    hm_hi, hm_lo = _split_bf16(hm)
    w_hi, w_lo = _split_bf16(wrt_ref[...])
    nt = (((1,), (1,)), ((), ()))
    dg = functools.partial(lax.dot_general, dimension_numbers=nt, preferred_element_type=jnp.float32)
    logits = dg(w_hi, hm_hi) + (dg(w_hi, hm_lo) + dg(w_lo, hm_hi)) + brt_ref[:, 0:1]

    sub = lax.broadcasted_iota(jnp.int32, (ROUTE_LOGIT_ROWS, tm), 0)
    big = jnp.int32(1 << 20)
    is_grp = sub < N_GROUPS
    gl = jnp.where(is_grp, logits, -jnp.inf)
    gmax = jnp.max(gl, axis=0, keepdims=True)
    gidx = jnp.min(jnp.where(gl == gmax, sub, big), axis=0, keepdims=True)
    gsum = jnp.sum(jnp.where(is_grp, jnp.exp(logits - gmax), 0.0), axis=0, keepdims=True)
    p_group = 1.0 / gsum

    eid = sub - N_GROUPS
    sel = (eid >= 0) & (eid < N_EXPERTS) & (lax.shift_right_arithmetic(eid, 3) == gidx)
    el = jnp.where(sel, logits, -jnp.inf)
    l0 = jnp.max(el, axis=0, keepdims=True)
    i0 = jnp.min(jnp.where(el == l0, sub, big), axis=0, keepdims=True)
    el2 = jnp.where(sub == i0, -jnp.inf, el)
    l1 = jnp.max(el2, axis=0, keepdims=True)
    i1 = jnp.min(jnp.where(el2 == l1, sub, big), axis=0, keepdims=True)
    t = jnp.exp(l1 - l0)
    w0 = 1.0 / (1.0 + t)
    w1 = t / (1.0 + t)

    half_rows = jnp.where(i >= pl.num_programs(0) // SEQ_PARTS, N_EXPERTS, 0)
    i0 = i0 + half_rows
    i1 = i1 + half_rows
    subs = lax.broadcasted_iota(jnp.int32, (ROUTE_SEG_ROWS, tm), 0)
    onehot = jnp.where((subs == i0) | (subs == i1), 1.0, 0.0)
    rr = lax.broadcasted_iota(jnp.int32, (tm, tm), 0)
    cc = lax.broadcasted_iota(jnp.int32, (tm, tm), 1)
    earlier = jnp.where(rr < cc, 1.0, 0.0).astype(jnp.bfloat16)
    carry = carry_ref[:, 0:1]
    prefix = _bdot(onehot.astype(jnp.bfloat16), earlier) + carry
    r0 = jnp.sum(jnp.where(subs == i0, prefix, 0.0), axis=0, keepdims=True)
    r1 = jnp.sum(jnp.where(subs == i1, prefix, 0.0), axis=0, keepdims=True)
    total = jnp.broadcast_to(carry + jnp.sum(onehot, axis=1, keepdims=True), carry_ref.shape)
    carry_ref[...] = total

    sub8 = lax.broadcasted_iota(jnp.int32, (ROW_TILE, tm), 0)
    ids = jnp.zeros((ROW_TILE, tm), jnp.int32)
    for idx, val in ((ID_E0, i0 - N_GROUPS), (ID_E1, i1 - N_GROUPS),
                     (ID_R0, r0.astype(jnp.int32)), (ID_R1, r1.astype(jnp.int32))):
        ids = jnp.where(sub8 == idx, val, ids)
    ids_all[:, pl.ds(pl.multiple_of(i * tm, tm), tm)] = ids

    cw8 = jnp.where(sub8 == CW_C0, p_group * w0, jnp.where(sub8 == CW_C1, p_group * w1, 0.0))
    cw_ref[...] = jnp.concatenate([cw8, jnp.zeros((LANES - ROW_TILE, tm), jnp.float32)], axis=0).T

    @pl.when(i == pl.num_programs(0) - 1)
    def _():
        subq = lax.broadcasted_iota(jnp.int32, (LANES, LANES), 0)
        laneq = lax.broadcasted_iota(jnp.int32, (LANES, LANES), 1)
        total = jnp.concatenate([carry_ref[...], jnp.zeros((LANES - ROUTE_SEG_ROWS, LANES), jnp.float32)], axis=0)
        cnt = total.astype(jnp.int32)
        tiles = lax.shift_right_logical(cnt + (EXP_TM - 1), EXP_TM.bit_length() - 1).astype(jnp.float32)
        incl = jnp.where(laneq <= subq, 1.0, 0.0).astype(jnp.bfloat16)
        tile_end = _bdot(incl, tiles.astype(jnp.bfloat16))
        tile_start = tile_end - tiles
        seg = (tile_start * EXP_TM).astype(jnp.int32)
        nact = jnp.max(tile_end, axis=0, keepdims=True)

        ids_full = ids_all[...]
        look = jnp.zeros_like(ids_full)
        for e in range(N_SEG):
            look = jnp.where(ids_full == e, seg[N_GROUPS + e, 0], look)
        pos01 = look + pltpu.roll(ids_full, ID_R0 - ID_E0, axis=0)
        assert EXP_TILES * EXP_TM <= 1 << POS_BITS
        pos_ref[...] = pos01 | (pltpu.roll(pos01, ROW_TILE - 1, axis=0) << POS_BITS)

        subp = lax.broadcasted_iota(jnp.int32, (LANES, PLAN_W), 0)
        tile = lax.broadcasted_iota(jnp.int32, (LANES, PLAN_W), 1).astype(jnp.float32)
        is_exp = (subp >= N_GROUPS) & (subp < N_GROUPS + N_SEG)
        end_col = tile_end[:, 0:1]
        nact_s = nact[:, 0:1]
        te = jnp.sum(jnp.where(is_exp & (tile >= end_col), 1.0, 0.0), axis=0, keepdims=True)
        te_last = jnp.sum(jnp.where(is_exp & (nact_s - 1.0 >= end_col), 1.0, 0.0), axis=0, keepdims=True)[:, 0:1]
        tile_row = tile[0:1, :]
        te = jnp.minimum(jnp.where(tile_row < nact_s, te, te_last), N_SEG - 1.0)
        mine = (subp - N_GROUPS).astype(jnp.float32) == te
        cnt_sel = jnp.sum(jnp.where(mine, total[:, 0:1], 0.0), axis=0, keepdims=True)
        start_sel = jnp.sum(jnp.where(mine, tile_start[:, 0:1], 0.0), axis=0, keepdims=True)
        end_sel = jnp.sum(jnp.where(mine, end_col, 0.0), axis=0, keepdims=True)
        valid = jnp.clip(cnt_sel - (tile_row - start_sel) * EXP_TM, 0.0, float(EXP_TM))
        valid = jnp.where(tile_row < nact_s, valid, 0.0)
        subr = lax.broadcasted_iota(jnp.int32, (ROW_TILE, PLAN_W), 0)
        plan = jnp.where(subr == PLAN_EXPERT, te, jnp.where(subr == PLAN_VALID, valid,
                         jnp.where(subr == PLAN_NACT, nact_s, jnp.where(subr == PLAN_NEXT, end_sel, 0.0))))
        plan_ref[...] = plan.astype(jnp.int32)


def _route_call(h, mod, wrt, brt):
    tm = ROUTE_TM
    return pl.pallas_call(
        _route_kernel,
        grid=(SEQ // tm,),
        in_specs=[
            pl.BlockSpec((tm * ROW_TILE, LANES), lambda i: (i, 0)),
            pl.BlockSpec(mod.shape, lambda i: (0, 0)),
            pl.BlockSpec((ROUTE_LOGIT_ROWS, D_MODEL), lambda i: (0, 0)),
            pl.BlockSpec((ROUTE_LOGIT_ROWS, LANES), lambda i: (0, 0)),
        ],
        out_specs=[
            pl.BlockSpec((ROW_TILE, SEQ), lambda i: (0, 0)),
            pl.BlockSpec((tm, LANES), lambda i: (i, 0)),
            pl.BlockSpec((ROW_TILE, PLAN_W), lambda i: (0, 0)),
        ],
        out_shape=[
            jax.ShapeDtypeStruct((ROW_TILE, SEQ), jnp.int32),
            jax.ShapeDtypeStruct((SEQ, LANES), jnp.float32),
            jax.ShapeDtypeStruct((ROW_TILE, PLAN_W), jnp.int32),
        ],
        scratch_shapes=[pltpu.VMEM((ROUTE_SEG_ROWS, LANES), jnp.float32),
                        pltpu.VMEM((ROW_TILE, SEQ), jnp.int32)],
        compiler_params=pltpu.CompilerParams(
            dimension_semantics=("arbitrary",), vmem_limit_bytes=VMEM_LIMIT),
        name="route",
    )(h, mod, wrt, brt)


SRC_UNROLL = 8
EXP_CHUNK = 256
CAST_CHUNKS = 8
PREP_AFTER_DOWN = 1
TILE_ROWS = EXP_TM * ROW_TILE


def _experts_kernel(te_ref, tv_ref, nact_ref, tnext_ref, pos_ref,
                    h_hbm, mod_ref, wg_hbm, wu_hbm, wd_hbm,
                    y_ref,
                    src_ref, ord_ref, hres, xbuf, xmat, wgs, wus, wds,
                    wgb0, wub0, wdb0, wgb1, wub1, wdb1, rsem, wsem):
    i = pl.program_id(0)
    last = pl.num_programs(0) - 1
    nact = nact_ref[0]
    wsets = ((wgb0, wub0, wdb0), (wgb1, wub1, wdb1))
    xcur = lax.rem(i, 2)
    part = lax.shift_right_logical(te_ref[i], N_EXPERTS.bit_length() - 1)

    def weight_copies(segment, st):
        e = segment & (N_EXPERTS - 1)
        return [pltpu.make_async_copy(w_hbm.at[e], stage.at[st], wsem.at[st, n])
                for n, (w_hbm, stage) in enumerate(((wg_hbm, wgs), (wu_hbm, wus), (wd_hbm, wds)))]

    def tile_after(t):
        return tnext_ref[jnp.minimum(t, last)]

    def gather_row(tile, k, s):
        local = (src_ref[tile * EXP_TM + k] - part * PART_TOKENS) & (PART_TOKENS - 1)
        xbuf[s, k * ROW_TILE:(k + 1) * ROW_TILE, :] = hres[pl.ds(pl.multiple_of(local * ROW_TILE, ROW_TILE),
                                                             ROW_TILE), :]

    def gather_items(tile, s):
        return [functools.partial(gather_row, tile, k, s) for k in range(EXP_TM)]

    def prepare_input(xs):
        x = _load_row_tiles(xbuf, EXP_TM, lead=(xs,))
        shift = mod_ref[0:1, 3 * D_MODEL:4 * D_MODEL]
        scale = mod_ref[0:1, 4 * D_MODEL:5 * D_MODEL]
        xmat[xs] = (x * (1.0 + scale) + shift).astype(jnp.bfloat16)

    def compute_chunks(xs, ws):
        wgb, wub, wdb = wsets[ws]

        state = {"act": []}

        def gate(c):
            def run():
                state["a"] = _bdot(xmat[xs], wgb[:, c * EXP_CHUNK:(c + 1) * EXP_CHUNK])
            return run

        def up(c):
            def run():
                a = state["a"]
                u = _bdot(xmat[xs], wub[:, c * EXP_CHUNK:(c + 1) * EXP_CHUNK])
                state["act"].append((a * jax.nn.sigmoid(a) * u).astype(jnp.bfloat16))
            return run

        def down(c):
            def run():
                if c == 0:
                    state["actf"] = jnp.concatenate(state["act"], axis=-1)
                yc = _bdot(state["actf"], wdb[:, c * EXP_CHUNK:(c + 1) * EXP_CHUNK])
                for jj in range(EXP_CHUNK // LANES):
                    j = c * (EXP_CHUNK // LANES) + jj
                    y_ref[pl.ds(j, EXP_TM, stride=ROW_TILE), :] = yc[:, jj * LANES:(jj + 1) * LANES]
            return run

        first = []
        for c in range(D_EXPERT // EXP_CHUNK):
            first += [gate(c), up(c)]
        return first, [down(c) for c in range(D_MODEL // EXP_CHUNK)]

    def resident_copy(p):
        rows = PART_TOKENS * ROW_TILE
        return pltpu.make_async_copy(h_hbm.at[pl.ds(pl.multiple_of(p * rows, rows), rows), :], hres, rsem.at[0])

    @pl.when(i == 0)
    def _():
        resident_copy(part).start()
        ord_ref[0] = 0
        for cp in weight_copies(te_ref[0], 0):
            cp.start()
        second = tile_after(0)

        @pl.when(second < nact)
        def _():
            for cp in weight_copies(te_ref[jnp.minimum(second, last)], 1):
                cp.start()

        def fill_body(tt, c):
            ts = [tt * SRC_UNROLL + u for u in range(SRC_UNROLL)]
            words = [pos_ref[t] for t in ts]
            for t, w in zip(ts, words):
                src_ref[w & ((1 << POS_BITS) - 1)] = t
                src_ref[lax.shift_right_logical(w, POS_BITS)] = t
            return c
        lax.fori_loop(0, SEQ // SRC_UNROLL, fill_body, 0)

        def pad_tile(t, c):
            pad_tok = lax.shift_right_logical(te_ref[t], N_EXPERTS.bit_length() - 1) * PART_TOKENS

            def pad_row(k, c2):
                src_ref[t * EXP_TM + k] = pad_tok
                return c2
            return lax.fori_loop(tv_ref[t], EXP_TM, pad_row, c)
        lax.fori_loop(0, nact, pad_tile, 0)

    active = i < nact
    prev = jnp.maximum(i - 1, 0)
    new_segment = (i == 0) | (te_ref[i] != te_ref[prev])

    @pl.when(active & ((i == 0) | (part != lax.shift_right_logical(te_ref[prev], N_EXPERTS.bit_length() - 1))))
    def _():
        @pl.when(i > 0)
        def _():
            resident_copy(part).start()
        resident_copy(part).wait()
        for item in gather_items(i, xcur):
            item()
        prepare_input(xcur)

    @pl.when(active & new_segment & (i > 0))
    def _():
        ord_ref[0] = ord_ref[0] + 1

    parity = ord_ref[0] & 1
    after1 = tile_after(i)
    after2 = tile_after(after1)
    after3 = tile_after(after2)

    def cast_items(st):
        def block(stage, dst, rows):
            def run():
                dst[rows, :] = stage[st, rows, :].astype(jnp.bfloat16)
            return run
        out = []
        for r in range(CAST_CHUNKS):
            up_rows = slice(r * D_MODEL // CAST_CHUNKS, (r + 1) * D_MODEL // CAST_CHUNKS)
            dn_rows = slice(r * D_EXPERT // CAST_CHUNKS, (r + 1) * D_EXPERT // CAST_CHUNKS)
            out += [block(stage, dst, rows)
                    for stage, dst, rows in zip((wgs, wus, wds), wsets[st], (up_rows, up_rows, dn_rows))]
        return out

    @pl.when(i == 0)
    def _():
        for cp in weight_copies(te_ref[0], 0):
            cp.wait()
        for item in cast_items(0):
            item()

        @pl.when(after2 < nact)
        def _():
            for cp in weight_copies(te_ref[jnp.minimum(after2, last)], 0):
                cp.start()

    def run_tile(par, extra_items):
        first, second = compute_chunks(xcur, par)
        gathers = gather_items(jnp.minimum(i + 1, nact - 1), 1 - xcur)
        per = -(-len(gathers) // len(first))
        per_extra = -(-len(extra_items) // (len(first) + len(second)))
        for n, chunk in enumerate(first + second):
            if n < len(first):
                for item in gathers[n * per:(n + 1) * per]:
                    item()
            for item in extra_items[n * per_extra:(n + 1) * per_extra]:
                item()
            chunk()
            if n == len(first) + PREP_AFTER_DOWN:
                prepare_input(1 - xcur)

    last_of_segment = (i + 1 >= nact) | (te_ref[jnp.minimum(i + 1, last)] != te_ref[i])
    cast_next = active & last_of_segment & (after1 < nact)

    for par in range(2):
        @pl.when(cast_next & (parity == par))
        def _():
            for cp in weight_copies(te_ref[jnp.minimum(after1, last)], 1 - par):
                cp.wait()
            run_tile(par, cast_items(1 - par))

            @pl.when(after3 < nact)
            def _():
                for cp in weight_copies(te_ref[jnp.minimum(after3, last)], 1 - par):
                    cp.start()

        @pl.when(active & jnp.logical_not(cast_next) & (parity == par))
        def _():
            run_tile(par, [])

    @pl.when(jnp.logical_not(active))
    def _():
        y_ref[...] = jnp.zeros_like(y_ref)


def _experts_call(te, tv, nact, tnext, pos, h, mod, wg, wu, wd):
    grid_spec = pltpu.PrefetchScalarGridSpec(
        num_scalar_prefetch=5,
        grid=(EXP_TILES,),
        in_specs=[
            pl.BlockSpec(memory_space=pl.ANY),
            pl.BlockSpec(mod.shape, lambda i, *_: (0, 0)),
            pl.BlockSpec(memory_space=pl.ANY),
            pl.BlockSpec(memory_space=pl.ANY),
            pl.BlockSpec(memory_space=pl.ANY),
        ],
        out_specs=pl.BlockSpec((TILE_ROWS, LANES), lambda i, *_: (i, 0)),
        scratch_shapes=[
            pltpu.SMEM((EXP_TILES * EXP_TM,), jnp.int32),
            pltpu.SMEM((1,), jnp.int32),
            pltpu.VMEM((PART_TOKENS * ROW_TILE, LANES), jnp.float32),
            pltpu.VMEM((2, TILE_ROWS, LANES), jnp.float32),
            pltpu.VMEM((2, EXP_TM, D_MODEL), jnp.bfloat16),
            pltpu.VMEM((2, D_MODEL, D_EXPERT), jnp.float32),
            pltpu.VMEM((2, D_MODEL, D_EXPERT), jnp.float32),
            pltpu.VMEM((2, D_EXPERT, D_MODEL), jnp.float32),
            pltpu.VMEM((D_MODEL, D_EXPERT), jnp.bfloat16),
            pltpu.VMEM((D_MODEL, D_EXPERT), jnp.bfloat16),
            pltpu.VMEM((D_EXPERT, D_MODEL), jnp.bfloat16),
            pltpu.VMEM((D_MODEL, D_EXPERT), jnp.bfloat16),
            pltpu.VMEM((D_MODEL, D_EXPERT), jnp.bfloat16),
            pltpu.VMEM((D_EXPERT, D_MODEL), jnp.bfloat16),
            pltpu.SemaphoreType.DMA((1,)),
            pltpu.SemaphoreType.DMA((2, 3)),
        ],
    )
    return pl.pallas_call(
        _experts_kernel,
        grid_spec=grid_spec,
        out_shape=jax.ShapeDtypeStruct((EXP_TILES * TILE_ROWS, LANES), jnp.float32),
        compiler_params=pltpu.CompilerParams(
            dimension_semantics=("arbitrary",), vmem_limit_bytes=EXPERTS_VMEM_LIMIT),
        name="experts",
    )(te, tv, nact, tnext, pos, h, mod, wg, wu, wd)


def _combine_kernel(pos_ref, h_ref, cw_ref, mod_ref, g_ref, b_ref, ys_hbm, o_ref, ybuf, sem):
    i = pl.program_id(0)
    tm = CMB_TM
    slot = lax.rem(i, 2)

    def start_row(tile, k, s):
        word = pos_ref[tile * tm + k]
        for half, p in ((0, word & ((1 << POS_BITS) - 1)), (1, lax.shift_right_logical(word, POS_BITS))):
            pltpu.make_async_copy(ys_hbm.at[pl.ds(pl.multiple_of(p * ROW_TILE, ROW_TILE), ROW_TILE), :],
                                  ybuf.at[s, half, pl.ds(pl.multiple_of(k * ROW_TILE, ROW_TILE), ROW_TILE), :],
                                  sem.at[s]).start(priority=half)

    @pl.when(i == 0)
    def _():
        def body(kk, c):
            for u in range(SRC_UNROLL):
                start_row(0, kk * SRC_UNROLL + u, 0)
            return c
        lax.fori_loop(0, tm // SRC_UNROLL, body, 0)

    for half in range(2):
        pltpu.make_async_copy(ys_hbm.at[pl.ds(0, tm * ROW_TILE), :], ybuf.at[slot, half], sem.at[slot]).wait()

    g2 = mod_ref[0:1, 5 * D_MODEL:6 * D_MODEL]

    def chunk(c):
        rows = slice(c * CMB_CHUNK, (c + 1) * CMB_CHUNK)
        y0 = _load_row_tiles(ybuf, CMB_CHUNK, lead=(slot, 0), first=c * CMB_CHUNK)
        y1 = _load_row_tiles(ybuf, CMB_CHUNK, lead=(slot, 1), first=c * CMB_CHUNK)
        ffn = cw_ref[rows, CW_C0:CW_C0 + 1] * y0 + cw_ref[rows, CW_C1:CW_C1 + 1] * y1
        h = _load_row_tiles(h_ref, CMB_CHUNK, first=c * CMB_CHUNK)
        o_ref[rows, :] = _layer_norm(DEEPNORM_ALPHA * h + g2 * ffn, g_ref[...], b_ref[...])

    n_chunks = tm // CMB_CHUNK

    @pl.when(i + 1 < pl.num_programs(0))
    def _():
        per = tm // n_chunks
        for c in range(n_chunks):
            for k in range(c * per, (c + 1) * per):
                start_row(i + 1, k, 1 - slot)
            chunk(c)

    @pl.when(i + 1 == pl.num_programs(0))
    def _():
        for c in range(n_chunks):
            chunk(c)


def _combine_call(pos, h, ys, cw, mod, g, b):
    tm = CMB_TM
    grid_spec = pltpu.PrefetchScalarGridSpec(
        num_scalar_prefetch=1,
        grid=(SEQ // tm,),
        in_specs=[
            pl.BlockSpec((tm * ROW_TILE, LANES), lambda i, *_: (i, 0)),
            pl.BlockSpec((tm, LANES), lambda i, *_: (i, 0)),
            pl.BlockSpec(mod.shape, lambda i, *_: (0, 0)),
            pl.BlockSpec((1, D_MODEL), lambda i, *_: (0, 0)),
            pl.BlockSpec((1, D_MODEL), lambda i, *_: (0, 0)),
            pl.BlockSpec(memory_space=pl.ANY),
        ],
        out_specs=pl.BlockSpec((tm, D_MODEL), lambda i, *_: (i, 0)),
        scratch_shapes=[
            pltpu.VMEM((2, 2, tm * ROW_TILE, LANES), jnp.float32),
            pltpu.SemaphoreType.DMA((2,)),
        ],
    )
    return pl.pallas_call(
        _combine_kernel,
        grid_spec=grid_spec,
        out_shape=jax.ShapeDtypeStruct((SEQ, D_MODEL), jnp.float32),
        compiler_params=pltpu.CompilerParams(
            dimension_semantics=("arbitrary",), vmem_limit_bytes=VMEM_LIMIT),
        name="combine",
    )(pos, h, cw, mod, g, b, ys)


def kernel(x, c, ctx, c_ctx, ln_in_g, ln_in_b, w_mod, b_mod, w_in, rpb, w_pool_grp, pool_scale,
           w_attn_proj, w_pool_proj, w_out, ln1_g, ln1_b, w_router_group, b_router_group,
           w_router_expert, b_router_expert, w_expert_gate, w_expert_up, w_expert_down, ln2_g, ln2_b):
    assert x.shape == (1, SEQ, D_MODEL) and ctx.shape == (1, CTX_LEN, D_MODEL)
    assert w_mod.shape[0] == 1, "single-layer trunk"
    f32, bf16 = jnp.float32, jnp.bfloat16
    row = lambda v: v.reshape(1, -1).astype(f32)

    cond = jnp.concatenate([c, c_ctx[None], jnp.zeros((MOD_ROWS - 2, D_MODEL), f32)], axis=0)
    mod = _mod_call(cond, w_mod[0], row(b_mod[0]))

    lng, lnb = row(ln_in_g), row(ln_in_b)
    w_in_b = w_in[0].astype(bf16)
    u, h0 = _proj_call(x[0], mod, lng, lnb, w_in_b, mod_row=0, latent=True, tm=PROJ_TM)
    kvc, = _proj_call(ctx[0], mod, lng, lnb, w_in_b[:, ATT_W:3 * ATT_W], mod_row=1, latent=False, tm=CTX_LEN)

    h1 = _mix_call(h0, mod, u, kvc, _attn_bias_table(rpb[0]),
                   w_pool_grp[0].astype(bf16), row(pool_scale[0]),
                   w_attn_proj[0].astype(bf16), w_pool_proj[0].astype(bf16), w_out[0].astype(bf16),
                   row(ln1_g[0]), row(ln1_b[0]))

    n_logit = N_GROUPS + N_EXPERTS
    wrt = jnp.concatenate([w_router_group[0].T, w_router_expert[0].T,
                           jnp.zeros((ROUTE_LOGIT_ROWS - n_logit, D_MODEL), f32)], axis=0)
    brt = jnp.concatenate([b_router_group[0], b_router_expert[0], jnp.zeros((ROUTE_LOGIT_ROWS - n_logit,), f32)])
    brt = jnp.broadcast_to(brt[:, None], (ROUTE_LOGIT_ROWS, LANES))
    pos, cw, plan = _route_call(h1, mod, wrt, brt)

    posw = pos[POS_PACKED]
    y = _experts_call(plan[PLAN_EXPERT, :EXP_TILES], plan[PLAN_VALID, :EXP_TILES], plan[PLAN_NACT, :1],
                      plan[PLAN_NEXT, :EXP_TILES], posw, h1, mod,
                      w_expert_gate[0], w_expert_up[0], w_expert_down[0])
    out = _combine_call(posw, h1, y, cw, mod, row(ln2_g[0]), row(ln2_b[0]))
    return out[None]
```

```python
import functools

import jax
import jax.numpy as jnp
from jax import lax
from jax.experimental import pallas as pl
from jax.experimental.pallas import tpu as pltpu

D_MODEL = 1024
SEQ = 16384
GRID_W = 64
ROWS = SEQ // GRID_W
CTX_LEN = 256
N_HEADS = 8
HEAD_DIM = 64
ATT_W = N_HEADS * HEAD_DIM
WIN_H = 8
WIN_W = 16
POOL_WINDOWS = (2, 4, 8, 16)
POOL_GROUPS = 4
POOL_DIM = 128
POOL_W = POOL_GROUPS * POOL_DIM
GATE_COL = 3 * ATT_W + POOL_W
N_GROUPS = 4
EXPERTS_PER_GROUP = 8
N_EXPERTS = N_GROUPS * EXPERTS_PER_GROUP
D_EXPERT = 512
N_MOD = 6
DEEPNORM_ALPHA = 2.0 ** 0.25
LN_EPS = 1e-5
NEG_INF = -1e30

LANES = 128
ROW_TILE = 8
MOD_ROWS = 8
PROJ_TM = 1024
PROJ_SUB = 256
MIX_ROWS = 8
MIX_TQ = MIX_ROWS * GRID_W
KV_HALO = 4 * GRID_W
POOL_HALO = 16
ROUTE_TM = 1024
ROUTE_LOGIT_ROWS = 40
ROUTE_SEG_ROWS = 72
EXP_TM = 256
SEQ_PARTS = 2
PART_TOKENS = SEQ // SEQ_PARTS
N_SEG = SEQ_PARTS * N_EXPERTS
EXP_TILES = 2 * SEQ // EXP_TM + N_SEG
CMB_TM = 512
CMB_CHUNK = 128
MERGE_ROWS = 256
BIAS_LANES = 1024
ATTN_AHEAD = 8
VMEM_LIMIT = 56 * 1024 * 1024
EXPERTS_VMEM_LIMIT = 60 * 1024 * 1024


def _layer_norm(x, g, b):
    mu = jnp.mean(x, axis=-1, keepdims=True)
    xc = x - mu
    var = jnp.mean(xc * xc, axis=-1, keepdims=True)
    return xc * lax.rsqrt(var + LN_EPS) * g + b


def _bdot(a, b):
    return jnp.dot(a, b, preferred_element_type=jnp.float32)


def _split_bf16(a):
    hi = a.astype(jnp.bfloat16)
    lo = (a - hi.astype(jnp.float32)).astype(jnp.bfloat16)
    return hi, lo


def _dot3(a, b):
    a_hi, a_lo = _split_bf16(a)
    b_hi, b_lo = _split_bf16(b)
    return _bdot(a_hi, b_hi) + (_bdot(a_hi, b_lo) + _bdot(a_lo, b_hi))


def _load_row_tiles(ref, tokens, lead=(), first=0):
    parts = [ref[(*lead, pl.ds(first * ROW_TILE + j, tokens, stride=ROW_TILE), slice(None))]
             for j in range(ROW_TILE)]
    return jnp.concatenate(parts, axis=-1)


def _store_row_tiles(ref, value, lead=()):
    tokens = value.shape[0]
    for j in range(ROW_TILE):
        ref[(*lead, pl.ds(j, tokens, stride=ROW_TILE), slice(None))] = value[:, j * LANES:(j + 1) * LANES]


def _mod_kernel(cond_ref, w_ref, b_ref, o_ref):
    cond = cond_ref[...]
    act = cond * jax.nn.sigmoid(cond)
    o_ref[...] = _dot3(act, w_ref[...]) + b_ref[...]


def _mod_call(cond, w_mod, b_mod):
    tn = 1536
    n = N_MOD * D_MODEL
    return pl.pallas_call(
        _mod_kernel,
        grid=(n // tn,),
        in_specs=[
            pl.BlockSpec((MOD_ROWS, D_MODEL), lambda i: (0, 0)),
            pl.BlockSpec((D_MODEL, tn), lambda i: (0, i)),
            pl.BlockSpec((1, tn), lambda i: (0, i)),
        ],
        out_specs=pl.BlockSpec((MOD_ROWS, tn), lambda i: (0, i)),
        out_shape=jax.ShapeDtypeStruct((MOD_ROWS, n), jnp.float32),
        compiler_params=pltpu.CompilerParams(
            dimension_semantics=("arbitrary",), vmem_limit_bytes=VMEM_LIMIT),
        name="mod",
    )(cond, w_mod, b_mod)


def _proj_kernel(x_ref, mod_ref, g_ref, b_ref, w_ref, o_ref, *h_out, mod_row, latent):
    shift = mod_ref[mod_row:mod_row + 1, 0:D_MODEL]
    scale = mod_ref[mod_row:mod_row + 1, D_MODEL:2 * D_MODEL]
    tm, n = o_ref.shape
    sub = min(tm, PROJ_SUB)

    def prep(r):
        rows = slice(r * sub, (r + 1) * sub)
        h = _layer_norm(x_ref[rows, :], g_ref[...], b_ref[...])
        if latent:
            h_out[0][rows, :] = h
        return (h * (1.0 + scale) + shift).astype(jnp.bfloat16)

    def finish(r, c, res):
        if latent and c == 0:
            lane = lax.broadcasted_iota(jnp.int32, (1, D_MODEL), 1)
            res = res * jnp.where(lane < ATT_W, HEAD_DIM ** -0.5, 1.0)
        if latent and c * D_MODEL >= GATE_COL:
            res = jax.nn.sigmoid(res)
        o_ref[r * sub:(r + 1) * sub, c * D_MODEL:(c + 1) * D_MODEL] = res.astype(jnp.bfloat16)

    hm = {0: prep(0)}
    waiting = None
    for r in range(tm // sub):
        for c in range(n // D_MODEL):
            res = _bdot(hm[r], w_ref[:, c * D_MODEL:(c + 1) * D_MODEL])
            if c == 0 and (r + 1) * sub < tm:
                hm[r + 1] = prep(r + 1)
            if waiting is not None:
                finish(*waiting)
            waiting = (r, c, res)
    finish(*waiting)


def _proj_call(x, mod, g, b, w, *, mod_row, latent, tm):
    rows, n = x.shape[0], w.shape[1]
    out_specs = [pl.BlockSpec((tm, n), lambda i: (i, 0))]
    out_shape = [jax.ShapeDtypeStruct((rows, n), jnp.bfloat16)]
    if latent:
        out_specs.append(pl.BlockSpec((tm, D_MODEL), lambda i: (i, 0)))
        out_shape.append(jax.ShapeDtypeStruct((rows, D_MODEL), jnp.float32))
    return pl.pallas_call(
        functools.partial(_proj_kernel, mod_row=mod_row, latent=latent),
        grid=(rows // tm,),
        in_specs=[
            pl.BlockSpec((tm, D_MODEL), lambda i: (i, 0)),
            pl.BlockSpec(mod.shape, lambda i: (0, 0)),
            pl.BlockSpec((1, D_MODEL), lambda i: (0, 0)),
            pl.BlockSpec((1, D_MODEL), lambda i: (0, 0)),
            pl.BlockSpec((D_MODEL, n), lambda i: (0, 0), pipeline_mode=pl.Buffered(1)),
        ],
        out_specs=out_specs,
        out_shape=out_shape,
        compiler_params=pltpu.CompilerParams(
            dimension_semantics=("arbitrary",), vmem_limit_bytes=VMEM_LIMIT),
        name="proj",
    )(x, mod, g, b, w)


def _attn_bias_table(rpb):
    col = jnp.arange(GRID_W, dtype=jnp.int32)
    col_start = jnp.clip(col - WIN_W // 2, 0, GRID_W - WIN_W)
    col_mask = (col[None, :] >= col_start[:, None]) & (col[None, :] < col_start[:, None] + WIN_W)
    col_off = jnp.clip(col[None, :] - col[:, None], 1 - WIN_W, WIN_W - 1) + (WIN_W - 1)
    onehot = (col_off[None] == jnp.arange(2 * WIN_W - 1, dtype=jnp.int32)[:, None, None]).astype(jnp.float32)
    tab = jnp.einsum("hrc,cqk->hqrk", rpb.astype(jnp.float32), onehot, precision=lax.Precision.HIGHEST)
    tab = jnp.where(col_mask[None, :, None, :], tab, NEG_INF)
    n_rows = 2 * WIN_H - 1
    flat = tab.reshape(N_HEADS // 2, 2 * GRID_W, n_rows * GRID_W)
    even = jnp.pad(flat, ((0, 0), (0, 0), (0, BIAS_LANES - n_rows * GRID_W)))
    odd = jnp.pad(flat[:, :, GRID_W:], ((0, 0), (0, 0), (0, BIAS_LANES - (n_rows - 1) * GRID_W)))

    def window_kernel(even_ref, odd_ref, o_ref):
        start = WIN_H - 1 - pl.program_id(0)
        base = pl.multiple_of(lax.shift_right_logical(start, 1) * LANES, LANES)
        width = WIN_H * GRID_W
        o_ref[0] = jnp.where((start & 1) == 0, even_ref[:, :, pl.ds(base, width)], odd_ref[:, :, pl.ds(base, width)])

    full = pl.BlockSpec(even.shape, lambda v: (0, 0, 0))
    return pl.pallas_call(
        window_kernel,
        grid=(WIN_H,),
        in_specs=[full, full],
        out_specs=pl.BlockSpec((1, N_HEADS // 2, 2 * GRID_W, WIN_H * GRID_W), lambda v: (v, 0, 0, 0)),
        out_shape=jax.ShapeDtypeStruct((WIN_H, N_HEADS // 2, 2 * GRID_W, WIN_H * GRID_W), jnp.float32),
        compiler_params=pltpu.CompilerParams(dimension_semantics=("arbitrary",), vmem_limit_bytes=VMEM_LIMIT),
        name="bias_table",
    )(even, odd)


def _mix_kernel(h_ref, mod_ref,
                q_ref, kp_ref, kc_ref, kn_ref, vp_ref, vc_ref, vn_ref,
                pp_ref, pc_ref, pn_ref, ga_ref, gb_ref,
                kvc_ref, bias_ref, wgrp_ref, pscale_ref, wap_ref, wpp_ref, wout_ref,
                ln1g_ref, ln1b_ref,
                o_ref,
                kbuf, vbuf, yabuf, pbuf, ypbuf, zbuf):
    b = pl.program_id(0)
    nb = pl.num_programs(0)

    kbuf[0:KV_HALO, :] = kp_ref[...]
    kbuf[KV_HALO:KV_HALO + MIX_TQ, :] = kc_ref[...]
    kbuf[KV_HALO + MIX_TQ:, :] = kn_ref[...]
    vbuf[0:KV_HALO, :] = vp_ref[...]
    vbuf[KV_HALO:KV_HALO + MIX_TQ, :] = vc_ref[...]
    vbuf[KV_HALO + MIX_TQ:, :] = vn_ref[...]

    lane = lax.broadcasted_iota(jnp.int32, (GRID_W, LANES), 1)
    first_head = lane < HEAD_DIM

    units = [(j, pair) for j in range(MIX_ROWS) for pair in range(N_HEADS // 2)]
    nt = (((1,), (1,)), ((), ()))

    def window(j):
        r = b * MIX_ROWS + j
        rs = jnp.clip(r - WIN_H // 2, 0, ROWS - WIN_H)
        return pl.multiple_of((rs - b * MIX_ROWS + WIN_H // 2) * GRID_W, GRID_W), r - rs

    def scores(j, pair):
        off, var = window(j)
        cols = slice(pair * LANES, (pair + 1) * LANES)
        q = q_ref[j * GRID_W:(j + 1) * GRID_W, cols]
        zero = jnp.zeros_like(q)
        q2 = jnp.concatenate([jnp.where(first_head, q, zero), jnp.where(first_head, zero, q)], axis=0)
        kw = kbuf[pl.ds(off, WIN_H * GRID_W), cols]
        s_loc = lax.dot_general(q2, kw, nt, preferred_element_type=jnp.float32) + bias_ref[var, pair]
        s_ctx = lax.dot_general(q2, kvc_ref[:, cols], nt, preferred_element_type=jnp.float32)
        return s_loc, s_ctx

    def values(j, pair, s_loc, s_ctx):
        off, _ = window(j)
        cols = slice(pair * LANES, (pair + 1) * LANES)
        vw = vbuf[pl.ds(off, WIN_H * GRID_W), cols]
        vctx = kvc_ref[:, ATT_W + pair * LANES:ATT_W + (pair + 1) * LANES]
        m = jnp.maximum(jnp.max(s_loc, axis=-1, keepdims=True), jnp.max(s_ctx, axis=-1, keepdims=True))
        p_loc = jnp.exp(s_loc - m).astype(jnp.bfloat16)
        p_ctx = jnp.exp(s_ctx - m).astype(jnp.bfloat16)
        ones = lambda rows: jnp.ones((rows, LANES), jnp.bfloat16)
        o2 = (_bdot(p_loc, jnp.concatenate([vw, ones(WIN_H * GRID_W)], axis=1))
              + _bdot(p_ctx, jnp.concatenate([vctx, ones(CTX_LEN)], axis=1)))
        o2 = o2[:, :LANES] / o2[:, LANES:]
        o_pair = jnp.where(first_head, o2[:GRID_W], o2[GRID_W:])
        yabuf[j * GRID_W:(j + 1) * GRID_W, cols] = o_pair.astype(jnp.bfloat16)

    pbuf[0:POOL_HALO, :] = jnp.where(b > 0, pp_ref[...].astype(jnp.float32), 0.0)
    pbuf[POOL_HALO:POOL_HALO + MIX_TQ, :] = pc_ref[...].astype(jnp.float32)
    pbuf[POOL_HALO + MIX_TQ:, :] = jnp.where(b < nb - 1, pn_ref[...].astype(jnp.float32), 0.0)
    edge = lax.broadcasted_iota(jnp.int32, (ROW_TILE, 1), 0)

    def pool_group(g):
        win = POOL_WINDOWS[g]
        lo, hi = win // 2, win - win // 2
        cols = slice(g * POOL_DIM, (g + 1) * POOL_DIM)
        acc = None
        for d in range(-lo, hi):
            term = pbuf[POOL_HALO + d:POOL_HALO + d + MIX_TQ, cols]
            acc = term if acc is None else acc + term
        assert max(lo, hi) <= ROW_TILE
        top = jnp.where(b == 0, 1.0 / (win - jnp.maximum(lo - edge, 0)).astype(jnp.float32), 1.0 / win)
        bot = jnp.where(b == nb - 1,
                        1.0 / (win - jnp.maximum(edge + hi - ROW_TILE, 0)).astype(jnp.float32), 1.0 / win)
        inv = jnp.concatenate([top, jnp.full((MIX_TQ - 2 * ROW_TILE, 1), 1.0 / win, jnp.float32), bot], axis=0)
        pooled = acc * inv - pbuf[POOL_HALO:POOL_HALO + MIX_TQ, cols]
        yp = _bdot(pooled.astype(jnp.bfloat16), wgrp_ref[g]) * pscale_ref[:, cols]
        ypbuf[:, cols] = yp.astype(jnp.bfloat16)

    def pooled_branch():
        zbuf[...] = gb_ref[...].astype(jnp.float32) * _bdot(ypbuf[...], wpp_ref[...])

    extra = {}
    for g in range(POOL_GROUPS):
        extra[(g + 1) * len(units) // (POOL_GROUPS + 2)] = functools.partial(pool_group, g)
    extra[(POOL_GROUPS + 1) * len(units) // (POOL_GROUPS + 2)] = pooled_branch
    pending = [scores(*u) for u in units[:ATTN_AHEAD]]
    for n, u in enumerate(units):
        if n + ATTN_AHEAD < len(units):
            pending.append(scores(*units[n + ATTN_AHEAD]))
        values(*u, *pending.pop(0))
        if n in extra:
            extra[n]()

    g1 = mod_ref[0:1, 2 * D_MODEL:3 * D_MODEL]
    n_chunks = MIX_TQ // MERGE_ROWS
    rows = lambda c: slice(c * MERGE_ROWS, (c + 1) * MERGE_ROWS)
    z, y = {}, {}

    def stage_a(c):
        ya = _bdot(yabuf[rows(c), :], wap_ref[...])
        z[c] = (ga_ref[rows(c), :].astype(jnp.float32) * ya + zbuf[rows(c), :]).astype(jnp.bfloat16)

    def stage_b(c):
        y[c] = _bdot(z.pop(c), wout_ref[...])

    def stage_c(c):
        hn = _layer_norm(DEEPNORM_ALPHA * h_ref[rows(c), :] + g1 * y.pop(c), ln1g_ref[...], ln1b_ref[...])
        for j in range(ROW_TILE):
            o_ref[pl.ds(c * MERGE_ROWS * ROW_TILE + j, MERGE_ROWS, stride=ROW_TILE), :] = hn[:, j * LANES:(j + 1) * LANES]

    for t in range(n_chunks + 2):
        if t < n_chunks:
            stage_a(t)
        if 0 <= t - 1 < n_chunks:
            stage_b(t - 1)
        if 0 <= t - 2 < n_chunks:
            stage_c(t - 2)


def _mix_call(h, mod, u, kvc, bias, wgrp, pscale, wap, wpp, wout, ln1g, ln1b):
    nb = SEQ // MIX_TQ
    halo_per_blk = MIX_TQ // KV_HALO
    n_halo = SEQ // KV_HALO
    ph_per_blk = MIX_TQ // POOL_HALO
    n_ph = SEQ // POOL_HALO

    def const(shape):
        return pl.BlockSpec(shape, lambda i: (0,) * len(shape), pipeline_mode=pl.Buffered(1))

    def prev_halo(c):
        return pl.BlockSpec((KV_HALO, ATT_W), lambda i: (jnp.maximum(i * halo_per_blk - 1, 0), c))

    def next_halo(c):
        return pl.BlockSpec((KV_HALO, ATT_W), lambda i: (jnp.minimum((i + 1) * halo_per_blk, n_halo - 1), c))

    def cur(c):
        return pl.BlockSpec((MIX_TQ, ATT_W), lambda i: (i, c))

    in_specs = [
        pl.BlockSpec((MIX_TQ, D_MODEL), lambda i: (i, 0)),
        const(mod.shape),
        cur(0),
        prev_halo(1), cur(1), next_halo(1),
        prev_halo(2), cur(2), next_halo(2),
        pl.BlockSpec((POOL_HALO, POOL_W), lambda i: (jnp.maximum(i * ph_per_blk - 1, 0), 3)),
        cur(3),
        pl.BlockSpec((POOL_HALO, POOL_W), lambda i: (jnp.minimum((i + 1) * ph_per_blk, n_ph - 1), 3)),
        pl.BlockSpec((MIX_TQ, D_MODEL), lambda i: (i, 2)),
        pl.BlockSpec((MIX_TQ, D_MODEL), lambda i: (i, 3)),
        const(kvc.shape), const(bias.shape), const(wgrp.shape), const(pscale.shape),
        const(wap.shape), const(wpp.shape), const(wout.shape),
        const((1, D_MODEL)), const((1, D_MODEL)),
    ]
    return pl.pallas_call(
        _mix_kernel,
        grid=(nb,),
        in_specs=in_specs,
        out_specs=pl.BlockSpec((MIX_TQ * ROW_TILE, LANES), lambda i: (i, 0)),
        out_shape=jax.ShapeDtypeStruct((SEQ * ROW_TILE, LANES), jnp.float32),
        scratch_shapes=[
            pltpu.VMEM((MIX_TQ + 2 * KV_HALO, ATT_W), jnp.bfloat16),
            pltpu.VMEM((MIX_TQ + 2 * KV_HALO, ATT_W), jnp.bfloat16),
            pltpu.VMEM((MIX_TQ, ATT_W), jnp.bfloat16),
            pltpu.VMEM((MIX_TQ + 2 * POOL_HALO, POOL_W), jnp.float32),
            pltpu.VMEM((MIX_TQ, POOL_W), jnp.bfloat16),
            pltpu.VMEM((MIX_TQ, D_MODEL), jnp.float32),
        ],
        compiler_params=pltpu.CompilerParams(
            dimension_semantics=("arbitrary",), vmem_limit_bytes=VMEM_LIMIT),
        name="mix",
    )(h, mod, u, u, u, u, u, u, u, u, u, u, u, u,
      kvc, bias, wgrp, pscale, wap, wpp, wout, ln1g, ln1b)


ID_E0, ID_E1, ID_R0, ID_R1 = 0, 1, 4, 5
POS_PACKED = 0
POS_BITS = 16
PLAN_EXPERT, PLAN_VALID, PLAN_NACT, PLAN_NEXT = 0, 1, 2, 3
PLAN_W = 2 * LANES
CW_C0, CW_C1 = 0, 1


def _route_kernel(h_ref, mod_ref, wrt_ref, brt_ref, pos_ref, cw_ref, plan_ref, carry_ref, ids_all):
    i = pl.program_id(0)
    tm = ROUTE_TM

    @pl.when(i == 0)
    def _():
        carry_ref[...] = jnp.zeros_like(carry_ref)

    shift = mod_ref[0:1, 3 * D_MODEL:4 * D_MODEL]
    scale = mod_ref[0:1, 4 * D_MODEL:5 * D_MODEL]
    hm = _load_row_tiles(h_ref, tm) * (1.0 + scale) + shift

    hm_hi, hm_lo = _split_bf16(hm)
    w_hi, w_lo = _split_bf16(wrt_ref[...])
    nt = (((1,), (1,)), ((), ()))
    dg = functools.partial(lax.dot_general, dimension_numbers=nt, preferred_element_type=jnp.float32)
    logits = dg(w_hi, hm_hi) + (dg(w_hi, hm_lo) + dg(w_lo, hm_hi)) + brt_ref[:, 0:1]

    sub = lax.broadcasted_iota(jnp.int32, (ROUTE_LOGIT_ROWS, tm), 0)
    big = jnp.int32(1 << 20)
    is_grp = sub < N_GROUPS
    gl = jnp.where(is_grp, logits, -jnp.inf)
    gmax = jnp.max(gl, axis=0, keepdims=True)
    gidx = jnp.min(jnp.where(gl == gmax, sub, big), axis=0, keepdims=True)
    gsum = jnp.sum(jnp.where(is_grp, jnp.exp(logits - gmax), 0.0), axis=0, keepdims=True)
    p_group = 1.0 / gsum

    eid = sub - N_GROUPS
    sel = (eid >= 0) & (eid < N_EXPERTS) & (lax.shift_right_arithmetic(eid, 3) == gidx)
    el = jnp.where(sel, logits, -jnp.inf)
    l0 = jnp.max(el, axis=0, keepdims=True)
    i0 = jnp.min(jnp.where(el == l0, sub, big), axis=0, keepdims=True)
    el2 = jnp.where(sub == i0, -jnp.inf, el)
    l1 = jnp.max(el2, axis=0, keepdims=True)
    i1 = jnp.min(jnp.where(el2 == l1, sub, big), axis=0, keepdims=True)
    t = jnp.exp(l1 - l0)
    w0 = 1.0 / (1.0 + t)
    w1 = t / (1.0 + t)

    half_rows = jnp.where(i >= pl.num_programs(0) // SEQ_PARTS, N_EXPERTS, 0)
    i0 = i0 + half_rows
    i1 = i1 + half_rows
    subs = lax.broadcasted_iota(jnp.int32, (ROUTE_SEG_ROWS, tm), 0)
    onehot = jnp.where((subs == i0) | (subs == i1), 1.0, 0.0)
    rr = lax.broadcasted_iota(jnp.int32, (tm, tm), 0)
    cc = lax.broadcasted_iota(jnp.int32, (tm, tm), 1)
    earlier = jnp.where(rr < cc, 1.0, 0.0).astype(jnp.bfloat16)
    carry = carry_ref[:, 0:1]
    prefix = _bdot(onehot.astype(jnp.bfloat16), earlier) + carry
    r0 = jnp.sum(jnp.where(subs == i0, prefix, 0.0), axis=0, keepdims=True)
    r1 = jnp.sum(jnp.where(subs == i1, prefix, 0.0), axis=0, keepdims=True)
    total = jnp.broadcast_to(carry + jnp.sum(onehot, axis=1, keepdims=True), carry_ref.shape)
    carry_ref[...] = total

    sub8 = lax.broadcasted_iota(jnp.int32, (ROW_TILE, tm), 0)
    ids = jnp.zeros((ROW_TILE, tm), jnp.int32)
    for idx, val in ((ID_E0, i0 - N_GROUPS), (ID_E1, i1 - N_GROUPS),
                     (ID_R0, r0.astype(jnp.int32)), (ID_R1, r1.astype(jnp.int32))):
        ids = jnp.where(sub8 == idx, val, ids)
    ids_all[:, pl.ds(pl.multiple_of(i * tm, tm), tm)] = ids

    cw8 = jnp.where(sub8 == CW_C0, p_group * w0, jnp.where(sub8 == CW_C1, p_group * w1, 0.0))
    cw_ref[...] = jnp.concatenate([cw8, jnp.zeros((LANES - ROW_TILE, tm), jnp.float32)], axis=0).T

    @pl.when(i == pl.num_programs(0) - 1)
    def _():
        subq = lax.broadcasted_iota(jnp.int32, (LANES, LANES), 0)
        laneq = lax.broadcasted_iota(jnp.int32, (LANES, LANES), 1)
        total = jnp.concatenate([carry_ref[...], jnp.zeros((LANES - ROUTE_SEG_ROWS, LANES), jnp.float32)], axis=0)
        cnt = total.astype(jnp.int32)
        tiles = lax.shift_right_logical(cnt + (EXP_TM - 1), EXP_TM.bit_length() - 1).astype(jnp.float32)
        incl = jnp.where(laneq <= subq, 1.0, 0.0).astype(jnp.bfloat16)
        tile_end = _bdot(incl, tiles.astype(jnp.bfloat16))
        tile_start = tile_end - tiles
        seg = (tile_start * EXP_TM).astype(jnp.int32)
        nact = jnp.max(tile_end, axis=0, keepdims=True)

        ids_full = ids_all[...]
        look = jnp.zeros_like(ids_full)
        for e in range(N_SEG):
            look = jnp.where(ids_full == e, seg[N_GROUPS + e, 0], look)
        pos01 = look + pltpu.roll(ids_full, ID_R0 - ID_E0, axis=0)
        assert EXP_TILES * EXP_TM <= 1 << POS_BITS
        pos_ref[...] = pos01 | (pltpu.roll(pos01, ROW_TILE - 1, axis=0) << POS_BITS)

        subp = lax.broadcasted_iota(jnp.int32, (LANES, PLAN_W), 0)
        tile = lax.broadcasted_iota(jnp.int32, (LANES, PLAN_W), 1).astype(jnp.float32)
        is_exp = (subp >= N_GROUPS) & (subp < N_GROUPS + N_SEG)
        end_col = tile_end[:, 0:1]
        nact_s = nact[:, 0:1]
        te = jnp.sum(jnp.where(is_exp & (tile >= end_col), 1.0, 0.0), axis=0, keepdims=True)
        te_last = jnp.sum(jnp.where(is_exp & (nact_s - 1.0 >= end_col), 1.0, 0.0), axis=0, keepdims=True)[:, 0:1]
        tile_row = tile[0:1, :]
        te = jnp.minimum(jnp.where(tile_row < nact_s, te, te_last), N_SEG - 1.0)
        mine = (subp - N_GROUPS).astype(jnp.float32) == te
        cnt_sel = jnp.sum(jnp.where(mine, total[:, 0:1], 0.0), axis=0, keepdims=True)
        start_sel = jnp.sum(jnp.where(mine, tile_start[:, 0:1], 0.0), axis=0, keepdims=True)
        end_sel = jnp.sum(jnp.where(mine, end_col, 0.0), axis=0, keepdims=True)
        valid = jnp.clip(cnt_sel - (tile_row - start_sel) * EXP_TM, 0.0, float(EXP_TM))
        valid = jnp.where(tile_row < nact_s, valid, 0.0)
        subr = lax.broadcasted_iota(jnp.int32, (ROW_TILE, PLAN_W), 0)
        plan = jnp.where(subr == PLAN_EXPERT, te, jnp.where(subr == PLAN_VALID, valid,
                         jnp.where(subr == PLAN_NACT, nact_s, jnp.where(subr == PLAN_NEXT, end_sel, 0.0))))
        plan_ref[...] = plan.astype(jnp.int32)


def _route_call(h, mod, wrt, brt):
    tm = ROUTE_TM
    return pl.pallas_call(
        _route_kernel,
        grid=(SEQ // tm,),
        in_specs=[
            pl.BlockSpec((tm * ROW_TILE, LANES), lambda i: (i, 0)),
            pl.BlockSpec(mod.shape, lambda i: (0, 0)),
            pl.BlockSpec((ROUTE_LOGIT_ROWS, D_MODEL), lambda i: (0, 0)),
            pl.BlockSpec((ROUTE_LOGIT_ROWS, LANES), lambda i: (0, 0)),
        ],
        out_specs=[
            pl.BlockSpec((ROW_TILE, SEQ), lambda i: (0, 0)),
            pl.BlockSpec((tm, LANES), lambda i: (i, 0)),
            pl.BlockSpec((ROW_TILE, PLAN_W), lambda i: (0, 0)),
        ],
        out_shape=[
            jax.ShapeDtypeStruct((ROW_TILE, SEQ), jnp.int32),
            jax.ShapeDtypeStruct((SEQ, LANES), jnp.float32),
            jax.ShapeDtypeStruct((ROW_TILE, PLAN_W), jnp.int32),
        ],
        scratch_shapes=[pltpu.VMEM((ROUTE_SEG_ROWS, LANES), jnp.float32),
                        pltpu.VMEM((ROW_TILE, SEQ), jnp.int32)],
        compiler_params=pltpu.CompilerParams(
            dimension_semantics=("arbitrary",), vmem_limit_bytes=VMEM_LIMIT),
        name="route",
    )(h, mod, wrt, brt)


SRC_UNROLL = 8
EXP_CHUNK = 256
CAST_CHUNKS = 8
PREP_AFTER_DOWN = 1
TILE_ROWS = EXP_TM * ROW_TILE


def _experts_kernel(te_ref, tv_ref, nact_ref, tnext_ref, pos_ref,
                    h_hbm, mod_ref, wg_hbm, wu_hbm, wd_hbm,
                    y_ref,
                    src_ref, ord_ref, hres, xbuf, xmat, wgs, wus, wds,
                    wgb0, wub0, wdb0, wgb1, wub1, wdb1, rsem, wsem):
    i = pl.program_id(0)
    last = pl.num_programs(0) - 1
    nact = nact_ref[0]
    wsets = ((wgb0, wub0, wdb0), (wgb1, wub1, wdb1))
    xcur = lax.rem(i, 2)
    part = lax.shift_right_logical(te_ref[i], N_EXPERTS.bit_length() - 1)

    def weight_copies(segment, st):
        e = segment & (N_EXPERTS - 1)
        return [pltpu.make_async_copy(w_hbm.at[e], stage.at[st], wsem.at[st, n])
                for n, (w_hbm, stage) in enumerate(((wg_hbm, wgs), (wu_hbm, wus), (wd_hbm, wds)))]

    def tile_after(t):
        return tnext_ref[jnp.minimum(t, last)]

    def gather_row(tile, k, s):
        local = (src_ref[tile * EXP_TM + k] - part * PART_TOKENS) & (PART_TOKENS - 1)
        xbuf[s, k * ROW_TILE:(k + 1) * ROW_TILE, :] = hres[pl.ds(pl.multiple_of(local * ROW_TILE, ROW_TILE),
                                                             ROW_TILE), :]

    def gather_items(tile, s):
        return [functools.partial(gather_row, tile, k, s) for k in range(EXP_TM)]

    def prepare_input(xs):
        x = _load_row_tiles(xbuf, EXP_TM, lead=(xs,))
        shift = mod_ref[0:1, 3 * D_MODEL:4 * D_MODEL]
        scale = mod_ref[0:1, 4 * D_MODEL:5 * D_MODEL]
        xmat[xs] = (x * (1.0 + scale) + shift).astype(jnp.bfloat16)

    def compute_chunks(xs, ws):
        wgb, wub, wdb = wsets[ws]

        state = {"act": []}

        def gate(c):
            def run():
                state["a"] = _bdot(xmat[xs], wgb[:, c * EXP_CHUNK:(c + 1) * EXP_CHUNK])
            return run

        def up(c):
            def run():
                a = state["a"]
                u = _bdot(xmat[xs], wub[:, c * EXP_CHUNK:(c + 1) * EXP_CHUNK])
                state["act"].append((a * jax.nn.sigmoid(a) * u).astype(jnp.bfloat16))
            return run

        def down(c):
            def run():
                if c == 0:
                    state["actf"] = jnp.concatenate(state["act"], axis=-1)
                yc = _bdot(state["actf"], wdb[:, c * EXP_CHUNK:(c + 1) * EXP_CHUNK])
                for jj in range(EXP_CHUNK // LANES):
                    j = c * (EXP_CHUNK // LANES) + jj
                    y_ref[pl.ds(j, EXP_TM, stride=ROW_TILE), :] = yc[:, jj * LANES:(jj + 1) * LANES]
            return run

        first = []
        for c in range(D_EXPERT // EXP_CHUNK):
            first += [gate(c), up(c)]
        return first, [down(c) for c in range(D_MODEL // EXP_CHUNK)]

    def resident_copy(p):
        rows = PART_TOKENS * ROW_TILE
        return pltpu.make_async_copy(h_hbm.at[pl.ds(pl.multiple_of(p * rows, rows), rows), :], hres, rsem.at[0])

    @pl.when(i == 0)
    def _():
        resident_copy(part).start()
        ord_ref[0] = 0
        for cp in weight_copies(te_ref[0], 0):
            cp.start()
        second = tile_after(0)

        @pl.when(second < nact)
        def _():
            for cp in weight_copies(te_ref[jnp.minimum(second, last)], 1):
                cp.start()

        def fill_body(tt, c):
            ts = [tt * SRC_UNROLL + u for u in range(SRC_UNROLL)]
            words = [pos_ref[t] for t in ts]
            for t, w in zip(ts, words):
                src_ref[w & ((1 << POS_BITS) - 1)] = t
                src_ref[lax.shift_right_logical(w, POS_BITS)] = t
            return c
        lax.fori_loop(0, SEQ // SRC_UNROLL, fill_body, 0)

        def pad_tile(t, c):
            pad_tok = lax.shift_right_logical(te_ref[t], N_EXPERTS.bit_length() - 1) * PART_TOKENS

            def pad_row(k, c2):
                src_ref[t * EXP_TM + k] = pad_tok
                return c2
            return lax.fori_loop(tv_ref[t], EXP_TM, pad_row, c)
        lax.fori_loop(0, nact, pad_tile, 0)

    active = i < nact
    prev = jnp.maximum(i - 1, 0)
    new_segment = (i == 0) | (te_ref[i] != te_ref[prev])

    @pl.when(active & ((i == 0) | (part != lax.shift_right_logical(te_ref[prev], N_EXPERTS.bit_length() - 1))))
    def _():
        @pl.when(i > 0)
        def _():
            resident_copy(part).start()
        resident_copy(part).wait()
        for item in gather_items(i, xcur):
            item()
        prepare_input(xcur)

    @pl.when(active & new_segment & (i > 0))
    def _():
        ord_ref[0] = ord_ref[0] + 1

    parity = ord_ref[0] & 1
    after1 = tile_after(i)
    after2 = tile_after(after1)
    after3 = tile_after(after2)

    def cast_items(st):
        def block(stage, dst, rows):
            def run():
                dst[rows, :] = stage[st, rows, :].astype(jnp.bfloat16)
            return run
        out = []
        for r in range(CAST_CHUNKS):
            up_rows = slice(r * D_MODEL // CAST_CHUNKS, (r + 1) * D_MODEL // CAST_CHUNKS)
            dn_rows = slice(r * D_EXPERT // CAST_CHUNKS, (r + 1) * D_EXPERT // CAST_CHUNKS)
            out += [block(stage, dst, rows)
                    for stage, dst, rows in zip((wgs, wus, wds), wsets[st], (up_rows, up_rows, dn_rows))]
        return out

    @pl.when(i == 0)
    def _():
        for cp in weight_copies(te_ref[0], 0):
            cp.wait()
        for item in cast_items(0):
            item()

        @pl.when(after2 < nact)
        def _():
            for cp in weight_copies(te_ref[jnp.minimum(after2, last)], 0):
                cp.start()

    def run_tile(par, extra_items):
        first, second = compute_chunks(xcur, par)
        gathers = gather_items(jnp.minimum(i + 1, nact - 1), 1 - xcur)
        per = -(-len(gathers) // len(first))
        per_extra = -(-len(extra_items) // (len(first) + len(second)))
        for n, chunk in enumerate(first + second):
            if n < len(first):
                for item in gathers[n * per:(n + 1) * per]:
                    item()
            for item in extra_items[n * per_extra:(n + 1) * per_extra]:
                item()
            chunk()
            if n == len(first) + PREP_AFTER_DOWN:
                prepare_input(1 - xcur)

    last_of_segment = (i + 1 >= nact) | (te_ref[jnp.minimum(i + 1, last)] != te_ref[i])
    cast_next = active & last_of_segment & (after1 < nact)

    for par in range(2):
        @pl.when(cast_next & (parity == par))
        def _():
            for cp in weight_copies(te_ref[jnp.minimum(after1, last)], 1 - par):
                cp.wait()
            run_tile(par, cast_items(1 - par))

            @pl.when(after3 < nact)
            def _():
                for cp in weight_copies(te_ref[jnp.minimum(after3, last)], 1 - par):
                    cp.start()

        @pl.when(active & jnp.logical_not(cast_next) & (parity == par))
        def _():
            run_tile(par, [])

    @pl.when(jnp.logical_not(active))
    def _():
        y_ref[...] = jnp.zeros_like(y_ref)


def _experts_call(te, tv, nact, tnext, pos, h, mod, wg, wu, wd):
    grid_spec = pltpu.PrefetchScalarGridSpec(
        num_scalar_prefetch=5,
        grid=(EXP_TILES,),
        in_specs=[
            pl.BlockSpec(memory_space=pl.ANY),
            pl.BlockSpec(mod.shape, lambda i, *_: (0, 0)),
            pl.BlockSpec(memory_space=pl.ANY),
            pl.BlockSpec(memory_space=pl.ANY),
            pl.BlockSpec(memory_space=pl.ANY),
        ],
        out_specs=pl.BlockSpec((TILE_ROWS, LANES), lambda i, *_: (i, 0)),
        scratch_shapes=[
            pltpu.SMEM((EXP_TILES * EXP_TM,), jnp.int32),
            pltpu.SMEM((1,), jnp.int32),
            pltpu.VMEM((PART_TOKENS * ROW_TILE, LANES), jnp.float32),
            pltpu.VMEM((2, TILE_ROWS, LANES), jnp.float32),
            pltpu.VMEM((2, EXP_TM, D_MODEL), jnp.bfloat16),
            pltpu.VMEM((2, D_MODEL, D_EXPERT), jnp.float32),
            pltpu.VMEM((2, D_MODEL, D_EXPERT), jnp.float32),
            pltpu.VMEM((2, D_EXPERT, D_MODEL), jnp.float32),
            pltpu.VMEM((D_MODEL, D_EXPERT), jnp.bfloat16),
            pltpu.VMEM((D_MODEL, D_EXPERT), jnp.bfloat16),
            pltpu.VMEM((D_EXPERT, D_MODEL), jnp.bfloat16),
            pltpu.VMEM((D_MODEL, D_EXPERT), jnp.bfloat16),
            pltpu.VMEM((D_MODEL, D_EXPERT), jnp.bfloat16),
            pltpu.VMEM((D_EXPERT, D_MODEL), jnp.bfloat16),
            pltpu.SemaphoreType.DMA((1,)),
            pltpu.SemaphoreType.DMA((2, 3)),
        ],
    )
    return pl.pallas_call(
        _experts_kernel,
        grid_spec=grid_spec,
        out_shape=jax.ShapeDtypeStruct((EXP_TILES * TILE_ROWS, LANES), jnp.float32),
        compiler_params=pltpu.CompilerParams(
            dimension_semantics=("arbitrary",), vmem_limit_bytes=EXPERTS_VMEM_LIMIT),
        name="experts",
    )(te, tv, nact, tnext, pos, h, mod, wg, wu, wd)


def _combine_kernel(pos_ref, h_ref, cw_ref, mod_ref, g_ref, b_ref, ys_hbm, o_ref, ybuf, sem):
    i = pl.program_id(0)
    tm = CMB_TM
    slot = lax.rem(i, 2)

    def start_row(tile, k, s):
        word = pos_ref[tile * tm + k]
        for half, p in ((0, word & ((1 << POS_BITS) - 1)), (1, lax.shift_right_logical(word, POS_BITS))):
            pltpu.make_async_copy(ys_hbm.at[pl.ds(pl.multiple_of(p * ROW_TILE, ROW_TILE), ROW_TILE), :],
                                  ybuf.at[s, half, pl.ds(pl.multiple_of(k * ROW_TILE, ROW_TILE), ROW_TILE), :],
                                  sem.at[s]).start(priority=half)

    @pl.when(i == 0)
    def _():
        def body(kk, c):
            for u in range(SRC_UNROLL):
                start_row(0, kk * SRC_UNROLL + u, 0)
            return c
        lax.fori_loop(0, tm // SRC_UNROLL, body, 0)

    for half in range(2):
        pltpu.make_async_copy(ys_hbm.at[pl.ds(0, tm * ROW_TILE), :], ybuf.at[slot, half], sem.at[slot]).wait()

    g2 = mod_ref[0:1, 5 * D_MODEL:6 * D_MODEL]

    def chunk(c):
        rows = slice(c * CMB_CHUNK, (c + 1) * CMB_CHUNK)
        y0 = _load_row_tiles(ybuf, CMB_CHUNK, lead=(slot, 0), first=c * CMB_CHUNK)
        y1 = _load_row_tiles(ybuf, CMB_CHUNK, lead=(slot, 1), first=c * CMB_CHUNK)
        ffn = cw_ref[rows, CW_C0:CW_C0 + 1] * y0 + cw_ref[rows, CW_C1:CW_C1 + 1] * y1
        h = _load_row_tiles(h_ref, CMB_CHUNK, first=c * CMB_CHUNK)
        o_ref[rows, :] = _layer_norm(DEEPNORM_ALPHA * h + g2 * ffn, g_ref[...], b_ref[...])

    n_chunks = tm // CMB_CHUNK

    @pl.when(i + 1 < pl.num_programs(0))
    def _():
        per = tm // n_chunks
        for c in range(n_chunks):
            for k in range(c * per, (c + 1) * per):
                start_row(i + 1, k, 1 - slot)
            chunk(c)

    @pl.when(i + 1 == pl.num_programs(0))
    def _():
        for c in range(n_chunks):
            chunk(c)


def _combine_call(pos, h, ys, cw, mod, g, b):
    tm = CMB_TM
    grid_spec = pltpu.PrefetchScalarGridSpec(
        num_scalar_prefetch=1,
        grid=(SEQ // tm,),
        in_specs=[
            pl.BlockSpec((tm * ROW_TILE, LANES), lambda i, *_: (i, 0)),
            pl.BlockSpec((tm, LANES), lambda i, *_: (i, 0)),
            pl.BlockSpec(mod.shape, lambda i, *_: (0, 0)),
            pl.BlockSpec((1, D_MODEL), lambda i, *_: (0, 0)),
            pl.BlockSpec((1, D_MODEL), lambda i, *_: (0, 0)),
            pl.BlockSpec(memory_space=pl.ANY),
        ],
        out_specs=pl.BlockSpec((tm, D_MODEL), lambda i, *_: (i, 0)),
        scratch_shapes=[
            pltpu.VMEM((2, 2, tm * ROW_TILE, LANES), jnp.float32),
            pltpu.SemaphoreType.DMA((2,)),
        ],
    )
    return pl.pallas_call(
        _combine_kernel,
        grid_spec=grid_spec,
        out_shape=jax.ShapeDtypeStruct((SEQ, D_MODEL), jnp.float32),
        compiler_params=pltpu.CompilerParams(
            dimension_semantics=("arbitrary",), vmem_limit_bytes=VMEM_LIMIT),
        name="combine",
    )(pos, h, cw, mod, g, b, ys)


def kernel(x, c, ctx, c_ctx, ln_in_g, ln_in_b, w_mod, b_mod, w_in, rpb, w_pool_grp, pool_scale,
           w_attn_proj, w_pool_proj, w_out, ln1_g, ln1_b, w_router_group, b_router_group,
           w_router_expert, b_router_expert, w_expert_gate, w_expert_up, w_expert_down, ln2_g, ln2_b):
    assert x.shape == (1, SEQ, D_MODEL) and ctx.shape == (1, CTX_LEN, D_MODEL)
    assert w_mod.shape[0] == 1, "single-layer trunk"
    f32, bf16 = jnp.float32, jnp.bfloat16
    row = lambda v: v.reshape(1, -1).astype(f32)

    cond = jnp.concatenate([c, c_ctx[None], jnp.zeros((MOD_ROWS - 2, D_MODEL), f32)], axis=0)
    mod = _mod_call(cond, w_mod[0], row(b_mod[0]))

    lng, lnb = row(ln_in_g), row(ln_in_b)
    w_in_b = w_in[0].astype(bf16)
    u, h0 = _proj_call(x[0], mod, lng, lnb, w_in_b, mod_row=0, latent=True, tm=PROJ_TM)
    kvc, = _proj_call(ctx[0], mod, lng, lnb, w_in_b[:, ATT_W:3 * ATT_W], mod_row=1, latent=False, tm=CTX_LEN)

    h1 = _mix_call(h0, mod, u, kvc, _attn_bias_table(rpb[0]),
                   w_pool_grp[0].astype(bf16), row(pool_scale[0]),
                   w_attn_proj[0].astype(bf16), w_pool_proj[0].astype(bf16), w_out[0].astype(bf16),
                   row(ln1_g[0]), row(ln1_b[0]))

    n_logit = N_GROUPS + N_EXPERTS
    wrt = jnp.concatenate([w_router_group[0].T, w_router_expert[0].T,
                           jnp.zeros((ROUTE_LOGIT_ROWS - n_logit, D_MODEL), f32)], axis=0)
    brt = jnp.concatenate([b_router_group[0], b_router_expert[0], jnp.zeros((ROUTE_LOGIT_ROWS - n_logit,), f32)])
    brt = jnp.broadcast_to(brt[:, None], (ROUTE_LOGIT_ROWS, LANES))
    pos, cw, plan = _route_call(h1, mod, wrt, brt)

    posw = pos[POS_PACKED]
    y = _experts_call(plan[PLAN_EXPERT, :EXP_TILES], plan[PLAN_VALID, :EXP_TILES], plan[PLAN_NACT, :1],
                      plan[PLAN_NEXT, :EXP_TILES], posw, h1, mod,
                      w_expert_gate[0], w_expert_up[0], w_expert_down[0])
    out = _combine_call(posw, h1, y, cw, mod, row(ln2_g[0]), row(ln2_b[0]))
    return out[None]
```

```python
import functools

import jax
import jax.numpy as jnp
from jax import lax
from jax.experimental import pallas as pl
from jax.experimental.pallas import tpu as pltpu

D_MODEL = 1024
SEQ = 16384
GRID_W = 64
ROWS = SEQ // GRID_W
CTX_LEN = 256
N_HEADS = 8
HEAD_DIM = 64
ATT_W = N_HEADS * HEAD_DIM
WIN_H = 8
WIN_W = 16
POOL_WINDOWS = (2, 4, 8, 16)
POOL_GROUPS = 4
POOL_DIM = 128
POOL_W = POOL_GROUPS * POOL_DIM
GATE_COL = 3 * ATT_W + POOL_W
N_GROUPS = 4
EXPERTS_PER_GROUP = 8
N_EXPERTS = N_GROUPS * EXPERTS_PER_GROUP
D_EXPERT = 512
N_MOD = 6
DEEPNORM_ALPHA = 2.0 ** 0.25
LN_EPS = 1e-5
NEG_INF = -1e30

LANES = 128
ROW_TILE = 8
MOD_ROWS = 8
PROJ_TM = 1024
PROJ_SUB = 256
MIX_ROWS = 8
MIX_TQ = MIX_ROWS * GRID_W
KV_HALO = 4 * GRID_W
POOL_HALO = 16
ROUTE_TM = 1024
ROUTE_LOGIT_ROWS = 40
ROUTE_SEG_ROWS = 72
EXP_TM = 256
SEQ_PARTS = 2
PART_TOKENS = SEQ // SEQ_PARTS
N_SEG = SEQ_PARTS * N_EXPERTS
EXP_TILES = 2 * SEQ // EXP_TM + N_SEG
CMB_TM = 512
CMB_CHUNK = 128
MERGE_ROWS = 256
BIAS_LANES = 1024
ATTN_AHEAD = 8
VMEM_LIMIT = 56 * 1024 * 1024
EXPERTS_VMEM_LIMIT = 60 * 1024 * 1024


def _layer_norm(x, g, b):
    mu = jnp.mean(x, axis=-1, keepdims=True)
    xc = x - mu
    var = jnp.mean(xc * xc, axis=-1, keepdims=True)
    return xc * lax.rsqrt(var + LN_EPS) * g + b


def _bdot(a, b):
    return jnp.dot(a, b, preferred_element_type=jnp.float32)


def _split_bf16(a):
    hi = a.astype(jnp.bfloat16)
    lo = (a - hi.astype(jnp.float32)).astype(jnp.bfloat16)
    return hi, lo


def _dot3(a, b):
    a_hi, a_lo = _split_bf16(a)
    b_hi, b_lo = _split_bf16(b)
    return _bdot(a_hi, b_hi) + (_bdot(a_hi, b_lo) + _bdot(a_lo, b_hi))


def _load_row_tiles(ref, tokens, lead=(), first=0):
    parts = [ref[(*lead, pl.ds(first * ROW_TILE + j, tokens, stride=ROW_TILE), slice(None))]
             for j in range(ROW_TILE)]
    return jnp.concatenate(parts, axis=-1)


def _store_row_tiles(ref, value, lead=()):
    tokens = value.shape[0]
    for j in range(ROW_TILE):
        ref[(*lead, pl.ds(j, tokens, stride=ROW_TILE), slice(None))] = value[:, j * LANES:(j + 1) * LANES]


def _mod_kernel(cond_ref, w_ref, b_ref, o_ref):
    cond = cond_ref[...]
    act = cond * jax.nn.sigmoid(cond)
    o_ref[...] = _dot3(act, w_ref[...]) + b_ref[...]


def _mod_call(cond, w_mod, b_mod):
    tn = 1536
    n = N_MOD * D_MODEL
    return pl.pallas_call(
        _mod_kernel,
        grid=(n // tn,),
        in_specs=[
            pl.BlockSpec((MOD_ROWS, D_MODEL), lambda i: (0, 0)),
            pl.BlockSpec((D_MODEL, tn), lambda i: (0, i)),
            pl.BlockSpec((1, tn), lambda i: (0, i)),
        ],
        out_specs=pl.BlockSpec((MOD_ROWS, tn), lambda i: (0, i)),
        out_shape=jax.ShapeDtypeStruct((MOD_ROWS, n), jnp.float32),
        compiler_params=pltpu.CompilerParams(
            dimension_semantics=("arbitrary",), vmem_limit_bytes=VMEM_LIMIT),
        name="mod",
    )(cond, w_mod, b_mod)


def _proj_kernel(x_ref, mod_ref, g_ref, b_ref, w_ref, o_ref, *h_out, mod_row, latent):
    shift = mod_ref[mod_row:mod_row + 1, 0:D_MODEL]
    scale = mod_ref[mod_row:mod_row + 1, D_MODEL:2 * D_MODEL]
    tm, n = o_ref.shape
    sub = min(tm, PROJ_SUB)

    def prep(r):
        rows = slice(r * sub, (r + 1) * sub)
        h = _layer_norm(x_ref[rows, :], g_ref[...], b_ref[...])
        if latent:
            h_out[0][rows, :] = h
        return (h * (1.0 + scale) + shift).astype(jnp.bfloat16)

    def finish(r, c, res):
        if latent and c == 0:
            lane = lax.broadcasted_iota(jnp.int32, (1, D_MODEL), 1)
            res = res * jnp.where(lane < ATT_W, HEAD_DIM ** -0.5, 1.0)
        if latent and c * D_MODEL >= GATE_COL:
            res = jax.nn.sigmoid(res)
        o_ref[r * sub:(r + 1) * sub, c * D_MODEL:(c + 1) * D_MODEL] = res.astype(jnp.bfloat16)

    hm = {0: prep(0)}
    waiting = None
    for r in range(tm // sub):
        for c in range(n // D_MODEL):
            res = _bdot(hm[r], w_ref[:, c * D_MODEL:(c + 1) * D_MODEL])
            if c == 0 and (r + 1) * sub < tm:
                hm[r + 1] = prep(r + 1)
            if waiting is not None:
                finish(*waiting)
            waiting = (r, c, res)
    finish(*waiting)


def _proj_call(x, mod, g, b, w, *, mod_row, latent, tm):
    rows, n = x.shape[0], w.shape[1]
    out_specs = [pl.BlockSpec((tm, n), lambda i: (i, 0))]
    out_shape = [jax.ShapeDtypeStruct((rows, n), jnp.bfloat16)]
    if latent:
        out_specs.append(pl.BlockSpec((tm, D_MODEL), lambda i: (i, 0)))
        out_shape.append(jax.ShapeDtypeStruct((rows, D_MODEL), jnp.float32))
    return pl.pallas_call(
        functools.partial(_proj_kernel, mod_row=mod_row, latent=latent),
        grid=(rows // tm,),
        in_specs=[
            pl.BlockSpec((tm, D_MODEL), lambda i: (i, 0)),
            pl.BlockSpec(mod.shape, lambda i: (0, 0)),
            pl.BlockSpec((1, D_MODEL), lambda i: (0, 0)),
            pl.BlockSpec((1, D_MODEL), lambda i: (0, 0)),
            pl.BlockSpec((D_MODEL, n), lambda i: (0, 0), pipeline_mode=pl.Buffered(1)),
        ],
        out_specs=out_specs,
        out_shape=out_shape,
        compiler_params=pltpu.CompilerParams(
            dimension_semantics=("arbitrary",), vmem_limit_bytes=VMEM_LIMIT),
        name="proj",
    )(x, mod, g, b, w)


def _attn_bias_table(rpb):
    col = jnp.arange(GRID_W, dtype=jnp.int32)
    col_start = jnp.clip(col - WIN_W // 2, 0, GRID_W - WIN_W)
    col_mask = (col[None, :] >= col_start[:, None]) & (col[None, :] < col_start[:, None] + WIN_W)
    col_off = jnp.clip(col[None, :] - col[:, None], 1 - WIN_W, WIN_W - 1) + (WIN_W - 1)
    onehot = (col_off[None] == jnp.arange(2 * WIN_W - 1, dtype=jnp.int32)[:, None, None]).astype(jnp.float32)
    tab = jnp.einsum("hrc,cqk->hqrk", rpb.astype(jnp.float32), onehot, precision=lax.Precision.HIGHEST)
    tab = jnp.where(col_mask[None, :, None, :], tab, NEG_INF)
    n_rows = 2 * WIN_H - 1
    flat = tab.reshape(N_HEADS // 2, 2 * GRID_W, n_rows * GRID_W)
    even = jnp.pad(flat, ((0, 0), (0, 0), (0, BIAS_LANES - n_rows * GRID_W)))
    odd = jnp.pad(flat[:, :, GRID_W:], ((0, 0), (0, 0), (0, BIAS_LANES - (n_rows - 1) * GRID_W)))

    def window_kernel(even_ref, odd_ref, o_ref):
        start = WIN_H - 1 - pl.program_id(0)
        base = pl.multiple_of(lax.shift_right_logical(start, 1) * LANES, LANES)
        width = WIN_H * GRID_W
        o_ref[0] = jnp.where((start & 1) == 0, even_ref[:, :, pl.ds(base, width)], odd_ref[:, :, pl.ds(base, width)])

    full = pl.BlockSpec(even.shape, lambda v: (0, 0, 0))
    return pl.pallas_call(
        window_kernel,
        grid=(WIN_H,),
        in_specs=[full, full],
        out_specs=pl.BlockSpec((1, N_HEADS // 2, 2 * GRID_W, WIN_H * GRID_W), lambda v: (v, 0, 0, 0)),
        out_shape=jax.ShapeDtypeStruct((WIN_H, N_HEADS // 2, 2 * GRID_W, WIN_H * GRID_W), jnp.float32),
        compiler_params=pltpu.CompilerParams(dimension_semantics=("arbitrary",), vmem_limit_bytes=VMEM_LIMIT),
        name="bias_table",
    )(even, odd)


def _mix_kernel(h_ref, mod_ref,
                q_ref, kp_ref, kc_ref, kn_ref, vp_ref, vc_ref, vn_ref,
                pp_ref, pc_ref, pn_ref, ga_ref, gb_ref,
                kvc_ref, bias_ref, wgrp_f32, pscale_ref, wap_f32, wpp_f32, wout_f32,
                ln1g_ref, ln1b_ref,
                o_ref,
                kbuf, vbuf, yabuf, pbuf, ypbuf, zbuf, wgrp_ref, wap_ref, wpp_ref, wout_ref):
    b = pl.program_id(0)
    nb = pl.num_programs(0)

    @pl.when(b == 0)
    def _():
        for src, dst in ((wgrp_f32, wgrp_ref), (wap_f32, wap_ref), (wpp_f32, wpp_ref), (wout_f32, wout_ref)):
            dst[...] = src[...].astype(jnp.bfloat16)

    kbuf[0:KV_HALO, :] = kp_ref[...]
    kbuf[KV_HALO:KV_HALO + MIX_TQ, :] = kc_ref[...]
    kbuf[KV_HALO + MIX_TQ:, :] = kn_ref[...]
    vbuf[0:KV_HALO, :] = vp_ref[...]
    vbuf[KV_HALO:KV_HALO + MIX_TQ, :] = vc_ref[...]
    vbuf[KV_HALO + MIX_TQ:, :] = vn_ref[...]

    lane = lax.broadcasted_iota(jnp.int32, (GRID_W, LANES), 1)
    first_head = lane < HEAD_DIM

    units = [(j, pair) for j in range(MIX_ROWS) for pair in range(N_HEADS // 2)]
    nt = (((1,), (1,)), ((), ()))

    def window(j):
        r = b * MIX_ROWS + j
        rs = jnp.clip(r - WIN_H // 2, 0, ROWS - WIN_H)
        return pl.multiple_of((rs - b * MIX_ROWS + WIN_H // 2) * GRID_W, GRID_W), r - rs

    def scores(j, pair):
        off, var = window(j)
        cols = slice(pair * LANES, (pair + 1) * LANES)
        q = q_ref[j * GRID_W:(j + 1) * GRID_W, cols]
        zero = jnp.zeros_like(q)
        q2 = jnp.concatenate([jnp.where(first_head, q, zero), jnp.where(first_head, zero, q)], axis=0)
        kw = kbuf[pl.ds(off, WIN_H * GRID_W), cols]
        s_loc = lax.dot_general(q2, kw, nt, preferred_element_type=jnp.float32) + bias_ref[var, pair]
        s_ctx = lax.dot_general(q2, kvc_ref[:, cols], nt, preferred_element_type=jnp.float32)
        return s_loc, s_ctx

    def values(j, pair, s_loc, s_ctx):
        off, _ = window(j)
        cols = slice(pair * LANES, (pair + 1) * LANES)
        vw = vbuf[pl.ds(off, WIN_H * GRID_W), cols]
        vctx = kvc_ref[:, ATT_W + pair * LANES:ATT_W + (pair + 1) * LANES]
        m = jnp.maximum(jnp.max(s_loc, axis=-1, keepdims=True), jnp.max(s_ctx, axis=-1, keepdims=True))
        p_loc = jnp.exp(s_loc - m).astype(jnp.bfloat16)
        p_ctx = jnp.exp(s_ctx - m).astype(jnp.bfloat16)
        ones = lambda rows: jnp.ones((rows, LANES), jnp.bfloat16)
        o2 = (_bdot(p_loc, jnp.concatenate([vw, ones(WIN_H * GRID_W)], axis=1))
              + _bdot(p_ctx, jnp.concatenate([vctx, ones(CTX_LEN)], axis=1)))
        o2 = o2[:, :LANES] / o2[:, LANES:]
        o_pair = jnp.where(first_head, o2[:GRID_W], o2[GRID_W:])
        yabuf[j * GRID_W:(j + 1) * GRID_W, cols] = o_pair.astype(jnp.bfloat16)

    pbuf[0:POOL_HALO, :] = jnp.where(b > 0, pp_ref[...].astype(jnp.float32), 0.0)
    pbuf[POOL_HALO:POOL_HALO + MIX_TQ, :] = pc_ref[...].astype(jnp.float32)
    pbuf[POOL_HALO + MIX_TQ:, :] = jnp.where(b < nb - 1, pn_ref[...].astype(jnp.float32), 0.0)
    edge = lax.broadcasted_iota(jnp.int32, (ROW_TILE, 1), 0)

    def pool_group(g):
        win = POOL_WINDOWS[g]
        lo, hi = win // 2, win - win // 2
        cols = slice(g * POOL_DIM, (g + 1) * POOL_DIM)
        acc = None
        for d in range(-lo, hi):
            term = pbuf[POOL_HALO + d:POOL_HALO + d + MIX_TQ, cols]
            acc = term if acc is None else acc + term
        assert max(lo, hi) <= ROW_TILE
        top = jnp.where(b == 0, 1.0 / (win - jnp.maximum(lo - edge, 0)).astype(jnp.float32), 1.0 / win)
        bot = jnp.where(b == nb - 1,
                        1.0 / (win - jnp.maximum(edge + hi - ROW_TILE, 0)).astype(jnp.float32), 1.0 / win)
        inv = jnp.concatenate([top, jnp.full((MIX_TQ - 2 * ROW_TILE, 1), 1.0 / win, jnp.float32), bot], axis=0)
        pooled = acc * inv - pbuf[POOL_HALO:POOL_HALO + MIX_TQ, cols]
        yp = _bdot(pooled.astype(jnp.bfloat16), wgrp_ref[g]) * pscale_ref[:, cols]
        ypbuf[:, cols] = yp.astype(jnp.bfloat16)

    def pooled_branch():
        zbuf[...] = gb_ref[...].astype(jnp.float32) * _bdot(ypbuf[...], wpp_ref[...])

    extra = {}
    for g in range(POOL_GROUPS):
        extra[(g + 1) * len(units) // (POOL_GROUPS + 2)] = functools.partial(pool_group, g)
    extra[(POOL_GROUPS + 1) * len(units) // (POOL_GROUPS + 2)] = pooled_branch
    pending = [scores(*u) for u in units[:ATTN_AHEAD]]
    for n, u in enumerate(units):
        if n + ATTN_AHEAD < len(units):
            pending.append(scores(*units[n + ATTN_AHEAD]))
        values(*u, *pending.pop(0))
        if n in extra:
            extra[n]()

    g1 = mod_ref[0:1, 2 * D_MODEL:3 * D_MODEL]
    n_chunks = MIX_TQ // MERGE_ROWS
    rows = lambda c: slice(c * MERGE_ROWS, (c + 1) * MERGE_ROWS)
    z, y = {}, {}

    def stage_a(c):
        ya = _bdot(yabuf[rows(c), :], wap_ref[...])
        z[c] = (ga_ref[rows(c), :].astype(jnp.float32) * ya + zbuf[rows(c), :]).astype(jnp.bfloat16)

    def stage_b(c):
        y[c] = _bdot(z.pop(c), wout_ref[...])

    def stage_c(c):
        hn = _layer_norm(DEEPNORM_ALPHA * h_ref[rows(c), :] + g1 * y.pop(c), ln1g_ref[...], ln1b_ref[...])
        for j in range(ROW_TILE):
            o_ref[pl.ds(c * MERGE_ROWS * ROW_TILE + j, MERGE_ROWS, stride=ROW_TILE), :] = hn[:, j * LANES:(j + 1) * LANES]

    for t in range(n_chunks + 2):
        if t < n_chunks:
            stage_a(t)
        if 0 <= t - 1 < n_chunks:
            stage_b(t - 1)
        if 0 <= t - 2 < n_chunks:
            stage_c(t - 2)


def _mix_call(h, mod, u, kvc, bias, wgrp, pscale, wap, wpp, wout, ln1g, ln1b):
    nb = SEQ // MIX_TQ
    halo_per_blk = MIX_TQ // KV_HALO
    n_halo = SEQ // KV_HALO
    ph_per_blk = MIX_TQ // POOL_HALO
    n_ph = SEQ // POOL_HALO

    def const(shape):
        return pl.BlockSpec(shape, lambda i: (0,) * len(shape), pipeline_mode=pl.Buffered(1))

    def prev_halo(c):
        return pl.BlockSpec((KV_HALO, ATT_W), lambda i: (jnp.maximum(i * halo_per_blk - 1, 0), c))

    def next_halo(c):
        return pl.BlockSpec((KV_HALO, ATT_W), lambda i: (jnp.minimum((i + 1) * halo_per_blk, n_halo - 1), c))

    def cur(c):
        return pl.BlockSpec((MIX_TQ, ATT_W), lambda i: (i, c))

    in_specs = [
        pl.BlockSpec((MIX_TQ, D_MODEL), lambda i: (i, 0)),
        const(mod.shape),
        cur(0),
        prev_halo(1), cur(1), next_halo(1),
        prev_halo(2), cur(2), next_halo(2),
        pl.BlockSpec((POOL_HALO, POOL_W), lambda i: (jnp.maximum(i * ph_per_blk - 1, 0), 3)),
        cur(3),
        pl.BlockSpec((POOL_HALO, POOL_W), lambda i: (jnp.minimum((i + 1) * ph_per_blk, n_ph - 1), 3)),
        pl.BlockSpec((MIX_TQ, D_MODEL), lambda i: (i, 2)),
        pl.BlockSpec((MIX_TQ, D_MODEL), lambda i: (i, 3)),
        const(kvc.shape), const(bias.shape), const(wgrp.shape), const(pscale.shape),
        const(wap.shape), const(wpp.shape), const(wout.shape),
        const((1, D_MODEL)), const((1, D_MODEL)),
    ]
    return pl.pallas_call(
        _mix_kernel,
        grid=(nb,),
        in_specs=in_specs,
        out_specs=pl.BlockSpec((MIX_TQ * ROW_TILE, LANES), lambda i: (i, 0)),
        out_shape=jax.ShapeDtypeStruct((SEQ * ROW_TILE, LANES), jnp.float32),
        scratch_shapes=[
            pltpu.VMEM((MIX_TQ + 2 * KV_HALO, ATT_W), jnp.bfloat16),
            pltpu.VMEM((MIX_TQ + 2 * KV_HALO, ATT_W), jnp.bfloat16),
            pltpu.VMEM((MIX_TQ, ATT_W), jnp.bfloat16),
            pltpu.VMEM((MIX_TQ + 2 * POOL_HALO, POOL_W), jnp.float32),
            pltpu.VMEM((MIX_TQ, POOL_W), jnp.bfloat16),
            pltpu.VMEM((MIX_TQ, D_MODEL), jnp.float32),
            pltpu.VMEM(wgrp.shape, jnp.bfloat16),
            pltpu.VMEM(wap.shape, jnp.bfloat16),
            pltpu.VMEM(wpp.shape, jnp.bfloat16),
            pltpu.VMEM(wout.shape, jnp.bfloat16),
        ],
        compiler_params=pltpu.CompilerParams(
            dimension_semantics=("arbitrary",), vmem_limit_bytes=VMEM_LIMIT),
        name="mix",
    )(h, mod, u, u, u, u, u, u, u, u, u, u, u, u,
      kvc, bias, wgrp, pscale, wap, wpp, wout, ln1g, ln1b)


ID_E0, ID_E1, ID_R0, ID_R1 = 0, 1, 4, 5
POS_PACKED = 0
POS_BITS = 16
PLAN_EXPERT, PLAN_VALID, PLAN_NACT, PLAN_NEXT = 0, 1, 2, 3
PLAN_W = 2 * LANES
CW_C0, CW_C1 = 0, 1


def _route_kernel(h_ref, mod_ref, wrt_ref, brt_ref, pos_ref, cw_ref, plan_ref, carry_ref, ids_all):
    i = pl.program_id(0)
    tm = ROUTE_TM

    @pl.when(i == 0)
    def _():
        carry_ref[...] = jnp.zeros_like(carry_ref)

    shift = mod_ref[0:1, 3 * D_MODEL:4 * D_MODEL]
    scale = mod_ref[0:1, 4 * D_MODEL:5 * D_MODEL]
    hm = _load_row_tiles(h_ref, tm) * (1.0 + scale) + shift

    hm_hi, hm_lo = _split_bf16(hm)
    w_hi, w_lo = _split_bf16(wrt_ref[...])
    nt = (((1,), (1,)), ((), ()))
    dg = functools.partial(lax.dot_general, dimension_numbers=nt, preferred_element_type=jnp.float32)
    logits = dg(w_hi, hm_hi) + (dg(w_hi, hm_lo) + dg(w_lo, hm_hi)) + brt_ref[:, 0:1]

    sub = lax.broadcasted_iota(jnp.int32, (ROUTE_LOGIT_ROWS, tm), 0)
    big = jnp.int32(1 << 20)
    is_grp = sub < N_GROUPS
    gl = jnp.where(is_grp, logits, -jnp.inf)
    gmax = jnp.max(gl, axis=0, keepdims=True)
    gidx = jnp.min(jnp.where(gl == gmax, sub, big), axis=0, keepdims=True)
    gsum = jnp.sum(jnp.where(is_grp, jnp.exp(logits - gmax), 0.0), axis=0, keepdims=True)
    p_group = 1.0 / gsum

    eid = sub - N_GROUPS
    sel = (eid >= 0) & (eid < N_EXPERTS) & (lax.shift_right_arithmetic(eid, 3) == gidx)
    el = jnp.where(sel, logits, -jnp.inf)
    l0 = jnp.max(el, axis=0, keepdims=True)
    i0 = jnp.min(jnp.where(el == l0, sub, big), axis=0, keepdims=True)
    el2 = jnp.where(sub == i0, -jnp.inf, el)
    l1 = jnp.max(el2, axis=0, keepdims=True)
    i1 = jnp.min(jnp.where(el2 == l1, sub, big), axis=0, keepdims=True)
    t = jnp.exp(l1 - l0)
    w0 = 1.0 / (1.0 + t)
    w1 = t / (1.0 + t)

    half_rows = jnp.where(i >= pl.num_programs(0) // SEQ_PARTS, N_EXPERTS, 0)
    i0 = i0 + half_rows
    i1 = i1 + half_rows
    subs = lax.broadcasted_iota(jnp.int32, (ROUTE_SEG_ROWS, tm), 0)
    onehot = jnp.where((subs == i0) | (subs == i1), 1.0, 0.0)
    rr = lax.broadcasted_iota(jnp.int32, (tm, tm), 0)
    cc = lax.broadcasted_iota(jnp.int32, (tm, tm), 1)
    earlier = jnp.where(rr < cc, 1.0, 0.0).astype(jnp.bfloat16)
    carry = carry_ref[:, 0:1]
    prefix = _bdot(onehot.astype(jnp.bfloat16), earlier) + carry
    r0 = jnp.sum(jnp.where(subs == i0, prefix, 0.0), axis=0, keepdims=True)
    r1 = jnp.sum(jnp.where(subs == i1, prefix, 0.0), axis=0, keepdims=True)
    total = jnp.broadcast_to(carry + jnp.sum(onehot, axis=1, keepdims=True), carry_ref.shape)
    carry_ref[...] = total

    sub8 = lax.broadcasted_iota(jnp.int32, (ROW_TILE, tm), 0)
    ids = jnp.zeros((ROW_TILE, tm), jnp.int32)
    for idx, val in ((ID_E0, i0 - N_GROUPS), (ID_E1, i1 - N_GROUPS),
                     (ID_R0, r0.astype(jnp.int32)), (ID_R1, r1.astype(jnp.int32))):
        ids = jnp.where(sub8 == idx, val, ids)
    ids_all[:, pl.ds(pl.multiple_of(i * tm, tm), tm)] = ids

    cw8 = jnp.where(sub8 == CW_C0, p_group * w0, jnp.where(sub8 == CW_C1, p_group * w1, 0.0))
    cw_ref[...] = jnp.concatenate([cw8, jnp.zeros((LANES - ROW_TILE, tm), jnp.float32)], axis=0).T

    @pl.when(i == pl.num_programs(0) - 1)
    def _():
        subq = lax.broadcasted_iota(jnp.int32, (LANES, LANES), 0)
        laneq = lax.broadcasted_iota(jnp.int32, (LANES, LANES), 1)
        total = jnp.concatenate([carry_ref[...], jnp.zeros((LANES - ROUTE_SEG_ROWS, LANES), jnp.float32)], axis=0)
        cnt = total.astype(jnp.int32)
        tiles = lax.shift_right_logical(cnt + (EXP_TM - 1), EXP_TM.bit_length() - 1).astype(jnp.float32)
        incl = jnp.where(laneq <= subq, 1.0, 0.0).astype(jnp.bfloat16)
        tile_end = _bdot(incl, tiles.astype(jnp.bfloat16))
        tile_start = tile_end - tiles
        seg = (tile_start * EXP_TM).astype(jnp.int32)
        nact = jnp.max(tile_end, axis=0, keepdims=True)

        ids_full = ids_all[...]
        look = jnp.zeros_like(ids_full)
        for e in range(N_SEG):
            look = jnp.where(ids_full == e, seg[N_GROUPS + e, 0], look)
        pos01 = look + pltpu.roll(ids_full, ID_R0 - ID_E0, axis=0)
        assert EXP_TILES * EXP_TM <= 1 << POS_BITS
        pos_ref[...] = pos01 | (pltpu.roll(pos01, ROW_TILE - 1, axis=0) << POS_BITS)

        subp = lax.broadcasted_iota(jnp.int32, (LANES, PLAN_W), 0)
        tile = lax.broadcasted_iota(jnp.int32, (LANES, PLAN_W), 1).astype(jnp.float32)
        is_exp = (subp >= N_GROUPS) & (subp < N_GROUPS + N_SEG)
        end_col = tile_end[:, 0:1]
        nact_s = nact[:, 0:1]
        te = jnp.sum(jnp.where(is_exp & (tile >= end_col), 1.0, 0.0), axis=0, keepdims=True)
        te_last = jnp.sum(jnp.where(is_exp & (nact_s - 1.0 >= end_col), 1.0, 0.0), axis=0, keepdims=True)[:, 0:1]
        tile_row = tile[0:1, :]
        te = jnp.minimum(jnp.where(tile_row < nact_s, te, te_last), N_SEG - 1.0)
        mine = (subp - N_GROUPS).astype(jnp.float32) == te
        cnt_sel = jnp.sum(jnp.where(mine, total[:, 0:1], 0.0), axis=0, keepdims=True)
        start_sel = jnp.sum(jnp.where(mine, tile_start[:, 0:1], 0.0), axis=0, keepdims=True)
        end_sel = jnp.sum(jnp.where(mine, end_col, 0.0), axis=0, keepdims=True)
        valid = jnp.clip(cnt_sel - (tile_row - start_sel) * EXP_TM, 0.0, float(EXP_TM))
        valid = jnp.where(tile_row < nact_s, valid, 0.0)
        subr = lax.broadcasted_iota(jnp.int32, (ROW_TILE, PLAN_W), 0)
        plan = jnp.where(subr == PLAN_EXPERT, te, jnp.where(subr == PLAN_VALID, valid,
                         jnp.where(subr == PLAN_NACT, nact_s, jnp.where(subr == PLAN_NEXT, end_sel, 0.0))))
        plan_ref[...] = plan.astype(jnp.int32)


def _route_call(h, mod, wrt, brt):
    tm = ROUTE_TM
    return pl.pallas_call(
        _route_kernel,
        grid=(SEQ // tm,),
        in_specs=[
            pl.BlockSpec((tm * ROW_TILE, LANES), lambda i: (i, 0)),
            pl.BlockSpec(mod.shape, lambda i: (0, 0)),
            pl.BlockSpec((ROUTE_LOGIT_ROWS, D_MODEL), lambda i: (0, 0)),
            pl.BlockSpec((ROUTE_LOGIT_ROWS, LANES), lambda i: (0, 0)),
        ],
        out_specs=[
            pl.BlockSpec((ROW_TILE, SEQ), lambda i: (0, 0)),
            pl.BlockSpec((tm, LANES), lambda i: (i, 0)),
            pl.BlockSpec((ROW_TILE, PLAN_W), lambda i: (0, 0)),
        ],
        out_shape=[
            jax.ShapeDtypeStruct((ROW_TILE, SEQ), jnp.int32),
            jax.ShapeDtypeStruct((SEQ, LANES), jnp.float32),
            jax.ShapeDtypeStruct((ROW_TILE, PLAN_W), jnp.int32),
        ],
        scratch_shapes=[pltpu.VMEM((ROUTE_SEG_ROWS, LANES), jnp.float32),
                        pltpu.VMEM((ROW_TILE, SEQ), jnp.int32)],
        compiler_params=pltpu.CompilerParams(
            dimension_semantics=("arbitrary",), vmem_limit_bytes=VMEM_LIMIT),
        name="route",
    )(h, mod, wrt, brt)


SRC_UNROLL = 8
EXP_CHUNK = 256
CAST_CHUNKS = 8
PREP_AFTER_DOWN = 1
TILE_ROWS = EXP_TM * ROW_TILE


def _experts_kernel(te_ref, tv_ref, nact_ref, tnext_ref, pos_ref,
                    h_hbm, mod_ref, wg_hbm, wu_hbm, wd_hbm,
                    y_ref,
                    src_ref, ord_ref, hres, xbuf, xmat, wgs, wus, wds,
                    wgb0, wub0, wdb0, wgb1, wub1, wdb1, rsem, wsem):
    i = pl.program_id(0)
    last = pl.num_programs(0) - 1
    nact = nact_ref[0]
    wsets = ((wgb0, wub0, wdb0), (wgb1, wub1, wdb1))
    xcur = lax.rem(i, 2)
    part = lax.shift_right_logical(te_ref[i], N_EXPERTS.bit_length() - 1)

    def weight_copies(segment, st):
        e = segment & (N_EXPERTS - 1)
        return [pltpu.make_async_copy(w_hbm.at[e], stage.at[st], wsem.at[st, n])
                for n, (w_hbm, stage) in enumerate(((wg_hbm, wgs), (wu_hbm, wus), (wd_hbm, wds)))]

    def tile_after(t):
        return tnext_ref[jnp.minimum(t, last)]

    def gather_row(tile, k, s):
        local = (src_ref[tile * EXP_TM + k] - part * PART_TOKENS) & (PART_TOKENS - 1)
        xbuf[s, k * ROW_TILE:(k + 1) * ROW_TILE, :] = hres[pl.ds(pl.multiple_of(local * ROW_TILE, ROW_TILE),
                                                             ROW_TILE), :]

    def gather_items(tile, s):
        return [functools.partial(gather_row, tile, k, s) for k in range(EXP_TM)]

    def prepare_input(xs):
        x = _load_row_tiles(xbuf, EXP_TM, lead=(xs,))
        shift = mod_ref[0:1, 3 * D_MODEL:4 * D_MODEL]
        scale = mod_ref[0:1, 4 * D_MODEL:5 * D_MODEL]
        xmat[xs] = (x * (1.0 + scale) + shift).astype(jnp.bfloat16)

    def compute_chunks(xs, ws):
        wgb, wub, wdb = wsets[ws]

        state = {"act": []}

        def gate(c):
            def run():
                state["a"] = _bdot(xmat[xs], wgb[:, c * EXP_CHUNK:(c + 1) * EXP_CHUNK])
            return run

        def up(c):
            def run():
                a = state["a"]
                u = _bdot(xmat[xs], wub[:, c * EXP_CHUNK:(c + 1) * EXP_CHUNK])
                state["act"].append((a * jax.nn.sigmoid(a) * u).astype(jnp.bfloat16))
            return run

        def down(c):
            def run():
                if c == 0:
                    state["actf"] = jnp.concatenate(state["act"], axis=-1)
                yc = _bdot(state["actf"], wdb[:, c * EXP_CHUNK:(c + 1) * EXP_CHUNK])
                for jj in range(EXP_CHUNK // LANES):
                    j = c * (EXP_CHUNK // LANES) + jj
                    y_ref[pl.ds(j, EXP_TM, stride=ROW_TILE), :] = yc[:, jj * LANES:(jj + 1) * LANES]
            return run

        first = []
        for c in range(D_EXPERT // EXP_CHUNK):
            first += [gate(c), up(c)]
        return first, [down(c) for c in range(D_MODEL // EXP_CHUNK)]

    def resident_copy(p):
        rows = PART_TOKENS * ROW_TILE
        return pltpu.make_async_copy(h_hbm.at[pl.ds(pl.multiple_of(p * rows, rows), rows), :], hres, rsem.at[0])

    @pl.when(i == 0)
    def _():
        resident_copy(part).start()
        ord_ref[0] = 0
        for cp in weight_copies(te_ref[0], 0):
            cp.start()
        second = tile_after(0)

        @pl.when(second < nact)
        def _():
            for cp in weight_copies(te_ref[jnp.minimum(second, last)], 1):
                cp.start()

        def fill_body(tt, c):
            ts = [tt * SRC_UNROLL + u for u in range(SRC_UNROLL)]
            words = [pos_ref[t] for t in ts]
            for t, w in zip(ts, words):
                src_ref[w & ((1 << POS_BITS) - 1)] = t
                src_ref[lax.shift_right_logical(w, POS_BITS)] = t
            return c
        lax.fori_loop(0, SEQ // SRC_UNROLL, fill_body, 0)

        def pad_tile(t, c):
            pad_tok = lax.shift_right_logical(te_ref[t], N_EXPERTS.bit_length() - 1) * PART_TOKENS

            def pad_row(k, c2):
                src_ref[t * EXP_TM + k] = pad_tok
                return c2
            return lax.fori_loop(tv_ref[t], EXP_TM, pad_row, c)
        lax.fori_loop(0, nact, pad_tile, 0)

    active = i < nact
    prev = jnp.maximum(i - 1, 0)
    new_segment = (i == 0) | (te_ref[i] != te_ref[prev])

    @pl.when(active & ((i == 0) | (part != lax.shift_right_logical(te_ref[prev], N_EXPERTS.bit_length() - 1))))
    def _():
        @pl.when(i > 0)
        def _():
            resident_copy(part).start()
        resident_copy(part).wait()
        for item in gather_items(i, xcur):
            item()
        prepare_input(xcur)

    @pl.when(active & new_segment & (i > 0))
    def _():
        ord_ref[0] = ord_ref[0] + 1

    parity = ord_ref[0] & 1
    after1 = tile_after(i)
    after2 = tile_after(after1)
    after3 = tile_after(after2)

    def cast_items(st):
        def block(stage, dst, rows):
            def run():
                dst[rows, :] = stage[st, rows, :].astype(jnp.bfloat16)
            return run
        out = []
        for r in range(CAST_CHUNKS):
            up_rows = slice(r * D_MODEL // CAST_CHUNKS, (r + 1) * D_MODEL // CAST_CHUNKS)
            dn_rows = slice(r * D_EXPERT // CAST_CHUNKS, (r + 1) * D_EXPERT // CAST_CHUNKS)
            out += [block(stage, dst, rows)
                    for stage, dst, rows in zip((wgs, wus, wds), wsets[st], (up_rows, up_rows, dn_rows))]
        return out

    @pl.when(i == 0)
    def _():
        for cp in weight_copies(te_ref[0], 0):
            cp.wait()
        for item in cast_items(0):
            item()

        @pl.when(after2 < nact)
        def _():
            for cp in weight_copies(te_ref[jnp.minimum(after2, last)], 0):
                cp.start()

    def run_tile(par, extra_items):
        first, second = compute_chunks(xcur, par)
        gathers = gather_items(jnp.minimum(i + 1, nact - 1), 1 - xcur)
        per = -(-len(gathers) // len(first))
        per_extra = -(-len(extra_items) // (len(first) + len(second)))
        for n, chunk in enumerate(first + second):
            if n < len(first):
                for item in gathers[n * per:(n + 1) * per]:
                    item()
            for item in extra_items[n * per_extra:(n + 1) * per_extra]:
                item()
            chunk()
            if n == len(first) + PREP_AFTER_DOWN:
                prepare_input(1 - xcur)

    last_of_segment = (i + 1 >= nact) | (te_ref[jnp.minimum(i + 1, last)] != te_ref[i])
    cast_next = active & last_of_segment & (after1 < nact)

    for par in range(2):
        @pl.when(cast_next & (parity == par))
        def _():
            for cp in weight_copies(te_ref[jnp.minimum(after1, last)], 1 - par):
                cp.wait()
            run_tile(par, cast_items(1 - par))

            @pl.when(after3 < nact)
            def _():
                for cp in weight_copies(te_ref[jnp.minimum(after3, last)], 1 - par):
                    cp.start()

        @pl.when(active & jnp.logical_not(cast_next) & (parity == par))
        def _():
            run_tile(par, [])

    @pl.when(jnp.logical_not(active))
    def _():
        y_ref[...] = jnp.zeros_like(y_ref)


def _experts_call(te, tv, nact, tnext, pos, h, mod, wg, wu, wd):
    grid_spec = pltpu.PrefetchScalarGridSpec(
        num_scalar_prefetch=5,
        grid=(EXP_TILES,),
        in_specs=[
            pl.BlockSpec(memory_space=pl.ANY),
            pl.BlockSpec(mod.shape, lambda i, *_: (0, 0)),
            pl.BlockSpec(memory_space=pl.ANY),
            pl.BlockSpec(memory_space=pl.ANY),
            pl.BlockSpec(memory_space=pl.ANY),
        ],
        out_specs=pl.BlockSpec((TILE_ROWS, LANES), lambda i, *_: (i, 0)),
        scratch_shapes=[
            pltpu.SMEM((EXP_TILES * EXP_TM,), jnp.int32),
            pltpu.SMEM((1,), jnp.int32),
            pltpu.VMEM((PART_TOKENS * ROW_TILE, LANES), jnp.float32),
            pltpu.VMEM((2, TILE_ROWS, LANES), jnp.float32),
            pltpu.VMEM((2, EXP_TM, D_MODEL), jnp.bfloat16),
            pltpu.VMEM((2, D_MODEL, D_EXPERT), jnp.float32),
            pltpu.VMEM((2, D_MODEL, D_EXPERT), jnp.float32),
            pltpu.VMEM((2, D_EXPERT, D_MODEL), jnp.float32),
            pltpu.VMEM((D_MODEL, D_EXPERT), jnp.bfloat16),
            pltpu.VMEM((D_MODEL, D_EXPERT), jnp.bfloat16),
            pltpu.VMEM((D_EXPERT, D_MODEL), jnp.bfloat16),
            pltpu.VMEM((D_MODEL, D_EXPERT), jnp.bfloat16),
            pltpu.VMEM((D_MODEL, D_EXPERT), jnp.bfloat16),
            pltpu.VMEM((D_EXPERT, D_MODEL), jnp.bfloat16),
            pltpu.SemaphoreType.DMA((1,)),
            pltpu.SemaphoreType.DMA((2, 3)),
        ],
    )
    return pl.pallas_call(
        _experts_kernel,
        grid_spec=grid_spec,
        out_shape=jax.ShapeDtypeStruct((EXP_TILES * TILE_ROWS, LANES), jnp.float32),
        compiler_params=pltpu.CompilerParams(
            dimension_semantics=("arbitrary",), vmem_limit_bytes=EXPERTS_VMEM_LIMIT),
        name="experts",
    )(te, tv, nact, tnext, pos, h, mod, wg, wu, wd)


def _combine_kernel(pos_ref, h_ref, cw_ref, mod_ref, g_ref, b_ref, ys_hbm, o_ref, ybuf, sem):
    i = pl.program_id(0)
    tm = CMB_TM
    slot = lax.rem(i, 2)

    def start_row(tile, k, s):
        word = pos_ref[tile * tm + k]
        for half, p in ((0, word & ((1 << POS_BITS) - 1)), (1, lax.shift_right_logical(word, POS_BITS))):
            pltpu.make_async_copy(ys_hbm.at[pl.ds(pl.multiple_of(p * ROW_TILE, ROW_TILE), ROW_TILE), :],
                                  ybuf.at[s, half, pl.ds(pl.multiple_of(k * ROW_TILE, ROW_TILE), ROW_TILE), :],
                                  sem.at[s]).start(priority=half)

    @pl.when(i == 0)
    def _():
        def body(kk, c):
            for u in range(SRC_UNROLL):
                start_row(0, kk * SRC_UNROLL + u, 0)
            return c
        lax.fori_loop(0, tm // SRC_UNROLL, body, 0)

    for half in range(2):
        pltpu.make_async_copy(ys_hbm.at[pl.ds(0, tm * ROW_TILE), :], ybuf.at[slot, half], sem.at[slot]).wait()

    g2 = mod_ref[0:1, 5 * D_MODEL:6 * D_MODEL]

    def chunk(c):
        rows = slice(c * CMB_CHUNK, (c + 1) * CMB_CHUNK)
        y0 = _load_row_tiles(ybuf, CMB_CHUNK, lead=(slot, 0), first=c * CMB_CHUNK)
        y1 = _load_row_tiles(ybuf, CMB_CHUNK, lead=(slot, 1), first=c * CMB_CHUNK)
        ffn = cw_ref[rows, CW_C0:CW_C0 + 1] * y0 + cw_ref[rows, CW_C1:CW_C1 + 1] * y1
        h = _load_row_tiles(h_ref, CMB_CHUNK, first=c * CMB_CHUNK)
        o_ref[rows, :] = _layer_norm(DEEPNORM_ALPHA * h + g2 * ffn, g_ref[...], b_ref[...])

    n_chunks = tm // CMB_CHUNK

    @pl.when(i + 1 < pl.num_programs(0))
    def _():
        per = tm // n_chunks
        for c in range(n_chunks):
            for k in range(c * per, (c + 1) * per):
                start_row(i + 1, k, 1 - slot)
            chunk(c)

    @pl.when(i + 1 == pl.num_programs(0))
    def _():
        for c in range(n_chunks):
            chunk(c)


def _combine_call(pos, h, ys, cw, mod, g, b):
    tm = CMB_TM
    grid_spec = pltpu.PrefetchScalarGridSpec(
        num_scalar_prefetch=1,
        grid=(SEQ // tm,),
        in_specs=[
            pl.BlockSpec((tm * ROW_TILE, LANES), lambda i, *_: (i, 0)),
            pl.BlockSpec((tm, LANES), lambda i, *_: (i, 0)),
            pl.BlockSpec(mod.shape, lambda i, *_: (0, 0)),
            pl.BlockSpec((1, D_MODEL), lambda i, *_: (0, 0)),
            pl.BlockSpec((1, D_MODEL), lambda i, *_: (0, 0)),
            pl.BlockSpec(memory_space=pl.ANY),
        ],
        out_specs=pl.BlockSpec((tm, D_MODEL), lambda i, *_: (i, 0)),
        scratch_shapes=[
            pltpu.VMEM((2, 2, tm * ROW_TILE, LANES), jnp.float32),
            pltpu.SemaphoreType.DMA((2,)),
        ],
    )
    return pl.pallas_call(
        _combine_kernel,
        grid_spec=grid_spec,
        out_shape=jax.ShapeDtypeStruct((SEQ, D_MODEL), jnp.float32),
        compiler_params=pltpu.CompilerParams(
            dimension_semantics=("arbitrary",), vmem_limit_bytes=VMEM_LIMIT),
        name="combine",
    )(pos, h, cw, mod, g, b, ys)


def kernel(x, c, ctx, c_ctx, ln_in_g, ln_in_b, w_mod, b_mod, w_in, rpb, w_pool_grp, pool_scale,
           w_attn_proj, w_pool_proj, w_out, ln1_g, ln1_b, w_router_group, b_router_group,
           w_router_expert, b_router_expert, w_expert_gate, w_expert_up, w_expert_down, ln2_g, ln2_b):
    assert x.shape == (1, SEQ, D_MODEL) and ctx.shape == (1, CTX_LEN, D_MODEL)
    assert w_mod.shape[0] == 1, "single-layer trunk"
    f32, bf16 = jnp.float32, jnp.bfloat16
    row = lambda v: v.reshape(1, -1).astype(f32)

    cond = jnp.concatenate([c, c_ctx[None], jnp.zeros((MOD_ROWS - 2, D_MODEL), f32)], axis=0)
    mod = _mod_call(cond, w_mod[0], row(b_mod[0]))

    lng, lnb = row(ln_in_g), row(ln_in_b)
    w_in_b = w_in[0].astype(bf16)
    u, h0 = _proj_call(x[0], mod, lng, lnb, w_in_b, mod_row=0, latent=True, tm=PROJ_TM)
    kvc, = _proj_call(ctx[0], mod, lng, lnb, w_in_b[:, ATT_W:3 * ATT_W], mod_row=1, latent=False, tm=CTX_LEN)

    h1 = _mix_call(h0, mod, u, kvc, _attn_bias_table(rpb[0]),
                   w_pool_grp[0], row(pool_scale[0]), w_attn_proj[0], w_pool_proj[0], w_out[0],
                   row(ln1_g[0]), row(ln1_b[0]))

    n_logit = N_GROUPS + N_EXPERTS
    wrt = jnp.concatenate([w_router_group[0].T, w_router_expert[0].T,
                           jnp.zeros((ROUTE_LOGIT_ROWS - n_logit, D_MODEL), f32)], axis=0)
    brt = jnp.concatenate([b_router_group[0], b_router_expert[0], jnp.zeros((ROUTE_LOGIT_ROWS - n_logit,), f32)])
    brt = jnp.broadcast_to(brt[:, None], (ROUTE_LOGIT_ROWS, LANES))
    pos, cw, plan = _route_call(h1, mod, wrt, brt)

    posw = pos[POS_PACKED]
    y = _experts_call(plan[PLAN_EXPERT, :EXP_TILES], plan[PLAN_VALID, :EXP_TILES], plan[PLAN_NACT, :1],
                      plan[PLAN_NEXT, :EXP_TILES], posw, h1, mod,
                      w_expert_gate[0], w_expert_up[0], w_expert_down[0])
    out = _combine_call(posw, h1, y, cw, mod, row(ln2_g[0]), row(ln2_b[0]))
    return out[None]
```

```python
import functools

import jax
import jax.numpy as jnp
from jax import lax
from jax.experimental import pallas as pl
from jax.experimental.pallas import tpu as pltpu

D_MODEL = 1024
SEQ = 16384
GRID_W = 64
ROWS = SEQ // GRID_W
CTX_LEN = 256
N_HEADS = 8
HEAD_DIM = 64
ATT_W = N_HEADS * HEAD_DIM
WIN_H = 8
WIN_W = 16
POOL_WINDOWS = (2, 4, 8, 16)
POOL_GROUPS = 4
POOL_DIM = 128
POOL_W = POOL_GROUPS * POOL_DIM
GATE_COL = 3 * ATT_W + POOL_W
N_GROUPS = 4
EXPERTS_PER_GROUP = 8
N_EXPERTS = N_GROUPS * EXPERTS_PER_GROUP
D_EXPERT = 512
N_MOD = 6
DEEPNORM_ALPHA = 2.0 ** 0.25
LN_EPS = 1e-5
NEG_INF = -1e30

LANES = 128
ROW_TILE = 8
MOD_ROWS = 8
PROJ_TM = 1024
PROJ_SUB = 256
MIX_ROWS = 8
MIX_TQ = MIX_ROWS * GRID_W
KV_HALO = 4 * GRID_W
POOL_HALO = 16
ROUTE_TM = 1024
ROUTE_LOGIT_ROWS = 40
ROUTE_SEG_ROWS = 72
EXP_TM = 256
SEQ_PARTS = 2
PART_TOKENS = SEQ // SEQ_PARTS
N_SEG = SEQ_PARTS * N_EXPERTS
EXP_TILES = 2 * SEQ // EXP_TM + N_SEG
CMB_TM = 512
CMB_CHUNK = 128
MERGE_ROWS = 256
BIAS_LANES = 1024
ATTN_AHEAD = 8
VMEM_LIMIT = 56 * 1024 * 1024
EXPERTS_VMEM_LIMIT = 60 * 1024 * 1024


def _layer_norm(x, g, b):
    mu = jnp.mean(x, axis=-1, keepdims=True)
    xc = x - mu
    var = jnp.mean(xc * xc, axis=-1, keepdims=True)
    return xc * lax.rsqrt(var + LN_EPS) * g + b


def _bdot(a, b):
    return jnp.dot(a, b, preferred_element_type=jnp.float32)


def _split_bf16(a):
    hi = a.astype(jnp.bfloat16)
    lo = (a - hi.astype(jnp.float32)).astype(jnp.bfloat16)
    return hi, lo


def _dot3(a, b):
    a_hi, a_lo = _split_bf16(a)
    b_hi, b_lo = _split_bf16(b)
    return _bdot(a_hi, b_hi) + (_bdot(a_hi, b_lo) + _bdot(a_lo, b_hi))


def _load_row_tiles(ref, tokens, lead=(), first=0):
    parts = [ref[(*lead, pl.ds(first * ROW_TILE + j, tokens, stride=ROW_TILE), slice(None))]
             for j in range(ROW_TILE)]
    return jnp.concatenate(parts, axis=-1)


def _store_row_tiles(ref, value, lead=()):
    tokens = value.shape[0]
    for j in range(ROW_TILE):
        ref[(*lead, pl.ds(j, tokens, stride=ROW_TILE), slice(None))] = value[:, j * LANES:(j + 1) * LANES]


def _mod_kernel(cond_ref, w_ref, b_ref, o_ref):
    cond = cond_ref[...]
    act = cond * jax.nn.sigmoid(cond)
    o_ref[...] = _dot3(act, w_ref[...]) + b_ref[...]


def _mod_call(cond, w_mod, b_mod):
    tn = 1536
    n = N_MOD * D_MODEL
    return pl.pallas_call(
        _mod_kernel,
        grid=(n // tn,),
        in_specs=[
            pl.BlockSpec((MOD_ROWS, D_MODEL), lambda i: (0, 0)),
            pl.BlockSpec((D_MODEL, tn), lambda i: (0, i)),
            pl.BlockSpec((1, tn), lambda i: (0, i)),
        ],
        out_specs=pl.BlockSpec((MOD_ROWS, tn), lambda i: (0, i)),
        out_shape=jax.ShapeDtypeStruct((MOD_ROWS, n), jnp.float32),
        compiler_params=pltpu.CompilerParams(
            dimension_semantics=("arbitrary",), vmem_limit_bytes=VMEM_LIMIT),
        name="mod",
    )(cond, w_mod, b_mod)


def _proj_kernel(x_ref, mod_ref, g_ref, b_ref, w_ref, o_ref, *h_out, mod_row, latent):
    shift = mod_ref[mod_row:mod_row + 1, 0:D_MODEL]
    scale = mod_ref[mod_row:mod_row + 1, D_MODEL:2 * D_MODEL]
    tm, n = o_ref.shape
    sub = min(tm, PROJ_SUB)

    def prep(r):
        rows = slice(r * sub, (r + 1) * sub)
        h = _layer_norm(x_ref[rows, :], g_ref[...], b_ref[...])
        if latent:
            h_out[0][rows, :] = h
        return (h * (1.0 + scale) + shift).astype(jnp.bfloat16)

    def finish(r, c, res):
        if latent and c == 0:
            lane = lax.broadcasted_iota(jnp.int32, (1, D_MODEL), 1)
            res = res * jnp.where(lane < ATT_W, HEAD_DIM ** -0.5, 1.0)
        if latent and c * D_MODEL >= GATE_COL:
            res = jax.nn.sigmoid(res)
        o_ref[r * sub:(r + 1) * sub, c * D_MODEL:(c + 1) * D_MODEL] = res.astype(jnp.bfloat16)

    hm = {0: prep(0)}
    waiting = None
    for r in range(tm // sub):
        for c in range(n // D_MODEL):
            res = _bdot(hm[r], w_ref[:, c * D_MODEL:(c + 1) * D_MODEL])
            if c == 0 and (r + 1) * sub < tm:
                hm[r + 1] = prep(r + 1)
            if waiting is not None:
                finish(*waiting)
            waiting = (r, c, res)
    finish(*waiting)


def _proj_call(x, mod, g, b, w, *, mod_row, latent, tm):
    rows, n = x.shape[0], w.shape[1]
    out_specs = [pl.BlockSpec((tm, n), lambda i: (i, 0))]
    out_shape = [jax.ShapeDtypeStruct((rows, n), jnp.bfloat16)]
    if latent:
        out_specs.append(pl.BlockSpec((tm, D_MODEL), lambda i: (i, 0)))
        out_shape.append(jax.ShapeDtypeStruct((rows, D_MODEL), jnp.float32))
    return pl.pallas_call(
        functools.partial(_proj_kernel, mod_row=mod_row, latent=latent),
        grid=(rows // tm,),
        in_specs=[
            pl.BlockSpec((tm, D_MODEL), lambda i: (i, 0)),
            pl.BlockSpec(mod.shape, lambda i: (0, 0)),
            pl.BlockSpec((1, D_MODEL), lambda i: (0, 0)),
            pl.BlockSpec((1, D_MODEL), lambda i: (0, 0)),
            pl.BlockSpec((D_MODEL, n), lambda i: (0, 0), pipeline_mode=pl.Buffered(1)),
        ],
        out_specs=out_specs,
        out_shape=out_shape,
        compiler_params=pltpu.CompilerParams(
            dimension_semantics=("arbitrary",), vmem_limit_bytes=VMEM_LIMIT),
        name="proj",
    )(x, mod, g, b, w)


def _attn_bias_table(rpb):
    col = jnp.arange(GRID_W, dtype=jnp.int32)
    col_start = jnp.clip(col - WIN_W // 2, 0, GRID_W - WIN_W)
    col_mask = (col[None, :] >= col_start[:, None]) & (col[None, :] < col_start[:, None] + WIN_W)
    col_off = jnp.clip(col[None, :] - col[:, None], 1 - WIN_W, WIN_W - 1) + (WIN_W - 1)
    onehot = (col_off[None] == jnp.arange(2 * WIN_W - 1, dtype=jnp.int32)[:, None, None]).astype(jnp.float32)
    tab = jnp.einsum("hrc,cqk->hqrk", rpb.astype(jnp.float32), onehot, precision=lax.Precision.HIGHEST)
    tab = jnp.where(col_mask[None, :, None, :], tab, NEG_INF)
    n_rows = 2 * WIN_H - 1
    flat = tab.reshape(N_HEADS // 2, 2 * GRID_W, n_rows * GRID_W)
    even = jnp.pad(flat, ((0, 0), (0, 0), (0, BIAS_LANES - n_rows * GRID_W)))
    odd = jnp.pad(flat[:, :, GRID_W:], ((0, 0), (0, 0), (0, BIAS_LANES - (n_rows - 1) * GRID_W)))

    def window_kernel(even_ref, odd_ref, o_ref):
        start = WIN_H - 1 - pl.program_id(0)
        base = pl.multiple_of(lax.shift_right_logical(start, 1) * LANES, LANES)
        width = WIN_H * GRID_W
        o_ref[0] = jnp.where((start & 1) == 0, even_ref[:, :, pl.ds(base, width)], odd_ref[:, :, pl.ds(base, width)])

    full = pl.BlockSpec(even.shape, lambda v: (0, 0, 0))
    return pl.pallas_call(
        window_kernel,
        grid=(WIN_H,),
        in_specs=[full, full],
        out_specs=pl.BlockSpec((1, N_HEADS // 2, 2 * GRID_W, WIN_H * GRID_W), lambda v: (v, 0, 0, 0)),
        out_shape=jax.ShapeDtypeStruct((WIN_H, N_HEADS // 2, 2 * GRID_W, WIN_H * GRID_W), jnp.float32),
        compiler_params=pltpu.CompilerParams(dimension_semantics=("arbitrary",), vmem_limit_bytes=VMEM_LIMIT),
        name="bias_table",
    )(even, odd)


def _mix_kernel(h_ref, mod_ref,
                q_ref, kp_ref, kc_ref, kn_ref, vp_ref, vc_ref, vn_ref,
                pp_ref, pc_ref, pn_ref, ga_ref, gb_ref,
                kvc_ref, bias_ref, wgrp_f32, pscale_ref, wap_f32, wpp_f32, wout_f32,
                ln1g_ref, ln1b_ref,
                o_ref,
                kbuf, vbuf, yabuf, pbuf, ypbuf, zbuf, wgrp_ref, wap_ref, wpp_ref, wout_ref):
    b = pl.program_id(0)
    nb = pl.num_programs(0)

    @pl.when(b == 0)
    def _():
        for src, dst in ((wgrp_f32, wgrp_ref), (wap_f32, wap_ref), (wpp_f32, wpp_ref), (wout_f32, wout_ref)):
            dst[...] = src[...].astype(jnp.bfloat16)

    kbuf[0:KV_HALO, :] = kp_ref[...]
    kbuf[KV_HALO:KV_HALO + MIX_TQ, :] = kc_ref[...]
    kbuf[KV_HALO + MIX_TQ:, :] = kn_ref[...]
    vbuf[0:KV_HALO, :] = vp_ref[...]
    vbuf[KV_HALO:KV_HALO + MIX_TQ, :] = vc_ref[...]
    vbuf[KV_HALO + MIX_TQ:, :] = vn_ref[...]

    lane = lax.broadcasted_iota(jnp.int32, (GRID_W, LANES), 1)
    first_head = lane < HEAD_DIM

    units = [(j, pair) for j in range(MIX_ROWS) for pair in range(N_HEADS // 2)]
    nt = (((1,), (1,)), ((), ()))

    def window(j):
        r = b * MIX_ROWS + j
        rs = jnp.clip(r - WIN_H // 2, 0, ROWS - WIN_H)
        return pl.multiple_of((rs - b * MIX_ROWS + WIN_H // 2) * GRID_W, GRID_W), r - rs

    def scores(j, pair):
        off, var = window(j)
        cols = slice(pair * LANES, (pair + 1) * LANES)
        q = q_ref[j * GRID_W:(j + 1) * GRID_W, cols]
        zero = jnp.zeros_like(q)
        q2 = jnp.concatenate([jnp.where(first_head, q, zero), jnp.where(first_head, zero, q)], axis=0)
        kw = kbuf[pl.ds(off, WIN_H * GRID_W), cols]
        s_loc = lax.dot_general(q2, kw, nt, preferred_element_type=jnp.float32) + bias_ref[var, pair]
        s_ctx = lax.dot_general(q2, kvc_ref[:, cols], nt, preferred_element_type=jnp.float32)
        return s_loc, s_ctx

    def values(j, pair, s_loc, s_ctx):
        off, _ = window(j)
        cols = slice(pair * LANES, (pair + 1) * LANES)
        vw = vbuf[pl.ds(off, WIN_H * GRID_W), cols]
        vctx = kvc_ref[:, ATT_W + pair * LANES:ATT_W + (pair + 1) * LANES]
        m = jnp.maximum(jnp.max(s_loc, axis=-1, keepdims=True), jnp.max(s_ctx, axis=-1, keepdims=True))
        p_loc = jnp.exp(s_loc - m).astype(jnp.bfloat16)
        p_ctx = jnp.exp(s_ctx - m).astype(jnp.bfloat16)
        ones = lambda rows: jnp.ones((rows, LANES), jnp.bfloat16)
        o2 = (_bdot(p_loc, jnp.concatenate([vw, ones(WIN_H * GRID_W)], axis=1))
              + _bdot(p_ctx, jnp.concatenate([vctx, ones(CTX_LEN)], axis=1)))
        o2 = o2[:, :LANES] / o2[:, LANES:]
        o_pair = jnp.where(first_head, o2[:GRID_W], o2[GRID_W:])
        yabuf[j * GRID_W:(j + 1) * GRID_W, cols] = o_pair.astype(jnp.bfloat16)

    pbuf[0:POOL_HALO, :] = jnp.where(b > 0, pp_ref[...].astype(jnp.float32), 0.0)
    pbuf[POOL_HALO:POOL_HALO + MIX_TQ, :] = pc_ref[...].astype(jnp.float32)
    pbuf[POOL_HALO + MIX_TQ:, :] = jnp.where(b < nb - 1, pn_ref[...].astype(jnp.float32), 0.0)
    edge = lax.broadcasted_iota(jnp.int32, (ROW_TILE, 1), 0)

    def pool_group(g):
        win = POOL_WINDOWS[g]
        lo, hi = win // 2, win - win // 2
        cols = slice(g * POOL_DIM, (g + 1) * POOL_DIM)
        acc = None
        for d in range(-lo, hi):
            term = pbuf[POOL_HALO + d:POOL_HALO + d + MIX_TQ, cols]
            acc = term if acc is None else acc + term
        assert max(lo, hi) <= ROW_TILE
        top = jnp.where(b == 0, 1.0 / (win - jnp.maximum(lo - edge, 0)).astype(jnp.float32), 1.0 / win)
        bot = jnp.where(b == nb - 1,
                        1.0 / (win - jnp.maximum(edge + hi - ROW_TILE, 0)).astype(jnp.float32), 1.0 / win)
        inv = jnp.concatenate([top, jnp.full((MIX_TQ - 2 * ROW_TILE, 1), 1.0 / win, jnp.float32), bot], axis=0)
        pooled = acc * inv - pbuf[POOL_HALO:POOL_HALO + MIX_TQ, cols]
        yp = _bdot(pooled.astype(jnp.bfloat16), wgrp_ref[g]) * pscale_ref[:, cols]
        ypbuf[:, cols] = yp.astype(jnp.bfloat16)

    def pooled_branch():
        zbuf[...] = gb_ref[...].astype(jnp.float32) * _bdot(ypbuf[...], wpp_ref[...])

    extra = {}
    for g in range(POOL_GROUPS):
        extra[(g + 1) * len(units) // (POOL_GROUPS + 2)] = functools.partial(pool_group, g)
    extra[(POOL_GROUPS + 1) * len(units) // (POOL_GROUPS + 2)] = pooled_branch
    pending = [scores(*u) for u in units[:ATTN_AHEAD]]
    for n, u in enumerate(units):
        if n + ATTN_AHEAD < len(units):
            pending.append(scores(*units[n + ATTN_AHEAD]))
        values(*u, *pending.pop(0))
        if n in extra:
            extra[n]()

    g1 = mod_ref[0:1, 2 * D_MODEL:3 * D_MODEL]
    n_chunks = MIX_TQ // MERGE_ROWS
    rows = lambda c: slice(c * MERGE_ROWS, (c + 1) * MERGE_ROWS)
    z, y = {}, {}

    def stage_a(c):
        ya = _bdot(yabuf[rows(c), :], wap_ref[...])
        z[c] = (ga_ref[rows(c), :].astype(jnp.float32) * ya + zbuf[rows(c), :]).astype(jnp.bfloat16)

    def stage_b(c):
        y[c] = _bdot(z.pop(c), wout_ref[...])

    def stage_c(c):
        hn = _layer_norm(DEEPNORM_ALPHA * h_ref[rows(c), :] + g1 * y.pop(c), ln1g_ref[...], ln1b_ref[...])
        for j in range(ROW_TILE):
            o_ref[pl.ds(c * MERGE_ROWS * ROW_TILE + j, MERGE_ROWS, stride=ROW_TILE), :] = hn[:, j * LANES:(j + 1) * LANES]

    for t in range(n_chunks + 2):
        if t < n_chunks:
            stage_a(t)
        if 0 <= t - 1 < n_chunks:
            stage_b(t - 1)
        if 0 <= t - 2 < n_chunks:
            stage_c(t - 2)


def _mix_call(h, mod, u, kvc, bias, wgrp, pscale, wap, wpp, wout, ln1g, ln1b):
    nb = SEQ // MIX_TQ
    halo_per_blk = MIX_TQ // KV_HALO
    n_halo = SEQ // KV_HALO
    ph_per_blk = MIX_TQ // POOL_HALO
    n_ph = SEQ // POOL_HALO

    def const(shape):
        return pl.BlockSpec(shape, lambda i: (0,) * len(shape), pipeline_mode=pl.Buffered(1))

    def prev_halo(c):
        return pl.BlockSpec((KV_HALO, ATT_W), lambda i: (jnp.maximum(i * halo_per_blk - 1, 0), c))

    def next_halo(c):
        return pl.BlockSpec((KV_HALO, ATT_W), lambda i: (jnp.minimum((i + 1) * halo_per_blk, n_halo - 1), c))

    def cur(c):
        return pl.BlockSpec((MIX_TQ, ATT_W), lambda i: (i, c))

    in_specs = [
        pl.BlockSpec((MIX_TQ, D_MODEL), lambda i: (i, 0)),
        const(mod.shape),
        cur(0),
        prev_halo(1), cur(1), next_halo(1),
        prev_halo(2), cur(2), next_halo(2),
        pl.BlockSpec((POOL_HALO, POOL_W), lambda i: (jnp.maximum(i * ph_per_blk - 1, 0), 3)),
        cur(3),
        pl.BlockSpec((POOL_HALO, POOL_W), lambda i: (jnp.minimum((i + 1) * ph_per_blk, n_ph - 1), 3)),
        pl.BlockSpec((MIX_TQ, D_MODEL), lambda i: (i, 2)),
        pl.BlockSpec((MIX_TQ, D_MODEL), lambda i: (i, 3)),
        const(kvc.shape), const(bias.shape), const(wgrp.shape), const(pscale.shape),
        const(wap.shape), const(wpp.shape), const(wout.shape),
        const((1, D_MODEL)), const((1, D_MODEL)),
    ]
    return pl.pallas_call(
        _mix_kernel,
        grid=(nb,),
        in_specs=in_specs,
        out_specs=pl.BlockSpec((MIX_TQ * ROW_TILE, LANES), lambda i: (i, 0)),
        out_shape=jax.ShapeDtypeStruct((SEQ * ROW_TILE, LANES), jnp.float32),
        scratch_shapes=[
            pltpu.VMEM((MIX_TQ + 2 * KV_HALO, ATT_W), jnp.bfloat16),
            pltpu.VMEM((MIX_TQ + 2 * KV_HALO, ATT_W), jnp.bfloat16),
            pltpu.VMEM((MIX_TQ, ATT_W), jnp.bfloat16),
            pltpu.VMEM((MIX_TQ + 2 * POOL_HALO, POOL_W), jnp.float32),
            pltpu.VMEM((MIX_TQ, POOL_W), jnp.bfloat16),
            pltpu.VMEM((MIX_TQ, D_MODEL), jnp.float32),
            pltpu.VMEM(wgrp.shape, jnp.bfloat16),
            pltpu.VMEM(wap.shape, jnp.bfloat16),
            pltpu.VMEM(wpp.shape, jnp.bfloat16),
            pltpu.VMEM(wout.shape, jnp.bfloat16),
        ],
        compiler_params=pltpu.CompilerParams(
            dimension_semantics=("arbitrary",), vmem_limit_bytes=VMEM_LIMIT),
        name="mix",
    )(h, mod, u, u, u, u, u, u, u, u, u, u, u, u,
      kvc, bias, wgrp, pscale, wap, wpp, wout, ln1g, ln1b)


ID_E0, ID_E1, ID_R0, ID_R1 = 0, 1, 4, 5
POS_PACKED = 0
POS_BITS = 16
PLAN_EXPERT, PLAN_VALID, PLAN_NACT, PLAN_NEXT = 0, 1, 2, 3
PLAN_W = 2 * LANES
CW_C0, CW_C1 = 0, 1


def _route_kernel(h_ref, mod_ref, wrt_ref, brt_ref, pos_ref, cw_ref, plan_ref, carry_ref, ids_all):
    i = pl.program_id(0)
    tm = ROUTE_TM

    @pl.when(i == 0)
    def _():
        carry_ref[...] = jnp.zeros_like(carry_ref)

    shift = mod_ref[0:1, 3 * D_MODEL:4 * D_MODEL]
    scale = mod_ref[0:1, 4 * D_MODEL:5 * D_MODEL]
    hm = _load_row_tiles(h_ref, tm) * (1.0 + scale) + shift

    hm_hi, hm_lo = _split_bf16(hm)
    w_hi, w_lo = _split_bf16(wrt_ref[...])
    nt = (((1,), (1,)), ((), ()))
    dg = functools.partial(lax.dot_general, dimension_numbers=nt, preferred_element_type=jnp.float32)
    logits = dg(w_hi, hm_hi) + (dg(w_hi, hm_lo) + dg(w_lo, hm_hi)) + brt_ref[:, 0:1]

    sub = lax.broadcasted_iota(jnp.int32, (ROUTE_LOGIT_ROWS, tm), 0)
    big = jnp.int32(1 << 20)
    is_grp = sub < N_GROUPS
    gl = jnp.where(is_grp, logits, -jnp.inf)
    gmax = jnp.max(gl, axis=0, keepdims=True)
    gidx = jnp.min(jnp.where(gl == gmax, sub, big), axis=0, keepdims=True)
    gsum = jnp.sum(jnp.where(is_grp, jnp.exp(logits - gmax), 0.0), axis=0, keepdims=True)
    p_group = 1.0 / gsum

    eid = sub - N_GROUPS
    sel = (eid >= 0) & (eid < N_EXPERTS) & (lax.shift_right_arithmetic(eid, 3) == gidx)
    el = jnp.where(sel, logits, -jnp.inf)
    l0 = jnp.max(el, axis=0, keepdims=True)
    i0 = jnp.min(jnp.where(el == l0, sub, big), axis=0, keepdims=True)
    el2 = jnp.where(sub == i0, -jnp.inf, el)
    l1 = jnp.max(el2, axis=0, keepdims=True)
    i1 = jnp.min(jnp.where(el2 == l1, sub, big), axis=0, keepdims=True)
    t = jnp.exp(l1 - l0)
    w0 = 1.0 / (1.0 + t)
    w1 = t / (1.0 + t)

    half_rows = jnp.where(i >= pl.num_programs(0) // SEQ_PARTS, N_EXPERTS, 0)
    i0 = i0 + half_rows
    i1 = i1 + half_rows
    subs = lax.broadcasted_iota(jnp.int32, (ROUTE_SEG_ROWS, tm), 0)
    onehot = jnp.where((subs == i0) | (subs == i1), 1.0, 0.0)
    rr = lax.broadcasted_iota(jnp.int32, (tm, tm), 0)
    cc = lax.broadcasted_iota(jnp.int32, (tm, tm), 1)
    earlier = jnp.where(rr < cc, 1.0, 0.0).astype(jnp.bfloat16)
    carry = carry_ref[:, 0:1]
    prefix = _bdot(onehot.astype(jnp.bfloat16), earlier) + carry
    r0 = jnp.sum(jnp.where(subs == i0, prefix, 0.0), axis=0, keepdims=True)
    r1 = jnp.sum(jnp.where(subs == i1, prefix, 0.0), axis=0, keepdims=True)
    total = jnp.broadcast_to(carry + jnp.sum(onehot, axis=1, keepdims=True), carry_ref.shape)
    carry_ref[...] = total

    sub8 = lax.broadcasted_iota(jnp.int32, (ROW_TILE, tm), 0)
    ids = jnp.zeros((ROW_TILE, tm), jnp.int32)
    for idx, val in ((ID_E0, i0 - N_GROUPS), (ID_E1, i1 - N_GROUPS),
                     (ID_R0, r0.astype(jnp.int32)), (ID_R1, r1.astype(jnp.int32))):
        ids = jnp.where(sub8 == idx, val, ids)
    ids_all[:, pl.ds(pl.multiple_of(i * tm, tm), tm)] = ids

    cw8 = jnp.where(sub8 == CW_C0, p_group * w0, jnp.where(sub8 == CW_C1, p_group * w1, 0.0))
    cw_ref[...] = jnp.concatenate([cw8, jnp.zeros((LANES - ROW_TILE, tm), jnp.float32)], axis=0).T

    @pl.when(i == pl.num_programs(0) - 1)
    def _():
        subq = lax.broadcasted_iota(jnp.int32, (LANES, LANES), 0)
        laneq = lax.broadcasted_iota(jnp.int32, (LANES, LANES), 1)
        total = jnp.concatenate([carry_ref[...], jnp.zeros((LANES - ROUTE_SEG_ROWS, LANES), jnp.float32)], axis=0)
        cnt = total.astype(jnp.int32)
        tiles = lax.shift_right_logical(cnt + (EXP_TM - 1), EXP_TM.bit_length() - 1).astype(jnp.float32)
        incl = jnp.where(laneq <= subq, 1.0, 0.0).astype(jnp.bfloat16)
        tile_end = _bdot(incl, tiles.astype(jnp.bfloat16))
        tile_start = tile_end - tiles
        seg = (tile_start * EXP_TM).astype(jnp.int32)
        nact = jnp.max(tile_end, axis=0, keepdims=True)

        ids_full = ids_all[...]
        look = jnp.zeros_like(ids_full)
        for e in range(N_SEG):
            look = jnp.where(ids_full == e, seg[N_GROUPS + e, 0], look)
        pos01 = look + pltpu.roll(ids_full, ID_R0 - ID_E0, axis=0)
        assert EXP_TILES * EXP_TM <= 1 << POS_BITS
        pos_ref[...] = pos01 | (pltpu.roll(pos01, ROW_TILE - 1, axis=0) << POS_BITS)

        subp = lax.broadcasted_iota(jnp.int32, (LANES, PLAN_W), 0)
        tile = lax.broadcasted_iota(jnp.int32, (LANES, PLAN_W), 1).astype(jnp.float32)
        is_exp = (subp >= N_GROUPS) & (subp < N_GROUPS + N_SEG)
        end_col = tile_end[:, 0:1]
        nact_s = nact[:, 0:1]
        te = jnp.sum(jnp.where(is_exp & (tile >= end_col), 1.0, 0.0), axis=0, keepdims=True)
        te_last = jnp.sum(jnp.where(is_exp & (nact_s - 1.0 >= end_col), 1.0, 0.0), axis=0, keepdims=True)[:, 0:1]
        tile_row = tile[0:1, :]
        te = jnp.minimum(jnp.where(tile_row < nact_s, te, te_last), N_SEG - 1.0)
        mine = (subp - N_GROUPS).astype(jnp.float32) == te
        cnt_sel = jnp.sum(jnp.where(mine, total[:, 0:1], 0.0), axis=0, keepdims=True)
        start_sel = jnp.sum(jnp.where(mine, tile_start[:, 0:1], 0.0), axis=0, keepdims=True)
        end_sel = jnp.sum(jnp.where(mine, end_col, 0.0), axis=0, keepdims=True)
        valid = jnp.clip(cnt_sel - (tile_row - start_sel) * EXP_TM, 0.0, float(EXP_TM))
        valid = jnp.where(tile_row < nact_s, valid, 0.0)
        subr = lax.broadcasted_iota(jnp.int32, (ROW_TILE, PLAN_W), 0)
        plan = jnp.where(subr == PLAN_EXPERT, te, jnp.where(subr == PLAN_VALID, valid,
                         jnp.where(subr == PLAN_NACT, nact_s, jnp.where(subr == PLAN_NEXT, end_sel, 0.0))))
        plan_ref[...] = plan.astype(jnp.int32)


def _route_call(h, mod, wrt, brt):
    tm = ROUTE_TM
    return pl.pallas_call(
        _route_kernel,
        grid=(SEQ // tm,),
        in_specs=[
            pl.BlockSpec((tm * ROW_TILE, LANES), lambda i: (i, 0)),
            pl.BlockSpec(mod.shape, lambda i: (0, 0)),
            pl.BlockSpec((ROUTE_LOGIT_ROWS, D_MODEL), lambda i: (0, 0)),
            pl.BlockSpec((ROUTE_LOGIT_ROWS, LANES), lambda i: (0, 0)),
        ],
        out_specs=[
            pl.BlockSpec((ROW_TILE, SEQ), lambda i: (0, 0)),
            pl.BlockSpec((tm, LANES), lambda i: (i, 0)),
            pl.BlockSpec((ROW_TILE, PLAN_W), lambda i: (0, 0)),
        ],
        out_shape=[
            jax.ShapeDtypeStruct((ROW_TILE, SEQ), jnp.int32),
            jax.ShapeDtypeStruct((SEQ, LANES), jnp.float32),
            jax.ShapeDtypeStruct((ROW_TILE, PLAN_W), jnp.int32),
        ],
        scratch_shapes=[pltpu.VMEM((ROUTE_SEG_ROWS, LANES), jnp.float32),
                        pltpu.VMEM((ROW_TILE, SEQ), jnp.int32)],
        compiler_params=pltpu.CompilerParams(
            dimension_semantics=("arbitrary",), vmem_limit_bytes=VMEM_LIMIT),
        name="route",
    )(h, mod, wrt, brt)


SRC_UNROLL = 8
EXP_CHUNK = 256
CAST_CHUNKS = 8
PREP_AFTER_DOWN = 1
TILE_ROWS = EXP_TM * ROW_TILE


def _experts_kernel(te_ref, tv_ref, nact_ref, tnext_ref, pos_ref,
                    h_hbm, mod_ref, wg_hbm, wu_hbm, wd_hbm,
                    y_ref,
                    src_ref, ord_ref, hres, xbuf, xmat, wgs, wus, wds,
                    wgb0, wub0, wdb0, wgb1, wub1, wdb1, rsem, wsem):
    i = pl.program_id(0)
    last = pl.num_programs(0) - 1
    nact = nact_ref[0]
    wsets = ((wgb0, wub0, wdb0), (wgb1, wub1, wdb1))
    xcur = lax.rem(i, 2)
    part = lax.shift_right_logical(te_ref[i], N_EXPERTS.bit_length() - 1)

    def weight_copies(segment, st):
        e = segment & (N_EXPERTS - 1)
        return [pltpu.make_async_copy(w_hbm.at[e], stage.at[st], wsem.at[st, n])
                for n, (w_hbm, stage) in enumerate(((wg_hbm, wgs), (wu_hbm, wus), (wd_hbm, wds)))]

    def tile_after(t):
        return tnext_ref[jnp.minimum(t, last)]

    def gather_row(tile, k, s):
        local = (src_ref[tile * EXP_TM + k] - part * PART_TOKENS) & (PART_TOKENS - 1)
        xbuf[s, k * ROW_TILE:(k + 1) * ROW_TILE, :] = hres[pl.ds(pl.multiple_of(local * ROW_TILE, ROW_TILE),
                                                             ROW_TILE), :]

    def gather_items(tile, s):
        return [functools.partial(gather_row, tile, k, s) for k in range(EXP_TM)]

    def prepare_input(xs):
        x = _load_row_tiles(xbuf, EXP_TM, lead=(xs,))
        shift = mod_ref[0:1, 3 * D_MODEL:4 * D_MODEL]
        scale = mod_ref[0:1, 4 * D_MODEL:5 * D_MODEL]
        xmat[xs] = (x * (1.0 + scale) + shift).astype(jnp.bfloat16)

    def compute_chunks(xs, ws):
        wgb, wub, wdb = wsets[ws]

        state = {"act": []}

        def gate(c):
            def run():
                state["a"] = _bdot(xmat[xs], wgb[:, c * EXP_CHUNK:(c + 1) * EXP_CHUNK])
            return run

        def up(c):
            def run():
                a = state["a"]
                u = _bdot(xmat[xs], wub[:, c * EXP_CHUNK:(c + 1) * EXP_CHUNK])
                state["act"].append((a * jax.nn.sigmoid(a) * u).astype(jnp.bfloat16))
            return run

        def down(c):
            def run():
                if c == 0:
                    state["actf"] = jnp.concatenate(state["act"], axis=-1)
                yc = _bdot(state["actf"], wdb[:, c * EXP_CHUNK:(c + 1) * EXP_CHUNK])
                for jj in range(EXP_CHUNK // LANES):
                    j = c * (EXP_CHUNK // LANES) + jj
                    y_ref[pl.ds(j, EXP_TM, stride=ROW_TILE), :] = yc[:, jj * LANES:(jj + 1) * LANES]
            return run

        first = []
        for c in range(D_EXPERT // EXP_CHUNK):
            first += [gate(c), up(c)]
        return first, [down(c) for c in range(D_MODEL // EXP_CHUNK)]

    def resident_copy(p):
        rows = PART_TOKENS * ROW_TILE
        return pltpu.make_async_copy(h_hbm.at[pl.ds(pl.multiple_of(p * rows, rows), rows), :], hres, rsem.at[0])

    @pl.when(i == 0)
    def _():
        resident_copy(part).start()
        ord_ref[0] = 0
        for cp in weight_copies(te_ref[0], 0):
            cp.start()
        second = tile_after(0)

        @pl.when(second < nact)
        def _():
            for cp in weight_copies(te_ref[jnp.minimum(second, last)], 1):
                cp.start()

        def fill_body(tt, c):
            ts = [tt * SRC_UNROLL + u for u in range(SRC_UNROLL)]
            words = [pos_ref[t] for t in ts]
            for t, w in zip(ts, words):
                src_ref[w & ((1 << POS_BITS) - 1)] = t
                src_ref[lax.shift_right_logical(w, POS_BITS)] = t
            return c
        lax.fori_loop(0, SEQ // SRC_UNROLL, fill_body, 0)

        def pad_tile(t, c):
            pad_tok = lax.shift_right_logical(te_ref[t], N_EXPERTS.bit_length() - 1) * PART_TOKENS

            def pad_row(k, c2):
                src_ref[t * EXP_TM + k] = pad_tok
                return c2
            return lax.fori_loop(tv_ref[t], EXP_TM, pad_row, c)
        lax.fori_loop(0, nact, pad_tile, 0)

    active = i < nact
    prev = jnp.maximum(i - 1, 0)
    new_segment = (i == 0) | (te_ref[i] != te_ref[prev])

    @pl.when(active & ((i == 0) | (part != lax.shift_right_logical(te_ref[prev], N_EXPERTS.bit_length() - 1))))
    def _():
        @pl.when(i > 0)
        def _():
            resident_copy(part).start()
        resident_copy(part).wait()
        for item in gather_items(i, xcur):
            item()
        prepare_input(xcur)

    @pl.when(active & new_segment & (i > 0))
    def _():
        ord_ref[0] = ord_ref[0] + 1

    parity = ord_ref[0] & 1
    after1 = tile_after(i)
    after2 = tile_after(after1)
    after3 = tile_after(after2)

    def cast_items(st):
        def block(stage, dst, rows):
            def run():
                dst[rows, :] = stage[st, rows, :].astype(jnp.bfloat16)
            return run
        out = []
        for r in range(CAST_CHUNKS):
            up_rows = slice(r * D_MODEL // CAST_CHUNKS, (r + 1) * D_MODEL // CAST_CHUNKS)
            dn_rows = slice(r * D_EXPERT // CAST_CHUNKS, (r + 1) * D_EXPERT // CAST_CHUNKS)
            out += [block(stage, dst, rows)
                    for stage, dst, rows in zip((wgs, wus, wds), wsets[st], (up_rows, up_rows, dn_rows))]
        return out

    @pl.when(i == 0)
    def _():
        for cp in weight_copies(te_ref[0], 0):
            cp.wait()
        for item in cast_items(0):
            item()

        @pl.when(after2 < nact)
        def _():
            for cp in weight_copies(te_ref[jnp.minimum(after2, last)], 0):
                cp.start()

    def run_tile(par, extra_items):
        first, second = compute_chunks(xcur, par)
        gathers = gather_items(jnp.minimum(i + 1, nact - 1), 1 - xcur)
        per = -(-len(gathers) // len(first))
        per_extra = -(-len(extra_items) // (len(first) + len(second)))
        for n, chunk in enumerate(first + second):
            if n < len(first):
                for item in gathers[n * per:(n + 1) * per]:
                    item()
            for item in extra_items[n * per_extra:(n + 1) * per_extra]:
                item()
            chunk()
            if n == len(first) + PREP_AFTER_DOWN:
                prepare_input(1 - xcur)

    last_of_segment = (i + 1 >= nact) | (te_ref[jnp.minimum(i + 1, last)] != te_ref[i])
    cast_next = active & last_of_segment & (after1 < nact)

    for par in range(2):
        @pl.when(cast_next & (parity == par))
        def _():
            for cp in weight_copies(te_ref[jnp.minimum(after1, last)], 1 - par):
                cp.wait()
            run_tile(par, cast_items(1 - par))

            @pl.when(after3 < nact)
            def _():
                for cp in weight_copies(te_ref[jnp.minimum(after3, last)], 1 - par):
                    cp.start()

        @pl.when(active & jnp.logical_not(cast_next) & (parity == par))
        def _():
            run_tile(par, [])

    @pl.when(jnp.logical_not(active))
    def _():
        y_ref[...] = jnp.zeros_like(y_ref)


def _experts_call(te, tv, nact, tnext, pos, h, mod, wg, wu, wd):
    grid_spec = pltpu.PrefetchScalarGridSpec(
        num_scalar_prefetch=5,
        grid=(EXP_TILES,),
        in_specs=[
            pl.BlockSpec(memory_space=pl.ANY),
            pl.BlockSpec(mod.shape, lambda i, *_: (0, 0)),
            pl.BlockSpec(memory_space=pl.ANY),
            pl.BlockSpec(memory_space=pl.ANY),
            pl.BlockSpec(memory_space=pl.ANY),
        ],
        out_specs=pl.BlockSpec((TILE_ROWS, LANES), lambda i, *_: (i, 0)),
        scratch_shapes=[
            pltpu.SMEM((EXP_TILES * EXP_TM,), jnp.int32),
            pltpu.SMEM((1,), jnp.int32),
            pltpu.VMEM((PART_TOKENS * ROW_TILE, LANES), jnp.float32),
            pltpu.VMEM((2, TILE_ROWS, LANES), jnp.float32),
            pltpu.VMEM((2, EXP_TM, D_MODEL), jnp.bfloat16),
            pltpu.VMEM((2, D_MODEL, D_EXPERT), jnp.float32),
            pltpu.VMEM((2, D_MODEL, D_EXPERT), jnp.float32),
            pltpu.VMEM((2, D_EXPERT, D_MODEL), jnp.float32),
            pltpu.VMEM((D_MODEL, D_EXPERT), jnp.bfloat16),
            pltpu.VMEM((D_MODEL, D_EXPERT), jnp.bfloat16),
            pltpu.VMEM((D_EXPERT, D_MODEL), jnp.bfloat16),
            pltpu.VMEM((D_MODEL, D_EXPERT), jnp.bfloat16),
            pltpu.VMEM((D_MODEL, D_EXPERT), jnp.bfloat16),
            pltpu.VMEM((D_EXPERT, D_MODEL), jnp.bfloat16),
            pltpu.SemaphoreType.DMA((1,)),
            pltpu.SemaphoreType.DMA((2, 3)),
        ],
    )
    return pl.pallas_call(
        _experts_kernel,
        grid_spec=grid_spec,
        out_shape=jax.ShapeDtypeStruct((EXP_TILES * TILE_ROWS, LANES), jnp.float32),
        compiler_params=pltpu.CompilerParams(
            dimension_semantics=("arbitrary",), vmem_limit_bytes=EXPERTS_VMEM_LIMIT),
        name="experts",
    )(te, tv, nact, tnext, pos, h, mod, wg, wu, wd)


def _combine_kernel(pos_ref, h_ref, cw_ref, mod_ref, g_ref, b_ref, ys_hbm, o_ref, ybuf, sem):
    i = pl.program_id(0)
    tm = CMB_TM
    slot = lax.rem(i, 2)

    def start_row(tile, k, s):
        word = pos_ref[tile * tm + k]
        for half, p in ((0, word & ((1 << POS_BITS) - 1)), (1, lax.shift_right_logical(word, POS_BITS))):
            pltpu.make_async_copy(ys_hbm.at[pl.ds(pl.multiple_of(p * ROW_TILE, ROW_TILE), ROW_TILE), :],
                                  ybuf.at[s, half, pl.ds(pl.multiple_of(k * ROW_TILE, ROW_TILE), ROW_TILE), :],
                                  sem.at[s]).start(priority=half)

    @pl.when(i == 0)
    def _():
        def body(kk, c):
            for u in range(SRC_UNROLL):
                start_row(0, kk * SRC_UNROLL + u, 0)
            return c
        lax.fori_loop(0, tm // SRC_UNROLL, body, 0)

    for half in range(2):
        pltpu.make_async_copy(ys_hbm.at[pl.ds(0, tm * ROW_TILE), :], ybuf.at[slot, half], sem.at[slot]).wait()

    g2 = mod_ref[0:1, 5 * D_MODEL:6 * D_MODEL]

    def chunk(c):
        rows = slice(c * CMB_CHUNK, (c + 1) * CMB_CHUNK)
        y0 = _load_row_tiles(ybuf, CMB_CHUNK, lead=(slot, 0), first=c * CMB_CHUNK)
        y1 = _load_row_tiles(ybuf, CMB_CHUNK, lead=(slot, 1), first=c * CMB_CHUNK)
        ffn = cw_ref[rows, CW_C0:CW_C0 + 1] * y0 + cw_ref[rows, CW_C1:CW_C1 + 1] * y1
        h = _load_row_tiles(h_ref, CMB_CHUNK, first=c * CMB_CHUNK)
        o_ref[rows, :] = _layer_norm(DEEPNORM_ALPHA * h + g2 * ffn, g_ref[...], b_ref[...])

    n_chunks = tm // CMB_CHUNK

    @pl.when(i + 1 < pl.num_programs(0))
    def _():
        per = tm // n_chunks
        for c in range(n_chunks):
            chunk(c)
            for k in range(c * per, (c + 1) * per):
                start_row(i + 1, k, 1 - slot)

    @pl.when(i + 1 == pl.num_programs(0))
    def _():
        for c in range(n_chunks):
            chunk(c)


def _combine_call(pos, h, ys, cw, mod, g, b):
    tm = CMB_TM
    grid_spec = pltpu.PrefetchScalarGridSpec(
        num_scalar_prefetch=1,
        grid=(SEQ // tm,),
        in_specs=[
            pl.BlockSpec((tm * ROW_TILE, LANES), lambda i, *_: (i, 0)),
            pl.BlockSpec((tm, LANES), lambda i, *_: (i, 0)),
            pl.BlockSpec(mod.shape, lambda i, *_: (0, 0)),
            pl.BlockSpec((1, D_MODEL), lambda i, *_: (0, 0)),
            pl.BlockSpec((1, D_MODEL), lambda i, *_: (0, 0)),
            pl.BlockSpec(memory_space=pl.ANY),
        ],
        out_specs=pl.BlockSpec((tm, D_MODEL), lambda i, *_: (i, 0)),
        scratch_shapes=[
            pltpu.VMEM((2, 2, tm * ROW_TILE, LANES), jnp.float32),
            pltpu.SemaphoreType.DMA((2,)),
        ],
    )
    return pl.pallas_call(
        _combine_kernel,
        grid_spec=grid_spec,
        out_shape=jax.ShapeDtypeStruct((SEQ, D_MODEL), jnp.float32),
        compiler_params=pltpu.CompilerParams(
            dimension_semantics=("arbitrary",), vmem_limit_bytes=VMEM_LIMIT),
        name="combine",
    )(pos, h, cw, mod, g, b, ys)


def kernel(x, c, ctx, c_ctx, ln_in_g, ln_in_b, w_mod, b_mod, w_in, rpb, w_pool_grp, pool_scale,
           w_attn_proj, w_pool_proj, w_out, ln1_g, ln1_b, w_router_group, b_router_group,
           w_router_expert, b_router_expert, w_expert_gate, w_expert_up, w_expert_down, ln2_g, ln2_b):
    assert x.shape == (1, SEQ, D_MODEL) and ctx.shape == (1, CTX_LEN, D_MODEL)
    assert w_mod.shape[0] == 1, "single-layer trunk"
    f32, bf16 = jnp.float32, jnp.bfloat16
    row = lambda v: v.reshape(1, -1).astype(f32)

    cond = jnp.concatenate([c, c_ctx[None], jnp.zeros((MOD_ROWS - 2, D_MODEL), f32)], axis=0)
    mod = _mod_call(cond, w_mod[0], row(b_mod[0]))

    lng, lnb = row(ln_in_g), row(ln_in_b)
    w_in_b = w_in[0].astype(bf16)
    u, h0 = _proj_call(x[0], mod, lng, lnb, w_in_b, mod_row=0, latent=True, tm=PROJ_TM)
    kvc, = _proj_call(ctx[0], mod, lng, lnb, w_in_b[:, ATT_W:3 * ATT_W], mod_row=1, latent=False, tm=CTX_LEN)

    h1 = _mix_call(h0, mod, u, kvc, _attn_bias_table(rpb[0]),
                   w_pool_grp[0], row(pool_scale[0]), w_attn_proj[0], w_pool_proj[0], w_out[0],
                   row(ln1_g[0]), row(ln1_b[0]))

    n_logit = N_GROUPS + N_EXPERTS
    wrt = jnp.concatenate([w_router_group[0].T, w_router_expert[0].T,
                           jnp.zeros((ROUTE_LOGIT_ROWS - n_logit, D_MODEL), f32)], axis=0)
    brt = jnp.concatenate([b_router_group[0], b_router_expert[0], jnp.zeros((ROUTE_LOGIT_ROWS - n_logit,), f32)])
    brt = jnp.broadcast_to(brt[:, None], (ROUTE_LOGIT_ROWS, LANES))
    pos, cw, plan = _route_call(h1, mod, wrt, brt)

    posw = pos[POS_PACKED]
    y = _experts_call(plan[PLAN_EXPERT, :EXP_TILES], plan[PLAN_VALID, :EXP_TILES], plan[PLAN_NACT, :1],
                      plan[PLAN_NEXT, :EXP_TILES], posw, h1, mod,
                      w_expert_gate[0], w_expert_up[0], w_expert_down[0])
    out = _combine_call(posw, h1, y, cw, mod, row(ln2_g[0]), row(ln2_b[0]))
    return out[None]
```
